```python
import jax, jax.numpy as jnp
from jax import lax
import numpy as np

D_MODEL = 1024
BATCH = 8
SEQ = 4096
DEPTH = 2

N_META = 16
RET_HEADS = 4
RET_HEAD_DIM = 128
RET_WIDTH = RET_HEADS * RET_HEAD_DIM
POOL_WINDOWS = (2, 4, 8, 16)
POOL_GROUPS = len(POOL_WINDOWS)
POOL_GROUP_DIM = 128
POOL_WIDTH = POOL_GROUPS * POOL_GROUP_DIM
CHUNK = 128
D_FF = 2816
ROPE_BASE = 10000.0
EPS = 1e-6
N_IN = 4 * RET_WIDTH + POOL_WIDTH + 2 * D_MODEL

kernel_name = 'hybrid_retention_pool_macaron'


def _rmsnorm(x, g):
    xf = x.astype(jnp.float32)
    y = xf * lax.rsqrt(jnp.mean(xf * xf, axis=-1, keepdims=True) + EPS)
    return (y * g.astype(jnp.float32)).astype(x.dtype)


def _swiglu(h, w_gate, w_up, w_down):
    return (jax.nn.silu(h @ w_gate) * (h @ w_up)) @ w_down


def _rotary(t, pos):
    half = t.shape[-1] // 2
    inv_freq = ROPE_BASE ** (-jnp.arange(half, dtype=jnp.float32) / half)
    ang = pos[:, None] * inv_freq[None, :]
    cos = jnp.cos(ang).astype(t.dtype)
    sin = jnp.sin(ang).astype(t.dtype)
    t1, t2 = t[..., :half], t[..., half:]
    return jnp.concatenate([t1 * cos - t2 * sin, t1 * sin + t2 * cos], axis=-1)


def _retention(q, k, v):
    b, L, _ = q.shape
    pad = (-L) % CHUNK
    Lp = L + pad
    n = Lp // CHUNK

    def heads(t):
        t = t.reshape(b, L, RET_HEADS, RET_HEAD_DIM).transpose(0, 2, 1, 3)
        return jnp.pad(t, ((0, 0), (0, 0), (pad, 0), (0, 0)))

    q, k, v = heads(q), heads(k), heads(v)
    pos = jnp.arange(Lp, dtype=jnp.float32) - pad
    q = _rotary(q, pos) * (RET_HEAD_DIM ** -0.5)
    k = _rotary(k, pos)
    shp = (b, RET_HEADS, n, CHUNK, RET_HEAD_DIM)
    q, k, v = q.reshape(shp), k.reshape(shp), v.reshape(shp)

    dt = q.dtype
    log_gamma = jnp.log1p(-(2.0 ** (-5.0 - jnp.arange(RET_HEADS, dtype=jnp.float32))))
    idx = jnp.arange(CHUNK, dtype=jnp.float32)
    diff = idx[:, None] - idx[None, :]
    intra = jnp.where(diff[None] >= 0, jnp.exp(diff[None] * log_gamma[:, None, None]), 0.0)
    k_decay = jnp.exp((CHUNK - 1.0 - idx)[None, :] * log_gamma[:, None])
    q_decay = jnp.exp((idx + 1.0)[None, :] * log_gamma[:, None])
    chunk_decay = jnp.exp(CHUNK * log_gamma)

    scores = jnp.einsum('bhncd,bhnmd->bhncm', q, k) * intra[None, :, None].astype(dt)
    inner = jnp.einsum('bhncm,bhnmd->bhncd', scores, v)
    kv = jnp.einsum('bhncd,bhnce->nbhde', k * k_decay[None, :, None, :, None].astype(dt), v)
    cd = chunk_decay[None, :, None, None].astype(dt)

    def step(state, kv_n):
        return state * cd + kv_n, state

    s0 = jnp.zeros((b, RET_HEADS, RET_HEAD_DIM, RET_HEAD_DIM), dt)
    _, s_prev = lax.scan(step, s0, kv)
    cross = jnp.einsum('bhncd,nbhde->bhnce', q * q_decay[None, :, None, :, None].astype(dt), s_prev)
    out = (inner + cross).reshape(b, RET_HEADS, Lp, RET_HEAD_DIM)[:, :, pad:]

    of = out.astype(jnp.float32)
    mu = jnp.mean(of, axis=-1, keepdims=True)
    var = jnp.mean(jnp.square(of - mu), axis=-1, keepdims=True)
    of = (of - mu) * lax.rsqrt(var + EPS)
    return of.astype(dt).transpose(0, 2, 1, 3).reshape(b, L, RET_WIDTH)


def _pool_mixer(u, maps, scale):
    b, L, _ = u.shape
    uf = u.astype(jnp.float32)
    t = jnp.arange(L, dtype=jnp.float32)[None, :, None]
    outs = []
    for g, w in enumerate(POOL_WINDOWS):
        ug = uf[..., g * POOL_GROUP_DIM:(g + 1) * POOL_GROUP_DIM]
        cs = jnp.cumsum(ug, axis=1)
        lag = jnp.pad(cs, ((0, 0), (w, 0), (0, 0)))[:, :L]
        mean = (cs - lag) / jnp.minimum(t + 1.0, float(w))
        pooled = (mean - ug).astype(u.dtype)
        outs.append(pooled @ maps[g])
    return jnp.concatenate(outs, axis=-1) * scale


def _fwd_setup_inputs(seed: int = 0) -> dict:
    key = jax.random.key(seed)
    ks = jax.random.split(key, 20)
    f32 = jnp.float32

    def w(k, shape, fan_in):
        return jax.random.normal(k, shape, f32) * (fan_in ** -0.5)

    def gain(k, shape):
        return 1.0 + 0.02 * jax.random.normal(k, shape, f32)

    return {
        'x': jax.random.normal(ks[0], (BATCH, SEQ, D_MODEL), f32),
        'meta': jax.random.normal(ks[1], (N_META, D_MODEL), f32),
        'ffn1_norm': gain(ks[2], (DEPTH, D_MODEL)),
        'ffn1_gate': w(ks[3], (DEPTH, D_MODEL, D_FF), D_MODEL),
        'ffn1_up': w(ks[4], (DEPTH, D_MODEL, D_FF), D_MODEL),
        'ffn1_down': w(ks[5], (DEPTH, D_FF, D_MODEL), D_FF),
        'mix_norm': gain(ks[6], (DEPTH, D_MODEL)),
        'w_in': w(ks[7], (DEPTH, D_MODEL, N_IN), D_MODEL),
        'pool_maps': w(ks[8], (DEPTH, POOL_GROUPS, POOL_GROUP_DIM, POOL_GROUP_DIM), POOL_GROUP_DIM),
        'pool_scale': gain(ks[9], (DEPTH, POOL_WIDTH)),
        'w_ret_up': w(ks[10], (DEPTH, RET_WIDTH, D_MODEL), RET_WIDTH),
        'w_pool_up': w(ks[11], (DEPTH, POOL_WIDTH, D_MODEL), POOL_WIDTH),
        'w_out': w(ks[12], (DEPTH, D_MODEL, D_MODEL), D_MODEL),
        'ffn2_norm': gain(ks[13], (DEPTH, D_MODEL)),
        'ffn2_gate': w(ks[14], (DEPTH, D_MODEL, D_FF), D_MODEL),
        'ffn2_up': w(ks[15], (DEPTH, D_MODEL, D_FF), D_MODEL),
        'ffn2_down': w(ks[16], (DEPTH, D_FF, D_MODEL), D_FF),
        'final_norm': gain(ks[17], (D_MODEL,)),
    }


def _fwd_reference(x, meta, ffn1_norm, ffn1_gate, ffn1_up, ffn1_down, mix_norm, w_in,
              pool_maps, pool_scale, w_ret_up, w_pool_up, w_out,
              ffn2_norm, ffn2_gate, ffn2_up, ffn2_down, final_norm):
    b = x.shape[0]
    meta_b = jnp.broadcast_to(meta[None].astype(x.dtype), (b, N_META, D_MODEL))
    h = jnp.concatenate([meta_b, x], axis=1)
    splits = np.cumsum([RET_WIDTH, RET_WIDTH, RET_WIDTH, RET_WIDTH, POOL_WIDTH, D_MODEL])
    for i in range(DEPTH):
        h = h + 0.5 * _swiglu(_rmsnorm(h, ffn1_norm[i]), ffn1_gate[i], ffn1_up[i], ffn1_down[i])
        z = _rmsnorm(h, mix_norm[i]) @ w_in[i]
        q, k, v, g_ret, u_pool, gate_a, gate_b = jnp.split(z, splits, axis=-1)
        ret = (_retention(q, k, v) * jax.nn.silu(g_ret)) @ w_ret_up[i]
        pool = _pool_mixer(u_pool, pool_maps[i], pool_scale[i]) @ w_pool_up[i]
        mixed = jax.nn.sigmoid(gate_a) * ret + jax.nn.sigmoid(gate_b) * pool
        h = h + mixed @ w_out[i]
        h = h + 0.5 * _swiglu(_rmsnorm(h, ffn2_norm[i]), ffn2_gate[i], ffn2_up[i], ffn2_down[i])
    h = _rmsnorm(h, final_norm)
    return h[:, N_META:]


import jax as _jax
import jax.numpy as _jnp

TWIN_FORMAT = 'train_step'
FWD_PARAMS = ['x', 'meta', 'ffn1_norm', 'ffn1_gate', 'ffn1_up', 'ffn1_down', 'mix_norm', 'w_in', 'pool_maps', 'pool_scale', 'w_ret_up', 'w_pool_up', 'w_out', 'ffn2_norm', 'ffn2_gate', 'ffn2_up', 'ffn2_down', 'final_norm']
TWIN_WEIGHTS = ['meta', 'ffn1_norm', 'ffn1_gate', 'ffn1_up', 'ffn1_down', 'mix_norm', 'w_in', 'pool_maps', 'pool_scale', 'w_ret_up', 'w_pool_up', 'w_out', 'ffn2_norm', 'ffn2_gate', 'ffn2_up', 'ffn2_down', 'final_norm']
TWIN_DIFF_INPUT = 'x'
TWIN_INPUTS = ['x', 'meta', 'ffn1_norm', 'ffn1_gate', 'ffn1_up', 'ffn1_down', 'mix_norm', 'w_in', 'pool_maps', 'pool_scale', 'w_ret_up', 'w_pool_up', 'w_out', 'ffn2_norm', 'ffn2_gate', 'ffn2_up', 'ffn2_down', 'final_norm', 'loss_target', 'm_meta', 'm_ffn1_norm', 'm_ffn1_gate', 'm_ffn1_up', 'm_ffn1_down', 'm_mix_norm', 'm_w_in', 'm_pool_maps', 'm_pool_scale', 'm_w_ret_up', 'm_w_pool_up', 'm_w_out', 'm_ffn2_norm', 'm_ffn2_gate', 'm_ffn2_up', 'm_ffn2_down', 'm_final_norm', 'v_meta', 'v_ffn1_norm', 'v_ffn1_gate', 'v_ffn1_up', 'v_ffn1_down', 'v_mix_norm', 'v_w_in', 'v_pool_maps', 'v_pool_scale', 'v_w_ret_up', 'v_w_pool_up', 'v_w_out', 'v_ffn2_norm', 'v_ffn2_gate', 'v_ffn2_up', 'v_ffn2_down', 'v_final_norm']
TWIN_OUTPUTS = ['loss', 'grad_x', 'grad_meta', 'grad_ffn1_norm', 'grad_ffn1_gate', 'grad_ffn1_up', 'grad_ffn1_down', 'grad_mix_norm', 'grad_w_in', 'grad_pool_maps', 'grad_pool_scale', 'grad_w_ret_up', 'grad_w_pool_up', 'grad_w_out', 'grad_ffn2_norm', 'grad_ffn2_gate', 'grad_ffn2_up', 'grad_ffn2_down', 'grad_final_norm', 'delta_meta', 'delta_ffn1_norm', 'delta_ffn1_gate', 'delta_ffn1_up', 'delta_ffn1_down', 'delta_mix_norm', 'delta_w_in', 'delta_pool_maps', 'delta_pool_scale', 'delta_w_ret_up', 'delta_w_pool_up', 'delta_w_out', 'delta_ffn2_norm', 'delta_ffn2_gate', 'delta_ffn2_up', 'delta_ffn2_down', 'delta_final_norm', 'new_m_meta', 'new_m_ffn1_norm', 'new_m_ffn1_gate', 'new_m_ffn1_up', 'new_m_ffn1_down', 'new_m_mix_norm', 'new_m_w_in', 'new_m_pool_maps', 'new_m_pool_scale', 'new_m_w_ret_up', 'new_m_w_pool_up', 'new_m_w_out', 'new_m_ffn2_norm', 'new_m_ffn2_gate', 'new_m_ffn2_up', 'new_m_ffn2_down', 'new_m_final_norm', 'new_v_meta', 'new_v_ffn1_norm', 'new_v_ffn1_gate', 'new_v_ffn1_up', 'new_v_ffn1_down', 'new_v_mix_norm', 'new_v_w_in', 'new_v_pool_maps', 'new_v_pool_scale', 'new_v_w_ret_up', 'new_v_w_pool_up', 'new_v_w_out', 'new_v_ffn2_norm', 'new_v_ffn2_gate', 'new_v_ffn2_up', 'new_v_ffn2_down', 'new_v_final_norm']
TWIN_LEAF_KINDS = {'loss': 'loss', 'grad_x': 'grad_x', 'grad_meta': 'grad_w', 'grad_ffn1_norm': 'grad_w', 'grad_ffn1_gate': 'grad_w', 'grad_ffn1_up': 'grad_w', 'grad_ffn1_down': 'grad_w', 'grad_mix_norm': 'grad_w', 'grad_w_in': 'grad_w', 'grad_pool_maps': 'grad_w', 'grad_pool_scale': 'grad_w', 'grad_w_ret_up': 'grad_w', 'grad_w_pool_up': 'grad_w', 'grad_w_out': 'grad_w', 'grad_ffn2_norm': 'grad_w', 'grad_ffn2_gate': 'grad_w', 'grad_ffn2_up': 'grad_w', 'grad_ffn2_down': 'grad_w', 'grad_final_norm': 'grad_w', 'delta_meta': 'delta_w', 'delta_ffn1_norm': 'delta_w', 'delta_ffn1_gate': 'delta_w', 'delta_ffn1_up': 'delta_w', 'delta_ffn1_down': 'delta_w', 'delta_mix_norm': 'delta_w', 'delta_w_in': 'delta_w', 'delta_pool_maps': 'delta_w', 'delta_pool_scale': 'delta_w', 'delta_w_ret_up': 'delta_w', 'delta_w_pool_up': 'delta_w', 'delta_w_out': 'delta_w', 'delta_ffn2_norm': 'delta_w', 'delta_ffn2_gate': 'delta_w', 'delta_ffn2_up': 'delta_w', 'delta_ffn2_down': 'delta_w', 'delta_final_norm': 'delta_w', 'new_m_meta': 'new_m', 'new_m_ffn1_norm': 'new_m', 'new_m_ffn1_gate': 'new_m', 'new_m_ffn1_up': 'new_m', 'new_m_ffn1_down': 'new_m', 'new_m_mix_norm': 'new_m', 'new_m_w_in': 'new_m', 'new_m_pool_maps': 'new_m', 'new_m_pool_scale': 'new_m', 'new_m_w_ret_up': 'new_m', 'new_m_w_pool_up': 'new_m', 'new_m_w_out': 'new_m', 'new_m_ffn2_norm': 'new_m', 'new_m_ffn2_gate': 'new_m', 'new_m_ffn2_up': 'new_m', 'new_m_ffn2_down': 'new_m', 'new_m_final_norm': 'new_m', 'new_v_meta': 'new_v', 'new_v_ffn1_norm': 'new_v', 'new_v_ffn1_gate': 'new_v', 'new_v_ffn1_up': 'new_v', 'new_v_ffn1_down': 'new_v', 'new_v_mix_norm': 'new_v', 'new_v_w_in': 'new_v', 'new_v_pool_maps': 'new_v', 'new_v_pool_scale': 'new_v', 'new_v_w_ret_up': 'new_v', 'new_v_w_pool_up': 'new_v', 'new_v_w_out': 'new_v', 'new_v_ffn2_norm': 'new_v', 'new_v_ffn2_gate': 'new_v', 'new_v_ffn2_up': 'new_v', 'new_v_ffn2_down': 'new_v', 'new_v_final_norm': 'new_v'}


def _forward(args):
    return _fwd_reference(*[args[k] for k in FWD_PARAMS])


def _output_shape():
    out = _jax.eval_shape(lambda: _forward(_fwd_setup_inputs(0)))
    return out.shape, out.dtype

N_MICROBATCH = 1
ADAM_LR = 0.001
ADAM_B1 = 0.9
ADAM_B2 = 0.999
ADAM_EPS = 1e-08
ADAM_WD = 0.01
ADAM_STEP = 10
PER_EXAMPLE_BATCH_AXIS = {'x': 0, 'loss_target': 0}
SHARED_INPUTS = []
_WEIGHT_DTYPES = {'meta': _jnp.float32, 'ffn1_norm': _jnp.float32, 'ffn1_gate': _jnp.float32, 'ffn1_up': _jnp.float32, 'ffn1_down': _jnp.float32, 'mix_norm': _jnp.float32, 'w_in': _jnp.float32, 'pool_maps': _jnp.float32, 'pool_scale': _jnp.float32, 'w_ret_up': _jnp.float32, 'w_pool_up': _jnp.float32, 'w_out': _jnp.float32, 'ffn2_norm': _jnp.float32, 'ffn2_gate': _jnp.float32, 'ffn2_up': _jnp.float32, 'ffn2_down': _jnp.float32, 'final_norm': _jnp.float32}
MOMENT_SCALE = {'meta': 8.490036e-03, 'ffn1_norm': 8.487066e-02, 'ffn1_gate': 3.606626e-02, 'ffn1_up': 3.497880e-02, 'ffn1_down': 5.785205e-02, 'mix_norm': 1.337781e-01, 'w_in': 6.157328e-02, 'pool_maps': 1.052655e-01, 'pool_scale': 1.072240e-01, 'w_ret_up': 5.014172e-02, 'w_pool_up': 7.395301e-02, 'w_out': 8.941294e-02, 'ffn2_norm': 6.393319e-02, 'ffn2_gate': 2.734570e-02, 'ffn2_up': 2.645241e-02, 'ffn2_down': 4.396475e-02, 'final_norm': 3.203034e+01}


def _to_microbatches(a, axis):
    t = _jnp.moveaxis(a, axis, 0)
    t = t.reshape((N_MICROBATCH, t.shape[0] // N_MICROBATCH) + t.shape[1:])
    return _jnp.moveaxis(t, 1, axis + 1)


def setup_inputs(seed: int = 0) -> dict:
    inp = _fwd_setup_inputs(seed)
    key = _jax.random.fold_in(_jax.random.key(seed), 7919)
    shape, _ = _output_shape()
    out = dict(inp)
    out["loss_target"] = _jax.random.normal(_jax.random.fold_in(key, 0), shape, _jnp.float32)
    for i, name in enumerate(TWIN_WEIGHTS):
        w = inp[name].astype(_jnp.float32)
        if MOMENT_SCALE is None:
            s = _jnp.sqrt(_jnp.mean(_jnp.square(w)) + 1e-30)
        else:
            s = MOMENT_SCALE[name]
        km, kv = _jax.random.split(_jax.random.fold_in(key, i + 1))
        out[name] = w
        out["m_" + name] = s * _jax.random.normal(km, w.shape, _jnp.float32)
        out["v_" + name] = (s * s) * _jax.random.uniform(kv, w.shape, _jnp.float32, 0.5, 1.5)
    if N_MICROBATCH > 1:
        for name, axis in PER_EXAMPLE_BATCH_AXIS.items():
            out[name] = _to_microbatches(out[name], axis)
    return {'x': out['x'], 'meta': out['meta'], 'ffn1_norm': out['ffn1_norm'], 'ffn1_gate': out['ffn1_gate'], 'ffn1_up': out['ffn1_up'], 'ffn1_down': out['ffn1_down'], 'mix_norm': out['mix_norm'], 'w_in': out['w_in'], 'pool_maps': out['pool_maps'], 'pool_scale': out['pool_scale'], 'w_ret_up': out['w_ret_up'], 'w_pool_up': out['w_pool_up'], 'w_out': out['w_out'], 'ffn2_norm': out['ffn2_norm'], 'ffn2_gate': out['ffn2_gate'], 'ffn2_up': out['ffn2_up'], 'ffn2_down': out['ffn2_down'], 'final_norm': out['final_norm'], 'loss_target': out['loss_target'], 'm_meta': out['m_meta'], 'm_ffn1_norm': out['m_ffn1_norm'], 'm_ffn1_gate': out['m_ffn1_gate'], 'm_ffn1_up': out['m_ffn1_up'], 'm_ffn1_down': out['m_ffn1_down'], 'm_mix_norm': out['m_mix_norm'], 'm_w_in': out['m_w_in'], 'm_pool_maps': out['m_pool_maps'], 'm_pool_scale': out['m_pool_scale'], 'm_w_ret_up': out['m_w_ret_up'], 'm_w_pool_up': out['m_w_pool_up'], 'm_w_out': out['m_w_out'], 'm_ffn2_norm': out['m_ffn2_norm'], 'm_ffn2_gate': out['m_ffn2_gate'], 'm_ffn2_up': out['m_ffn2_up'], 'm_ffn2_down': out['m_ffn2_down'], 'm_final_norm': out['m_final_norm'], 'v_meta': out['v_meta'], 'v_ffn1_norm': out['v_ffn1_norm'], 'v_ffn1_gate': out['v_ffn1_gate'], 'v_ffn1_up': out['v_ffn1_up'], 'v_ffn1_down': out['v_ffn1_down'], 'v_mix_norm': out['v_mix_norm'], 'v_w_in': out['v_w_in'], 'v_pool_maps': out['v_pool_maps'], 'v_pool_scale': out['v_pool_scale'], 'v_w_ret_up': out['v_w_ret_up'], 'v_w_pool_up': out['v_w_pool_up'], 'v_w_out': out['v_w_out'], 'v_ffn2_norm': out['v_ffn2_norm'], 'v_ffn2_gate': out['v_ffn2_gate'], 'v_ffn2_up': out['v_ffn2_up'], 'v_ffn2_down': out['v_ffn2_down'], 'v_final_norm': out['v_final_norm']}


def _loss(weights, diff, rest, loss_target):
    with _jax.named_scope("forward"):
        args = {**rest, TWIN_DIFF_INPUT: diff, **{k: w.astype(_WEIGHT_DTYPES[k]) for k, w in weights.items()}}
        y = _forward(args)
    with _jax.named_scope("loss_head"):
        err = _jnp.square(y.astype(_jnp.float32) - loss_target)
        return 0.5 * _jnp.sum(_jnp.mean(err, axis=-1)) if err.ndim else 0.5 * err


def _adamw(w, g, m, v):
    m = ADAM_B1 * m + (1.0 - ADAM_B1) * g
    v = ADAM_B2 * v + (1.0 - ADAM_B2) * _jnp.square(g)
    m_hat = m / (1.0 - ADAM_B1 ** ADAM_STEP)
    v_hat = v / (1.0 - ADAM_B2 ** ADAM_STEP)
    delta = -ADAM_LR * (m_hat / (_jnp.sqrt(v_hat) + ADAM_EPS) + ADAM_WD * w)
    return delta, m, v


def reference(x, meta, ffn1_norm, ffn1_gate, ffn1_up, ffn1_down, mix_norm, w_in, pool_maps, pool_scale, w_ret_up, w_pool_up, w_out, ffn2_norm, ffn2_gate, ffn2_up, ffn2_down, final_norm, loss_target, m_meta, m_ffn1_norm, m_ffn1_gate, m_ffn1_up, m_ffn1_down, m_mix_norm, m_w_in, m_pool_maps, m_pool_scale, m_w_ret_up, m_w_pool_up, m_w_out, m_ffn2_norm, m_ffn2_gate, m_ffn2_up, m_ffn2_down, m_final_norm, v_meta, v_ffn1_norm, v_ffn1_gate, v_ffn1_up, v_ffn1_down, v_mix_norm, v_w_in, v_pool_maps, v_pool_scale, v_w_ret_up, v_w_pool_up, v_w_out, v_ffn2_norm, v_ffn2_gate, v_ffn2_up, v_ffn2_down, v_final_norm):
    given = dict(x=x, meta=meta, ffn1_norm=ffn1_norm, ffn1_gate=ffn1_gate, ffn1_up=ffn1_up, ffn1_down=ffn1_down, mix_norm=mix_norm, w_in=w_in, pool_maps=pool_maps, pool_scale=pool_scale, w_ret_up=w_ret_up, w_pool_up=w_pool_up, w_out=w_out, ffn2_norm=ffn2_norm, ffn2_gate=ffn2_gate, ffn2_up=ffn2_up, ffn2_down=ffn2_down, final_norm=final_norm, loss_target=loss_target, m_meta=m_meta, m_ffn1_norm=m_ffn1_norm, m_ffn1_gate=m_ffn1_gate, m_ffn1_up=m_ffn1_up, m_ffn1_down=m_ffn1_down, m_mix_norm=m_mix_norm, m_w_in=m_w_in, m_pool_maps=m_pool_maps, m_pool_scale=m_pool_scale, m_w_ret_up=m_w_ret_up, m_w_pool_up=m_w_pool_up, m_w_out=m_w_out, m_ffn2_norm=m_ffn2_norm, m_ffn2_gate=m_ffn2_gate, m_ffn2_up=m_ffn2_up, m_ffn2_down=m_ffn2_down, m_final_norm=m_final_norm, v_meta=v_meta, v_ffn1_norm=v_ffn1_norm, v_ffn1_gate=v_ffn1_gate, v_ffn1_up=v_ffn1_up, v_ffn1_down=v_ffn1_down, v_mix_norm=v_mix_norm, v_w_in=v_w_in, v_pool_maps=v_pool_maps, v_pool_scale=v_pool_scale, v_w_ret_up=v_w_ret_up, v_w_pool_up=v_w_pool_up, v_w_out=v_w_out, v_ffn2_norm=v_ffn2_norm, v_ffn2_gate=v_ffn2_gate, v_ffn2_up=v_ffn2_up, v_ffn2_down=v_ffn2_down, v_final_norm=v_final_norm)
    weights = {n: given[n] for n in TWIN_WEIGHTS}
    shared = {n: given[n] for n in SHARED_INPUTS}
    per_example = {n: given[n] for n in ['x']}
    grad_fn = _jax.value_and_grad(_loss, argnums=(0, 1))

    def one_microbatch(ex, loss_target):
        ex = dict(ex)
        diff = ex.pop(TWIN_DIFF_INPUT)
        return grad_fn(weights, diff, {**shared, **ex}, loss_target)

    if N_MICROBATCH == 1:
        loss, (grad_w, grad_x) = one_microbatch(per_example, given["loss_target"])
    else:
        def body(carry, xs):
            loss_sum, grad_sum = carry
            l_k, (gw_k, gx_k) = one_microbatch(xs[0], xs[1])
            with _jax.named_scope("update"):
                return (loss_sum + l_k, _jax.tree.map(_jnp.add, grad_sum, gw_k)), gx_k

        init = (_jnp.zeros((), _jnp.float32), _jax.tree.map(_jnp.zeros_like, weights))
        (loss, grad_w), grad_x = _jax.lax.scan(body, init, (per_example, given["loss_target"]))
    with _jax.named_scope("update"):
        delta_w, new_m, new_v = {}, {}, {}
        for n in TWIN_WEIGHTS:
            delta_w[n], new_m[n], new_v[n] = _adamw(weights[n], grad_w[n], given["m_" + n], given["v_" + n])
    return (loss, grad_x, *[grad_w[n] for n in TWIN_WEIGHTS], *[delta_w[n] for n in TWIN_WEIGHTS],
            *[new_m[n] for n in TWIN_WEIGHTS], *[new_v[n] for n in TWIN_WEIGHTS])
```

```python
import functools

import jax
import jax.numpy as jnp
from jax import lax
from jax.experimental import pallas as pl
from jax.experimental.pallas import tpu as pltpu

f32 = jnp.float32
bf16 = jnp.bfloat16
MESH = pl.DeviceIdType.MESH
NDEV = 8
N_META = 16
HEADS = 4
HD = 128
CHUNK = 128
RW = HEADS * HD
POOL_WINDOWS = (2, 4, 8, 16)
ROPE_BASE = 10000.0
EPS = 1e-6
ADAM_LR = 0.001
ADAM_B1 = 0.9
ADAM_B2 = 0.999
ADAM_EPS = 1e-08
ADAM_WD = 0.01
ADAM_STEP = 10
VMEM_CAP_MB = 60


def _cp(vmem_mb, sem=None):
    return pltpu.CompilerParams(vmem_limit_bytes=min(vmem_mb, VMEM_CAP_MB) * 2**20, dimension_semantics=sem)


def _row_tile(tp, want=384):
    return want if tp % want == 0 else 128


def _resident(shape):
    nd = len(shape)
    return pl.BlockSpec(shape, lambda *_: (0,) * nd, pipeline_mode=pl.Buffered(1))


def _dot_nn(a, b):
    return lax.dot_general(a, b, (((1,), (0,)), ((), ())), preferred_element_type=f32)


def _dot_nt(a, b):
    return lax.dot_general(a, b, (((1,), (1,)), ((), ())), preferred_element_type=f32)


def _dot_tn(a, b):
    return lax.dot_general(a, b, (((0,), (0,)), ((), ())), preferred_element_type=f32)


def _rms(h):
    rs = lax.rsqrt(jnp.mean(h * h, axis=-1, keepdims=True) + EPS)
    return h * rs, rs


def _rms_bwd(dn, g, hhat, rs):
    dhh = dn * g
    return rs * (dhh - hhat * jnp.mean(dhh * hhat, axis=-1, keepdims=True))


def _sigmoid(x):
    return jax.nn.sigmoid(x)


def _me():
    return lax.axis_index("x"), lax.axis_index("y"), lax.axis_index("c")


def _peer(idx):
    return (idx // 4, (idx // 2) % 2, idx % 2)


def all_gather(name, arrays):
    n = len(arrays)

    def body(*refs):
        ins, outs = refs[:n], refs[n:2 * n]
        send_sems, recv_sems, local_sems = refs[2 * n:]
        x, y, c = _me()
        me = 4 * x + 2 * y + c
        locals_ = []
        for k in range(n):
            cp = pltpu.make_async_copy(ins[k], outs[k].at[me], local_sems.at[k])
            cp.start()
            locals_.append(cp)
        for d in range(1, NDEV):
            for k in range(n):
                pltpu.make_async_remote_copy(
                    src_ref=ins[k], dst_ref=outs[k].at[me], send_sem=send_sems.at[k], recv_sem=recv_sems.at[k],
                    device_id=_peer((me + d) % NDEV), device_id_type=MESH).start()
        for k in range(n):
            seven = outs[k].at[pl.ds(0, NDEV - 1)]
            w = pltpu.make_async_remote_copy(src_ref=seven, dst_ref=seven, send_sem=send_sems.at[k],
                                             recv_sem=recv_sems.at[k], device_id=(x, y, c), device_id_type=MESH)
            w.wait_send()
            w.wait_recv()
            locals_[k].wait()

    anyspec = pl.BlockSpec(memory_space=pl.ANY)
    return pl.pallas_call(
        body, name=name,
        out_shape=[jax.ShapeDtypeStruct((NDEV,) + a.shape, a.dtype) for a in arrays],
        in_specs=[anyspec] * n, out_specs=[anyspec] * n,
        scratch_shapes=[pltpu.SemaphoreType.DMA((n,)), pltpu.SemaphoreType.DMA((n,)), pltpu.SemaphoreType.DMA((n,))],
    )(*arrays)


def shard_exchange(name, grads):
    n = len(grads)

    def body(*refs):
        ins, outs = refs[:n], refs[n:2 * n]
        send_sems, recv_sems, local_sems = refs[2 * n:]
        x, y, c = _me()
        me = 4 * x + 2 * y + c
        locals_ = []
        for k in range(n):
            cp = pltpu.make_async_copy(ins[k].at[:, me], outs[k].at[me], local_sems.at[k])
            cp.start()
            locals_.append(cp)
        for d in range(1, NDEV):
            p = (me + d) % NDEV
            for k in range(n):
                pltpu.make_async_remote_copy(
                    src_ref=ins[k].at[:, p], dst_ref=outs[k].at[me], send_sem=send_sems.at[k], recv_sem=recv_sems.at[k],
                    device_id=_peer(p), device_id_type=MESH).start()
        for k in range(n):
            seven = outs[k].at[pl.ds(0, NDEV - 1)]
            w = pltpu.make_async_remote_copy(src_ref=seven, dst_ref=seven, send_sem=send_sems.at[k],
                                             recv_sem=recv_sems.at[k], device_id=(x, y, c), device_id_type=MESH)
            w.wait_send()
            w.wait_recv()
            locals_[k].wait()

    anyspec = pl.BlockSpec(memory_space=pl.ANY)
    return pl.pallas_call(
        body, name=name,
        out_shape=[jax.ShapeDtypeStruct((NDEV, g.shape[0]) + g.shape[2:], g.dtype) for g in grads],
        in_specs=[anyspec] * n, out_specs=[anyspec] * n,
        scratch_shapes=[pltpu.SemaphoreType.DMA((n,)), pltpu.SemaphoreType.DMA((n,)), pltpu.SemaphoreType.DMA((n,))],
    )(*grads)


def prep_layer(layer, col_sharded, row_sharded):
    nc, nr = len(col_sharded), len(row_sharded)

    def body(*refs):
        ins, outs = refs[:nc + nr], refs[nc + nr:]
        for k in range(nc):
            outs[k][...] = ins[k][...].T.astype(bf16)
        for k in range(nc, nc + nr):
            outs[k][...] = ins[k][...].astype(bf16)

    arrs = list(col_sharded) + list(row_sharded)
    in_specs = [pl.BlockSpec((None,) + a.shape[1:], lambda i: (layer, 0, 0)) for a in arrs]
    out_shapes = [jax.ShapeDtypeStruct((a.shape[2], a.shape[1]), bf16) for a in col_sharded]
    out_shapes += [jax.ShapeDtypeStruct(a.shape[1:], bf16) for a in row_sharded]
    out_specs = [pl.BlockSpec(s.shape, lambda i: (0, 0)) for s in out_shapes]
    return pl.pallas_call(body, name=f"prep_layer{layer}", grid=(1,), in_specs=in_specs, out_specs=out_specs,
                          out_shape=out_shapes, compiler_params=_cp(48))(*arrs)


def ffn_fwd(name, h, g, wgT, wuT, wd):
    tp, d = h.shape
    ff = wgT.shape[0]
    tm = _row_tile(tp)

    def body(h_ref, g_ref, wg_ref, wu_ref, wd_ref, ho_ref, a_ref, b_ref):
        hh = h_ref[...]
        hhat, _ = _rms(hh)
        n = (hhat * g_ref[...]).astype(bf16)
        a = _dot_nt(n, wg_ref[...])
        b = _dot_nt(n, wu_ref[...])
        s = (a * _sigmoid(a)) * b
        ho_ref[...] = hh + 0.5 * _dot_nn(s.astype(bf16), wd_ref[...])
        a_ref[...] = a.astype(bf16)
        b_ref[...] = b.astype(bf16)

    row = lambda w: pl.BlockSpec((tm, w), lambda i: (i, 0))
    return pl.pallas_call(
        body, name=name, grid=(tp // tm,),
        in_specs=[row(d), _resident((1, d)), _resident((ff, d)), _resident((ff, d)), _resident((ff, d))],
        out_specs=[row(d), row(ff), row(ff)],
        out_shape=[jax.ShapeDtypeStruct((tp, d), f32), jax.ShapeDtypeStruct((tp, ff), bf16),
                   jax.ShapeDtypeStruct((tp, ff), bf16)],
        compiler_params=_cp(56, ("arbitrary",)))(h, g, wgT, wuT, wd)


def ffn_bwd(name, dy, h, g, a, b, wgT, wuT, wd):
    tp, d = h.shape
    ff = wgT.shape[0]
    tm = _row_tile(tp, 192)

    def body(dy_ref, h_ref, g_ref, a_ref, b_ref, wg_ref, wu_ref, wd_ref, dh_ref, lhs_ref, rhs_ref, dg_ref):
        dyv = dy_ref[...]
        hhat, rs = _rms(h_ref[...])
        gv = g_ref[...]
        n = hhat * gv
        dyh = (0.5 * dyv).astype(bf16)
        ds = _dot_nt(dyh, wd_ref[...])
        av = a_ref[...].astype(f32)
        bv = b_ref[...].astype(f32)
        sg = _sigmoid(av)
        sa = av * sg
        da = (ds * bv * (sg * (1.0 + av * (1.0 - sg)))).astype(bf16)
        db = (ds * sa).astype(bf16)
        dn = _dot_nn(da, wg_ref[...]) + _dot_nn(db, wu_ref[...])
        dh_ref[...] = dyv + _rms_bwd(dn, gv, hhat, rs)

        @pl.when(pl.program_id(0) == 0)
        def _():
            dg_ref[...] = jnp.zeros_like(dg_ref)

        dg_ref[0:1, :] += jnp.sum(dn * hhat, axis=0, keepdims=True)
        lhs_ref[0] = da
        lhs_ref[1] = db
        lhs_ref[2] = (sa * bv).astype(bf16)
        rhs_ref[0] = n.astype(bf16)
        rhs_ref[1] = dyh

    row = lambda w: pl.BlockSpec((tm, w), lambda i: (i, 0))
    return pl.pallas_call(
        body, name=name, grid=(tp // tm,),
        in_specs=[row(d), row(d), _resident((1, d)), row(ff), row(ff),
                  _resident((ff, d)), _resident((ff, d)), _resident((ff, d))],
        out_specs=[row(d), pl.BlockSpec((3, tm, ff), lambda i: (0, i, 0)), pl.BlockSpec((2, tm, d), lambda i: (0, i, 0)),
                   pl.BlockSpec((8, d), lambda i: (0, 0))],
        out_shape=[jax.ShapeDtypeStruct((tp, d), f32), jax.ShapeDtypeStruct((3, tp, ff), bf16),
                   jax.ShapeDtypeStruct((2, tp, d), bf16), jax.ShapeDtypeStruct((8, d), f32)],
        compiler_params=_cp(58, ("arbitrary",)))(dy, h, g, a, b, wgT, wuT, wd)


def mm_tn(name, lhs, rhs, rhs_of):
    nb, tp, m = lhs.shape
    n = rhs.shape[2]
    tk = _row_tile(tp)
    nk = tp // tk

    def body(l_ref, r_ref, o_ref, acc_ref):
        k = pl.program_id(1)

        @pl.when(k == 0)
        def _():
            acc_ref[...] = jnp.zeros_like(acc_ref)

        acc_ref[...] += _dot_tn(l_ref[...], r_ref[...])

        @pl.when(k == nk - 1)
        def _():
            o_ref[...] = acc_ref[...].astype(o_ref.dtype)

    return pl.pallas_call(
        body, name=name, grid=(nb, nk),
        in_specs=[pl.BlockSpec((None, tk, m), lambda b, k: (b, k, 0)),
                  pl.BlockSpec((None, tk, n), lambda b, k: (rhs_of(b), k, 0))],
        out_specs=pl.BlockSpec((None, m, n), lambda b, k: (b, 0, 0)),
        out_shape=jax.ShapeDtypeStruct((nb, m, n), bf16),
        scratch_shapes=[pltpu.VMEM((m, n), f32)],
        compiler_params=_cp(56, ("arbitrary", "arbitrary")))(lhs, rhs)


def mix_in_fwd(name, h, g, winT):
    tp, d = h.shape
    nin = winT.shape[0]
    tm = _row_tile(tp)

    def body(h_ref, g_ref, w_ref, zq_ref, zg_ref, zu_ref, zgate_ref):
        hhat, _ = _rms(h_ref[...])
        z = _dot_nt((hhat * g_ref[...]).astype(bf16), w_ref[...])
        zq_ref[...] = z[:, :3 * RW]
        zg_ref[...] = z[:, 3 * RW:4 * RW]
        zu_ref[...] = z[:, 4 * RW:5 * RW]
        zgate_ref[...] = z[:, 5 * RW:]

    row = lambda w: pl.BlockSpec((tm, w), lambda i: (i, 0))
    widths = (3 * RW, RW, RW, 2 * d)
    return pl.pallas_call(
        body, name=name, grid=(tp // tm,),
        in_specs=[row(d), _resident((1, d)), _resident((nin, d))],
        out_specs=[row(w) for w in widths],
        out_shape=[jax.ShapeDtypeStruct((tp, w), f32) for w in widths],
        compiler_params=_cp(56, ("arbitrary",)))(h, g, winT)


def mix_in_bwd(name, dq, dk, dv, dzg, dzu, dzgate, h, g, winT, dres):
    tp, d = h.shape
    nin = winT.shape[0]
    tm = _row_tile(tp)

    def body(dq_ref, dk_ref, dv_ref, dzg_ref, dzu_ref, dzgate_ref, h_ref, g_ref, w_ref, dres_ref, dh_ref, dz_ref, n_ref, dg_ref):
        dz = jnp.concatenate([dq_ref[...], dk_ref[...], dv_ref[...], dzg_ref[...], dzu_ref[...], dzgate_ref[...]], axis=-1)
        dn = _dot_nn(dz, w_ref[...])
        hhat, rs = _rms(h_ref[...])
        gv = g_ref[...]
        dh_ref[...] = dres_ref[...] + _rms_bwd(dn, gv, hhat, rs)

        @pl.when(pl.program_id(0) == 0)
        def _():
            dg_ref[...] = jnp.zeros_like(dg_ref)

        dg_ref[0:1, :] += jnp.sum(dn * hhat, axis=0, keepdims=True)
        dz_ref[...] = dz
        n_ref[...] = (hhat * gv).astype(bf16)

    row = lambda w: pl.BlockSpec((tm, w), lambda i: (i, 0))
    return pl.pallas_call(
        body, name=name, grid=(tp // tm,),
        in_specs=[row(RW)] * 5 + [row(2 * d), row(d), _resident((1, d)), _resident((nin, d)), row(d)],
        out_specs=[row(d), pl.BlockSpec((None, tm, nin), lambda i: (0, i, 0)), pl.BlockSpec((None, tm, d), lambda i: (0, i, 0)),
                   pl.BlockSpec((8, d), lambda i: (0, 0))],
        out_shape=[jax.ShapeDtypeStruct((tp, d), f32), jax.ShapeDtypeStruct((1, tp, nin), bf16),
                   jax.ShapeDtypeStruct((1, tp, d), bf16), jax.ShapeDtypeStruct((8, d), f32)],
        compiler_params=_cp(56, ("arbitrary",)))(dq, dk, dv, dzg, dzu, dzgate, h, g, winT, dres)


def _retention_tables(tp, pad):
    half = HD // 2
    inv_freq = ROPE_BASE ** (-jnp.arange(half, dtype=f32) / half)
    pos = jnp.arange(tp, dtype=f32) - pad
    ang = pos[:, None] * inv_freq[None, :]
    cos, sin = jnp.cos(ang), jnp.sin(ang)
    cos2 = jnp.concatenate([cos, cos], axis=-1)
    sin2 = jnp.concatenate([-sin, sin], axis=-1)
    log_gamma = jnp.log1p(-(2.0 ** (-5.0 - jnp.arange(HEADS, dtype=f32))))
    idx = jnp.arange(CHUNK, dtype=f32)
    diff = idx[:, None] - idx[None, :]
    intra = jnp.where(diff[None] >= 0, jnp.exp(diff[None] * log_gamma[:, None, None]), 0.0)
    k_decay = jnp.exp((CHUNK - 1.0 - idx)[None, :] * log_gamma[:, None])
    q_decay = jnp.exp((idx + 1.0)[None, :] * log_gamma[:, None])
    chunk_decay = jnp.exp(CHUNK * log_gamma)
    full = (HEADS, CHUNK, HD)
    dec = jnp.stack([intra, jnp.broadcast_to(k_decay[:, :, None], full), jnp.broadcast_to(q_decay[:, :, None], full),
                     jnp.broadcast_to(chunk_decay[:, None, None], full)], axis=1)
    return cos2, sin2, dec


def _rot(t, cos2, sin2):
    return t * cos2 + pltpu.roll(t, HD // 2, 1) * sin2


def _rot_t(t, cos2, sin2):
    return t * cos2 - pltpu.roll(t, HD // 2, 1) * sin2


def retention_fwd(name, zq, cos2, sin2, dec):
    tp = zq.shape[0]
    nch = tp // CHUNK
    scale = HD ** -0.5

    def body(q_ref, k_ref, v_ref, cos_ref, sin_ref, dec_ref, out_ref, st_ref, s_ref):
        s_ref[...] = jnp.zeros_like(s_ref)
        intra, kdec, qdec, cdec = dec_ref[0], dec_ref[1], dec_ref[2], dec_ref[3]

        def chunk(n, carry):
            rows = pl.ds(pl.multiple_of(n * CHUNK, CHUNK), CHUNK)
            cosv, sinv = cos_ref[rows, :], sin_ref[rows, :]
            qr = _rot(q_ref[rows, :], cosv, sinv) * scale
            kr = _rot(k_ref[rows, :], cosv, sinv)
            vb = v_ref[rows, :].astype(bf16)
            sc = (_dot_nt(qr.astype(bf16), kr.astype(bf16)) * intra).astype(bf16)
            sv = s_ref[...]
            sb = sv.astype(bf16)
            out_ref[rows, :] = _dot_nn(sc, vb) + _dot_nn((qr * qdec).astype(bf16), sb)
            st_ref[n] = sb
            s_ref[...] = sv * cdec + _dot_tn((kr * kdec).astype(bf16), vb)
            return carry

        lax.fori_loop(0, nch, chunk, 0)

    col = lambda off: pl.BlockSpec((tp, HD), lambda h: (0, off + h))
    return pl.pallas_call(
        body, name=name, grid=(HEADS,),
        in_specs=[col(0), col(HEADS), col(2 * HEADS), _resident((tp, HD)), _resident((tp, HD)),
                  pl.BlockSpec((None, 4, CHUNK, HD), lambda h: (h, 0, 0, 0))],
        out_specs=[col(0), pl.BlockSpec((None, nch, HD, HD), lambda h: (h, 0, 0, 0))],
        out_shape=[jax.ShapeDtypeStruct((tp, RW), f32), jax.ShapeDtypeStruct((HEADS, nch, HD, HD), bf16)],
        scratch_shapes=[pltpu.VMEM((HD, HD), f32)],
        compiler_params=_cp(48, ("arbitrary",)))(zq, zq, zq, cos2, sin2, dec)


def retention_bwd(name, zq, cos2, sin2, dec, states, dout, pad):
    tp = zq.shape[0]
    nch = tp // CHUNK
    scale = HD ** -0.5

    def body(q_ref, k_ref, v_ref, cos_ref, sin_ref, dec_ref, st_ref, do_ref, dq_ref, dk_ref, dv_ref, g_ref):
        g_ref[...] = jnp.zeros_like(g_ref)
        intra, kdec, qdec, cdec = dec_ref[0], dec_ref[1], dec_ref[2], dec_ref[3]

        def chunk(t, carry):
            n = nch - 1 - t
            rows = pl.ds(pl.multiple_of(n * CHUNK, CHUNK), CHUNK)
            cosv, sinv = cos_ref[rows, :], sin_ref[rows, :]
            qr = _rot(q_ref[rows, :], cosv, sinv) * scale
            kr = _rot(k_ref[rows, :], cosv, sinv)
            qb, kb = qr.astype(bf16), kr.astype(bf16)
            vb = v_ref[rows, :].astype(bf16)
            qd = (qr * qdec).astype(bf16)
            kd = (kr * kdec).astype(bf16)
            sc = (_dot_nt(qb, kb) * intra).astype(bf16)
            dob = do_ref[rows, :]
            sb = st_ref[n]
            gv = g_ref[...]
            gb = gv.astype(bf16)
            dsc = (_dot_nt(dob, vb) * intra).astype(bf16)
            dv = _dot_tn(sc, dob) + _dot_nn(kd, gb)
            dqr = _dot_nn(dsc, kb) + _dot_nt(dob, sb) * qdec
            dkr = _dot_tn(dsc, qb) + _dot_nt(vb, gb) * kdec
            g_ref[...] = gv * cdec + _dot_tn(qd, dob)
            keep = (lax.broadcasted_iota(jnp.int32, (CHUNK, HD), 0) + n * CHUNK) >= pad
            dq_ref[rows, :] = jnp.where(keep, _rot_t(dqr * scale, cosv, sinv), 0.0).astype(bf16)
            dk_ref[rows, :] = jnp.where(keep, _rot_t(dkr, cosv, sinv), 0.0).astype(bf16)
            dv_ref[rows, :] = jnp.where(keep, dv, 0.0).astype(bf16)
            return carry

        lax.fori_loop(0, nch, chunk, 0)

    col = lambda off: pl.BlockSpec((tp, HD), lambda h: (0, off + h))
    return pl.pallas_call(
        body, name=name, grid=(HEADS,),
        in_specs=[col(0), col(HEADS), col(2 * HEADS), _resident((tp, HD)), _resident((tp, HD)),
                  pl.BlockSpec((None, 4, CHUNK, HD), lambda h: (h, 0, 0, 0)),
                  pl.BlockSpec((None, nch, HD, HD), lambda h: (h, 0, 0, 0)), col(0)],
        out_specs=[col(0)] * 3,
        out_shape=[jax.ShapeDtypeStruct((tp, RW), bf16)] * 3,
        scratch_shapes=[pltpu.VMEM((HD, HD), f32)],
        compiler_params=_cp(48, ("arbitrary",)))(zq, zq, zq, cos2, sin2, dec, states, dout)


def _window_sum(xv, steps, tp, forward):
    s = xv
    for j in range(steps):
        sh = 2 ** j
        s = s + pltpu.roll(s, (tp - sh) if forward else sh, 0)
    return s


def pool_fwd(name, zu, maps, scale, pad):
    tp = zu.shape[0]

    def body(u_ref, maps_ref, scale_ref, pooled_ref, p_ref):
        row = lax.broadcasted_iota(jnp.int32, (tp, HD), 0)
        for gi, w in enumerate(POOL_WINDOWS):
            cols = slice(gi * HD, (gi + 1) * HD)
            xv = u_ref[:, cols]
            cnt = jnp.clip(row - (pad - 1), 1, w).astype(f32)
            pooled = jnp.where(row >= pad, _window_sum(xv, gi + 1, tp, False) / cnt - xv, 0.0).astype(bf16)
            pooled_ref[:, cols] = pooled
            p_ref[:, cols] = (_dot_nn(pooled, maps_ref[gi].astype(bf16)) * scale_ref[:, cols]).astype(bf16)

    return pl.pallas_call(
        body, name=name,
        out_shape=[jax.ShapeDtypeStruct((tp, RW), bf16), jax.ShapeDtypeStruct((tp, RW), bf16)],
        compiler_params=_cp(56))(zu, maps, scale)


def pool_bwd(name, dp, pooled, maps, scale, pad):
    tp = dp.shape[0]

    def body(dp_ref, pooled_ref, maps_ref, scale_ref, du_ref, dmaps_ref, dscale_ref):
        row = lax.broadcasted_iota(jnp.int32, (tp, HD), 0)
        dscale_ref[...] = jnp.zeros_like(dscale_ref)
        for gi, w in enumerate(POOL_WINDOWS):
            cols = slice(gi * HD, (gi + 1) * HD)
            mb = maps_ref[gi].astype(bf16)
            pooled = pooled_ref[:, cols]
            dpf = dp_ref[:, cols].astype(f32)
            dscale_ref[0:1, cols] = jnp.sum(dpf * _dot_nn(pooled, mb), axis=0, keepdims=True)
            dpm = (dpf * scale_ref[:, cols]).astype(bf16)
            dmaps_ref[gi * HD:(gi + 1) * HD, :] = _dot_tn(pooled, dpm)
            dpool = jnp.where(row >= pad, _dot_nt(dpm, mb), 0.0)
            cnt = jnp.clip(row - (pad - 1), 1, w).astype(f32)
            du = _window_sum(dpool / cnt, gi + 1, tp, True) - dpool
            du_ref[:, cols] = jnp.where(row >= pad, du, 0.0).astype(bf16)

    return pl.pallas_call(
        body, name=name,
        out_shape=[jax.ShapeDtypeStruct((tp, RW), bf16), jax.ShapeDtypeStruct((HEADS * HD, HD), f32),
                   jax.ShapeDtypeStruct((8, RW), f32)],
        compiler_params=_cp(56))(dp, pooled, maps, scale)


def _group_norm(o):
    mu = jnp.mean(o, axis=-1, keepdims=True)
    oc = o - mu
    rstd = lax.rsqrt(jnp.mean(oc * oc, axis=-1, keepdims=True) + EPS)
    return oc * rstd, rstd


def mix_out_fwd(name, h, oraw, zg, zgate, p, wretT, wpoolT, wout):
    tp, d = h.shape
    tm = _row_tile(tp)

    def body(h_ref, o_ref, zg_ref, zgate_ref, p_ref, wr_ref, wp_ref, wo_ref, ho_ref, rp_ref, ret_ref, pool_ref, mixed_ref):
        parts = []
        for hh in range(HEADS):
            cols = slice(hh * HD, (hh + 1) * HD)
            rhat, _ = _group_norm(o_ref[:, cols])
            gv = zg_ref[:, cols]
            parts.append(rhat * (gv * _sigmoid(gv)))
        r = jnp.concatenate(parts, axis=-1).astype(bf16)
        pv = p_ref[...]
        ret = _dot_nt(r, wr_ref[...])
        pool = _dot_nt(pv, wp_ref[...])
        mixed = (_sigmoid(zgate_ref[:, :d]) * ret + _sigmoid(zgate_ref[:, d:]) * pool).astype(bf16)
        ho_ref[...] = h_ref[...] + _dot_nn(mixed, wo_ref[...])
        rp_ref[0] = r
        rp_ref[1] = pv
        ret_ref[...] = ret.astype(bf16)
        pool_ref[...] = pool.astype(bf16)
        mixed_ref[...] = mixed

    row = lambda w: pl.BlockSpec((tm, w), lambda i: (i, 0))
    return pl.pallas_call(
        body, name=name, grid=(tp // tm,),
        in_specs=[row(d), row(RW), row(RW), row(2 * d), row(RW), _resident((d, RW)), _resident((d, RW)), _resident((d, d))],
        out_specs=[row(d), pl.BlockSpec((2, tm, RW), lambda i: (0, i, 0)), row(d), row(d),
                   pl.BlockSpec((None, tm, d), lambda i: (0, i, 0))],
        out_shape=[jax.ShapeDtypeStruct((tp, d), f32), jax.ShapeDtypeStruct((2, tp, RW), bf16),
                   jax.ShapeDtypeStruct((tp, d), bf16), jax.ShapeDtypeStruct((tp, d), bf16),
                   jax.ShapeDtypeStruct((1, tp, d), bf16)],
        compiler_params=_cp(48, ("arbitrary",)))(h, oraw, zg, zgate, p, wretT, wpoolT, wout)


def mix_out_bwd(name, dy, oraw, zg, zgate, ret, pool, wretT, wpoolT, wout):
    tp, d = dy.shape
    tm = _row_tile(tp)

    def body(dy_ref, o_ref, zg_ref, zgate_ref, ret_ref, pool_ref, wr_ref, wp_ref, wo_ref,
             do_ref, dzg_ref, dzgate_ref, dp_ref, drp_ref, dyb_ref):
        dyb = dy_ref[...].astype(bf16)
        dmixed = _dot_nt(dyb, wo_ref[...])
        sa = _sigmoid(zgate_ref[:, :d])
        sb = _sigmoid(zgate_ref[:, d:])
        dret = dmixed * sa
        dpool = dmixed * sb
        dzgate_ref[:, :d] = (dret * ret_ref[...].astype(f32) * (1.0 - sa)).astype(bf16)
        dzgate_ref[:, d:] = (dpool * pool_ref[...].astype(f32) * (1.0 - sb)).astype(bf16)
        dretb, dpoolb = dret.astype(bf16), dpool.astype(bf16)
        dr = _dot_nn(dretb, wr_ref[...])
        dp_ref[...] = _dot_nn(dpoolb, wp_ref[...]).astype(bf16)
        for hh in range(HEADS):
            cols = slice(hh * HD, (hh + 1) * HD)
            rhat, rstd = _group_norm(o_ref[:, cols])
            gv = zg_ref[:, cols]
            sg = _sigmoid(gv)
            drh = dr[:, cols]
            drhat = drh * (gv * sg)
            dzg_ref[:, cols] = (drh * rhat * (sg * (1.0 + gv * (1.0 - sg)))).astype(bf16)
            do = rstd * (drhat - jnp.mean(drhat, axis=-1, keepdims=True)
                         - rhat * jnp.mean(drhat * rhat, axis=-1, keepdims=True))
            do_ref[:, cols] = do.astype(bf16)
        drp_ref[0] = dretb
        drp_ref[1] = dpoolb
        dyb_ref[...] = dyb

    row = lambda w: pl.BlockSpec((tm, w), lambda i: (i, 0))
    return pl.pallas_call(
        body, name=name, grid=(tp // tm,),
        in_specs=[row(d), row(RW), row(RW), row(2 * d), row(d), row(d), _resident((d, RW)), _resident((d, RW)),
                  _resident((d, d))],
        out_specs=[row(RW), row(RW), row(2 * d), row(RW), pl.BlockSpec((2, tm, d), lambda i: (0, i, 0)),
                   pl.BlockSpec((None, tm, d), lambda i: (0, i, 0))],
        out_shape=[jax.ShapeDtypeStruct((tp, RW), bf16), jax.ShapeDtypeStruct((tp, RW), bf16),
                   jax.ShapeDtypeStruct((tp, 2 * d), bf16), jax.ShapeDtypeStruct((tp, RW), bf16),
                   jax.ShapeDtypeStruct((2, tp, d), bf16), jax.ShapeDtypeStruct((1, tp, d), bf16)],
        compiler_params=_cp(48, ("arbitrary",)))(dy, oraw, zg, zgate, ret, pool, wretT, wpoolT, wout)


def final_loss(name, h, g, target):
    tp, d = h.shape
    tm = CHUNK

    def body(h_ref, g_ref, t_ref, dh_ref, loss_ref, dg_ref):
        i = pl.program_id(0)
        hhat, rs = _rms(h_ref[...])
        gv = g_ref[...]
        err = jnp.where(i >= 1, hhat * gv - t_ref[...], 0.0)
        dyv = err / d
        dh_ref[...] = _rms_bwd(dyv, gv, hhat, rs)

        @pl.when(i == 0)
        def _():
            loss_ref[...] = jnp.zeros_like(loss_ref)
            dg_ref[...] = jnp.zeros_like(dg_ref)

        loss_ref[...] += 0.5 * jnp.sum(jnp.sum(err * err, axis=-1, keepdims=True) / d)
        dg_ref[0:1, :] += jnp.sum(dyv * hhat, axis=0, keepdims=True)

    return pl.pallas_call(
        body, name=name, grid=(tp // tm,),
        in_specs=[pl.BlockSpec((tm, d), lambda i: (i, 0)), _resident((1, d)),
                  pl.BlockSpec((tm, d), lambda i: (jnp.maximum(i - 1, 0), 0))],
        out_specs=[pl.BlockSpec((tm, d), lambda i: (i, 0)), pl.BlockSpec((8, 128), lambda i: (0, 0)),
                   pl.BlockSpec((8, d), lambda i: (0, 0))],
        out_shape=[jax.ShapeDtypeStruct((tp, d), f32), jax.ShapeDtypeStruct((8, 128), f32),
                   jax.ShapeDtypeStruct((8, d), f32)],
        compiler_params=_cp(32, ("arbitrary",)))(h, g, target)


def _adamw(w, g, m, v):
    m = ADAM_B1 * m + (1.0 - ADAM_B1) * g
    v = ADAM_B2 * v + (1.0 - ADAM_B2) * (g * g)
    m_hat = m / (1.0 - ADAM_B1 ** ADAM_STEP)
    v_hat = v / (1.0 - ADAM_B2 ** ADAM_STEP)
    delta = -ADAM_LR * (m_hat / (jnp.sqrt(v_hat) + ADAM_EPS) + ADAM_WD * w)
    return delta, m, v


def adam_big(name, recv, b, layer, transposed, w, m, v, prev):
    r, c = recv.shape[2:]
    wshape = w.shape[1:]

    def body(*refs):
        recv_ref, w_ref, m_ref, v_ref = refs[:4]
        g_ref, d_ref, nm_ref, nv_ref = refs[-4:]
        g = recv_ref[0].astype(f32)
        for j in range(1, NDEV):
            g = g + recv_ref[j].astype(f32)
        if transposed:
            g = g.T
        delta, nm, nv = _adamw(w_ref[...], g, m_ref[...], v_ref[...])
        g_ref[...] = g
        d_ref[...] = delta
        nm_ref[...] = nm
        nv_ref[...] = nv

    wspec = pl.BlockSpec((None,) + wshape, lambda i: (layer, 0, 0))
    in_specs = [pl.BlockSpec((NDEV, None, r, c), lambda i: (0, b, 0, 0)), wspec, wspec, wspec]
    args = [recv, w, m, v]
    aliases = {}
    if prev is not None:
        in_specs += [pl.BlockSpec(memory_space=pl.ANY)] * 4
        args += list(prev)
        aliases = {4 + k: k for k in range(4)}
    return pl.pallas_call(
        body, name=name, grid=(1,), in_specs=in_specs, out_specs=[wspec] * 4,
        out_shape=[jax.ShapeDtypeStruct(w.shape, f32)] * 4, input_output_aliases=aliases,
        compiler_params=_cp(56))(*args)


def adam_small(name, ga, gmaps, gmeta, norms, pool_scale, pool_maps, meta, final_norm, d):
    nd = d // NDEV

    def body(ga_ref, gmaps_ref, gmeta_ref, *refs):
        ins, outs = refs[:21], refs[21:]
        x, y, c = _me()
        me = 4 * x + 2 * y + c

        def total(ref, rows):
            t = ref[0, rows, :]
            for j in range(1, NDEV):
                t = t + ref[j, rows, :]
            return t

        outs[0][...] = total(ga_ref, slice(0, 8))[:, :128]

        def update(k, g, o):
            w_ref, m_ref, v_ref = ins[3 * k:3 * k + 3]
            delta, nm, nv = _adamw(w_ref[...], g, m_ref[...], v_ref[...])
            for ref, val in zip(outs[o:o + 4], (g, delta, nm, nv)):
                ref[...] = val

        two = lax.broadcasted_iota(jnp.int32, (2, d), 0)
        for k in range(3):
            g0 = total(ga_ref, slice(16 + 8 * k, 17 + 8 * k))
            g1 = total(ga_ref, slice(40 + 8 * k, 41 + 8 * k))
            update(k, jnp.where(two == 0, g0, g1), 1 + 4 * k)
        s0 = total(ga_ref, slice(64, 65))[:, :RW]
        s1 = total(ga_ref, slice(72, 73))[:, :RW]
        update(3, jnp.where(two[:, :RW] == 0, s0, s1), 13)
        update(4, total(gmaps_ref, slice(None)), 17)
        update(5, total(gmeta_ref, pl.ds(pl.multiple_of(me * N_META, N_META), N_META)), 21)
        update(6, total(ga_ref, slice(8, 9)), 25)

    flat = []
    for trip in (*norms, pool_scale, pool_maps, meta, final_norm):
        flat += list(trip)
    out_shapes = [jax.ShapeDtypeStruct((8, 128), f32)]
    for trip in (*norms, pool_scale, pool_maps, meta, final_norm):
        out_shapes += [jax.ShapeDtypeStruct(trip[0].shape, f32)] * 4
    return pl.pallas_call(body, name=name, out_shape=out_shapes, compiler_params=_cp(32))(ga, gmaps, gmeta, *flat)


def kernel(x, meta, ffn1_norm, ffn1_gate, ffn1_up, ffn1_down, mix_norm, w_in, pool_maps, pool_scale, w_ret_up, w_pool_up, w_out, ffn2_norm, ffn2_gate, ffn2_up, ffn2_down, final_norm, loss_target, m_meta, m_ffn1_norm, m_ffn1_gate, m_ffn1_up, m_ffn1_down, m_mix_norm, m_w_in, m_pool_maps, m_pool_scale, m_w_ret_up, m_w_pool_up, m_w_out, m_ffn2_norm, m_ffn2_gate, m_ffn2_up, m_ffn2_down, m_final_norm, v_meta, v_ffn1_norm, v_ffn1_gate, v_ffn1_up, v_ffn1_down, v_mix_norm, v_w_in, v_pool_maps, v_pool_scale, v_w_ret_up, v_w_pool_up, v_w_out, v_ffn2_norm, v_ffn2_gate, v_ffn2_up, v_ffn2_down, v_final_norm):
    seq, d = x.shape[1], x.shape[2]
    depth = ffn1_gate.shape[0]
    ff = ffn1_gate.shape[2] * NDEV
    nin = w_in.shape[2] * NDEV
    length = seq + N_META
    pad = (-length) % CHUNK
    tp = length + pad
    assert pad % 8 == 0 and pad + N_META == CHUNK and depth == 2 and nin == 5 * RW + 2 * d

    gathered = []
    for layer in range(depth):
        shards = prep_layer(layer, [ffn1_gate, ffn1_up, ffn2_gate, ffn2_up, w_in, w_ret_up, w_pool_up],
                            [ffn1_down, ffn2_down, w_out])
        full = all_gather(f"gather_weights{layer}", shards)
        gathered.append([a.reshape((NDEV * a.shape[1],) + a.shape[2:]) for a in full])
    meta_all, = all_gather("gather_meta", [meta])
    meta_full = jnp.transpose(meta_all, (1, 0, 2)).reshape(N_META, d)

    cos2, sin2, dec = _retention_tables(tp, pad)
    h = jnp.concatenate([jnp.zeros((pad, d), f32), meta_full, x[0]], axis=0)

    saved = []
    for layer in range(depth):
        g1T, u1T, g2T, u2T, winT, wretT, wpoolT, d1, d2, wout = gathered[layer]
        row = lambda a: a[layer:layer + 1]
        s = {"h0": h}
        h, s["a1"], s["b1"] = ffn_fwd(f"ffn1_fwd{layer}", h, row(ffn1_norm), g1T, u1T, d1)
        s["h1"] = h
        s["zq"], s["zg"], zu, s["zgate"] = mix_in_fwd(f"mix_in_fwd{layer}", h, row(mix_norm), winT)
        s["oraw"], s["states"] = retention_fwd(f"retention_fwd{layer}", s["zq"], cos2, sin2, dec)
        s["pooled"], p = pool_fwd(f"pool_fwd{layer}", zu, pool_maps[layer], row(pool_scale), pad)
        h, s["rp"], s["ret"], s["pool"], s["mixed"] = mix_out_fwd(
            f"mix_out_fwd{layer}", h, s["oraw"], s["zg"], s["zgate"], p, wretT, wpoolT, wout)
        s["h2"] = h
        h, s["a2"], s["b2"] = ffn_fwd(f"ffn2_fwd{layer}", h, row(ffn2_norm), g2T, u2T, d2)
        saved.append(s)

    dh, loss_part, dg_final = final_loss("final_loss", h, final_norm.reshape(1, d), loss_target[0])

    small = {}
    recv = [None] * depth
    for layer in reversed(range(depth)):
        g1T, u1T, g2T, u2T, winT, wretT, wpoolT, d1, d2, wout = gathered[layer]
        row = lambda a: a[layer:layer + 1]
        s = saved[layer]
        dh, lhs2, rhs2, small[("ffn2", layer)] = ffn_bwd(
            f"ffn2_bwd{layer}", dh, s["h2"], row(ffn2_norm), s["a2"], s["b2"], g2T, u2T, d2)
        gw_ffn2 = mm_tn(f"ffn2_wgrad{layer}", lhs2, rhs2, lambda b: b // 2)
        do, dzg, dzgate, dp, drp, dyb = mix_out_bwd(
            f"mix_out_bwd{layer}", dh, s["oraw"], s["zg"], s["zgate"], s["ret"], s["pool"], wretT, wpoolT, wout)
        gw_out = mm_tn(f"w_out_wgrad{layer}", s["mixed"], dyb, lambda b: b)
        gw_rp = mm_tn(f"up_wgrad{layer}", drp, s["rp"], lambda b: b)
        dq, dk, dv = retention_bwd(f"retention_bwd{layer}", s["zq"], cos2, sin2, dec, s["states"], do, pad)
        dzu, small[("maps", layer)], small[("scale", layer)] = pool_bwd(
            f"pool_bwd{layer}", dp, s["pooled"], pool_maps[layer], row(pool_scale), pad)
        dh, dz, n2, small[("mix", layer)] = mix_in_bwd(
            f"mix_in_bwd{layer}", dq, dk, dv, dzg, dzu, dzgate, s["h1"], row(mix_norm), winT, dh)
        gw_in = mm_tn(f"w_in_wgrad{layer}", dz, n2, lambda b: b)
        dh, lhs1, rhs1, small[("ffn1", layer)] = ffn_bwd(
            f"ffn1_bwd{layer}", dh, s["h0"], row(ffn1_norm), s["a1"], s["b1"], g1T, u1T, d1)
        gw_ffn1 = mm_tn(f"ffn1_wgrad{layer}", lhs1, rhs1, lambda b: b // 2)
        grads = [gw_ffn1, gw_ffn2, gw_in, gw_out, gw_rp]
        recv[layer] = shard_exchange(
            f"exchange_grads{layer}", [g.reshape(g.shape[0], NDEV, g.shape[1] // NDEV, g.shape[2]) for g in grads])

    grad_x = dh[CHUNK:][None]

    big = {}
    for layer in reversed(range(depth)):
        r_ffn1, r_ffn2, r_in, r_out, r_rp = recv[layer]
        plan = [("ffn1_gate", r_ffn1, 0, True, ffn1_gate, m_ffn1_gate, v_ffn1_gate),
                ("ffn1_up", r_ffn1, 1, True, ffn1_up, m_ffn1_up, v_ffn1_up),
                ("ffn1_down", r_ffn1, 2, False, ffn1_down, m_ffn1_down, v_ffn1_down),
                ("w_in", r_in, 0, True, w_in, m_w_in, v_w_in),
                ("w_ret_up", r_rp, 0, True, w_ret_up, m_w_ret_up, v_w_ret_up),
                ("w_pool_up", r_rp, 1, True, w_pool_up, m_w_pool_up, v_w_pool_up),
                ("w_out", r_out, 0, False, w_out, m_w_out, v_w_out),
                ("ffn2_gate", r_ffn2, 0, True, ffn2_gate, m_ffn2_gate, v_ffn2_gate),
                ("ffn2_up", r_ffn2, 1, True, ffn2_up, m_ffn2_up, v_ffn2_up),
                ("ffn2_down", r_ffn2, 2, False, ffn2_down, m_ffn2_down, v_ffn2_down)]
        for nm, rv, b, tr, w, m, v in plan:
            big[nm] = adam_big(f"adam_{nm}{layer}", rv, b, layer, tr, w, m, v, big.get(nm))

    def row8(a):
        return a if a.shape[1] == d else jnp.pad(a, ((0, 0), (0, d - a.shape[1])))

    loss_rows = jnp.pad(loss_part, ((0, 0), (0, d - 128))) if d > 128 else loss_part
    parts = [loss_rows, dg_final]
    for layer in range(depth):
        parts += [small[("ffn1", layer)], small[("mix", layer)], small[("ffn2", layer)]]
    parts += [row8(small[("scale", layer)]) for layer in range(depth)]
    pack_a = jnp.concatenate(parts, axis=0)
    pack_maps = jnp.concatenate([small[("maps", layer)] for layer in range(depth)], axis=0)
    dmeta = dh[pad:CHUNK]
    pack_meta = jnp.transpose(dmeta.reshape(N_META, NDEV, d // NDEV), (1, 0, 2)).reshape(NDEV * N_META, d // NDEV)
    ga, gmaps, gmeta = all_gather("gather_small_grads", [pack_a, pack_maps, pack_meta])

    maps2 = lambda a: a.reshape(depth * HEADS * HD, HD)
    res = adam_small(
        "adam_small", ga, gmaps, gmeta,
        [(ffn1_norm, m_ffn1_norm, v_ffn1_norm), (mix_norm, m_mix_norm, v_mix_norm), (ffn2_norm, m_ffn2_norm, v_ffn2_norm)],
        (pool_scale, m_pool_scale, v_pool_scale), (maps2(pool_maps), maps2(m_pool_maps), maps2(v_pool_maps)),
        (meta, m_meta, v_meta), tuple(a.reshape(1, d) for a in (final_norm, m_final_norm, v_final_norm)), d)
    loss = res[0][0, 0]
    sm = {}
    for k, nm in enumerate(["ffn1_norm", "mix_norm", "ffn2_norm", "pool_scale", "pool_maps", "meta", "final_norm"]):
        sm[nm] = list(res[1 + 4 * k:5 + 4 * k])
    sm["pool_maps"] = [a.reshape(pool_maps.shape) for a in sm["pool_maps"]]
    sm["final_norm"] = [a.reshape(d) for a in sm["final_norm"]]

    names = ["meta", "ffn1_norm", "ffn1_gate", "ffn1_up", "ffn1_down", "mix_norm", "w_in", "pool_maps", "pool_scale",
             "w_ret_up", "w_pool_up", "w_out", "ffn2_norm", "ffn2_gate", "ffn2_up", "ffn2_down", "final_norm"]
    allw = {**{k: list(v) for k, v in big.items()}, **sm}
    outs = [loss, grad_x]
    for kind in range(4):
        outs += [allw[nm][kind] for nm in names]
    return tuple(outs)
```

```python
import functools

import jax
import jax.numpy as jnp
from jax import lax
from jax.experimental import pallas as pl
from jax.experimental.pallas import tpu as pltpu

f32 = jnp.float32
bf16 = jnp.bfloat16
MESH = pl.DeviceIdType.MESH
NDEV = 8
N_META = 16
HEADS = 4
HD = 128
CHUNK = 128
RW = HEADS * HD
POOL_WINDOWS = (2, 4, 8, 16)
ROPE_BASE = 10000.0
EPS = 1e-6
ADAM_LR = 0.001
ADAM_B1 = 0.9
ADAM_B2 = 0.999
ADAM_EPS = 1e-08
ADAM_WD = 0.01
ADAM_STEP = 10
VMEM_CAP_MB = 60


def _cp(vmem_mb, sem=None):
    return pltpu.CompilerParams(vmem_limit_bytes=min(vmem_mb, VMEM_CAP_MB) * 2**20, dimension_semantics=sem)


def _row_tile(tp, want=384):
    return want if tp % want == 0 else 128


def _resident(shape):
    nd = len(shape)
    return pl.BlockSpec(shape, lambda *_: (0,) * nd, pipeline_mode=pl.Buffered(1))


def _skip(nd, body):
    return (lambda *refs: body(*refs[nd:])) if nd else body


def _dot_nn(a, b):
    return lax.dot_general(a, b, (((1,), (0,)), ((), ())), preferred_element_type=f32)


def _dot_nt(a, b):
    return lax.dot_general(a, b, (((1,), (1,)), ((), ())), preferred_element_type=f32)


def _dot_tn(a, b):
    return lax.dot_general(a, b, (((0,), (0,)), ((), ())), preferred_element_type=f32)


def _rms(h):
    rs = lax.rsqrt(jnp.mean(h * h, axis=-1, keepdims=True) + EPS)
    return h * rs, rs


def _rms_bwd(dn, g, hhat, rs):
    dhh = dn * g
    return rs * (dhh - hhat * jnp.mean(dhh * hhat, axis=-1, keepdims=True))


def _sigmoid(x):
    return jax.nn.sigmoid(x)


def _me():
    return lax.axis_index("x"), lax.axis_index("y"), lax.axis_index("c")


def _peer(idx):
    return (idx // 4, (idx // 2) % 2, idx % 2)


def all_gather(name, arrays):
    n = len(arrays)

    def body(*refs):
        ins, outs = refs[:n], refs[n:2 * n]
        send_sems, recv_sems, local_sems = refs[2 * n:]
        x, y, c = _me()
        me = 4 * x + 2 * y + c
        locals_ = []
        for k in range(n):
            cp = pltpu.make_async_copy(ins[k], outs[k].at[me], local_sems.at[k])
            cp.start()
            locals_.append(cp)
        for d in range(1, NDEV):
            for k in range(n):
                pltpu.make_async_remote_copy(
                    src_ref=ins[k], dst_ref=outs[k].at[me], send_sem=send_sems.at[k], recv_sem=recv_sems.at[k],
                    device_id=_peer((me + d) % NDEV), device_id_type=MESH).start()
        for k in range(n):
            seven = outs[k].at[pl.ds(0, NDEV - 1)]
            w = pltpu.make_async_remote_copy(src_ref=seven, dst_ref=seven, send_sem=send_sems.at[k],
                                             recv_sem=recv_sems.at[k], device_id=(x, y, c), device_id_type=MESH)
            w.wait_send()
            w.wait_recv()
            locals_[k].wait()

    anyspec = pl.BlockSpec(memory_space=pl.ANY)
    return pl.pallas_call(
        body, name=name,
        out_shape=[jax.ShapeDtypeStruct((NDEV,) + a.shape, a.dtype) for a in arrays],
        in_specs=[anyspec] * n, out_specs=[anyspec] * n,
        scratch_shapes=[pltpu.SemaphoreType.DMA((n,)), pltpu.SemaphoreType.DMA((n,)), pltpu.SemaphoreType.DMA((n,))],
    )(*arrays)


_HBM = pl.BlockSpec(memory_space=pltpu.HBM)
_SEM = pl.BlockSpec(memory_space=pltpu.SEMAPHORE)
_ANY = pl.BlockSpec(memory_space=pl.ANY)
_EFFECT = pltpu.SideEffectType.DATAFLOW_SIDE_EFFECTING


def _in_hbm(a):
    return pltpu.with_memory_space_constraint(a, pltpu.HBM)


def gather_start(name, lands, deps=()):
    n, nd = len(lands), len(deps)

    def body(*refs):
        land = refs[nd:nd + n]
        send_sems, recv_sems = refs[nd + n:nd + n + 2]
        token = refs[-1]
        x, y, c = _me()
        me = 4 * x + 2 * y + c
        for d in range(1, NDEV):
            for k in range(n):
                pltpu.make_async_remote_copy(
                    src_ref=land[k].at[me], dst_ref=land[k].at[me], send_sem=send_sems.at[k], recv_sem=recv_sems.at[k],
                    device_id=_peer((me + d) % NDEV), device_id_type=MESH).start()
        token[...] = jnp.zeros_like(token)

    res = pl.pallas_call(
        body, name=name,
        out_shape=(pltpu.SemaphoreType.DMA((n,)), pltpu.SemaphoreType.DMA((n,)),
                   *[pltpu.HBM(a.shape, a.dtype) for a in lands], jax.ShapeDtypeStruct((8, 128), f32)),
        in_specs=[_ANY] * nd + [_HBM] * n,
        out_specs=(_SEM, _SEM, *[_HBM] * n, pl.BlockSpec(memory_space=pltpu.VMEM)),
        input_output_aliases={nd + k: 2 + k for k in range(n)},
        compiler_params=pltpu.CompilerParams(has_side_effects=_EFFECT),
    )(*deps, *[_in_hbm(a) for a in lands])
    return res[0], res[1], list(res[2:2 + n]), res[-1]


def exchange_start(name, grads, deps=()):
    n, nd = len(grads), len(deps)
    lands = [lax.empty((NDEV - 1, g.shape[0]) + g.shape[2:], g.dtype) for g in grads]

    def body(*refs):
        src = refs[nd:nd + n]
        land = refs[nd + n:nd + 2 * n]
        send_sems, recv_sems = refs[nd + 2 * n:nd + 2 * n + 2]
        token = refs[-1]
        x, y, c = _me()
        me = 4 * x + 2 * y + c
        for d in range(1, NDEV):
            p = (me + d) % NDEV
            for k in range(n):
                pltpu.make_async_remote_copy(
                    src_ref=src[k].at[:, p], dst_ref=land[k].at[d - 1], send_sem=send_sems.at[k], recv_sem=recv_sems.at[k],
                    device_id=_peer(p), device_id_type=MESH).start()
        token[...] = jnp.zeros_like(token)

    both = list(grads) + lands
    res = pl.pallas_call(
        body, name=name,
        out_shape=(pltpu.SemaphoreType.DMA((n,)), pltpu.SemaphoreType.DMA((n,)),
                   *[pltpu.HBM(a.shape, a.dtype) for a in both], jax.ShapeDtypeStruct((8, 128), f32)),
        in_specs=[_ANY] * nd + [_HBM] * (2 * n),
        out_specs=(_SEM, _SEM, *[_HBM] * (2 * n), pl.BlockSpec(memory_space=pltpu.VMEM)),
        input_output_aliases={nd + k: 2 + k for k in range(2 * n)},
        compiler_params=pltpu.CompilerParams(has_side_effects=_EFFECT),
    )(*deps, *[_in_hbm(a) for a in both])
    return res[0], res[1], list(res[2:2 + n]), list(res[2 + n:2 + 2 * n]), res[-1]


def copies_wait(name, send_sems, recv_sems, sent, lands, after):
    ns, n = len(sent), len(lands)

    def body(*refs):
        land = refs[ns:ns + n]
        ssem, rsem = refs[ns + n:ns + n + 2]
        x, y, c = _me()
        for k in range(n):
            seven = land[k].at[pl.ds(0, NDEV - 1)]
            w = pltpu.make_async_remote_copy(src_ref=seven, dst_ref=seven, send_sem=ssem.at[k], recv_sem=rsem.at[k],
                                             device_id=(x, y, c), device_id_type=MESH)
            w.wait_send()
            w.wait_recv()

    both = list(sent) + list(lands)
    res = pl.pallas_call(
        body, name=name, out_shape=tuple(pltpu.HBM(a.shape, a.dtype) for a in both),
        in_specs=[_HBM] * (ns + n) + [_SEM, _SEM, _ANY], out_specs=tuple([_HBM] * (ns + n)),
        input_output_aliases={k: k for k in range(ns + n)},
        compiler_params=pltpu.CompilerParams(has_side_effects=_EFFECT),
    )(*both, send_sems, recv_sems, after)
    return list(res[:ns]), list(res[ns:])


def prep_layer(layer, me, col_sharded, row_sharded):
    nc, nr = len(col_sharded), len(row_sharded)

    def body(me_ref, *refs):
        ins, outs = refs[:nc + nr], refs[nc + nr:]
        for k in range(nc):
            outs[k][...] = ins[k][...].T.astype(bf16)
        for k in range(nc, nc + nr):
            outs[k][...] = ins[k][...].astype(bf16)

    arrs = list(col_sharded) + list(row_sharded)
    in_specs = [pl.BlockSpec((None,) + a.shape[1:], lambda i, me_ref: (layer, 0, 0)) for a in arrs]
    shapes = [(a.shape[2], a.shape[1]) for a in col_sharded] + [a.shape[1:] for a in row_sharded]
    out_specs = [pl.BlockSpec((None,) + s, lambda i, me_ref: (me_ref[0], 0, 0)) for s in shapes]
    return pl.pallas_call(
        body, name=f"prep_layer{layer}",
        grid_spec=pltpu.PrefetchScalarGridSpec(num_scalar_prefetch=1, grid=(1,), in_specs=in_specs, out_specs=out_specs),
        out_shape=[jax.ShapeDtypeStruct((NDEV,) + s, bf16) for s in shapes], compiler_params=_cp(48))(me, *arrs)


def ffn_fwd(name, h, g, wgT, wuT, wd, deps=()):
    tp, d = h.shape
    ff = wgT.shape[0]
    tm = _row_tile(tp)

    def body(h_ref, g_ref, wg_ref, wu_ref, wd_ref, ho_ref, a_ref, b_ref):
        hh = h_ref[...]
        hhat, _ = _rms(hh)
        n = (hhat * g_ref[...]).astype(bf16)
        a = _dot_nt(n, wg_ref[...])
        b = _dot_nt(n, wu_ref[...])
        s = (a * _sigmoid(a)) * b
        ho_ref[...] = hh + 0.5 * _dot_nn(s.astype(bf16), wd_ref[...])
        a_ref[...] = a.astype(bf16)
        b_ref[...] = b.astype(bf16)

    row = lambda w: pl.BlockSpec((tm, w), lambda i: (i, 0))
    return pl.pallas_call(
        _skip(len(deps), body), name=name, grid=(tp // tm,),
        in_specs=[_ANY] * len(deps) + [row(d), _resident((1, d)), _resident((ff, d)), _resident((ff, d)), _resident((ff, d))],
        out_specs=[row(d), row(ff), row(ff)],
        out_shape=[jax.ShapeDtypeStruct((tp, d), f32), jax.ShapeDtypeStruct((tp, ff), bf16),
                   jax.ShapeDtypeStruct((tp, ff), bf16)],
        compiler_params=_cp(56, ("arbitrary",)))(*deps, h, g, wgT, wuT, wd)


def ffn_bwd(name, dy, h, g, a, b, wgT, wuT, wd, deps=()):
    tp, d = h.shape
    ff = wgT.shape[0]
    tm = _row_tile(tp, 192)

    def body(dy_ref, h_ref, g_ref, a_ref, b_ref, wg_ref, wu_ref, wd_ref, dh_ref, lhs_ref, rhs_ref, dg_ref):
        dyv = dy_ref[...]
        hhat, rs = _rms(h_ref[...])
        gv = g_ref[...]
        n = hhat * gv
        dyh = (0.5 * dyv).astype(bf16)
        ds = _dot_nt(dyh, wd_ref[...])
        av = a_ref[...].astype(f32)
        bv = b_ref[...].astype(f32)
        sg = _sigmoid(av)
        sa = av * sg
        da = (ds * bv * (sg * (1.0 + av * (1.0 - sg)))).astype(bf16)
        db = (ds * sa).astype(bf16)
        dn = _dot_nn(da, wg_ref[...]) + _dot_nn(db, wu_ref[...])
        dh_ref[...] = dyv + _rms_bwd(dn, gv, hhat, rs)

        @pl.when(pl.program_id(0) == 0)
        def _():
            dg_ref[...] = jnp.zeros_like(dg_ref)

        dg_ref[0:1, :] += jnp.sum(dn * hhat, axis=0, keepdims=True)
        lhs_ref[0] = da
        lhs_ref[1] = db
        lhs_ref[2] = (sa * bv).astype(bf16)
        rhs_ref[0] = n.astype(bf16)
        rhs_ref[1] = dyh

    row = lambda w: pl.BlockSpec((tm, w), lambda i: (i, 0))
    return pl.pallas_call(
        _skip(len(deps), body), name=name, grid=(tp // tm,),
        in_specs=[_ANY] * len(deps) + [row(d), row(d), _resident((1, d)), row(ff), row(ff),
                  _resident((ff, d)), _resident((ff, d)), _resident((ff, d))],
        out_specs=[row(d), pl.BlockSpec((3, tm, ff), lambda i: (0, i, 0)), pl.BlockSpec((2, tm, d), lambda i: (0, i, 0)),
                   pl.BlockSpec((8, d), lambda i: (0, 0))],
        out_shape=[jax.ShapeDtypeStruct((tp, d), f32), jax.ShapeDtypeStruct((3, tp, ff), bf16),
                   jax.ShapeDtypeStruct((2, tp, d), bf16), jax.ShapeDtypeStruct((8, d), f32)],
        compiler_params=_cp(58, ("arbitrary",)))(*deps, dy, h, g, a, b, wgT, wuT, wd)


def mm_tn(name, lhs, rhs, rhs_of):
    nb, tp, m = lhs.shape
    n = rhs.shape[2]
    tk = _row_tile(tp)
    nk = tp // tk

    def body(l_ref, r_ref, o_ref, acc_ref):
        k = pl.program_id(1)

        @pl.when(k == 0)
        def _():
            acc_ref[...] = jnp.zeros_like(acc_ref)

        acc_ref[...] += _dot_tn(l_ref[...], r_ref[...])

        @pl.when(k == nk - 1)
        def _():
            o_ref[...] = acc_ref[...].astype(o_ref.dtype)

    return pl.pallas_call(
        body, name=name, grid=(nb, nk),
        in_specs=[pl.BlockSpec((None, tk, m), lambda b, k: (b, k, 0)),
                  pl.BlockSpec((None, tk, n), lambda b, k: (rhs_of(b), k, 0))],
        out_specs=pl.BlockSpec((None, m, n), lambda b, k: (b, 0, 0)),
        out_shape=jax.ShapeDtypeStruct((nb, m, n), bf16),
        scratch_shapes=[pltpu.VMEM((m, n), f32)],
        compiler_params=_cp(56, ("arbitrary", "arbitrary")))(lhs, rhs)


def mix_in_fwd(name, h, g, winT):
    tp, d = h.shape
    nin = winT.shape[0]
    tm = _row_tile(tp)

    def body(h_ref, g_ref, w_ref, zq_ref, zg_ref, zu_ref, zgate_ref):
        hhat, _ = _rms(h_ref[...])
        z = _dot_nt((hhat * g_ref[...]).astype(bf16), w_ref[...])
        zq_ref[...] = z[:, :3 * RW]
        zg_ref[...] = z[:, 3 * RW:4 * RW]
        zu_ref[...] = z[:, 4 * RW:5 * RW]
        zgate_ref[...] = z[:, 5 * RW:]

    row = lambda w: pl.BlockSpec((tm, w), lambda i: (i, 0))
    widths = (3 * RW, RW, RW, 2 * d)
    return pl.pallas_call(
        body, name=name, grid=(tp // tm,),
        in_specs=[row(d), _resident((1, d)), _resident((nin, d))],
        out_specs=[row(w) for w in widths],
        out_shape=[jax.ShapeDtypeStruct((tp, w), f32) for w in widths],
        compiler_params=_cp(56, ("arbitrary",)))(h, g, winT)


def mix_in_bwd(name, dq, dk, dv, dzg, dzu, dzgate, h, g, winT, dres):
    tp, d = h.shape
    nin = winT.shape[0]
    tm = _row_tile(tp)

    def body(dq_ref, dk_ref, dv_ref, dzg_ref, dzu_ref, dzgate_ref, h_ref, g_ref, w_ref, dres_ref, dh_ref, dz_ref, n_ref, dg_ref):
        dz = jnp.concatenate([dq_ref[...], dk_ref[...], dv_ref[...], dzg_ref[...], dzu_ref[...], dzgate_ref[...]], axis=-1)
        dn = _dot_nn(dz, w_ref[...])
        hhat, rs = _rms(h_ref[...])
        gv = g_ref[...]
        dh_ref[...] = dres_ref[...] + _rms_bwd(dn, gv, hhat, rs)

        @pl.when(pl.program_id(0) == 0)
        def _():
            dg_ref[...] = jnp.zeros_like(dg_ref)

        dg_ref[0:1, :] += jnp.sum(dn * hhat, axis=0, keepdims=True)
        dz_ref[...] = dz
        n_ref[...] = (hhat * gv).astype(bf16)

    row = lambda w: pl.BlockSpec((tm, w), lambda i: (i, 0))
    return pl.pallas_call(
        body, name=name, grid=(tp // tm,),
        in_specs=[row(RW)] * 5 + [row(2 * d), row(d), _resident((1, d)), _resident((nin, d)), row(d)],
        out_specs=[row(d), pl.BlockSpec((None, tm, nin), lambda i: (0, i, 0)), pl.BlockSpec((None, tm, d), lambda i: (0, i, 0)),
                   pl.BlockSpec((8, d), lambda i: (0, 0))],
        out_shape=[jax.ShapeDtypeStruct((tp, d), f32), jax.ShapeDtypeStruct((1, tp, nin), bf16),
                   jax.ShapeDtypeStruct((1, tp, d), bf16), jax.ShapeDtypeStruct((8, d), f32)],
        compiler_params=_cp(56, ("arbitrary",)))(dq, dk, dv, dzg, dzu, dzgate, h, g, winT, dres)


def _retention_tables(tp, pad):
    half = HD // 2
    inv_freq = ROPE_BASE ** (-jnp.arange(half, dtype=f32) / half)
    pos = jnp.arange(tp, dtype=f32) - pad
    ang = pos[:, None] * inv_freq[None, :]
    cos, sin = jnp.cos(ang), jnp.sin(ang)
    cos2 = jnp.concatenate([cos, cos], axis=-1)
    sin2 = jnp.concatenate([-sin, sin], axis=-1)
    log_gamma = jnp.log1p(-(2.0 ** (-5.0 - jnp.arange(HEADS, dtype=f32))))
    idx = jnp.arange(CHUNK, dtype=f32)
    diff = idx[:, None] - idx[None, :]
    intra = jnp.where(diff[None] >= 0, jnp.exp(diff[None] * log_gamma[:, None, None]), 0.0)
    k_decay = jnp.exp((CHUNK - 1.0 - idx)[None, :] * log_gamma[:, None])
    q_decay = jnp.exp((idx + 1.0)[None, :] * log_gamma[:, None])
    chunk_decay = jnp.exp(CHUNK * log_gamma)
    full = (HEADS, CHUNK, HD)
    dec = jnp.stack([intra, jnp.broadcast_to(k_decay[:, :, None], full), jnp.broadcast_to(q_decay[:, :, None], full),
                     jnp.broadcast_to(chunk_decay[:, None, None], full)], axis=1)
    return cos2, sin2, dec


def _rot(t, cos2, sin2):
    return t * cos2 + pltpu.roll(t, HD // 2, 1) * sin2


def _rot_t(t, cos2, sin2):
    return t * cos2 - pltpu.roll(t, HD // 2, 1) * sin2


def retention_fwd(name, zq, cos2, sin2, dec):
    tp = zq.shape[0]
    nch = tp // CHUNK
    scale = HD ** -0.5

    def body(q_ref, k_ref, v_ref, cos_ref, sin_ref, dec_ref, out_ref, st_ref, s_ref):
        s_ref[...] = jnp.zeros_like(s_ref)
        intra, kdec, qdec, cdec = dec_ref[0], dec_ref[1], dec_ref[2], dec_ref[3]

        def chunk(n, carry):
            rows = pl.ds(pl.multiple_of(n * CHUNK, CHUNK), CHUNK)
            cosv, sinv = cos_ref[rows, :], sin_ref[rows, :]
            qr = _rot(q_ref[rows, :], cosv, sinv) * scale
            kr = _rot(k_ref[rows, :], cosv, sinv)
            vb = v_ref[rows, :].astype(bf16)
            sc = (_dot_nt(qr.astype(bf16), kr.astype(bf16)) * intra).astype(bf16)
            sv = s_ref[...]
            sb = sv.astype(bf16)
            out_ref[rows, :] = _dot_nn(sc, vb) + _dot_nn((qr * qdec).astype(bf16), sb)
            st_ref[n] = sb
            s_ref[...] = sv * cdec + _dot_tn((kr * kdec).astype(bf16), vb)
            return carry

        lax.fori_loop(0, nch, chunk, 0)

    col = lambda off: pl.BlockSpec((tp, HD), lambda h: (0, off + h))
    return pl.pallas_call(
        body, name=name, grid=(HEADS,),
        in_specs=[col(0), col(HEADS), col(2 * HEADS), _resident((tp, HD)), _resident((tp, HD)),
                  pl.BlockSpec((None, 4, CHUNK, HD), lambda h: (h, 0, 0, 0))],
        out_specs=[col(0), pl.BlockSpec((None, nch, HD, HD), lambda h: (h, 0, 0, 0))],
        out_shape=[jax.ShapeDtypeStruct((tp, RW), f32), jax.ShapeDtypeStruct((HEADS, nch, HD, HD), bf16)],
        scratch_shapes=[pltpu.VMEM((HD, HD), f32)],
        compiler_params=_cp(48, ("arbitrary",)))(zq, zq, zq, cos2, sin2, dec)


def retention_bwd(name, zq, cos2, sin2, dec, states, dout, pad, deps=()):
    tp = zq.shape[0]
    nch = tp // CHUNK
    scale = HD ** -0.5

    def body(q_ref, k_ref, v_ref, cos_ref, sin_ref, dec_ref, st_ref, do_ref, dq_ref, dk_ref, dv_ref, g_ref):
        g_ref[...] = jnp.zeros_like(g_ref)
        intra, kdec, qdec, cdec = dec_ref[0], dec_ref[1], dec_ref[2], dec_ref[3]

        def chunk(t, carry):
            n = nch - 1 - t
            rows = pl.ds(pl.multiple_of(n * CHUNK, CHUNK), CHUNK)
            cosv, sinv = cos_ref[rows, :], sin_ref[rows, :]
            qr = _rot(q_ref[rows, :], cosv, sinv) * scale
            kr = _rot(k_ref[rows, :], cosv, sinv)
            qb, kb = qr.astype(bf16), kr.astype(bf16)
            vb = v_ref[rows, :].astype(bf16)
            qd = (qr * qdec).astype(bf16)
            kd = (kr * kdec).astype(bf16)
            sc = (_dot_nt(qb, kb) * intra).astype(bf16)
            dob = do_ref[rows, :]
            sb = st_ref[n]
            gv = g_ref[...]
            gb = gv.astype(bf16)
            dsc = (_dot_nt(dob, vb) * intra).astype(bf16)
            dv = _dot_tn(sc, dob) + _dot_nn(kd, gb)
            dqr = _dot_nn(dsc, kb) + _dot_nt(dob, sb) * qdec
            dkr = _dot_tn(dsc, qb) + _dot_nt(vb, gb) * kdec
            g_ref[...] = gv * cdec + _dot_tn(qd, dob)
            keep = (lax.broadcasted_iota(jnp.int32, (CHUNK, HD), 0) + n * CHUNK) >= pad
            dq_ref[rows, :] = jnp.where(keep, _rot_t(dqr * scale, cosv, sinv), 0.0).astype(bf16)
            dk_ref[rows, :] = jnp.where(keep, _rot_t(dkr, cosv, sinv), 0.0).astype(bf16)
            dv_ref[rows, :] = jnp.where(keep, dv, 0.0).astype(bf16)
            return carry

        lax.fori_loop(0, nch, chunk, 0)

    col = lambda off: pl.BlockSpec((tp, HD), lambda h: (0, off + h))
    return pl.pallas_call(
        _skip(len(deps), body), name=name, grid=(HEADS,),
        in_specs=[_ANY] * len(deps) + [col(0), col(HEADS), col(2 * HEADS), _resident((tp, HD)), _resident((tp, HD)),
                  pl.BlockSpec((None, 4, CHUNK, HD), lambda h: (h, 0, 0, 0)),
                  pl.BlockSpec((None, nch, HD, HD), lambda h: (h, 0, 0, 0)), col(0)],
        out_specs=[col(0)] * 3,
        out_shape=[jax.ShapeDtypeStruct((tp, RW), bf16)] * 3,
        scratch_shapes=[pltpu.VMEM((HD, HD), f32)],
        compiler_params=_cp(48, ("arbitrary",)))(*deps, zq, zq, zq, cos2, sin2, dec, states, dout)


def _window_sum(xv, steps, tp, forward):
    s = xv
    for j in range(steps):
        sh = 2 ** j
        s = s + pltpu.roll(s, (tp - sh) if forward else sh, 0)
    return s


def pool_fwd(name, zu, maps, scale, pad):
    tp = zu.shape[0]

    def body(u_ref, maps_ref, scale_ref, pooled_ref, p_ref):
        row = lax.broadcasted_iota(jnp.int32, (tp, HD), 0)
        for gi, w in enumerate(POOL_WINDOWS):
            cols = slice(gi * HD, (gi + 1) * HD)
            xv = u_ref[:, cols]
            cnt = jnp.clip(row - (pad - 1), 1, w).astype(f32)
            pooled = jnp.where(row >= pad, _window_sum(xv, gi + 1, tp, False) / cnt - xv, 0.0).astype(bf16)
            pooled_ref[:, cols] = pooled
            p_ref[:, cols] = (_dot_nn(pooled, maps_ref[gi].astype(bf16)) * scale_ref[:, cols]).astype(bf16)

    return pl.pallas_call(
        body, name=name,
        out_shape=[jax.ShapeDtypeStruct((tp, RW), bf16), jax.ShapeDtypeStruct((tp, RW), bf16)],
        compiler_params=_cp(56))(zu, maps, scale)


def pool_bwd(name, dp, pooled, maps, scale, pad):
    tp = dp.shape[0]

    def body(dp_ref, pooled_ref, maps_ref, scale_ref, du_ref, dmaps_ref, dscale_ref):
        row = lax.broadcasted_iota(jnp.int32, (tp, HD), 0)
        dscale_ref[...] = jnp.zeros_like(dscale_ref)
        for gi, w in enumerate(POOL_WINDOWS):
            cols = slice(gi * HD, (gi + 1) * HD)
            mb = maps_ref[gi].astype(bf16)
            pooled = pooled_ref[:, cols]
            dpf = dp_ref[:, cols].astype(f32)
            dscale_ref[0:1, cols] = jnp.sum(dpf * _dot_nn(pooled, mb), axis=0, keepdims=True)
            dpm = (dpf * scale_ref[:, cols]).astype(bf16)
            dmaps_ref[gi * HD:(gi + 1) * HD, :] = _dot_tn(pooled, dpm)
            dpool = jnp.where(row >= pad, _dot_nt(dpm, mb), 0.0)
            cnt = jnp.clip(row - (pad - 1), 1, w).astype(f32)
            du = _window_sum(dpool / cnt, gi + 1, tp, True) - dpool
            du_ref[:, cols] = jnp.where(row >= pad, du, 0.0).astype(bf16)

    return pl.pallas_call(
        body, name=name,
        out_shape=[jax.ShapeDtypeStruct((tp, RW), bf16), jax.ShapeDtypeStruct((HEADS * HD, HD), f32),
                   jax.ShapeDtypeStruct((8, RW), f32)],
        compiler_params=_cp(56))(dp, pooled, maps, scale)


def _group_norm(o):
    mu = jnp.mean(o, axis=-1, keepdims=True)
    oc = o - mu
    rstd = lax.rsqrt(jnp.mean(oc * oc, axis=-1, keepdims=True) + EPS)
    return oc * rstd, rstd


def mix_out_fwd(name, h, oraw, zg, zgate, p, wretT, wpoolT, wout):
    tp, d = h.shape
    tm = _row_tile(tp)

    def body(h_ref, o_ref, zg_ref, zgate_ref, p_ref, wr_ref, wp_ref, wo_ref, ho_ref, rp_ref, ret_ref, pool_ref, mixed_ref):
        parts = []
        for hh in range(HEADS):
            cols = slice(hh * HD, (hh + 1) * HD)
            rhat, _ = _group_norm(o_ref[:, cols])
            gv = zg_ref[:, cols]
            parts.append(rhat * (gv * _sigmoid(gv)))
        r = jnp.concatenate(parts, axis=-1).astype(bf16)
        pv = p_ref[...]
        ret = _dot_nt(r, wr_ref[...])
        pool = _dot_nt(pv, wp_ref[...])
        mixed = (_sigmoid(zgate_ref[:, :d]) * ret + _sigmoid(zgate_ref[:, d:]) * pool).astype(bf16)
        ho_ref[...] = h_ref[...] + _dot_nn(mixed, wo_ref[...])
        rp_ref[0] = r
        rp_ref[1] = pv
        ret_ref[...] = ret.astype(bf16)
        pool_ref[...] = pool.astype(bf16)
        mixed_ref[...] = mixed

    row = lambda w: pl.BlockSpec((tm, w), lambda i: (i, 0))
    return pl.pallas_call(
        body, name=name, grid=(tp // tm,),
        in_specs=[row(d), row(RW), row(RW), row(2 * d), row(RW), _resident((d, RW)), _resident((d, RW)), _resident((d, d))],
        out_specs=[row(d), pl.BlockSpec((2, tm, RW), lambda i: (0, i, 0)), row(d), row(d),
                   pl.BlockSpec((None, tm, d), lambda i: (0, i, 0))],
        out_shape=[jax.ShapeDtypeStruct((tp, d), f32), jax.ShapeDtypeStruct((2, tp, RW), bf16),
                   jax.ShapeDtypeStruct((tp, d), bf16), jax.ShapeDtypeStruct((tp, d), bf16),
                   jax.ShapeDtypeStruct((1, tp, d), bf16)],
        compiler_params=_cp(48, ("arbitrary",)))(h, oraw, zg, zgate, p, wretT, wpoolT, wout)


def mix_out_bwd(name, dy, oraw, zg, zgate, ret, pool, wretT, wpoolT, wout, deps=()):
    tp, d = dy.shape
    tm = _row_tile(tp)

    def body(dy_ref, o_ref, zg_ref, zgate_ref, ret_ref, pool_ref, wr_ref, wp_ref, wo_ref,
             do_ref, dzg_ref, dzgate_ref, dp_ref, drp_ref, dyb_ref):
        dyb = dy_ref[...].astype(bf16)
        dmixed = _dot_nt(dyb, wo_ref[...])
        sa = _sigmoid(zgate_ref[:, :d])
        sb = _sigmoid(zgate_ref[:, d:])
        dret = dmixed * sa
        dpool = dmixed * sb
        dzgate_ref[:, :d] = (dret * ret_ref[...].astype(f32) * (1.0 - sa)).astype(bf16)
        dzgate_ref[:, d:] = (dpool * pool_ref[...].astype(f32) * (1.0 - sb)).astype(bf16)
        dretb, dpoolb = dret.astype(bf16), dpool.astype(bf16)
        dr = _dot_nn(dretb, wr_ref[...])
        dp_ref[...] = _dot_nn(dpoolb, wp_ref[...]).astype(bf16)
        for hh in range(HEADS):
            cols = slice(hh * HD, (hh + 1) * HD)
            rhat, rstd = _group_norm(o_ref[:, cols])
            gv = zg_ref[:, cols]
            sg = _sigmoid(gv)
            drh = dr[:, cols]
            drhat = drh * (gv * sg)
            dzg_ref[:, cols] = (drh * rhat * (sg * (1.0 + gv * (1.0 - sg)))).astype(bf16)
            do = rstd * (drhat - jnp.mean(drhat, axis=-1, keepdims=True)
                         - rhat * jnp.mean(drhat * rhat, axis=-1, keepdims=True))
            do_ref[:, cols] = do.astype(bf16)
        drp_ref[0] = dretb
        drp_ref[1] = dpoolb
        dyb_ref[...] = dyb

    row = lambda w: pl.BlockSpec((tm, w), lambda i: (i, 0))
    return pl.pallas_call(
        _skip(len(deps), body), name=name, grid=(tp // tm,),
        in_specs=[_ANY] * len(deps) + [row(d), row(RW), row(RW), row(2 * d), row(d), row(d), _resident((d, RW)), _resident((d, RW)),
                  _resident((d, d))],
        out_specs=[row(RW), row(RW), row(2 * d), row(RW), pl.BlockSpec((2, tm, d), lambda i: (0, i, 0)),
                   pl.BlockSpec((None, tm, d), lambda i: (0, i, 0))],
        out_shape=[jax.ShapeDtypeStruct((tp, RW), bf16), jax.ShapeDtypeStruct((tp, RW), bf16),
                   jax.ShapeDtypeStruct((tp, 2 * d), bf16), jax.ShapeDtypeStruct((tp, RW), bf16),
                   jax.ShapeDtypeStruct((2, tp, d), bf16), jax.ShapeDtypeStruct((1, tp, d), bf16)],
        compiler_params=_cp(48, ("arbitrary",)))(*deps, dy, oraw, zg, zgate, ret, pool, wretT, wpoolT, wout)


def final_loss(name, h, g, target):
    tp, d = h.shape
    tm = CHUNK

    def body(h_ref, g_ref, t_ref, dh_ref, loss_ref, dg_ref):
        i = pl.program_id(0)
        hhat, rs = _rms(h_ref[...])
        gv = g_ref[...]
        err = jnp.where(i >= 1, hhat * gv - t_ref[...], 0.0)
        dyv = err / d
        dh_ref[...] = _rms_bwd(dyv, gv, hhat, rs)

        @pl.when(i == 0)
        def _():
            loss_ref[...] = jnp.zeros_like(loss_ref)
            dg_ref[...] = jnp.zeros_like(dg_ref)

        loss_ref[...] += 0.5 * jnp.sum(jnp.sum(err * err, axis=-1, keepdims=True) / d)
        dg_ref[0:1, :] += jnp.sum(dyv * hhat, axis=0, keepdims=True)

    return pl.pallas_call(
        body, name=name, grid=(tp // tm,),
        in_specs=[pl.BlockSpec((tm, d), lambda i: (i, 0)), _resident((1, d)),
                  pl.BlockSpec((tm, d), lambda i: (jnp.maximum(i - 1, 0), 0))],
        out_specs=[pl.BlockSpec((tm, d), lambda i: (i, 0)), pl.BlockSpec((8, 128), lambda i: (0, 0)),
                   pl.BlockSpec((8, d), lambda i: (0, 0))],
        out_shape=[jax.ShapeDtypeStruct((tp, d), f32), jax.ShapeDtypeStruct((8, 128), f32),
                   jax.ShapeDtypeStruct((8, d), f32)],
        compiler_params=_cp(32, ("arbitrary",)))(h, g, target)


def _adamw(w, g, m, v):
    m = ADAM_B1 * m + (1.0 - ADAM_B1) * g
    v = ADAM_B2 * v + (1.0 - ADAM_B2) * (g * g)
    m_hat = m / (1.0 - ADAM_B1 ** ADAM_STEP)
    v_hat = v / (1.0 - ADAM_B2 ** ADAM_STEP)
    delta = -ADAM_LR * (m_hat / (jnp.sqrt(v_hat) + ADAM_EPS) + ADAM_WD * w)
    return delta, m, v


def adam_big(name, me, recv, own, b, layer, transposed, w, m, v, prev):
    r, c = recv.shape[2:]
    wshape = w.shape[1:]

    def body(me_ref, recv_ref, own_ref, w_ref, m_ref, v_ref, *rest):
        g_ref, d_ref, nm_ref, nv_ref = rest[-4:]
        g = own_ref[...].astype(f32)
        for j in range(NDEV - 1):
            g = g + recv_ref[j].astype(f32)
        if transposed:
            g = g.T
        delta, nm, nv = _adamw(w_ref[...], g, m_ref[...], v_ref[...])
        g_ref[...] = g
        d_ref[...] = delta
        nm_ref[...] = nm
        nv_ref[...] = nv

    wspec = pl.BlockSpec((None,) + wshape, lambda i, me_ref: (layer, 0, 0))
    in_specs = [pl.BlockSpec((NDEV - 1, None, r, c), lambda i, me_ref: (0, b, 0, 0)),
                pl.BlockSpec((None, None, r, c), lambda i, me_ref: (b, me_ref[0], 0, 0)), wspec, wspec, wspec]
    args = [recv, own, w, m, v]
    aliases = {}
    if prev is not None:
        in_specs += [_ANY] * 4
        args += list(prev)
        aliases = {6 + k: k for k in range(4)}
    return pl.pallas_call(
        body, name=name,
        grid_spec=pltpu.PrefetchScalarGridSpec(num_scalar_prefetch=1, grid=(1,), in_specs=in_specs, out_specs=[wspec] * 4),
        out_shape=[jax.ShapeDtypeStruct(w.shape, f32)] * 4, input_output_aliases=aliases,
        compiler_params=_cp(56))(me, *args)


def adam_small(name, ga, gmaps, gmeta, norms, pool_scale, pool_maps, meta, final_norm, d):
    nd = d // NDEV

    def body(ga_ref, gmaps_ref, gmeta_ref, *refs):
        ins, outs = refs[:21], refs[21:]
        x, y, c = _me()
        me = 4 * x + 2 * y + c

        def total(ref, rows):
            t = ref[0, rows, :]
            for j in range(1, NDEV):
                t = t + ref[j, rows, :]
            return t

        outs[0][...] = total(ga_ref, slice(0, 8))[:, :128]

        def update(k, g, o):
            w_ref, m_ref, v_ref = ins[3 * k:3 * k + 3]
            delta, nm, nv = _adamw(w_ref[...], g, m_ref[...], v_ref[...])
            for ref, val in zip(outs[o:o + 4], (g, delta, nm, nv)):
                ref[...] = val

        two = lax.broadcasted_iota(jnp.int32, (2, d), 0)
        for k in range(3):
            g0 = total(ga_ref, slice(16 + 8 * k, 17 + 8 * k))
            g1 = total(ga_ref, slice(40 + 8 * k, 41 + 8 * k))
            update(k, jnp.where(two == 0, g0, g1), 1 + 4 * k)
        s0 = total(ga_ref, slice(64, 65))[:, :RW]
        s1 = total(ga_ref, slice(72, 73))[:, :RW]
        update(3, jnp.where(two[:, :RW] == 0, s0, s1), 13)
        update(4, total(gmaps_ref, slice(None)), 17)
        update(5, total(gmeta_ref, pl.ds(pl.multiple_of(me * N_META, N_META), N_META)), 21)
        update(6, total(ga_ref, slice(8, 9)), 25)

    flat = []
    for trip in (*norms, pool_scale, pool_maps, meta, final_norm):
        flat += list(trip)
    out_shapes = [jax.ShapeDtypeStruct((8, 128), f32)]
    for trip in (*norms, pool_scale, pool_maps, meta, final_norm):
        out_shapes += [jax.ShapeDtypeStruct(trip[0].shape, f32)] * 4
    return pl.pallas_call(body, name=name, out_shape=out_shapes, compiler_params=_cp(32))(ga, gmaps, gmeta, *flat)


def kernel(x, meta, ffn1_norm, ffn1_gate, ffn1_up, ffn1_down, mix_norm, w_in, pool_maps, pool_scale, w_ret_up, w_pool_up, w_out, ffn2_norm, ffn2_gate, ffn2_up, ffn2_down, final_norm, loss_target, m_meta, m_ffn1_norm, m_ffn1_gate, m_ffn1_up, m_ffn1_down, m_mix_norm, m_w_in, m_pool_maps, m_pool_scale, m_w_ret_up, m_w_pool_up, m_w_out, m_ffn2_norm, m_ffn2_gate, m_ffn2_up, m_ffn2_down, m_final_norm, v_meta, v_ffn1_norm, v_ffn1_gate, v_ffn1_up, v_ffn1_down, v_mix_norm, v_w_in, v_pool_maps, v_pool_scale, v_w_ret_up, v_w_pool_up, v_w_out, v_ffn2_norm, v_ffn2_gate, v_ffn2_up, v_ffn2_down, v_final_norm):
    seq, d = x.shape[1], x.shape[2]
    depth = ffn1_gate.shape[0]
    ff = ffn1_gate.shape[2] * NDEV
    nin = w_in.shape[2] * NDEV
    length = seq + N_META
    pad = (-length) % CHUNK
    tp = length + pad
    assert pad % 8 == 0 and pad + N_META == CHUNK and depth == 2 and nin == 5 * RW + 2 * d

    ix, iy, ic = _me()
    me = (4 * ix + 2 * iy + ic).astype(jnp.int32).reshape(1)

    gathers = {}
    token = None
    for layer in range(depth):
        lands = prep_layer(layer, me, [ffn1_gate, ffn1_up, ffn2_gate, ffn2_up, w_in, w_ret_up, w_pool_up],
                           [ffn1_down, ffn2_down, w_out])
        g1T, u1T, g2T, u2T, winT, wretT, wpoolT, d1, d2, wout = lands
        for part, group in (("ffn1", [g1T, u1T, d1]), ("mix", [winT, wretT, wpoolT, wout]), ("ffn2", [g2T, u2T, d2])):
            ssem, rsem, group, token = gather_start(f"gather_start_{part}{layer}", group, () if token is None else (token,))
            gathers[(part, layer)] = (ssem, rsem, group)

    def gathered(part, layer, after):
        ssem, rsem, group = gathers[(part, layer)]
        _, full = copies_wait(f"gather_wait_{part}{layer}", ssem, rsem, (), group, after)
        return [a.reshape((NDEV * a.shape[1],) + a.shape[2:]) for a in full]

    meta_all, = all_gather("gather_meta", [meta])
    meta_full = jnp.transpose(meta_all, (1, 0, 2)).reshape(N_META, d)

    cos2, sin2, dec = _retention_tables(tp, pad)
    h = jnp.concatenate([jnp.zeros((pad, d), f32), meta_full, x[0]], axis=0)

    saved = []
    weights = []
    for layer in range(depth):
        row = lambda a: a[layer:layer + 1]
        s = {"h0": h}
        g1T, u1T, d1 = gathered("ffn1", layer, token if layer == 0 else h)
        h, s["a1"], s["b1"] = ffn_fwd(f"ffn1_fwd{layer}", h, row(ffn1_norm), g1T, u1T, d1)
        s["h1"] = h
        winT, wretT, wpoolT, wout = gathered("mix", layer, h)
        s["zq"], s["zg"], zu, s["zgate"] = mix_in_fwd(f"mix_in_fwd{layer}", h, row(mix_norm), winT)
        s["oraw"], s["states"] = retention_fwd(f"retention_fwd{layer}", s["zq"], cos2, sin2, dec)
        s["pooled"], p = pool_fwd(f"pool_fwd{layer}", zu, pool_maps[layer], row(pool_scale), pad)
        h, s["rp"], s["ret"], s["pool"], s["mixed"] = mix_out_fwd(
            f"mix_out_fwd{layer}", h, s["oraw"], s["zg"], s["zgate"], p, wretT, wpoolT, wout)
        s["h2"] = h
        g2T, u2T, d2 = gathered("ffn2", layer, h)
        h, s["a2"], s["b2"] = ffn_fwd(f"ffn2_fwd{layer}", h, row(ffn2_norm), g2T, u2T, d2)
        saved.append(s)
        weights.append((g1T, u1T, g2T, u2T, winT, wretT, wpoolT, d1, d2, wout))

    dh, loss_part, dg_final = final_loss("final_loss", h, final_norm.reshape(1, d), loss_target[0])

    small = {}
    exchanges = {}
    token = None

    def exchange(part, layer, grads):
        by_dest = [g.reshape(g.shape[0], NDEV, g.shape[1] // NDEV, g.shape[2]) for g in grads]
        ssem, rsem, sent, lands, tok = exchange_start(f"exchange_start_{part}{layer}", by_dest)
        exchanges[(part, layer)] = (ssem, rsem, sent, lands)
        return (tok,)

    for layer in reversed(range(depth)):
        g1T, u1T, g2T, u2T, winT, wretT, wpoolT, d1, d2, wout = weights[layer]
        row = lambda a: a[layer:layer + 1]
        s = saved[layer]
        dh, lhs2, rhs2, small[("ffn2", layer)] = ffn_bwd(
            f"ffn2_bwd{layer}", dh, s["h2"], row(ffn2_norm), s["a2"], s["b2"], g2T, u2T, d2, () if token is None else token)
        token = exchange("ffn2", layer, [mm_tn(f"ffn2_wgrad{layer}", lhs2, rhs2, lambda b: b // 2)])
        do, dzg, dzgate, dp, drp, dyb = mix_out_bwd(
            f"mix_out_bwd{layer}", dh, s["oraw"], s["zg"], s["zgate"], s["ret"], s["pool"], wretT, wpoolT, wout, token)
        token = exchange("mix", layer, [mm_tn(f"w_out_wgrad{layer}", s["mixed"], dyb, lambda b: b),
                                        mm_tn(f"up_wgrad{layer}", drp, s["rp"], lambda b: b)])
        dq, dk, dv = retention_bwd(f"retention_bwd{layer}", s["zq"], cos2, sin2, dec, s["states"], do, pad, token)
        dzu, small[("maps", layer)], small[("scale", layer)] = pool_bwd(
            f"pool_bwd{layer}", dp, s["pooled"], pool_maps[layer], row(pool_scale), pad)
        dh, dz, n2, small[("mix", layer)] = mix_in_bwd(
            f"mix_in_bwd{layer}", dq, dk, dv, dzg, dzu, dzgate, s["h1"], row(mix_norm), winT, dh)
        token = exchange("w_in", layer, [mm_tn(f"w_in_wgrad{layer}", dz, n2, lambda b: b)])
        dh, lhs1, rhs1, small[("ffn1", layer)] = ffn_bwd(
            f"ffn1_bwd{layer}", dh, s["h0"], row(ffn1_norm), s["a1"], s["b1"], g1T, u1T, d1, token)
        token = exchange("ffn1", layer, [mm_tn(f"ffn1_wgrad{layer}", lhs1, rhs1, lambda b: b // 2)])

    grad_x = dh[CHUNK:][None]

    big = {}
    after = dh
    plans = {
        "ffn2": [("ffn2_gate", 0, 0, True, ffn2_gate, m_ffn2_gate, v_ffn2_gate),
                 ("ffn2_up", 0, 1, True, ffn2_up, m_ffn2_up, v_ffn2_up),
                 ("ffn2_down", 0, 2, False, ffn2_down, m_ffn2_down, v_ffn2_down)],
        "mix": [("w_out", 0, 0, False, w_out, m_w_out, v_w_out),
                ("w_ret_up", 1, 0, True, w_ret_up, m_w_ret_up, v_w_ret_up),
                ("w_pool_up", 1, 1, True, w_pool_up, m_w_pool_up, v_w_pool_up)],
        "w_in": [("w_in", 0, 0, True, w_in, m_w_in, v_w_in)],
        "ffn1": [("ffn1_gate", 0, 0, True, ffn1_gate, m_ffn1_gate, v_ffn1_gate),
                 ("ffn1_up", 0, 1, True, ffn1_up, m_ffn1_up, v_ffn1_up),
                 ("ffn1_down", 0, 2, False, ffn1_down, m_ffn1_down, v_ffn1_down)]}
    for layer in reversed(range(depth)):
        for part in ("ffn2", "mix", "w_in", "ffn1"):
            ssem, rsem, sent, lands = exchanges[(part, layer)]
            sent, lands = copies_wait(f"exchange_wait_{part}{layer}", ssem, rsem, sent, lands, after)
            for nm, k, b, tr, w, m, v in plans[part]:
                big[nm] = adam_big(f"adam_{nm}{layer}", me, lands[k], sent[k], b, layer, tr, w, m, v, big.get(nm))
                after = big[nm][0]

    def row8(a):
        return a if a.shape[1] == d else jnp.pad(a, ((0, 0), (0, d - a.shape[1])))

    loss_rows = jnp.pad(loss_part, ((0, 0), (0, d - 128))) if d > 128 else loss_part
    parts = [loss_rows, dg_final]
    for layer in range(depth):
        parts += [small[("ffn1", layer)], small[("mix", layer)], small[("ffn2", layer)]]
    parts += [row8(small[("scale", layer)]) for layer in range(depth)]
    pack_a = jnp.concatenate(parts, axis=0)
    pack_maps = jnp.concatenate([small[("maps", layer)] for layer in range(depth)], axis=0)
    dmeta = dh[pad:CHUNK]
    pack_meta = jnp.transpose(dmeta.reshape(N_META, NDEV, d // NDEV), (1, 0, 2)).reshape(NDEV * N_META, d // NDEV)
    ga, gmaps, gmeta = all_gather("gather_small_grads", [pack_a, pack_maps, pack_meta])

    maps2 = lambda a: a.reshape(depth * HEADS * HD, HD)
    res = adam_small(
        "adam_small", ga, gmaps, gmeta,
        [(ffn1_norm, m_ffn1_norm, v_ffn1_norm), (mix_norm, m_mix_norm, v_mix_norm), (ffn2_norm, m_ffn2_norm, v_ffn2_norm)],
        (pool_scale, m_pool_scale, v_pool_scale), (maps2(pool_maps), maps2(m_pool_maps), maps2(v_pool_maps)),
        (meta, m_meta, v_meta), tuple(a.reshape(1, d) for a in (final_norm, m_final_norm, v_final_norm)), d)
    loss = res[0][0, 0]
    sm = {}
    for k, nm in enumerate(["ffn1_norm", "mix_norm", "ffn2_norm", "pool_scale", "pool_maps", "meta", "final_norm"]):
        sm[nm] = list(res[1 + 4 * k:5 + 4 * k])
    sm["pool_maps"] = [a.reshape(pool_maps.shape) for a in sm["pool_maps"]]
    sm["final_norm"] = [a.reshape(d) for a in sm["final_norm"]]

    names = ["meta", "ffn1_norm", "ffn1_gate", "ffn1_up", "ffn1_down", "mix_norm", "w_in", "pool_maps", "pool_scale",
             "w_ret_up", "w_pool_up", "w_out", "ffn2_norm", "ffn2_gate", "ffn2_up", "ffn2_down", "final_norm"]
    allw = {**{k: list(v) for k, v in big.items()}, **sm}
    outs = [loss, grad_x]
    for kind in range(4):
        outs += [allw[nm][kind] for nm in names]
    return tuple(outs)
```

```python
import functools

import jax
import jax.numpy as jnp
from jax import lax
from jax.experimental import pallas as pl
from jax.experimental.pallas import tpu as pltpu

f32 = jnp.float32
bf16 = jnp.bfloat16
MESH = pl.DeviceIdType.MESH
NDEV = 8
N_META = 16
HEADS = 4
HD = 128
CHUNK = 128
RW = HEADS * HD
POOL_WINDOWS = (2, 4, 8, 16)
ROPE_BASE = 10000.0
EPS = 1e-6
ADAM_LR = 0.001
ADAM_B1 = 0.9
ADAM_B2 = 0.999
ADAM_EPS = 1e-08
ADAM_WD = 0.01
ADAM_STEP = 10
VMEM_CAP_MB = 60


def _cp(vmem_mb, sem=None):
    return pltpu.CompilerParams(vmem_limit_bytes=min(vmem_mb, VMEM_CAP_MB) * 2**20, dimension_semantics=sem)


def _row_tile(tp, want=384):
    return want if tp % want == 0 else 128


def _resident(shape):
    nd = len(shape)
    return pl.BlockSpec(shape, lambda *_: (0,) * nd, pipeline_mode=pl.Buffered(1))


def _skip(nd, body):
    return (lambda *refs: body(*refs[nd:])) if nd else body


def _dot_nn(a, b):
    return lax.dot_general(a, b, (((1,), (0,)), ((), ())), preferred_element_type=f32)


def _dot_nt(a, b):
    return lax.dot_general(a, b, (((1,), (1,)), ((), ())), preferred_element_type=f32)


def _dot_tn(a, b):
    return lax.dot_general(a, b, (((0,), (0,)), ((), ())), preferred_element_type=f32)


def _rms(h):
    rs = lax.rsqrt(jnp.mean(h * h, axis=-1, keepdims=True) + EPS)
    return h * rs, rs


def _rms_bwd(dn, g, hhat, rs):
    dhh = dn * g
    return rs * (dhh - hhat * jnp.mean(dhh * hhat, axis=-1, keepdims=True))


def _sigmoid(x):
    return jax.nn.sigmoid(x)


def _me():
    return lax.axis_index("x"), lax.axis_index("y"), lax.axis_index("c")


def _peer(idx):
    return (idx // 4, (idx // 2) % 2, idx % 2)


def all_gather(name, arrays):
    n = len(arrays)

    def body(*refs):
        ins, outs = refs[:n], refs[n:2 * n]
        send_sems, recv_sems, local_sems = refs[2 * n:]
        x, y, c = _me()
        me = 4 * x + 2 * y + c
        locals_ = []
        for k in range(n):
            cp = pltpu.make_async_copy(ins[k], outs[k].at[me], local_sems.at[k])
            cp.start()
            locals_.append(cp)
        for d in range(1, NDEV):
            for k in range(n):
                pltpu.make_async_remote_copy(
                    src_ref=ins[k], dst_ref=outs[k].at[me], send_sem=send_sems.at[k], recv_sem=recv_sems.at[k],
                    device_id=_peer((me + d) % NDEV), device_id_type=MESH).start()
        for k in range(n):
            seven = outs[k].at[pl.ds(0, NDEV - 1)]
            w = pltpu.make_async_remote_copy(src_ref=seven, dst_ref=seven, send_sem=send_sems.at[k],
                                             recv_sem=recv_sems.at[k], device_id=(x, y, c), device_id_type=MESH)
            w.wait_send()
            w.wait_recv()
            locals_[k].wait()

    anyspec = pl.BlockSpec(memory_space=pl.ANY)
    return pl.pallas_call(
        body, name=name,
        out_shape=[jax.ShapeDtypeStruct((NDEV,) + a.shape, a.dtype) for a in arrays],
        in_specs=[anyspec] * n, out_specs=[anyspec] * n,
        scratch_shapes=[pltpu.SemaphoreType.DMA((n,)), pltpu.SemaphoreType.DMA((n,)), pltpu.SemaphoreType.DMA((n,))],
    )(*arrays)


_HBM = pl.BlockSpec(memory_space=pltpu.HBM)
_SEM = pl.BlockSpec(memory_space=pltpu.SEMAPHORE)
_ANY = pl.BlockSpec(memory_space=pl.ANY)
_EFFECT = pltpu.SideEffectType.DATAFLOW_SIDE_EFFECTING


def _in_hbm(a):
    return pltpu.with_memory_space_constraint(a, pltpu.HBM)


def gather_start(name, lands, deps=()):
    n, nd = len(lands), len(deps)

    def body(*refs):
        land = refs[nd:nd + n]
        send_sems, recv_sems = refs[nd + n:nd + n + 2]
        token = refs[-1]
        x, y, c = _me()
        me = 4 * x + 2 * y + c
        for d in range(1, NDEV):
            for k in range(n):
                pltpu.make_async_remote_copy(
                    src_ref=land[k].at[me], dst_ref=land[k].at[me], send_sem=send_sems.at[k], recv_sem=recv_sems.at[k],
                    device_id=_peer((me + d) % NDEV), device_id_type=MESH).start()
        token[...] = jnp.zeros_like(token)

    res = pl.pallas_call(
        body, name=name,
        out_shape=(pltpu.SemaphoreType.DMA((n,)), pltpu.SemaphoreType.DMA((n,)),
                   *[pltpu.HBM(a.shape, a.dtype) for a in lands], jax.ShapeDtypeStruct((8, 128), f32)),
        in_specs=[_ANY] * nd + [_HBM] * n,
        out_specs=(_SEM, _SEM, *[_HBM] * n, pl.BlockSpec(memory_space=pltpu.VMEM)),
        input_output_aliases={nd + k: 2 + k for k in range(n)},
        compiler_params=pltpu.CompilerParams(has_side_effects=_EFFECT),
    )(*deps, *[_in_hbm(a) for a in lands])
    return res[0], res[1], list(res[2:2 + n]), res[-1]


def exchange_start(name, grads, deps=()):
    n, nd = len(grads), len(deps)
    lands = [lax.empty((NDEV - 1, g.shape[0]) + g.shape[2:], g.dtype) for g in grads]

    def body(*refs):
        src = refs[nd:nd + n]
        land = refs[nd + n:nd + 2 * n]
        send_sems, recv_sems = refs[nd + 2 * n:nd + 2 * n + 2]
        token = refs[-1]
        x, y, c = _me()
        me = 4 * x + 2 * y + c
        for d in range(1, NDEV):
            p = (me + d) % NDEV
            for k in range(n):
                pltpu.make_async_remote_copy(
                    src_ref=src[k].at[:, p], dst_ref=land[k].at[d - 1], send_sem=send_sems.at[k], recv_sem=recv_sems.at[k],
                    device_id=_peer(p), device_id_type=MESH).start()
        token[...] = jnp.zeros_like(token)

    both = list(grads) + lands
    res = pl.pallas_call(
        body, name=name,
        out_shape=(pltpu.SemaphoreType.DMA((n,)), pltpu.SemaphoreType.DMA((n,)),
                   *[pltpu.HBM(a.shape, a.dtype) for a in both], jax.ShapeDtypeStruct((8, 128), f32)),
        in_specs=[_ANY] * nd + [_HBM] * (2 * n),
        out_specs=(_SEM, _SEM, *[_HBM] * (2 * n), pl.BlockSpec(memory_space=pltpu.VMEM)),
        input_output_aliases={nd + k: 2 + k for k in range(2 * n)},
        compiler_params=pltpu.CompilerParams(has_side_effects=_EFFECT),
    )(*deps, *[_in_hbm(a) for a in both])
    return res[0], res[1], list(res[2:2 + n]), list(res[2 + n:2 + 2 * n]), res[-1]


def copies_wait(name, send_sems, recv_sems, sent, lands, after):
    ns, n = len(sent), len(lands)

    def body(*refs):
        land = refs[ns:ns + n]
        ssem, rsem = refs[ns + n:ns + n + 2]
        x, y, c = _me()
        for k in range(n):
            seven = land[k].at[pl.ds(0, NDEV - 1)]
            w = pltpu.make_async_remote_copy(src_ref=seven, dst_ref=seven, send_sem=ssem.at[k], recv_sem=rsem.at[k],
                                             device_id=(x, y, c), device_id_type=MESH)
            w.wait_send()
            w.wait_recv()

    both = list(sent) + list(lands)
    res = pl.pallas_call(
        body, name=name, out_shape=tuple(pltpu.HBM(a.shape, a.dtype) for a in both),
        in_specs=[_HBM] * (ns + n) + [_SEM, _SEM, _ANY], out_specs=tuple([_HBM] * (ns + n)),
        input_output_aliases={k: k for k in range(ns + n)},
        compiler_params=pltpu.CompilerParams(has_side_effects=_EFFECT),
    )(*both, send_sems, recv_sems, after)
    return list(res[:ns]), list(res[ns:])


def prep_layer(layer, me, col_sharded, row_sharded):
    nc, nr = len(col_sharded), len(row_sharded)

    def body(me_ref, *refs):
        ins, outs = refs[:nc + nr], refs[nc + nr:]
        for k in range(nc):
            outs[k][...] = ins[k][...].T.astype(bf16)
        for k in range(nc, nc + nr):
            outs[k][...] = ins[k][...].astype(bf16)

    arrs = list(col_sharded) + list(row_sharded)
    in_specs = [pl.BlockSpec((None,) + a.shape[1:], lambda i, me_ref: (layer, 0, 0)) for a in arrs]
    shapes = [(a.shape[2], a.shape[1]) for a in col_sharded] + [a.shape[1:] for a in row_sharded]
    out_specs = [pl.BlockSpec((None,) + s, lambda i, me_ref: (me_ref[0], 0, 0)) for s in shapes]
    return pl.pallas_call(
        body, name=f"prep_layer{layer}",
        grid_spec=pltpu.PrefetchScalarGridSpec(num_scalar_prefetch=1, grid=(1,), in_specs=in_specs, out_specs=out_specs),
        out_shape=[jax.ShapeDtypeStruct((NDEV,) + s, bf16) for s in shapes], compiler_params=_cp(48))(me, *arrs)


def slot_in(name, me, arrays):
    n = len(arrays)

    def body(me_ref, *refs):
        for k in range(n):
            refs[n + k][...] = refs[k][...]

    in_specs = [pl.BlockSpec(a.shape, lambda i, me_ref: (0, 0)) for a in arrays]
    out_specs = [pl.BlockSpec((None,) + a.shape, lambda i, me_ref: (me_ref[0], 0, 0)) for a in arrays]
    return pl.pallas_call(
        body, name=name,
        grid_spec=pltpu.PrefetchScalarGridSpec(num_scalar_prefetch=1, grid=(1,), in_specs=in_specs, out_specs=out_specs),
        out_shape=[jax.ShapeDtypeStruct((NDEV,) + a.shape, a.dtype) for a in arrays])(me, *arrays)


def ffn_fwd(name, h, g, wgT, wuT, wd, deps=()):
    tp, d = h.shape
    ff = wgT.shape[0]
    tm = _row_tile(tp)

    def body(h_ref, g_ref, wg_ref, wu_ref, wd_ref, ho_ref, a_ref, b_ref):
        hh = h_ref[...]
        hhat, _ = _rms(hh)
        n = (hhat * g_ref[...]).astype(bf16)
        a = _dot_nt(n, wg_ref[...])
        b = _dot_nt(n, wu_ref[...])
        s = (a * _sigmoid(a)) * b
        ho_ref[...] = hh + 0.5 * _dot_nn(s.astype(bf16), wd_ref[...])
        a_ref[...] = a.astype(bf16)
        b_ref[...] = b.astype(bf16)

    row = lambda w: pl.BlockSpec((tm, w), lambda i: (i, 0))
    return pl.pallas_call(
        _skip(len(deps), body), name=name, grid=(tp // tm,),
        in_specs=[_ANY] * len(deps) + [row(d), _resident((1, d)), _resident((ff, d)), _resident((ff, d)), _resident((ff, d))],
        out_specs=[row(d), row(ff), row(ff)],
        out_shape=[jax.ShapeDtypeStruct((tp, d), f32), jax.ShapeDtypeStruct((tp, ff), bf16),
                   jax.ShapeDtypeStruct((tp, ff), bf16)],
        compiler_params=_cp(56, ("arbitrary",)))(*deps, h, g, wgT, wuT, wd)


def ffn_bwd(name, dy, h, g, a, b, wgT, wuT, wd, deps=()):
    tp, d = h.shape
    ff = wgT.shape[0]
    tm = _row_tile(tp, 192)

    def body(dy_ref, h_ref, g_ref, a_ref, b_ref, wg_ref, wu_ref, wd_ref, dh_ref, lhs_ref, rhs_ref, dg_ref):
        dyv = dy_ref[...]
        hhat, rs = _rms(h_ref[...])
        gv = g_ref[...]
        n = hhat * gv
        dyh = (0.5 * dyv).astype(bf16)
        ds = _dot_nt(dyh, wd_ref[...])
        av = a_ref[...].astype(f32)
        bv = b_ref[...].astype(f32)
        sg = _sigmoid(av)
        sa = av * sg
        da = (ds * bv * (sg * (1.0 + av * (1.0 - sg)))).astype(bf16)
        db = (ds * sa).astype(bf16)
        dn = _dot_nn(da, wg_ref[...]) + _dot_nn(db, wu_ref[...])
        dh_ref[...] = dyv + _rms_bwd(dn, gv, hhat, rs)

        @pl.when(pl.program_id(0) == 0)
        def _():
            dg_ref[...] = jnp.zeros_like(dg_ref)

        dg_ref[0:1, :] += jnp.sum(dn * hhat, axis=0, keepdims=True)
        lhs_ref[0] = da
        lhs_ref[1] = db
        lhs_ref[2] = (sa * bv).astype(bf16)
        rhs_ref[0] = n.astype(bf16)
        rhs_ref[1] = dyh

    row = lambda w: pl.BlockSpec((tm, w), lambda i: (i, 0))
    return pl.pallas_call(
        _skip(len(deps), body), name=name, grid=(tp // tm,),
        in_specs=[_ANY] * len(deps) + [row(d), row(d), _resident((1, d)), row(ff), row(ff),
                  _resident((ff, d)), _resident((ff, d)), _resident((ff, d))],
        out_specs=[row(d), pl.BlockSpec((3, tm, ff), lambda i: (0, i, 0)), pl.BlockSpec((2, tm, d), lambda i: (0, i, 0)),
                   pl.BlockSpec((8, d), lambda i: (0, 0))],
        out_shape=[jax.ShapeDtypeStruct((tp, d), f32), jax.ShapeDtypeStruct((3, tp, ff), bf16),
                   jax.ShapeDtypeStruct((2, tp, d), bf16), jax.ShapeDtypeStruct((8, d), f32)],
        compiler_params=_cp(58, ("arbitrary",)))(*deps, dy, h, g, a, b, wgT, wuT, wd)


def mm_tn(name, lhs, rhs, rhs_of, deps=()):
    nb, tp, m = lhs.shape
    n = rhs.shape[2]
    fixed = m * n * (4 + 2 * 2)
    tk = next((t for t in (1408, 704, 384) if tp % t == 0 and fixed + 2 * t * (m + n) * 2 <= 46 * 2**20), 128)
    nk = tp // tk

    def body(l_ref, r_ref, o_ref, acc_ref):
        k = pl.program_id(1)

        @pl.when(k == 0)
        def _():
            acc_ref[...] = jnp.zeros_like(acc_ref)

        acc_ref[...] += _dot_tn(l_ref[...], r_ref[...])

        @pl.when(k == nk - 1)
        def _():
            o_ref[...] = acc_ref[...].astype(o_ref.dtype)

    return pl.pallas_call(
        _skip(len(deps), body), name=name, grid=(nb, nk),
        in_specs=[_ANY] * len(deps) + [pl.BlockSpec((None, tk, m), lambda b, k: (b, k, 0)),
                                       pl.BlockSpec((None, tk, n), lambda b, k: (rhs_of(b), k, 0))],
        out_specs=pl.BlockSpec((None, m, n), lambda b, k: (b, 0, 0)),
        out_shape=jax.ShapeDtypeStruct((nb, m, n), bf16),
        scratch_shapes=[pltpu.VMEM((m, n), f32)],
        compiler_params=_cp(56, ("arbitrary", "arbitrary")))(*deps, lhs, rhs)


def mix_in_fwd(name, h, g, winT):
    tp, d = h.shape
    nin = winT.shape[0]
    tm = _row_tile(tp)

    def body(h_ref, g_ref, w_ref, zq_ref, zg_ref, zu_ref, zgate_ref):
        hhat, _ = _rms(h_ref[...])
        z = _dot_nt((hhat * g_ref[...]).astype(bf16), w_ref[...])
        zq_ref[...] = z[:, :3 * RW]
        zg_ref[...] = z[:, 3 * RW:4 * RW]
        zu_ref[...] = z[:, 4 * RW:5 * RW]
        zgate_ref[...] = z[:, 5 * RW:]

    row = lambda w: pl.BlockSpec((tm, w), lambda i: (i, 0))
    widths = (3 * RW, RW, RW, 2 * d)
    return pl.pallas_call(
        body, name=name, grid=(tp // tm,),
        in_specs=[row(d), _resident((1, d)), _resident((nin, d))],
        out_specs=[row(w) for w in widths],
        out_shape=[jax.ShapeDtypeStruct((tp, w), f32) for w in widths],
        compiler_params=_cp(56, ("arbitrary",)))(h, g, winT)


def mix_in_bwd(name, dq, dk, dv, dzg, dzu, dzgate, h, g, winT, dres):
    tp, d = h.shape
    nin = winT.shape[0]
    tm = _row_tile(tp)

    def body(dq_ref, dk_ref, dv_ref, dzg_ref, dzu_ref, dzgate_ref, h_ref, g_ref, w_ref, dres_ref, dh_ref, dz_ref, n_ref, dg_ref):
        dz = jnp.concatenate([dq_ref[...], dk_ref[...], dv_ref[...], dzg_ref[...], dzu_ref[...], dzgate_ref[...]], axis=-1)
        dn = _dot_nn(dz, w_ref[...])
        hhat, rs = _rms(h_ref[...])
        gv = g_ref[...]
        dh_ref[...] = dres_ref[...] + _rms_bwd(dn, gv, hhat, rs)

        @pl.when(pl.program_id(0) == 0)
        def _():
            dg_ref[...] = jnp.zeros_like(dg_ref)

        dg_ref[0:1, :] += jnp.sum(dn * hhat, axis=0, keepdims=True)
        dz_ref[...] = dz
        n_ref[...] = (hhat * gv).astype(bf16)

    row = lambda w: pl.BlockSpec((tm, w), lambda i: (i, 0))
    return pl.pallas_call(
        body, name=name, grid=(tp // tm,),
        in_specs=[row(RW)] * 5 + [row(2 * d), row(d), _resident((1, d)), _resident((nin, d)), row(d)],
        out_specs=[row(d), pl.BlockSpec((None, tm, nin), lambda i: (0, i, 0)), pl.BlockSpec((None, tm, d), lambda i: (0, i, 0)),
                   pl.BlockSpec((8, d), lambda i: (0, 0))],
        out_shape=[jax.ShapeDtypeStruct((tp, d), f32), jax.ShapeDtypeStruct((1, tp, nin), bf16),
                   jax.ShapeDtypeStruct((1, tp, d), bf16), jax.ShapeDtypeStruct((8, d), f32)],
        compiler_params=_cp(56, ("arbitrary",)))(dq, dk, dv, dzg, dzu, dzgate, h, g, winT, dres)


def _retention_tables(tp, pad):
    half = HD // 2
    inv_freq = ROPE_BASE ** (-jnp.arange(half, dtype=f32) / half)
    pos = jnp.arange(tp, dtype=f32) - pad
    ang = pos[:, None] * inv_freq[None, :]
    cos, sin = jnp.cos(ang), jnp.sin(ang)
    cos2 = jnp.concatenate([cos, cos], axis=-1)
    sin2 = jnp.concatenate([-sin, sin], axis=-1)
    log_gamma = jnp.log1p(-(2.0 ** (-5.0 - jnp.arange(HEADS, dtype=f32))))
    idx = jnp.arange(CHUNK, dtype=f32)
    diff = idx[:, None] - idx[None, :]
    intra = jnp.where(diff[None] >= 0, jnp.exp(diff[None] * log_gamma[:, None, None]), 0.0)
    k_decay = jnp.exp((CHUNK - 1.0 - idx)[None, :] * log_gamma[:, None])
    q_decay = jnp.exp((idx + 1.0)[None, :] * log_gamma[:, None])
    chunk_decay = jnp.exp(CHUNK * log_gamma)
    full = (HEADS, CHUNK, HD)
    dec = jnp.stack([intra, jnp.broadcast_to(k_decay[:, :, None], full), jnp.broadcast_to(q_decay[:, :, None], full),
                     jnp.broadcast_to(chunk_decay[:, None, None], full)], axis=1)
    return cos2, sin2, dec


def _rot(t, cos2, sin2):
    return t * cos2 + pltpu.roll(t, HD // 2, 1) * sin2


def _rot_t(t, cos2, sin2):
    return t * cos2 - pltpu.roll(t, HD // 2, 1) * sin2


def retention_fwd(name, zq, cos2, sin2, dec):
    tp = zq.shape[0]
    nch = tp // CHUNK
    scale = HD ** -0.5

    def body(q_ref, k_ref, v_ref, cos_ref, sin_ref, dec_ref, out_ref, st_ref, s_ref):
        s_ref[...] = jnp.zeros_like(s_ref)
        intra, kdec, qdec, cdec = dec_ref[0], dec_ref[1], dec_ref[2], dec_ref[3]

        def chunk(n, carry):
            rows = pl.ds(pl.multiple_of(n * CHUNK, CHUNK), CHUNK)
            cosv, sinv = cos_ref[rows, :], sin_ref[rows, :]
            qr = _rot(q_ref[rows, :], cosv, sinv) * scale
            kr = _rot(k_ref[rows, :], cosv, sinv)
            vb = v_ref[rows, :].astype(bf16)
            sc = (_dot_nt(qr.astype(bf16), kr.astype(bf16)) * intra).astype(bf16)
            sv = s_ref[...]
            sb = sv.astype(bf16)
            out_ref[rows, :] = _dot_nn(sc, vb) + _dot_nn((qr * qdec).astype(bf16), sb)
            st_ref[n] = sb
            s_ref[...] = sv * cdec + _dot_tn((kr * kdec).astype(bf16), vb)
            return carry

        lax.fori_loop(0, nch, chunk, 0)

    col = lambda off: pl.BlockSpec((tp, HD), lambda h: (0, off + h))
    return pl.pallas_call(
        body, name=name, grid=(HEADS,),
        in_specs=[col(0), col(HEADS), col(2 * HEADS), _resident((tp, HD)), _resident((tp, HD)),
                  pl.BlockSpec((None, 4, CHUNK, HD), lambda h: (h, 0, 0, 0))],
        out_specs=[col(0), pl.BlockSpec((None, nch, HD, HD), lambda h: (h, 0, 0, 0))],
        out_shape=[jax.ShapeDtypeStruct((tp, RW), f32), jax.ShapeDtypeStruct((HEADS, nch, HD, HD), bf16)],
        scratch_shapes=[pltpu.VMEM((HD, HD), f32)],
        compiler_params=_cp(48, ("arbitrary",)))(zq, zq, zq, cos2, sin2, dec)


def retention_bwd(name, zq, cos2, sin2, dec, states, dout, pad, deps=()):
    tp = zq.shape[0]
    nch = tp // CHUNK
    scale = HD ** -0.5

    def body(q_ref, k_ref, v_ref, cos_ref, sin_ref, dec_ref, st_ref, do_ref, dq_ref, dk_ref, dv_ref, g_ref):
        g_ref[...] = jnp.zeros_like(g_ref)
        intra, kdec, qdec, cdec = dec_ref[0], dec_ref[1], dec_ref[2], dec_ref[3]

        def chunk(t, carry):
            n = nch - 1 - t
            rows = pl.ds(pl.multiple_of(n * CHUNK, CHUNK), CHUNK)
            cosv, sinv = cos_ref[rows, :], sin_ref[rows, :]
            qr = _rot(q_ref[rows, :], cosv, sinv) * scale
            kr = _rot(k_ref[rows, :], cosv, sinv)
            qb, kb = qr.astype(bf16), kr.astype(bf16)
            vb = v_ref[rows, :].astype(bf16)
            qd = (qr * qdec).astype(bf16)
            kd = (kr * kdec).astype(bf16)
            sc = (_dot_nt(qb, kb) * intra).astype(bf16)
            dob = do_ref[rows, :]
            sb = st_ref[n]
            gv = g_ref[...]
            gb = gv.astype(bf16)
            dsc = (_dot_nt(dob, vb) * intra).astype(bf16)
            dv = _dot_tn(sc, dob) + _dot_nn(kd, gb)
            dqr = _dot_nn(dsc, kb) + _dot_nt(dob, sb) * qdec
            dkr = _dot_tn(dsc, qb) + _dot_nt(vb, gb) * kdec
            g_ref[...] = gv * cdec + _dot_tn(qd, dob)
            keep = (lax.broadcasted_iota(jnp.int32, (CHUNK, HD), 0) + n * CHUNK) >= pad
            dq_ref[rows, :] = jnp.where(keep, _rot_t(dqr * scale, cosv, sinv), 0.0).astype(bf16)
            dk_ref[rows, :] = jnp.where(keep, _rot_t(dkr, cosv, sinv), 0.0).astype(bf16)
            dv_ref[rows, :] = jnp.where(keep, dv, 0.0).astype(bf16)
            return carry

        lax.fori_loop(0, nch, chunk, 0)

    col = lambda off: pl.BlockSpec((tp, HD), lambda h: (0, off + h))
    return pl.pallas_call(
        _skip(len(deps), body), name=name, grid=(HEADS,),
        in_specs=[_ANY] * len(deps) + [col(0), col(HEADS), col(2 * HEADS), _resident((tp, HD)), _resident((tp, HD)),
                  pl.BlockSpec((None, 4, CHUNK, HD), lambda h: (h, 0, 0, 0)),
                  pl.BlockSpec((None, nch, HD, HD), lambda h: (h, 0, 0, 0)), col(0)],
        out_specs=[col(0)] * 3,
        out_shape=[jax.ShapeDtypeStruct((tp, RW), bf16)] * 3,
        scratch_shapes=[pltpu.VMEM((HD, HD), f32)],
        compiler_params=_cp(48, ("arbitrary",)))(*deps, zq, zq, zq, cos2, sin2, dec, states, dout)


def _window_sum(xv, steps, tp, forward):
    s = xv
    for j in range(steps):
        sh = 2 ** j
        s = s + pltpu.roll(s, (tp - sh) if forward else sh, 0)
    return s


def pool_fwd(name, zu, maps, scale, pad):
    tp = zu.shape[0]

    def body(u_ref, maps_ref, scale_ref, pooled_ref, p_ref):
        row = lax.broadcasted_iota(jnp.int32, (tp, HD), 0)
        for gi, w in enumerate(POOL_WINDOWS):
            cols = slice(gi * HD, (gi + 1) * HD)
            xv = u_ref[:, cols]
            cnt = jnp.clip(row - (pad - 1), 1, w).astype(f32)
            pooled = jnp.where(row >= pad, _window_sum(xv, gi + 1, tp, False) / cnt - xv, 0.0).astype(bf16)
            pooled_ref[:, cols] = pooled
            p_ref[:, cols] = (_dot_nn(pooled, maps_ref[gi].astype(bf16)) * scale_ref[:, cols]).astype(bf16)

    return pl.pallas_call(
        body, name=name,
        out_shape=[jax.ShapeDtypeStruct((tp, RW), bf16), jax.ShapeDtypeStruct((tp, RW), bf16)],
        compiler_params=_cp(56))(zu, maps, scale)


def pool_bwd(name, dp, pooled, maps, scale, pad):
    tp = dp.shape[0]

    def body(dp_ref, pooled_ref, maps_ref, scale_ref, du_ref, dmaps_ref, dscale_ref):
        row = lax.broadcasted_iota(jnp.int32, (tp, HD), 0)
        dscale_ref[...] = jnp.zeros_like(dscale_ref)
        for gi, w in enumerate(POOL_WINDOWS):
            cols = slice(gi * HD, (gi + 1) * HD)
            mb = maps_ref[gi].astype(bf16)
            pooled = pooled_ref[:, cols]
            dpf = dp_ref[:, cols].astype(f32)
            dscale_ref[0:1, cols] = jnp.sum(dpf * _dot_nn(pooled, mb), axis=0, keepdims=True)
            dpm = (dpf * scale_ref[:, cols]).astype(bf16)
            dmaps_ref[gi * HD:(gi + 1) * HD, :] = _dot_tn(pooled, dpm)
            dpool = jnp.where(row >= pad, _dot_nt(dpm, mb), 0.0)
            cnt = jnp.clip(row - (pad - 1), 1, w).astype(f32)
            du = _window_sum(dpool / cnt, gi + 1, tp, True) - dpool
            du_ref[:, cols] = jnp.where(row >= pad, du, 0.0).astype(bf16)

    return pl.pallas_call(
        body, name=name,
        out_shape=[jax.ShapeDtypeStruct((tp, RW), bf16), jax.ShapeDtypeStruct((HEADS * HD, HD), f32),
                   jax.ShapeDtypeStruct((8, RW), f32)],
        compiler_params=_cp(56))(dp, pooled, maps, scale)


def _group_norm(o):
    mu = jnp.mean(o, axis=-1, keepdims=True)
    oc = o - mu
    rstd = lax.rsqrt(jnp.mean(oc * oc, axis=-1, keepdims=True) + EPS)
    return oc * rstd, rstd


def mix_out_fwd(name, h, oraw, zg, zgate, p, wretT, wpoolT, wout):
    tp, d = h.shape
    tm = _row_tile(tp)

    def body(h_ref, o_ref, zg_ref, zgate_ref, p_ref, wr_ref, wp_ref, wo_ref, ho_ref, rp_ref, ret_ref, pool_ref, mixed_ref):
        parts = []
        for hh in range(HEADS):
            cols = slice(hh * HD, (hh + 1) * HD)
            rhat, _ = _group_norm(o_ref[:, cols])
            gv = zg_ref[:, cols]
            parts.append(rhat * (gv * _sigmoid(gv)))
        r = jnp.concatenate(parts, axis=-1).astype(bf16)
        pv = p_ref[...]
        ret = _dot_nt(r, wr_ref[...])
        pool = _dot_nt(pv, wp_ref[...])
        mixed = (_sigmoid(zgate_ref[:, :d]) * ret + _sigmoid(zgate_ref[:, d:]) * pool).astype(bf16)
        ho_ref[...] = h_ref[...] + _dot_nn(mixed, wo_ref[...])
        rp_ref[0] = r
        rp_ref[1] = pv
        ret_ref[...] = ret.astype(bf16)
        pool_ref[...] = pool.astype(bf16)
        mixed_ref[...] = mixed

    row = lambda w: pl.BlockSpec((tm, w), lambda i: (i, 0))
    return pl.pallas_call(
        body, name=name, grid=(tp // tm,),
        in_specs=[row(d), row(RW), row(RW), row(2 * d), row(RW), _resident((d, RW)), _resident((d, RW)), _resident((d, d))],
        out_specs=[row(d), pl.BlockSpec((2, tm, RW), lambda i: (0, i, 0)), row(d), row(d),
                   pl.BlockSpec((None, tm, d), lambda i: (0, i, 0))],
        out_shape=[jax.ShapeDtypeStruct((tp, d), f32), jax.ShapeDtypeStruct((2, tp, RW), bf16),
                   jax.ShapeDtypeStruct((tp, d), bf16), jax.ShapeDtypeStruct((tp, d), bf16),
                   jax.ShapeDtypeStruct((1, tp, d), bf16)],
        compiler_params=_cp(48, ("arbitrary",)))(h, oraw, zg, zgate, p, wretT, wpoolT, wout)


def mix_out_bwd(name, dy, oraw, zg, zgate, ret, pool, wretT, wpoolT, wout, deps=()):
    tp, d = dy.shape
    tm = _row_tile(tp)

    def body(dy_ref, o_ref, zg_ref, zgate_ref, ret_ref, pool_ref, wr_ref, wp_ref, wo_ref,
             do_ref, dzg_ref, dzgate_ref, dp_ref, drp_ref, dyb_ref):
        dyb = dy_ref[...].astype(bf16)
        dmixed = _dot_nt(dyb, wo_ref[...])
        sa = _sigmoid(zgate_ref[:, :d])
        sb = _sigmoid(zgate_ref[:, d:])
        dret = dmixed * sa
        dpool = dmixed * sb
        dzgate_ref[:, :d] = (dret * ret_ref[...].astype(f32) * (1.0 - sa)).astype(bf16)
        dzgate_ref[:, d:] = (dpool * pool_ref[...].astype(f32) * (1.0 - sb)).astype(bf16)
        dretb, dpoolb = dret.astype(bf16), dpool.astype(bf16)
        dr = _dot_nn(dretb, wr_ref[...])
        dp_ref[...] = _dot_nn(dpoolb, wp_ref[...]).astype(bf16)
        for hh in range(HEADS):
            cols = slice(hh * HD, (hh + 1) * HD)
            rhat, rstd = _group_norm(o_ref[:, cols])
            gv = zg_ref[:, cols]
            sg = _sigmoid(gv)
            drh = dr[:, cols]
            drhat = drh * (gv * sg)
            dzg_ref[:, cols] = (drh * rhat * (sg * (1.0 + gv * (1.0 - sg)))).astype(bf16)
            do = rstd * (drhat - jnp.mean(drhat, axis=-1, keepdims=True)
                         - rhat * jnp.mean(drhat * rhat, axis=-1, keepdims=True))
            do_ref[:, cols] = do.astype(bf16)
        drp_ref[0] = dretb
        drp_ref[1] = dpoolb
        dyb_ref[...] = dyb

    row = lambda w: pl.BlockSpec((tm, w), lambda i: (i, 0))
    return pl.pallas_call(
        _skip(len(deps), body), name=name, grid=(tp // tm,),
        in_specs=[_ANY] * len(deps) + [row(d), row(RW), row(RW), row(2 * d), row(d), row(d), _resident((d, RW)), _resident((d, RW)),
                  _resident((d, d))],
        out_specs=[row(RW), row(RW), row(2 * d), row(RW), pl.BlockSpec((2, tm, d), lambda i: (0, i, 0)),
                   pl.BlockSpec((None, tm, d), lambda i: (0, i, 0))],
        out_shape=[jax.ShapeDtypeStruct((tp, RW), bf16), jax.ShapeDtypeStruct((tp, RW), bf16),
                   jax.ShapeDtypeStruct((tp, 2 * d), bf16), jax.ShapeDtypeStruct((tp, RW), bf16),
                   jax.ShapeDtypeStruct((2, tp, d), bf16), jax.ShapeDtypeStruct((1, tp, d), bf16)],
        compiler_params=_cp(48, ("arbitrary",)))(*deps, dy, oraw, zg, zgate, ret, pool, wretT, wpoolT, wout)


def final_loss(name, h, g, target):
    tp, d = h.shape
    tm = CHUNK

    def body(h_ref, g_ref, t_ref, dh_ref, loss_ref, dg_ref):
        i = pl.program_id(0)
        hhat, rs = _rms(h_ref[...])
        gv = g_ref[...]
        err = jnp.where(i >= 1, hhat * gv - t_ref[...], 0.0)
        dyv = err / d
        dh_ref[...] = _rms_bwd(dyv, gv, hhat, rs)

        @pl.when(i == 0)
        def _():
            loss_ref[...] = jnp.zeros_like(loss_ref)
            dg_ref[...] = jnp.zeros_like(dg_ref)

        loss_ref[...] += 0.5 * jnp.sum(jnp.sum(err * err, axis=-1, keepdims=True) / d)
        dg_ref[0:1, :] += jnp.sum(dyv * hhat, axis=0, keepdims=True)

    return pl.pallas_call(
        body, name=name, grid=(tp // tm,),
        in_specs=[pl.BlockSpec((tm, d), lambda i: (i, 0)), _resident((1, d)),
                  pl.BlockSpec((tm, d), lambda i: (jnp.maximum(i - 1, 0), 0))],
        out_specs=[pl.BlockSpec((tm, d), lambda i: (i, 0)), pl.BlockSpec((8, 128), lambda i: (0, 0)),
                   pl.BlockSpec((8, d), lambda i: (0, 0))],
        out_shape=[jax.ShapeDtypeStruct((tp, d), f32), jax.ShapeDtypeStruct((8, 128), f32),
                   jax.ShapeDtypeStruct((8, d), f32)],
        compiler_params=_cp(32, ("arbitrary",)))(h, g, target)


def _adamw(w, g, m, v):
    m = ADAM_B1 * m + (1.0 - ADAM_B1) * g
    v = ADAM_B2 * v + (1.0 - ADAM_B2) * (g * g)
    m_hat = m / (1.0 - ADAM_B1 ** ADAM_STEP)
    v_hat = v / (1.0 - ADAM_B2 ** ADAM_STEP)
    delta = -ADAM_LR * (m_hat / (jnp.sqrt(v_hat) + ADAM_EPS) + ADAM_WD * w)
    return delta, m, v


def adam_big(name, me, recv, own, b, layer, transposed, w, m, v, prev):
    r, c = recv.shape[2:]
    wshape = w.shape[1:]

    def body(me_ref, recv_ref, own_ref, w_ref, m_ref, v_ref, *rest):
        g_ref, d_ref, nm_ref, nv_ref = rest[-4:]
        g = own_ref[...].astype(f32)
        for j in range(NDEV - 1):
            g = g + recv_ref[j].astype(f32)
        if transposed:
            g = g.T
        delta, nm, nv = _adamw(w_ref[...], g, m_ref[...], v_ref[...])
        g_ref[...] = g
        d_ref[...] = delta
        nm_ref[...] = nm
        nv_ref[...] = nv

    wspec = pl.BlockSpec((None,) + wshape, lambda i, me_ref: (layer, 0, 0))
    in_specs = [pl.BlockSpec((NDEV - 1, None, r, c), lambda i, me_ref: (0, b, 0, 0)),
                pl.BlockSpec((None, None, r, c), lambda i, me_ref: (b, me_ref[0], 0, 0)), wspec, wspec, wspec]
    args = [recv, own, w, m, v]
    aliases = {}
    if prev is not None:
        in_specs += [_ANY] * 4
        args += list(prev)
        aliases = {6 + k: k for k in range(4)}
    return pl.pallas_call(
        body, name=name,
        grid_spec=pltpu.PrefetchScalarGridSpec(num_scalar_prefetch=1, grid=(1,), in_specs=in_specs, out_specs=[wspec] * 4),
        out_shape=[jax.ShapeDtypeStruct(w.shape, f32)] * 4, input_output_aliases=aliases,
        compiler_params=_cp(56))(me, *args)


def adam_small(name, ga0, gmaps0, gmeta, ga1, gmaps1, norms, pool_scale, pool_maps, meta, final_norm, d):
    def body(ga0_ref, gmaps0_ref, gmeta_ref, ga1_ref, gmaps1_ref, *refs):
        ins, outs = refs[:21], refs[21:]
        x, y, c = _me()
        me = 4 * x + 2 * y + c

        def total(ref, rows):
            t = ref[0, rows, :]
            for j in range(1, NDEV):
                t = t + ref[j, rows, :]
            return t

        row = lambda r: slice(r, r + 1)
        outs[0][...] = jnp.broadcast_to(total(ga1_ref, row(0))[:, :128], (8, 128))

        def update(k, g, o):
            w_ref, m_ref, v_ref = ins[3 * k:3 * k + 3]
            delta, nm, nv = _adamw(w_ref[...], g, m_ref[...], v_ref[...])
            for ref, val in zip(outs[o:o + 4], (g, delta, nm, nv)):
                ref[...] = val

        two = lax.broadcasted_iota(jnp.int32, (2, d), 0)
        for k in range(3):
            update(k, jnp.where(two == 0, total(ga0_ref, row(k)), total(ga1_ref, row(2 + k))), 1 + 4 * k)
        update(3, jnp.where(two[:, :RW] == 0, total(ga0_ref, row(3))[:, :RW], total(ga1_ref, row(5))[:, :RW]), 13)
        update(4, jnp.concatenate([total(gmaps0_ref, slice(None)), total(gmaps1_ref, slice(None))], axis=0), 17)
        update(5, total(gmeta_ref, pl.ds(pl.multiple_of(me * N_META, N_META), N_META)), 21)
        update(6, total(ga1_ref, row(1)), 25)

    flat = []
    for trip in (*norms, pool_scale, pool_maps, meta, final_norm):
        flat += list(trip)
    out_shapes = [jax.ShapeDtypeStruct((8, 128), f32)]
    for trip in (*norms, pool_scale, pool_maps, meta, final_norm):
        out_shapes += [jax.ShapeDtypeStruct(trip[0].shape, f32)] * 4
    return pl.pallas_call(body, name=name, out_shape=out_shapes,
                          compiler_params=_cp(32))(ga0, gmaps0, gmeta, ga1, gmaps1, *flat)


def kernel(x, meta, ffn1_norm, ffn1_gate, ffn1_up, ffn1_down, mix_norm, w_in, pool_maps, pool_scale, w_ret_up, w_pool_up, w_out, ffn2_norm, ffn2_gate, ffn2_up, ffn2_down, final_norm, loss_target, m_meta, m_ffn1_norm, m_ffn1_gate, m_ffn1_up, m_ffn1_down, m_mix_norm, m_w_in, m_pool_maps, m_pool_scale, m_w_ret_up, m_w_pool_up, m_w_out, m_ffn2_norm, m_ffn2_gate, m_ffn2_up, m_ffn2_down, m_final_norm, v_meta, v_ffn1_norm, v_ffn1_gate, v_ffn1_up, v_ffn1_down, v_mix_norm, v_w_in, v_pool_maps, v_pool_scale, v_w_ret_up, v_w_pool_up, v_w_out, v_ffn2_norm, v_ffn2_gate, v_ffn2_up, v_ffn2_down, v_final_norm):
    seq, d = x.shape[1], x.shape[2]
    depth = ffn1_gate.shape[0]
    ff = ffn1_gate.shape[2] * NDEV
    nin = w_in.shape[2] * NDEV
    length = seq + N_META
    pad = (-length) % CHUNK
    tp = length + pad
    assert pad % 8 == 0 and pad + N_META == CHUNK and depth == 2 and nin == 5 * RW + 2 * d

    ix, iy, ic = _me()
    me = (4 * ix + 2 * iy + ic).astype(jnp.int32).reshape(1)

    meta_all, = all_gather("gather_meta", [meta])
    meta_full = jnp.transpose(meta_all, (1, 0, 2)).reshape(N_META, d)

    gathers = {}
    token = meta_all
    for layer in range(depth):
        lands = prep_layer(layer, me, [ffn1_gate, ffn1_up, ffn2_gate, ffn2_up, w_in, w_ret_up, w_pool_up],
                           [ffn1_down, ffn2_down, w_out])
        g1T, u1T, g2T, u2T, winT, wretT, wpoolT, d1, d2, wout = lands
        for part, group in (("ffn1", [g1T, u1T, d1]), ("mix", [winT, wretT, wpoolT, wout]), ("ffn2", [g2T, u2T, d2])):
            ssem, rsem, group, token = gather_start(f"gather_start_{part}{layer}", group, (token,))
            gathers[(part, layer)] = (ssem, rsem, group)

    def gathered(part, layer, after):
        ssem, rsem, group = gathers[(part, layer)]
        _, full = copies_wait(f"gather_wait_{part}{layer}", ssem, rsem, (), group, after)
        return [a.reshape((NDEV * a.shape[1],) + a.shape[2:]) for a in full]

    cos2, sin2, dec = _retention_tables(tp, pad)
    h = jnp.concatenate([jnp.zeros((pad, d), f32), meta_full, x[0]], axis=0)

    saved = []
    weights = []
    for layer in range(depth):
        row = lambda a: a[layer:layer + 1]
        s = {"h0": h}
        g1T, u1T, d1 = gathered("ffn1", layer, token if layer == 0 else h)
        h, s["a1"], s["b1"] = ffn_fwd(f"ffn1_fwd{layer}", h, row(ffn1_norm), g1T, u1T, d1)
        s["h1"] = h
        winT, wretT, wpoolT, wout = gathered("mix", layer, h)
        s["zq"], s["zg"], zu, s["zgate"] = mix_in_fwd(f"mix_in_fwd{layer}", h, row(mix_norm), winT)
        s["oraw"], s["states"] = retention_fwd(f"retention_fwd{layer}", s["zq"], cos2, sin2, dec)
        s["pooled"], p = pool_fwd(f"pool_fwd{layer}", zu, pool_maps[layer], row(pool_scale), pad)
        h, s["rp"], s["ret"], s["pool"], s["mixed"] = mix_out_fwd(
            f"mix_out_fwd{layer}", h, s["oraw"], s["zg"], s["zgate"], p, wretT, wpoolT, wout)
        s["h2"] = h
        g2T, u2T, d2 = gathered("ffn2", layer, h)
        h, s["a2"], s["b2"] = ffn_fwd(f"ffn2_fwd{layer}", h, row(ffn2_norm), g2T, u2T, d2)
        saved.append(s)
        weights.append((g1T, u1T, g2T, u2T, winT, wretT, wpoolT, d1, d2, wout))

    dh, loss_part, dg_final = final_loss("final_loss", h, final_norm.reshape(1, d), loss_target[0])

    small = {}
    small_gathers = {}
    exchanges = {}
    token = None

    def rows8(vals):
        at = lax.broadcasted_iota(jnp.int32, (8, d), 0)
        out = jnp.zeros((8, d), f32)
        for k, v in enumerate(vals):
            r0 = v[0:1]
            r0 = r0 if r0.shape[1] == d else jnp.pad(r0, ((0, 0), (0, d - r0.shape[1])))
            out = jnp.where(at == k, r0, out)
        return out

    def exchange(part, layer, grads):
        by_dest = [g.reshape(g.shape[0], NDEV, g.shape[1] // NDEV, g.shape[2]) for g in grads]
        ssem, rsem, sent, lands, tok = exchange_start(f"exchange_start_{part}{layer}", by_dest)
        exchanges[(part, layer)] = (ssem, rsem, sent, lands)
        return (tok,)

    for layer in reversed(range(depth)):
        g1T, u1T, g2T, u2T, winT, wretT, wpoolT, d1, d2, wout = weights[layer]
        row = lambda a: a[layer:layer + 1]
        s = saved[layer]
        dh, lhs2, rhs2, small[("ffn2", layer)] = ffn_bwd(
            f"ffn2_bwd{layer}", dh, s["h2"], row(ffn2_norm), s["a2"], s["b2"], g2T, u2T, d2, () if token is None else token)
        token = exchange("ffn2", layer, [mm_tn(f"ffn2_wgrad{layer}", lhs2, rhs2, lambda b: b // 2)])
        do, dzg, dzgate, dp, drp, dyb = mix_out_bwd(
            f"mix_out_bwd{layer}", dh, s["oraw"], s["zg"], s["zgate"], s["ret"], s["pool"], wretT, wpoolT, wout, token)
        token = exchange("mix", layer, [mm_tn(f"w_out_wgrad{layer}", s["mixed"], dyb, lambda b: b),
                                        mm_tn(f"up_wgrad{layer}", drp, s["rp"], lambda b: b)])
        dq, dk, dv = retention_bwd(f"retention_bwd{layer}", s["zq"], cos2, sin2, dec, s["states"], do, pad, token)
        dzu, small[("maps", layer)], small[("scale", layer)] = pool_bwd(
            f"pool_bwd{layer}", dp, s["pooled"], pool_maps[layer], row(pool_scale), pad)
        dh, dz, n2, small[("mix", layer)] = mix_in_bwd(
            f"mix_in_bwd{layer}", dq, dk, dv, dzg, dzu, dzgate, s["h1"], row(mix_norm), winT, dh)
        token = exchange("w_in", layer, [mm_tn(f"w_in_wgrad{layer}", dz, n2, lambda b: b)])
        dh, lhs1, rhs1, small[("ffn1", layer)] = ffn_bwd(
            f"ffn1_bwd{layer}", dh, s["h0"], row(ffn1_norm), s["a1"], s["b1"], g1T, u1T, d1, token)
        rows = [small[("ffn1", layer)], small[("mix", layer)], small[("ffn2", layer)], small[("scale", layer)]]
        packs = [rows8([loss_part, dg_final] + rows if layer == depth - 1 else rows), small[("maps", layer)]]
        if layer == 0:
            dmeta = dh[pad:CHUNK]
            packs.append(jnp.transpose(dmeta.reshape(N_META, NDEV, d // NDEV), (1, 0, 2)).reshape(NDEV * N_META, d // NDEV))
        ssem, rsem, lands, tok = gather_start(f"small_start{layer}", slot_in(f"small_slot{layer}", me, packs))
        small_gathers[layer] = (ssem, rsem, lands)
        token = exchange("ffn1", layer, [mm_tn(f"ffn1_wgrad{layer}", lhs1, rhs1, lambda b: b // 2, (tok,))])

    grad_x = dh[CHUNK:][None]

    big = {}
    after = dh
    plans = {
        "ffn2": [("ffn2_gate", 0, 0, True, ffn2_gate, m_ffn2_gate, v_ffn2_gate),
                 ("ffn2_up", 0, 1, True, ffn2_up, m_ffn2_up, v_ffn2_up),
                 ("ffn2_down", 0, 2, False, ffn2_down, m_ffn2_down, v_ffn2_down)],
        "mix": [("w_out", 0, 0, False, w_out, m_w_out, v_w_out),
                ("w_ret_up", 1, 0, True, w_ret_up, m_w_ret_up, v_w_ret_up),
                ("w_pool_up", 1, 1, True, w_pool_up, m_w_pool_up, v_w_pool_up)],
        "w_in": [("w_in", 0, 0, True, w_in, m_w_in, v_w_in)],
        "ffn1": [("ffn1_gate", 0, 0, True, ffn1_gate, m_ffn1_gate, v_ffn1_gate),
                 ("ffn1_up", 0, 1, True, ffn1_up, m_ffn1_up, v_ffn1_up),
                 ("ffn1_down", 0, 2, False, ffn1_down, m_ffn1_down, v_ffn1_down)]}
    for layer in reversed(range(depth)):
        for part in ("ffn2", "mix", "w_in", "ffn1"):
            ssem, rsem, sent, lands = exchanges[(part, layer)]
            sent, lands = copies_wait(f"exchange_wait_{part}{layer}", ssem, rsem, sent, lands, after)
            for nm, k, b, tr, w, m, v in plans[part]:
                big[nm] = adam_big(f"adam_{nm}{layer}", me, lands[k], sent[k], b, layer, tr, w, m, v, big.get(nm))
                after = big[nm][0]

    gsmall = []
    for layer in range(depth):
        ssem, rsem, lands = small_gathers[layer]
        gsmall += copies_wait(f"small_wait{layer}", ssem, rsem, (), lands, after)[1]

    maps2 = lambda a: a.reshape(depth * HEADS * HD, HD)
    res = adam_small(
        "adam_small", *gsmall,
        [(ffn1_norm, m_ffn1_norm, v_ffn1_norm), (mix_norm, m_mix_norm, v_mix_norm), (ffn2_norm, m_ffn2_norm, v_ffn2_norm)],
        (pool_scale, m_pool_scale, v_pool_scale), (maps2(pool_maps), maps2(m_pool_maps), maps2(v_pool_maps)),
        (meta, m_meta, v_meta), tuple(a.reshape(1, d) for a in (final_norm, m_final_norm, v_final_norm)), d)
    loss = res[0][0, 0]
    sm = {}
    for k, nm in enumerate(["ffn1_norm", "mix_norm", "ffn2_norm", "pool_scale", "pool_maps", "meta", "final_norm"]):
        sm[nm] = list(res[1 + 4 * k:5 + 4 * k])
    sm["pool_maps"] = [a.reshape(pool_maps.shape) for a in sm["pool_maps"]]
    sm["final_norm"] = [a.reshape(d) for a in sm["final_norm"]]

    names = ["meta", "ffn1_norm", "ffn1_gate", "ffn1_up", "ffn1_down", "mix_norm", "w_in", "pool_maps", "pool_scale",
             "w_ret_up", "w_pool_up", "w_out", "ffn2_norm", "ffn2_gate", "ffn2_up", "ffn2_down", "final_norm"]
    allw = {**{k: list(v) for k, v in big.items()}, **sm}
    outs = [loss, grad_x]
    for kind in range(4):
        outs += [allw[nm][kind] for nm in names]
    return tuple(outs)
```

```python
import functools

import jax
import jax.numpy as jnp
from jax import lax
from jax.experimental import pallas as pl
from jax.experimental.pallas import tpu as pltpu

f32 = jnp.float32
bf16 = jnp.bfloat16
MESH = pl.DeviceIdType.MESH
NDEV = 8
N_META = 16
HEADS = 4
HD = 128
CHUNK = 128
RW = HEADS * HD
POOL_WINDOWS = (2, 4, 8, 16)
ROPE_BASE = 10000.0
EPS = 1e-6
ADAM_LR = 0.001
ADAM_B1 = 0.9
ADAM_B2 = 0.999
ADAM_EPS = 1e-08
ADAM_WD = 0.01
ADAM_STEP = 10
VMEM_CAP_MB = 60


def _cp(vmem_mb, sem=None):
    return pltpu.CompilerParams(vmem_limit_bytes=min(vmem_mb, VMEM_CAP_MB) * 2**20, dimension_semantics=sem)


def _row_tile(tp, want=384):
    return want if tp % want == 0 else 128


def _resident(shape):
    nd = len(shape)
    return pl.BlockSpec(shape, lambda *_: (0,) * nd, pipeline_mode=pl.Buffered(1))


def _skip(nd, body):
    return (lambda *refs: body(*refs[nd:])) if nd else body


def _dot_nn(a, b):
    return lax.dot_general(a, b, (((1,), (0,)), ((), ())), preferred_element_type=f32)


def _dot_nt(a, b):
    return lax.dot_general(a, b, (((1,), (1,)), ((), ())), preferred_element_type=f32)


def _dot_tn(a, b):
    return lax.dot_general(a, b, (((0,), (0,)), ((), ())), preferred_element_type=f32)


def _rms(h):
    rs = lax.rsqrt(jnp.mean(h * h, axis=-1, keepdims=True) + EPS)
    return h * rs, rs


def _rms_bwd(dn, g, hhat, rs):
    dhh = dn * g
    return rs * (dhh - hhat * jnp.mean(dhh * hhat, axis=-1, keepdims=True))


def _sigmoid(x):
    return jax.nn.sigmoid(x)


def _me():
    return lax.axis_index("x"), lax.axis_index("y"), lax.axis_index("c")


def _peer(idx):
    return (idx // 4, (idx // 2) % 2, idx % 2)


def all_gather(name, arrays):
    n = len(arrays)

    def body(*refs):
        ins, outs = refs[:n], refs[n:2 * n]
        send_sems, recv_sems, local_sems = refs[2 * n:]
        x, y, c = _me()
        me = 4 * x + 2 * y + c
        locals_ = []
        for k in range(n):
            cp = pltpu.make_async_copy(ins[k], outs[k].at[me], local_sems.at[k])
            cp.start()
            locals_.append(cp)
        for d in range(1, NDEV):
            for k in range(n):
                pltpu.make_async_remote_copy(
                    src_ref=ins[k], dst_ref=outs[k].at[me], send_sem=send_sems.at[k], recv_sem=recv_sems.at[k],
                    device_id=_peer((me + d) % NDEV), device_id_type=MESH).start()
        for k in range(n):
            seven = outs[k].at[pl.ds(0, NDEV - 1)]
            w = pltpu.make_async_remote_copy(src_ref=seven, dst_ref=seven, send_sem=send_sems.at[k],
                                             recv_sem=recv_sems.at[k], device_id=(x, y, c), device_id_type=MESH)
            w.wait_send()
            w.wait_recv()
            locals_[k].wait()

    anyspec = pl.BlockSpec(memory_space=pl.ANY)
    return pl.pallas_call(
        body, name=name,
        out_shape=[jax.ShapeDtypeStruct((NDEV,) + a.shape, a.dtype) for a in arrays],
        in_specs=[anyspec] * n, out_specs=[anyspec] * n,
        scratch_shapes=[pltpu.SemaphoreType.DMA((n,)), pltpu.SemaphoreType.DMA((n,)), pltpu.SemaphoreType.DMA((n,))],
    )(*arrays)


_HBM = pl.BlockSpec(memory_space=pltpu.HBM)
_SEM = pl.BlockSpec(memory_space=pltpu.SEMAPHORE)
_ANY = pl.BlockSpec(memory_space=pl.ANY)
_EFFECT = pltpu.SideEffectType.DATAFLOW_SIDE_EFFECTING


def _in_hbm(a):
    return pltpu.with_memory_space_constraint(a, pltpu.HBM)


def gather_start(name, lands, deps=()):
    n, nd = len(lands), len(deps)

    def body(*refs):
        land = refs[nd:nd + n]
        send_sems, recv_sems = refs[nd + n:nd + n + 2]
        token = refs[-1]
        x, y, c = _me()
        me = 4 * x + 2 * y + c
        for d in range(1, NDEV):
            for k in range(n):
                pltpu.make_async_remote_copy(
                    src_ref=land[k].at[me], dst_ref=land[k].at[me], send_sem=send_sems.at[k], recv_sem=recv_sems.at[k],
                    device_id=_peer((me + d) % NDEV), device_id_type=MESH).start()
        token[...] = jnp.zeros_like(token)

    res = pl.pallas_call(
        body, name=name,
        out_shape=(pltpu.SemaphoreType.DMA((n,)), pltpu.SemaphoreType.DMA((n,)),
                   *[pltpu.HBM(a.shape, a.dtype) for a in lands], jax.ShapeDtypeStruct((8, 128), f32)),
        in_specs=[_ANY] * nd + [_HBM] * n,
        out_specs=(_SEM, _SEM, *[_HBM] * n, pl.BlockSpec(memory_space=pltpu.VMEM)),
        input_output_aliases={nd + k: 2 + k for k in range(n)},
        compiler_params=pltpu.CompilerParams(has_side_effects=_EFFECT),
    )(*deps, *[_in_hbm(a) for a in lands])
    return res[0], res[1], list(res[2:2 + n]), res[-1]


def _other_chips(x, y):
    return [(1 - x, y), (x, 1 - y), (1 - x, 1 - y)]


def gather_start_chips(name, lands, deps=()):
    n, nd = len(lands), len(deps)

    def body(*refs):
        land = refs[nd:nd + n]
        send_sems, recv_sems = refs[nd + n:nd + n + 2]
        token = refs[-1]
        x, y, c = _me()
        me = 4 * x + 2 * y + c
        for k in range(n):
            for to in [(x, y, 1 - c)] + [(cx, cy, c) for cx, cy in _other_chips(x, y)]:
                pltpu.make_async_remote_copy(
                    src_ref=land[k].at[me], dst_ref=land[k].at[me], send_sem=send_sems.at[k], recv_sem=recv_sems.at[k],
                    device_id=to, device_id_type=MESH).start()
        token[...] = jnp.zeros_like(token)

    res = pl.pallas_call(
        body, name=name,
        out_shape=(pltpu.SemaphoreType.DMA((n,)), pltpu.SemaphoreType.DMA((n,)),
                   *[pltpu.HBM(a.shape, a.dtype) for a in lands], jax.ShapeDtypeStruct((8, 128), f32)),
        in_specs=[_ANY] * nd + [_HBM] * n,
        out_specs=(_SEM, _SEM, *[_HBM] * n, pl.BlockSpec(memory_space=pltpu.VMEM)),
        input_output_aliases={nd + k: 2 + k for k in range(n)},
        compiler_params=pltpu.CompilerParams(has_side_effects=_EFFECT),
    )(*deps, *[_in_hbm(a) for a in lands])
    return res[0], res[1], list(res[2:2 + n]), res[-1]


def gather_forward(name, send_sems, recv_sems, lands, after):
    n = len(lands)

    def body(*refs):
        land = refs[:n]
        ssem, rsem = refs[n:n + 2]
        send2, recv2 = refs[n + 3:n + 5]
        token = refs[-1]
        x, y, c = _me()
        for k in range(n):
            four = land[k].at[pl.ds(0, 4)]
            w = pltpu.make_async_remote_copy(src_ref=four, dst_ref=four, send_sem=ssem.at[k], recv_sem=rsem.at[k],
                                             device_id=(x, y, c), device_id_type=MESH)
            w.wait_send()
            w.wait_recv()
            for cx, cy in _other_chips(x, y):
                slot = 4 * cx + 2 * cy + c
                pltpu.make_async_remote_copy(
                    src_ref=land[k].at[slot], dst_ref=land[k].at[slot], send_sem=send2.at[k], recv_sem=recv2.at[k],
                    device_id=(x, y, 1 - c), device_id_type=MESH).start()
        token[...] = jnp.zeros_like(token)

    res = pl.pallas_call(
        body, name=name,
        out_shape=(pltpu.SemaphoreType.DMA((n,)), pltpu.SemaphoreType.DMA((n,)),
                   *[pltpu.HBM(a.shape, a.dtype) for a in lands], jax.ShapeDtypeStruct((8, 128), f32)),
        in_specs=[_HBM] * n + [_SEM, _SEM, _ANY],
        out_specs=(_SEM, _SEM, *[_HBM] * n, pl.BlockSpec(memory_space=pltpu.VMEM)),
        input_output_aliases={k: 2 + k for k in range(n)},
        compiler_params=pltpu.CompilerParams(has_side_effects=_EFFECT),
    )(*lands, send_sems, recv_sems, after)
    return res[0], res[1], list(res[2:2 + n]), res[-1]


def exchange_start(name, grads, deps=()):
    n, nd = len(grads), len(deps)
    lands = [lax.empty((NDEV - 1, g.shape[0]) + g.shape[2:], g.dtype) for g in grads]

    def body(*refs):
        src = refs[nd:nd + n]
        land = refs[nd + n:nd + 2 * n]
        send_sems, recv_sems = refs[nd + 2 * n:nd + 2 * n + 2]
        token = refs[-1]
        x, y, c = _me()
        me = 4 * x + 2 * y + c
        for d in range(1, NDEV):
            p = (me + d) % NDEV
            for k in range(n):
                pltpu.make_async_remote_copy(
                    src_ref=src[k].at[:, p], dst_ref=land[k].at[d - 1], send_sem=send_sems.at[k], recv_sem=recv_sems.at[k],
                    device_id=_peer(p), device_id_type=MESH).start()
        token[...] = jnp.zeros_like(token)

    both = list(grads) + lands
    res = pl.pallas_call(
        body, name=name,
        out_shape=(pltpu.SemaphoreType.DMA((n,)), pltpu.SemaphoreType.DMA((n,)),
                   *[pltpu.HBM(a.shape, a.dtype) for a in both], jax.ShapeDtypeStruct((8, 128), f32)),
        in_specs=[_ANY] * nd + [_HBM] * (2 * n),
        out_specs=(_SEM, _SEM, *[_HBM] * (2 * n), pl.BlockSpec(memory_space=pltpu.VMEM)),
        input_output_aliases={nd + k: 2 + k for k in range(2 * n)},
        compiler_params=pltpu.CompilerParams(has_side_effects=_EFFECT),
    )(*deps, *[_in_hbm(a) for a in both])
    return res[0], res[1], list(res[2:2 + n]), list(res[2 + n:2 + 2 * n]), res[-1]


def copies_wait(name, send_sems, recv_sems, sent, lands, after, count=NDEV - 1):
    ns, n = len(sent), len(lands)

    def body(*refs):
        land = refs[ns:ns + n]
        ssem, rsem = refs[ns + n:ns + n + 2]
        x, y, c = _me()
        for k in range(n):
            seven = land[k].at[pl.ds(0, count)]
            w = pltpu.make_async_remote_copy(src_ref=seven, dst_ref=seven, send_sem=ssem.at[k], recv_sem=rsem.at[k],
                                             device_id=(x, y, c), device_id_type=MESH)
            w.wait_send()
            w.wait_recv()

    both = list(sent) + list(lands)
    res = pl.pallas_call(
        body, name=name, out_shape=tuple(pltpu.HBM(a.shape, a.dtype) for a in both),
        in_specs=[_HBM] * (ns + n) + [_SEM, _SEM, _ANY], out_specs=tuple([_HBM] * (ns + n)),
        input_output_aliases={k: k for k in range(ns + n)},
        compiler_params=pltpu.CompilerParams(has_side_effects=_EFFECT),
    )(*both, send_sems, recv_sems, after)
    return list(res[:ns]), list(res[ns:])


def prep_layer(layer, me, col_sharded, row_sharded):
    nc, nr = len(col_sharded), len(row_sharded)

    def body(me_ref, *refs):
        ins, outs = refs[:nc + nr], refs[nc + nr:]
        for k in range(nc):
            outs[k][...] = ins[k][...].T.astype(bf16)
        for k in range(nc, nc + nr):
            outs[k][...] = ins[k][...].astype(bf16)

    arrs = list(col_sharded) + list(row_sharded)
    in_specs = [pl.BlockSpec((None,) + a.shape[1:], lambda i, me_ref: (layer, 0, 0)) for a in arrs]
    shapes = [(a.shape[2], a.shape[1]) for a in col_sharded] + [a.shape[1:] for a in row_sharded]
    out_specs = [pl.BlockSpec((None,) + s, lambda i, me_ref: (me_ref[0], 0, 0)) for s in shapes]
    return pl.pallas_call(
        body, name=f"prep_layer{layer}",
        grid_spec=pltpu.PrefetchScalarGridSpec(num_scalar_prefetch=1, grid=(1,), in_specs=in_specs, out_specs=out_specs),
        out_shape=[jax.ShapeDtypeStruct((NDEV,) + s, bf16) for s in shapes], compiler_params=_cp(48))(me, *arrs)


def slot_in(name, me, arrays):
    n = len(arrays)

    def body(me_ref, *refs):
        for k in range(n):
            refs[n + k][...] = refs[k][...]

    in_specs = [pl.BlockSpec(a.shape, lambda i, me_ref: (0, 0)) for a in arrays]
    out_specs = [pl.BlockSpec((None,) + a.shape, lambda i, me_ref: (me_ref[0], 0, 0)) for a in arrays]
    return pl.pallas_call(
        body, name=name,
        grid_spec=pltpu.PrefetchScalarGridSpec(num_scalar_prefetch=1, grid=(1,), in_specs=in_specs, out_specs=out_specs),
        out_shape=[jax.ShapeDtypeStruct((NDEV,) + a.shape, a.dtype) for a in arrays])(me, *arrays)


def ffn_fwd(name, h, g, wgT, wuT, wd, deps=()):
    tp, d = h.shape
    ff = wgT.shape[0]
    tm = _row_tile(tp)

    def body(h_ref, g_ref, wg_ref, wu_ref, wd_ref, ho_ref, a_ref, b_ref):
        hh = h_ref[...]
        hhat, _ = _rms(hh)
        n = (hhat * g_ref[...]).astype(bf16)
        a = _dot_nt(n, wg_ref[...])
        b = _dot_nt(n, wu_ref[...])
        s = (a * _sigmoid(a)) * b
        ho_ref[...] = hh + 0.5 * _dot_nn(s.astype(bf16), wd_ref[...])
        a_ref[...] = a.astype(bf16)
        b_ref[...] = b.astype(bf16)

    row = lambda w: pl.BlockSpec((tm, w), lambda i: (i, 0))
    return pl.pallas_call(
        _skip(len(deps), body), name=name, grid=(tp // tm,),
        in_specs=[_ANY] * len(deps) + [row(d), _resident((1, d)), _resident((ff, d)), _resident((ff, d)), _resident((ff, d))],
        out_specs=[row(d), row(ff), row(ff)],
        out_shape=[jax.ShapeDtypeStruct((tp, d), f32), jax.ShapeDtypeStruct((tp, ff), bf16),
                   jax.ShapeDtypeStruct((tp, ff), bf16)],
        compiler_params=_cp(56, ("arbitrary",)))(*deps, h, g, wgT, wuT, wd)


def ffn_bwd(name, dy, h, g, a, b, wgT, wuT, wd, deps=()):
    tp, d = h.shape
    ff = wgT.shape[0]
    tm = _row_tile(tp, 192)

    def body(dy_ref, h_ref, g_ref, a_ref, b_ref, wg_ref, wu_ref, wd_ref, dh_ref, lhs_ref, rhs_ref, dg_ref):
        dyv = dy_ref[...]
        hhat, rs = _rms(h_ref[...])
        gv = g_ref[...]
        n = hhat * gv
        dyh = (0.5 * dyv).astype(bf16)
        ds = _dot_nt(dyh, wd_ref[...])
        av = a_ref[...].astype(f32)
        bv = b_ref[...].astype(f32)
        sg = _sigmoid(av)
        sa = av * sg
        da = (ds * bv * (sg * (1.0 + av * (1.0 - sg)))).astype(bf16)
        db = (ds * sa).astype(bf16)
        dn = _dot_nn(da, wg_ref[...]) + _dot_nn(db, wu_ref[...])
        dh_ref[...] = dyv + _rms_bwd(dn, gv, hhat, rs)

        @pl.when(pl.program_id(0) == 0)
        def _():
            dg_ref[...] = jnp.zeros_like(dg_ref)

        dg_ref[0:1, :] += jnp.sum(dn * hhat, axis=0, keepdims=True)
        lhs_ref[0] = da
        lhs_ref[1] = db
        lhs_ref[2] = (sa * bv).astype(bf16)
        rhs_ref[0] = n.astype(bf16)
        rhs_ref[1] = dyh

    row = lambda w: pl.BlockSpec((tm, w), lambda i: (i, 0))
    return pl.pallas_call(
        _skip(len(deps), body), name=name, grid=(tp // tm,),
        in_specs=[_ANY] * len(deps) + [row(d), row(d), _resident((1, d)), row(ff), row(ff),
                  _resident((ff, d)), _resident((ff, d)), _resident((ff, d))],
        out_specs=[row(d), pl.BlockSpec((3, tm, ff), lambda i: (0, i, 0)), pl.BlockSpec((2, tm, d), lambda i: (0, i, 0)),
                   pl.BlockSpec((8, d), lambda i: (0, 0))],
        out_shape=[jax.ShapeDtypeStruct((tp, d), f32), jax.ShapeDtypeStruct((3, tp, ff), bf16),
                   jax.ShapeDtypeStruct((2, tp, d), bf16), jax.ShapeDtypeStruct((8, d), f32)],
        compiler_params=_cp(58, ("arbitrary",)))(*deps, dy, h, g, a, b, wgT, wuT, wd)


def mm_tn(name, lhs, rhs, rhs_of, deps=()):
    nb, tp, m = lhs.shape
    n = rhs.shape[2]
    fixed = m * n * (4 + 2 * 2)
    tk = next((t for t in (1408, 704, 384) if tp % t == 0 and fixed + 2 * t * (m + n) * 2 <= 46 * 2**20), 128)
    nk = tp // tk

    def body(l_ref, r_ref, o_ref, acc_ref):
        k = pl.program_id(1)

        @pl.when(k == 0)
        def _():
            acc_ref[...] = jnp.zeros_like(acc_ref)

        acc_ref[...] += _dot_tn(l_ref[...], r_ref[...])

        @pl.when(k == nk - 1)
        def _():
            o_ref[...] = acc_ref[...].astype(o_ref.dtype)

    return pl.pallas_call(
        _skip(len(deps), body), name=name, grid=(nb, nk),
        in_specs=[_ANY] * len(deps) + [pl.BlockSpec((None, tk, m), lambda b, k: (b, k, 0)),
                                       pl.BlockSpec((None, tk, n), lambda b, k: (rhs_of(b), k, 0))],
        out_specs=pl.BlockSpec((None, m, n), lambda b, k: (b, 0, 0)),
        out_shape=jax.ShapeDtypeStruct((nb, m, n), bf16),
        scratch_shapes=[pltpu.VMEM((m, n), f32)],
        compiler_params=_cp(56, ("arbitrary", "arbitrary")))(*deps, lhs, rhs)


def mix_in_fwd(name, h, g, winT):
    tp, d = h.shape
    nin = winT.shape[0]
    tm = _row_tile(tp)

    def body(h_ref, g_ref, w_ref, zq_ref, zg_ref, zu_ref, zgate_ref):
        hhat, _ = _rms(h_ref[...])
        z = _dot_nt((hhat * g_ref[...]).astype(bf16), w_ref[...])
        zq_ref[...] = z[:, :3 * RW]
        zg_ref[...] = z[:, 3 * RW:4 * RW]
        zu_ref[...] = z[:, 4 * RW:5 * RW]
        zgate_ref[...] = z[:, 5 * RW:]

    row = lambda w: pl.BlockSpec((tm, w), lambda i: (i, 0))
    widths = (3 * RW, RW, RW, 2 * d)
    return pl.pallas_call(
        body, name=name, grid=(tp // tm,),
        in_specs=[row(d), _resident((1, d)), _resident((nin, d))],
        out_specs=[row(w) for w in widths],
        out_shape=[jax.ShapeDtypeStruct((tp, w), f32) for w in widths],
        compiler_params=_cp(56, ("arbitrary",)))(h, g, winT)


def mix_in_bwd(name, dq, dk, dv, dzg, dzu, dzgate, h, g, winT, dres):
    tp, d = h.shape
    nin = winT.shape[0]
    tm = _row_tile(tp)

    def body(dq_ref, dk_ref, dv_ref, dzg_ref, dzu_ref, dzgate_ref, h_ref, g_ref, w_ref, dres_ref, dh_ref, dz_ref, n_ref, dg_ref):
        dz = jnp.concatenate([dq_ref[...], dk_ref[...], dv_ref[...], dzg_ref[...], dzu_ref[...], dzgate_ref[...]], axis=-1)
        dn = _dot_nn(dz, w_ref[...])
        hhat, rs = _rms(h_ref[...])
        gv = g_ref[...]
        dh_ref[...] = dres_ref[...] + _rms_bwd(dn, gv, hhat, rs)

        @pl.when(pl.program_id(0) == 0)
        def _():
            dg_ref[...] = jnp.zeros_like(dg_ref)

        dg_ref[0:1, :] += jnp.sum(dn * hhat, axis=0, keepdims=True)
        dz_ref[...] = dz
        n_ref[...] = (hhat * gv).astype(bf16)

    row = lambda w: pl.BlockSpec((tm, w), lambda i: (i, 0))
    return pl.pallas_call(
        body, name=name, grid=(tp // tm,),
        in_specs=[row(RW)] * 5 + [row(2 * d), row(d), _resident((1, d)), _resident((nin, d)), row(d)],
        out_specs=[row(d), pl.BlockSpec((None, tm, nin), lambda i: (0, i, 0)), pl.BlockSpec((None, tm, d), lambda i: (0, i, 0)),
                   pl.BlockSpec((8, d), lambda i: (0, 0))],
        out_shape=[jax.ShapeDtypeStruct((tp, d), f32), jax.ShapeDtypeStruct((1, tp, nin), bf16),
                   jax.ShapeDtypeStruct((1, tp, d), bf16), jax.ShapeDtypeStruct((8, d), f32)],
        compiler_params=_cp(56, ("arbitrary",)))(dq, dk, dv, dzg, dzu, dzgate, h, g, winT, dres)


def _retention_tables(tp, pad):
    half = HD // 2
    inv_freq = ROPE_BASE ** (-jnp.arange(half, dtype=f32) / half)
    pos = jnp.arange(tp, dtype=f32) - pad
    ang = pos[:, None] * inv_freq[None, :]
    cos, sin = jnp.cos(ang), jnp.sin(ang)
    cos2 = jnp.concatenate([cos, cos], axis=-1)
    sin2 = jnp.concatenate([-sin, sin], axis=-1)
    log_gamma = jnp.log1p(-(2.0 ** (-5.0 - jnp.arange(HEADS, dtype=f32))))
    idx = jnp.arange(CHUNK, dtype=f32)
    diff = idx[:, None] - idx[None, :]
    intra = jnp.where(diff[None] >= 0, jnp.exp(diff[None] * log_gamma[:, None, None]), 0.0)
    k_decay = jnp.exp((CHUNK - 1.0 - idx)[None, :] * log_gamma[:, None])
    q_decay = jnp.exp((idx + 1.0)[None, :] * log_gamma[:, None])
    chunk_decay = jnp.exp(CHUNK * log_gamma)
    full = (HEADS, CHUNK, HD)
    dec = jnp.stack([intra, jnp.broadcast_to(k_decay[:, :, None], full), jnp.broadcast_to(q_decay[:, :, None], full),
                     jnp.broadcast_to(chunk_decay[:, None, None], full)], axis=1)
    return cos2, sin2, dec


def _rot(t, cos2, sin2):
    return t * cos2 + pltpu.roll(t, HD // 2, 1) * sin2


def _rot_t(t, cos2, sin2):
    return t * cos2 - pltpu.roll(t, HD // 2, 1) * sin2


def retention_fwd(name, zq, cos2, sin2, dec):
    tp = zq.shape[0]
    nch = tp // CHUNK
    scale = HD ** -0.5

    def body(q_ref, k_ref, v_ref, cos_ref, sin_ref, dec_ref, out_ref, st_ref, s_ref):
        s_ref[...] = jnp.zeros_like(s_ref)
        intra, kdec, qdec, cdec = dec_ref[0], dec_ref[1], dec_ref[2], dec_ref[3]

        def chunk(n, carry):
            rows = pl.ds(pl.multiple_of(n * CHUNK, CHUNK), CHUNK)
            cosv, sinv = cos_ref[rows, :], sin_ref[rows, :]
            qr = _rot(q_ref[rows, :], cosv, sinv) * scale
            kr = _rot(k_ref[rows, :], cosv, sinv)
            vb = v_ref[rows, :].astype(bf16)
            sc = (_dot_nt(qr.astype(bf16), kr.astype(bf16)) * intra).astype(bf16)
            sv = s_ref[...]
            sb = sv.astype(bf16)
            out_ref[rows, :] = _dot_nn(sc, vb) + _dot_nn((qr * qdec).astype(bf16), sb)
            st_ref[n] = sb
            s_ref[...] = sv * cdec + _dot_tn((kr * kdec).astype(bf16), vb)
            return carry

        lax.fori_loop(0, nch, chunk, 0)

    col = lambda off: pl.BlockSpec((tp, HD), lambda h: (0, off + h))
    return pl.pallas_call(
        body, name=name, grid=(HEADS,),
        in_specs=[col(0), col(HEADS), col(2 * HEADS), _resident((tp, HD)), _resident((tp, HD)),
                  pl.BlockSpec((None, 4, CHUNK, HD), lambda h: (h, 0, 0, 0))],
        out_specs=[col(0), pl.BlockSpec((None, nch, HD, HD), lambda h: (h, 0, 0, 0))],
        out_shape=[jax.ShapeDtypeStruct((tp, RW), f32), jax.ShapeDtypeStruct((HEADS, nch, HD, HD), bf16)],
        scratch_shapes=[pltpu.VMEM((HD, HD), f32)],
        compiler_params=_cp(48, ("arbitrary",)))(zq, zq, zq, cos2, sin2, dec)


def retention_bwd(name, zq, cos2, sin2, dec, states, dout, pad, deps=()):
    tp = zq.shape[0]
    nch = tp // CHUNK
    scale = HD ** -0.5

    def body(q_ref, k_ref, v_ref, cos_ref, sin_ref, dec_ref, st_ref, do_ref, dq_ref, dk_ref, dv_ref, g_ref):
        g_ref[...] = jnp.zeros_like(g_ref)
        intra, kdec, qdec, cdec = dec_ref[0], dec_ref[1], dec_ref[2], dec_ref[3]

        def chunk(t, carry):
            n = nch - 1 - t
            rows = pl.ds(pl.multiple_of(n * CHUNK, CHUNK), CHUNK)
            cosv, sinv = cos_ref[rows, :], sin_ref[rows, :]
            qr = _rot(q_ref[rows, :], cosv, sinv) * scale
            kr = _rot(k_ref[rows, :], cosv, sinv)
            qb, kb = qr.astype(bf16), kr.astype(bf16)
            vb = v_ref[rows, :].astype(bf16)
            qd = (qr * qdec).astype(bf16)
            kd = (kr * kdec).astype(bf16)
            sc = (_dot_nt(qb, kb) * intra).astype(bf16)
            dob = do_ref[rows, :]
            sb = st_ref[n]
            gv = g_ref[...]
            gb = gv.astype(bf16)
            dsc = (_dot_nt(dob, vb) * intra).astype(bf16)
            dv = _dot_tn(sc, dob) + _dot_nn(kd, gb)
            dqr = _dot_nn(dsc, kb) + _dot_nt(dob, sb) * qdec
            dkr = _dot_tn(dsc, qb) + _dot_nt(vb, gb) * kdec
            g_ref[...] = gv * cdec + _dot_tn(qd, dob)
            keep = (lax.broadcasted_iota(jnp.int32, (CHUNK, HD), 0) + n * CHUNK) >= pad
            dq_ref[rows, :] = jnp.where(keep, _rot_t(dqr * scale, cosv, sinv), 0.0).astype(bf16)
            dk_ref[rows, :] = jnp.where(keep, _rot_t(dkr, cosv, sinv), 0.0).astype(bf16)
            dv_ref[rows, :] = jnp.where(keep, dv, 0.0).astype(bf16)
            return carry

        lax.fori_loop(0, nch, chunk, 0)

    col = lambda off: pl.BlockSpec((tp, HD), lambda h: (0, off + h))
    return pl.pallas_call(
        _skip(len(deps), body), name=name, grid=(HEADS,),
        in_specs=[_ANY] * len(deps) + [col(0), col(HEADS), col(2 * HEADS), _resident((tp, HD)), _resident((tp, HD)),
                  pl.BlockSpec((None, 4, CHUNK, HD), lambda h: (h, 0, 0, 0)),
                  pl.BlockSpec((None, nch, HD, HD), lambda h: (h, 0, 0, 0)), col(0)],
        out_specs=[col(0)] * 3,
        out_shape=[jax.ShapeDtypeStruct((tp, RW), bf16)] * 3,
        scratch_shapes=[pltpu.VMEM((HD, HD), f32)],
        compiler_params=_cp(48, ("arbitrary",)))(*deps, zq, zq, zq, cos2, sin2, dec, states, dout)


def _window_sum(xv, steps, tp, forward):
    s = xv
    for j in range(steps):
        sh = 2 ** j
        s = s + pltpu.roll(s, (tp - sh) if forward else sh, 0)
    return s


def pool_fwd(name, zu, maps, scale, pad):
    tp = zu.shape[0]

    def body(u_ref, maps_ref, scale_ref, pooled_ref, p_ref):
        row = lax.broadcasted_iota(jnp.int32, (tp, HD), 0)
        for gi, w in enumerate(POOL_WINDOWS):
            cols = slice(gi * HD, (gi + 1) * HD)
            xv = u_ref[:, cols]
            cnt = jnp.clip(row - (pad - 1), 1, w).astype(f32)
            pooled = jnp.where(row >= pad, _window_sum(xv, gi + 1, tp, False) / cnt - xv, 0.0).astype(bf16)
            pooled_ref[:, cols] = pooled
            p_ref[:, cols] = (_dot_nn(pooled, maps_ref[gi].astype(bf16)) * scale_ref[:, cols]).astype(bf16)

    return pl.pallas_call(
        body, name=name,
        out_shape=[jax.ShapeDtypeStruct((tp, RW), bf16), jax.ShapeDtypeStruct((tp, RW), bf16)],
        compiler_params=_cp(56))(zu, maps, scale)


def pool_bwd(name, dp, pooled, maps, scale, pad):
    tp = dp.shape[0]

    def body(dp_ref, pooled_ref, maps_ref, scale_ref, du_ref, dmaps_ref, dscale_ref):
        row = lax.broadcasted_iota(jnp.int32, (tp, HD), 0)
        dscale_ref[...] = jnp.zeros_like(dscale_ref)
        for gi, w in enumerate(POOL_WINDOWS):
            cols = slice(gi * HD, (gi + 1) * HD)
            mb = maps_ref[gi].astype(bf16)
            pooled = pooled_ref[:, cols]
            dpf = dp_ref[:, cols].astype(f32)
            dscale_ref[0:1, cols] = jnp.sum(dpf * _dot_nn(pooled, mb), axis=0, keepdims=True)
            dpm = (dpf * scale_ref[:, cols]).astype(bf16)
            dmaps_ref[gi * HD:(gi + 1) * HD, :] = _dot_tn(pooled, dpm)
            dpool = jnp.where(row >= pad, _dot_nt(dpm, mb), 0.0)
            cnt = jnp.clip(row - (pad - 1), 1, w).astype(f32)
            du = _window_sum(dpool / cnt, gi + 1, tp, True) - dpool
            du_ref[:, cols] = jnp.where(row >= pad, du, 0.0).astype(bf16)

    return pl.pallas_call(
        body, name=name,
        out_shape=[jax.ShapeDtypeStruct((tp, RW), bf16), jax.ShapeDtypeStruct((HEADS * HD, HD), f32),
                   jax.ShapeDtypeStruct((8, RW), f32)],
        compiler_params=_cp(56))(dp, pooled, maps, scale)


def _group_norm(o):
    mu = jnp.mean(o, axis=-1, keepdims=True)
    oc = o - mu
    rstd = lax.rsqrt(jnp.mean(oc * oc, axis=-1, keepdims=True) + EPS)
    return oc * rstd, rstd


def mix_out_fwd(name, h, oraw, zg, zgate, p, wretT, wpoolT, wout, deps=()):
    tp, d = h.shape
    tm = _row_tile(tp)

    def body(h_ref, o_ref, zg_ref, zgate_ref, p_ref, wr_ref, wp_ref, wo_ref, ho_ref, rp_ref, ret_ref, pool_ref, mixed_ref):
        parts = []
        for hh in range(HEADS):
            cols = slice(hh * HD, (hh + 1) * HD)
            rhat, _ = _group_norm(o_ref[:, cols])
            gv = zg_ref[:, cols]
            parts.append(rhat * (gv * _sigmoid(gv)))
        r = jnp.concatenate(parts, axis=-1).astype(bf16)
        pv = p_ref[...]
        ret = _dot_nt(r, wr_ref[...])
        pool = _dot_nt(pv, wp_ref[...])
        mixed = (_sigmoid(zgate_ref[:, :d]) * ret + _sigmoid(zgate_ref[:, d:]) * pool).astype(bf16)
        ho_ref[...] = h_ref[...] + _dot_nn(mixed, wo_ref[...])
        rp_ref[0] = r
        rp_ref[1] = pv
        ret_ref[...] = ret.astype(bf16)
        pool_ref[...] = pool.astype(bf16)
        mixed_ref[...] = mixed

    row = lambda w: pl.BlockSpec((tm, w), lambda i: (i, 0))
    return pl.pallas_call(
        _skip(len(deps), body), name=name, grid=(tp // tm,),
        in_specs=[_ANY] * len(deps) + [row(d), row(RW), row(RW), row(2 * d), row(RW), _resident((d, RW)), _resident((d, RW)),
                                       _resident((d, d))],
        out_specs=[row(d), pl.BlockSpec((2, tm, RW), lambda i: (0, i, 0)), row(d), row(d),
                   pl.BlockSpec((None, tm, d), lambda i: (0, i, 0))],
        out_shape=[jax.ShapeDtypeStruct((tp, d), f32), jax.ShapeDtypeStruct((2, tp, RW), bf16),
                   jax.ShapeDtypeStruct((tp, d), bf16), jax.ShapeDtypeStruct((tp, d), bf16),
                   jax.ShapeDtypeStruct((1, tp, d), bf16)],
        compiler_params=_cp(48, ("arbitrary",)))(*deps, h, oraw, zg, zgate, p, wretT, wpoolT, wout)


def mix_out_bwd(name, dy, oraw, zg, zgate, ret, pool, wretT, wpoolT, wout, deps=()):
    tp, d = dy.shape
    tm = _row_tile(tp)

    def body(dy_ref, o_ref, zg_ref, zgate_ref, ret_ref, pool_ref, wr_ref, wp_ref, wo_ref,
             do_ref, dzg_ref, dzgate_ref, dp_ref, drp_ref, dyb_ref):
        dyb = dy_ref[...].astype(bf16)
        dmixed = _dot_nt(dyb, wo_ref[...])
        sa = _sigmoid(zgate_ref[:, :d])
        sb = _sigmoid(zgate_ref[:, d:])
        dret = dmixed * sa
        dpool = dmixed * sb
        dzgate_ref[:, :d] = (dret * ret_ref[...].astype(f32) * (1.0 - sa)).astype(bf16)
        dzgate_ref[:, d:] = (dpool * pool_ref[...].astype(f32) * (1.0 - sb)).astype(bf16)
        dretb, dpoolb = dret.astype(bf16), dpool.astype(bf16)
        dr = _dot_nn(dretb, wr_ref[...])
        dp_ref[...] = _dot_nn(dpoolb, wp_ref[...]).astype(bf16)
        for hh in range(HEADS):
            cols = slice(hh * HD, (hh + 1) * HD)
            rhat, rstd = _group_norm(o_ref[:, cols])
            gv = zg_ref[:, cols]
            sg = _sigmoid(gv)
            drh = dr[:, cols]
            drhat = drh * (gv * sg)
            dzg_ref[:, cols] = (drh * rhat * (sg * (1.0 + gv * (1.0 - sg)))).astype(bf16)
            do = rstd * (drhat - jnp.mean(drhat, axis=-1, keepdims=True)
                         - rhat * jnp.mean(drhat * rhat, axis=-1, keepdims=True))
            do_ref[:, cols] = do.astype(bf16)
        drp_ref[0] = dretb
        drp_ref[1] = dpoolb
        dyb_ref[...] = dyb

    row = lambda w: pl.BlockSpec((tm, w), lambda i: (i, 0))
    return pl.pallas_call(
        _skip(len(deps), body), name=name, grid=(tp // tm,),
        in_specs=[_ANY] * len(deps) + [row(d), row(RW), row(RW), row(2 * d), row(d), row(d), _resident((d, RW)), _resident((d, RW)),
                  _resident((d, d))],
        out_specs=[row(RW), row(RW), row(2 * d), row(RW), pl.BlockSpec((2, tm, d), lambda i: (0, i, 0)),
                   pl.BlockSpec((None, tm, d), lambda i: (0, i, 0))],
        out_shape=[jax.ShapeDtypeStruct((tp, RW), bf16), jax.ShapeDtypeStruct((tp, RW), bf16),
                   jax.ShapeDtypeStruct((tp, 2 * d), bf16), jax.ShapeDtypeStruct((tp, RW), bf16),
                   jax.ShapeDtypeStruct((2, tp, d), bf16), jax.ShapeDtypeStruct((1, tp, d), bf16)],
        compiler_params=_cp(48, ("arbitrary",)))(*deps, dy, oraw, zg, zgate, ret, pool, wretT, wpoolT, wout)


def final_loss(name, h, g, target):
    tp, d = h.shape
    tm = CHUNK

    def body(h_ref, g_ref, t_ref, dh_ref, loss_ref, dg_ref):
        i = pl.program_id(0)
        hhat, rs = _rms(h_ref[...])
        gv = g_ref[...]
        err = jnp.where(i >= 1, hhat * gv - t_ref[...], 0.0)
        dyv = err / d
        dh_ref[...] = _rms_bwd(dyv, gv, hhat, rs)

        @pl.when(i == 0)
        def _():
            loss_ref[...] = jnp.zeros_like(loss_ref)
            dg_ref[...] = jnp.zeros_like(dg_ref)

        loss_ref[...] += 0.5 * jnp.sum(jnp.sum(err * err, axis=-1, keepdims=True) / d)
        dg_ref[0:1, :] += jnp.sum(dyv * hhat, axis=0, keepdims=True)

    return pl.pallas_call(
        body, name=name, grid=(tp // tm,),
        in_specs=[pl.BlockSpec((tm, d), lambda i: (i, 0)), _resident((1, d)),
                  pl.BlockSpec((tm, d), lambda i: (jnp.maximum(i - 1, 0), 0))],
        out_specs=[pl.BlockSpec((tm, d), lambda i: (i, 0)), pl.BlockSpec((8, 128), lambda i: (0, 0)),
                   pl.BlockSpec((8, d), lambda i: (0, 0))],
        out_shape=[jax.ShapeDtypeStruct((tp, d), f32), jax.ShapeDtypeStruct((8, 128), f32),
                   jax.ShapeDtypeStruct((8, d), f32)],
        compiler_params=_cp(32, ("arbitrary",)))(h, g, target)


def _adamw(w, g, m, v):
    m = ADAM_B1 * m + (1.0 - ADAM_B1) * g
    v = ADAM_B2 * v + (1.0 - ADAM_B2) * (g * g)
    m_hat = m / (1.0 - ADAM_B1 ** ADAM_STEP)
    v_hat = v / (1.0 - ADAM_B2 ** ADAM_STEP)
    delta = -ADAM_LR * (m_hat / (jnp.sqrt(v_hat) + ADAM_EPS) + ADAM_WD * w)
    return delta, m, v


def adam_big(name, me, recv, own, b, layer, transposed, w, m, v, prev):
    r, c = recv.shape[2:]
    wshape = w.shape[1:]

    def body(me_ref, recv_ref, own_ref, w_ref, m_ref, v_ref, *rest):
        g_ref, d_ref, nm_ref, nv_ref = rest[-4:]
        g = own_ref[...].astype(f32)
        for j in range(NDEV - 1):
            g = g + recv_ref[j].astype(f32)
        if transposed:
            g = g.T
        delta, nm, nv = _adamw(w_ref[...], g, m_ref[...], v_ref[...])
        g_ref[...] = g
        d_ref[...] = delta
        nm_ref[...] = nm
        nv_ref[...] = nv

    wspec = pl.BlockSpec((None,) + wshape, lambda i, me_ref: (layer, 0, 0))
    in_specs = [pl.BlockSpec((NDEV - 1, None, r, c), lambda i, me_ref: (0, b, 0, 0)),
                pl.BlockSpec((None, None, r, c), lambda i, me_ref: (b, me_ref[0], 0, 0)), wspec, wspec, wspec]
    args = [recv, own, w, m, v]
    aliases = {}
    if prev is not None:
        in_specs += [_ANY] * 4
        args += list(prev)
        aliases = {6 + k: k for k in range(4)}
    return pl.pallas_call(
        body, name=name,
        grid_spec=pltpu.PrefetchScalarGridSpec(num_scalar_prefetch=1, grid=(1,), in_specs=in_specs, out_specs=[wspec] * 4),
        out_shape=[jax.ShapeDtypeStruct(w.shape, f32)] * 4, input_output_aliases=aliases,
        compiler_params=_cp(56))(me, *args)


def adam_small(name, ga0, gmaps0, gmeta, ga1, gmaps1, norms, pool_scale, pool_maps, meta, final_norm, d):
    def body(ga0_ref, gmaps0_ref, gmeta_ref, ga1_ref, gmaps1_ref, *refs):
        ins, outs = refs[:21], refs[21:]
        x, y, c = _me()
        me = 4 * x + 2 * y + c

        def total(ref, rows):
            t = ref[0, rows, :]
            for j in range(1, NDEV):
                t = t + ref[j, rows, :]
            return t

        row = lambda r: slice(r, r + 1)
        outs[0][...] = jnp.broadcast_to(total(ga1_ref, row(0))[:, :128], (8, 128))

        def update(k, g, o):
            w_ref, m_ref, v_ref = ins[3 * k:3 * k + 3]
            delta, nm, nv = _adamw(w_ref[...], g, m_ref[...], v_ref[...])
            for ref, val in zip(outs[o:o + 4], (g, delta, nm, nv)):
                ref[...] = val

        two = lax.broadcasted_iota(jnp.int32, (2, d), 0)
        for k in range(3):
            update(k, jnp.where(two == 0, total(ga0_ref, row(k)), total(ga1_ref, row(2 + k))), 1 + 4 * k)
        update(3, jnp.where(two[:, :RW] == 0, total(ga0_ref, row(3))[:, :RW], total(ga1_ref, row(5))[:, :RW]), 13)
        update(4, jnp.concatenate([total(gmaps0_ref, slice(None)), total(gmaps1_ref, slice(None))], axis=0), 17)
        update(5, total(gmeta_ref, pl.ds(pl.multiple_of(me * N_META, N_META), N_META)), 21)
        update(6, total(ga1_ref, row(1)), 25)

    flat = []
    for trip in (*norms, pool_scale, pool_maps, meta, final_norm):
        flat += list(trip)
    out_shapes = [jax.ShapeDtypeStruct((8, 128), f32)]
    for trip in (*norms, pool_scale, pool_maps, meta, final_norm):
        out_shapes += [jax.ShapeDtypeStruct(trip[0].shape, f32)] * 4
    return pl.pallas_call(body, name=name, out_shape=out_shapes,
                          compiler_params=_cp(32))(ga0, gmaps0, gmeta, ga1, gmaps1, *flat)


def kernel(x, meta, ffn1_norm, ffn1_gate, ffn1_up, ffn1_down, mix_norm, w_in, pool_maps, pool_scale, w_ret_up, w_pool_up, w_out, ffn2_norm, ffn2_gate, ffn2_up, ffn2_down, final_norm, loss_target, m_meta, m_ffn1_norm, m_ffn1_gate, m_ffn1_up, m_ffn1_down, m_mix_norm, m_w_in, m_pool_maps, m_pool_scale, m_w_ret_up, m_w_pool_up, m_w_out, m_ffn2_norm, m_ffn2_gate, m_ffn2_up, m_ffn2_down, m_final_norm, v_meta, v_ffn1_norm, v_ffn1_gate, v_ffn1_up, v_ffn1_down, v_mix_norm, v_w_in, v_pool_maps, v_pool_scale, v_w_ret_up, v_w_pool_up, v_w_out, v_ffn2_norm, v_ffn2_gate, v_ffn2_up, v_ffn2_down, v_final_norm):
    seq, d = x.shape[1], x.shape[2]
    depth = ffn1_gate.shape[0]
    ff = ffn1_gate.shape[2] * NDEV
    nin = w_in.shape[2] * NDEV
    length = seq + N_META
    pad = (-length) % CHUNK
    tp = length + pad
    assert pad % 8 == 0 and pad + N_META == CHUNK and depth == 2 and nin == 5 * RW + 2 * d

    ix, iy, ic = _me()
    me = (4 * ix + 2 * iy + ic).astype(jnp.int32).reshape(1)

    meta_all, = all_gather("gather_meta", [meta])
    meta_full = jnp.transpose(meta_all, (1, 0, 2)).reshape(N_META, d)

    gathers = {}
    token = meta_all
    for layer in range(depth):
        lands = prep_layer(layer, me, [ffn1_gate, ffn1_up, ffn2_gate, ffn2_up, w_in, w_ret_up, w_pool_up],
                           [ffn1_down, ffn2_down, w_out])
        g1T, u1T, g2T, u2T, winT, wretT, wpoolT, d1, d2, wout = lands
        for part, group in (("ffn1", [g1T, u1T, d1]), ("mix", [winT, wretT, wpoolT, wout]), ("ffn2", [g2T, u2T, d2])):
            ssem, rsem, group, token = gather_start_chips(f"gather_start_{part}{layer}", group, (token,))
            gathers[(part, layer)] = (ssem, rsem, group)

    def forward(part, layer, after):
        ssem, rsem, group = gathers[(part, layer)]
        ssem, rsem, group, tok = gather_forward(f"gather_forward_{part}{layer}", ssem, rsem, group, after)
        gathers[(part, layer)] = (ssem, rsem, group)
        return tok

    def gathered(part, layer, after):
        ssem, rsem, group = gathers[(part, layer)]
        _, full = copies_wait(f"gather_wait_{part}{layer}", ssem, rsem, (), group, after, 3)
        return [a.reshape((NDEV * a.shape[1],) + a.shape[2:]) for a in full]

    cos2, sin2, dec = _retention_tables(tp, pad)
    h = jnp.concatenate([jnp.zeros((pad, d), f32), meta_full, x[0]], axis=0)

    saved = []
    weights = []
    tok = forward("ffn1", 0, token)
    for layer in range(depth):
        row = lambda a: a[layer:layer + 1]
        s = {"h0": h}
        g1T, u1T, d1 = gathered("ffn1", layer, tok if layer == 0 else h)
        tok = forward("mix", layer, h) if layer else None
        h, s["a1"], s["b1"] = ffn_fwd(f"ffn1_fwd{layer}", h, row(ffn1_norm), g1T, u1T, d1, (tok,) if layer else ())
        s["h1"] = h
        if layer == 0:
            tok = forward("mix", layer, h)
        winT, wretT, wpoolT, wout = gathered("mix", layer, tok if layer == 0 else h)
        s["zq"], s["zg"], zu, s["zgate"] = mix_in_fwd(f"mix_in_fwd{layer}", h, row(mix_norm), winT)
        s["oraw"], s["states"] = retention_fwd(f"retention_fwd{layer}", s["zq"], cos2, sin2, dec)
        s["pooled"], p = pool_fwd(f"pool_fwd{layer}", zu, pool_maps[layer], row(pool_scale), pad)
        tok = forward("ffn2", layer, p)
        h, s["rp"], s["ret"], s["pool"], s["mixed"] = mix_out_fwd(
            f"mix_out_fwd{layer}", h, s["oraw"], s["zg"], s["zgate"], p, wretT, wpoolT, wout, (tok,))
        s["h2"] = h
        g2T, u2T, d2 = gathered("ffn2", layer, h)
        tok = (forward("ffn1", layer + 1, h),) if layer + 1 < depth else ()
        h, s["a2"], s["b2"] = ffn_fwd(f"ffn2_fwd{layer}", h, row(ffn2_norm), g2T, u2T, d2, tok)
        saved.append(s)
        weights.append((g1T, u1T, g2T, u2T, winT, wretT, wpoolT, d1, d2, wout))

    dh, loss_part, dg_final = final_loss("final_loss", h, final_norm.reshape(1, d), loss_target[0])

    small = {}
    small_gathers = {}
    exchanges = {}
    token = None

    def rows8(vals):
        at = lax.broadcasted_iota(jnp.int32, (8, d), 0)
        out = jnp.zeros((8, d), f32)
        for k, v in enumerate(vals):
            r0 = v[0:1]
            r0 = r0 if r0.shape[1] == d else jnp.pad(r0, ((0, 0), (0, d - r0.shape[1])))
            out = jnp.where(at == k, r0, out)
        return out

    def exchange(part, layer, grads):
        by_dest = [g.reshape(g.shape[0], NDEV, g.shape[1] // NDEV, g.shape[2]) for g in grads]
        ssem, rsem, sent, lands, tok = exchange_start(f"exchange_start_{part}{layer}", by_dest)
        exchanges[(part, layer)] = (ssem, rsem, sent, lands)
        return (tok,)

    for layer in reversed(range(depth)):
        g1T, u1T, g2T, u2T, winT, wretT, wpoolT, d1, d2, wout = weights[layer]
        row = lambda a: a[layer:layer + 1]
        s = saved[layer]
        dh, lhs2, rhs2, small[("ffn2", layer)] = ffn_bwd(
            f"ffn2_bwd{layer}", dh, s["h2"], row(ffn2_norm), s["a2"], s["b2"], g2T, u2T, d2, () if token is None else token)
        token = exchange("ffn2", layer, [mm_tn(f"ffn2_wgrad{layer}", lhs2, rhs2, lambda b: b // 2)])
        do, dzg, dzgate, dp, drp, dyb = mix_out_bwd(
            f"mix_out_bwd{layer}", dh, s["oraw"], s["zg"], s["zgate"], s["ret"], s["pool"], wretT, wpoolT, wout, token)
        token = exchange("mix", layer, [mm_tn(f"w_out_wgrad{layer}", s["mixed"], dyb, lambda b: b),
                                        mm_tn(f"up_wgrad{layer}", drp, s["rp"], lambda b: b)])
        dq, dk, dv = retention_bwd(f"retention_bwd{layer}", s["zq"], cos2, sin2, dec, s["states"], do, pad, token)
        dzu, small[("maps", layer)], small[("scale", layer)] = pool_bwd(
            f"pool_bwd{layer}", dp, s["pooled"], pool_maps[layer], row(pool_scale), pad)
        dh, dz, n2, small[("mix", layer)] = mix_in_bwd(
            f"mix_in_bwd{layer}", dq, dk, dv, dzg, dzu, dzgate, s["h1"], row(mix_norm), winT, dh)
        token = exchange("w_in", layer, [mm_tn(f"w_in_wgrad{layer}", dz, n2, lambda b: b)])
        dh, lhs1, rhs1, small[("ffn1", layer)] = ffn_bwd(
            f"ffn1_bwd{layer}", dh, s["h0"], row(ffn1_norm), s["a1"], s["b1"], g1T, u1T, d1, token)
        rows = [small[("ffn1", layer)], small[("mix", layer)], small[("ffn2", layer)], small[("scale", layer)]]
        packs = [rows8([loss_part, dg_final] + rows if layer == depth - 1 else rows), small[("maps", layer)]]
        if layer == 0:
            dmeta = dh[pad:CHUNK]
            packs.append(jnp.transpose(dmeta.reshape(N_META, NDEV, d // NDEV), (1, 0, 2)).reshape(NDEV * N_META, d // NDEV))
        ssem, rsem, lands, tok = gather_start(f"small_start{layer}", slot_in(f"small_slot{layer}", me, packs))
        small_gathers[layer] = (ssem, rsem, lands)
        token = exchange("ffn1", layer, [mm_tn(f"ffn1_wgrad{layer}", lhs1, rhs1, lambda b: b // 2, (tok,))])

    grad_x = dh[CHUNK:][None]

    big = {}
    after = dh
    plans = {
        "ffn2": [("ffn2_gate", 0, 0, True, ffn2_gate, m_ffn2_gate, v_ffn2_gate),
                 ("ffn2_up", 0, 1, True, ffn2_up, m_ffn2_up, v_ffn2_up),
                 ("ffn2_down", 0, 2, False, ffn2_down, m_ffn2_down, v_ffn2_down)],
        "mix": [("w_out", 0, 0, False, w_out, m_w_out, v_w_out),
                ("w_ret_up", 1, 0, True, w_ret_up, m_w_ret_up, v_w_ret_up),
                ("w_pool_up", 1, 1, True, w_pool_up, m_w_pool_up, v_w_pool_up)],
        "w_in": [("w_in", 0, 0, True, w_in, m_w_in, v_w_in)],
        "ffn1": [("ffn1_gate", 0, 0, True, ffn1_gate, m_ffn1_gate, v_ffn1_gate),
                 ("ffn1_up", 0, 1, True, ffn1_up, m_ffn1_up, v_ffn1_up),
                 ("ffn1_down", 0, 2, False, ffn1_down, m_ffn1_down, v_ffn1_down)]}
    for layer in reversed(range(depth)):
        for part in ("ffn2", "mix", "w_in", "ffn1"):
            ssem, rsem, sent, lands = exchanges[(part, layer)]
            sent, lands = copies_wait(f"exchange_wait_{part}{layer}", ssem, rsem, sent, lands, after)
            for nm, k, b, tr, w, m, v in plans[part]:
                big[nm] = adam_big(f"adam_{nm}{layer}", me, lands[k], sent[k], b, layer, tr, w, m, v, big.get(nm))
                after = big[nm][0]

    gsmall = []
    for layer in range(depth):
        ssem, rsem, lands = small_gathers[layer]
        gsmall += copies_wait(f"small_wait{layer}", ssem, rsem, (), lands, after)[1]

    maps2 = lambda a: a.reshape(depth * HEADS * HD, HD)
    res = adam_small(
        "adam_small", *gsmall,
        [(ffn1_norm, m_ffn1_norm, v_ffn1_norm), (mix_norm, m_mix_norm, v_mix_norm), (ffn2_norm, m_ffn2_norm, v_ffn2_norm)],
        (pool_scale, m_pool_scale, v_pool_scale), (maps2(pool_maps), maps2(m_pool_maps), maps2(v_pool_maps)),
        (meta, m_meta, v_meta), tuple(a.reshape(1, d) for a in (final_norm, m_final_norm, v_final_norm)), d)
    loss = res[0][0, 0]
    sm = {}
    for k, nm in enumerate(["ffn1_norm", "mix_norm", "ffn2_norm", "pool_scale", "pool_maps", "meta", "final_norm"]):
        sm[nm] = list(res[1 + 4 * k:5 + 4 * k])
    sm["pool_maps"] = [a.reshape(pool_maps.shape) for a in sm["pool_maps"]]
    sm["final_norm"] = [a.reshape(d) for a in sm["final_norm"]]

    names = ["meta", "ffn1_norm", "ffn1_gate", "ffn1_up", "ffn1_down", "mix_norm", "w_in", "pool_maps", "pool_scale",
             "w_ret_up", "w_pool_up", "w_out", "ffn2_norm", "ffn2_gate", "ffn2_up", "ffn2_down", "final_norm"]
    allw = {**{k: list(v) for k, v in big.items()}, **sm}
    outs = [loss, grad_x]
    for kind in range(4):
        outs += [allw[nm][kind] for nm in names]
    return tuple(outs)
```

```python
import functools

import jax
import jax.numpy as jnp
from jax import lax
from jax.experimental import pallas as pl
from jax.experimental.pallas import tpu as pltpu

f32 = jnp.float32
bf16 = jnp.bfloat16
MESH = pl.DeviceIdType.MESH
NDEV = 8
N_META = 16
HEADS = 4
HD = 128
CHUNK = 128
RW = HEADS * HD
POOL_WINDOWS = (2, 4, 8, 16)
ROPE_BASE = 10000.0
EPS = 1e-6
ADAM_LR = 0.001
ADAM_B1 = 0.9
ADAM_B2 = 0.999
ADAM_EPS = 1e-08
ADAM_WD = 0.01
ADAM_STEP = 10
VMEM_CAP_MB = 60


def _cp(vmem_mb, sem=None):
    return pltpu.CompilerParams(vmem_limit_bytes=min(vmem_mb, VMEM_CAP_MB) * 2**20, dimension_semantics=sem)


def _row_tile(tp, want=384):
    return want if tp % want == 0 else 128


def _resident(shape):
    nd = len(shape)
    return pl.BlockSpec(shape, lambda *_: (0,) * nd, pipeline_mode=pl.Buffered(1))


def _skip(nd, body):
    return (lambda *refs: body(*refs[nd:])) if nd else body


def _dot_nn(a, b):
    return lax.dot_general(a, b, (((1,), (0,)), ((), ())), preferred_element_type=f32)


def _dot_nt(a, b):
    return lax.dot_general(a, b, (((1,), (1,)), ((), ())), preferred_element_type=f32)


def _dot_tn(a, b):
    return lax.dot_general(a, b, (((0,), (0,)), ((), ())), preferred_element_type=f32)


def _rms(h):
    rs = lax.rsqrt(jnp.mean(h * h, axis=-1, keepdims=True) + EPS)
    return h * rs, rs


def _rms_bwd(dn, g, hhat, rs):
    dhh = dn * g
    return rs * (dhh - hhat * jnp.mean(dhh * hhat, axis=-1, keepdims=True))


def _sigmoid(x):
    return jax.nn.sigmoid(x)


def _me():
    return lax.axis_index("x"), lax.axis_index("y"), lax.axis_index("c")


def _peer(idx):
    return (idx // 4, (idx // 2) % 2, idx % 2)


def all_gather(name, arrays):
    n = len(arrays)

    def body(*refs):
        ins, outs = refs[:n], refs[n:2 * n]
        send_sems, recv_sems, local_sems = refs[2 * n:]
        x, y, c = _me()
        me = 4 * x + 2 * y + c
        locals_ = []
        for k in range(n):
            cp = pltpu.make_async_copy(ins[k], outs[k].at[me], local_sems.at[k])
            cp.start()
            locals_.append(cp)
        for d in range(1, NDEV):
            for k in range(n):
                pltpu.make_async_remote_copy(
                    src_ref=ins[k], dst_ref=outs[k].at[me], send_sem=send_sems.at[k], recv_sem=recv_sems.at[k],
                    device_id=_peer((me + d) % NDEV), device_id_type=MESH).start()
        for k in range(n):
            seven = outs[k].at[pl.ds(0, NDEV - 1)]
            w = pltpu.make_async_remote_copy(src_ref=seven, dst_ref=seven, send_sem=send_sems.at[k],
                                             recv_sem=recv_sems.at[k], device_id=(x, y, c), device_id_type=MESH)
            w.wait_send()
            w.wait_recv()
            locals_[k].wait()

    anyspec = pl.BlockSpec(memory_space=pl.ANY)
    return pl.pallas_call(
        body, name=name,
        out_shape=[jax.ShapeDtypeStruct((NDEV,) + a.shape, a.dtype) for a in arrays],
        in_specs=[anyspec] * n, out_specs=[anyspec] * n,
        scratch_shapes=[pltpu.SemaphoreType.DMA((n,)), pltpu.SemaphoreType.DMA((n,)), pltpu.SemaphoreType.DMA((n,))],
    )(*arrays)


_HBM = pl.BlockSpec(memory_space=pltpu.HBM)
_SEM = pl.BlockSpec(memory_space=pltpu.SEMAPHORE)
_ANY = pl.BlockSpec(memory_space=pl.ANY)
_EFFECT = pltpu.SideEffectType.DATAFLOW_SIDE_EFFECTING


def _in_hbm(a):
    return pltpu.with_memory_space_constraint(a, pltpu.HBM)


def gather_start(name, lands, deps=()):
    n, nd = len(lands), len(deps)

    def body(*refs):
        land = refs[nd:nd + n]
        send_sems, recv_sems = refs[nd + n:nd + n + 2]
        token = refs[-1]
        x, y, c = _me()
        me = 4 * x + 2 * y + c
        for d in range(1, NDEV):
            for k in range(n):
                pltpu.make_async_remote_copy(
                    src_ref=land[k].at[me], dst_ref=land[k].at[me], send_sem=send_sems.at[k], recv_sem=recv_sems.at[k],
                    device_id=_peer((me + d) % NDEV), device_id_type=MESH).start()
        token[...] = jnp.zeros_like(token)

    res = pl.pallas_call(
        body, name=name,
        out_shape=(pltpu.SemaphoreType.DMA((n,)), pltpu.SemaphoreType.DMA((n,)),
                   *[pltpu.HBM(a.shape, a.dtype) for a in lands], jax.ShapeDtypeStruct((8, 128), f32)),
        in_specs=[_ANY] * nd + [_HBM] * n,
        out_specs=(_SEM, _SEM, *[_HBM] * n, pl.BlockSpec(memory_space=pltpu.VMEM)),
        input_output_aliases={nd + k: 2 + k for k in range(n)},
        compiler_params=pltpu.CompilerParams(has_side_effects=_EFFECT),
    )(*deps, *[_in_hbm(a) for a in lands])
    return res[0], res[1], list(res[2:2 + n]), res[-1]


def _other_chips(x, y):
    return [(1 - x, y), (x, 1 - y), (1 - x, 1 - y)]


def gather_start_chips(name, lands, deps=()):
    n, nd = len(lands), len(deps)

    def body(*refs):
        land = refs[nd:nd + n]
        send_sems, recv_sems = refs[nd + n:nd + n + 2]
        token = refs[-1]
        x, y, c = _me()
        me = 4 * x + 2 * y + c
        for k in range(n):
            for to in [(x, y, 1 - c)] + [(cx, cy, c) for cx, cy in _other_chips(x, y)]:
                pltpu.make_async_remote_copy(
                    src_ref=land[k].at[me], dst_ref=land[k].at[me], send_sem=send_sems.at[k], recv_sem=recv_sems.at[k],
                    device_id=to, device_id_type=MESH).start()
        token[...] = jnp.zeros_like(token)

    res = pl.pallas_call(
        body, name=name,
        out_shape=(pltpu.SemaphoreType.DMA((n,)), pltpu.SemaphoreType.DMA((n,)),
                   *[pltpu.HBM(a.shape, a.dtype) for a in lands], jax.ShapeDtypeStruct((8, 128), f32)),
        in_specs=[_ANY] * nd + [_HBM] * n,
        out_specs=(_SEM, _SEM, *[_HBM] * n, pl.BlockSpec(memory_space=pltpu.VMEM)),
        input_output_aliases={nd + k: 2 + k for k in range(n)},
        compiler_params=pltpu.CompilerParams(has_side_effects=_EFFECT),
    )(*deps, *[_in_hbm(a) for a in lands])
    return res[0], res[1], list(res[2:2 + n]), res[-1]


def gather_forward(name, send_sems, recv_sems, lands, after):
    n = len(lands)

    def body(*refs):
        land = refs[:n]
        ssem, rsem = refs[n:n + 2]
        send2, recv2 = refs[n + 3:n + 5]
        token = refs[-1]
        x, y, c = _me()
        for k in range(n):
            four = land[k].at[pl.ds(0, 4)]
            w = pltpu.make_async_remote_copy(src_ref=four, dst_ref=four, send_sem=ssem.at[k], recv_sem=rsem.at[k],
                                             device_id=(x, y, c), device_id_type=MESH)
            w.wait_send()
            w.wait_recv()
            for cx, cy in _other_chips(x, y):
                slot = 4 * cx + 2 * cy + c
                pltpu.make_async_remote_copy(
                    src_ref=land[k].at[slot], dst_ref=land[k].at[slot], send_sem=send2.at[k], recv_sem=recv2.at[k],
                    device_id=(x, y, 1 - c), device_id_type=MESH).start()
        token[...] = jnp.zeros_like(token)

    res = pl.pallas_call(
        body, name=name,
        out_shape=(pltpu.SemaphoreType.DMA((n,)), pltpu.SemaphoreType.DMA((n,)),
                   *[pltpu.HBM(a.shape, a.dtype) for a in lands], jax.ShapeDtypeStruct((8, 128), f32)),
        in_specs=[_HBM] * n + [_SEM, _SEM, _ANY],
        out_specs=(_SEM, _SEM, *[_HBM] * n, pl.BlockSpec(memory_space=pltpu.VMEM)),
        input_output_aliases={k: 2 + k for k in range(n)},
        compiler_params=pltpu.CompilerParams(has_side_effects=_EFFECT),
    )(*lands, send_sems, recv_sems, after)
    return res[0], res[1], list(res[2:2 + n]), res[-1]


def exchange_start(name, grads, deps=()):
    n, nd = len(grads), len(deps)
    lands = [lax.empty((NDEV - 1, g.shape[0]) + g.shape[2:], g.dtype) for g in grads]

    def body(*refs):
        src = refs[nd:nd + n]
        land = refs[nd + n:nd + 2 * n]
        send_sems, recv_sems = refs[nd + 2 * n:nd + 2 * n + 2]
        token = refs[-1]
        x, y, c = _me()
        me = 4 * x + 2 * y + c
        for d in range(1, NDEV):
            p = (me + d) % NDEV
            for k in range(n):
                pltpu.make_async_remote_copy(
                    src_ref=src[k].at[:, p], dst_ref=land[k].at[d - 1], send_sem=send_sems.at[k], recv_sem=recv_sems.at[k],
                    device_id=_peer(p), device_id_type=MESH).start()
        token[...] = jnp.zeros_like(token)

    both = list(grads) + lands
    res = pl.pallas_call(
        body, name=name,
        out_shape=(pltpu.SemaphoreType.DMA((n,)), pltpu.SemaphoreType.DMA((n,)),
                   *[pltpu.HBM(a.shape, a.dtype) for a in both], jax.ShapeDtypeStruct((8, 128), f32)),
        in_specs=[_ANY] * nd + [_HBM] * (2 * n),
        out_specs=(_SEM, _SEM, *[_HBM] * (2 * n), pl.BlockSpec(memory_space=pltpu.VMEM)),
        input_output_aliases={nd + k: 2 + k for k in range(2 * n)},
        compiler_params=pltpu.CompilerParams(has_side_effects=_EFFECT),
    )(*deps, *[_in_hbm(a) for a in both])
    return res[0], res[1], list(res[2:2 + n]), list(res[2 + n:2 + 2 * n]), res[-1]


def copies_wait(name, send_sems, recv_sems, sent, lands, after, count=NDEV - 1):
    ns, n = len(sent), len(lands)

    def body(*refs):
        land = refs[ns:ns + n]
        ssem, rsem = refs[ns + n:ns + n + 2]
        x, y, c = _me()
        for k in range(n):
            seven = land[k].at[pl.ds(0, count)]
            w = pltpu.make_async_remote_copy(src_ref=seven, dst_ref=seven, send_sem=ssem.at[k], recv_sem=rsem.at[k],
                                             device_id=(x, y, c), device_id_type=MESH)
            w.wait_send()
            w.wait_recv()

    both = list(sent) + list(lands)
    res = pl.pallas_call(
        body, name=name, out_shape=tuple(pltpu.HBM(a.shape, a.dtype) for a in both),
        in_specs=[_HBM] * (ns + n) + [_SEM, _SEM, _ANY], out_specs=tuple([_HBM] * (ns + n)),
        input_output_aliases={k: k for k in range(ns + n)},
        compiler_params=pltpu.CompilerParams(has_side_effects=_EFFECT),
    )(*both, send_sems, recv_sems, after)
    return list(res[:ns]), list(res[ns:])


def prep_layer(layer, me, col_sharded, row_sharded):
    nc, nr = len(col_sharded), len(row_sharded)

    def body(me_ref, *refs):
        ins, outs = refs[:nc + nr], refs[nc + nr:]
        for k in range(nc):
            outs[k][...] = ins[k][...].T.astype(bf16)
        for k in range(nc, nc + nr):
            outs[k][...] = ins[k][...].astype(bf16)

    arrs = list(col_sharded) + list(row_sharded)
    in_specs = [pl.BlockSpec((None,) + a.shape[1:], lambda i, me_ref: (layer, 0, 0)) for a in arrs]
    shapes = [(a.shape[2], a.shape[1]) for a in col_sharded] + [a.shape[1:] for a in row_sharded]
    out_specs = [pl.BlockSpec((None,) + s, lambda i, me_ref: (me_ref[0], 0, 0)) for s in shapes]
    return pl.pallas_call(
        body, name=f"prep_layer{layer}",
        grid_spec=pltpu.PrefetchScalarGridSpec(num_scalar_prefetch=1, grid=(1,), in_specs=in_specs, out_specs=out_specs),
        out_shape=[jax.ShapeDtypeStruct((NDEV,) + s, bf16) for s in shapes], compiler_params=_cp(48))(me, *arrs)


def slot_in(name, me, arrays):
    n = len(arrays)

    def body(me_ref, *refs):
        for k in range(n):
            refs[n + k][...] = refs[k][...]

    in_specs = [pl.BlockSpec(a.shape, lambda i, me_ref: (0, 0)) for a in arrays]
    out_specs = [pl.BlockSpec((None,) + a.shape, lambda i, me_ref: (me_ref[0], 0, 0)) for a in arrays]
    return pl.pallas_call(
        body, name=name,
        grid_spec=pltpu.PrefetchScalarGridSpec(num_scalar_prefetch=1, grid=(1,), in_specs=in_specs, out_specs=out_specs),
        out_shape=[jax.ShapeDtypeStruct((NDEV,) + a.shape, a.dtype) for a in arrays])(me, *arrays)


def ffn_fwd(name, h, g, wgT, wuT, wd, deps=()):
    tp, d = h.shape
    ff = wgT.shape[0]
    tm = _row_tile(tp)

    def body(h_ref, g_ref, wg_ref, wu_ref, wd_ref, ho_ref, a_ref, b_ref):
        hh = h_ref[...]
        hhat, _ = _rms(hh)
        n = (hhat * g_ref[...]).astype(bf16)
        a = _dot_nt(n, wg_ref[...])
        b = _dot_nt(n, wu_ref[...])
        s = (a * _sigmoid(a)) * b
        ho_ref[...] = hh + 0.5 * _dot_nn(s.astype(bf16), wd_ref[...])
        a_ref[...] = a.astype(bf16)
        b_ref[...] = b.astype(bf16)

    row = lambda w: pl.BlockSpec((tm, w), lambda i: (i, 0))
    return pl.pallas_call(
        _skip(len(deps), body), name=name, grid=(tp // tm,),
        in_specs=[_ANY] * len(deps) + [row(d), _resident((1, d)), _resident((ff, d)), _resident((ff, d)), _resident((ff, d))],
        out_specs=[row(d), row(ff), row(ff)],
        out_shape=[jax.ShapeDtypeStruct((tp, d), f32), jax.ShapeDtypeStruct((tp, ff), bf16),
                   jax.ShapeDtypeStruct((tp, ff), bf16)],
        compiler_params=_cp(56, ("arbitrary",)))(*deps, h, g, wgT, wuT, wd)


def ffn_bwd(name, dy, h, g, a, b, wgT, wuT, wd, deps=()):
    tp, d = h.shape
    ff = wgT.shape[0]
    tm = _row_tile(tp, 192)

    def body(dy_ref, h_ref, g_ref, a_ref, b_ref, wg_ref, wu_ref, wd_ref, dh_ref, lhs_ref, rhs_ref, dg_ref):
        dyv = dy_ref[...]
        hhat, rs = _rms(h_ref[...])
        gv = g_ref[...]
        n = hhat * gv
        dyh = (0.5 * dyv).astype(bf16)
        ds = _dot_nt(dyh, wd_ref[...])
        av = a_ref[...].astype(f32)
        bv = b_ref[...].astype(f32)
        sg = _sigmoid(av)
        sa = av * sg
        da = (ds * bv * (sg * (1.0 + av * (1.0 - sg)))).astype(bf16)
        db = (ds * sa).astype(bf16)
        dn = _dot_nn(da, wg_ref[...]) + _dot_nn(db, wu_ref[...])
        dh_ref[...] = dyv + _rms_bwd(dn, gv, hhat, rs)

        @pl.when(pl.program_id(0) == 0)
        def _():
            dg_ref[...] = jnp.zeros_like(dg_ref)

        dg_ref[0:1, :] += jnp.sum(dn * hhat, axis=0, keepdims=True)
        lhs_ref[0] = da
        lhs_ref[1] = db
        lhs_ref[2] = (sa * bv).astype(bf16)
        rhs_ref[0] = n.astype(bf16)
        rhs_ref[1] = dyh

    row = lambda w: pl.BlockSpec((tm, w), lambda i: (i, 0))
    return pl.pallas_call(
        _skip(len(deps), body), name=name, grid=(tp // tm,),
        in_specs=[_ANY] * len(deps) + [row(d), row(d), _resident((1, d)), row(ff), row(ff),
                  _resident((ff, d)), _resident((ff, d)), _resident((ff, d))],
        out_specs=[row(d), pl.BlockSpec((3, tm, ff), lambda i: (0, i, 0)), pl.BlockSpec((2, tm, d), lambda i: (0, i, 0)),
                   pl.BlockSpec((8, d), lambda i: (0, 0))],
        out_shape=[jax.ShapeDtypeStruct((tp, d), f32), jax.ShapeDtypeStruct((3, tp, ff), bf16),
                   jax.ShapeDtypeStruct((2, tp, d), bf16), jax.ShapeDtypeStruct((8, d), f32)],
        compiler_params=_cp(58, ("arbitrary",)))(*deps, dy, h, g, a, b, wgT, wuT, wd)


def mm_tn(name, lhs, rhs, rhs_of, deps=()):
    nb, tp, m = lhs.shape
    n = rhs.shape[2]
    fixed = m * n * (4 + 2 * 2)
    tk = next((t for t in (1408, 704, 384) if tp % t == 0 and fixed + 2 * t * (m + n) * 2 <= 46 * 2**20), 128)
    nk = tp // tk

    def body(l_ref, r_ref, o_ref, acc_ref):
        k = pl.program_id(1)

        @pl.when(k == 0)
        def _():
            acc_ref[...] = jnp.zeros_like(acc_ref)

        acc_ref[...] += _dot_tn(l_ref[...], r_ref[...])

        @pl.when(k == nk - 1)
        def _():
            o_ref[...] = acc_ref[...].astype(o_ref.dtype)

    return pl.pallas_call(
        _skip(len(deps), body), name=name, grid=(nb, nk),
        in_specs=[_ANY] * len(deps) + [pl.BlockSpec((None, tk, m), lambda b, k: (b, k, 0)),
                                       pl.BlockSpec((None, tk, n), lambda b, k: (rhs_of(b), k, 0))],
        out_specs=pl.BlockSpec((None, m, n), lambda b, k: (b, 0, 0)),
        out_shape=jax.ShapeDtypeStruct((nb, m, n), bf16),
        scratch_shapes=[pltpu.VMEM((m, n), f32)],
        compiler_params=_cp(56, ("arbitrary", "arbitrary")))(*deps, lhs, rhs)


def mix_in_fwd(name, h, g, winT):
    tp, d = h.shape
    nin = winT.shape[0]
    tm = _row_tile(tp)

    def body(h_ref, g_ref, w_ref, zq_ref, zg_ref, zu_ref, zgate_ref):
        hhat, _ = _rms(h_ref[...])
        z = _dot_nt((hhat * g_ref[...]).astype(bf16), w_ref[...])
        zq_ref[...] = z[:, :3 * RW]
        zg_ref[...] = z[:, 3 * RW:4 * RW]
        zu_ref[...] = z[:, 4 * RW:5 * RW]
        zgate_ref[...] = z[:, 5 * RW:]

    row = lambda w: pl.BlockSpec((tm, w), lambda i: (i, 0))
    widths = (3 * RW, RW, RW, 2 * d)
    return pl.pallas_call(
        body, name=name, grid=(tp // tm,),
        in_specs=[row(d), _resident((1, d)), _resident((nin, d))],
        out_specs=[row(w) for w in widths],
        out_shape=[jax.ShapeDtypeStruct((tp, w), f32) for w in widths],
        compiler_params=_cp(56, ("arbitrary",)))(h, g, winT)


def mix_in_bwd(name, dq, dk, dv, dzg, dzu, dzgate, h, g, winT, dres):
    tp, d = h.shape
    nin = winT.shape[0]
    tm = _row_tile(tp)

    def body(dq_ref, dk_ref, dv_ref, dzg_ref, dzu_ref, dzgate_ref, h_ref, g_ref, w_ref, dres_ref, dh_ref, dz_ref, n_ref, dg_ref):
        dz = jnp.concatenate([dq_ref[...], dk_ref[...], dv_ref[...], dzg_ref[...], dzu_ref[...], dzgate_ref[...]], axis=-1)
        dn = _dot_nn(dz, w_ref[...])
        hhat, rs = _rms(h_ref[...])
        gv = g_ref[...]
        dh_ref[...] = dres_ref[...] + _rms_bwd(dn, gv, hhat, rs)

        @pl.when(pl.program_id(0) == 0)
        def _():
            dg_ref[...] = jnp.zeros_like(dg_ref)

        dg_ref[0:1, :] += jnp.sum(dn * hhat, axis=0, keepdims=True)
        dz_ref[...] = dz
        n_ref[...] = (hhat * gv).astype(bf16)

    row = lambda w: pl.BlockSpec((tm, w), lambda i: (i, 0))
    return pl.pallas_call(
        body, name=name, grid=(tp // tm,),
        in_specs=[row(RW)] * 5 + [row(2 * d), row(d), _resident((1, d)), _resident((nin, d)), row(d)],
        out_specs=[row(d), pl.BlockSpec((None, tm, nin), lambda i: (0, i, 0)), pl.BlockSpec((None, tm, d), lambda i: (0, i, 0)),
                   pl.BlockSpec((8, d), lambda i: (0, 0))],
        out_shape=[jax.ShapeDtypeStruct((tp, d), f32), jax.ShapeDtypeStruct((1, tp, nin), bf16),
                   jax.ShapeDtypeStruct((1, tp, d), bf16), jax.ShapeDtypeStruct((8, d), f32)],
        compiler_params=_cp(56, ("arbitrary",)))(dq, dk, dv, dzg, dzu, dzgate, h, g, winT, dres)


def _retention_tables(tp, pad):
    half = HD // 2
    inv_freq = ROPE_BASE ** (-jnp.arange(half, dtype=f32) / half)
    pos = jnp.arange(tp, dtype=f32) - pad
    ang = pos[:, None] * inv_freq[None, :]
    cos, sin = jnp.cos(ang), jnp.sin(ang)
    cos2 = jnp.concatenate([cos, cos], axis=-1)
    sin2 = jnp.concatenate([-sin, sin], axis=-1)
    log_gamma = jnp.log1p(-(2.0 ** (-5.0 - jnp.arange(HEADS, dtype=f32))))
    idx = jnp.arange(CHUNK, dtype=f32)
    diff = idx[:, None] - idx[None, :]
    intra = jnp.where(diff[None] >= 0, jnp.exp(diff[None] * log_gamma[:, None, None]), 0.0)
    k_decay = jnp.exp((CHUNK - 1.0 - idx)[None, :] * log_gamma[:, None])
    q_decay = jnp.exp((idx + 1.0)[None, :] * log_gamma[:, None])
    chunk_decay = jnp.exp(CHUNK * log_gamma)
    full = (HEADS, CHUNK, HD)
    dec = jnp.stack([intra, jnp.broadcast_to(k_decay[:, :, None], full), jnp.broadcast_to(q_decay[:, :, None], full),
                     jnp.broadcast_to(chunk_decay[:, None, None], full)], axis=1)
    return cos2, sin2, dec


def _rot(t, cos2, sin2):
    return t * cos2 + pltpu.roll(t, HD // 2, 1) * sin2


def _rot_t(t, cos2, sin2):
    return t * cos2 - pltpu.roll(t, HD // 2, 1) * sin2


def retention_fwd(name, zq, cos2, sin2, dec):
    tp = zq.shape[0]
    nch = tp // CHUNK
    scale = HD ** -0.5

    def body(q_ref, k_ref, v_ref, cos_ref, sin_ref, dec_ref, out_ref, st_ref, s_ref):
        @pl.when(pl.program_id(0) == 0)
        def _():
            s_ref[...] = jnp.zeros_like(s_ref)

        cosv, sinv = cos_ref[...], sin_ref[...]
        for hh in range(HEADS):
            cols = slice(hh * HD, (hh + 1) * HD)
            qr = _rot(q_ref[:, cols], cosv, sinv) * scale
            kr = _rot(k_ref[:, cols], cosv, sinv)
            vb = v_ref[:, cols].astype(bf16)
            sc = (_dot_nt(qr.astype(bf16), kr.astype(bf16)) * dec_ref[hh, 0]).astype(bf16)
            sv = s_ref[hh]
            sb = sv.astype(bf16)
            out_ref[:, cols] = _dot_nn(sc, vb) + _dot_nn((qr * dec_ref[hh, 2]).astype(bf16), sb)
            st_ref[hh, 0] = sb
            s_ref[hh] = sv * dec_ref[hh, 3] + _dot_tn((kr * dec_ref[hh, 1]).astype(bf16), vb)

    part = lambda j: pl.BlockSpec((CHUNK, RW), lambda n: (n, j))
    return pl.pallas_call(
        body, name=name, grid=(nch,),
        in_specs=[part(0), part(1), part(2), pl.BlockSpec((CHUNK, HD), lambda n: (n, 0)),
                  pl.BlockSpec((CHUNK, HD), lambda n: (n, 0)), _resident((HEADS, 4, CHUNK, HD))],
        out_specs=[part(0), pl.BlockSpec((HEADS, 1, HD, HD), lambda n: (0, n, 0, 0))],
        out_shape=[jax.ShapeDtypeStruct((tp, RW), f32), jax.ShapeDtypeStruct((HEADS, nch, HD, HD), bf16)],
        scratch_shapes=[pltpu.VMEM((HEADS, HD, HD), f32)],
        compiler_params=_cp(32, ("arbitrary",)))(zq, zq, zq, cos2, sin2, dec)


def retention_bwd(name, zq, cos2, sin2, dec, states, dout, pad, deps=()):
    tp = zq.shape[0]
    nch = tp // CHUNK
    scale = HD ** -0.5

    def body(q_ref, k_ref, v_ref, cos_ref, sin_ref, dec_ref, st_ref, do_ref, dq_ref, dk_ref, dv_ref, g_ref):
        @pl.when(pl.program_id(0) == 0)
        def _():
            g_ref[...] = jnp.zeros_like(g_ref)

        n = nch - 1 - pl.program_id(0)
        cosv, sinv = cos_ref[...], sin_ref[...]
        keep = (lax.broadcasted_iota(jnp.int32, (CHUNK, HD), 0) + n * CHUNK) >= pad
        for hh in range(HEADS):
            cols = slice(hh * HD, (hh + 1) * HD)
            intra, kdec, qdec = dec_ref[hh, 0], dec_ref[hh, 1], dec_ref[hh, 2]
            qr = _rot(q_ref[:, cols], cosv, sinv) * scale
            kr = _rot(k_ref[:, cols], cosv, sinv)
            qb, kb = qr.astype(bf16), kr.astype(bf16)
            vb = v_ref[:, cols].astype(bf16)
            qd = (qr * qdec).astype(bf16)
            kd = (kr * kdec).astype(bf16)
            sc = (_dot_nt(qb, kb) * intra).astype(bf16)
            dob = do_ref[:, cols]
            sb = st_ref[hh, 0]
            gv = g_ref[hh]
            gb = gv.astype(bf16)
            dsc = (_dot_nt(dob, vb) * intra).astype(bf16)
            dv = _dot_tn(sc, dob) + _dot_nn(kd, gb)
            dqr = _dot_nn(dsc, kb) + _dot_nt(dob, sb) * qdec
            dkr = _dot_tn(dsc, qb) + _dot_nt(vb, gb) * kdec
            g_ref[hh] = gv * dec_ref[hh, 3] + _dot_tn(qd, dob)
            dq_ref[:, cols] = jnp.where(keep, _rot_t(dqr * scale, cosv, sinv), 0.0).astype(bf16)
            dk_ref[:, cols] = jnp.where(keep, _rot_t(dkr, cosv, sinv), 0.0).astype(bf16)
            dv_ref[:, cols] = jnp.where(keep, dv, 0.0).astype(bf16)

    part = lambda j: pl.BlockSpec((CHUNK, RW), lambda t: (nch - 1 - t, j))
    table = pl.BlockSpec((CHUNK, HD), lambda t: (nch - 1 - t, 0))
    return pl.pallas_call(
        _skip(len(deps), body), name=name, grid=(nch,),
        in_specs=[_ANY] * len(deps) + [part(0), part(1), part(2), table, table, _resident((HEADS, 4, CHUNK, HD)),
                                       pl.BlockSpec((HEADS, 1, HD, HD), lambda t: (0, nch - 1 - t, 0, 0)), part(0)],
        out_specs=[part(0)] * 3,
        out_shape=[jax.ShapeDtypeStruct((tp, RW), bf16)] * 3,
        scratch_shapes=[pltpu.VMEM((HEADS, HD, HD), f32)],
        compiler_params=_cp(32, ("arbitrary",)))(*deps, zq, zq, zq, cos2, sin2, dec, states, dout)


def _window_sum(xv, steps, tp, forward):
    s = xv
    for j in range(steps):
        sh = 2 ** j
        s = s + pltpu.roll(s, (tp - sh) if forward else sh, 0)
    return s


def pool_fwd(name, zu, maps, scale, pad):
    tp = zu.shape[0]

    def body(u_ref, maps_ref, scale_ref, pooled_ref, p_ref):
        row = lax.broadcasted_iota(jnp.int32, (tp, HD), 0)
        for gi, w in enumerate(POOL_WINDOWS):
            cols = slice(gi * HD, (gi + 1) * HD)
            xv = u_ref[:, cols]
            cnt = jnp.clip(row - (pad - 1), 1, w).astype(f32)
            pooled = jnp.where(row >= pad, _window_sum(xv, gi + 1, tp, False) / cnt - xv, 0.0).astype(bf16)
            pooled_ref[:, cols] = pooled
            p_ref[:, cols] = (_dot_nn(pooled, maps_ref[gi].astype(bf16)) * scale_ref[:, cols]).astype(bf16)

    return pl.pallas_call(
        body, name=name,
        out_shape=[jax.ShapeDtypeStruct((tp, RW), bf16), jax.ShapeDtypeStruct((tp, RW), bf16)],
        compiler_params=_cp(56))(zu, maps, scale)


def pool_bwd(name, dp, pooled, maps, scale, pad):
    tp = dp.shape[0]

    def body(dp_ref, pooled_ref, maps_ref, scale_ref, du_ref, dmaps_ref, dscale_ref):
        row = lax.broadcasted_iota(jnp.int32, (tp, HD), 0)
        dscale_ref[...] = jnp.zeros_like(dscale_ref)
        for gi, w in enumerate(POOL_WINDOWS):
            cols = slice(gi * HD, (gi + 1) * HD)
            mb = maps_ref[gi].astype(bf16)
            pooled = pooled_ref[:, cols]
            dpf = dp_ref[:, cols].astype(f32)
            dscale_ref[0:1, cols] = jnp.sum(dpf * _dot_nn(pooled, mb), axis=0, keepdims=True)
            dpm = (dpf * scale_ref[:, cols]).astype(bf16)
            dmaps_ref[gi * HD:(gi + 1) * HD, :] = _dot_tn(pooled, dpm)
            dpool = jnp.where(row >= pad, _dot_nt(dpm, mb), 0.0)
            cnt = jnp.clip(row - (pad - 1), 1, w).astype(f32)
            du = _window_sum(dpool / cnt, gi + 1, tp, True) - dpool
            du_ref[:, cols] = jnp.where(row >= pad, du, 0.0).astype(bf16)

    return pl.pallas_call(
        body, name=name,
        out_shape=[jax.ShapeDtypeStruct((tp, RW), bf16), jax.ShapeDtypeStruct((HEADS * HD, HD), f32),
                   jax.ShapeDtypeStruct((8, RW), f32)],
        compiler_params=_cp(56))(dp, pooled, maps, scale)


def _group_norm(o):
    mu = jnp.mean(o, axis=-1, keepdims=True)
    oc = o - mu
    rstd = lax.rsqrt(jnp.mean(oc * oc, axis=-1, keepdims=True) + EPS)
    return oc * rstd, rstd


def mix_out_fwd(name, h, oraw, zg, zgate, p, wretT, wpoolT, wout, deps=()):
    tp, d = h.shape
    tm = _row_tile(tp)

    def body(h_ref, o_ref, zg_ref, zgate_ref, p_ref, wr_ref, wp_ref, wo_ref, ho_ref, rp_ref, ret_ref, pool_ref, mixed_ref):
        parts = []
        for hh in range(HEADS):
            cols = slice(hh * HD, (hh + 1) * HD)
            rhat, _ = _group_norm(o_ref[:, cols])
            gv = zg_ref[:, cols]
            parts.append(rhat * (gv * _sigmoid(gv)))
        r = jnp.concatenate(parts, axis=-1).astype(bf16)
        pv = p_ref[...]
        ret = _dot_nt(r, wr_ref[...])
        pool = _dot_nt(pv, wp_ref[...])
        mixed = (_sigmoid(zgate_ref[:, :d]) * ret + _sigmoid(zgate_ref[:, d:]) * pool).astype(bf16)
        ho_ref[...] = h_ref[...] + _dot_nn(mixed, wo_ref[...])
        rp_ref[0] = r
        rp_ref[1] = pv
        ret_ref[...] = ret.astype(bf16)
        pool_ref[...] = pool.astype(bf16)
        mixed_ref[...] = mixed

    row = lambda w: pl.BlockSpec((tm, w), lambda i: (i, 0))
    return pl.pallas_call(
        _skip(len(deps), body), name=name, grid=(tp // tm,),
        in_specs=[_ANY] * len(deps) + [row(d), row(RW), row(RW), row(2 * d), row(RW), _resident((d, RW)), _resident((d, RW)),
                                       _resident((d, d))],
        out_specs=[row(d), pl.BlockSpec((2, tm, RW), lambda i: (0, i, 0)), row(d), row(d),
                   pl.BlockSpec((None, tm, d), lambda i: (0, i, 0))],
        out_shape=[jax.ShapeDtypeStruct((tp, d), f32), jax.ShapeDtypeStruct((2, tp, RW), bf16),
                   jax.ShapeDtypeStruct((tp, d), bf16), jax.ShapeDtypeStruct((tp, d), bf16),
                   jax.ShapeDtypeStruct((1, tp, d), bf16)],
        compiler_params=_cp(48, ("arbitrary",)))(*deps, h, oraw, zg, zgate, p, wretT, wpoolT, wout)


def mix_out_bwd(name, dy, oraw, zg, zgate, ret, pool, wretT, wpoolT, wout, deps=()):
    tp, d = dy.shape
    tm = _row_tile(tp)

    def body(dy_ref, o_ref, zg_ref, zgate_ref, ret_ref, pool_ref, wr_ref, wp_ref, wo_ref,
             do_ref, dzg_ref, dzgate_ref, dp_ref, drp_ref, dyb_ref):
        dyb = dy_ref[...].astype(bf16)
        dmixed = _dot_nt(dyb, wo_ref[...])
        sa = _sigmoid(zgate_ref[:, :d])
        sb = _sigmoid(zgate_ref[:, d:])
        dret = dmixed * sa
        dpool = dmixed * sb
        dzgate_ref[:, :d] = (dret * ret_ref[...].astype(f32) * (1.0 - sa)).astype(bf16)
        dzgate_ref[:, d:] = (dpool * pool_ref[...].astype(f32) * (1.0 - sb)).astype(bf16)
        dretb, dpoolb = dret.astype(bf16), dpool.astype(bf16)
        dr = _dot_nn(dretb, wr_ref[...])
        dp_ref[...] = _dot_nn(dpoolb, wp_ref[...]).astype(bf16)
        for hh in range(HEADS):
            cols = slice(hh * HD, (hh + 1) * HD)
            rhat, rstd = _group_norm(o_ref[:, cols])
            gv = zg_ref[:, cols]
            sg = _sigmoid(gv)
            drh = dr[:, cols]
            drhat = drh * (gv * sg)
            dzg_ref[:, cols] = (drh * rhat * (sg * (1.0 + gv * (1.0 - sg)))).astype(bf16)
            do = rstd * (drhat - jnp.mean(drhat, axis=-1, keepdims=True)
                         - rhat * jnp.mean(drhat * rhat, axis=-1, keepdims=True))
            do_ref[:, cols] = do.astype(bf16)
        drp_ref[0] = dretb
        drp_ref[1] = dpoolb
        dyb_ref[...] = dyb

    row = lambda w: pl.BlockSpec((tm, w), lambda i: (i, 0))
    return pl.pallas_call(
        _skip(len(deps), body), name=name, grid=(tp // tm,),
        in_specs=[_ANY] * len(deps) + [row(d), row(RW), row(RW), row(2 * d), row(d), row(d), _resident((d, RW)), _resident((d, RW)),
                  _resident((d, d))],
        out_specs=[row(RW), row(RW), row(2 * d), row(RW), pl.BlockSpec((2, tm, d), lambda i: (0, i, 0)),
                   pl.BlockSpec((None, tm, d), lambda i: (0, i, 0))],
        out_shape=[jax.ShapeDtypeStruct((tp, RW), bf16), jax.ShapeDtypeStruct((tp, RW), bf16),
                   jax.ShapeDtypeStruct((tp, 2 * d), bf16), jax.ShapeDtypeStruct((tp, RW), bf16),
                   jax.ShapeDtypeStruct((2, tp, d), bf16), jax.ShapeDtypeStruct((1, tp, d), bf16)],
        compiler_params=_cp(48, ("arbitrary",)))(*deps, dy, oraw, zg, zgate, ret, pool, wretT, wpoolT, wout)


def final_loss(name, h, g, target):
    tp, d = h.shape
    tm = _row_tile(tp)
    nsub = tm // CHUNK

    def body(h_ref, g_ref, *rest):
        t_refs = rest[:nsub]
        dh_ref, loss_ref, dg_ref = rest[nsub:]
        i = pl.program_id(0)

        @pl.when(i == 0)
        def _():
            loss_ref[...] = jnp.zeros_like(loss_ref)
            dg_ref[...] = jnp.zeros_like(dg_ref)

        gv = g_ref[...]
        for j in range(nsub):
            rows = slice(j * CHUNK, (j + 1) * CHUNK)
            hhat, rs = _rms(h_ref[rows, :])
            err = jnp.where(i * nsub + j >= 1, hhat * gv - t_refs[j][...], 0.0)
            dyv = err / d
            dh_ref[rows, :] = _rms_bwd(dyv, gv, hhat, rs)
            loss_ref[...] += 0.5 * jnp.sum(jnp.sum(err * err, axis=-1, keepdims=True) / d)
            dg_ref[0:1, :] += jnp.sum(dyv * hhat, axis=0, keepdims=True)

    lagged = lambda j: pl.BlockSpec((CHUNK, d), lambda i: (jnp.maximum(i * nsub + j - 1, 0), 0))
    return pl.pallas_call(
        body, name=name, grid=(tp // tm,),
        in_specs=[pl.BlockSpec((tm, d), lambda i: (i, 0)), _resident((1, d))] + [lagged(j) for j in range(nsub)],
        out_specs=[pl.BlockSpec((tm, d), lambda i: (i, 0)), pl.BlockSpec((8, 128), lambda i: (0, 0)),
                   pl.BlockSpec((8, d), lambda i: (0, 0))],
        out_shape=[jax.ShapeDtypeStruct((tp, d), f32), jax.ShapeDtypeStruct((8, 128), f32),
                   jax.ShapeDtypeStruct((8, d), f32)],
        compiler_params=_cp(32, ("arbitrary",)))(h, g, *[target] * nsub)


def _adamw(w, g, m, v):
    m = ADAM_B1 * m + (1.0 - ADAM_B1) * g
    v = ADAM_B2 * v + (1.0 - ADAM_B2) * (g * g)
    m_hat = m / (1.0 - ADAM_B1 ** ADAM_STEP)
    v_hat = v / (1.0 - ADAM_B2 ** ADAM_STEP)
    delta = -ADAM_LR * (m_hat / (jnp.sqrt(v_hat) + ADAM_EPS) + ADAM_WD * w)
    return delta, m, v


def adam_big(name, me, recv, own, b, layer, transposed, w, m, v, prev):
    r, c = recv.shape[2:]
    wshape = w.shape[1:]
    nchunk = 1 if transposed else next(k for k in (4, 2, 1) if r % (16 * k) == 0)
    rc = r // nchunk

    def body(me_ref, recv_ref, own_ref, w_ref, m_ref, v_ref, *rest):
        g_ref, d_ref, nm_ref, nv_ref = rest[-4:]
        g = own_ref[...].astype(f32)
        for j in range(NDEV - 1):
            g = g + recv_ref[j].astype(f32)
        if transposed:
            g = g.T
        delta, nm, nv = _adamw(w_ref[...], g, m_ref[...], v_ref[...])
        g_ref[...] = g
        d_ref[...] = delta
        nm_ref[...] = nm
        nv_ref[...] = nv

    wblock = wshape if transposed else (rc, c)
    wspec = pl.BlockSpec((None,) + wblock, lambda i, me_ref: (layer, i, 0))
    in_specs = [pl.BlockSpec((NDEV - 1, None, rc, c), lambda i, me_ref: (0, b, i, 0)),
                pl.BlockSpec((None, None, rc, c), lambda i, me_ref: (b, me_ref[0], i, 0)), wspec, wspec, wspec]
    args = [recv, own, w, m, v]
    aliases = {}
    if prev is not None:
        in_specs += [_ANY] * 4
        args += list(prev)
        aliases = {6 + k: k for k in range(4)}
    return pl.pallas_call(
        body, name=name,
        grid_spec=pltpu.PrefetchScalarGridSpec(num_scalar_prefetch=1, grid=(nchunk,), in_specs=in_specs,
                                               out_specs=[wspec] * 4),
        out_shape=[jax.ShapeDtypeStruct(w.shape, f32)] * 4, input_output_aliases=aliases,
        compiler_params=_cp(56))(me, *args)


def adam_small(name, ga0, gmaps0, gmeta, ga1, gmaps1, norms, pool_scale, pool_maps, meta, final_norm, d):
    def body(ga0_ref, gmaps0_ref, gmeta_ref, ga1_ref, gmaps1_ref, *refs):
        ins, outs = refs[:21], refs[21:]
        x, y, c = _me()
        me = 4 * x + 2 * y + c

        def total(ref, rows):
            t = ref[0, rows, :]
            for j in range(1, NDEV):
                t = t + ref[j, rows, :]
            return t

        row = lambda r: slice(r, r + 1)
        outs[0][...] = jnp.broadcast_to(total(ga1_ref, row(0))[:, :128], (8, 128))

        def update(k, g, o):
            w_ref, m_ref, v_ref = ins[3 * k:3 * k + 3]
            delta, nm, nv = _adamw(w_ref[...], g, m_ref[...], v_ref[...])
            for ref, val in zip(outs[o:o + 4], (g, delta, nm, nv)):
                ref[...] = val

        two = lax.broadcasted_iota(jnp.int32, (2, d), 0)
        for k in range(3):
            update(k, jnp.where(two == 0, total(ga0_ref, row(k)), total(ga1_ref, row(2 + k))), 1 + 4 * k)
        update(3, jnp.where(two[:, :RW] == 0, total(ga0_ref, row(3))[:, :RW], total(ga1_ref, row(5))[:, :RW]), 13)
        update(4, jnp.concatenate([total(gmaps0_ref, slice(None)), total(gmaps1_ref, slice(None))], axis=0), 17)
        update(5, total(gmeta_ref, pl.ds(pl.multiple_of(me * N_META, N_META), N_META)), 21)
        update(6, total(ga1_ref, row(1)), 25)

    flat = []
    for trip in (*norms, pool_scale, pool_maps, meta, final_norm):
        flat += list(trip)
    out_shapes = [jax.ShapeDtypeStruct((8, 128), f32)]
    for trip in (*norms, pool_scale, pool_maps, meta, final_norm):
        out_shapes += [jax.ShapeDtypeStruct(trip[0].shape, f32)] * 4
    return pl.pallas_call(body, name=name, out_shape=out_shapes,
                          compiler_params=_cp(32))(ga0, gmaps0, gmeta, ga1, gmaps1, *flat)


def kernel(x, meta, ffn1_norm, ffn1_gate, ffn1_up, ffn1_down, mix_norm, w_in, pool_maps, pool_scale, w_ret_up, w_pool_up, w_out, ffn2_norm, ffn2_gate, ffn2_up, ffn2_down, final_norm, loss_target, m_meta, m_ffn1_norm, m_ffn1_gate, m_ffn1_up, m_ffn1_down, m_mix_norm, m_w_in, m_pool_maps, m_pool_scale, m_w_ret_up, m_w_pool_up, m_w_out, m_ffn2_norm, m_ffn2_gate, m_ffn2_up, m_ffn2_down, m_final_norm, v_meta, v_ffn1_norm, v_ffn1_gate, v_ffn1_up, v_ffn1_down, v_mix_norm, v_w_in, v_pool_maps, v_pool_scale, v_w_ret_up, v_w_pool_up, v_w_out, v_ffn2_norm, v_ffn2_gate, v_ffn2_up, v_ffn2_down, v_final_norm):
    seq, d = x.shape[1], x.shape[2]
    depth = ffn1_gate.shape[0]
    ff = ffn1_gate.shape[2] * NDEV
    nin = w_in.shape[2] * NDEV
    length = seq + N_META
    pad = (-length) % CHUNK
    tp = length + pad
    assert pad % 8 == 0 and pad + N_META == CHUNK and depth == 2 and nin == 5 * RW + 2 * d

    ix, iy, ic = _me()
    me = (4 * ix + 2 * iy + ic).astype(jnp.int32).reshape(1)

    meta_all, = all_gather("gather_meta", [meta])
    meta_full = jnp.transpose(meta_all, (1, 0, 2)).reshape(N_META, d)

    gathers = {}
    token = meta_all
    tview = lambda *arrs: [jnp.swapaxes(a, 1, 2) for a in arrs]
    t_g1, t_u1, t_g2, t_u2, t_in = (tview(w, m, v) for w, m, v in (
        (ffn1_gate, m_ffn1_gate, v_ffn1_gate), (ffn1_up, m_ffn1_up, v_ffn1_up), (ffn2_gate, m_ffn2_gate, v_ffn2_gate),
        (ffn2_up, m_ffn2_up, v_ffn2_up), (w_in, m_w_in, v_w_in)))
    for layer in range(depth):
        lands = prep_layer(layer, me, [w_ret_up, w_pool_up],
                           [t_g1[0], t_u1[0], t_g2[0], t_u2[0], t_in[0], ffn1_down, ffn2_down, w_out])
        wretT, wpoolT, g1T, u1T, g2T, u2T, winT, d1, d2, wout = lands
        for part, group in (("ffn1", [g1T, u1T, d1]), ("mix", [winT, wretT, wpoolT, wout]), ("ffn2", [g2T, u2T, d2])):
            ssem, rsem, group, token = gather_start_chips(f"gather_start_{part}{layer}", group, (token,))
            gathers[(part, layer)] = (ssem, rsem, group)

    def forward(part, layer, after):
        ssem, rsem, group = gathers[(part, layer)]
        ssem, rsem, group, tok = gather_forward(f"gather_forward_{part}{layer}", ssem, rsem, group, after)
        gathers[(part, layer)] = (ssem, rsem, group)
        return tok

    def gathered(part, layer, after):
        ssem, rsem, group = gathers[(part, layer)]
        _, full = copies_wait(f"gather_wait_{part}{layer}", ssem, rsem, (), group, after, 3)
        return [a.reshape((NDEV * a.shape[1],) + a.shape[2:]) for a in full]

    cos2, sin2, dec = _retention_tables(tp, pad)
    h = jnp.concatenate([jnp.zeros((pad, d), f32), meta_full, x[0]], axis=0)

    saved = []
    weights = []
    tok = forward("ffn1", 0, token)
    for layer in range(depth):
        row = lambda a: a[layer:layer + 1]
        s = {"h0": h}
        g1T, u1T, d1 = gathered("ffn1", layer, tok if layer == 0 else h)
        tok = forward("mix", layer, h) if layer else None
        h, s["a1"], s["b1"] = ffn_fwd(f"ffn1_fwd{layer}", h, row(ffn1_norm), g1T, u1T, d1, (tok,) if layer else ())
        s["h1"] = h
        if layer == 0:
            tok = forward("mix", layer, h)
        winT, wretT, wpoolT, wout = gathered("mix", layer, tok if layer == 0 else h)
        s["zq"], s["zg"], zu, s["zgate"] = mix_in_fwd(f"mix_in_fwd{layer}", h, row(mix_norm), winT)
        s["oraw"], s["states"] = retention_fwd(f"retention_fwd{layer}", s["zq"], cos2, sin2, dec)
        s["pooled"], p = pool_fwd(f"pool_fwd{layer}", zu, pool_maps[layer], row(pool_scale), pad)
        tok = forward("ffn2", layer, p)
        h, s["rp"], s["ret"], s["pool"], s["mixed"] = mix_out_fwd(
            f"mix_out_fwd{layer}", h, s["oraw"], s["zg"], s["zgate"], p, wretT, wpoolT, wout, (tok,))
        s["h2"] = h
        g2T, u2T, d2 = gathered("ffn2", layer, h)
        tok = (forward("ffn1", layer + 1, h),) if layer + 1 < depth else ()
        h, s["a2"], s["b2"] = ffn_fwd(f"ffn2_fwd{layer}", h, row(ffn2_norm), g2T, u2T, d2, tok)
        saved.append(s)
        weights.append((g1T, u1T, g2T, u2T, winT, wretT, wpoolT, d1, d2, wout))

    dh, loss_part, dg_final = final_loss("final_loss", h, final_norm.reshape(1, d), loss_target[0])

    small = {}
    small_gathers = {}
    exchanges = {}
    token = None

    def rows8(vals):
        at = lax.broadcasted_iota(jnp.int32, (8, d), 0)
        out = jnp.zeros((8, d), f32)
        for k, v in enumerate(vals):
            r0 = v[0:1]
            r0 = r0 if r0.shape[1] == d else jnp.pad(r0, ((0, 0), (0, d - r0.shape[1])))
            out = jnp.where(at == k, r0, out)
        return out

    def exchange(part, layer, grads):
        by_dest = [g.reshape(g.shape[0], NDEV, g.shape[1] // NDEV, g.shape[2]) for g in grads]
        ssem, rsem, sent, lands, tok = exchange_start(f"exchange_start_{part}{layer}", by_dest)
        exchanges[(part, layer)] = (ssem, rsem, sent, lands)
        return (tok,)

    for layer in reversed(range(depth)):
        g1T, u1T, g2T, u2T, winT, wretT, wpoolT, d1, d2, wout = weights[layer]
        row = lambda a: a[layer:layer + 1]
        s = saved[layer]
        dh, lhs2, rhs2, small[("ffn2", layer)] = ffn_bwd(
            f"ffn2_bwd{layer}", dh, s["h2"], row(ffn2_norm), s["a2"], s["b2"], g2T, u2T, d2, () if token is None else token)
        token = exchange("ffn2", layer, [mm_tn(f"ffn2_wgrad{layer}", lhs2, rhs2, lambda b: b // 2)])
        do, dzg, dzgate, dp, drp, dyb = mix_out_bwd(
            f"mix_out_bwd{layer}", dh, s["oraw"], s["zg"], s["zgate"], s["ret"], s["pool"], wretT, wpoolT, wout, token)
        token = exchange("mix", layer, [mm_tn(f"w_out_wgrad{layer}", s["mixed"], dyb, lambda b: b),
                                        mm_tn(f"up_wgrad{layer}", drp, s["rp"], lambda b: b)])
        dq, dk, dv = retention_bwd(f"retention_bwd{layer}", s["zq"], cos2, sin2, dec, s["states"], do, pad, token)
        dzu, small[("maps", layer)], small[("scale", layer)] = pool_bwd(
            f"pool_bwd{layer}", dp, s["pooled"], pool_maps[layer], row(pool_scale), pad)
        dh, dz, n2, small[("mix", layer)] = mix_in_bwd(
            f"mix_in_bwd{layer}", dq, dk, dv, dzg, dzu, dzgate, s["h1"], row(mix_norm), winT, dh)
        token = exchange("w_in", layer, [mm_tn(f"w_in_wgrad{layer}", dz, n2, lambda b: b)])
        dh, lhs1, rhs1, small[("ffn1", layer)] = ffn_bwd(
            f"ffn1_bwd{layer}", dh, s["h0"], row(ffn1_norm), s["a1"], s["b1"], g1T, u1T, d1, token)
        rows = [small[("ffn1", layer)], small[("mix", layer)], small[("ffn2", layer)], small[("scale", layer)]]
        packs = [rows8([loss_part, dg_final] + rows if layer == depth - 1 else rows), small[("maps", layer)]]
        if layer == 0:
            dmeta = dh[pad:CHUNK]
            packs.append(jnp.transpose(dmeta.reshape(N_META, NDEV, d // NDEV), (1, 0, 2)).reshape(NDEV * N_META, d // NDEV))
        ssem, rsem, lands, tok = gather_start(f"small_start{layer}", slot_in(f"small_slot{layer}", me, packs))
        small_gathers[layer] = (ssem, rsem, lands)
        token = exchange("ffn1", layer, [mm_tn(f"ffn1_wgrad{layer}", lhs1, rhs1, lambda b: b // 2, (tok,))])

    grad_x = dh[CHUNK:][None]

    big = {}
    after = token[0]
    plans = {
        "ffn2": [("ffn2_gate", 0, 0, False, *t_g2), ("ffn2_up", 0, 1, False, *t_u2),
                 ("ffn2_down", 0, 2, False, ffn2_down, m_ffn2_down, v_ffn2_down)],
        "mix": [("w_out", 0, 0, False, w_out, m_w_out, v_w_out),
                ("w_ret_up", 1, 0, True, w_ret_up, m_w_ret_up, v_w_ret_up),
                ("w_pool_up", 1, 1, True, w_pool_up, m_w_pool_up, v_w_pool_up)],
        "w_in": [("w_in", 0, 0, False, *t_in)],
        "ffn1": [("ffn1_gate", 0, 0, False, *t_g1), ("ffn1_up", 0, 1, False, *t_u1),
                 ("ffn1_down", 0, 2, False, ffn1_down, m_ffn1_down, v_ffn1_down)]}
    for layer in reversed(range(depth)):
        for part in ("ffn2", "mix", "w_in", "ffn1"):
            ssem, rsem, sent, lands = exchanges[(part, layer)]
            sent, lands = copies_wait(f"exchange_wait_{part}{layer}", ssem, rsem, sent, lands, after)
            for nm, k, b, tr, w, m, v in plans[part]:
                big[nm] = adam_big(f"adam_{nm}{layer}", me, lands[k], sent[k], b, layer, tr, w, m, v, big.get(nm))
                after = big[nm][0]

    gsmall = []
    for layer in range(depth):
        ssem, rsem, lands = small_gathers[layer]
        gsmall += copies_wait(f"small_wait{layer}", ssem, rsem, (), lands, after)[1]

    maps2 = lambda a: a.reshape(depth * HEADS * HD, HD)
    res = adam_small(
        "adam_small", *gsmall,
        [(ffn1_norm, m_ffn1_norm, v_ffn1_norm), (mix_norm, m_mix_norm, v_mix_norm), (ffn2_norm, m_ffn2_norm, v_ffn2_norm)],
        (pool_scale, m_pool_scale, v_pool_scale), (maps2(pool_maps), maps2(m_pool_maps), maps2(v_pool_maps)),
        (meta, m_meta, v_meta), tuple(a.reshape(1, d) for a in (final_norm, m_final_norm, v_final_norm)), d)
    loss = res[0][0, 0]
    sm = {}
    for k, nm in enumerate(["ffn1_norm", "mix_norm", "ffn2_norm", "pool_scale", "pool_maps", "meta", "final_norm"]):
        sm[nm] = list(res[1 + 4 * k:5 + 4 * k])
    sm["pool_maps"] = [a.reshape(pool_maps.shape) for a in sm["pool_maps"]]
    sm["final_norm"] = [a.reshape(d) for a in sm["final_norm"]]

    names = ["meta", "ffn1_norm", "ffn1_gate", "ffn1_up", "ffn1_down", "mix_norm", "w_in", "pool_maps", "pool_scale",
             "w_ret_up", "w_pool_up", "w_out", "ffn2_norm", "ffn2_gate", "ffn2_up", "ffn2_down", "final_norm"]
    for nm in ("ffn1_gate", "ffn1_up", "ffn2_gate", "ffn2_up", "w_in"):
        big[nm] = tview(*big[nm])
    allw = {**{k: list(v) for k, v in big.items()}, **sm}
    outs = [loss, grad_x]
    for kind in range(4):
        outs += [allw[nm][kind] for nm in names]
    return tuple(outs)
```

```python
import functools

import jax
import jax.numpy as jnp
from jax import lax
from jax.experimental import pallas as pl
from jax.experimental.pallas import tpu as pltpu

f32 = jnp.float32
bf16 = jnp.bfloat16
MESH = pl.DeviceIdType.MESH
NDEV = 8
N_META = 16
HEADS = 4
HD = 128
CHUNK = 128
RW = HEADS * HD
POOL_WINDOWS = (2, 4, 8, 16)
ROPE_BASE = 10000.0
EPS = 1e-6
ADAM_LR = 0.001
ADAM_B1 = 0.9
ADAM_B2 = 0.999
ADAM_EPS = 1e-08
ADAM_WD = 0.01
ADAM_STEP = 10
VMEM_CAP_MB = 60


def _cp(vmem_mb, sem=None):
    return pltpu.CompilerParams(vmem_limit_bytes=min(vmem_mb, VMEM_CAP_MB) * 2**20, dimension_semantics=sem)


def _row_tile(tp, want=384):
    return want if tp % want == 0 else 128


def _resident(shape):
    nd = len(shape)
    return pl.BlockSpec(shape, lambda *_: (0,) * nd, pipeline_mode=pl.Buffered(1))


def _skip(nd, body):
    return (lambda *refs: body(*refs[nd:])) if nd else body


def _dot_nn(a, b):
    return lax.dot_general(a, b, (((1,), (0,)), ((), ())), preferred_element_type=f32)


def _dot_nt(a, b):
    return lax.dot_general(a, b, (((1,), (1,)), ((), ())), preferred_element_type=f32)


def _dot_tn(a, b):
    return lax.dot_general(a, b, (((0,), (0,)), ((), ())), preferred_element_type=f32)


def _rms(h):
    rs = lax.rsqrt(jnp.mean(h * h, axis=-1, keepdims=True) + EPS)
    return h * rs, rs


def _rms_bwd(dn, g, hhat, rs):
    dhh = dn * g
    return rs * (dhh - hhat * jnp.mean(dhh * hhat, axis=-1, keepdims=True))


def _sigmoid(x):
    return jax.nn.sigmoid(x)


def _me():
    return lax.axis_index("x"), lax.axis_index("y"), lax.axis_index("c")


def _peer(idx):
    return (idx // 4, (idx // 2) % 2, idx % 2)


def all_gather(name, arrays):
    n = len(arrays)

    def body(*refs):
        ins, outs = refs[:n], refs[n:2 * n]
        send_sems, recv_sems, local_sems = refs[2 * n:]
        x, y, c = _me()
        me = 4 * x + 2 * y + c
        locals_ = []
        for k in range(n):
            cp = pltpu.make_async_copy(ins[k], outs[k].at[me], local_sems.at[k])
            cp.start()
            locals_.append(cp)
        for d in range(1, NDEV):
            for k in range(n):
                pltpu.make_async_remote_copy(
                    src_ref=ins[k], dst_ref=outs[k].at[me], send_sem=send_sems.at[k], recv_sem=recv_sems.at[k],
                    device_id=_peer((me + d) % NDEV), device_id_type=MESH).start()
        for k in range(n):
            seven = outs[k].at[pl.ds(0, NDEV - 1)]
            w = pltpu.make_async_remote_copy(src_ref=seven, dst_ref=seven, send_sem=send_sems.at[k],
                                             recv_sem=recv_sems.at[k], device_id=(x, y, c), device_id_type=MESH)
            w.wait_send()
            w.wait_recv()
            locals_[k].wait()

    anyspec = pl.BlockSpec(memory_space=pl.ANY)
    return pl.pallas_call(
        body, name=name,
        out_shape=[jax.ShapeDtypeStruct((NDEV,) + a.shape, a.dtype) for a in arrays],
        in_specs=[anyspec] * n, out_specs=[anyspec] * n,
        scratch_shapes=[pltpu.SemaphoreType.DMA((n,)), pltpu.SemaphoreType.DMA((n,)), pltpu.SemaphoreType.DMA((n,))],
    )(*arrays)


_HBM = pl.BlockSpec(memory_space=pltpu.HBM)
_SEM = pl.BlockSpec(memory_space=pltpu.SEMAPHORE)
_ANY = pl.BlockSpec(memory_space=pl.ANY)
_EFFECT = pltpu.SideEffectType.DATAFLOW_SIDE_EFFECTING


def _in_hbm(a):
    return pltpu.with_memory_space_constraint(a, pltpu.HBM)


def gather_start(name, lands, deps=()):
    n, nd = len(lands), len(deps)

    def body(*refs):
        land = refs[nd:nd + n]
        send_sems, recv_sems = refs[nd + n:nd + n + 2]
        token = refs[-1]
        x, y, c = _me()
        me = 4 * x + 2 * y + c
        for d in range(1, NDEV):
            for k in range(n):
                pltpu.make_async_remote_copy(
                    src_ref=land[k].at[me], dst_ref=land[k].at[me], send_sem=send_sems.at[k], recv_sem=recv_sems.at[k],
                    device_id=_peer((me + d) % NDEV), device_id_type=MESH).start()
        token[...] = jnp.zeros_like(token)

    res = pl.pallas_call(
        body, name=name,
        out_shape=(pltpu.SemaphoreType.DMA((n,)), pltpu.SemaphoreType.DMA((n,)),
                   *[pltpu.HBM(a.shape, a.dtype) for a in lands], jax.ShapeDtypeStruct((8, 128), f32)),
        in_specs=[_ANY] * nd + [_HBM] * n,
        out_specs=(_SEM, _SEM, *[_HBM] * n, pl.BlockSpec(memory_space=pltpu.VMEM)),
        input_output_aliases={nd + k: 2 + k for k in range(n)},
        compiler_params=pltpu.CompilerParams(has_side_effects=_EFFECT),
    )(*deps, *[_in_hbm(a) for a in lands])
    return res[0], res[1], list(res[2:2 + n]), res[-1]


def _other_chips(x, y):
    return [(1 - x, y), (x, 1 - y), (1 - x, 1 - y)]


def gather_start_chips(name, lands, deps=()):
    n, nd = len(lands), len(deps)

    def body(*refs):
        land = refs[nd:nd + n]
        send_sems, recv_sems = refs[nd + n:nd + n + 2]
        token = refs[-1]
        x, y, c = _me()
        me = 4 * x + 2 * y + c
        for k in range(n):
            for to in [(x, y, 1 - c)] + [(cx, cy, c) for cx, cy in _other_chips(x, y)]:
                pltpu.make_async_remote_copy(
                    src_ref=land[k].at[me], dst_ref=land[k].at[me], send_sem=send_sems.at[k], recv_sem=recv_sems.at[k],
                    device_id=to, device_id_type=MESH).start()
        token[...] = jnp.zeros_like(token)

    res = pl.pallas_call(
        body, name=name,
        out_shape=(pltpu.SemaphoreType.DMA((n,)), pltpu.SemaphoreType.DMA((n,)),
                   *[pltpu.HBM(a.shape, a.dtype) for a in lands], jax.ShapeDtypeStruct((8, 128), f32)),
        in_specs=[_ANY] * nd + [_HBM] * n,
        out_specs=(_SEM, _SEM, *[_HBM] * n, pl.BlockSpec(memory_space=pltpu.VMEM)),
        input_output_aliases={nd + k: 2 + k for k in range(n)},
        compiler_params=pltpu.CompilerParams(has_side_effects=_EFFECT),
    )(*deps, *[_in_hbm(a) for a in lands])
    return res[0], res[1], list(res[2:2 + n]), res[-1]


def gather_forward(name, send_sems, recv_sems, lands, after):
    n = len(lands)

    def body(*refs):
        land = refs[:n]
        ssem, rsem = refs[n:n + 2]
        send2, recv2 = refs[n + 3:n + 5]
        token = refs[-1]
        x, y, c = _me()
        for k in range(n):
            four = land[k].at[pl.ds(0, 4)]
            w = pltpu.make_async_remote_copy(src_ref=four, dst_ref=four, send_sem=ssem.at[k], recv_sem=rsem.at[k],
                                             device_id=(x, y, c), device_id_type=MESH)
            w.wait_send()
            w.wait_recv()
            for cx, cy in _other_chips(x, y):
                slot = 4 * cx + 2 * cy + c
                pltpu.make_async_remote_copy(
                    src_ref=land[k].at[slot], dst_ref=land[k].at[slot], send_sem=send2.at[k], recv_sem=recv2.at[k],
                    device_id=(x, y, 1 - c), device_id_type=MESH).start()
        token[...] = jnp.zeros_like(token)

    res = pl.pallas_call(
        body, name=name,
        out_shape=(pltpu.SemaphoreType.DMA((n,)), pltpu.SemaphoreType.DMA((n,)),
                   *[pltpu.HBM(a.shape, a.dtype) for a in lands], jax.ShapeDtypeStruct((8, 128), f32)),
        in_specs=[_HBM] * n + [_SEM, _SEM, _ANY],
        out_specs=(_SEM, _SEM, *[_HBM] * n, pl.BlockSpec(memory_space=pltpu.VMEM)),
        input_output_aliases={k: 2 + k for k in range(n)},
        compiler_params=pltpu.CompilerParams(has_side_effects=_EFFECT),
    )(*lands, send_sems, recv_sems, after)
    return res[0], res[1], list(res[2:2 + n]), res[-1]


def exchange_start(name, grads, deps=()):
    n, nd = len(grads), len(deps)
    lands = [lax.empty((NDEV - 1, g.shape[0]) + g.shape[2:], g.dtype) for g in grads]

    def body(*refs):
        src = refs[nd:nd + n]
        land = refs[nd + n:nd + 2 * n]
        send_sems, recv_sems = refs[nd + 2 * n:nd + 2 * n + 2]
        token = refs[-1]
        x, y, c = _me()
        me = 4 * x + 2 * y + c
        for d in range(1, NDEV):
            p = (me + d) % NDEV
            for k in range(n):
                pltpu.make_async_remote_copy(
                    src_ref=src[k].at[:, p], dst_ref=land[k].at[d - 1], send_sem=send_sems.at[k], recv_sem=recv_sems.at[k],
                    device_id=_peer(p), device_id_type=MESH).start()
        token[...] = jnp.zeros_like(token)

    both = list(grads) + lands
    res = pl.pallas_call(
        body, name=name,
        out_shape=(pltpu.SemaphoreType.DMA((n,)), pltpu.SemaphoreType.DMA((n,)),
                   *[pltpu.HBM(a.shape, a.dtype) for a in both], jax.ShapeDtypeStruct((8, 128), f32)),
        in_specs=[_ANY] * nd + [_HBM] * (2 * n),
        out_specs=(_SEM, _SEM, *[_HBM] * (2 * n), pl.BlockSpec(memory_space=pltpu.VMEM)),
        input_output_aliases={nd + k: 2 + k for k in range(2 * n)},
        compiler_params=pltpu.CompilerParams(has_side_effects=_EFFECT),
    )(*deps, *[_in_hbm(a) for a in both])
    return res[0], res[1], list(res[2:2 + n]), list(res[2 + n:2 + 2 * n]), res[-1]


def copies_wait(name, send_sems, recv_sems, sent, lands, after, count=NDEV - 1):
    ns, n = len(sent), len(lands)

    def body(*refs):
        land = refs[ns:ns + n]
        ssem, rsem = refs[ns + n:ns + n + 2]
        x, y, c = _me()
        for k in range(n):
            seven = land[k].at[pl.ds(0, count)]
            w = pltpu.make_async_remote_copy(src_ref=seven, dst_ref=seven, send_sem=ssem.at[k], recv_sem=rsem.at[k],
                                             device_id=(x, y, c), device_id_type=MESH)
            w.wait_send()
            w.wait_recv()

    both = list(sent) + list(lands)
    res = pl.pallas_call(
        body, name=name, out_shape=tuple(pltpu.HBM(a.shape, a.dtype) for a in both),
        in_specs=[_HBM] * (ns + n) + [_SEM, _SEM, _ANY], out_specs=tuple([_HBM] * (ns + n)),
        input_output_aliases={k: k for k in range(ns + n)},
        compiler_params=pltpu.CompilerParams(has_side_effects=_EFFECT),
    )(*both, send_sems, recv_sems, after)
    return list(res[:ns]), list(res[ns:])


def prep_layer(layer, me, col_sharded, row_sharded):
    nc, nr = len(col_sharded), len(row_sharded)

    def body(me_ref, *refs):
        ins, outs = refs[:nc + nr], refs[nc + nr:]
        for k in range(nc):
            outs[k][...] = ins[k][...].T.astype(bf16)
        for k in range(nc, nc + nr):
            outs[k][...] = ins[k][...].astype(bf16)

    arrs = list(col_sharded) + list(row_sharded)
    in_specs = [pl.BlockSpec((None,) + a.shape[1:], lambda i, me_ref: (layer, 0, 0)) for a in arrs]
    shapes = [(a.shape[2], a.shape[1]) for a in col_sharded] + [a.shape[1:] for a in row_sharded]
    out_specs = [pl.BlockSpec((None,) + s, lambda i, me_ref: (me_ref[0], 0, 0)) for s in shapes]
    return pl.pallas_call(
        body, name=f"prep_layer{layer}",
        grid_spec=pltpu.PrefetchScalarGridSpec(num_scalar_prefetch=1, grid=(1,), in_specs=in_specs, out_specs=out_specs),
        out_shape=[jax.ShapeDtypeStruct((NDEV,) + s, bf16) for s in shapes], compiler_params=_cp(48))(me, *arrs)


def slot_in(name, me, arrays):
    n = len(arrays)

    def body(me_ref, *refs):
        for k in range(n):
            refs[n + k][...] = refs[k][...]

    in_specs = [pl.BlockSpec(a.shape, lambda i, me_ref: (0, 0)) for a in arrays]
    out_specs = [pl.BlockSpec((None,) + a.shape, lambda i, me_ref: (me_ref[0], 0, 0)) for a in arrays]
    return pl.pallas_call(
        body, name=name,
        grid_spec=pltpu.PrefetchScalarGridSpec(num_scalar_prefetch=1, grid=(1,), in_specs=in_specs, out_specs=out_specs),
        out_shape=[jax.ShapeDtypeStruct((NDEV,) + a.shape, a.dtype) for a in arrays])(me, *arrays)


def _ff_chunks(ff, want=768):
    if ff % 256:
        return [slice(0, ff)]
    return [slice(c, min(c + want, ff)) for c in range(0, ff, want)]


def ffn_fwd(name, h, g, wgT, wuT, wd, deps=()):
    tp, d = h.shape
    ff = wgT.shape[0]
    tm = _row_tile(tp)

    def body(h_ref, g_ref, wg_ref, wu_ref, wd_ref, ho_ref, a_ref, b_ref):
        hh = h_ref[...]
        hhat, _ = _rms(hh)
        n = (hhat * g_ref[...]).astype(bf16)
        acc = None
        for cols in _ff_chunks(ff):
            a = _dot_nt(n, wg_ref[cols, :])
            b = _dot_nt(n, wu_ref[cols, :])
            part = _dot_nn(((a * _sigmoid(a)) * b).astype(bf16), wd_ref[cols, :])
            acc = part if acc is None else acc + part
            a_ref[:, cols] = a.astype(bf16)
            b_ref[:, cols] = b.astype(bf16)
        ho_ref[...] = hh + 0.5 * acc

    row = lambda w: pl.BlockSpec((tm, w), lambda i: (i, 0))
    return pl.pallas_call(
        _skip(len(deps), body), name=name, grid=(tp // tm,),
        in_specs=[_ANY] * len(deps) + [row(d), _resident((1, d)), _resident((ff, d)), _resident((ff, d)), _resident((ff, d))],
        out_specs=[row(d), row(ff), row(ff)],
        out_shape=[jax.ShapeDtypeStruct((tp, d), f32), jax.ShapeDtypeStruct((tp, ff), bf16),
                   jax.ShapeDtypeStruct((tp, ff), bf16)],
        compiler_params=_cp(56, ("arbitrary",)))(*deps, h, g, wgT, wuT, wd)


def ffn_bwd(name, dy, h, g, a, b, wgT, wuT, wd, deps=()):
    tp, d = h.shape
    ff = wgT.shape[0]
    tm = _row_tile(tp, 192)

    def body(dy_ref, h_ref, g_ref, a_ref, b_ref, wg_ref, wu_ref, wd_ref, dh_ref, lhs_ref, rhs_ref, dg_ref):
        dyv = dy_ref[...]
        hhat, rs = _rms(h_ref[...])
        gv = g_ref[...]
        n = hhat * gv
        dyh = (0.5 * dyv).astype(bf16)
        dn = None
        for cols in _ff_chunks(ff):
            ds = _dot_nt(dyh, wd_ref[cols, :])
            av = a_ref[:, cols].astype(f32)
            bv = b_ref[:, cols].astype(f32)
            sg = _sigmoid(av)
            sa = av * sg
            da = (ds * bv * (sg * (1.0 + av * (1.0 - sg)))).astype(bf16)
            db = (ds * sa).astype(bf16)
            part = _dot_nn(da, wg_ref[cols, :]) + _dot_nn(db, wu_ref[cols, :])
            dn = part if dn is None else dn + part
            lhs_ref[0, :, cols] = da
            lhs_ref[1, :, cols] = db
            lhs_ref[2, :, cols] = (sa * bv).astype(bf16)
        dh_ref[...] = dyv + _rms_bwd(dn, gv, hhat, rs)

        @pl.when(pl.program_id(0) == 0)
        def _():
            dg_ref[...] = jnp.zeros_like(dg_ref)

        dg_ref[0:1, :] += jnp.sum(dn * hhat, axis=0, keepdims=True)
        rhs_ref[0] = n.astype(bf16)
        rhs_ref[1] = dyh

    row = lambda w: pl.BlockSpec((tm, w), lambda i: (i, 0))
    return pl.pallas_call(
        _skip(len(deps), body), name=name, grid=(tp // tm,),
        in_specs=[_ANY] * len(deps) + [row(d), row(d), _resident((1, d)), row(ff), row(ff),
                  _resident((ff, d)), _resident((ff, d)), _resident((ff, d))],
        out_specs=[row(d), pl.BlockSpec((3, tm, ff), lambda i: (0, i, 0)), pl.BlockSpec((2, tm, d), lambda i: (0, i, 0)),
                   pl.BlockSpec((8, d), lambda i: (0, 0))],
        out_shape=[jax.ShapeDtypeStruct((tp, d), f32), jax.ShapeDtypeStruct((3, tp, ff), bf16),
                   jax.ShapeDtypeStruct((2, tp, d), bf16), jax.ShapeDtypeStruct((8, d), f32)],
        compiler_params=_cp(58, ("arbitrary",)))(*deps, dy, h, g, a, b, wgT, wuT, wd)


def mm_tn(name, lhs, rhs, rhs_of, deps=(), only=None):
    _, tp, m = lhs.shape
    b0, nb = (0, lhs.shape[0]) if only is None else (only, 1)
    n = rhs.shape[2]
    fixed = m * n * (4 + 2 * 2)
    tk = next((t for t in (1408, 704, 384) if tp % t == 0 and fixed + 2 * t * (m + n) * 2 <= 46 * 2**20), 128)
    nk = tp // tk

    def body(l_ref, r_ref, o_ref, acc_ref):
        k = pl.program_id(1)

        @pl.when(k == 0)
        def _():
            acc_ref[...] = jnp.zeros_like(acc_ref)

        acc_ref[...] += _dot_tn(l_ref[...], r_ref[...])

        @pl.when(k == nk - 1)
        def _():
            o_ref[...] = acc_ref[...].astype(o_ref.dtype)

    return pl.pallas_call(
        _skip(len(deps), body), name=name, grid=(nb, nk),
        in_specs=[_ANY] * len(deps) + [pl.BlockSpec((None, tk, m), lambda b, k: (b0 + b, k, 0)),
                                       pl.BlockSpec((None, tk, n), lambda b, k: (rhs_of(b0 + b), k, 0))],
        out_specs=pl.BlockSpec((None, m, n), lambda b, k: (b, 0, 0)),
        out_shape=jax.ShapeDtypeStruct((nb, m, n), bf16),
        scratch_shapes=[pltpu.VMEM((m, n), f32)],
        compiler_params=_cp(56, ("arbitrary", "arbitrary")))(*deps, lhs, rhs)


def mix_in_fwd(name, h, g, winT):
    tp, d = h.shape
    nin = winT.shape[0]
    tm = _row_tile(tp)

    def body(h_ref, g_ref, w_ref, zq_ref, zg_ref, zu_ref, zgate_ref):
        hhat, _ = _rms(h_ref[...])
        z = _dot_nt((hhat * g_ref[...]).astype(bf16), w_ref[...])
        zq_ref[...] = z[:, :3 * RW]
        zg_ref[...] = z[:, 3 * RW:4 * RW]
        zu_ref[...] = z[:, 4 * RW:5 * RW]
        zgate_ref[...] = z[:, 5 * RW:]

    row = lambda w: pl.BlockSpec((tm, w), lambda i: (i, 0))
    widths = (3 * RW, RW, RW, 2 * d)
    return pl.pallas_call(
        body, name=name, grid=(tp // tm,),
        in_specs=[row(d), _resident((1, d)), _resident((nin, d))],
        out_specs=[row(w) for w in widths],
        out_shape=[jax.ShapeDtypeStruct((tp, w), f32) for w in widths],
        compiler_params=_cp(56, ("arbitrary",)))(h, g, winT)


def mix_in_bwd(name, dq, dk, dv, dzg, dzu, dzgate, h, g, winT, dres):
    tp, d = h.shape
    nin = winT.shape[0]
    tm = _row_tile(tp)

    def body(dq_ref, dk_ref, dv_ref, dzg_ref, dzu_ref, dzgate_ref, h_ref, g_ref, w_ref, dres_ref, dh_ref, dz_ref, n_ref, dg_ref):
        dz = jnp.concatenate([dq_ref[...], dk_ref[...], dv_ref[...], dzg_ref[...], dzu_ref[...], dzgate_ref[...]], axis=-1)
        dn = _dot_nn(dz, w_ref[...])
        hhat, rs = _rms(h_ref[...])
        gv = g_ref[...]
        dh_ref[...] = dres_ref[...] + _rms_bwd(dn, gv, hhat, rs)

        @pl.when(pl.program_id(0) == 0)
        def _():
            dg_ref[...] = jnp.zeros_like(dg_ref)

        dg_ref[0:1, :] += jnp.sum(dn * hhat, axis=0, keepdims=True)
        dz_ref[...] = dz
        n_ref[...] = (hhat * gv).astype(bf16)

    row = lambda w: pl.BlockSpec((tm, w), lambda i: (i, 0))
    return pl.pallas_call(
        body, name=name, grid=(tp // tm,),
        in_specs=[row(RW)] * 5 + [row(2 * d), row(d), _resident((1, d)), _resident((nin, d)), row(d)],
        out_specs=[row(d), pl.BlockSpec((None, tm, nin), lambda i: (0, i, 0)), pl.BlockSpec((None, tm, d), lambda i: (0, i, 0)),
                   pl.BlockSpec((8, d), lambda i: (0, 0))],
        out_shape=[jax.ShapeDtypeStruct((tp, d), f32), jax.ShapeDtypeStruct((1, tp, nin), bf16),
                   jax.ShapeDtypeStruct((1, tp, d), bf16), jax.ShapeDtypeStruct((8, d), f32)],
        compiler_params=_cp(56, ("arbitrary",)))(dq, dk, dv, dzg, dzu, dzgate, h, g, winT, dres)


def _retention_tables(tp, pad):
    half = HD // 2
    inv_freq = ROPE_BASE ** (-jnp.arange(half, dtype=f32) / half)
    pos = jnp.arange(tp, dtype=f32) - pad
    ang = pos[:, None] * inv_freq[None, :]
    cos, sin = jnp.cos(ang), jnp.sin(ang)
    cos2 = jnp.concatenate([cos, cos], axis=-1)
    sin2 = jnp.concatenate([-sin, sin], axis=-1)
    log_gamma = jnp.log1p(-(2.0 ** (-5.0 - jnp.arange(HEADS, dtype=f32))))
    idx = jnp.arange(CHUNK, dtype=f32)
    diff = idx[:, None] - idx[None, :]
    intra = jnp.where(diff[None] >= 0, jnp.exp(diff[None] * log_gamma[:, None, None]), 0.0)
    k_decay = jnp.exp((CHUNK - 1.0 - idx)[None, :] * log_gamma[:, None])
    q_decay = jnp.exp((idx + 1.0)[None, :] * log_gamma[:, None])
    chunk_decay = jnp.exp(CHUNK * log_gamma)
    full = (HEADS, CHUNK, HD)
    dec = jnp.stack([intra, jnp.broadcast_to(k_decay[:, :, None], full), jnp.broadcast_to(q_decay[:, :, None], full),
                     jnp.broadcast_to(chunk_decay[:, None, None], full)], axis=1)
    return cos2, sin2, dec


def _rot(t, cos2, sin2):
    return t * cos2 + pltpu.roll(t, HD // 2, 1) * sin2


def _rot_t(t, cos2, sin2):
    return t * cos2 - pltpu.roll(t, HD // 2, 1) * sin2


def retention_fwd(name, zq, cos2, sin2, dec):
    tp = zq.shape[0]
    nch = tp // CHUNK
    scale = HD ** -0.5

    def body(q_ref, k_ref, v_ref, cos_ref, sin_ref, dec_ref, out_ref, st_ref, s_ref):
        @pl.when(pl.program_id(0) == 0)
        def _():
            s_ref[...] = jnp.zeros_like(s_ref)

        cosv, sinv = cos_ref[...], sin_ref[...]
        for hh in range(HEADS):
            cols = slice(hh * HD, (hh + 1) * HD)
            qr = _rot(q_ref[:, cols], cosv, sinv) * scale
            kr = _rot(k_ref[:, cols], cosv, sinv)
            vb = v_ref[:, cols].astype(bf16)
            sc = (_dot_nt(qr.astype(bf16), kr.astype(bf16)) * dec_ref[hh, 0]).astype(bf16)
            sv = s_ref[hh]
            sb = sv.astype(bf16)
            out_ref[:, cols] = _dot_nn(sc, vb) + _dot_nn((qr * dec_ref[hh, 2]).astype(bf16), sb)
            st_ref[hh, 0] = sb
            s_ref[hh] = sv * dec_ref[hh, 3] + _dot_tn((kr * dec_ref[hh, 1]).astype(bf16), vb)

    part = lambda j: pl.BlockSpec((CHUNK, RW), lambda n: (n, j))
    return pl.pallas_call(
        body, name=name, grid=(nch,),
        in_specs=[part(0), part(1), part(2), pl.BlockSpec((CHUNK, HD), lambda n: (n, 0)),
                  pl.BlockSpec((CHUNK, HD), lambda n: (n, 0)), _resident((HEADS, 4, CHUNK, HD))],
        out_specs=[part(0), pl.BlockSpec((HEADS, 1, HD, HD), lambda n: (0, n, 0, 0))],
        out_shape=[jax.ShapeDtypeStruct((tp, RW), f32), jax.ShapeDtypeStruct((HEADS, nch, HD, HD), bf16)],
        scratch_shapes=[pltpu.VMEM((HEADS, HD, HD), f32)],
        compiler_params=_cp(32, ("arbitrary",)))(zq, zq, zq, cos2, sin2, dec)


def retention_bwd(name, zq, cos2, sin2, dec, states, dout, pad, deps=()):
    tp = zq.shape[0]
    nch = tp // CHUNK
    scale = HD ** -0.5

    def body(q_ref, k_ref, v_ref, cos_ref, sin_ref, dec_ref, st_ref, do_ref, dq_ref, dk_ref, dv_ref, g_ref):
        @pl.when(pl.program_id(0) == 0)
        def _():
            g_ref[...] = jnp.zeros_like(g_ref)

        n = nch - 1 - pl.program_id(0)
        cosv, sinv = cos_ref[...], sin_ref[...]
        keep = (lax.broadcasted_iota(jnp.int32, (CHUNK, HD), 0) + n * CHUNK) >= pad
        for hh in range(HEADS):
            cols = slice(hh * HD, (hh + 1) * HD)
            intra, kdec, qdec = dec_ref[hh, 0], dec_ref[hh, 1], dec_ref[hh, 2]
            qr = _rot(q_ref[:, cols], cosv, sinv) * scale
            kr = _rot(k_ref[:, cols], cosv, sinv)
            qb, kb = qr.astype(bf16), kr.astype(bf16)
            vb = v_ref[:, cols].astype(bf16)
            qd = (qr * qdec).astype(bf16)
            kd = (kr * kdec).astype(bf16)
            sc = (_dot_nt(qb, kb) * intra).astype(bf16)
            dob = do_ref[:, cols]
            sb = st_ref[hh, 0]
            gv = g_ref[hh]
            gb = gv.astype(bf16)
            dsc = (_dot_nt(dob, vb) * intra).astype(bf16)
            dv = _dot_tn(sc, dob) + _dot_nn(kd, gb)
            dqr = _dot_nn(dsc, kb) + _dot_nt(dob, sb) * qdec
            dkr = _dot_tn(dsc, qb) + _dot_nt(vb, gb) * kdec
            g_ref[hh] = gv * dec_ref[hh, 3] + _dot_tn(qd, dob)
            dq_ref[:, cols] = jnp.where(keep, _rot_t(dqr * scale, cosv, sinv), 0.0).astype(bf16)
            dk_ref[:, cols] = jnp.where(keep, _rot_t(dkr, cosv, sinv), 0.0).astype(bf16)
            dv_ref[:, cols] = jnp.where(keep, dv, 0.0).astype(bf16)

    part = lambda j: pl.BlockSpec((CHUNK, RW), lambda t: (nch - 1 - t, j))
    table = pl.BlockSpec((CHUNK, HD), lambda t: (nch - 1 - t, 0))
    return pl.pallas_call(
        _skip(len(deps), body), name=name, grid=(nch,),
        in_specs=[_ANY] * len(deps) + [part(0), part(1), part(2), table, table, _resident((HEADS, 4, CHUNK, HD)),
                                       pl.BlockSpec((HEADS, 1, HD, HD), lambda t: (0, nch - 1 - t, 0, 0)), part(0)],
        out_specs=[part(0)] * 3,
        out_shape=[jax.ShapeDtypeStruct((tp, RW), bf16)] * 3,
        scratch_shapes=[pltpu.VMEM((HEADS, HD, HD), f32)],
        compiler_params=_cp(32, ("arbitrary",)))(*deps, zq, zq, zq, cos2, sin2, dec, states, dout)


def _window_sum(xv, steps, tp, forward):
    s = xv
    for j in range(steps):
        sh = 2 ** j
        s = s + pltpu.roll(s, (tp - sh) if forward else sh, 0)
    return s


def pool_fwd(name, zu, maps, scale, pad):
    tp = zu.shape[0]

    def body(u_ref, maps_ref, scale_ref, pooled_ref, p_ref):
        row = lax.broadcasted_iota(jnp.int32, (tp, HD), 0)
        for gi, w in enumerate(POOL_WINDOWS):
            cols = slice(gi * HD, (gi + 1) * HD)
            xv = u_ref[:, cols]
            cnt = jnp.clip(row - (pad - 1), 1, w).astype(f32)
            pooled = jnp.where(row >= pad, _window_sum(xv, gi + 1, tp, False) / cnt - xv, 0.0).astype(bf16)
            pooled_ref[:, cols] = pooled
            p_ref[:, cols] = (_dot_nn(pooled, maps_ref[gi].astype(bf16)) * scale_ref[:, cols]).astype(bf16)

    return pl.pallas_call(
        body, name=name,
        out_shape=[jax.ShapeDtypeStruct((tp, RW), bf16), jax.ShapeDtypeStruct((tp, RW), bf16)],
        compiler_params=_cp(56))(zu, maps, scale)


def pool_bwd(name, dp, pooled, maps, scale, pad):
    tp = dp.shape[0]

    def body(dp_ref, pooled_ref, maps_ref, scale_ref, du_ref, dmaps_ref, dscale_ref):
        row = lax.broadcasted_iota(jnp.int32, (tp, HD), 0)
        dscale_ref[...] = jnp.zeros_like(dscale_ref)
        for gi, w in enumerate(POOL_WINDOWS):
            cols = slice(gi * HD, (gi + 1) * HD)
            mb = maps_ref[gi].astype(bf16)
            pooled = pooled_ref[:, cols]
            dpf = dp_ref[:, cols].astype(f32)
            dscale_ref[0:1, cols] = jnp.sum(dpf * _dot_nn(pooled, mb), axis=0, keepdims=True)
            dpm = (dpf * scale_ref[:, cols]).astype(bf16)
            dmaps_ref[gi * HD:(gi + 1) * HD, :] = _dot_tn(pooled, dpm)
            dpool = jnp.where(row >= pad, _dot_nt(dpm, mb), 0.0)
            cnt = jnp.clip(row - (pad - 1), 1, w).astype(f32)
            du = _window_sum(dpool / cnt, gi + 1, tp, True) - dpool
            du_ref[:, cols] = jnp.where(row >= pad, du, 0.0).astype(bf16)

    return pl.pallas_call(
        body, name=name,
        out_shape=[jax.ShapeDtypeStruct((tp, RW), bf16), jax.ShapeDtypeStruct((HEADS * HD, HD), f32),
                   jax.ShapeDtypeStruct((8, RW), f32)],
        compiler_params=_cp(56))(dp, pooled, maps, scale)


def _group_norm(o):
    mu = jnp.mean(o, axis=-1, keepdims=True)
    oc = o - mu
    rstd = lax.rsqrt(jnp.mean(oc * oc, axis=-1, keepdims=True) + EPS)
    return oc * rstd, rstd


def mix_out_fwd(name, h, oraw, zg, zgate, p, wretT, wpoolT, wout, deps=()):
    tp, d = h.shape
    tm = _row_tile(tp)

    def body(h_ref, o_ref, zg_ref, zgate_ref, p_ref, wr_ref, wp_ref, wo_ref, ho_ref, rp_ref, ret_ref, pool_ref, mixed_ref):
        parts = []
        for hh in range(HEADS):
            cols = slice(hh * HD, (hh + 1) * HD)
            rhat, _ = _group_norm(o_ref[:, cols])
            gv = zg_ref[:, cols]
            parts.append(rhat * (gv * _sigmoid(gv)))
        r = jnp.concatenate(parts, axis=-1).astype(bf16)
        pv = p_ref[...]
        ret = _dot_nt(r, wr_ref[...])
        pool = _dot_nt(pv, wp_ref[...])
        mixed = (_sigmoid(zgate_ref[:, :d]) * ret + _sigmoid(zgate_ref[:, d:]) * pool).astype(bf16)
        ho_ref[...] = h_ref[...] + _dot_nn(mixed, wo_ref[...])
        rp_ref[0] = r
        rp_ref[1] = pv
        ret_ref[...] = ret.astype(bf16)
        pool_ref[...] = pool.astype(bf16)
        mixed_ref[...] = mixed

    row = lambda w: pl.BlockSpec((tm, w), lambda i: (i, 0))
    return pl.pallas_call(
        _skip(len(deps), body), name=name, grid=(tp // tm,),
        in_specs=[_ANY] * len(deps) + [row(d), row(RW), row(RW), row(2 * d), row(RW), _resident((d, RW)), _resident((d, RW)),
                                       _resident((d, d))],
        out_specs=[row(d), pl.BlockSpec((2, tm, RW), lambda i: (0, i, 0)), row(d), row(d),
                   pl.BlockSpec((None, tm, d), lambda i: (0, i, 0))],
        out_shape=[jax.ShapeDtypeStruct((tp, d), f32), jax.ShapeDtypeStruct((2, tp, RW), bf16),
                   jax.ShapeDtypeStruct((tp, d), bf16), jax.ShapeDtypeStruct((tp, d), bf16),
                   jax.ShapeDtypeStruct((1, tp, d), bf16)],
        compiler_params=_cp(48, ("arbitrary",)))(*deps, h, oraw, zg, zgate, p, wretT, wpoolT, wout)


def mix_out_bwd(name, dy, oraw, zg, zgate, ret, pool, wretT, wpoolT, wout, deps=()):
    tp, d = dy.shape
    tm = _row_tile(tp)

    def body(dy_ref, o_ref, zg_ref, zgate_ref, ret_ref, pool_ref, wr_ref, wp_ref, wo_ref,
             do_ref, dzg_ref, dzgate_ref, dp_ref, drp_ref, dyb_ref):
        dyb = dy_ref[...].astype(bf16)
        dmixed = _dot_nt(dyb, wo_ref[...])
        sa = _sigmoid(zgate_ref[:, :d])
        sb = _sigmoid(zgate_ref[:, d:])
        dret = dmixed * sa
        dpool = dmixed * sb
        dzgate_ref[:, :d] = (dret * ret_ref[...].astype(f32) * (1.0 - sa)).astype(bf16)
        dzgate_ref[:, d:] = (dpool * pool_ref[...].astype(f32) * (1.0 - sb)).astype(bf16)
        dretb, dpoolb = dret.astype(bf16), dpool.astype(bf16)
        dr = _dot_nn(dretb, wr_ref[...])
        dp_ref[...] = _dot_nn(dpoolb, wp_ref[...]).astype(bf16)
        for hh in range(HEADS):
            cols = slice(hh * HD, (hh + 1) * HD)
            rhat, rstd = _group_norm(o_ref[:, cols])
            gv = zg_ref[:, cols]
            sg = _sigmoid(gv)
            drh = dr[:, cols]
            drhat = drh * (gv * sg)
            dzg_ref[:, cols] = (drh * rhat * (sg * (1.0 + gv * (1.0 - sg)))).astype(bf16)
            do = rstd * (drhat - jnp.mean(drhat, axis=-1, keepdims=True)
                         - rhat * jnp.mean(drhat * rhat, axis=-1, keepdims=True))
            do_ref[:, cols] = do.astype(bf16)
        drp_ref[0] = dretb
        drp_ref[1] = dpoolb
        dyb_ref[...] = dyb

    row = lambda w: pl.BlockSpec((tm, w), lambda i: (i, 0))
    return pl.pallas_call(
        _skip(len(deps), body), name=name, grid=(tp // tm,),
        in_specs=[_ANY] * len(deps) + [row(d), row(RW), row(RW), row(2 * d), row(d), row(d), _resident((d, RW)), _resident((d, RW)),
                  _resident((d, d))],
        out_specs=[row(RW), row(RW), row(2 * d), row(RW), pl.BlockSpec((2, tm, d), lambda i: (0, i, 0)),
                   pl.BlockSpec((None, tm, d), lambda i: (0, i, 0))],
        out_shape=[jax.ShapeDtypeStruct((tp, RW), bf16), jax.ShapeDtypeStruct((tp, RW), bf16),
                   jax.ShapeDtypeStruct((tp, 2 * d), bf16), jax.ShapeDtypeStruct((tp, RW), bf16),
                   jax.ShapeDtypeStruct((2, tp, d), bf16), jax.ShapeDtypeStruct((1, tp, d), bf16)],
        compiler_params=_cp(48, ("arbitrary",)))(*deps, dy, oraw, zg, zgate, ret, pool, wretT, wpoolT, wout)


def final_loss(name, h, g, target):
    tp, d = h.shape
    tm = _row_tile(tp)
    nsub = tm // CHUNK

    def body(h_ref, g_ref, *rest):
        t_refs = rest[:nsub]
        dh_ref, loss_ref, dg_ref = rest[nsub:]
        i = pl.program_id(0)

        @pl.when(i == 0)
        def _():
            loss_ref[...] = jnp.zeros_like(loss_ref)
            dg_ref[...] = jnp.zeros_like(dg_ref)

        gv = g_ref[...]
        for j in range(nsub):
            rows = slice(j * CHUNK, (j + 1) * CHUNK)
            hhat, rs = _rms(h_ref[rows, :])
            err = jnp.where(i * nsub + j >= 1, hhat * gv - t_refs[j][...], 0.0)
            dyv = err / d
            dh_ref[rows, :] = _rms_bwd(dyv, gv, hhat, rs)
            loss_ref[...] += 0.5 * jnp.sum(jnp.sum(err * err, axis=-1, keepdims=True) / d)
            dg_ref[0:1, :] += jnp.sum(dyv * hhat, axis=0, keepdims=True)

    lagged = lambda j: pl.BlockSpec((CHUNK, d), lambda i: (jnp.maximum(i * nsub + j - 1, 0), 0))
    return pl.pallas_call(
        body, name=name, grid=(tp // tm,),
        in_specs=[pl.BlockSpec((tm, d), lambda i: (i, 0)), _resident((1, d))] + [lagged(j) for j in range(nsub)],
        out_specs=[pl.BlockSpec((tm, d), lambda i: (i, 0)), pl.BlockSpec((8, 128), lambda i: (0, 0)),
                   pl.BlockSpec((8, d), lambda i: (0, 0))],
        out_shape=[jax.ShapeDtypeStruct((tp, d), f32), jax.ShapeDtypeStruct((8, 128), f32),
                   jax.ShapeDtypeStruct((8, d), f32)],
        compiler_params=_cp(32, ("arbitrary",)))(h, g, *[target] * nsub)


def _adamw(w, g, m, v):
    m = ADAM_B1 * m + (1.0 - ADAM_B1) * g
    v = ADAM_B2 * v + (1.0 - ADAM_B2) * (g * g)
    m_hat = m / (1.0 - ADAM_B1 ** ADAM_STEP)
    v_hat = v / (1.0 - ADAM_B2 ** ADAM_STEP)
    delta = -ADAM_LR * (m_hat / (jnp.sqrt(v_hat) + ADAM_EPS) + ADAM_WD * w)
    return delta, m, v


def adam_big(name, me, recv, own, b, layer, transposed, w, m, v, prev):
    r, c = recv.shape[2:]
    wshape = w.shape[1:]
    nchunk = 1 if transposed else next(k for k in (4, 2, 1) if r % (16 * k) == 0)
    rc = r // nchunk

    def body(me_ref, recv_ref, own_ref, w_ref, m_ref, v_ref, *rest):
        g_ref, d_ref, nm_ref, nv_ref = rest[-4:]
        g = own_ref[...].astype(f32)
        for j in range(NDEV - 1):
            g = g + recv_ref[j].astype(f32)
        if transposed:
            g = g.T
        delta, nm, nv = _adamw(w_ref[...], g, m_ref[...], v_ref[...])
        g_ref[...] = g
        d_ref[...] = delta
        nm_ref[...] = nm
        nv_ref[...] = nv

    wblock = wshape if transposed else (rc, c)
    wspec = pl.BlockSpec((None,) + wblock, lambda i, me_ref: (layer, i, 0))
    in_specs = [pl.BlockSpec((NDEV - 1, None, rc, c), lambda i, me_ref: (0, b, i, 0)),
                pl.BlockSpec((None, None, rc, c), lambda i, me_ref: (b, me_ref[0], i, 0)), wspec, wspec, wspec]
    args = [recv, own, w, m, v]
    aliases = {}
    if prev is not None:
        in_specs += [_ANY] * 4
        args += list(prev)
        aliases = {6 + k: k for k in range(4)}
    return pl.pallas_call(
        body, name=name,
        grid_spec=pltpu.PrefetchScalarGridSpec(num_scalar_prefetch=1, grid=(nchunk,), in_specs=in_specs,
                                               out_specs=[wspec] * 4),
        out_shape=[jax.ShapeDtypeStruct(w.shape, f32)] * 4, input_output_aliases=aliases,
        compiler_params=_cp(56))(me, *args)


def adam_small(name, ga0, gmaps0, gmeta, ga1, gmaps1, norms, pool_scale, pool_maps, meta, final_norm, d):
    def body(ga0_ref, gmaps0_ref, gmeta_ref, ga1_ref, gmaps1_ref, *refs):
        ins, outs = refs[:21], refs[21:]
        x, y, c = _me()
        me = 4 * x + 2 * y + c

        def total(ref, rows):
            t = ref[0, rows, :]
            for j in range(1, NDEV):
                t = t + ref[j, rows, :]
            return t

        row = lambda r: slice(r, r + 1)
        outs[0][...] = jnp.broadcast_to(total(ga1_ref, row(0))[:, :128], (8, 128))

        def update(k, g, o):
            w_ref, m_ref, v_ref = ins[3 * k:3 * k + 3]
            delta, nm, nv = _adamw(w_ref[...], g, m_ref[...], v_ref[...])
            for ref, val in zip(outs[o:o + 4], (g, delta, nm, nv)):
                ref[...] = val

        two = lax.broadcasted_iota(jnp.int32, (2, d), 0)
        for k in range(3):
            update(k, jnp.where(two == 0, total(ga0_ref, row(k)), total(ga1_ref, row(2 + k))), 1 + 4 * k)
        update(3, jnp.where(two[:, :RW] == 0, total(ga0_ref, row(3))[:, :RW], total(ga1_ref, row(5))[:, :RW]), 13)
        update(4, jnp.concatenate([total(gmaps0_ref, slice(None)), total(gmaps1_ref, slice(None))], axis=0), 17)
        update(5, total(gmeta_ref, pl.ds(pl.multiple_of(me * N_META, N_META), N_META)), 21)
        update(6, total(ga1_ref, row(1)), 25)

    flat = []
    for trip in (*norms, pool_scale, pool_maps, meta, final_norm):
        flat += list(trip)
    out_shapes = [jax.ShapeDtypeStruct((8, 128), f32)]
    for trip in (*norms, pool_scale, pool_maps, meta, final_norm):
        out_shapes += [jax.ShapeDtypeStruct(trip[0].shape, f32)] * 4
    return pl.pallas_call(body, name=name, out_shape=out_shapes,
                          compiler_params=_cp(32))(ga0, gmaps0, gmeta, ga1, gmaps1, *flat)


def kernel(x, meta, ffn1_norm, ffn1_gate, ffn1_up, ffn1_down, mix_norm, w_in, pool_maps, pool_scale, w_ret_up, w_pool_up, w_out, ffn2_norm, ffn2_gate, ffn2_up, ffn2_down, final_norm, loss_target, m_meta, m_ffn1_norm, m_ffn1_gate, m_ffn1_up, m_ffn1_down, m_mix_norm, m_w_in, m_pool_maps, m_pool_scale, m_w_ret_up, m_w_pool_up, m_w_out, m_ffn2_norm, m_ffn2_gate, m_ffn2_up, m_ffn2_down, m_final_norm, v_meta, v_ffn1_norm, v_ffn1_gate, v_ffn1_up, v_ffn1_down, v_mix_norm, v_w_in, v_pool_maps, v_pool_scale, v_w_ret_up, v_w_pool_up, v_w_out, v_ffn2_norm, v_ffn2_gate, v_ffn2_up, v_ffn2_down, v_final_norm):
    seq, d = x.shape[1], x.shape[2]
    depth = ffn1_gate.shape[0]
    ff = ffn1_gate.shape[2] * NDEV
    nin = w_in.shape[2] * NDEV
    length = seq + N_META
    pad = (-length) % CHUNK
    tp = length + pad
    assert pad % 8 == 0 and pad + N_META == CHUNK and depth == 2 and nin == 5 * RW + 2 * d

    ix, iy, ic = _me()
    me = (4 * ix + 2 * iy + ic).astype(jnp.int32).reshape(1)

    meta_all, = all_gather("gather_meta", [meta])
    meta_full = jnp.transpose(meta_all, (1, 0, 2)).reshape(N_META, d)

    gathers = {}
    token = meta_all
    tview = lambda *arrs: [jnp.swapaxes(a, 1, 2) for a in arrs]
    t_g1, t_u1, t_g2, t_u2, t_in = (tview(w, m, v) for w, m, v in (
        (ffn1_gate, m_ffn1_gate, v_ffn1_gate), (ffn1_up, m_ffn1_up, v_ffn1_up), (ffn2_gate, m_ffn2_gate, v_ffn2_gate),
        (ffn2_up, m_ffn2_up, v_ffn2_up), (w_in, m_w_in, v_w_in)))
    for layer in range(depth):
        lands = prep_layer(layer, me, [w_ret_up, w_pool_up],
                           [t_g1[0], t_u1[0], t_g2[0], t_u2[0], t_in[0], ffn1_down, ffn2_down, w_out])
        wretT, wpoolT, g1T, u1T, g2T, u2T, winT, d1, d2, wout = lands
        for part, group in (("ffn1", [g1T, u1T, d1]), ("mix", [winT, wretT, wpoolT, wout]), ("ffn2", [g2T, u2T, d2])):
            ssem, rsem, group, token = gather_start_chips(f"gather_start_{part}{layer}", group, (token,))
            gathers[(part, layer)] = (ssem, rsem, group)

    def forward(part, layer, after):
        ssem, rsem, group = gathers[(part, layer)]
        ssem, rsem, group, tok = gather_forward(f"gather_forward_{part}{layer}", ssem, rsem, group, after)
        gathers[(part, layer)] = (ssem, rsem, group)
        return tok

    def gathered(part, layer, after):
        ssem, rsem, group = gathers[(part, layer)]
        _, full = copies_wait(f"gather_wait_{part}{layer}", ssem, rsem, (), group, after, 3)
        return [a.reshape((NDEV * a.shape[1],) + a.shape[2:]) for a in full]

    cos2, sin2, dec = _retention_tables(tp, pad)
    h = jnp.concatenate([jnp.zeros((pad, d), f32), meta_full, x[0]], axis=0)

    saved = []
    weights = []
    tok = forward("ffn1", 0, token)
    for layer in range(depth):
        row = lambda a: a[layer:layer + 1]
        s = {"h0": h}
        g1T, u1T, d1 = gathered("ffn1", layer, tok if layer == 0 else h)
        tok = forward("mix", layer, h) if layer else None
        h, s["a1"], s["b1"] = ffn_fwd(f"ffn1_fwd{layer}", h, row(ffn1_norm), g1T, u1T, d1, (tok,) if layer else ())
        s["h1"] = h
        if layer == 0:
            tok = forward("mix", layer, h)
        winT, wretT, wpoolT, wout = gathered("mix", layer, tok if layer == 0 else h)
        s["zq"], s["zg"], zu, s["zgate"] = mix_in_fwd(f"mix_in_fwd{layer}", h, row(mix_norm), winT)
        s["oraw"], s["states"] = retention_fwd(f"retention_fwd{layer}", s["zq"], cos2, sin2, dec)
        s["pooled"], p = pool_fwd(f"pool_fwd{layer}", zu, pool_maps[layer], row(pool_scale), pad)
        tok = forward("ffn2", layer, p)
        h, s["rp"], s["ret"], s["pool"], s["mixed"] = mix_out_fwd(
            f"mix_out_fwd{layer}", h, s["oraw"], s["zg"], s["zgate"], p, wretT, wpoolT, wout, (tok,))
        s["h2"] = h
        g2T, u2T, d2 = gathered("ffn2", layer, h)
        tok = (forward("ffn1", layer + 1, h),) if layer + 1 < depth else ()
        h, s["a2"], s["b2"] = ffn_fwd(f"ffn2_fwd{layer}", h, row(ffn2_norm), g2T, u2T, d2, tok)
        saved.append(s)
        weights.append((g1T, u1T, g2T, u2T, winT, wretT, wpoolT, d1, d2, wout))

    dh, loss_part, dg_final = final_loss("final_loss", h, final_norm.reshape(1, d), loss_target[0])

    small = {}
    small_gathers = {}
    exchanges = {}
    token = None

    def rows8(vals):
        at = lax.broadcasted_iota(jnp.int32, (8, d), 0)
        out = jnp.zeros((8, d), f32)
        for k, v in enumerate(vals):
            r0 = v[0:1]
            r0 = r0 if r0.shape[1] == d else jnp.pad(r0, ((0, 0), (0, d - r0.shape[1])))
            out = jnp.where(at == k, r0, out)
        return out

    def exchange(part, layer, grads):
        by_dest = [g.reshape(g.shape[0], NDEV, g.shape[1] // NDEV, g.shape[2]) for g in grads]
        ssem, rsem, sent, lands, tok = exchange_start(f"exchange_start_{part}{layer}", by_dest)
        exchanges[(part, layer)] = (ssem, rsem, sent, lands)
        return (tok,)

    for layer in reversed(range(depth)):
        g1T, u1T, g2T, u2T, winT, wretT, wpoolT, d1, d2, wout = weights[layer]
        row = lambda a: a[layer:layer + 1]
        s = saved[layer]
        dh, lhs2, rhs2, small[("ffn2", layer)] = ffn_bwd(
            f"ffn2_bwd{layer}", dh, s["h2"], row(ffn2_norm), s["a2"], s["b2"], g2T, u2T, d2, () if token is None else token)
        token = exchange("ffn2", layer, [mm_tn(f"ffn2_wgrad{layer}", lhs2, rhs2, lambda b: b // 2)])
        do, dzg, dzgate, dp, drp, dyb = mix_out_bwd(
            f"mix_out_bwd{layer}", dh, s["oraw"], s["zg"], s["zgate"], s["ret"], s["pool"], wretT, wpoolT, wout, token)
        token = exchange("mix", layer, [mm_tn(f"w_out_wgrad{layer}", s["mixed"], dyb, lambda b: b),
                                        mm_tn(f"up_wgrad{layer}", drp, s["rp"], lambda b: b)])
        dq, dk, dv = retention_bwd(f"retention_bwd{layer}", s["zq"], cos2, sin2, dec, s["states"], do, pad, token)
        dzu, small[("maps", layer)], small[("scale", layer)] = pool_bwd(
            f"pool_bwd{layer}", dp, s["pooled"], pool_maps[layer], row(pool_scale), pad)
        dh, dz, n2, small[("mix", layer)] = mix_in_bwd(
            f"mix_in_bwd{layer}", dq, dk, dv, dzg, dzu, dzgate, s["h1"], row(mix_norm), winT, dh)
        token = exchange("w_in", layer, [mm_tn(f"w_in_wgrad{layer}", dz, n2, lambda b: b)])
        dh, lhs1, rhs1, small[("ffn1", layer)] = ffn_bwd(
            f"ffn1_bwd{layer}", dh, s["h0"], row(ffn1_norm), s["a1"], s["b1"], g1T, u1T, d1, token)
        rows = [small[("ffn1", layer)], small[("mix", layer)], small[("ffn2", layer)], small[("scale", layer)]]
        packs = [rows8([loss_part, dg_final] + rows if layer == depth - 1 else rows), small[("maps", layer)]]
        if layer == 0:
            dmeta = dh[pad:CHUNK]
            packs.append(jnp.transpose(dmeta.reshape(N_META, NDEV, d // NDEV), (1, 0, 2)).reshape(NDEV * N_META, d // NDEV))
        ssem, rsem, lands, tok = gather_start(f"small_start{layer}", slot_in(f"small_slot{layer}", me, packs))
        small_gathers[layer] = (ssem, rsem, lands)
        if layer:
            token = exchange("ffn1", layer, [mm_tn(f"ffn1_wgrad{layer}", lhs1, rhs1, lambda b: b // 2, (tok,))])
        else:
            token = (tok,)
            for j, nm in enumerate(("ffn1_gate", "ffn1_up", "ffn1_down")):
                token = exchange(nm, layer, [mm_tn(f"{nm}_wgrad{layer}", lhs1, rhs1, lambda b: b // 2, token, only=j)])

    grad_x = dh[CHUNK:][None]

    big = {}
    after = token[0]
    plans = {
        "ffn2": [("ffn2_gate", 0, 0, False, *t_g2), ("ffn2_up", 0, 1, False, *t_u2),
                 ("ffn2_down", 0, 2, False, ffn2_down, m_ffn2_down, v_ffn2_down)],
        "mix": [("w_out", 0, 0, False, w_out, m_w_out, v_w_out),
                ("w_ret_up", 1, 0, True, w_ret_up, m_w_ret_up, v_w_ret_up),
                ("w_pool_up", 1, 1, True, w_pool_up, m_w_pool_up, v_w_pool_up)],
        "w_in": [("w_in", 0, 0, False, *t_in)],
        "ffn1": [("ffn1_gate", 0, 0, False, *t_g1), ("ffn1_up", 0, 1, False, *t_u1),
                 ("ffn1_down", 0, 2, False, ffn1_down, m_ffn1_down, v_ffn1_down)]}
    for nm, k, b, tr, w, m, v in plans["ffn1"]:
        plans[nm] = [(nm, 0, 0, tr, w, m, v)]
    for layer in reversed(range(depth)):
        for part in ("ffn2", "mix", "w_in") + (("ffn1",) if layer else ("ffn1_gate", "ffn1_up", "ffn1_down")):
            ssem, rsem, sent, lands = exchanges[(part, layer)]
            sent, lands = copies_wait(f"exchange_wait_{part}{layer}", ssem, rsem, sent, lands, after)
            for nm, k, b, tr, w, m, v in plans[part]:
                big[nm] = adam_big(f"adam_{nm}{layer}", me, lands[k], sent[k], b, layer, tr, w, m, v, big.get(nm))
                after = big[nm][0]

    gsmall = []
    for layer in range(depth):
        ssem, rsem, lands = small_gathers[layer]
        gsmall += copies_wait(f"small_wait{layer}", ssem, rsem, (), lands, after)[1]

    maps2 = lambda a: a.reshape(depth * HEADS * HD, HD)
    res = adam_small(
        "adam_small", *gsmall,
        [(ffn1_norm, m_ffn1_norm, v_ffn1_norm), (mix_norm, m_mix_norm, v_mix_norm), (ffn2_norm, m_ffn2_norm, v_ffn2_norm)],
        (pool_scale, m_pool_scale, v_pool_scale), (maps2(pool_maps), maps2(m_pool_maps), maps2(v_pool_maps)),
        (meta, m_meta, v_meta), tuple(a.reshape(1, d) for a in (final_norm, m_final_norm, v_final_norm)), d)
    loss = res[0][0, 0]
    sm = {}
    for k, nm in enumerate(["ffn1_norm", "mix_norm", "ffn2_norm", "pool_scale", "pool_maps", "meta", "final_norm"]):
        sm[nm] = list(res[1 + 4 * k:5 + 4 * k])
    sm["pool_maps"] = [a.reshape(pool_maps.shape) for a in sm["pool_maps"]]
    sm["final_norm"] = [a.reshape(d) for a in sm["final_norm"]]

    names = ["meta", "ffn1_norm", "ffn1_gate", "ffn1_up", "ffn1_down", "mix_norm", "w_in", "pool_maps", "pool_scale",
             "w_ret_up", "w_pool_up", "w_out", "ffn2_norm", "ffn2_gate", "ffn2_up", "ffn2_down", "final_norm"]
    for nm in ("ffn1_gate", "ffn1_up", "ffn2_gate", "ffn2_up", "w_in"):
        big[nm] = tview(*big[nm])
    allw = {**{k: list(v) for k, v in big.items()}, **sm}
    outs = [loss, grad_x]
    for kind in range(4):
        outs += [allw[nm][kind] for nm in names]
    return tuple(outs)
```

```python
import functools

import jax
import jax.numpy as jnp
from jax import lax
from jax.experimental import pallas as pl
from jax.experimental.pallas import tpu as pltpu

f32 = jnp.float32
bf16 = jnp.bfloat16
MESH = pl.DeviceIdType.MESH
NDEV = 8
N_META = 16
HEADS = 4
HD = 128
CHUNK = 128
RW = HEADS * HD
POOL_WINDOWS = (2, 4, 8, 16)
ROPE_BASE = 10000.0
EPS = 1e-6
ADAM_LR = 0.001
ADAM_B1 = 0.9
ADAM_B2 = 0.999
ADAM_EPS = 1e-08
ADAM_WD = 0.01
ADAM_STEP = 10
VMEM_CAP_MB = 60


def _cp(vmem_mb, sem=None):
    return pltpu.CompilerParams(vmem_limit_bytes=min(vmem_mb, VMEM_CAP_MB) * 2**20, dimension_semantics=sem)


def _row_tile(tp, want=384):
    return want if tp % want == 0 else 128


def _resident(shape):
    nd = len(shape)
    return pl.BlockSpec(shape, lambda *_: (0,) * nd, pipeline_mode=pl.Buffered(1))


def _skip(nd, body):
    return (lambda *refs: body(*refs[nd:])) if nd else body


def _dot_nn(a, b):
    return lax.dot_general(a, b, (((1,), (0,)), ((), ())), preferred_element_type=f32)


def _dot_nt(a, b):
    return lax.dot_general(a, b, (((1,), (1,)), ((), ())), preferred_element_type=f32)


def _dot_tn(a, b):
    return lax.dot_general(a, b, (((0,), (0,)), ((), ())), preferred_element_type=f32)


def _rms(h):
    rs = lax.rsqrt(jnp.mean(h * h, axis=-1, keepdims=True) + EPS)
    return h * rs, rs


def _rms_bwd(dn, g, hhat, rs):
    dhh = dn * g
    return rs * (dhh - hhat * jnp.mean(dhh * hhat, axis=-1, keepdims=True))


def _sigmoid(x):
    return jax.nn.sigmoid(x)


def _me():
    return lax.axis_index("x"), lax.axis_index("y"), lax.axis_index("c")


def _peer(idx):
    return (idx // 4, (idx // 2) % 2, idx % 2)


def all_gather(name, arrays):
    n = len(arrays)

    def body(*refs):
        ins, outs = refs[:n], refs[n:2 * n]
        send_sems, recv_sems, local_sems = refs[2 * n:]
        x, y, c = _me()
        me = 4 * x + 2 * y + c
        locals_ = []
        for k in range(n):
            cp = pltpu.make_async_copy(ins[k], outs[k].at[me], local_sems.at[k])
            cp.start()
            locals_.append(cp)
        for d in range(1, NDEV):
            for k in range(n):
                pltpu.make_async_remote_copy(
                    src_ref=ins[k], dst_ref=outs[k].at[me], send_sem=send_sems.at[k], recv_sem=recv_sems.at[k],
                    device_id=_peer((me + d) % NDEV), device_id_type=MESH).start()
        for k in range(n):
            seven = outs[k].at[pl.ds(0, NDEV - 1)]
            w = pltpu.make_async_remote_copy(src_ref=seven, dst_ref=seven, send_sem=send_sems.at[k],
                                             recv_sem=recv_sems.at[k], device_id=(x, y, c), device_id_type=MESH)
            w.wait_send()
            w.wait_recv()
            locals_[k].wait()

    anyspec = pl.BlockSpec(memory_space=pl.ANY)
    return pl.pallas_call(
        body, name=name,
        out_shape=[jax.ShapeDtypeStruct((NDEV,) + a.shape, a.dtype) for a in arrays],
        in_specs=[anyspec] * n, out_specs=[anyspec] * n,
        scratch_shapes=[pltpu.SemaphoreType.DMA((n,)), pltpu.SemaphoreType.DMA((n,)), pltpu.SemaphoreType.DMA((n,))],
    )(*arrays)


_HBM = pl.BlockSpec(memory_space=pltpu.HBM)
_SEM = pl.BlockSpec(memory_space=pltpu.SEMAPHORE)
_ANY = pl.BlockSpec(memory_space=pl.ANY)
_EFFECT = pltpu.SideEffectType.DATAFLOW_SIDE_EFFECTING


def _in_hbm(a):
    return pltpu.with_memory_space_constraint(a, pltpu.HBM)


def gather_start(name, lands, deps=()):
    n, nd = len(lands), len(deps)

    def body(*refs):
        land = refs[nd:nd + n]
        send_sems, recv_sems = refs[nd + n:nd + n + 2]
        token = refs[-1]
        x, y, c = _me()
        me = 4 * x + 2 * y + c
        for d in range(1, NDEV):
            for k in range(n):
                pltpu.make_async_remote_copy(
                    src_ref=land[k].at[me], dst_ref=land[k].at[me], send_sem=send_sems.at[k], recv_sem=recv_sems.at[k],
                    device_id=_peer((me + d) % NDEV), device_id_type=MESH).start()
        token[...] = jnp.zeros_like(token)

    res = pl.pallas_call(
        body, name=name,
        out_shape=(pltpu.SemaphoreType.DMA((n,)), pltpu.SemaphoreType.DMA((n,)),
                   *[pltpu.HBM(a.shape, a.dtype) for a in lands], jax.ShapeDtypeStruct((8, 128), f32)),
        in_specs=[_ANY] * nd + [_HBM] * n,
        out_specs=(_SEM, _SEM, *[_HBM] * n, pl.BlockSpec(memory_space=pltpu.VMEM)),
        input_output_aliases={nd + k: 2 + k for k in range(n)},
        compiler_params=pltpu.CompilerParams(has_side_effects=_EFFECT),
    )(*deps, *[_in_hbm(a) for a in lands])
    return res[0], res[1], list(res[2:2 + n]), res[-1]


def _other_chips(x, y):
    return [(1 - x, y), (x, 1 - y), (1 - x, 1 - y)]


def gather_start_chips(name, lands, deps=()):
    n, nd = len(lands), len(deps)

    def body(*refs):
        land = refs[nd:nd + n]
        send_sems, recv_sems = refs[nd + n:nd + n + 2]
        token = refs[-1]
        x, y, c = _me()
        me = 4 * x + 2 * y + c
        for k in range(n):
            for to in [(x, y, 1 - c)] + [(cx, cy, c) for cx, cy in _other_chips(x, y)]:
                pltpu.make_async_remote_copy(
                    src_ref=land[k].at[me], dst_ref=land[k].at[me], send_sem=send_sems.at[k], recv_sem=recv_sems.at[k],
                    device_id=to, device_id_type=MESH).start()
        token[...] = jnp.zeros_like(token)

    res = pl.pallas_call(
        body, name=name,
        out_shape=(pltpu.SemaphoreType.DMA((n,)), pltpu.SemaphoreType.DMA((n,)),
                   *[pltpu.HBM(a.shape, a.dtype) for a in lands], jax.ShapeDtypeStruct((8, 128), f32)),
        in_specs=[_ANY] * nd + [_HBM] * n,
        out_specs=(_SEM, _SEM, *[_HBM] * n, pl.BlockSpec(memory_space=pltpu.VMEM)),
        input_output_aliases={nd + k: 2 + k for k in range(n)},
        compiler_params=pltpu.CompilerParams(has_side_effects=_EFFECT),
    )(*deps, *[_in_hbm(a) for a in lands])
    return res[0], res[1], list(res[2:2 + n]), res[-1]


def gather_forward(name, send_sems, recv_sems, lands, after):
    n = len(lands)

    def body(*refs):
        land = refs[:n]
        ssem, rsem = refs[n:n + 2]
        send2, recv2 = refs[n + 3:n + 5]
        token = refs[-1]
        x, y, c = _me()
        for k in range(n):
            four = land[k].at[pl.ds(0, 4)]
            w = pltpu.make_async_remote_copy(src_ref=four, dst_ref=four, send_sem=ssem.at[k], recv_sem=rsem.at[k],
                                             device_id=(x, y, c), device_id_type=MESH)
            w.wait_send()
            w.wait_recv()
            for cx, cy in _other_chips(x, y):
                slot = 4 * cx + 2 * cy + c
                pltpu.make_async_remote_copy(
                    src_ref=land[k].at[slot], dst_ref=land[k].at[slot], send_sem=send2.at[k], recv_sem=recv2.at[k],
                    device_id=(x, y, 1 - c), device_id_type=MESH).start()
        token[...] = jnp.zeros_like(token)

    res = pl.pallas_call(
        body, name=name,
        out_shape=(pltpu.SemaphoreType.DMA((n,)), pltpu.SemaphoreType.DMA((n,)),
                   *[pltpu.HBM(a.shape, a.dtype) for a in lands], jax.ShapeDtypeStruct((8, 128), f32)),
        in_specs=[_HBM] * n + [_SEM, _SEM, _ANY],
        out_specs=(_SEM, _SEM, *[_HBM] * n, pl.BlockSpec(memory_space=pltpu.VMEM)),
        input_output_aliases={k: 2 + k for k in range(n)},
        compiler_params=pltpu.CompilerParams(has_side_effects=_EFFECT),
    )(*lands, send_sems, recv_sems, after)
    return res[0], res[1], list(res[2:2 + n]), res[-1]


def exchange_start(name, grads, deps=()):
    n, nd = len(grads), len(deps)
    lands = [lax.empty((NDEV - 1, g.shape[0]) + g.shape[2:], g.dtype) for g in grads]

    def body(*refs):
        src = refs[nd:nd + n]
        land = refs[nd + n:nd + 2 * n]
        send_sems, recv_sems = refs[nd + 2 * n:nd + 2 * n + 2]
        token = refs[-1]
        x, y, c = _me()
        me = 4 * x + 2 * y + c
        for d in range(1, NDEV):
            p = (me + d) % NDEV
            for k in range(n):
                pltpu.make_async_remote_copy(
                    src_ref=src[k].at[:, p], dst_ref=land[k].at[d - 1], send_sem=send_sems.at[k], recv_sem=recv_sems.at[k],
                    device_id=_peer(p), device_id_type=MESH).start()
        token[...] = jnp.zeros_like(token)

    both = list(grads) + lands
    res = pl.pallas_call(
        body, name=name,
        out_shape=(pltpu.SemaphoreType.DMA((n,)), pltpu.SemaphoreType.DMA((n,)),
                   *[pltpu.HBM(a.shape, a.dtype) for a in both], jax.ShapeDtypeStruct((8, 128), f32)),
        in_specs=[_ANY] * nd + [_HBM] * (2 * n),
        out_specs=(_SEM, _SEM, *[_HBM] * (2 * n), pl.BlockSpec(memory_space=pltpu.VMEM)),
        input_output_aliases={nd + k: 2 + k for k in range(2 * n)},
        compiler_params=pltpu.CompilerParams(has_side_effects=_EFFECT),
    )(*deps, *[_in_hbm(a) for a in both])
    return res[0], res[1], list(res[2:2 + n]), list(res[2 + n:2 + 2 * n]), res[-1]


def copies_wait(name, send_sems, recv_sems, sent, lands, after, count=NDEV - 1):
    ns, n = len(sent), len(lands)

    def body(*refs):
        land = refs[ns:ns + n]
        ssem, rsem = refs[ns + n:ns + n + 2]
        x, y, c = _me()
        for k in range(n):
            seven = land[k].at[pl.ds(0, count)]
            w = pltpu.make_async_remote_copy(src_ref=seven, dst_ref=seven, send_sem=ssem.at[k], recv_sem=rsem.at[k],
                                             device_id=(x, y, c), device_id_type=MESH)
            w.wait_send()
            w.wait_recv()

    both = list(sent) + list(lands)
    res = pl.pallas_call(
        body, name=name, out_shape=tuple(pltpu.HBM(a.shape, a.dtype) for a in both),
        in_specs=[_HBM] * (ns + n) + [_SEM, _SEM, _ANY], out_specs=tuple([_HBM] * (ns + n)),
        input_output_aliases={k: k for k in range(ns + n)},
        compiler_params=pltpu.CompilerParams(has_side_effects=_EFFECT),
    )(*both, send_sems, recv_sems, after)
    return list(res[:ns]), list(res[ns:])


def prep_layer(layer, me, col_sharded, row_sharded):
    nc, nr = len(col_sharded), len(row_sharded)

    def body(me_ref, *refs):
        ins, outs = refs[:nc + nr], refs[nc + nr:]
        for k in range(nc):
            outs[k][...] = ins[k][...].T.astype(bf16)
        for k in range(nc, nc + nr):
            outs[k][...] = ins[k][...].astype(bf16)

    arrs = list(col_sharded) + list(row_sharded)
    in_specs = [pl.BlockSpec((None,) + a.shape[1:], lambda i, me_ref: (layer, 0, 0)) for a in arrs]
    shapes = [(a.shape[2], a.shape[1]) for a in col_sharded] + [a.shape[1:] for a in row_sharded]
    out_specs = [pl.BlockSpec((None,) + s, lambda i, me_ref: (me_ref[0], 0, 0)) for s in shapes]
    return pl.pallas_call(
        body, name=f"prep_layer{layer}",
        grid_spec=pltpu.PrefetchScalarGridSpec(num_scalar_prefetch=1, grid=(1,), in_specs=in_specs, out_specs=out_specs),
        out_shape=[jax.ShapeDtypeStruct((NDEV,) + s, bf16) for s in shapes], compiler_params=_cp(48))(me, *arrs)


def slot_in(name, me, arrays):
    n = len(arrays)

    def body(me_ref, *refs):
        for k in range(n):
            refs[n + k][...] = refs[k][...]

    in_specs = [pl.BlockSpec(a.shape, lambda i, me_ref: (0, 0)) for a in arrays]
    out_specs = [pl.BlockSpec((None,) + a.shape, lambda i, me_ref: (me_ref[0], 0, 0)) for a in arrays]
    return pl.pallas_call(
        body, name=name,
        grid_spec=pltpu.PrefetchScalarGridSpec(num_scalar_prefetch=1, grid=(1,), in_specs=in_specs, out_specs=out_specs),
        out_shape=[jax.ShapeDtypeStruct((NDEV,) + a.shape, a.dtype) for a in arrays])(me, *arrays)


def _ff_chunks(ff, want=768):
    if ff % 256:
        return [slice(0, ff)]
    return [slice(c, min(c + want, ff)) for c in range(0, ff, want)]


def ffn_fwd(name, h, g, wgT, wuT, wd, deps=()):
    tp, d = h.shape
    ff = wgT.shape[0]
    tm = _row_tile(tp, 704)

    def body(h_ref, g_ref, wg_ref, wu_ref, wd_ref, ho_ref, a_ref, b_ref):
        hh = h_ref[...]
        hhat, _ = _rms(hh)
        n = (hhat * g_ref[...]).astype(bf16)
        acc = None
        for cols in _ff_chunks(ff):
            a = _dot_nt(n, wg_ref[cols, :])
            b = _dot_nt(n, wu_ref[cols, :])
            part = _dot_nn(((a * _sigmoid(a)) * b).astype(bf16), wd_ref[cols, :])
            acc = part if acc is None else acc + part
            a_ref[:, cols] = a.astype(bf16)
            b_ref[:, cols] = b.astype(bf16)
        ho_ref[...] = hh + 0.5 * acc

    row = lambda w: pl.BlockSpec((tm, w), lambda i: (i, 0))
    return pl.pallas_call(
        _skip(len(deps), body), name=name, grid=(tp // tm,),
        in_specs=[_ANY] * len(deps) + [row(d), _resident((1, d)), _resident((ff, d)), _resident((ff, d)), _resident((ff, d))],
        out_specs=[row(d), row(ff), row(ff)],
        out_shape=[jax.ShapeDtypeStruct((tp, d), f32), jax.ShapeDtypeStruct((tp, ff), bf16),
                   jax.ShapeDtypeStruct((tp, ff), bf16)],
        compiler_params=_cp(56, ("arbitrary",)))(*deps, h, g, wgT, wuT, wd)


def ffn_bwd(name, dy, h, g, a, b, wgT, wuT, wd, deps=()):
    tp, d = h.shape
    ff = wgT.shape[0]
    tm = _row_tile(tp, 384)

    def body(dy_ref, h_ref, g_ref, a_ref, b_ref, wg_ref, wu_ref, wd_ref, dh_ref, lhs_ref, rhs_ref, dg_ref):
        dyv = dy_ref[...]
        hhat, rs = _rms(h_ref[...])
        gv = g_ref[...]
        n = hhat * gv
        dyh = (0.5 * dyv).astype(bf16)
        dn = None
        for cols in _ff_chunks(ff):
            ds = _dot_nt(dyh, wd_ref[cols, :])
            av = a_ref[:, cols].astype(f32)
            bv = b_ref[:, cols].astype(f32)
            sg = _sigmoid(av)
            sa = av * sg
            da = (ds * bv * (sg * (1.0 + av * (1.0 - sg)))).astype(bf16)
            db = (ds * sa).astype(bf16)
            part = _dot_nn(da, wg_ref[cols, :]) + _dot_nn(db, wu_ref[cols, :])
            dn = part if dn is None else dn + part
            lhs_ref[0, :, cols] = da
            lhs_ref[1, :, cols] = db
            lhs_ref[2, :, cols] = (sa * bv).astype(bf16)
        dh_ref[...] = dyv + _rms_bwd(dn, gv, hhat, rs)

        @pl.when(pl.program_id(0) == 0)
        def _():
            dg_ref[...] = jnp.zeros_like(dg_ref)

        dg_ref[0:1, :] += jnp.sum(dn * hhat, axis=0, keepdims=True)
        rhs_ref[0] = n.astype(bf16)
        rhs_ref[1] = dyh

    row = lambda w: pl.BlockSpec((tm, w), lambda i: (i, 0))
    return pl.pallas_call(
        _skip(len(deps), body), name=name, grid=(tp // tm,),
        in_specs=[_ANY] * len(deps) + [row(d), row(d), _resident((1, d)), row(ff), row(ff),
                  _resident((ff, d)), _resident((ff, d)), _resident((ff, d))],
        out_specs=[row(d), pl.BlockSpec((3, tm, ff), lambda i: (0, i, 0)), pl.BlockSpec((2, tm, d), lambda i: (0, i, 0)),
                   pl.BlockSpec((8, d), lambda i: (0, 0))],
        out_shape=[jax.ShapeDtypeStruct((tp, d), f32), jax.ShapeDtypeStruct((3, tp, ff), bf16),
                   jax.ShapeDtypeStruct((2, tp, d), bf16), jax.ShapeDtypeStruct((8, d), f32)],
        compiler_params=_cp(58, ("arbitrary",)))(*deps, dy, h, g, a, b, wgT, wuT, wd)


def mm_tn(name, lhs, rhs, rhs_of, deps=(), only=None):
    _, tp, m = lhs.shape
    b0, nb = (0, lhs.shape[0]) if only is None else (only, 1)
    n = rhs.shape[2]
    fixed = m * n * (4 + 2 * 2)
    tk = next((t for t in (1408, 704, 384) if tp % t == 0 and fixed + 2 * t * (m + n) * 2 <= 46 * 2**20), 128)
    nk = tp // tk

    def body(l_ref, r_ref, o_ref, acc_ref):
        k = pl.program_id(1)

        @pl.when(k == 0)
        def _():
            acc_ref[...] = jnp.zeros_like(acc_ref)

        acc_ref[...] += _dot_tn(l_ref[...], r_ref[...])

        @pl.when(k == nk - 1)
        def _():
            o_ref[...] = acc_ref[...].astype(o_ref.dtype)

    return pl.pallas_call(
        _skip(len(deps), body), name=name, grid=(nb, nk),
        in_specs=[_ANY] * len(deps) + [pl.BlockSpec((None, tk, m), lambda b, k: (b0 + b, k, 0)),
                                       pl.BlockSpec((None, tk, n), lambda b, k: (rhs_of(b0 + b), k, 0))],
        out_specs=pl.BlockSpec((None, m, n), lambda b, k: (b, 0, 0)),
        out_shape=jax.ShapeDtypeStruct((nb, m, n), bf16),
        scratch_shapes=[pltpu.VMEM((m, n), f32)],
        compiler_params=_cp(56, ("arbitrary", "arbitrary")))(*deps, lhs, rhs)


def mix_in_fwd(name, h, g, winT):
    tp, d = h.shape
    nin = winT.shape[0]
    tm = _row_tile(tp)

    def body(h_ref, g_ref, w_ref, zq_ref, zg_ref, zu_ref, zgate_ref):
        hhat, _ = _rms(h_ref[...])
        z = _dot_nt((hhat * g_ref[...]).astype(bf16), w_ref[...])
        zq_ref[...] = z[:, :3 * RW].astype(zq_ref.dtype)
        zg_ref[...] = z[:, 3 * RW:4 * RW].astype(zg_ref.dtype)
        zu_ref[...] = z[:, 4 * RW:5 * RW]
        zgate_ref[...] = z[:, 5 * RW:].astype(zgate_ref.dtype)

    row = lambda w: pl.BlockSpec((tm, w), lambda i: (i, 0))
    widths = (3 * RW, RW, RW, 2 * d)
    return pl.pallas_call(
        body, name=name, grid=(tp // tm,),
        in_specs=[row(d), _resident((1, d)), _resident((nin, d))],
        out_specs=[row(w) for w in widths],
        out_shape=[jax.ShapeDtypeStruct((tp, w), dt) for w, dt in zip(widths, (bf16, bf16, f32, bf16))],
        compiler_params=_cp(56, ("arbitrary",)))(h, g, winT)


def mix_in_bwd(name, dq, dk, dv, dzg, dzu, dzgate, h, g, winT, dres):
    tp, d = h.shape
    nin = winT.shape[0]
    tm = _row_tile(tp)

    def body(dq_ref, dk_ref, dv_ref, dzg_ref, dzu_ref, dzgate_ref, h_ref, g_ref, w_ref, dres_ref, dh_ref, dz_ref, n_ref, dg_ref):
        dz = jnp.concatenate([dq_ref[...], dk_ref[...], dv_ref[...], dzg_ref[...], dzu_ref[...], dzgate_ref[...]], axis=-1)
        dn = _dot_nn(dz, w_ref[...])
        hhat, rs = _rms(h_ref[...])
        gv = g_ref[...]
        dh_ref[...] = dres_ref[...] + _rms_bwd(dn, gv, hhat, rs)

        @pl.when(pl.program_id(0) == 0)
        def _():
            dg_ref[...] = jnp.zeros_like(dg_ref)

        dg_ref[0:1, :] += jnp.sum(dn * hhat, axis=0, keepdims=True)
        dz_ref[...] = dz
        n_ref[...] = (hhat * gv).astype(bf16)

    row = lambda w: pl.BlockSpec((tm, w), lambda i: (i, 0))
    return pl.pallas_call(
        body, name=name, grid=(tp // tm,),
        in_specs=[row(RW)] * 5 + [row(2 * d), row(d), _resident((1, d)), _resident((nin, d)), row(d)],
        out_specs=[row(d), pl.BlockSpec((None, tm, nin), lambda i: (0, i, 0)), pl.BlockSpec((None, tm, d), lambda i: (0, i, 0)),
                   pl.BlockSpec((8, d), lambda i: (0, 0))],
        out_shape=[jax.ShapeDtypeStruct((tp, d), f32), jax.ShapeDtypeStruct((1, tp, nin), bf16),
                   jax.ShapeDtypeStruct((1, tp, d), bf16), jax.ShapeDtypeStruct((8, d), f32)],
        compiler_params=_cp(56, ("arbitrary",)))(dq, dk, dv, dzg, dzu, dzgate, h, g, winT, dres)


def _retention_tables(tp, pad):
    half = HD // 2
    inv_freq = ROPE_BASE ** (-jnp.arange(half, dtype=f32) / half)
    pos = jnp.arange(tp, dtype=f32) - pad
    ang = pos[:, None] * inv_freq[None, :]
    cos, sin = jnp.cos(ang), jnp.sin(ang)
    cos2 = jnp.concatenate([cos, cos], axis=-1)
    sin2 = jnp.concatenate([-sin, sin], axis=-1)
    log_gamma = jnp.log1p(-(2.0 ** (-5.0 - jnp.arange(HEADS, dtype=f32))))
    idx = jnp.arange(CHUNK, dtype=f32)
    diff = idx[:, None] - idx[None, :]
    intra = jnp.where(diff[None] >= 0, jnp.exp(diff[None] * log_gamma[:, None, None]), 0.0)
    k_decay = jnp.exp((CHUNK - 1.0 - idx)[None, :] * log_gamma[:, None])
    q_decay = jnp.exp((idx + 1.0)[None, :] * log_gamma[:, None])
    chunk_decay = jnp.exp(CHUNK * log_gamma)
    full = (HEADS, CHUNK, HD)
    dec = jnp.stack([intra, jnp.broadcast_to(k_decay[:, :, None], full), jnp.broadcast_to(q_decay[:, :, None], full),
                     jnp.broadcast_to(chunk_decay[:, None, None], full)], axis=1)
    return cos2, sin2, dec


def _rot(t, cos2, sin2):
    return t * cos2 + pltpu.roll(t, HD // 2, 1) * sin2


def _rot_t(t, cos2, sin2):
    return t * cos2 - pltpu.roll(t, HD // 2, 1) * sin2


def _chunks_per_step(nch):
    return 3 if nch % 3 == 0 else 1


def retention_fwd(name, zq, cos2, sin2, dec):
    tp = zq.shape[0]
    nch = tp // CHUNK
    per = _chunks_per_step(nch)
    scale = HD ** -0.5

    def body(q_ref, k_ref, v_ref, cos_ref, sin_ref, dec_ref, out_ref, st_ref, s_ref):
        @pl.when(pl.program_id(0) == 0)
        def _():
            s_ref[...] = jnp.zeros_like(s_ref)

        state = [s_ref[hh] for hh in range(HEADS)]
        for j in range(per):
            rows = slice(j * CHUNK, (j + 1) * CHUNK)
            cosv, sinv = cos_ref[rows, :], sin_ref[rows, :]
            for hh in range(HEADS):
                cols = slice(hh * HD, (hh + 1) * HD)
                qr = _rot(q_ref[rows, cols].astype(f32), cosv, sinv) * scale
                kr = _rot(k_ref[rows, cols].astype(f32), cosv, sinv)
                vb = v_ref[rows, cols].astype(bf16)
                sc = (_dot_nt(qr.astype(bf16), kr.astype(bf16)) * dec_ref[hh, 0]).astype(bf16)
                sb = state[hh].astype(bf16)
                out_ref[rows, cols] = _dot_nn(sc, vb) + _dot_nn((qr * dec_ref[hh, 2]).astype(bf16), sb)
                st_ref[hh, j] = sb
                state[hh] = state[hh] * dec_ref[hh, 3] + _dot_tn((kr * dec_ref[hh, 1]).astype(bf16), vb)
        for hh in range(HEADS):
            s_ref[hh] = state[hh]

    part = lambda j: pl.BlockSpec((per * CHUNK, RW), lambda n: (n, j))
    table = pl.BlockSpec((per * CHUNK, HD), lambda n: (n, 0))
    return pl.pallas_call(
        body, name=name, grid=(nch // per,),
        in_specs=[part(0), part(1), part(2), table, table, _resident((HEADS, 4, CHUNK, HD))],
        out_specs=[part(0), pl.BlockSpec((HEADS, per, HD, HD), lambda n: (0, n, 0, 0))],
        out_shape=[jax.ShapeDtypeStruct((tp, RW), f32), jax.ShapeDtypeStruct((HEADS, nch, HD, HD), bf16)],
        scratch_shapes=[pltpu.VMEM((HEADS, HD, HD), f32)],
        compiler_params=_cp(32, ("arbitrary",)))(zq, zq, zq, cos2, sin2, dec)


def retention_bwd(name, zq, cos2, sin2, dec, states, dout, pad, deps=()):
    tp = zq.shape[0]
    nch = tp // CHUNK
    per = _chunks_per_step(nch)
    nblk = nch // per
    scale = HD ** -0.5

    def body(q_ref, k_ref, v_ref, cos_ref, sin_ref, dec_ref, st_ref, do_ref, dq_ref, dk_ref, dv_ref, g_ref):
        @pl.when(pl.program_id(0) == 0)
        def _():
            g_ref[...] = jnp.zeros_like(g_ref)

        first_row = (nblk - 1 - pl.program_id(0)) * (per * CHUNK)
        gstate = [g_ref[hh] for hh in range(HEADS)]
        for j in reversed(range(per)):
            rows = slice(j * CHUNK, (j + 1) * CHUNK)
            cosv, sinv = cos_ref[rows, :], sin_ref[rows, :]
            keep = (lax.broadcasted_iota(jnp.int32, (CHUNK, HD), 0) + (first_row + j * CHUNK)) >= pad
            for hh in range(HEADS):
                cols = slice(hh * HD, (hh + 1) * HD)
                intra, kdec, qdec = dec_ref[hh, 0], dec_ref[hh, 1], dec_ref[hh, 2]
                qr = _rot(q_ref[rows, cols].astype(f32), cosv, sinv) * scale
                kr = _rot(k_ref[rows, cols].astype(f32), cosv, sinv)
                qb, kb = qr.astype(bf16), kr.astype(bf16)
                vb = v_ref[rows, cols].astype(bf16)
                qd = (qr * qdec).astype(bf16)
                kd = (kr * kdec).astype(bf16)
                sc = (_dot_nt(qb, kb) * intra).astype(bf16)
                dob = do_ref[rows, cols]
                sb = st_ref[hh, j]
                gb = gstate[hh].astype(bf16)
                dsc = (_dot_nt(dob, vb) * intra).astype(bf16)
                dv = _dot_tn(sc, dob) + _dot_nn(kd, gb)
                dqr = _dot_nn(dsc, kb) + _dot_nt(dob, sb) * qdec
                dkr = _dot_tn(dsc, qb) + _dot_nt(vb, gb) * kdec
                gstate[hh] = gstate[hh] * dec_ref[hh, 3] + _dot_tn(qd, dob)
                dq_ref[rows, cols] = jnp.where(keep, _rot_t(dqr * scale, cosv, sinv), 0.0).astype(bf16)
                dk_ref[rows, cols] = jnp.where(keep, _rot_t(dkr, cosv, sinv), 0.0).astype(bf16)
                dv_ref[rows, cols] = jnp.where(keep, dv, 0.0).astype(bf16)
        for hh in range(HEADS):
            g_ref[hh] = gstate[hh]

    part = lambda j: pl.BlockSpec((per * CHUNK, RW), lambda t: (nblk - 1 - t, j))
    table = pl.BlockSpec((per * CHUNK, HD), lambda t: (nblk - 1 - t, 0))
    return pl.pallas_call(
        _skip(len(deps), body), name=name, grid=(nblk,),
        in_specs=[_ANY] * len(deps) + [part(0), part(1), part(2), table, table, _resident((HEADS, 4, CHUNK, HD)),
                                       pl.BlockSpec((HEADS, per, HD, HD), lambda t: (0, nblk - 1 - t, 0, 0)), part(0)],
        out_specs=[part(0)] * 3,
        out_shape=[jax.ShapeDtypeStruct((tp, RW), bf16)] * 3,
        scratch_shapes=[pltpu.VMEM((HEADS, HD, HD), f32)],
        compiler_params=_cp(32, ("arbitrary",)))(*deps, zq, zq, zq, cos2, sin2, dec, states, dout)


def _window_sum(xv, steps, tp, forward):
    s = xv
    for j in range(steps):
        sh = 2 ** j
        s = s + pltpu.roll(s, (tp - sh) if forward else sh, 0)
    return s


def pool_fwd(name, zu, maps, scale, pad):
    tp = zu.shape[0]

    def body(u_ref, maps_ref, scale_ref, pooled_ref, p_ref):
        row = lax.broadcasted_iota(jnp.int32, (tp, HD), 0)
        for gi, w in enumerate(POOL_WINDOWS):
            cols = slice(gi * HD, (gi + 1) * HD)
            xv = u_ref[:, cols]
            cnt = jnp.clip(row - (pad - 1), 1, w).astype(f32)
            pooled = jnp.where(row >= pad, _window_sum(xv, gi + 1, tp, False) / cnt - xv, 0.0).astype(bf16)
            pooled_ref[:, cols] = pooled
            p_ref[:, cols] = (_dot_nn(pooled, maps_ref[gi].astype(bf16)) * scale_ref[:, cols]).astype(bf16)

    return pl.pallas_call(
        body, name=name,
        out_shape=[jax.ShapeDtypeStruct((tp, RW), bf16), jax.ShapeDtypeStruct((tp, RW), bf16)],
        compiler_params=_cp(56))(zu, maps, scale)


def pool_bwd(name, dp, pooled, maps, scale, pad):
    tp = dp.shape[0]

    def body(dp_ref, pooled_ref, maps_ref, scale_ref, du_ref, dmaps_ref, dscale_ref):
        row = lax.broadcasted_iota(jnp.int32, (tp, HD), 0)
        dscale_ref[...] = jnp.zeros_like(dscale_ref)
        for gi, w in enumerate(POOL_WINDOWS):
            cols = slice(gi * HD, (gi + 1) * HD)
            mb = maps_ref[gi].astype(bf16)
            pooled = pooled_ref[:, cols]
            dpf = dp_ref[:, cols].astype(f32)
            dscale_ref[0:1, cols] = jnp.sum(dpf * _dot_nn(pooled, mb), axis=0, keepdims=True)
            dpm = (dpf * scale_ref[:, cols]).astype(bf16)
            dmaps_ref[gi * HD:(gi + 1) * HD, :] = _dot_tn(pooled, dpm)
            dpool = jnp.where(row >= pad, _dot_nt(dpm, mb), 0.0)
            cnt = jnp.clip(row - (pad - 1), 1, w).astype(f32)
            du = _window_sum(dpool / cnt, gi + 1, tp, True) - dpool
            du_ref[:, cols] = jnp.where(row >= pad, du, 0.0).astype(bf16)

    return pl.pallas_call(
        body, name=name,
        out_shape=[jax.ShapeDtypeStruct((tp, RW), bf16), jax.ShapeDtypeStruct((HEADS * HD, HD), f32),
                   jax.ShapeDtypeStruct((8, RW), f32)],
        compiler_params=_cp(56))(dp, pooled, maps, scale)


def _group_norm(o):
    mu = jnp.mean(o, axis=-1, keepdims=True)
    oc = o - mu
    rstd = lax.rsqrt(jnp.mean(oc * oc, axis=-1, keepdims=True) + EPS)
    return oc * rstd, rstd


def mix_out_fwd(name, h, oraw, zg, zgate, p, wretT, wpoolT, wout, deps=()):
    tp, d = h.shape
    tm = _row_tile(tp)

    def body(h_ref, o_ref, zg_ref, zgate_ref, p_ref, wr_ref, wp_ref, wo_ref, ho_ref, rp_ref, ret_ref, pool_ref, mixed_ref):
        parts = []
        for hh in range(HEADS):
            cols = slice(hh * HD, (hh + 1) * HD)
            rhat, _ = _group_norm(o_ref[:, cols])
            gv = zg_ref[:, cols].astype(f32)
            parts.append(rhat * (gv * _sigmoid(gv)))
        r = jnp.concatenate(parts, axis=-1).astype(bf16)
        pv = p_ref[...]
        ret = _dot_nt(r, wr_ref[...])
        pool = _dot_nt(pv, wp_ref[...])
        mixed = (_sigmoid(zgate_ref[:, :d].astype(f32)) * ret + _sigmoid(zgate_ref[:, d:].astype(f32)) * pool).astype(bf16)
        ho_ref[...] = h_ref[...] + _dot_nn(mixed, wo_ref[...])
        rp_ref[0] = r
        rp_ref[1] = pv
        ret_ref[...] = ret.astype(bf16)
        pool_ref[...] = pool.astype(bf16)
        mixed_ref[...] = mixed

    row = lambda w: pl.BlockSpec((tm, w), lambda i: (i, 0))
    return pl.pallas_call(
        _skip(len(deps), body), name=name, grid=(tp // tm,),
        in_specs=[_ANY] * len(deps) + [row(d), row(RW), row(RW), row(2 * d), row(RW), _resident((d, RW)), _resident((d, RW)),
                                       _resident((d, d))],
        out_specs=[row(d), pl.BlockSpec((2, tm, RW), lambda i: (0, i, 0)), row(d), row(d),
                   pl.BlockSpec((None, tm, d), lambda i: (0, i, 0))],
        out_shape=[jax.ShapeDtypeStruct((tp, d), f32), jax.ShapeDtypeStruct((2, tp, RW), bf16),
                   jax.ShapeDtypeStruct((tp, d), bf16), jax.ShapeDtypeStruct((tp, d), bf16),
                   jax.ShapeDtypeStruct((1, tp, d), bf16)],
        compiler_params=_cp(48, ("arbitrary",)))(*deps, h, oraw, zg, zgate, p, wretT, wpoolT, wout)


def mix_out_bwd(name, dy, oraw, zg, zgate, ret, pool, wretT, wpoolT, wout, deps=()):
    tp, d = dy.shape
    tm = _row_tile(tp)

    def body(dy_ref, o_ref, zg_ref, zgate_ref, ret_ref, pool_ref, wr_ref, wp_ref, wo_ref,
             do_ref, dzg_ref, dzgate_ref, dp_ref, drp_ref, dyb_ref):
        dyb = dy_ref[...].astype(bf16)
        dmixed = _dot_nt(dyb, wo_ref[...])
        sa = _sigmoid(zgate_ref[:, :d].astype(f32))
        sb = _sigmoid(zgate_ref[:, d:].astype(f32))
        dret = dmixed * sa
        dpool = dmixed * sb
        dzgate_ref[:, :d] = (dret * ret_ref[...].astype(f32) * (1.0 - sa)).astype(bf16)
        dzgate_ref[:, d:] = (dpool * pool_ref[...].astype(f32) * (1.0 - sb)).astype(bf16)
        dretb, dpoolb = dret.astype(bf16), dpool.astype(bf16)
        dr = _dot_nn(dretb, wr_ref[...])
        dp_ref[...] = _dot_nn(dpoolb, wp_ref[...]).astype(bf16)
        for hh in range(HEADS):
            cols = slice(hh * HD, (hh + 1) * HD)
            rhat, rstd = _group_norm(o_ref[:, cols])
            gv = zg_ref[:, cols].astype(f32)
            sg = _sigmoid(gv)
            drh = dr[:, cols]
            drhat = drh * (gv * sg)
            dzg_ref[:, cols] = (drh * rhat * (sg * (1.0 + gv * (1.0 - sg)))).astype(bf16)
            do = rstd * (drhat - jnp.mean(drhat, axis=-1, keepdims=True)
                         - rhat * jnp.mean(drhat * rhat, axis=-1, keepdims=True))
            do_ref[:, cols] = do.astype(bf16)
        drp_ref[0] = dretb
        drp_ref[1] = dpoolb
        dyb_ref[...] = dyb

    row = lambda w: pl.BlockSpec((tm, w), lambda i: (i, 0))
    return pl.pallas_call(
        _skip(len(deps), body), name=name, grid=(tp // tm,),
        in_specs=[_ANY] * len(deps) + [row(d), row(RW), row(RW), row(2 * d), row(d), row(d), _resident((d, RW)), _resident((d, RW)),
                  _resident((d, d))],
        out_specs=[row(RW), row(RW), row(2 * d), row(RW), pl.BlockSpec((2, tm, d), lambda i: (0, i, 0)),
                   pl.BlockSpec((None, tm, d), lambda i: (0, i, 0))],
        out_shape=[jax.ShapeDtypeStruct((tp, RW), bf16), jax.ShapeDtypeStruct((tp, RW), bf16),
                   jax.ShapeDtypeStruct((tp, 2 * d), bf16), jax.ShapeDtypeStruct((tp, RW), bf16),
                   jax.ShapeDtypeStruct((2, tp, d), bf16), jax.ShapeDtypeStruct((1, tp, d), bf16)],
        compiler_params=_cp(48, ("arbitrary",)))(*deps, dy, oraw, zg, zgate, ret, pool, wretT, wpoolT, wout)


def final_loss(name, h, g, target):
    tp, d = h.shape
    tm = _row_tile(tp)
    nsub = tm // CHUNK

    def body(h_ref, g_ref, *rest):
        t_refs = rest[:nsub]
        dh_ref, loss_ref, dg_ref = rest[nsub:]
        i = pl.program_id(0)

        @pl.when(i == 0)
        def _():
            loss_ref[...] = jnp.zeros_like(loss_ref)
            dg_ref[...] = jnp.zeros_like(dg_ref)

        gv = g_ref[...]
        for j in range(nsub):
            rows = slice(j * CHUNK, (j + 1) * CHUNK)
            hhat, rs = _rms(h_ref[rows, :])
            err = jnp.where(i * nsub + j >= 1, hhat * gv - t_refs[j][...], 0.0)
            dyv = err / d
            dh_ref[rows, :] = _rms_bwd(dyv, gv, hhat, rs)
            loss_ref[...] += 0.5 * jnp.sum(jnp.sum(err * err, axis=-1, keepdims=True) / d)
            dg_ref[0:1, :] += jnp.sum(dyv * hhat, axis=0, keepdims=True)

    lagged = lambda j: pl.BlockSpec((CHUNK, d), lambda i: (jnp.maximum(i * nsub + j - 1, 0), 0))
    return pl.pallas_call(
        body, name=name, grid=(tp // tm,),
        in_specs=[pl.BlockSpec((tm, d), lambda i: (i, 0)), _resident((1, d))] + [lagged(j) for j in range(nsub)],
        out_specs=[pl.BlockSpec((tm, d), lambda i: (i, 0)), pl.BlockSpec((8, 128), lambda i: (0, 0)),
                   pl.BlockSpec((8, d), lambda i: (0, 0))],
        out_shape=[jax.ShapeDtypeStruct((tp, d), f32), jax.ShapeDtypeStruct((8, 128), f32),
                   jax.ShapeDtypeStruct((8, d), f32)],
        compiler_params=_cp(32, ("arbitrary",)))(h, g, *[target] * nsub)


def _adamw(w, g, m, v):
    m = ADAM_B1 * m + (1.0 - ADAM_B1) * g
    v = ADAM_B2 * v + (1.0 - ADAM_B2) * (g * g)
    m_hat = m / (1.0 - ADAM_B1 ** ADAM_STEP)
    v_hat = v / (1.0 - ADAM_B2 ** ADAM_STEP)
    delta = -ADAM_LR * (m_hat / (jnp.sqrt(v_hat) + ADAM_EPS) + ADAM_WD * w)
    return delta, m, v


def adam_big(name, me, recv, own, b, layer, transposed, w, m, v, prev):
    r, c = recv.shape[2:]
    wshape = w.shape[1:]
    nchunk = 1 if transposed else next(k for k in (4, 2, 1) if r % (16 * k) == 0)
    rc = r // nchunk

    def body(me_ref, recv_ref, own_ref, w_ref, m_ref, v_ref, *rest):
        g_ref, d_ref, nm_ref, nv_ref = rest[-4:]
        g = own_ref[...].astype(f32)
        for j in range(NDEV - 1):
            g = g + recv_ref[j].astype(f32)
        if transposed:
            g = g.T
        delta, nm, nv = _adamw(w_ref[...], g, m_ref[...], v_ref[...])
        g_ref[...] = g
        d_ref[...] = delta
        nm_ref[...] = nm
        nv_ref[...] = nv

    wblock = wshape if transposed else (rc, c)
    wspec = pl.BlockSpec((None,) + wblock, lambda i, me_ref: (layer, i, 0))
    in_specs = [pl.BlockSpec((NDEV - 1, None, rc, c), lambda i, me_ref: (0, b, i, 0)),
                pl.BlockSpec((None, None, rc, c), lambda i, me_ref: (b, me_ref[0], i, 0)), wspec, wspec, wspec]
    args = [recv, own, w, m, v]
    aliases = {}
    if prev is not None:
        in_specs += [_ANY] * 4
        args += list(prev)
        aliases = {6 + k: k for k in range(4)}
    return pl.pallas_call(
        body, name=name,
        grid_spec=pltpu.PrefetchScalarGridSpec(num_scalar_prefetch=1, grid=(nchunk,), in_specs=in_specs,
                                               out_specs=[wspec] * 4),
        out_shape=[jax.ShapeDtypeStruct(w.shape, f32)] * 4, input_output_aliases=aliases,
        compiler_params=_cp(56))(me, *args)


def adam_small(name, ga0, gmaps0, gmeta, ga1, gmaps1, norms, pool_scale, pool_maps, meta, final_norm, d):
    def body(ga0_ref, gmaps0_ref, gmeta_ref, ga1_ref, gmaps1_ref, *refs):
        ins, outs = refs[:21], refs[21:]
        x, y, c = _me()
        me = 4 * x + 2 * y + c

        def total(ref, rows):
            t = ref[0, rows, :]
            for j in range(1, NDEV):
                t = t + ref[j, rows, :]
            return t

        row = lambda r: slice(r, r + 1)
        outs[0][...] = jnp.broadcast_to(total(ga1_ref, row(0))[:, :128], (8, 128))

        def update(k, g, o):
            w_ref, m_ref, v_ref = ins[3 * k:3 * k + 3]
            delta, nm, nv = _adamw(w_ref[...], g, m_ref[...], v_ref[...])
            for ref, val in zip(outs[o:o + 4], (g, delta, nm, nv)):
                ref[...] = val

        two = lax.broadcasted_iota(jnp.int32, (2, d), 0)
        for k in range(3):
            update(k, jnp.where(two == 0, total(ga0_ref, row(k)), total(ga1_ref, row(2 + k))), 1 + 4 * k)
        update(3, jnp.where(two[:, :RW] == 0, total(ga0_ref, row(3))[:, :RW], total(ga1_ref, row(5))[:, :RW]), 13)
        update(4, jnp.concatenate([total(gmaps0_ref, slice(None)), total(gmaps1_ref, slice(None))], axis=0), 17)
        update(5, total(gmeta_ref, pl.ds(pl.multiple_of(me * N_META, N_META), N_META)), 21)
        update(6, total(ga1_ref, row(1)), 25)

    flat = []
    for trip in (*norms, pool_scale, pool_maps, meta, final_norm):
        flat += list(trip)
    out_shapes = [jax.ShapeDtypeStruct((8, 128), f32)]
    for trip in (*norms, pool_scale, pool_maps, meta, final_norm):
        out_shapes += [jax.ShapeDtypeStruct(trip[0].shape, f32)] * 4
    return pl.pallas_call(body, name=name, out_shape=out_shapes,
                          compiler_params=_cp(32))(ga0, gmaps0, gmeta, ga1, gmaps1, *flat)


def kernel(x, meta, ffn1_norm, ffn1_gate, ffn1_up, ffn1_down, mix_norm, w_in, pool_maps, pool_scale, w_ret_up, w_pool_up, w_out, ffn2_norm, ffn2_gate, ffn2_up, ffn2_down, final_norm, loss_target, m_meta, m_ffn1_norm, m_ffn1_gate, m_ffn1_up, m_ffn1_down, m_mix_norm, m_w_in, m_pool_maps, m_pool_scale, m_w_ret_up, m_w_pool_up, m_w_out, m_ffn2_norm, m_ffn2_gate, m_ffn2_up, m_ffn2_down, m_final_norm, v_meta, v_ffn1_norm, v_ffn1_gate, v_ffn1_up, v_ffn1_down, v_mix_norm, v_w_in, v_pool_maps, v_pool_scale, v_w_ret_up, v_w_pool_up, v_w_out, v_ffn2_norm, v_ffn2_gate, v_ffn2_up, v_ffn2_down, v_final_norm):
    seq, d = x.shape[1], x.shape[2]
    depth = ffn1_gate.shape[0]
    ff = ffn1_gate.shape[2] * NDEV
    nin = w_in.shape[2] * NDEV
    length = seq + N_META
    pad = (-length) % CHUNK
    tp = length + pad
    assert pad % 8 == 0 and pad + N_META == CHUNK and depth == 2 and nin == 5 * RW + 2 * d

    ix, iy, ic = _me()
    me = (4 * ix + 2 * iy + ic).astype(jnp.int32).reshape(1)

    meta_all, = all_gather("gather_meta", [meta])
    meta_full = jnp.transpose(meta_all, (1, 0, 2)).reshape(N_META, d)

    gathers = {}
    token = meta_all
    tview = lambda *arrs: [jnp.swapaxes(a, 1, 2) for a in arrs]
    t_g1, t_u1, t_g2, t_u2, t_in = (tview(w, m, v) for w, m, v in (
        (ffn1_gate, m_ffn1_gate, v_ffn1_gate), (ffn1_up, m_ffn1_up, v_ffn1_up), (ffn2_gate, m_ffn2_gate, v_ffn2_gate),
        (ffn2_up, m_ffn2_up, v_ffn2_up), (w_in, m_w_in, v_w_in)))
    for layer in range(depth):
        lands = prep_layer(layer, me, [w_ret_up, w_pool_up],
                           [t_g1[0], t_u1[0], t_g2[0], t_u2[0], t_in[0], ffn1_down, ffn2_down, w_out])
        wretT, wpoolT, g1T, u1T, g2T, u2T, winT, d1, d2, wout = lands
        for part, group in (("ffn1", [g1T, u1T, d1]), ("mix", [winT, wretT, wpoolT, wout]), ("ffn2", [g2T, u2T, d2])):
            ssem, rsem, group, token = gather_start_chips(f"gather_start_{part}{layer}", group, (token,))
            gathers[(part, layer)] = (ssem, rsem, group)

    def forward(part, layer, after):
        ssem, rsem, group = gathers[(part, layer)]
        ssem, rsem, group, tok = gather_forward(f"gather_forward_{part}{layer}", ssem, rsem, group, after)
        gathers[(part, layer)] = (ssem, rsem, group)
        return tok

    def gathered(part, layer, after):
        ssem, rsem, group = gathers[(part, layer)]
        _, full = copies_wait(f"gather_wait_{part}{layer}", ssem, rsem, (), group, after, 3)
        return [a.reshape((NDEV * a.shape[1],) + a.shape[2:]) for a in full]

    cos2, sin2, dec = _retention_tables(tp, pad)
    h = jnp.concatenate([jnp.zeros((pad, d), f32), meta_full, x[0]], axis=0)

    saved = []
    weights = []
    tok = forward("ffn1", 0, token)
    for layer in range(depth):
        row = lambda a: a[layer:layer + 1]
        s = {"h0": h}
        g1T, u1T, d1 = gathered("ffn1", layer, tok if layer == 0 else h)
        tok = forward("mix", layer, h) if layer else None
        h, s["a1"], s["b1"] = ffn_fwd(f"ffn1_fwd{layer}", h, row(ffn1_norm), g1T, u1T, d1, (tok,) if layer else ())
        s["h1"] = h
        if layer == 0:
            tok = forward("mix", layer, h)
        winT, wretT, wpoolT, wout = gathered("mix", layer, tok if layer == 0 else h)
        s["zq"], s["zg"], zu, s["zgate"] = mix_in_fwd(f"mix_in_fwd{layer}", h, row(mix_norm), winT)
        s["oraw"], s["states"] = retention_fwd(f"retention_fwd{layer}", s["zq"], cos2, sin2, dec)
        s["pooled"], p = pool_fwd(f"pool_fwd{layer}", zu, pool_maps[layer], row(pool_scale), pad)
        tok = forward("ffn2", layer, p)
        h, s["rp"], s["ret"], s["pool"], s["mixed"] = mix_out_fwd(
            f"mix_out_fwd{layer}", h, s["oraw"], s["zg"], s["zgate"], p, wretT, wpoolT, wout, (tok,))
        s["h2"] = h
        g2T, u2T, d2 = gathered("ffn2", layer, h)
        tok = (forward("ffn1", layer + 1, h),) if layer + 1 < depth else ()
        h, s["a2"], s["b2"] = ffn_fwd(f"ffn2_fwd{layer}", h, row(ffn2_norm), g2T, u2T, d2, tok)
        saved.append(s)
        weights.append((g1T, u1T, g2T, u2T, winT, wretT, wpoolT, d1, d2, wout))

    dh, loss_part, dg_final = final_loss("final_loss", h, final_norm.reshape(1, d), loss_target[0])

    small = {}
    small_gathers = {}
    exchanges = {}
    token = None

    def rows8(vals):
        at = lax.broadcasted_iota(jnp.int32, (8, d), 0)
        out = jnp.zeros((8, d), f32)
        for k, v in enumerate(vals):
            r0 = v[0:1]
            r0 = r0 if r0.shape[1] == d else jnp.pad(r0, ((0, 0), (0, d - r0.shape[1])))
            out = jnp.where(at == k, r0, out)
        return out

    def exchange(part, layer, grads):
        by_dest = [g.reshape(g.shape[0], NDEV, g.shape[1] // NDEV, g.shape[2]) for g in grads]
        ssem, rsem, sent, lands, tok = exchange_start(f"exchange_start_{part}{layer}", by_dest)
        exchanges[(part, layer)] = (ssem, rsem, sent, lands)
        return (tok,)

    for layer in reversed(range(depth)):
        g1T, u1T, g2T, u2T, winT, wretT, wpoolT, d1, d2, wout = weights[layer]
        row = lambda a: a[layer:layer + 1]
        s = saved[layer]
        dh, lhs2, rhs2, small[("ffn2", layer)] = ffn_bwd(
            f"ffn2_bwd{layer}", dh, s["h2"], row(ffn2_norm), s["a2"], s["b2"], g2T, u2T, d2, () if token is None else token)
        token = exchange("ffn2", layer, [mm_tn(f"ffn2_wgrad{layer}", lhs2, rhs2, lambda b: b // 2)])
        do, dzg, dzgate, dp, drp, dyb = mix_out_bwd(
            f"mix_out_bwd{layer}", dh, s["oraw"], s["zg"], s["zgate"], s["ret"], s["pool"], wretT, wpoolT, wout, token)
        token = exchange("mix", layer, [mm_tn(f"w_out_wgrad{layer}", s["mixed"], dyb, lambda b: b),
                                        mm_tn(f"up_wgrad{layer}", drp, s["rp"], lambda b: b)])
        dq, dk, dv = retention_bwd(f"retention_bwd{layer}", s["zq"], cos2, sin2, dec, s["states"], do, pad, token)
        dzu, small[("maps", layer)], small[("scale", layer)] = pool_bwd(
            f"pool_bwd{layer}", dp, s["pooled"], pool_maps[layer], row(pool_scale), pad)
        dh, dz, n2, small[("mix", layer)] = mix_in_bwd(
            f"mix_in_bwd{layer}", dq, dk, dv, dzg, dzu, dzgate, s["h1"], row(mix_norm), winT, dh)
        token = exchange("w_in", layer, [mm_tn(f"w_in_wgrad{layer}", dz, n2, lambda b: b)])
        dh, lhs1, rhs1, small[("ffn1", layer)] = ffn_bwd(
            f"ffn1_bwd{layer}", dh, s["h0"], row(ffn1_norm), s["a1"], s["b1"], g1T, u1T, d1, token)
        rows = [small[("ffn1", layer)], small[("mix", layer)], small[("ffn2", layer)], small[("scale", layer)]]
        packs = [rows8([loss_part, dg_final] + rows if layer == depth - 1 else rows), small[("maps", layer)]]
        if layer == 0:
            dmeta = dh[pad:CHUNK]
            packs.append(jnp.transpose(dmeta.reshape(N_META, NDEV, d // NDEV), (1, 0, 2)).reshape(NDEV * N_META, d // NDEV))
        ssem, rsem, lands, tok = gather_start(f"small_start{layer}", slot_in(f"small_slot{layer}", me, packs))
        small_gathers[layer] = (ssem, rsem, lands)
        if layer:
            token = exchange("ffn1", layer, [mm_tn(f"ffn1_wgrad{layer}", lhs1, rhs1, lambda b: b // 2, (tok,))])
        else:
            token = (tok,)
            for j, nm in enumerate(("ffn1_gate", "ffn1_up", "ffn1_down")):
                token = exchange(nm, layer, [mm_tn(f"{nm}_wgrad{layer}", lhs1, rhs1, lambda b: b // 2, token, only=j)])

    grad_x = dh[CHUNK:][None]

    big = {}
    after = token[0]
    plans = {
        "ffn2": [("ffn2_gate", 0, 0, False, *t_g2), ("ffn2_up", 0, 1, False, *t_u2),
                 ("ffn2_down", 0, 2, False, ffn2_down, m_ffn2_down, v_ffn2_down)],
        "mix": [("w_out", 0, 0, False, w_out, m_w_out, v_w_out),
                ("w_ret_up", 1, 0, True, w_ret_up, m_w_ret_up, v_w_ret_up),
                ("w_pool_up", 1, 1, True, w_pool_up, m_w_pool_up, v_w_pool_up)],
        "w_in": [("w_in", 0, 0, False, *t_in)],
        "ffn1": [("ffn1_gate", 0, 0, False, *t_g1), ("ffn1_up", 0, 1, False, *t_u1),
                 ("ffn1_down", 0, 2, False, ffn1_down, m_ffn1_down, v_ffn1_down)]}
    for nm, k, b, tr, w, m, v in plans["ffn1"]:
        plans[nm] = [(nm, 0, 0, tr, w, m, v)]
    for layer in reversed(range(depth)):
        for part in ("ffn2", "mix", "w_in") + (("ffn1",) if layer else ("ffn1_gate", "ffn1_up", "ffn1_down")):
            ssem, rsem, sent, lands = exchanges[(part, layer)]
            sent, lands = copies_wait(f"exchange_wait_{part}{layer}", ssem, rsem, sent, lands, after)
            for nm, k, b, tr, w, m, v in plans[part]:
                big[nm] = adam_big(f"adam_{nm}{layer}", me, lands[k], sent[k], b, layer, tr, w, m, v, big.get(nm))
                after = big[nm][0]

    gsmall = []
    for layer in range(depth):
        ssem, rsem, lands = small_gathers[layer]
        gsmall += copies_wait(f"small_wait{layer}", ssem, rsem, (), lands, after)[1]

    maps2 = lambda a: a.reshape(depth * HEADS * HD, HD)
    res = adam_small(
        "adam_small", *gsmall,
        [(ffn1_norm, m_ffn1_norm, v_ffn1_norm), (mix_norm, m_mix_norm, v_mix_norm), (ffn2_norm, m_ffn2_norm, v_ffn2_norm)],
        (pool_scale, m_pool_scale, v_pool_scale), (maps2(pool_maps), maps2(m_pool_maps), maps2(v_pool_maps)),
        (meta, m_meta, v_meta), tuple(a.reshape(1, d) for a in (final_norm, m_final_norm, v_final_norm)), d)
    loss = res[0][0, 0]
    sm = {}
    for k, nm in enumerate(["ffn1_norm", "mix_norm", "ffn2_norm", "pool_scale", "pool_maps", "meta", "final_norm"]):
        sm[nm] = list(res[1 + 4 * k:5 + 4 * k])
    sm["pool_maps"] = [a.reshape(pool_maps.shape) for a in sm["pool_maps"]]
    sm["final_norm"] = [a.reshape(d) for a in sm["final_norm"]]

    names = ["meta", "ffn1_norm", "ffn1_gate", "ffn1_up", "ffn1_down", "mix_norm", "w_in", "pool_maps", "pool_scale",
             "w_ret_up", "w_pool_up", "w_out", "ffn2_norm", "ffn2_gate", "ffn2_up", "ffn2_down", "final_norm"]
    for nm in ("ffn1_gate", "ffn1_up", "ffn2_gate", "ffn2_up", "w_in"):
        big[nm] = tview(*big[nm])
    allw = {**{k: list(v) for k, v in big.items()}, **sm}
    outs = [loss, grad_x]
    for kind in range(4):
        outs += [allw[nm][kind] for nm in names]
    return tuple(outs)
```

```python
import functools

import jax
import jax.numpy as jnp
from jax import lax
from jax.experimental import pallas as pl
from jax.experimental.pallas import tpu as pltpu

f32 = jnp.float32
bf16 = jnp.bfloat16
MESH = pl.DeviceIdType.MESH
NDEV = 8
N_META = 16
HEADS = 4
HD = 128
CHUNK = 128
RW = HEADS * HD
POOL_WINDOWS = (2, 4, 8, 16)
ROPE_BASE = 10000.0
EPS = 1e-6
ADAM_LR = 0.001
ADAM_B1 = 0.9
ADAM_B2 = 0.999
ADAM_EPS = 1e-08
ADAM_WD = 0.01
ADAM_STEP = 10
VMEM_CAP_MB = 60


def _cp(vmem_mb, sem=None):
    return pltpu.CompilerParams(vmem_limit_bytes=min(vmem_mb, VMEM_CAP_MB) * 2**20, dimension_semantics=sem)


def _row_tile(tp, want=384):
    return want if tp % want == 0 else 128


def _resident(shape):
    nd = len(shape)
    return pl.BlockSpec(shape, lambda *_: (0,) * nd, pipeline_mode=pl.Buffered(1))


def _skip(nd, body):
    return (lambda *refs: body(*refs[nd:])) if nd else body


def _dot_nn(a, b):
    return lax.dot_general(a, b, (((1,), (0,)), ((), ())), preferred_element_type=f32)


def _dot_nt(a, b):
    return lax.dot_general(a, b, (((1,), (1,)), ((), ())), preferred_element_type=f32)


def _dot_tn(a, b):
    return lax.dot_general(a, b, (((0,), (0,)), ((), ())), preferred_element_type=f32)


def _rms(h):
    rs = lax.rsqrt(jnp.mean(h * h, axis=-1, keepdims=True) + EPS)
    return h * rs, rs


def _rms_bwd(dn, g, hhat, rs):
    dhh = dn * g
    return rs * (dhh - hhat * jnp.mean(dhh * hhat, axis=-1, keepdims=True))


def _sigmoid(x):
    return jax.nn.sigmoid(x)


def _me():
    return lax.axis_index("x"), lax.axis_index("y"), lax.axis_index("c")


def _peer(idx):
    return (idx // 4, (idx // 2) % 2, idx % 2)


def all_gather(name, arrays):
    n = len(arrays)

    def body(*refs):
        ins, outs = refs[:n], refs[n:2 * n]
        send_sems, recv_sems, local_sems = refs[2 * n:]
        x, y, c = _me()
        me = 4 * x + 2 * y + c
        locals_ = []
        for k in range(n):
            cp = pltpu.make_async_copy(ins[k], outs[k].at[me], local_sems.at[k])
            cp.start()
            locals_.append(cp)
        for d in range(1, NDEV):
            for k in range(n):
                pltpu.make_async_remote_copy(
                    src_ref=ins[k], dst_ref=outs[k].at[me], send_sem=send_sems.at[k], recv_sem=recv_sems.at[k],
                    device_id=_peer((me + d) % NDEV), device_id_type=MESH).start()
        for k in range(n):
            seven = outs[k].at[pl.ds(0, NDEV - 1)]
            w = pltpu.make_async_remote_copy(src_ref=seven, dst_ref=seven, send_sem=send_sems.at[k],
                                             recv_sem=recv_sems.at[k], device_id=(x, y, c), device_id_type=MESH)
            w.wait_send()
            w.wait_recv()
            locals_[k].wait()

    anyspec = pl.BlockSpec(memory_space=pl.ANY)
    return pl.pallas_call(
        body, name=name,
        out_shape=[jax.ShapeDtypeStruct((NDEV,) + a.shape, a.dtype) for a in arrays],
        in_specs=[anyspec] * n, out_specs=[anyspec] * n,
        scratch_shapes=[pltpu.SemaphoreType.DMA((n,)), pltpu.SemaphoreType.DMA((n,)), pltpu.SemaphoreType.DMA((n,))],
    )(*arrays)


_HBM = pl.BlockSpec(memory_space=pltpu.HBM)
_SEM = pl.BlockSpec(memory_space=pltpu.SEMAPHORE)
_ANY = pl.BlockSpec(memory_space=pl.ANY)
_EFFECT = pltpu.SideEffectType.DATAFLOW_SIDE_EFFECTING


def _in_hbm(a):
    return pltpu.with_memory_space_constraint(a, pltpu.HBM)


def gather_start(name, lands, deps=()):
    n, nd = len(lands), len(deps)

    def body(*refs):
        land = refs[nd:nd + n]
        send_sems, recv_sems = refs[nd + n:nd + n + 2]
        token = refs[-1]
        x, y, c = _me()
        me = 4 * x + 2 * y + c
        for d in range(1, NDEV):
            for k in range(n):
                pltpu.make_async_remote_copy(
                    src_ref=land[k].at[me], dst_ref=land[k].at[me], send_sem=send_sems.at[k], recv_sem=recv_sems.at[k],
                    device_id=_peer((me + d) % NDEV), device_id_type=MESH).start()
        token[...] = jnp.zeros_like(token)

    res = pl.pallas_call(
        body, name=name,
        out_shape=(pltpu.SemaphoreType.DMA((n,)), pltpu.SemaphoreType.DMA((n,)),
                   *[pltpu.HBM(a.shape, a.dtype) for a in lands], jax.ShapeDtypeStruct((8, 128), f32)),
        in_specs=[_ANY] * nd + [_HBM] * n,
        out_specs=(_SEM, _SEM, *[_HBM] * n, pl.BlockSpec(memory_space=pltpu.VMEM)),
        input_output_aliases={nd + k: 2 + k for k in range(n)},
        compiler_params=pltpu.CompilerParams(has_side_effects=_EFFECT),
    )(*deps, *[_in_hbm(a) for a in lands])
    return res[0], res[1], list(res[2:2 + n]), res[-1]


def _other_chips(x, y):
    return [(1 - x, y), (x, 1 - y), (1 - x, 1 - y)]


def gather_start_chips(name, groups, deps=()):
    sizes = [len(g) for g in groups]
    lands = [a for g in groups for a in g]
    n, nd, ng = len(lands), len(deps), len(groups)

    def body(*refs):
        land = refs[nd:nd + n]
        sems = refs[nd + n:nd + n + 2 * ng]
        token = refs[-1]
        x, y, c = _me()
        me = 4 * x + 2 * y + c
        k = 0
        for g, size in enumerate(sizes):
            for j in range(size):
                for to in [(x, y, 1 - c)] + [(cx, cy, c) for cx, cy in _other_chips(x, y)]:
                    pltpu.make_async_remote_copy(
                        src_ref=land[k].at[me], dst_ref=land[k].at[me], send_sem=sems[2 * g].at[j],
                        recv_sem=sems[2 * g + 1].at[j], device_id=to, device_id_type=MESH).start()
                k += 1
        token[...] = jnp.zeros_like(token)

    res = pl.pallas_call(
        body, name=name,
        out_shape=(*[pltpu.SemaphoreType.DMA((size,)) for size in sizes for _ in range(2)],
                   *[pltpu.HBM(a.shape, a.dtype) for a in lands], jax.ShapeDtypeStruct((8, 128), f32)),
        in_specs=[_ANY] * nd + [_HBM] * n,
        out_specs=(*[_SEM] * (2 * ng), *[_HBM] * n, pl.BlockSpec(memory_space=pltpu.VMEM)),
        input_output_aliases={nd + k: 2 * ng + k for k in range(n)},
        compiler_params=pltpu.CompilerParams(has_side_effects=_EFFECT),
    )(*deps, *[_in_hbm(a) for a in lands])
    out, k = [], 2 * ng
    for g, size in enumerate(sizes):
        out.append((res[2 * g], res[2 * g + 1], list(res[k:k + size])))
        k += size
    return out, res[-1]


def gather_forward(name, send_sems, recv_sems, lands, after):
    n = len(lands)

    def body(*refs):
        land = refs[:n]
        ssem, rsem = refs[n:n + 2]
        send2, recv2 = refs[n + 3:n + 5]
        token = refs[-1]
        x, y, c = _me()
        for k in range(n):
            four = land[k].at[pl.ds(0, 4)]
            w = pltpu.make_async_remote_copy(src_ref=four, dst_ref=four, send_sem=ssem.at[k], recv_sem=rsem.at[k],
                                             device_id=(x, y, c), device_id_type=MESH)
            w.wait_send()
            w.wait_recv()
            for cx, cy in _other_chips(x, y):
                slot = 4 * cx + 2 * cy + c
                pltpu.make_async_remote_copy(
                    src_ref=land[k].at[slot], dst_ref=land[k].at[slot], send_sem=send2.at[k], recv_sem=recv2.at[k],
                    device_id=(x, y, 1 - c), device_id_type=MESH).start()
        token[...] = jnp.zeros_like(token)

    res = pl.pallas_call(
        body, name=name,
        out_shape=(pltpu.SemaphoreType.DMA((n,)), pltpu.SemaphoreType.DMA((n,)),
                   *[pltpu.HBM(a.shape, a.dtype) for a in lands], jax.ShapeDtypeStruct((8, 128), f32)),
        in_specs=[_HBM] * n + [_SEM, _SEM, _ANY],
        out_specs=(_SEM, _SEM, *[_HBM] * n, pl.BlockSpec(memory_space=pltpu.VMEM)),
        input_output_aliases={k: 2 + k for k in range(n)},
        compiler_params=pltpu.CompilerParams(has_side_effects=_EFFECT),
    )(*lands, send_sems, recv_sems, after)
    return res[0], res[1], list(res[2:2 + n]), res[-1]


def exchange_start(name, grads, deps=()):
    n, nd = len(grads), len(deps)
    lands = [lax.empty((NDEV - 1, g.shape[0]) + g.shape[2:], g.dtype) for g in grads]

    def body(*refs):
        src = refs[nd:nd + n]
        land = refs[nd + n:nd + 2 * n]
        send_sems, recv_sems = refs[nd + 2 * n:nd + 2 * n + 2]
        token = refs[-1]
        x, y, c = _me()
        me = 4 * x + 2 * y + c
        for d in range(1, NDEV):
            p = (me + d) % NDEV
            for k in range(n):
                pltpu.make_async_remote_copy(
                    src_ref=src[k].at[:, p], dst_ref=land[k].at[d - 1], send_sem=send_sems.at[k], recv_sem=recv_sems.at[k],
                    device_id=_peer(p), device_id_type=MESH).start()
        token[...] = jnp.zeros_like(token)

    both = list(grads) + lands
    res = pl.pallas_call(
        body, name=name,
        out_shape=(pltpu.SemaphoreType.DMA((n,)), pltpu.SemaphoreType.DMA((n,)),
                   *[pltpu.HBM(a.shape, a.dtype) for a in both], jax.ShapeDtypeStruct((8, 128), f32)),
        in_specs=[_ANY] * nd + [_HBM] * (2 * n),
        out_specs=(_SEM, _SEM, *[_HBM] * (2 * n), pl.BlockSpec(memory_space=pltpu.VMEM)),
        input_output_aliases={nd + k: 2 + k for k in range(2 * n)},
        compiler_params=pltpu.CompilerParams(has_side_effects=_EFFECT),
    )(*deps, *[_in_hbm(a) for a in both])
    return res[0], res[1], list(res[2:2 + n]), list(res[2 + n:2 + 2 * n]), res[-1]


def copies_wait(name, send_sems, recv_sems, sent, lands, after, count=NDEV - 1):
    ns, n = len(sent), len(lands)

    def body(*refs):
        land = refs[ns:ns + n]
        ssem, rsem = refs[ns + n:ns + n + 2]
        x, y, c = _me()
        for k in range(n):
            seven = land[k].at[pl.ds(0, count)]
            w = pltpu.make_async_remote_copy(src_ref=seven, dst_ref=seven, send_sem=ssem.at[k], recv_sem=rsem.at[k],
                                             device_id=(x, y, c), device_id_type=MESH)
            w.wait_send()
            w.wait_recv()

    both = list(sent) + list(lands)
    res = pl.pallas_call(
        body, name=name, out_shape=tuple(pltpu.HBM(a.shape, a.dtype) for a in both),
        in_specs=[_HBM] * (ns + n) + [_SEM, _SEM, _ANY], out_specs=tuple([_HBM] * (ns + n)),
        input_output_aliases={k: k for k in range(ns + n)},
        compiler_params=pltpu.CompilerParams(has_side_effects=_EFFECT),
    )(*both, send_sems, recv_sems, after)
    return list(res[:ns]), list(res[ns:])


def prep_layer(layer, me, col_sharded, row_sharded):
    nc, nr = len(col_sharded), len(row_sharded)

    def body(me_ref, *refs):
        ins, outs = refs[:nc + nr], refs[nc + nr:]
        for k in range(nc):
            outs[k][...] = ins[k][...].T.astype(bf16)
        for k in range(nc, nc + nr):
            outs[k][...] = ins[k][...].astype(bf16)

    arrs = list(col_sharded) + list(row_sharded)
    in_specs = [pl.BlockSpec((None,) + a.shape[1:], lambda i, me_ref: (layer, 0, 0)) for a in arrs]
    shapes = [(a.shape[2], a.shape[1]) for a in col_sharded] + [a.shape[1:] for a in row_sharded]
    out_specs = [pl.BlockSpec((None,) + s, lambda i, me_ref: (me_ref[0], 0, 0)) for s in shapes]
    return pl.pallas_call(
        body, name=f"prep_layer{layer}",
        grid_spec=pltpu.PrefetchScalarGridSpec(num_scalar_prefetch=1, grid=(1,), in_specs=in_specs, out_specs=out_specs),
        out_shape=[jax.ShapeDtypeStruct((NDEV,) + s, bf16) for s in shapes], compiler_params=_cp(48))(me, *arrs)


def slot_in(name, me, arrays):
    n = len(arrays)

    def body(me_ref, *refs):
        for k in range(n):
            refs[n + k][...] = refs[k][...]

    in_specs = [pl.BlockSpec(a.shape, lambda i, me_ref: (0, 0)) for a in arrays]
    out_specs = [pl.BlockSpec((None,) + a.shape, lambda i, me_ref: (me_ref[0], 0, 0)) for a in arrays]
    return pl.pallas_call(
        body, name=name,
        grid_spec=pltpu.PrefetchScalarGridSpec(num_scalar_prefetch=1, grid=(1,), in_specs=in_specs, out_specs=out_specs),
        out_shape=[jax.ShapeDtypeStruct((NDEV,) + a.shape, a.dtype) for a in arrays])(me, *arrays)


def _ff_chunks(ff, want=768):
    if ff % 256:
        return [slice(0, ff)]
    return [slice(c, min(c + want, ff)) for c in range(0, ff, want)]


def ffn_fwd(name, h, g, wgT, wuT, wd, deps=()):
    tp, d = h.shape
    ff = wgT.shape[0]
    tm = _row_tile(tp, 704)

    def body(h_ref, g_ref, wg_ref, wu_ref, wd_ref, ho_ref, a_ref, b_ref):
        hh = h_ref[...]
        hhat, _ = _rms(hh)
        n = (hhat * g_ref[...]).astype(bf16)
        acc = None
        for cols in _ff_chunks(ff):
            a = _dot_nt(n, wg_ref[cols, :])
            b = _dot_nt(n, wu_ref[cols, :])
            part = _dot_nn(((a * _sigmoid(a)) * b).astype(bf16), wd_ref[cols, :])
            acc = part if acc is None else acc + part
            a_ref[:, cols] = a.astype(bf16)
            b_ref[:, cols] = b.astype(bf16)
        ho_ref[...] = hh + 0.5 * acc

    row = lambda w: pl.BlockSpec((tm, w), lambda i: (i, 0))
    return pl.pallas_call(
        _skip(len(deps), body), name=name, grid=(tp // tm,),
        in_specs=[_ANY] * len(deps) + [row(d), _resident((1, d)), _resident((ff, d)), _resident((ff, d)), _resident((ff, d))],
        out_specs=[row(d), row(ff), row(ff)],
        out_shape=[jax.ShapeDtypeStruct((tp, d), f32), jax.ShapeDtypeStruct((tp, ff), bf16),
                   jax.ShapeDtypeStruct((tp, ff), bf16)],
        compiler_params=_cp(56, ("arbitrary",)))(*deps, h, g, wgT, wuT, wd)


def ffn_bwd(name, dy, h, g, a, b, wgT, wuT, wd, deps=()):
    tp, d = h.shape
    ff = wgT.shape[0]
    tm = _row_tile(tp, 384)

    def body(dy_ref, h_ref, g_ref, a_ref, b_ref, wg_ref, wu_ref, wd_ref, dh_ref, lhs_ref, rhs_ref, dg_ref):
        dyv = dy_ref[...]
        hhat, rs = _rms(h_ref[...])
        gv = g_ref[...]
        n = hhat * gv
        dyh = (0.5 * dyv).astype(bf16)
        dn = None
        for cols in _ff_chunks(ff):
            ds = _dot_nt(dyh, wd_ref[cols, :])
            av = a_ref[:, cols].astype(f32)
            bv = b_ref[:, cols].astype(f32)
            sg = _sigmoid(av)
            sa = av * sg
            da = (ds * bv * (sg * (1.0 + av * (1.0 - sg)))).astype(bf16)
            db = (ds * sa).astype(bf16)
            part = _dot_nn(da, wg_ref[cols, :]) + _dot_nn(db, wu_ref[cols, :])
            dn = part if dn is None else dn + part
            lhs_ref[0, :, cols] = da
            lhs_ref[1, :, cols] = db
            lhs_ref[2, :, cols] = (sa * bv).astype(bf16)
        dh_ref[...] = dyv + _rms_bwd(dn, gv, hhat, rs)

        @pl.when(pl.program_id(0) == 0)
        def _():
            dg_ref[...] = jnp.zeros_like(dg_ref)

        dg_ref[0:1, :] += jnp.sum(dn * hhat, axis=0, keepdims=True)
        rhs_ref[0] = n.astype(bf16)
        rhs_ref[1] = dyh

    row = lambda w: pl.BlockSpec((tm, w), lambda i: (i, 0))
    return pl.pallas_call(
        _skip(len(deps), body), name=name, grid=(tp // tm,),
        in_specs=[_ANY] * len(deps) + [row(d), row(d), _resident((1, d)), row(ff), row(ff),
                  _resident((ff, d)), _resident((ff, d)), _resident((ff, d))],
        out_specs=[row(d), pl.BlockSpec((3, tm, ff), lambda i: (0, i, 0)), pl.BlockSpec((2, tm, d), lambda i: (0, i, 0)),
                   pl.BlockSpec((8, d), lambda i: (0, 0))],
        out_shape=[jax.ShapeDtypeStruct((tp, d), f32), jax.ShapeDtypeStruct((3, tp, ff), bf16),
                   jax.ShapeDtypeStruct((2, tp, d), bf16), jax.ShapeDtypeStruct((8, d), f32)],
        compiler_params=_cp(58, ("arbitrary",)))(*deps, dy, h, g, a, b, wgT, wuT, wd)


def mm_tn(name, lhs, rhs, rhs_of, deps=(), only=None):
    _, tp, m = lhs.shape
    b0, nb = (0, lhs.shape[0]) if only is None else (only, 1)
    n = rhs.shape[2]
    def fits(t, ms):
        return tp % t == 0 and (m // ms) * n * (4 + 2 * 2) + 2 * t * (m // ms + n) * 2 <= 46 * 2**20

    tk, msplit = next(((t, ms) for t in (1408, 704, 384) for ms in (1, 2) if m % (256 * ms) == 0 and fits(t, ms)),
                      (128, 1))
    nk = tp // tk
    mb = m // msplit

    def body(l_ref, r_ref, o_ref, acc_ref):
        k = pl.program_id(2)

        @pl.when(k == 0)
        def _():
            acc_ref[...] = jnp.zeros_like(acc_ref)

        acc_ref[...] += _dot_tn(l_ref[...], r_ref[...])

        @pl.when(k == nk - 1)
        def _():
            o_ref[...] = acc_ref[...].astype(o_ref.dtype)

    return pl.pallas_call(
        _skip(len(deps), body), name=name, grid=(nb, msplit, nk),
        in_specs=[_ANY] * len(deps) + [pl.BlockSpec((None, tk, mb), lambda b, j, k: (b0 + b, k, j)),
                                       pl.BlockSpec((None, tk, n), lambda b, j, k: (rhs_of(b0 + b), k, 0))],
        out_specs=pl.BlockSpec((None, mb, n), lambda b, j, k: (b, j, 0)),
        out_shape=jax.ShapeDtypeStruct((nb, m, n), bf16),
        scratch_shapes=[pltpu.VMEM((mb, n), f32)],
        compiler_params=_cp(56, ("arbitrary", "arbitrary", "arbitrary")))(*deps, lhs, rhs)


def mix_in_fwd(name, h, g, winT, cos2, sin2):
    tp, d = h.shape
    nin = winT.shape[0]
    tm = _row_tile(tp)

    def body(h_ref, g_ref, w_ref, cos_ref, sin_ref, zq_ref, zg_ref, zu_ref, zgate_ref):
        hhat, _ = _rms(h_ref[...])
        z = _dot_nt((hhat * g_ref[...]).astype(bf16), w_ref[...])
        cosv, sinv = cos_ref[...], sin_ref[...]
        for hh in range(HEADS):
            qcols, kcols = slice(hh * HD, (hh + 1) * HD), slice(RW + hh * HD, RW + (hh + 1) * HD)
            zq_ref[:, qcols] = (_rot(z[:, qcols], cosv, sinv) * HD ** -0.5).astype(bf16)
            zq_ref[:, kcols] = _rot(z[:, kcols], cosv, sinv).astype(bf16)
        zq_ref[:, 2 * RW:] = z[:, 2 * RW:3 * RW].astype(bf16)
        zg_ref[...] = z[:, 3 * RW:4 * RW].astype(zg_ref.dtype)
        zu_ref[...] = z[:, 4 * RW:5 * RW]
        zgate_ref[...] = z[:, 5 * RW:].astype(zgate_ref.dtype)

    row = lambda w: pl.BlockSpec((tm, w), lambda i: (i, 0))
    widths = (3 * RW, RW, RW, 2 * d)
    return pl.pallas_call(
        body, name=name, grid=(tp // tm,),
        in_specs=[row(d), _resident((1, d)), _resident((nin, d)), row(HD), row(HD)],
        out_specs=[row(w) for w in widths],
        out_shape=[jax.ShapeDtypeStruct((tp, w), dt) for w, dt in zip(widths, (bf16, bf16, f32, bf16))],
        compiler_params=_cp(56, ("arbitrary",)))(h, g, winT, cos2, sin2)


def mix_in_bwd(name, dq, dk, dv, dzg, dzu, dzgate, h, g, winT, dres):
    tp, d = h.shape
    nin = winT.shape[0]
    tm = _row_tile(tp)

    def body(dq_ref, dk_ref, dv_ref, dzg_ref, dzu_ref, dzgate_ref, h_ref, g_ref, w_ref, dres_ref, dh_ref, dz_ref, n_ref, dg_ref):
        dz = jnp.concatenate([dq_ref[...], dk_ref[...], dv_ref[...], dzg_ref[...], dzu_ref[...], dzgate_ref[...]], axis=-1)
        dn = _dot_nn(dz, w_ref[...])
        hhat, rs = _rms(h_ref[...])
        gv = g_ref[...]
        dh_ref[...] = dres_ref[...] + _rms_bwd(dn, gv, hhat, rs)

        @pl.when(pl.program_id(0) == 0)
        def _():
            dg_ref[...] = jnp.zeros_like(dg_ref)

        dg_ref[0:1, :] += jnp.sum(dn * hhat, axis=0, keepdims=True)
        dz_ref[...] = dz
        n_ref[...] = (hhat * gv).astype(bf16)

    row = lambda w: pl.BlockSpec((tm, w), lambda i: (i, 0))
    return pl.pallas_call(
        body, name=name, grid=(tp // tm,),
        in_specs=[row(RW)] * 5 + [row(2 * d), row(d), _resident((1, d)), _resident((nin, d)), row(d)],
        out_specs=[row(d), pl.BlockSpec((None, tm, nin), lambda i: (0, i, 0)), pl.BlockSpec((None, tm, d), lambda i: (0, i, 0)),
                   pl.BlockSpec((8, d), lambda i: (0, 0))],
        out_shape=[jax.ShapeDtypeStruct((tp, d), f32), jax.ShapeDtypeStruct((1, tp, nin), bf16),
                   jax.ShapeDtypeStruct((1, tp, d), bf16), jax.ShapeDtypeStruct((8, d), f32)],
        compiler_params=_cp(56, ("arbitrary",)))(dq, dk, dv, dzg, dzu, dzgate, h, g, winT, dres)


def _retention_tables(tp, pad):
    half = HD // 2
    inv_freq = ROPE_BASE ** (-jnp.arange(half, dtype=f32) / half)
    pos = jnp.arange(tp, dtype=f32) - pad
    ang = pos[:, None] * inv_freq[None, :]
    cos, sin = jnp.cos(ang), jnp.sin(ang)
    cos2 = jnp.concatenate([cos, cos], axis=-1)
    sin2 = jnp.concatenate([-sin, sin], axis=-1)
    log_gamma = jnp.log1p(-(2.0 ** (-5.0 - jnp.arange(HEADS, dtype=f32))))
    idx = jnp.arange(CHUNK, dtype=f32)
    diff = idx[:, None] - idx[None, :]
    intra = jnp.where(diff[None] >= 0, jnp.exp(diff[None] * log_gamma[:, None, None]), 0.0)
    k_decay = jnp.exp((CHUNK - 1.0 - idx)[None, :] * log_gamma[:, None])
    q_decay = jnp.exp((idx + 1.0)[None, :] * log_gamma[:, None])
    chunk_decay = jnp.exp(CHUNK * log_gamma)
    full = (HEADS, CHUNK, HD)
    dec = jnp.stack([intra, jnp.broadcast_to(k_decay[:, :, None], full), jnp.broadcast_to(q_decay[:, :, None], full),
                     jnp.broadcast_to(chunk_decay[:, None, None], full)], axis=1)
    return cos2, sin2, dec


def _rot(t, cos2, sin2):
    return t * cos2 + pltpu.roll(t, HD // 2, 1) * sin2


def _rot_t(t, cos2, sin2):
    return t * cos2 - pltpu.roll(t, HD // 2, 1) * sin2


def _chunks_per_step(nch):
    return 3 if nch % 3 == 0 else 1


def retention_fwd(name, zq, dec):
    tp = zq.shape[0]
    nch = tp // CHUNK
    per = _chunks_per_step(nch)

    def body(q_ref, k_ref, v_ref, dec_ref, out_ref, st_ref, s_ref):
        @pl.when(pl.program_id(0) == 0)
        def _():
            s_ref[...] = jnp.zeros_like(s_ref)

        state = [s_ref[hh] for hh in range(HEADS)]
        for j in range(per):
            rows = slice(j * CHUNK, (j + 1) * CHUNK)
            for hh in range(HEADS):
                cols = slice(hh * HD, (hh + 1) * HD)
                qb, kb, vb = q_ref[rows, cols], k_ref[rows, cols], v_ref[rows, cols]
                sc = (_dot_nt(qb, kb) * dec_ref[hh, 0]).astype(bf16)
                sb = state[hh].astype(bf16)
                out_ref[rows, cols] = _dot_nn(sc, vb) + _dot_nn((qb.astype(f32) * dec_ref[hh, 2]).astype(bf16), sb)
                st_ref[hh, j] = sb
                state[hh] = state[hh] * dec_ref[hh, 3] + _dot_tn((kb.astype(f32) * dec_ref[hh, 1]).astype(bf16), vb)
        for hh in range(HEADS):
            s_ref[hh] = state[hh]

    part = lambda j: pl.BlockSpec((per * CHUNK, RW), lambda n: (n, j))
    return pl.pallas_call(
        body, name=name, grid=(nch // per,),
        in_specs=[part(0), part(1), part(2), _resident((HEADS, 4, CHUNK, HD))],
        out_specs=[part(0), pl.BlockSpec((HEADS, per, HD, HD), lambda n: (0, n, 0, 0))],
        out_shape=[jax.ShapeDtypeStruct((tp, RW), f32), jax.ShapeDtypeStruct((HEADS, nch, HD, HD), bf16)],
        scratch_shapes=[pltpu.VMEM((HEADS, HD, HD), f32)],
        compiler_params=_cp(32, ("arbitrary",)))(zq, zq, zq, dec)


def retention_bwd(name, zq, cos2, sin2, dec, states, dout, pad, deps=()):
    tp = zq.shape[0]
    nch = tp // CHUNK
    per = _chunks_per_step(nch)
    nblk = nch // per
    scale = HD ** -0.5

    def body(q_ref, k_ref, v_ref, cos_ref, sin_ref, dec_ref, st_ref, do_ref, dq_ref, dk_ref, dv_ref, g_ref):
        @pl.when(pl.program_id(0) == 0)
        def _():
            g_ref[...] = jnp.zeros_like(g_ref)

        first_row = (nblk - 1 - pl.program_id(0)) * (per * CHUNK)
        gstate = [g_ref[hh] for hh in range(HEADS)]
        for j in reversed(range(per)):
            rows = slice(j * CHUNK, (j + 1) * CHUNK)
            cosv, sinv = cos_ref[rows, :], sin_ref[rows, :]
            keep = (lax.broadcasted_iota(jnp.int32, (CHUNK, HD), 0) + (first_row + j * CHUNK)) >= pad
            for hh in range(HEADS):
                cols = slice(hh * HD, (hh + 1) * HD)
                intra, kdec, qdec = dec_ref[hh, 0], dec_ref[hh, 1], dec_ref[hh, 2]
                qb, kb, vb = q_ref[rows, cols], k_ref[rows, cols], v_ref[rows, cols]
                qd = (qb.astype(f32) * qdec).astype(bf16)
                kd = (kb.astype(f32) * kdec).astype(bf16)
                sc = (_dot_nt(qb, kb) * intra).astype(bf16)
                dob = do_ref[rows, cols]
                sb = st_ref[hh, j]
                gb = gstate[hh].astype(bf16)
                dsc = (_dot_nt(dob, vb) * intra).astype(bf16)
                dv = _dot_tn(sc, dob) + _dot_nn(kd, gb)
                dqr = _dot_nn(dsc, kb) + _dot_nt(dob, sb) * qdec
                dkr = _dot_tn(dsc, qb) + _dot_nt(vb, gb) * kdec
                gstate[hh] = gstate[hh] * dec_ref[hh, 3] + _dot_tn(qd, dob)
                dq_ref[rows, cols] = jnp.where(keep, _rot_t(dqr * scale, cosv, sinv), 0.0).astype(bf16)
                dk_ref[rows, cols] = jnp.where(keep, _rot_t(dkr, cosv, sinv), 0.0).astype(bf16)
                dv_ref[rows, cols] = jnp.where(keep, dv, 0.0).astype(bf16)
        for hh in range(HEADS):
            g_ref[hh] = gstate[hh]

    part = lambda j: pl.BlockSpec((per * CHUNK, RW), lambda t: (nblk - 1 - t, j))
    table = pl.BlockSpec((per * CHUNK, HD), lambda t: (nblk - 1 - t, 0))
    return pl.pallas_call(
        _skip(len(deps), body), name=name, grid=(nblk,),
        in_specs=[_ANY] * len(deps) + [part(0), part(1), part(2), table, table, _resident((HEADS, 4, CHUNK, HD)),
                                       pl.BlockSpec((HEADS, per, HD, HD), lambda t: (0, nblk - 1 - t, 0, 0)), part(0)],
        out_specs=[part(0)] * 3,
        out_shape=[jax.ShapeDtypeStruct((tp, RW), bf16)] * 3,
        scratch_shapes=[pltpu.VMEM((HEADS, HD, HD), f32)],
        compiler_params=_cp(32, ("arbitrary",)))(*deps, zq, zq, zq, cos2, sin2, dec, states, dout)


def _window_sum(xv, steps, tp, forward):
    s = xv
    for j in range(steps):
        sh = 2 ** j
        s = s + pltpu.roll(s, (tp - sh) if forward else sh, 0)
    return s


def pool_fwd(name, zu, maps, scale, pad):
    tp = zu.shape[0]

    def body(u_ref, maps_ref, scale_ref, pooled_ref, p_ref):
        row = lax.broadcasted_iota(jnp.int32, (tp, HD), 0)
        for gi, w in enumerate(POOL_WINDOWS):
            cols = slice(gi * HD, (gi + 1) * HD)
            xv = u_ref[:, cols]
            cnt = jnp.clip(row - (pad - 1), 1, w).astype(f32)
            pooled = jnp.where(row >= pad, _window_sum(xv, gi + 1, tp, False) / cnt - xv, 0.0).astype(bf16)
            pooled_ref[:, cols] = pooled
            p_ref[:, cols] = (_dot_nn(pooled, maps_ref[gi].astype(bf16)) * scale_ref[:, cols]).astype(bf16)

    return pl.pallas_call(
        body, name=name,
        out_shape=[jax.ShapeDtypeStruct((tp, RW), bf16), jax.ShapeDtypeStruct((tp, RW), bf16)],
        compiler_params=_cp(56))(zu, maps, scale)


def pool_bwd(name, dp, pooled, maps, scale, pad):
    tp = dp.shape[0]

    def body(dp_ref, pooled_ref, maps_ref, scale_ref, du_ref, dmaps_ref, dscale_ref):
        row = lax.broadcasted_iota(jnp.int32, (tp, HD), 0)
        dscale_ref[...] = jnp.zeros_like(dscale_ref)
        for gi, w in enumerate(POOL_WINDOWS):
            cols = slice(gi * HD, (gi + 1) * HD)
            mb = maps_ref[gi].astype(bf16)
            pooled = pooled_ref[:, cols]
            dpf = dp_ref[:, cols].astype(f32)
            dscale_ref[0:1, cols] = jnp.sum(dpf * _dot_nn(pooled, mb), axis=0, keepdims=True)
            dpm = (dpf * scale_ref[:, cols]).astype(bf16)
            dmaps_ref[gi * HD:(gi + 1) * HD, :] = _dot_tn(pooled, dpm)
            dpool = jnp.where(row >= pad, _dot_nt(dpm, mb), 0.0)
            cnt = jnp.clip(row - (pad - 1), 1, w).astype(f32)
            du = _window_sum(dpool / cnt, gi + 1, tp, True) - dpool
            du_ref[:, cols] = jnp.where(row >= pad, du, 0.0).astype(bf16)

    return pl.pallas_call(
        body, name=name,
        out_shape=[jax.ShapeDtypeStruct((tp, RW), bf16), jax.ShapeDtypeStruct((HEADS * HD, HD), f32),
                   jax.ShapeDtypeStruct((8, RW), f32)],
        compiler_params=_cp(56))(dp, pooled, maps, scale)


def _group_norm(o):
    mu = jnp.mean(o, axis=-1, keepdims=True)
    oc = o - mu
    rstd = lax.rsqrt(jnp.mean(oc * oc, axis=-1, keepdims=True) + EPS)
    return oc * rstd, rstd


def mix_out_fwd(name, h, oraw, zg, zgate, p, wretT, wpoolT, wout, deps=()):
    tp, d = h.shape
    tm = _row_tile(tp)

    def body(h_ref, o_ref, zg_ref, zgate_ref, p_ref, wr_ref, wp_ref, wo_ref, ho_ref, rp_ref, ret_ref, pool_ref, mixed_ref):
        parts = []
        for hh in range(HEADS):
            cols = slice(hh * HD, (hh + 1) * HD)
            rhat, _ = _group_norm(o_ref[:, cols])
            gv = zg_ref[:, cols].astype(f32)
            parts.append(rhat * (gv * _sigmoid(gv)))
        r = jnp.concatenate(parts, axis=-1).astype(bf16)
        pv = p_ref[...]
        ret = _dot_nt(r, wr_ref[...])
        pool = _dot_nt(pv, wp_ref[...])
        mixed = (_sigmoid(zgate_ref[:, :d].astype(f32)) * ret + _sigmoid(zgate_ref[:, d:].astype(f32)) * pool).astype(bf16)
        ho_ref[...] = h_ref[...] + _dot_nn(mixed, wo_ref[...])
        rp_ref[0] = r
        rp_ref[1] = pv
        ret_ref[...] = ret.astype(bf16)
        pool_ref[...] = pool.astype(bf16)
        mixed_ref[...] = mixed

    row = lambda w: pl.BlockSpec((tm, w), lambda i: (i, 0))
    return pl.pallas_call(
        _skip(len(deps), body), name=name, grid=(tp // tm,),
        in_specs=[_ANY] * len(deps) + [row(d), row(RW), row(RW), row(2 * d), row(RW), _resident((d, RW)), _resident((d, RW)),
                                       _resident((d, d))],
        out_specs=[row(d), pl.BlockSpec((2, tm, RW), lambda i: (0, i, 0)), row(d), row(d),
                   pl.BlockSpec((None, tm, d), lambda i: (0, i, 0))],
        out_shape=[jax.ShapeDtypeStruct((tp, d), f32), jax.ShapeDtypeStruct((2, tp, RW), bf16),
                   jax.ShapeDtypeStruct((tp, d), bf16), jax.ShapeDtypeStruct((tp, d), bf16),
                   jax.ShapeDtypeStruct((1, tp, d), bf16)],
        compiler_params=_cp(48, ("arbitrary",)))(*deps, h, oraw, zg, zgate, p, wretT, wpoolT, wout)


def mix_out_bwd(name, dy, oraw, zg, zgate, ret, pool, wretT, wpoolT, wout, deps=()):
    tp, d = dy.shape
    tm = _row_tile(tp)

    def body(dy_ref, o_ref, zg_ref, zgate_ref, ret_ref, pool_ref, wr_ref, wp_ref, wo_ref,
             do_ref, dzg_ref, dzgate_ref, dp_ref, drp_ref, dyb_ref):
        dyb = dy_ref[...].astype(bf16)
        dmixed = _dot_nt(dyb, wo_ref[...])
        sa = _sigmoid(zgate_ref[:, :d].astype(f32))
        sb = _sigmoid(zgate_ref[:, d:].astype(f32))
        dret = dmixed * sa
        dpool = dmixed * sb
        dzgate_ref[:, :d] = (dret * ret_ref[...].astype(f32) * (1.0 - sa)).astype(bf16)
        dzgate_ref[:, d:] = (dpool * pool_ref[...].astype(f32) * (1.0 - sb)).astype(bf16)
        dretb, dpoolb = dret.astype(bf16), dpool.astype(bf16)
        dr = _dot_nn(dretb, wr_ref[...])
        dp_ref[...] = _dot_nn(dpoolb, wp_ref[...]).astype(bf16)
        for hh in range(HEADS):
            cols = slice(hh * HD, (hh + 1) * HD)
            rhat, rstd = _group_norm(o_ref[:, cols])
            gv = zg_ref[:, cols].astype(f32)
            sg = _sigmoid(gv)
            drh = dr[:, cols]
            drhat = drh * (gv * sg)
            dzg_ref[:, cols] = (drh * rhat * (sg * (1.0 + gv * (1.0 - sg)))).astype(bf16)
            do = rstd * (drhat - jnp.mean(drhat, axis=-1, keepdims=True)
                         - rhat * jnp.mean(drhat * rhat, axis=-1, keepdims=True))
            do_ref[:, cols] = do.astype(bf16)
        drp_ref[0] = dretb
        drp_ref[1] = dpoolb
        dyb_ref[...] = dyb

    row = lambda w: pl.BlockSpec((tm, w), lambda i: (i, 0))
    return pl.pallas_call(
        _skip(len(deps), body), name=name, grid=(tp // tm,),
        in_specs=[_ANY] * len(deps) + [row(d), row(RW), row(RW), row(2 * d), row(d), row(d), _resident((d, RW)), _resident((d, RW)),
                  _resident((d, d))],
        out_specs=[row(RW), row(RW), row(2 * d), row(RW), pl.BlockSpec((2, tm, d), lambda i: (0, i, 0)),
                   pl.BlockSpec((None, tm, d), lambda i: (0, i, 0))],
        out_shape=[jax.ShapeDtypeStruct((tp, RW), bf16), jax.ShapeDtypeStruct((tp, RW), bf16),
                   jax.ShapeDtypeStruct((tp, 2 * d), bf16), jax.ShapeDtypeStruct((tp, RW), bf16),
                   jax.ShapeDtypeStruct((2, tp, d), bf16), jax.ShapeDtypeStruct((1, tp, d), bf16)],
        compiler_params=_cp(48, ("arbitrary",)))(*deps, dy, oraw, zg, zgate, ret, pool, wretT, wpoolT, wout)


def final_loss(name, h, g, target):
    tp, d = h.shape
    tm = _row_tile(tp)
    nsub = tm // CHUNK

    def body(h_ref, g_ref, *rest):
        t_refs = rest[:nsub]
        dh_ref, loss_ref, dg_ref = rest[nsub:]
        i = pl.program_id(0)

        @pl.when(i == 0)
        def _():
            loss_ref[...] = jnp.zeros_like(loss_ref)
            dg_ref[...] = jnp.zeros_like(dg_ref)

        gv = g_ref[...]
        for j in range(nsub):
            rows = slice(j * CHUNK, (j + 1) * CHUNK)
            hhat, rs = _rms(h_ref[rows, :])
            err = jnp.where(i * nsub + j >= 1, hhat * gv - t_refs[j][...], 0.0)
            dyv = err / d
            dh_ref[rows, :] = _rms_bwd(dyv, gv, hhat, rs)
            loss_ref[...] += 0.5 * jnp.sum(jnp.sum(err * err, axis=-1, keepdims=True) / d)
            dg_ref[0:1, :] += jnp.sum(dyv * hhat, axis=0, keepdims=True)

    lagged = lambda j: pl.BlockSpec((CHUNK, d), lambda i: (jnp.maximum(i * nsub + j - 1, 0), 0))
    return pl.pallas_call(
        body, name=name, grid=(tp // tm,),
        in_specs=[pl.BlockSpec((tm, d), lambda i: (i, 0)), _resident((1, d))] + [lagged(j) for j in range(nsub)],
        out_specs=[pl.BlockSpec((tm, d), lambda i: (i, 0)), pl.BlockSpec((8, 128), lambda i: (0, 0)),
                   pl.BlockSpec((8, d), lambda i: (0, 0))],
        out_shape=[jax.ShapeDtypeStruct((tp, d), f32), jax.ShapeDtypeStruct((8, 128), f32),
                   jax.ShapeDtypeStruct((8, d), f32)],
        compiler_params=_cp(32, ("arbitrary",)))(h, g, *[target] * nsub)


def _adamw(w, g, m, v):
    m = ADAM_B1 * m + (1.0 - ADAM_B1) * g
    v = ADAM_B2 * v + (1.0 - ADAM_B2) * (g * g)
    m_hat = m / (1.0 - ADAM_B1 ** ADAM_STEP)
    v_hat = v / (1.0 - ADAM_B2 ** ADAM_STEP)
    delta = -ADAM_LR * (m_hat / (jnp.sqrt(v_hat) + ADAM_EPS) + ADAM_WD * w)
    return delta, m, v


def adam_big(name, me, recv, own, b, layer, transposed, w, m, v, prev):
    r, c = recv.shape[2:]
    wshape = w.shape[1:]
    nchunk = 1 if transposed else next(k for k in (4, 2, 1) if r % (16 * k) == 0)
    rc = r // nchunk

    def body(me_ref, recv_ref, own_ref, w_ref, m_ref, v_ref, *rest):
        g_ref, d_ref, nm_ref, nv_ref = rest[-4:]
        g = own_ref[...].astype(f32)
        for j in range(NDEV - 1):
            g = g + recv_ref[j].astype(f32)
        if transposed:
            g = g.T
        delta, nm, nv = _adamw(w_ref[...], g, m_ref[...], v_ref[...])
        g_ref[...] = g
        d_ref[...] = delta
        nm_ref[...] = nm
        nv_ref[...] = nv

    wblock = wshape if transposed else (rc, c)
    wspec = pl.BlockSpec((None,) + wblock, lambda i, me_ref: (layer, i, 0))
    in_specs = [pl.BlockSpec((NDEV - 1, None, rc, c), lambda i, me_ref: (0, b, i, 0)),
                pl.BlockSpec((None, None, rc, c), lambda i, me_ref: (b, me_ref[0], i, 0)), wspec, wspec, wspec]
    args = [recv, own, w, m, v]
    aliases = {}
    if prev is not None:
        in_specs += [_ANY] * 4
        args += list(prev)
        aliases = {6 + k: k for k in range(4)}
    return pl.pallas_call(
        body, name=name,
        grid_spec=pltpu.PrefetchScalarGridSpec(num_scalar_prefetch=1, grid=(nchunk,), in_specs=in_specs,
                                               out_specs=[wspec] * 4),
        out_shape=[jax.ShapeDtypeStruct(w.shape, f32)] * 4, input_output_aliases=aliases,
        compiler_params=_cp(56))(me, *args)


def adam_small(name, ga0, gmaps0, gmeta, ga1, gmaps1, norms, pool_scale, pool_maps, meta, final_norm, d):
    def body(ga0_ref, gmaps0_ref, gmeta_ref, ga1_ref, gmaps1_ref, *refs):
        ins, outs = refs[:21], refs[21:]
        x, y, c = _me()
        me = 4 * x + 2 * y + c

        def total(ref, rows):
            t = ref[0, rows, :]
            for j in range(1, NDEV):
                t = t + ref[j, rows, :]
            return t

        row = lambda r: slice(r, r + 1)
        outs[0][...] = jnp.broadcast_to(total(ga1_ref, row(0))[:, :128], (8, 128))

        def update(k, g, o):
            w_ref, m_ref, v_ref = ins[3 * k:3 * k + 3]
            delta, nm, nv = _adamw(w_ref[...], g, m_ref[...], v_ref[...])
            for ref, val in zip(outs[o:o + 4], (g, delta, nm, nv)):
                ref[...] = val

        two = lax.broadcasted_iota(jnp.int32, (2, d), 0)
        for k in range(3):
            update(k, jnp.where(two == 0, total(ga0_ref, row(k)), total(ga1_ref, row(2 + k))), 1 + 4 * k)
        update(3, jnp.where(two[:, :RW] == 0, total(ga0_ref, row(3))[:, :RW], total(ga1_ref, row(5))[:, :RW]), 13)
        update(4, jnp.concatenate([total(gmaps0_ref, slice(None)), total(gmaps1_ref, slice(None))], axis=0), 17)
        update(5, total(gmeta_ref, pl.ds(pl.multiple_of(me * N_META, N_META), N_META)), 21)
        update(6, total(ga1_ref, row(1)), 25)

    flat = []
    for trip in (*norms, pool_scale, pool_maps, meta, final_norm):
        flat += list(trip)
    out_shapes = [jax.ShapeDtypeStruct((8, 128), f32)]
    for trip in (*norms, pool_scale, pool_maps, meta, final_norm):
        out_shapes += [jax.ShapeDtypeStruct(trip[0].shape, f32)] * 4
    return pl.pallas_call(body, name=name, out_shape=out_shapes,
                          compiler_params=_cp(32))(ga0, gmaps0, gmeta, ga1, gmaps1, *flat)


def kernel(x, meta, ffn1_norm, ffn1_gate, ffn1_up, ffn1_down, mix_norm, w_in, pool_maps, pool_scale, w_ret_up, w_pool_up, w_out, ffn2_norm, ffn2_gate, ffn2_up, ffn2_down, final_norm, loss_target, m_meta, m_ffn1_norm, m_ffn1_gate, m_ffn1_up, m_ffn1_down, m_mix_norm, m_w_in, m_pool_maps, m_pool_scale, m_w_ret_up, m_w_pool_up, m_w_out, m_ffn2_norm, m_ffn2_gate, m_ffn2_up, m_ffn2_down, m_final_norm, v_meta, v_ffn1_norm, v_ffn1_gate, v_ffn1_up, v_ffn1_down, v_mix_norm, v_w_in, v_pool_maps, v_pool_scale, v_w_ret_up, v_w_pool_up, v_w_out, v_ffn2_norm, v_ffn2_gate, v_ffn2_up, v_ffn2_down, v_final_norm):
    seq, d = x.shape[1], x.shape[2]
    depth = ffn1_gate.shape[0]
    ff = ffn1_gate.shape[2] * NDEV
    nin = w_in.shape[2] * NDEV
    length = seq + N_META
    pad = (-length) % CHUNK
    tp = length + pad
    assert pad % 8 == 0 and pad + N_META == CHUNK and depth == 2 and nin == 5 * RW + 2 * d

    ix, iy, ic = _me()
    me = (4 * ix + 2 * iy + ic).astype(jnp.int32).reshape(1)

    meta_all, = all_gather("gather_meta", [meta])
    meta_full = jnp.transpose(meta_all, (1, 0, 2)).reshape(N_META, d)

    gathers = {}
    token = meta_all
    tview = lambda *arrs: [jnp.swapaxes(a, 1, 2) for a in arrs]
    t_g1, t_u1, t_g2, t_u2, t_in = (tview(w, m, v) for w, m, v in (
        (ffn1_gate, m_ffn1_gate, v_ffn1_gate), (ffn1_up, m_ffn1_up, v_ffn1_up), (ffn2_gate, m_ffn2_gate, v_ffn2_gate),
        (ffn2_up, m_ffn2_up, v_ffn2_up), (w_in, m_w_in, v_w_in)))
    keys, groups = [], []
    for layer in range(depth):
        lands = prep_layer(layer, me, [w_ret_up, w_pool_up],
                           [t_g1[0], t_u1[0], t_g2[0], t_u2[0], t_in[0], ffn1_down, ffn2_down, w_out])
        wretT, wpoolT, g1T, u1T, g2T, u2T, winT, d1, d2, wout = lands
        keys += [("ffn1", layer), ("mix", layer), ("ffn2", layer)]
        groups += [[g1T, u1T, d1], [winT, wretT, wpoolT, wout], [g2T, u2T, d2]]
    started, token = gather_start_chips("gather_start", groups, (token,))
    gathers = dict(zip(keys, started))

    def forward(part, layer, after):
        ssem, rsem, group = gathers[(part, layer)]
        ssem, rsem, group, tok = gather_forward(f"gather_forward_{part}{layer}", ssem, rsem, group, after)
        gathers[(part, layer)] = (ssem, rsem, group)
        return tok

    def gathered(part, layer, after):
        ssem, rsem, group = gathers[(part, layer)]
        _, full = copies_wait(f"gather_wait_{part}{layer}", ssem, rsem, (), group, after, 3)
        return [a.reshape((NDEV * a.shape[1],) + a.shape[2:]) for a in full]

    cos2, sin2, dec = _retention_tables(tp, pad)
    h = jnp.concatenate([jnp.zeros((pad, d), f32), meta_full, x[0]], axis=0)

    saved = []
    weights = []
    tok = forward("ffn1", 0, token)
    for layer in range(depth):
        row = lambda a: a[layer:layer + 1]
        s = {"h0": h}
        g1T, u1T, d1 = gathered("ffn1", layer, tok if layer == 0 else h)
        tok = forward("mix", layer, h) if layer else None
        h, s["a1"], s["b1"] = ffn_fwd(f"ffn1_fwd{layer}", h, row(ffn1_norm), g1T, u1T, d1, (tok,) if layer else ())
        s["h1"] = h
        if layer == 0:
            tok = forward("mix", layer, h)
        winT, wretT, wpoolT, wout = gathered("mix", layer, tok if layer == 0 else h)
        s["zq"], s["zg"], zu, s["zgate"] = mix_in_fwd(f"mix_in_fwd{layer}", h, row(mix_norm), winT, cos2, sin2)
        s["oraw"], s["states"] = retention_fwd(f"retention_fwd{layer}", s["zq"], dec)
        s["pooled"], p = pool_fwd(f"pool_fwd{layer}", zu, pool_maps[layer], row(pool_scale), pad)
        tok = forward("ffn2", layer, p)
        h, s["rp"], s["ret"], s["pool"], s["mixed"] = mix_out_fwd(
            f"mix_out_fwd{layer}", h, s["oraw"], s["zg"], s["zgate"], p, wretT, wpoolT, wout, (tok,))
        s["h2"] = h
        g2T, u2T, d2 = gathered("ffn2", layer, h)
        tok = (forward("ffn1", layer + 1, h),) if layer + 1 < depth else ()
        h, s["a2"], s["b2"] = ffn_fwd(f"ffn2_fwd{layer}", h, row(ffn2_norm), g2T, u2T, d2, tok)
        saved.append(s)
        weights.append((g1T, u1T, g2T, u2T, winT, wretT, wpoolT, d1, d2, wout))

    dh, loss_part, dg_final = final_loss("final_loss", h, final_norm.reshape(1, d), loss_target[0])

    small = {}
    small_gathers = {}
    exchanges = {}
    token = None

    def rows8(vals):
        at = lax.broadcasted_iota(jnp.int32, (8, d), 0)
        out = jnp.zeros((8, d), f32)
        for k, v in enumerate(vals):
            r0 = v[0:1]
            r0 = r0 if r0.shape[1] == d else jnp.pad(r0, ((0, 0), (0, d - r0.shape[1])))
            out = jnp.where(at == k, r0, out)
        return out

    def exchange(part, layer, grads):
        by_dest = [g.reshape(g.shape[0], NDEV, g.shape[1] // NDEV, g.shape[2]) for g in grads]
        ssem, rsem, sent, lands, tok = exchange_start(f"exchange_start_{part}{layer}", by_dest)
        exchanges[(part, layer)] = (ssem, rsem, sent, lands)
        return (tok,)

    for layer in reversed(range(depth)):
        g1T, u1T, g2T, u2T, winT, wretT, wpoolT, d1, d2, wout = weights[layer]
        row = lambda a: a[layer:layer + 1]
        s = saved[layer]
        dh, lhs2, rhs2, small[("ffn2", layer)] = ffn_bwd(
            f"ffn2_bwd{layer}", dh, s["h2"], row(ffn2_norm), s["a2"], s["b2"], g2T, u2T, d2, () if token is None else token)
        token = exchange("ffn2", layer, [mm_tn(f"ffn2_wgrad{layer}", lhs2, rhs2, lambda b: b // 2)])
        do, dzg, dzgate, dp, drp, dyb = mix_out_bwd(
            f"mix_out_bwd{layer}", dh, s["oraw"], s["zg"], s["zgate"], s["ret"], s["pool"], wretT, wpoolT, wout, token)
        token = exchange("mix", layer, [mm_tn(f"w_out_wgrad{layer}", s["mixed"], dyb, lambda b: b),
                                        mm_tn(f"up_wgrad{layer}", drp, s["rp"], lambda b: b)])
        dq, dk, dv = retention_bwd(f"retention_bwd{layer}", s["zq"], cos2, sin2, dec, s["states"], do, pad, token)
        dzu, small[("maps", layer)], small[("scale", layer)] = pool_bwd(
            f"pool_bwd{layer}", dp, s["pooled"], pool_maps[layer], row(pool_scale), pad)
        dh, dz, n2, small[("mix", layer)] = mix_in_bwd(
            f"mix_in_bwd{layer}", dq, dk, dv, dzg, dzu, dzgate, s["h1"], row(mix_norm), winT, dh)
        token = exchange("w_in", layer, [mm_tn(f"w_in_wgrad{layer}", dz, n2, lambda b: b)])
        dh, lhs1, rhs1, small[("ffn1", layer)] = ffn_bwd(
            f"ffn1_bwd{layer}", dh, s["h0"], row(ffn1_norm), s["a1"], s["b1"], g1T, u1T, d1, token)
        rows = [small[("ffn1", layer)], small[("mix", layer)], small[("ffn2", layer)], small[("scale", layer)]]
        packs = [rows8([loss_part, dg_final] + rows if layer == depth - 1 else rows), small[("maps", layer)]]
        if layer == 0:
            dmeta = dh[pad:CHUNK]
            packs.append(jnp.transpose(dmeta.reshape(N_META, NDEV, d // NDEV), (1, 0, 2)).reshape(NDEV * N_META, d // NDEV))
        ssem, rsem, lands, tok = gather_start(f"small_start{layer}", slot_in(f"small_slot{layer}", me, packs))
        small_gathers[layer] = (ssem, rsem, lands)
        if layer:
            token = exchange("ffn1", layer, [mm_tn(f"ffn1_wgrad{layer}", lhs1, rhs1, lambda b: b // 2, (tok,))])
        else:
            token = (tok,)
            for j, nm in enumerate(("ffn1_gate", "ffn1_up", "ffn1_down")):
                token = exchange(nm, layer, [mm_tn(f"{nm}_wgrad{layer}", lhs1, rhs1, lambda b: b // 2, token, only=j)])

    grad_x = dh[CHUNK:][None]

    big = {}
    after = token[0]
    plans = {
        "ffn2": [("ffn2_gate", 0, 0, False, *t_g2), ("ffn2_up", 0, 1, False, *t_u2),
                 ("ffn2_down", 0, 2, False, ffn2_down, m_ffn2_down, v_ffn2_down)],
        "mix": [("w_out", 0, 0, False, w_out, m_w_out, v_w_out),
                ("w_ret_up", 1, 0, True, w_ret_up, m_w_ret_up, v_w_ret_up),
                ("w_pool_up", 1, 1, True, w_pool_up, m_w_pool_up, v_w_pool_up)],
        "w_in": [("w_in", 0, 0, False, *t_in)],
        "ffn1": [("ffn1_gate", 0, 0, False, *t_g1), ("ffn1_up", 0, 1, False, *t_u1),
                 ("ffn1_down", 0, 2, False, ffn1_down, m_ffn1_down, v_ffn1_down)]}
    for nm, k, b, tr, w, m, v in plans["ffn1"]:
        plans[nm] = [(nm, 0, 0, tr, w, m, v)]
    for layer in reversed(range(depth)):
        for part in ("ffn2", "mix", "w_in") + (("ffn1",) if layer else ("ffn1_gate", "ffn1_up", "ffn1_down")):
            ssem, rsem, sent, lands = exchanges[(part, layer)]
            sent, lands = copies_wait(f"exchange_wait_{part}{layer}", ssem, rsem, sent, lands, after)
            for nm, k, b, tr, w, m, v in plans[part]:
                big[nm] = adam_big(f"adam_{nm}{layer}", me, lands[k], sent[k], b, layer, tr, w, m, v, big.get(nm))
                after = big[nm][0]

    gsmall = []
    for layer in range(depth):
        ssem, rsem, lands = small_gathers[layer]
        gsmall += copies_wait(f"small_wait{layer}", ssem, rsem, (), lands, after)[1]

    maps2 = lambda a: a.reshape(depth * HEADS * HD, HD)
    res = adam_small(
        "adam_small", *gsmall,
        [(ffn1_norm, m_ffn1_norm, v_ffn1_norm), (mix_norm, m_mix_norm, v_mix_norm), (ffn2_norm, m_ffn2_norm, v_ffn2_norm)],
        (pool_scale, m_pool_scale, v_pool_scale), (maps2(pool_maps), maps2(m_pool_maps), maps2(v_pool_maps)),
        (meta, m_meta, v_meta), tuple(a.reshape(1, d) for a in (final_norm, m_final_norm, v_final_norm)), d)
    loss = res[0][0, 0]
    sm = {}
    for k, nm in enumerate(["ffn1_norm", "mix_norm", "ffn2_norm", "pool_scale", "pool_maps", "meta", "final_norm"]):
        sm[nm] = list(res[1 + 4 * k:5 + 4 * k])
    sm["pool_maps"] = [a.reshape(pool_maps.shape) for a in sm["pool_maps"]]
    sm["final_norm"] = [a.reshape(d) for a in sm["final_norm"]]

    names = ["meta", "ffn1_norm", "ffn1_gate", "ffn1_up", "ffn1_down", "mix_norm", "w_in", "pool_maps", "pool_scale",
             "w_ret_up", "w_pool_up", "w_out", "ffn2_norm", "ffn2_gate", "ffn2_up", "ffn2_down", "final_norm"]
    for nm in ("ffn1_gate", "ffn1_up", "ffn2_gate", "ffn2_up", "w_in"):
        big[nm] = tview(*big[nm])
    allw = {**{k: list(v) for k, v in big.items()}, **sm}
    outs = [loss, grad_x]
    for kind in range(4):
        outs += [allw[nm][kind] for nm in names]
    return tuple(outs)
```

```python
import functools

import jax
import jax.numpy as jnp
from jax import lax
from jax.experimental import pallas as pl
from jax.experimental.pallas import tpu as pltpu

f32 = jnp.float32
bf16 = jnp.bfloat16
MESH = pl.DeviceIdType.MESH
NDEV = 8
N_META = 16
HEADS = 4
HD = 128
CHUNK = 128
RW = HEADS * HD
POOL_WINDOWS = (2, 4, 8, 16)
ROPE_BASE = 10000.0
EPS = 1e-6
ADAM_LR = 0.001
ADAM_B1 = 0.9
ADAM_B2 = 0.999
ADAM_EPS = 1e-08
ADAM_WD = 0.01
ADAM_STEP = 10
VMEM_CAP_MB = 60


def _cp(vmem_mb, sem=None):
    return pltpu.CompilerParams(vmem_limit_bytes=min(vmem_mb, VMEM_CAP_MB) * 2**20, dimension_semantics=sem)


def _row_tile(tp, want=384):
    return want if tp % want == 0 else 128


def _resident(shape):
    nd = len(shape)
    return pl.BlockSpec(shape, lambda *_: (0,) * nd, pipeline_mode=pl.Buffered(1))


def _skip(nd, body):
    return (lambda *refs: body(*refs[nd:])) if nd else body


def _dot_nn(a, b):
    return lax.dot_general(a, b, (((1,), (0,)), ((), ())), preferred_element_type=f32)


def _dot_nt(a, b):
    return lax.dot_general(a, b, (((1,), (1,)), ((), ())), preferred_element_type=f32)


def _dot_tn(a, b):
    return lax.dot_general(a, b, (((0,), (0,)), ((), ())), preferred_element_type=f32)


def _rms(h):
    rs = lax.rsqrt(jnp.mean(h * h, axis=-1, keepdims=True) + EPS)
    return h * rs, rs


def _rms_bwd(dn, g, hhat, rs):
    dhh = dn * g
    return rs * (dhh - hhat * jnp.mean(dhh * hhat, axis=-1, keepdims=True))


def _sigmoid(x):
    return jax.nn.sigmoid(x)


def _me():
    return lax.axis_index("x"), lax.axis_index("y"), lax.axis_index("c")


def _peer(idx):
    return (idx // 4, (idx // 2) % 2, idx % 2)


def all_gather(name, arrays):
    n = len(arrays)

    def body(*refs):
        ins, outs = refs[:n], refs[n:2 * n]
        send_sems, recv_sems, local_sems = refs[2 * n:]
        x, y, c = _me()
        me = 4 * x + 2 * y + c
        locals_ = []
        for k in range(n):
            cp = pltpu.make_async_copy(ins[k], outs[k].at[me], local_sems.at[k])
            cp.start()
            locals_.append(cp)
        for d in range(1, NDEV):
            for k in range(n):
                pltpu.make_async_remote_copy(
                    src_ref=ins[k], dst_ref=outs[k].at[me], send_sem=send_sems.at[k], recv_sem=recv_sems.at[k],
                    device_id=_peer((me + d) % NDEV), device_id_type=MESH).start()
        for k in range(n):
            seven = outs[k].at[pl.ds(0, NDEV - 1)]
            w = pltpu.make_async_remote_copy(src_ref=seven, dst_ref=seven, send_sem=send_sems.at[k],
                                             recv_sem=recv_sems.at[k], device_id=(x, y, c), device_id_type=MESH)
            w.wait_send()
            w.wait_recv()
            locals_[k].wait()

    anyspec = pl.BlockSpec(memory_space=pl.ANY)
    return pl.pallas_call(
        body, name=name,
        out_shape=[jax.ShapeDtypeStruct((NDEV,) + a.shape, a.dtype) for a in arrays],
        in_specs=[anyspec] * n, out_specs=[anyspec] * n,
        scratch_shapes=[pltpu.SemaphoreType.DMA((n,)), pltpu.SemaphoreType.DMA((n,)), pltpu.SemaphoreType.DMA((n,))],
    )(*arrays)


_HBM = pl.BlockSpec(memory_space=pltpu.HBM)
_SEM = pl.BlockSpec(memory_space=pltpu.SEMAPHORE)
_ANY = pl.BlockSpec(memory_space=pl.ANY)
_EFFECT = pltpu.SideEffectType.DATAFLOW_SIDE_EFFECTING


def _in_hbm(a):
    return pltpu.with_memory_space_constraint(a, pltpu.HBM)


def gather_start(name, lands, deps=()):
    n, nd = len(lands), len(deps)

    def body(*refs):
        land = refs[nd:nd + n]
        send_sems, recv_sems = refs[nd + n:nd + n + 2]
        token = refs[-1]
        x, y, c = _me()
        me = 4 * x + 2 * y + c
        for d in range(1, NDEV):
            for k in range(n):
                pltpu.make_async_remote_copy(
                    src_ref=land[k].at[me], dst_ref=land[k].at[me], send_sem=send_sems.at[k], recv_sem=recv_sems.at[k],
                    device_id=_peer((me + d) % NDEV), device_id_type=MESH).start()
        token[...] = jnp.zeros_like(token)

    res = pl.pallas_call(
        body, name=name,
        out_shape=(pltpu.SemaphoreType.DMA((n,)), pltpu.SemaphoreType.DMA((n,)),
                   *[pltpu.HBM(a.shape, a.dtype) for a in lands], jax.ShapeDtypeStruct((8, 128), f32)),
        in_specs=[_ANY] * nd + [_HBM] * n,
        out_specs=(_SEM, _SEM, *[_HBM] * n, pl.BlockSpec(memory_space=pltpu.VMEM)),
        input_output_aliases={nd + k: 2 + k for k in range(n)},
        compiler_params=pltpu.CompilerParams(has_side_effects=_EFFECT),
    )(*deps, *[_in_hbm(a) for a in lands])
    return res[0], res[1], list(res[2:2 + n]), res[-1]


def _other_chips(x, y):
    return [(1 - x, y), (x, 1 - y), (1 - x, 1 - y)]


def gather_start_chips(name, groups, deps=()):
    sizes = [len(g) for g in groups]
    lands = [a for g in groups for a in g]
    n, nd, ng = len(lands), len(deps), len(groups)

    def body(*refs):
        land = refs[nd:nd + n]
        sems = refs[nd + n:nd + n + 2 * ng]
        token = refs[-1]
        x, y, c = _me()
        me = 4 * x + 2 * y + c
        k = 0
        for g, size in enumerate(sizes):
            for j in range(size):
                for to in [(x, y, 1 - c)] + [(cx, cy, c) for cx, cy in _other_chips(x, y)]:
                    pltpu.make_async_remote_copy(
                        src_ref=land[k].at[me], dst_ref=land[k].at[me], send_sem=sems[2 * g].at[j],
                        recv_sem=sems[2 * g + 1].at[j], device_id=to, device_id_type=MESH).start()
                k += 1
        token[...] = jnp.zeros_like(token)

    res = pl.pallas_call(
        body, name=name,
        out_shape=(*[pltpu.SemaphoreType.DMA((size,)) for size in sizes for _ in range(2)],
                   *[pltpu.HBM(a.shape, a.dtype) for a in lands], jax.ShapeDtypeStruct((8, 128), f32)),
        in_specs=[_ANY] * nd + [_HBM] * n,
        out_specs=(*[_SEM] * (2 * ng), *[_HBM] * n, pl.BlockSpec(memory_space=pltpu.VMEM)),
        input_output_aliases={nd + k: 2 * ng + k for k in range(n)},
        compiler_params=pltpu.CompilerParams(has_side_effects=_EFFECT),
    )(*deps, *[_in_hbm(a) for a in lands])
    out, k = [], 2 * ng
    for g, size in enumerate(sizes):
        out.append((res[2 * g], res[2 * g + 1], list(res[k:k + size])))
        k += size
    return out, res[-1]


def gather_forward(name, send_sems, recv_sems, lands, after):
    n = len(lands)

    def body(*refs):
        land = refs[:n]
        ssem, rsem = refs[n:n + 2]
        send2, recv2 = refs[n + 3:n + 5]
        token = refs[-1]
        x, y, c = _me()
        for k in range(n):
            four = land[k].at[pl.ds(0, 4)]
            w = pltpu.make_async_remote_copy(src_ref=four, dst_ref=four, send_sem=ssem.at[k], recv_sem=rsem.at[k],
                                             device_id=(x, y, c), device_id_type=MESH)
            w.wait_send()
            w.wait_recv()
            for cx, cy in _other_chips(x, y):
                slot = 4 * cx + 2 * cy + c
                pltpu.make_async_remote_copy(
                    src_ref=land[k].at[slot], dst_ref=land[k].at[slot], send_sem=send2.at[k], recv_sem=recv2.at[k],
                    device_id=(x, y, 1 - c), device_id_type=MESH).start()
        token[...] = jnp.zeros_like(token)

    res = pl.pallas_call(
        body, name=name,
        out_shape=(pltpu.SemaphoreType.DMA((n,)), pltpu.SemaphoreType.DMA((n,)),
                   *[pltpu.HBM(a.shape, a.dtype) for a in lands], jax.ShapeDtypeStruct((8, 128), f32)),
        in_specs=[_HBM] * n + [_SEM, _SEM, _ANY],
        out_specs=(_SEM, _SEM, *[_HBM] * n, pl.BlockSpec(memory_space=pltpu.VMEM)),
        input_output_aliases={k: 2 + k for k in range(n)},
        compiler_params=pltpu.CompilerParams(has_side_effects=_EFFECT),
    )(*lands, send_sems, recv_sems, after)
    return res[0], res[1], list(res[2:2 + n]), res[-1]


def exchange_start(name, grads, deps=()):
    n, nd = len(grads), len(deps)
    lands = [lax.empty((NDEV - 1, g.shape[0]) + g.shape[2:], g.dtype) for g in grads]

    def body(*refs):
        src = refs[nd:nd + n]
        land = refs[nd + n:nd + 2 * n]
        send_sems, recv_sems = refs[nd + 2 * n:nd + 2 * n + 2]
        token = refs[-1]
        x, y, c = _me()
        me = 4 * x + 2 * y + c
        for d in range(1, NDEV):
            p = (me + d) % NDEV
            for k in range(n):
                pltpu.make_async_remote_copy(
                    src_ref=src[k].at[:, p], dst_ref=land[k].at[d - 1], send_sem=send_sems.at[k], recv_sem=recv_sems.at[k],
                    device_id=_peer(p), device_id_type=MESH).start()
        token[...] = jnp.zeros_like(token)

    both = list(grads) + lands
    res = pl.pallas_call(
        body, name=name,
        out_shape=(pltpu.SemaphoreType.DMA((n,)), pltpu.SemaphoreType.DMA((n,)),
                   *[pltpu.HBM(a.shape, a.dtype) for a in both], jax.ShapeDtypeStruct((8, 128), f32)),
        in_specs=[_ANY] * nd + [_HBM] * (2 * n),
        out_specs=(_SEM, _SEM, *[_HBM] * (2 * n), pl.BlockSpec(memory_space=pltpu.VMEM)),
        input_output_aliases={nd + k: 2 + k for k in range(2 * n)},
        compiler_params=pltpu.CompilerParams(has_side_effects=_EFFECT),
    )(*deps, *[_in_hbm(a) for a in both])
    return res[0], res[1], list(res[2:2 + n]), list(res[2 + n:2 + 2 * n]), res[-1]


def copies_wait(name, send_sems, recv_sems, sent, lands, after, count=NDEV - 1):
    ns, n = len(sent), len(lands)

    def body(*refs):
        land = refs[ns:ns + n]
        ssem, rsem = refs[ns + n:ns + n + 2]
        x, y, c = _me()
        for k in range(n):
            seven = land[k].at[pl.ds(0, count)]
            w = pltpu.make_async_remote_copy(src_ref=seven, dst_ref=seven, send_sem=ssem.at[k], recv_sem=rsem.at[k],
                                             device_id=(x, y, c), device_id_type=MESH)
            w.wait_send()
            w.wait_recv()

    both = list(sent) + list(lands)
    res = pl.pallas_call(
        body, name=name, out_shape=tuple(pltpu.HBM(a.shape, a.dtype) for a in both),
        in_specs=[_HBM] * (ns + n) + [_SEM, _SEM, _ANY], out_specs=tuple([_HBM] * (ns + n)),
        input_output_aliases={k: k for k in range(ns + n)},
        compiler_params=pltpu.CompilerParams(has_side_effects=_EFFECT),
    )(*both, send_sems, recv_sems, after)
    return list(res[:ns]), list(res[ns:])


def prep_layer(layer, me, col_sharded, row_sharded, deps=()):
    nc, nr = len(col_sharded), len(row_sharded)

    def body(me_ref, *refs):
        ins, outs = refs[:nc + nr], refs[nc + nr + len(deps):]
        for k in range(nc):
            outs[k][...] = ins[k][...].T.astype(bf16)
        for k in range(nc, nc + nr):
            outs[k][...] = ins[k][...].astype(bf16)

    arrs = list(col_sharded) + list(row_sharded)
    in_specs = [pl.BlockSpec((None,) + a.shape[1:], lambda i, me_ref: (layer, 0, 0)) for a in arrs]
    shapes = [(a.shape[2], a.shape[1]) for a in col_sharded] + [a.shape[1:] for a in row_sharded]
    out_specs = [pl.BlockSpec((None,) + s, lambda i, me_ref: (me_ref[0], 0, 0)) for s in shapes]
    return pl.pallas_call(
        body, name=f"prep_layer{layer}",
        grid_spec=pltpu.PrefetchScalarGridSpec(num_scalar_prefetch=1, grid=(1,), in_specs=in_specs + [_ANY] * len(deps),
                                               out_specs=out_specs),
        out_shape=[jax.ShapeDtypeStruct((NDEV,) + s, bf16) for s in shapes], compiler_params=_cp(48))(me, *arrs, *deps)


def slot_in(name, me, arrays):
    n = len(arrays)

    def body(me_ref, *refs):
        for k in range(n):
            refs[n + k][...] = refs[k][...]

    in_specs = [pl.BlockSpec(a.shape, lambda i, me_ref: (0, 0)) for a in arrays]
    out_specs = [pl.BlockSpec((None,) + a.shape, lambda i, me_ref: (me_ref[0], 0, 0)) for a in arrays]
    return pl.pallas_call(
        body, name=name,
        grid_spec=pltpu.PrefetchScalarGridSpec(num_scalar_prefetch=1, grid=(1,), in_specs=in_specs, out_specs=out_specs),
        out_shape=[jax.ShapeDtypeStruct((NDEV,) + a.shape, a.dtype) for a in arrays])(me, *arrays)


def _ff_chunks(ff, want=768):
    if ff % 256:
        return [slice(0, ff)]
    return [slice(c, min(c + want, ff)) for c in range(0, ff, want)]


def ffn_fwd(name, h, g, wgT, wuT, wd, deps=()):
    tp, d = h.shape
    ff = wgT.shape[0]
    tm = _row_tile(tp, 704)

    def body(h_ref, g_ref, wg_ref, wu_ref, wd_ref, ho_ref, a_ref, b_ref):
        hh = h_ref[...]
        hhat, _ = _rms(hh)
        n = (hhat * g_ref[...]).astype(bf16)
        acc = None
        for cols in _ff_chunks(ff):
            a = _dot_nt(n, wg_ref[cols, :])
            b = _dot_nt(n, wu_ref[cols, :])
            part = _dot_nn(((a * _sigmoid(a)) * b).astype(bf16), wd_ref[cols, :])
            acc = part if acc is None else acc + part
            a_ref[:, cols] = a.astype(bf16)
            b_ref[:, cols] = b.astype(bf16)
        ho_ref[...] = hh + 0.5 * acc

    row = lambda w: pl.BlockSpec((tm, w), lambda i: (i, 0))
    return pl.pallas_call(
        _skip(len(deps), body), name=name, grid=(tp // tm,),
        in_specs=[_ANY] * len(deps) + [row(d), _resident((1, d)), _resident((ff, d)), _resident((ff, d)), _resident((ff, d))],
        out_specs=[row(d), row(ff), row(ff)],
        out_shape=[jax.ShapeDtypeStruct((tp, d), f32), jax.ShapeDtypeStruct((tp, ff), bf16),
                   jax.ShapeDtypeStruct((tp, ff), bf16)],
        compiler_params=_cp(56, ("arbitrary",)))(*deps, h, g, wgT, wuT, wd)


def ffn_bwd(name, dy, h, g, a, b, wgT, wuT, wd, deps=()):
    tp, d = h.shape
    ff = wgT.shape[0]
    tm = _row_tile(tp, 384)

    def body(dy_ref, h_ref, g_ref, a_ref, b_ref, wg_ref, wu_ref, wd_ref, dh_ref, lhs_ref, rhs_ref, dg_ref):
        dyv = dy_ref[...]
        hhat, rs = _rms(h_ref[...])
        gv = g_ref[...]
        n = hhat * gv
        dyh = (0.5 * dyv).astype(bf16)
        dn = None
        for cols in _ff_chunks(ff):
            ds = _dot_nt(dyh, wd_ref[cols, :])
            av = a_ref[:, cols].astype(f32)
            bv = b_ref[:, cols].astype(f32)
            sg = _sigmoid(av)
            sa = av * sg
            da = (ds * bv * (sg * (1.0 + av * (1.0 - sg)))).astype(bf16)
            db = (ds * sa).astype(bf16)
            part = _dot_nn(da, wg_ref[cols, :]) + _dot_nn(db, wu_ref[cols, :])
            dn = part if dn is None else dn + part
            lhs_ref[0, :, cols] = da
            lhs_ref[1, :, cols] = db
            lhs_ref[2, :, cols] = (sa * bv).astype(bf16)
        dh_ref[...] = dyv + _rms_bwd(dn, gv, hhat, rs)

        @pl.when(pl.program_id(0) == 0)
        def _():
            dg_ref[...] = jnp.zeros_like(dg_ref)

        dg_ref[0:1, :] += jnp.sum(dn * hhat, axis=0, keepdims=True)
        rhs_ref[0] = n.astype(bf16)
        rhs_ref[1] = dyh

    row = lambda w: pl.BlockSpec((tm, w), lambda i: (i, 0))
    return pl.pallas_call(
        _skip(len(deps), body), name=name, grid=(tp // tm,),
        in_specs=[_ANY] * len(deps) + [row(d), row(d), _resident((1, d)), row(ff), row(ff),
                  _resident((ff, d)), _resident((ff, d)), _resident((ff, d))],
        out_specs=[row(d), pl.BlockSpec((3, tm, ff), lambda i: (0, i, 0)), pl.BlockSpec((2, tm, d), lambda i: (0, i, 0)),
                   pl.BlockSpec((8, d), lambda i: (0, 0))],
        out_shape=[jax.ShapeDtypeStruct((tp, d), f32), jax.ShapeDtypeStruct((3, tp, ff), bf16),
                   jax.ShapeDtypeStruct((2, tp, d), bf16), jax.ShapeDtypeStruct((8, d), f32)],
        compiler_params=_cp(58, ("arbitrary",)))(*deps, dy, h, g, a, b, wgT, wuT, wd)


def mm_tn(name, lhs, rhs, rhs_of, deps=(), only=None):
    _, tp, m = lhs.shape
    b0, nb = (0, lhs.shape[0]) if only is None else (only, 1)
    n = rhs.shape[2]
    def fits(t, ms):
        mb = m // ms
        return (tp % t == 0 and m % (128 * ms) == 0
                and 2 * t * (mb + n) * 2 + mb * n * (2 * 2 + 4 + (4 if t < tp else 0)) <= 54 * 2**20)

    tk, msplit = next(((t, ms) for t in (tp, 1408, 704, 384) for ms in (1, 2, 4) if fits(t, ms)), (128, 1))
    nk = tp // tk
    mb = m // msplit

    def body(l_ref, r_ref, o_ref, acc_ref):
        if nk == 1:
            o_ref[...] = _dot_tn(l_ref[...], r_ref[...]).astype(o_ref.dtype)
            return
        k = pl.program_id(2)

        @pl.when(k == 0)
        def _():
            acc_ref[...] = jnp.zeros_like(acc_ref)

        acc_ref[...] += _dot_tn(l_ref[...], r_ref[...])

        @pl.when(k == nk - 1)
        def _():
            o_ref[...] = acc_ref[...].astype(o_ref.dtype)

    return pl.pallas_call(
        _skip(len(deps), body), name=name, grid=(nb, msplit, nk),
        in_specs=[_ANY] * len(deps) + [pl.BlockSpec((None, tk, mb), lambda b, j, k: (b0 + b, k, j)),
                                       pl.BlockSpec((None, tk, n), lambda b, j, k: (rhs_of(b0 + b), k, 0))],
        out_specs=pl.BlockSpec((None, mb, n), lambda b, j, k: (b, j, 0)),
        out_shape=jax.ShapeDtypeStruct((nb, m, n), bf16),
        scratch_shapes=[pltpu.VMEM((mb, n) if nk > 1 else (8, 128), f32)],
        compiler_params=_cp(60, ("arbitrary", "arbitrary", "arbitrary")))(*deps, lhs, rhs)


def mix_in_fwd(name, h, g, winT, cos2, sin2):
    tp, d = h.shape
    nin = winT.shape[0]
    tm = _row_tile(tp)

    def body(h_ref, g_ref, w_ref, cos_ref, sin_ref, zq_ref, zg_ref, zu_ref, zgate_ref):
        hhat, _ = _rms(h_ref[...])
        z = _dot_nt((hhat * g_ref[...]).astype(bf16), w_ref[...])
        cosv, sinv = cos_ref[...], sin_ref[...]
        for hh in range(HEADS):
            qcols, kcols = slice(hh * HD, (hh + 1) * HD), slice(RW + hh * HD, RW + (hh + 1) * HD)
            zq_ref[:, qcols] = (_rot(z[:, qcols], cosv, sinv) * HD ** -0.5).astype(bf16)
            zq_ref[:, kcols] = _rot(z[:, kcols], cosv, sinv).astype(bf16)
        zq_ref[:, 2 * RW:] = z[:, 2 * RW:3 * RW].astype(bf16)
        zg_ref[...] = z[:, 3 * RW:4 * RW].astype(zg_ref.dtype)
        zu_ref[...] = z[:, 4 * RW:5 * RW]
        zgate_ref[...] = z[:, 5 * RW:].astype(zgate_ref.dtype)

    row = lambda w: pl.BlockSpec((tm, w), lambda i: (i, 0))
    widths = (3 * RW, RW, RW, 2 * d)
    return pl.pallas_call(
        body, name=name, grid=(tp // tm,),
        in_specs=[row(d), _resident((1, d)), _resident((nin, d)), row(HD), row(HD)],
        out_specs=[row(w) for w in widths],
        out_shape=[jax.ShapeDtypeStruct((tp, w), dt) for w, dt in zip(widths, (bf16, bf16, f32, bf16))],
        compiler_params=_cp(56, ("arbitrary",)))(h, g, winT, cos2, sin2)


def mix_in_bwd(name, dq, dk, dv, dzg, dzu, dzgate, h, g, winT, dres):
    tp, d = h.shape
    nin = winT.shape[0]
    tm = _row_tile(tp)

    def body(dq_ref, dk_ref, dv_ref, dzg_ref, dzu_ref, dzgate_ref, h_ref, g_ref, w_ref, dres_ref, dh_ref, dz_ref, n_ref, dg_ref):
        dz = jnp.concatenate([dq_ref[...], dk_ref[...], dv_ref[...], dzg_ref[...], dzu_ref[...], dzgate_ref[...]], axis=-1)
        dn = _dot_nn(dz, w_ref[...])
        hhat, rs = _rms(h_ref[...])
        gv = g_ref[...]
        dh_ref[...] = dres_ref[...] + _rms_bwd(dn, gv, hhat, rs)

        @pl.when(pl.program_id(0) == 0)
        def _():
            dg_ref[...] = jnp.zeros_like(dg_ref)

        dg_ref[0:1, :] += jnp.sum(dn * hhat, axis=0, keepdims=True)
        dz_ref[...] = dz
        n_ref[...] = (hhat * gv).astype(bf16)

    row = lambda w: pl.BlockSpec((tm, w), lambda i: (i, 0))
    return pl.pallas_call(
        body, name=name, grid=(tp // tm,),
        in_specs=[row(RW)] * 5 + [row(2 * d), row(d), _resident((1, d)), _resident((nin, d)), row(d)],
        out_specs=[row(d), pl.BlockSpec((None, tm, nin), lambda i: (0, i, 0)), pl.BlockSpec((None, tm, d), lambda i: (0, i, 0)),
                   pl.BlockSpec((8, d), lambda i: (0, 0))],
        out_shape=[jax.ShapeDtypeStruct((tp, d), f32), jax.ShapeDtypeStruct((1, tp, nin), bf16),
                   jax.ShapeDtypeStruct((1, tp, d), bf16), jax.ShapeDtypeStruct((8, d), f32)],
        compiler_params=_cp(56, ("arbitrary",)))(dq, dk, dv, dzg, dzu, dzgate, h, g, winT, dres)


def _retention_tables(tp, pad):
    half = HD // 2
    inv_freq = ROPE_BASE ** (-jnp.arange(half, dtype=f32) / half)
    pos = jnp.arange(tp, dtype=f32) - pad
    ang = pos[:, None] * inv_freq[None, :]
    cos, sin = jnp.cos(ang), jnp.sin(ang)
    cos2 = jnp.concatenate([cos, cos], axis=-1)
    sin2 = jnp.concatenate([-sin, sin], axis=-1)
    log_gamma = jnp.log1p(-(2.0 ** (-5.0 - jnp.arange(HEADS, dtype=f32))))
    idx = jnp.arange(CHUNK, dtype=f32)
    diff = idx[:, None] - idx[None, :]
    intra = jnp.where(diff[None] >= 0, jnp.exp(diff[None] * log_gamma[:, None, None]), 0.0)
    k_decay = jnp.exp((CHUNK - 1.0 - idx)[None, :] * log_gamma[:, None])
    q_decay = jnp.exp((idx + 1.0)[None, :] * log_gamma[:, None])
    chunk_decay = jnp.exp(CHUNK * log_gamma)
    full = (HEADS, CHUNK, HD)
    dec = jnp.stack([intra, jnp.broadcast_to(k_decay[:, :, None], full), jnp.broadcast_to(q_decay[:, :, None], full),
                     jnp.broadcast_to(chunk_decay[:, None, None], full)], axis=1)
    return cos2, sin2, dec


def _rot(t, cos2, sin2):
    return t * cos2 + pltpu.roll(t, HD // 2, 1) * sin2


def _rot_t(t, cos2, sin2):
    return t * cos2 - pltpu.roll(t, HD // 2, 1) * sin2


def _chunks_per_step(nch):
    return 3 if nch % 3 == 0 else 1


def retention_fwd(name, zq, dec):
    tp = zq.shape[0]
    nch = tp // CHUNK
    per = _chunks_per_step(nch)

    def body(q_ref, k_ref, v_ref, dec_ref, out_ref, st_ref, s_ref):
        @pl.when(pl.program_id(0) == 0)
        def _():
            s_ref[...] = jnp.zeros_like(s_ref)

        state = [s_ref[hh] for hh in range(HEADS)]
        for j in range(per):
            rows = slice(j * CHUNK, (j + 1) * CHUNK)
            for hh in range(HEADS):
                cols = slice(hh * HD, (hh + 1) * HD)
                qb, kb, vb = q_ref[rows, cols], k_ref[rows, cols], v_ref[rows, cols]
                sc = (_dot_nt(qb, kb) * dec_ref[hh, 0]).astype(bf16)
                sb = state[hh].astype(bf16)
                out_ref[rows, cols] = _dot_nn(sc, vb) + _dot_nn((qb.astype(f32) * dec_ref[hh, 2]).astype(bf16), sb)
                st_ref[hh, j] = sb
                state[hh] = state[hh] * dec_ref[hh, 3] + _dot_tn((kb.astype(f32) * dec_ref[hh, 1]).astype(bf16), vb)
        for hh in range(HEADS):
            s_ref[hh] = state[hh]

    part = lambda j: pl.BlockSpec((per * CHUNK, RW), lambda n: (n, j))
    return pl.pallas_call(
        body, name=name, grid=(nch // per,),
        in_specs=[part(0), part(1), part(2), _resident((HEADS, 4, CHUNK, HD))],
        out_specs=[part(0), pl.BlockSpec((HEADS, per, HD, HD), lambda n: (0, n, 0, 0))],
        out_shape=[jax.ShapeDtypeStruct((tp, RW), f32), jax.ShapeDtypeStruct((HEADS, nch, HD, HD), bf16)],
        scratch_shapes=[pltpu.VMEM((HEADS, HD, HD), f32)],
        compiler_params=_cp(32, ("arbitrary",)))(zq, zq, zq, dec)


def retention_bwd(name, zq, cos2, sin2, dec, states, dout, pad, deps=()):
    tp = zq.shape[0]
    nch = tp // CHUNK
    per = _chunks_per_step(nch)
    nblk = nch // per
    scale = HD ** -0.5

    def body(q_ref, k_ref, v_ref, cos_ref, sin_ref, dec_ref, st_ref, do_ref, dq_ref, dk_ref, dv_ref, g_ref):
        @pl.when(pl.program_id(0) == 0)
        def _():
            g_ref[...] = jnp.zeros_like(g_ref)

        first_row = (nblk - 1 - pl.program_id(0)) * (per * CHUNK)
        gstate = [g_ref[hh] for hh in range(HEADS)]
        for j in reversed(range(per)):
            rows = slice(j * CHUNK, (j + 1) * CHUNK)
            cosv, sinv = cos_ref[rows, :], sin_ref[rows, :]
            keep = (lax.broadcasted_iota(jnp.int32, (CHUNK, HD), 0) + (first_row + j * CHUNK)) >= pad
            for hh in range(HEADS):
                cols = slice(hh * HD, (hh + 1) * HD)
                intra, kdec, qdec = dec_ref[hh, 0], dec_ref[hh, 1], dec_ref[hh, 2]
                qb, kb, vb = q_ref[rows, cols], k_ref[rows, cols], v_ref[rows, cols]
                qd = (qb.astype(f32) * qdec).astype(bf16)
                kd = (kb.astype(f32) * kdec).astype(bf16)
                sc = (_dot_nt(qb, kb) * intra).astype(bf16)
                dob = do_ref[rows, cols]
                sb = st_ref[hh, j]
                gb = gstate[hh].astype(bf16)
                dsc = (_dot_nt(dob, vb) * intra).astype(bf16)
                dv = _dot_tn(sc, dob) + _dot_nn(kd, gb)
                dqr = _dot_nn(dsc, kb) + _dot_nt(dob, sb) * qdec
                dkr = _dot_tn(dsc, qb) + _dot_nt(vb, gb) * kdec
                gstate[hh] = gstate[hh] * dec_ref[hh, 3] + _dot_tn(qd, dob)
                dq_ref[rows, cols] = jnp.where(keep, _rot_t(dqr * scale, cosv, sinv), 0.0).astype(bf16)
                dk_ref[rows, cols] = jnp.where(keep, _rot_t(dkr, cosv, sinv), 0.0).astype(bf16)
                dv_ref[rows, cols] = jnp.where(keep, dv, 0.0).astype(bf16)
        for hh in range(HEADS):
            g_ref[hh] = gstate[hh]

    part = lambda j: pl.BlockSpec((per * CHUNK, RW), lambda t: (nblk - 1 - t, j))
    table = pl.BlockSpec((per * CHUNK, HD), lambda t: (nblk - 1 - t, 0))
    return pl.pallas_call(
        _skip(len(deps), body), name=name, grid=(nblk,),
        in_specs=[_ANY] * len(deps) + [part(0), part(1), part(2), table, table, _resident((HEADS, 4, CHUNK, HD)),
                                       pl.BlockSpec((HEADS, per, HD, HD), lambda t: (0, nblk - 1 - t, 0, 0)), part(0)],
        out_specs=[part(0)] * 3,
        out_shape=[jax.ShapeDtypeStruct((tp, RW), bf16)] * 3,
        scratch_shapes=[pltpu.VMEM((HEADS, HD, HD), f32)],
        compiler_params=_cp(32, ("arbitrary",)))(*deps, zq, zq, zq, cos2, sin2, dec, states, dout)


def _window_sum(xv, steps, tp, forward):
    s = xv
    for j in range(steps):
        sh = 2 ** j
        s = s + pltpu.roll(s, (tp - sh) if forward else sh, 0)
    return s


def pool_fwd(name, zu, maps, scale, pad):
    tp = zu.shape[0]

    def body(u_ref, maps_ref, scale_ref, pooled_ref, p_ref):
        row = lax.broadcasted_iota(jnp.int32, (tp, HD), 0)
        for gi, w in enumerate(POOL_WINDOWS):
            cols = slice(gi * HD, (gi + 1) * HD)
            xv = u_ref[:, cols]
            cnt = jnp.clip(row - (pad - 1), 1, w).astype(f32)
            pooled = jnp.where(row >= pad, _window_sum(xv, gi + 1, tp, False) / cnt - xv, 0.0).astype(bf16)
            pooled_ref[:, cols] = pooled
            p_ref[:, cols] = (_dot_nn(pooled, maps_ref[gi].astype(bf16)) * scale_ref[:, cols]).astype(bf16)

    return pl.pallas_call(
        body, name=name,
        out_shape=[jax.ShapeDtypeStruct((tp, RW), bf16), jax.ShapeDtypeStruct((tp, RW), bf16)],
        compiler_params=_cp(56))(zu, maps, scale)


def pool_bwd(name, dp, pooled, maps, scale, pad):
    tp = dp.shape[0]

    def body(dp_ref, pooled_ref, maps_ref, scale_ref, du_ref, dmaps_ref, dscale_ref):
        row = lax.broadcasted_iota(jnp.int32, (tp, HD), 0)
        dscale_ref[...] = jnp.zeros_like(dscale_ref)
        for gi, w in enumerate(POOL_WINDOWS):
            cols = slice(gi * HD, (gi + 1) * HD)
            mb = maps_ref[gi].astype(bf16)
            pooled = pooled_ref[:, cols]
            dpf = dp_ref[:, cols].astype(f32)
            dscale_ref[0:1, cols] = jnp.sum(dpf * _dot_nn(pooled, mb), axis=0, keepdims=True)
            dpm = (dpf * scale_ref[:, cols]).astype(bf16)
            dmaps_ref[gi * HD:(gi + 1) * HD, :] = _dot_tn(pooled, dpm)
            dpool = jnp.where(row >= pad, _dot_nt(dpm, mb), 0.0)
            cnt = jnp.clip(row - (pad - 1), 1, w).astype(f32)
            du = _window_sum(dpool / cnt, gi + 1, tp, True) - dpool
            du_ref[:, cols] = jnp.where(row >= pad, du, 0.0).astype(bf16)

    return pl.pallas_call(
        body, name=name,
        out_shape=[jax.ShapeDtypeStruct((tp, RW), bf16), jax.ShapeDtypeStruct((HEADS * HD, HD), f32),
                   jax.ShapeDtypeStruct((8, RW), f32)],
        compiler_params=_cp(56))(dp, pooled, maps, scale)


def _group_norm(o):
    mu = jnp.mean(o, axis=-1, keepdims=True)
    oc = o - mu
    rstd = lax.rsqrt(jnp.mean(oc * oc, axis=-1, keepdims=True) + EPS)
    return oc * rstd, rstd


def mix_out_fwd(name, h, oraw, zg, zgate, p, wretT, wpoolT, wout, deps=()):
    tp, d = h.shape
    tm = _row_tile(tp)

    def body(h_ref, o_ref, zg_ref, zgate_ref, p_ref, wr_ref, wp_ref, wo_ref, ho_ref, rp_ref, ret_ref, pool_ref, mixed_ref):
        parts = []
        for hh in range(HEADS):
            cols = slice(hh * HD, (hh + 1) * HD)
            rhat, _ = _group_norm(o_ref[:, cols])
            gv = zg_ref[:, cols].astype(f32)
            parts.append(rhat * (gv * _sigmoid(gv)))
        r = jnp.concatenate(parts, axis=-1).astype(bf16)
        pv = p_ref[...]
        ret = _dot_nt(r, wr_ref[...])
        pool = _dot_nt(pv, wp_ref[...])
        mixed = (_sigmoid(zgate_ref[:, :d].astype(f32)) * ret + _sigmoid(zgate_ref[:, d:].astype(f32)) * pool).astype(bf16)
        ho_ref[...] = h_ref[...] + _dot_nn(mixed, wo_ref[...])
        rp_ref[0] = r
        rp_ref[1] = pv
        ret_ref[...] = ret.astype(bf16)
        pool_ref[...] = pool.astype(bf16)
        mixed_ref[...] = mixed

    row = lambda w: pl.BlockSpec((tm, w), lambda i: (i, 0))
    return pl.pallas_call(
        _skip(len(deps), body), name=name, grid=(tp // tm,),
        in_specs=[_ANY] * len(deps) + [row(d), row(RW), row(RW), row(2 * d), row(RW), _resident((d, RW)), _resident((d, RW)),
                                       _resident((d, d))],
        out_specs=[row(d), pl.BlockSpec((2, tm, RW), lambda i: (0, i, 0)), row(d), row(d),
                   pl.BlockSpec((None, tm, d), lambda i: (0, i, 0))],
        out_shape=[jax.ShapeDtypeStruct((tp, d), f32), jax.ShapeDtypeStruct((2, tp, RW), bf16),
                   jax.ShapeDtypeStruct((tp, d), bf16), jax.ShapeDtypeStruct((tp, d), bf16),
                   jax.ShapeDtypeStruct((1, tp, d), bf16)],
        compiler_params=_cp(48, ("arbitrary",)))(*deps, h, oraw, zg, zgate, p, wretT, wpoolT, wout)


def mix_out_bwd(name, dy, oraw, zg, zgate, ret, pool, wretT, wpoolT, wout, deps=()):
    tp, d = dy.shape
    tm = _row_tile(tp)

    def body(dy_ref, o_ref, zg_ref, zgate_ref, ret_ref, pool_ref, wr_ref, wp_ref, wo_ref,
             do_ref, dzg_ref, dzgate_ref, dp_ref, drp_ref, dyb_ref):
        dyb = dy_ref[...].astype(bf16)
        dmixed = _dot_nt(dyb, wo_ref[...])
        sa = _sigmoid(zgate_ref[:, :d].astype(f32))
        sb = _sigmoid(zgate_ref[:, d:].astype(f32))
        dret = dmixed * sa
        dpool = dmixed * sb
        dzgate_ref[:, :d] = (dret * ret_ref[...].astype(f32) * (1.0 - sa)).astype(bf16)
        dzgate_ref[:, d:] = (dpool * pool_ref[...].astype(f32) * (1.0 - sb)).astype(bf16)
        dretb, dpoolb = dret.astype(bf16), dpool.astype(bf16)
        dr = _dot_nn(dretb, wr_ref[...])
        dp_ref[...] = _dot_nn(dpoolb, wp_ref[...]).astype(bf16)
        for hh in range(HEADS):
            cols = slice(hh * HD, (hh + 1) * HD)
            rhat, rstd = _group_norm(o_ref[:, cols])
            gv = zg_ref[:, cols].astype(f32)
            sg = _sigmoid(gv)
            drh = dr[:, cols]
            drhat = drh * (gv * sg)
            dzg_ref[:, cols] = (drh * rhat * (sg * (1.0 + gv * (1.0 - sg)))).astype(bf16)
            do = rstd * (drhat - jnp.mean(drhat, axis=-1, keepdims=True)
                         - rhat * jnp.mean(drhat * rhat, axis=-1, keepdims=True))
            do_ref[:, cols] = do.astype(bf16)
        drp_ref[0] = dretb
        drp_ref[1] = dpoolb
        dyb_ref[...] = dyb

    row = lambda w: pl.BlockSpec((tm, w), lambda i: (i, 0))
    return pl.pallas_call(
        _skip(len(deps), body), name=name, grid=(tp // tm,),
        in_specs=[_ANY] * len(deps) + [row(d), row(RW), row(RW), row(2 * d), row(d), row(d), _resident((d, RW)), _resident((d, RW)),
                  _resident((d, d))],
        out_specs=[row(RW), row(RW), row(2 * d), row(RW), pl.BlockSpec((2, tm, d), lambda i: (0, i, 0)),
                   pl.BlockSpec((None, tm, d), lambda i: (0, i, 0))],
        out_shape=[jax.ShapeDtypeStruct((tp, RW), bf16), jax.ShapeDtypeStruct((tp, RW), bf16),
                   jax.ShapeDtypeStruct((tp, 2 * d), bf16), jax.ShapeDtypeStruct((tp, RW), bf16),
                   jax.ShapeDtypeStruct((2, tp, d), bf16), jax.ShapeDtypeStruct((1, tp, d), bf16)],
        compiler_params=_cp(48, ("arbitrary",)))(*deps, dy, oraw, zg, zgate, ret, pool, wretT, wpoolT, wout)


def final_loss(name, h, g, target):
    tp, d = h.shape
    tm = _row_tile(tp)
    nsub = tm // CHUNK

    def body(h_ref, g_ref, *rest):
        t_refs = rest[:nsub]
        dh_ref, loss_ref, dg_ref = rest[nsub:]
        i = pl.program_id(0)

        @pl.when(i == 0)
        def _():
            loss_ref[...] = jnp.zeros_like(loss_ref)
            dg_ref[...] = jnp.zeros_like(dg_ref)

        gv = g_ref[...]
        for j in range(nsub):
            rows = slice(j * CHUNK, (j + 1) * CHUNK)
            hhat, rs = _rms(h_ref[rows, :])
            err = jnp.where(i * nsub + j >= 1, hhat * gv - t_refs[j][...], 0.0)
            dyv = err / d
            dh_ref[rows, :] = _rms_bwd(dyv, gv, hhat, rs)
            loss_ref[...] += 0.5 * jnp.sum(jnp.sum(err * err, axis=-1, keepdims=True) / d)
            dg_ref[0:1, :] += jnp.sum(dyv * hhat, axis=0, keepdims=True)

    lagged = lambda j: pl.BlockSpec((CHUNK, d), lambda i: (jnp.maximum(i * nsub + j - 1, 0), 0))
    return pl.pallas_call(
        body, name=name, grid=(tp // tm,),
        in_specs=[pl.BlockSpec((tm, d), lambda i: (i, 0)), _resident((1, d))] + [lagged(j) for j in range(nsub)],
        out_specs=[pl.BlockSpec((tm, d), lambda i: (i, 0)), pl.BlockSpec((8, 128), lambda i: (0, 0)),
                   pl.BlockSpec((8, d), lambda i: (0, 0))],
        out_shape=[jax.ShapeDtypeStruct((tp, d), f32), jax.ShapeDtypeStruct((8, 128), f32),
                   jax.ShapeDtypeStruct((8, d), f32)],
        compiler_params=_cp(32, ("arbitrary",)))(h, g, *[target] * nsub)


def _adamw(w, g, m, v):
    m = ADAM_B1 * m + (1.0 - ADAM_B1) * g
    v = ADAM_B2 * v + (1.0 - ADAM_B2) * (g * g)
    m_hat = m / (1.0 - ADAM_B1 ** ADAM_STEP)
    v_hat = v / (1.0 - ADAM_B2 ** ADAM_STEP)
    delta = -ADAM_LR * (m_hat / (jnp.sqrt(v_hat) + ADAM_EPS) + ADAM_WD * w)
    return delta, m, v


def adam_big(name, me, recv, own, b, layer, transposed, w, m, v, prev):
    r, c = recv.shape[2:]
    wshape = w.shape[1:]
    nchunk = 1 if transposed else next(k for k in (4, 2, 1) if r % (16 * k) == 0)
    rc = r // nchunk

    def body(me_ref, recv_ref, own_ref, w_ref, m_ref, v_ref, *rest):
        g_ref, d_ref, nm_ref, nv_ref = rest[-4:]
        g = own_ref[...].astype(f32)
        for j in range(NDEV - 1):
            g = g + recv_ref[j].astype(f32)
        if transposed:
            g = g.T
        delta, nm, nv = _adamw(w_ref[...], g, m_ref[...], v_ref[...])
        g_ref[...] = g
        d_ref[...] = delta
        nm_ref[...] = nm
        nv_ref[...] = nv

    wblock = wshape if transposed else (rc, c)
    wspec = pl.BlockSpec((None,) + wblock, lambda i, me_ref: (layer, i, 0))
    in_specs = [pl.BlockSpec((NDEV - 1, None, rc, c), lambda i, me_ref: (0, b, i, 0)),
                pl.BlockSpec((None, None, rc, c), lambda i, me_ref: (b, me_ref[0], i, 0)), wspec, wspec, wspec]
    args = [recv, own, w, m, v]
    aliases = {}
    if prev is not None:
        in_specs += [_ANY] * 4
        args += list(prev)
        aliases = {6 + k: k for k in range(4)}
    return pl.pallas_call(
        body, name=name,
        grid_spec=pltpu.PrefetchScalarGridSpec(num_scalar_prefetch=1, grid=(nchunk,), in_specs=in_specs,
                                               out_specs=[wspec] * 4),
        out_shape=[jax.ShapeDtypeStruct(w.shape, f32)] * 4, input_output_aliases=aliases,
        compiler_params=_cp(56))(me, *args)


def adam_small(name, ga0, gmaps0, gmeta, ga1, gmaps1, norms, pool_scale, pool_maps, meta, final_norm, d):
    def body(ga0_ref, gmaps0_ref, gmeta_ref, ga1_ref, gmaps1_ref, *refs):
        ins, outs = refs[:21], refs[21:]
        x, y, c = _me()
        me = 4 * x + 2 * y + c

        def total(ref, rows):
            t = ref[0, rows, :]
            for j in range(1, NDEV):
                t = t + ref[j, rows, :]
            return t

        row = lambda r: slice(r, r + 1)
        outs[0][...] = jnp.broadcast_to(total(ga1_ref, row(0))[:, :128], (8, 128))

        def update(k, g, o):
            w_ref, m_ref, v_ref = ins[3 * k:3 * k + 3]
            delta, nm, nv = _adamw(w_ref[...], g, m_ref[...], v_ref[...])
            for ref, val in zip(outs[o:o + 4], (g, delta, nm, nv)):
                ref[...] = val

        two = lax.broadcasted_iota(jnp.int32, (2, d), 0)
        for k in range(3):
            update(k, jnp.where(two == 0, total(ga0_ref, row(k)), total(ga1_ref, row(2 + k))), 1 + 4 * k)
        update(3, jnp.where(two[:, :RW] == 0, total(ga0_ref, row(3))[:, :RW], total(ga1_ref, row(5))[:, :RW]), 13)
        update(4, jnp.concatenate([total(gmaps0_ref, slice(None)), total(gmaps1_ref, slice(None))], axis=0), 17)
        update(5, total(gmeta_ref, pl.ds(pl.multiple_of(me * N_META, N_META), N_META)), 21)
        update(6, total(ga1_ref, row(1)), 25)

    flat = []
    for trip in (*norms, pool_scale, pool_maps, meta, final_norm):
        flat += list(trip)
    out_shapes = [jax.ShapeDtypeStruct((8, 128), f32)]
    for trip in (*norms, pool_scale, pool_maps, meta, final_norm):
        out_shapes += [jax.ShapeDtypeStruct(trip[0].shape, f32)] * 4
    return pl.pallas_call(body, name=name, out_shape=out_shapes,
                          compiler_params=_cp(32))(ga0, gmaps0, gmeta, ga1, gmaps1, *flat)


def kernel(x, meta, ffn1_norm, ffn1_gate, ffn1_up, ffn1_down, mix_norm, w_in, pool_maps, pool_scale, w_ret_up, w_pool_up, w_out, ffn2_norm, ffn2_gate, ffn2_up, ffn2_down, final_norm, loss_target, m_meta, m_ffn1_norm, m_ffn1_gate, m_ffn1_up, m_ffn1_down, m_mix_norm, m_w_in, m_pool_maps, m_pool_scale, m_w_ret_up, m_w_pool_up, m_w_out, m_ffn2_norm, m_ffn2_gate, m_ffn2_up, m_ffn2_down, m_final_norm, v_meta, v_ffn1_norm, v_ffn1_gate, v_ffn1_up, v_ffn1_down, v_mix_norm, v_w_in, v_pool_maps, v_pool_scale, v_w_ret_up, v_w_pool_up, v_w_out, v_ffn2_norm, v_ffn2_gate, v_ffn2_up, v_ffn2_down, v_final_norm):
    seq, d = x.shape[1], x.shape[2]
    depth = ffn1_gate.shape[0]
    ff = ffn1_gate.shape[2] * NDEV
    nin = w_in.shape[2] * NDEV
    length = seq + N_META
    pad = (-length) % CHUNK
    tp = length + pad
    assert pad % 8 == 0 and pad + N_META == CHUNK and depth == 2 and nin == 5 * RW + 2 * d

    ix, iy, ic = _me()
    me = (4 * ix + 2 * iy + ic).astype(jnp.int32).reshape(1)

    meta_all, = all_gather("gather_meta", [meta])
    meta_full = jnp.transpose(meta_all, (1, 0, 2)).reshape(N_META, d)

    gathers = {}
    token = meta_all
    tview = lambda *arrs: [jnp.swapaxes(a, 1, 2) for a in arrs]
    t_g1, t_u1, t_g2, t_u2, t_in = (tview(w, m, v) for w, m, v in (
        (ffn1_gate, m_ffn1_gate, v_ffn1_gate), (ffn1_up, m_ffn1_up, v_ffn1_up), (ffn2_gate, m_ffn2_gate, v_ffn2_gate),
        (ffn2_up, m_ffn2_up, v_ffn2_up), (w_in, m_w_in, v_w_in)))
    keys, groups = [], []
    for layer in range(depth):
        lands = prep_layer(layer, me, [w_ret_up, w_pool_up],
                           [t_g1[0], t_u1[0], t_g2[0], t_u2[0], t_in[0], ffn1_down, ffn2_down, w_out],
                           (token,) if layer else ())
        wretT, wpoolT, g1T, u1T, g2T, u2T, winT, d1, d2, wout = lands
        keys += [("ffn1", layer), ("mix", layer), ("ffn2", layer)]
        groups += [[g1T, u1T, d1], [winT, wretT, wpoolT, wout], [g2T, u2T, d2]]
        if layer == 0:
            first, token = gather_start_chips("gather_start_first", groups[:1], (token,))
    started, token = gather_start_chips("gather_start_rest", groups[1:], (token,))
    gathers = dict(zip(keys, first + started))

    def forward(part, layer, after):
        ssem, rsem, group = gathers[(part, layer)]
        ssem, rsem, group, tok = gather_forward(f"gather_forward_{part}{layer}", ssem, rsem, group, after)
        gathers[(part, layer)] = (ssem, rsem, group)
        return tok

    def gathered(part, layer, after):
        ssem, rsem, group = gathers[(part, layer)]
        _, full = copies_wait(f"gather_wait_{part}{layer}", ssem, rsem, (), group, after, 3)
        return [a.reshape((NDEV * a.shape[1],) + a.shape[2:]) for a in full]

    cos2, sin2, dec = _retention_tables(tp, pad)
    h = jnp.concatenate([jnp.zeros((pad, d), f32), meta_full + token[0, 0], x[0]], axis=0)

    saved = []
    weights = []
    tok = forward("ffn1", 0, token)
    for layer in range(depth):
        row = lambda a: a[layer:layer + 1]
        s = {"h0": h}
        g1T, u1T, d1 = gathered("ffn1", layer, tok if layer == 0 else h)
        tok = forward("mix", layer, h) if layer else None
        h, s["a1"], s["b1"] = ffn_fwd(f"ffn1_fwd{layer}", h, row(ffn1_norm), g1T, u1T, d1, (tok,) if layer else ())
        s["h1"] = h
        if layer == 0:
            tok = forward("mix", layer, h)
        winT, wretT, wpoolT, wout = gathered("mix", layer, tok if layer == 0 else h)
        s["zq"], s["zg"], zu, s["zgate"] = mix_in_fwd(f"mix_in_fwd{layer}", h, row(mix_norm), winT, cos2, sin2)
        s["oraw"], s["states"] = retention_fwd(f"retention_fwd{layer}", s["zq"], dec)
        s["pooled"], p = pool_fwd(f"pool_fwd{layer}", zu, pool_maps[layer], row(pool_scale), pad)
        tok = forward("ffn2", layer, p)
        h, s["rp"], s["ret"], s["pool"], s["mixed"] = mix_out_fwd(
            f"mix_out_fwd{layer}", h, s["oraw"], s["zg"], s["zgate"], p, wretT, wpoolT, wout, (tok,))
        s["h2"] = h
        g2T, u2T, d2 = gathered("ffn2", layer, h)
        tok = (forward("ffn1", layer + 1, h),) if layer + 1 < depth else ()
        h, s["a2"], s["b2"] = ffn_fwd(f"ffn2_fwd{layer}", h, row(ffn2_norm), g2T, u2T, d2, tok)
        saved.append(s)
        weights.append((g1T, u1T, g2T, u2T, winT, wretT, wpoolT, d1, d2, wout))

    dh, loss_part, dg_final = final_loss("final_loss", h, final_norm.reshape(1, d), loss_target[0])

    small = {}
    small_gathers = {}
    exchanges = {}
    token = None

    def rows8(vals):
        at = lax.broadcasted_iota(jnp.int32, (8, d), 0)
        out = jnp.zeros((8, d), f32)
        for k, v in enumerate(vals):
            r0 = v[0:1]
            r0 = r0 if r0.shape[1] == d else jnp.pad(r0, ((0, 0), (0, d - r0.shape[1])))
            out = jnp.where(at == k, r0, out)
        return out

    def exchange(part, layer, grads):
        by_dest = [g.reshape(g.shape[0], NDEV, g.shape[1] // NDEV, g.shape[2]) for g in grads]
        ssem, rsem, sent, lands, tok = exchange_start(f"exchange_start_{part}{layer}", by_dest)
        exchanges[(part, layer)] = (ssem, rsem, sent, lands)
        return (tok,)

    for layer in reversed(range(depth)):
        g1T, u1T, g2T, u2T, winT, wretT, wpoolT, d1, d2, wout = weights[layer]
        row = lambda a: a[layer:layer + 1]
        s = saved[layer]
        dh, lhs2, rhs2, small[("ffn2", layer)] = ffn_bwd(
            f"ffn2_bwd{layer}", dh, s["h2"], row(ffn2_norm), s["a2"], s["b2"], g2T, u2T, d2, () if token is None else token)
        token = exchange("ffn2", layer, [mm_tn(f"ffn2_wgrad{layer}", lhs2, rhs2, lambda b: b // 2)])
        do, dzg, dzgate, dp, drp, dyb = mix_out_bwd(
            f"mix_out_bwd{layer}", dh, s["oraw"], s["zg"], s["zgate"], s["ret"], s["pool"], wretT, wpoolT, wout, token)
        gw_mix = [mm_tn(f"w_out_wgrad{layer}", s["mixed"], dyb, lambda b: b),
                  mm_tn(f"up_wgrad{layer}", drp, s["rp"], lambda b: b)]
        dq, dk, dv = retention_bwd(f"retention_bwd{layer}", s["zq"], cos2, sin2, dec, s["states"], do, pad, token)
        dzu, small[("maps", layer)], small[("scale", layer)] = pool_bwd(
            f"pool_bwd{layer}", dp, s["pooled"], pool_maps[layer], row(pool_scale), pad)
        dh, dz, n2, small[("mix", layer)] = mix_in_bwd(
            f"mix_in_bwd{layer}", dq, dk, dv, dzg, dzu, dzgate, s["h1"], row(mix_norm), winT, dh)
        token = exchange("mix", layer, gw_mix + [mm_tn(f"w_in_wgrad{layer}", dz, n2, lambda b: b)])
        dh, lhs1, rhs1, small[("ffn1", layer)] = ffn_bwd(
            f"ffn1_bwd{layer}", dh, s["h0"], row(ffn1_norm), s["a1"], s["b1"], g1T, u1T, d1, token)
        rows = [small[("ffn1", layer)], small[("mix", layer)], small[("ffn2", layer)], small[("scale", layer)]]
        packs = [rows8([loss_part, dg_final] + rows if layer == depth - 1 else rows), small[("maps", layer)]]
        if layer == 0:
            dmeta = dh[pad:CHUNK]
            packs.append(jnp.transpose(dmeta.reshape(N_META, NDEV, d // NDEV), (1, 0, 2)).reshape(NDEV * N_META, d // NDEV))
        ssem, rsem, lands, tok = gather_start(f"small_start{layer}", slot_in(f"small_slot{layer}", me, packs))
        small_gathers[layer] = (ssem, rsem, lands)
        if layer:
            token = exchange("ffn1", layer, [mm_tn(f"ffn1_wgrad{layer}", lhs1, rhs1, lambda b: b // 2, (tok,))])
        else:
            token = (tok,)
            for j, nm in enumerate(("ffn1_gate", "ffn1_up", "ffn1_down")):
                token = exchange(nm, layer, [mm_tn(f"{nm}_wgrad{layer}", lhs1, rhs1, lambda b: b // 2, token, only=j)])

    grad_x = dh[CHUNK:][None]

    big = {}
    after = token[0]
    plans = {
        "ffn2": [("ffn2_gate", 0, 0, False, *t_g2), ("ffn2_up", 0, 1, False, *t_u2),
                 ("ffn2_down", 0, 2, False, ffn2_down, m_ffn2_down, v_ffn2_down)],
        "mix": [("w_out", 0, 0, False, w_out, m_w_out, v_w_out),
                ("w_ret_up", 1, 0, True, w_ret_up, m_w_ret_up, v_w_ret_up),
                ("w_pool_up", 1, 1, True, w_pool_up, m_w_pool_up, v_w_pool_up), ("w_in", 2, 0, False, *t_in)],
        "ffn1": [("ffn1_gate", 0, 0, False, *t_g1), ("ffn1_up", 0, 1, False, *t_u1),
                 ("ffn1_down", 0, 2, False, ffn1_down, m_ffn1_down, v_ffn1_down)]}
    for nm, k, b, tr, w, m, v in plans["ffn1"]:
        plans[nm] = [(nm, 0, 0, tr, w, m, v)]
    for layer in reversed(range(depth)):
        for part in ("ffn2", "mix") + (("ffn1",) if layer else ("ffn1_gate", "ffn1_up", "ffn1_down")):
            ssem, rsem, sent, lands = exchanges[(part, layer)]
            sent, lands = copies_wait(f"exchange_wait_{part}{layer}", ssem, rsem, sent, lands, after)
            for nm, k, b, tr, w, m, v in plans[part]:
                big[nm] = adam_big(f"adam_{nm}{layer}", me, lands[k], sent[k], b, layer, tr, w, m, v, big.get(nm))
                after = big[nm][0]

    gsmall = []
    for layer in range(depth):
        ssem, rsem, lands = small_gathers[layer]
        gsmall += copies_wait(f"small_wait{layer}", ssem, rsem, (), lands, after)[1]

    maps2 = lambda a: a.reshape(depth * HEADS * HD, HD)
    res = adam_small(
        "adam_small", *gsmall,
        [(ffn1_norm, m_ffn1_norm, v_ffn1_norm), (mix_norm, m_mix_norm, v_mix_norm), (ffn2_norm, m_ffn2_norm, v_ffn2_norm)],
        (pool_scale, m_pool_scale, v_pool_scale), (maps2(pool_maps), maps2(m_pool_maps), maps2(v_pool_maps)),
        (meta, m_meta, v_meta), tuple(a.reshape(1, d) for a in (final_norm, m_final_norm, v_final_norm)), d)
    loss = res[0][0, 0]
    sm = {}
    for k, nm in enumerate(["ffn1_norm", "mix_norm", "ffn2_norm", "pool_scale", "pool_maps", "meta", "final_norm"]):
        sm[nm] = list(res[1 + 4 * k:5 + 4 * k])
    sm["pool_maps"] = [a.reshape(pool_maps.shape) for a in sm["pool_maps"]]
    sm["final_norm"] = [a.reshape(d) for a in sm["final_norm"]]

    names = ["meta", "ffn1_norm", "ffn1_gate", "ffn1_up", "ffn1_down", "mix_norm", "w_in", "pool_maps", "pool_scale",
             "w_ret_up", "w_pool_up", "w_out", "ffn2_norm", "ffn2_gate", "ffn2_up", "ffn2_down", "final_norm"]
    for nm in ("ffn1_gate", "ffn1_up", "ffn2_gate", "ffn2_up", "w_in"):
        big[nm] = tview(*big[nm])
    allw = {**{k: list(v) for k, v in big.items()}, **sm}
    outs = [loss, grad_x]
    for kind in range(4):
        outs += [allw[nm][kind] for nm in names]
    return tuple(outs)
```

```python
import functools

import jax
import jax.numpy as jnp
from jax import lax
from jax.experimental import pallas as pl
from jax.experimental.pallas import tpu as pltpu

f32 = jnp.float32
bf16 = jnp.bfloat16
MESH = pl.DeviceIdType.MESH
NDEV = 8
N_META = 16
HEADS = 4
HD = 128
CHUNK = 128
RW = HEADS * HD
POOL_WINDOWS = (2, 4, 8, 16)
ROPE_BASE = 10000.0
EPS = 1e-6
ADAM_LR = 0.001
ADAM_B1 = 0.9
ADAM_B2 = 0.999
ADAM_EPS = 1e-08
ADAM_WD = 0.01
ADAM_STEP = 10
VMEM_CAP_MB = 60


def _cp(vmem_mb, sem=None):
    return pltpu.CompilerParams(vmem_limit_bytes=min(vmem_mb, VMEM_CAP_MB) * 2**20, dimension_semantics=sem)


def _row_tile(tp, want=384):
    return want if tp % want == 0 else 128


def _resident(shape):
    nd = len(shape)
    return pl.BlockSpec(shape, lambda *_: (0,) * nd, pipeline_mode=pl.Buffered(1))


def _skip(nd, body):
    return (lambda *refs: body(*refs[nd:])) if nd else body


def _dot_nn(a, b):
    return lax.dot_general(a, b, (((1,), (0,)), ((), ())), preferred_element_type=f32)


def _dot_nt(a, b):
    return lax.dot_general(a, b, (((1,), (1,)), ((), ())), preferred_element_type=f32)


def _dot_tn(a, b):
    return lax.dot_general(a, b, (((0,), (0,)), ((), ())), preferred_element_type=f32)


def _rms(h):
    rs = lax.rsqrt(jnp.mean(h * h, axis=-1, keepdims=True) + EPS)
    return h * rs, rs


def _rms_bwd(dn, g, hhat, rs):
    dhh = dn * g
    return rs * (dhh - hhat * jnp.mean(dhh * hhat, axis=-1, keepdims=True))


def _sigmoid(x):
    return jax.nn.sigmoid(x)


def _me():
    return lax.axis_index("x"), lax.axis_index("y"), lax.axis_index("c")


def _peer(idx):
    return (idx // 4, (idx // 2) % 2, idx % 2)


def all_gather(name, arrays):
    n = len(arrays)

    def body(*refs):
        ins, outs = refs[:n], refs[n:2 * n]
        send_sems, recv_sems, local_sems = refs[2 * n:]
        x, y, c = _me()
        me = 4 * x + 2 * y + c
        locals_ = []
        for k in range(n):
            cp = pltpu.make_async_copy(ins[k], outs[k].at[me], local_sems.at[k])
            cp.start()
            locals_.append(cp)
        for d in range(1, NDEV):
            for k in range(n):
                pltpu.make_async_remote_copy(
                    src_ref=ins[k], dst_ref=outs[k].at[me], send_sem=send_sems.at[k], recv_sem=recv_sems.at[k],
                    device_id=_peer((me + d) % NDEV), device_id_type=MESH).start()
        for k in range(n):
            seven = outs[k].at[pl.ds(0, NDEV - 1)]
            w = pltpu.make_async_remote_copy(src_ref=seven, dst_ref=seven, send_sem=send_sems.at[k],
                                             recv_sem=recv_sems.at[k], device_id=(x, y, c), device_id_type=MESH)
            w.wait_send()
            w.wait_recv()
            locals_[k].wait()

    anyspec = pl.BlockSpec(memory_space=pl.ANY)
    return pl.pallas_call(
        body, name=name,
        out_shape=[jax.ShapeDtypeStruct((NDEV,) + a.shape, a.dtype) for a in arrays],
        in_specs=[anyspec] * n, out_specs=[anyspec] * n,
        scratch_shapes=[pltpu.SemaphoreType.DMA((n,)), pltpu.SemaphoreType.DMA((n,)), pltpu.SemaphoreType.DMA((n,))],
    )(*arrays)


_HBM = pl.BlockSpec(memory_space=pltpu.HBM)
_SEM = pl.BlockSpec(memory_space=pltpu.SEMAPHORE)
_ANY = pl.BlockSpec(memory_space=pl.ANY)
_EFFECT = pltpu.SideEffectType.DATAFLOW_SIDE_EFFECTING


def _in_hbm(a):
    return pltpu.with_memory_space_constraint(a, pltpu.HBM)


def gather_start(name, lands, deps=()):
    n, nd = len(lands), len(deps)

    def body(*refs):
        land = refs[nd:nd + n]
        send_sems, recv_sems = refs[nd + n:nd + n + 2]
        token = refs[-1]
        x, y, c = _me()
        me = 4 * x + 2 * y + c
        for d in range(1, NDEV):
            for k in range(n):
                pltpu.make_async_remote_copy(
                    src_ref=land[k].at[me], dst_ref=land[k].at[me], send_sem=send_sems.at[k], recv_sem=recv_sems.at[k],
                    device_id=_peer((me + d) % NDEV), device_id_type=MESH).start()
        token[...] = jnp.zeros_like(token)

    res = pl.pallas_call(
        body, name=name,
        out_shape=(pltpu.SemaphoreType.DMA((n,)), pltpu.SemaphoreType.DMA((n,)),
                   *[pltpu.HBM(a.shape, a.dtype) for a in lands], jax.ShapeDtypeStruct((8, 128), f32)),
        in_specs=[_ANY] * nd + [_HBM] * n,
        out_specs=(_SEM, _SEM, *[_HBM] * n, pl.BlockSpec(memory_space=pltpu.VMEM)),
        input_output_aliases={nd + k: 2 + k for k in range(n)},
        compiler_params=pltpu.CompilerParams(has_side_effects=_EFFECT),
    )(*deps, *[_in_hbm(a) for a in lands])
    return res[0], res[1], list(res[2:2 + n]), res[-1]


def _other_chips(x, y):
    return [(1 - x, y), (x, 1 - y), (1 - x, 1 - y)]


def gather_start_chips(name, groups, deps=()):
    sizes = [len(g) for g in groups]
    lands = [a for g in groups for a in g]
    n, nd, ng = len(lands), len(deps), len(groups)

    def body(*refs):
        land = refs[nd:nd + n]
        sems = refs[nd + n:nd + n + 2 * ng]
        token = refs[-1]
        x, y, c = _me()
        me = 4 * x + 2 * y + c
        k = 0
        for g, size in enumerate(sizes):
            for j in range(size):
                for to in [(x, y, 1 - c)] + [(cx, cy, c) for cx, cy in _other_chips(x, y)]:
                    pltpu.make_async_remote_copy(
                        src_ref=land[k].at[me], dst_ref=land[k].at[me], send_sem=sems[2 * g].at[j],
                        recv_sem=sems[2 * g + 1].at[j], device_id=to, device_id_type=MESH).start()
                k += 1
        token[...] = jnp.zeros_like(token)

    res = pl.pallas_call(
        body, name=name,
        out_shape=(*[pltpu.SemaphoreType.DMA((size,)) for size in sizes for _ in range(2)],
                   *[pltpu.HBM(a.shape, a.dtype) for a in lands], jax.ShapeDtypeStruct((8, 128), f32)),
        in_specs=[_ANY] * nd + [_HBM] * n,
        out_specs=(*[_SEM] * (2 * ng), *[_HBM] * n, pl.BlockSpec(memory_space=pltpu.VMEM)),
        input_output_aliases={nd + k: 2 * ng + k for k in range(n)},
        compiler_params=pltpu.CompilerParams(has_side_effects=_EFFECT),
    )(*deps, *[_in_hbm(a) for a in lands])
    out, k = [], 2 * ng
    for g, size in enumerate(sizes):
        out.append((res[2 * g], res[2 * g + 1], list(res[k:k + size])))
        k += size
    return out, res[-1]


def gather_forward(name, send_sems, recv_sems, lands, after):
    n = len(lands)

    def body(*refs):
        land = refs[:n]
        ssem, rsem = refs[n:n + 2]
        send2, recv2 = refs[n + 3:n + 5]
        token = refs[-1]
        x, y, c = _me()
        for k in range(n):
            four = land[k].at[pl.ds(0, 4)]
            w = pltpu.make_async_remote_copy(src_ref=four, dst_ref=four, send_sem=ssem.at[k], recv_sem=rsem.at[k],
                                             device_id=(x, y, c), device_id_type=MESH)
            w.wait_send()
            w.wait_recv()
            for cx, cy in _other_chips(x, y):
                slot = 4 * cx + 2 * cy + c
                pltpu.make_async_remote_copy(
                    src_ref=land[k].at[slot], dst_ref=land[k].at[slot], send_sem=send2.at[k], recv_sem=recv2.at[k],
                    device_id=(x, y, 1 - c), device_id_type=MESH).start()
        token[...] = jnp.zeros_like(token)

    res = pl.pallas_call(
        body, name=name,
        out_shape=(pltpu.SemaphoreType.DMA((n,)), pltpu.SemaphoreType.DMA((n,)),
                   *[pltpu.HBM(a.shape, a.dtype) for a in lands], jax.ShapeDtypeStruct((8, 128), f32)),
        in_specs=[_HBM] * n + [_SEM, _SEM, _ANY],
        out_specs=(_SEM, _SEM, *[_HBM] * n, pl.BlockSpec(memory_space=pltpu.VMEM)),
        input_output_aliases={k: 2 + k for k in range(n)},
        compiler_params=pltpu.CompilerParams(has_side_effects=_EFFECT),
    )(*lands, send_sems, recv_sems, after)
    return res[0], res[1], list(res[2:2 + n]), res[-1]


def exchange_start(name, grads, deps=()):
    n, nd = len(grads), len(deps)
    lands = [lax.empty((NDEV - 1, g.shape[0]) + g.shape[2:], g.dtype) for g in grads]

    def body(*refs):
        src = refs[nd:nd + n]
        land = refs[nd + n:nd + 2 * n]
        send_sems, recv_sems = refs[nd + 2 * n:nd + 2 * n + 2]
        token = refs[-1]
        x, y, c = _me()
        me = 4 * x + 2 * y + c
        for d in range(1, NDEV):
            p = (me + d) % NDEV
            for k in range(n):
                pltpu.make_async_remote_copy(
                    src_ref=src[k].at[:, p], dst_ref=land[k].at[d - 1], send_sem=send_sems.at[k], recv_sem=recv_sems.at[k],
                    device_id=_peer(p), device_id_type=MESH).start()
        token[...] = jnp.zeros_like(token)

    both = list(grads) + lands
    res = pl.pallas_call(
        body, name=name,
        out_shape=(pltpu.SemaphoreType.DMA((n,)), pltpu.SemaphoreType.DMA((n,)),
                   *[pltpu.HBM(a.shape, a.dtype) for a in both], jax.ShapeDtypeStruct((8, 128), f32)),
        in_specs=[_ANY] * nd + [_HBM] * (2 * n),
        out_specs=(_SEM, _SEM, *[_HBM] * (2 * n), pl.BlockSpec(memory_space=pltpu.VMEM)),
        input_output_aliases={nd + k: 2 + k for k in range(2 * n)},
        compiler_params=pltpu.CompilerParams(has_side_effects=_EFFECT),
    )(*deps, *[_in_hbm(a) for a in both])
    return res[0], res[1], list(res[2:2 + n]), list(res[2 + n:2 + 2 * n]), res[-1]


def copies_wait(name, send_sems, recv_sems, sent, lands, after, count=NDEV - 1):
    ns, n = len(sent), len(lands)

    def body(*refs):
        land = refs[ns:ns + n]
        ssem, rsem = refs[ns + n:ns + n + 2]
        x, y, c = _me()
        for k in range(n):
            seven = land[k].at[pl.ds(0, count)]
            w = pltpu.make_async_remote_copy(src_ref=seven, dst_ref=seven, send_sem=ssem.at[k], recv_sem=rsem.at[k],
                                             device_id=(x, y, c), device_id_type=MESH)
            w.wait_send()
            w.wait_recv()

    both = list(sent) + list(lands)
    res = pl.pallas_call(
        body, name=name, out_shape=tuple(pltpu.HBM(a.shape, a.dtype) for a in both),
        in_specs=[_HBM] * (ns + n) + [_SEM, _SEM, _ANY], out_specs=tuple([_HBM] * (ns + n)),
        input_output_aliases={k: k for k in range(ns + n)},
        compiler_params=pltpu.CompilerParams(has_side_effects=_EFFECT),
    )(*both, send_sems, recv_sems, after)
    return list(res[:ns]), list(res[ns:])


def prep_layer(layer, me, col_sharded, row_sharded, deps=()):
    nc, nr = len(col_sharded), len(row_sharded)

    def body(me_ref, *refs):
        ins, outs = refs[:nc + nr], refs[nc + nr + len(deps):]
        for k in range(nc):
            outs[k][...] = ins[k][...].T.astype(bf16)
        for k in range(nc, nc + nr):
            outs[k][...] = ins[k][...].astype(bf16)

    arrs = list(col_sharded) + list(row_sharded)
    in_specs = [pl.BlockSpec((None,) + a.shape[1:], lambda i, me_ref: (layer, 0, 0)) for a in arrs]
    shapes = [(a.shape[2], a.shape[1]) for a in col_sharded] + [a.shape[1:] for a in row_sharded]
    out_specs = [pl.BlockSpec((None,) + s, lambda i, me_ref: (me_ref[0], 0, 0)) for s in shapes]
    return pl.pallas_call(
        body, name=f"prep_layer{layer}",
        grid_spec=pltpu.PrefetchScalarGridSpec(num_scalar_prefetch=1, grid=(1,), in_specs=in_specs + [_ANY] * len(deps),
                                               out_specs=out_specs),
        out_shape=[jax.ShapeDtypeStruct((NDEV,) + s, bf16) for s in shapes], compiler_params=_cp(48))(me, *arrs, *deps)


def slot_in(name, me, arrays):
    n = len(arrays)

    def body(me_ref, *refs):
        for k in range(n):
            refs[n + k][...] = refs[k][...]

    in_specs = [pl.BlockSpec(a.shape, lambda i, me_ref: (0, 0)) for a in arrays]
    out_specs = [pl.BlockSpec((None,) + a.shape, lambda i, me_ref: (me_ref[0], 0, 0)) for a in arrays]
    return pl.pallas_call(
        body, name=name,
        grid_spec=pltpu.PrefetchScalarGridSpec(num_scalar_prefetch=1, grid=(1,), in_specs=in_specs, out_specs=out_specs),
        out_shape=[jax.ShapeDtypeStruct((NDEV,) + a.shape, a.dtype) for a in arrays])(me, *arrays)


def _ff_chunks(ff, want=768):
    if ff % 256:
        return [slice(0, ff)]
    return [slice(c, min(c + want, ff)) for c in range(0, ff, want)]


def ffn_fwd(name, h, g, wgT, wuT, wd, deps=(), keep_s=False):
    tp, d = h.shape
    ff = wgT.shape[0]
    tm = _row_tile(tp, 704)

    def body(h_ref, g_ref, wg_ref, wu_ref, wd_ref, ho_ref, a_ref, b_ref, *s_ref):
        hh = h_ref[...]
        hhat, _ = _rms(hh)
        n = (hhat * g_ref[...]).astype(bf16)
        acc = None
        for cols in _ff_chunks(ff):
            a = _dot_nt(n, wg_ref[cols, :])
            b = _dot_nt(n, wu_ref[cols, :])
            s = ((a * _sigmoid(a)) * b).astype(bf16)
            part = _dot_nn(s, wd_ref[cols, :])
            acc = part if acc is None else acc + part
            a_ref[:, cols] = a.astype(bf16)
            b_ref[:, cols] = b.astype(bf16)
            if keep_s:
                s_ref[0][:, cols] = s
        ho_ref[...] = hh + 0.5 * acc

    row = lambda w: pl.BlockSpec((tm, w), lambda i: (i, 0))
    extra = keep_s * [1]
    return pl.pallas_call(
        _skip(len(deps), body), name=name, grid=(tp // tm,),
        in_specs=[_ANY] * len(deps) + [row(d), _resident((1, d)), _resident((ff, d)), _resident((ff, d)), _resident((ff, d))],
        out_specs=[row(d), row(ff), row(ff)] + [pl.BlockSpec((None, tm, ff), lambda i: (0, i, 0)) for _ in extra],
        out_shape=[jax.ShapeDtypeStruct((tp, d), f32), jax.ShapeDtypeStruct((tp, ff), bf16),
                   jax.ShapeDtypeStruct((tp, ff), bf16)] + [jax.ShapeDtypeStruct((1, tp, ff), bf16) for _ in extra],
        compiler_params=_cp(58, ("arbitrary",)))(*deps, h, g, wgT, wuT, wd)


def ffn_bwd(name, dy, h, g, a, b, wgT, wuT, wd, deps=()):
    tp, d = h.shape
    ff = wgT.shape[0]
    tm = _row_tile(tp, 384)

    def body(dy_ref, h_ref, g_ref, a_ref, b_ref, wg_ref, wu_ref, wd_ref, dh_ref, lhs_ref, rhs_ref, dg_ref):
        dyv = dy_ref[...]
        hhat, rs = _rms(h_ref[...])
        gv = g_ref[...]
        n = hhat * gv
        dyh = (0.5 * dyv).astype(bf16)
        dn = None
        for cols in _ff_chunks(ff):
            ds = _dot_nt(dyh, wd_ref[cols, :])
            av = a_ref[:, cols].astype(f32)
            bv = b_ref[:, cols].astype(f32)
            sg = _sigmoid(av)
            sa = av * sg
            da = (ds * bv * (sg * (1.0 + av * (1.0 - sg)))).astype(bf16)
            db = (ds * sa).astype(bf16)
            part = _dot_nn(da, wg_ref[cols, :]) + _dot_nn(db, wu_ref[cols, :])
            dn = part if dn is None else dn + part
            lhs_ref[0, :, cols] = da
            lhs_ref[1, :, cols] = db
            lhs_ref[2, :, cols] = (sa * bv).astype(bf16)
        dh_ref[...] = dyv + _rms_bwd(dn, gv, hhat, rs)

        @pl.when(pl.program_id(0) == 0)
        def _():
            dg_ref[...] = jnp.zeros_like(dg_ref)

        dg_ref[0:1, :] += jnp.sum(dn * hhat, axis=0, keepdims=True)
        rhs_ref[0] = n.astype(bf16)
        rhs_ref[1] = dyh

    row = lambda w: pl.BlockSpec((tm, w), lambda i: (i, 0))
    return pl.pallas_call(
        _skip(len(deps), body), name=name, grid=(tp // tm,),
        in_specs=[_ANY] * len(deps) + [row(d), row(d), _resident((1, d)), row(ff), row(ff),
                  _resident((ff, d)), _resident((ff, d)), _resident((ff, d))],
        out_specs=[row(d), pl.BlockSpec((3, tm, ff), lambda i: (0, i, 0)), pl.BlockSpec((2, tm, d), lambda i: (0, i, 0)),
                   pl.BlockSpec((8, d), lambda i: (0, 0))],
        out_shape=[jax.ShapeDtypeStruct((tp, d), f32), jax.ShapeDtypeStruct((3, tp, ff), bf16),
                   jax.ShapeDtypeStruct((2, tp, d), bf16), jax.ShapeDtypeStruct((8, d), f32)],
        compiler_params=_cp(58, ("arbitrary",)))(*deps, dy, h, g, a, b, wgT, wuT, wd)


def mm_tn(name, lhs, rhs, rhs_of, deps=(), only=None):
    _, tp, m = lhs.shape
    b0, nb = (0, lhs.shape[0]) if only is None else (only, 1)
    n = rhs.shape[2]
    def fits(t, ms):
        mb = m // ms
        return (tp % t == 0 and m % (128 * ms) == 0
                and 2 * t * (mb + n) * 2 + mb * n * (2 * 2 + 4 + (4 if t < tp else 0)) <= 54 * 2**20)

    tk, msplit = next(((t, ms) for t in (tp, 1408, 704, 384) for ms in (1, 2, 4) if fits(t, ms)), (128, 1))
    nk = tp // tk
    mb = m // msplit

    def body(l_ref, r_ref, o_ref, acc_ref):
        if nk == 1:
            o_ref[...] = _dot_tn(l_ref[...], r_ref[...]).astype(o_ref.dtype)
            return
        k = pl.program_id(2)

        @pl.when(k == 0)
        def _():
            acc_ref[...] = jnp.zeros_like(acc_ref)

        acc_ref[...] += _dot_tn(l_ref[...], r_ref[...])

        @pl.when(k == nk - 1)
        def _():
            o_ref[...] = acc_ref[...].astype(o_ref.dtype)

    return pl.pallas_call(
        _skip(len(deps), body), name=name, grid=(nb, msplit, nk),
        in_specs=[_ANY] * len(deps) + [pl.BlockSpec((None, tk, mb), lambda b, j, k: (b0 + b, k, j)),
                                       pl.BlockSpec((None, tk, n), lambda b, j, k: (rhs_of(b0 + b), k, 0))],
        out_specs=pl.BlockSpec((None, mb, n), lambda b, j, k: (b, j, 0)),
        out_shape=jax.ShapeDtypeStruct((nb, m, n), bf16),
        scratch_shapes=[pltpu.VMEM((mb, n) if nk > 1 else (8, 128), f32)],
        compiler_params=_cp(60, ("arbitrary", "arbitrary", "arbitrary")))(*deps, lhs, rhs)


def mix_in_fwd(name, h, g, winT, cos2, sin2):
    tp, d = h.shape
    nin = winT.shape[0]
    tm = _row_tile(tp)

    def body(h_ref, g_ref, w_ref, cos_ref, sin_ref, zq_ref, zg_ref, zu_ref, zgate_ref):
        hhat, _ = _rms(h_ref[...])
        z = _dot_nt((hhat * g_ref[...]).astype(bf16), w_ref[...])
        cosv, sinv = cos_ref[...], sin_ref[...]
        for hh in range(HEADS):
            qcols, kcols = slice(hh * HD, (hh + 1) * HD), slice(RW + hh * HD, RW + (hh + 1) * HD)
            zq_ref[:, qcols] = (_rot(z[:, qcols], cosv, sinv) * HD ** -0.5).astype(bf16)
            zq_ref[:, kcols] = _rot(z[:, kcols], cosv, sinv).astype(bf16)
        zq_ref[:, 2 * RW:] = z[:, 2 * RW:3 * RW].astype(bf16)
        zg_ref[...] = z[:, 3 * RW:4 * RW].astype(zg_ref.dtype)
        zu_ref[...] = z[:, 4 * RW:5 * RW]
        zgate_ref[...] = z[:, 5 * RW:].astype(zgate_ref.dtype)

    row = lambda w: pl.BlockSpec((tm, w), lambda i: (i, 0))
    widths = (3 * RW, RW, RW, 2 * d)
    return pl.pallas_call(
        body, name=name, grid=(tp // tm,),
        in_specs=[row(d), _resident((1, d)), _resident((nin, d)), row(HD), row(HD)],
        out_specs=[row(w) for w in widths],
        out_shape=[jax.ShapeDtypeStruct((tp, w), dt) for w, dt in zip(widths, (bf16, bf16, f32, bf16))],
        compiler_params=_cp(56, ("arbitrary",)))(h, g, winT, cos2, sin2)


def mix_in_bwd(name, dq, dk, dv, dzg, dzu, dzgate, h, g, winT, dres, keep_half=False):
    tp, d = h.shape
    nin = winT.shape[0]
    tm = _row_tile(tp)

    def body(dq_ref, dk_ref, dv_ref, dzg_ref, dzu_ref, dzgate_ref, h_ref, g_ref, w_ref, dres_ref, dh_ref, dz_ref, n_ref, dg_ref,
             *half_ref):
        dz = jnp.concatenate([dq_ref[...], dk_ref[...], dv_ref[...], dzg_ref[...], dzu_ref[...], dzgate_ref[...]], axis=-1)
        dn = _dot_nn(dz, w_ref[...])
        hhat, rs = _rms(h_ref[...])
        gv = g_ref[...]
        dh = dres_ref[...] + _rms_bwd(dn, gv, hhat, rs)
        dh_ref[...] = dh
        if keep_half:
            half_ref[0][...] = (0.5 * dh).astype(bf16)

        @pl.when(pl.program_id(0) == 0)
        def _():
            dg_ref[...] = jnp.zeros_like(dg_ref)

        dg_ref[0:1, :] += jnp.sum(dn * hhat, axis=0, keepdims=True)
        dz_ref[...] = dz
        n_ref[...] = (hhat * gv).astype(bf16)

    row = lambda w: pl.BlockSpec((tm, w), lambda i: (i, 0))
    extra = keep_half * [1]
    return pl.pallas_call(
        body, name=name, grid=(tp // tm,),
        in_specs=[row(RW)] * 5 + [row(2 * d), row(d), _resident((1, d)), _resident((nin, d)), row(d)],
        out_specs=[row(d), pl.BlockSpec((None, tm, nin), lambda i: (0, i, 0)), pl.BlockSpec((None, tm, d), lambda i: (0, i, 0)),
                   pl.BlockSpec((8, d), lambda i: (0, 0))] + [pl.BlockSpec((None, tm, d), lambda i: (0, i, 0)) for _ in extra],
        out_shape=[jax.ShapeDtypeStruct((tp, d), f32), jax.ShapeDtypeStruct((1, tp, nin), bf16),
                   jax.ShapeDtypeStruct((1, tp, d), bf16), jax.ShapeDtypeStruct((8, d), f32)]
        + [jax.ShapeDtypeStruct((1, tp, d), bf16) for _ in extra],
        compiler_params=_cp(56, ("arbitrary",)))(dq, dk, dv, dzg, dzu, dzgate, h, g, winT, dres)


def _retention_tables(tp, pad):
    half = HD // 2
    inv_freq = ROPE_BASE ** (-jnp.arange(half, dtype=f32) / half)
    pos = jnp.arange(tp, dtype=f32) - pad
    ang = pos[:, None] * inv_freq[None, :]
    cos, sin = jnp.cos(ang), jnp.sin(ang)
    cos2 = jnp.concatenate([cos, cos], axis=-1)
    sin2 = jnp.concatenate([-sin, sin], axis=-1)
    log_gamma = jnp.log1p(-(2.0 ** (-5.0 - jnp.arange(HEADS, dtype=f32))))
    idx = jnp.arange(CHUNK, dtype=f32)
    diff = idx[:, None] - idx[None, :]
    intra = jnp.where(diff[None] >= 0, jnp.exp(diff[None] * log_gamma[:, None, None]), 0.0)
    k_decay = jnp.exp((CHUNK - 1.0 - idx)[None, :] * log_gamma[:, None])
    q_decay = jnp.exp((idx + 1.0)[None, :] * log_gamma[:, None])
    chunk_decay = jnp.exp(CHUNK * log_gamma)
    full = (HEADS, CHUNK, HD)
    dec = jnp.stack([intra, jnp.broadcast_to(k_decay[:, :, None], full), jnp.broadcast_to(q_decay[:, :, None], full),
                     jnp.broadcast_to(chunk_decay[:, None, None], full)], axis=1)
    return cos2, sin2, dec


def _rot(t, cos2, sin2):
    return t * cos2 + pltpu.roll(t, HD // 2, 1) * sin2


def _rot_t(t, cos2, sin2):
    return t * cos2 - pltpu.roll(t, HD // 2, 1) * sin2


def _chunks_per_step(nch):
    return 3 if nch % 3 == 0 else 1


def retention_fwd(name, zq, dec):
    tp = zq.shape[0]
    nch = tp // CHUNK
    per = _chunks_per_step(nch)

    def body(q_ref, k_ref, v_ref, dec_ref, out_ref, st_ref, s_ref):
        @pl.when(pl.program_id(0) == 0)
        def _():
            s_ref[...] = jnp.zeros_like(s_ref)

        state = [s_ref[hh] for hh in range(HEADS)]
        for j in range(per):
            rows = slice(j * CHUNK, (j + 1) * CHUNK)
            for hh in range(HEADS):
                cols = slice(hh * HD, (hh + 1) * HD)
                qb, kb, vb = q_ref[rows, cols], k_ref[rows, cols], v_ref[rows, cols]
                sc = (_dot_nt(qb, kb) * dec_ref[hh, 0]).astype(bf16)
                sb = state[hh].astype(bf16)
                out_ref[rows, cols] = _dot_nn(sc, vb) + _dot_nn((qb.astype(f32) * dec_ref[hh, 2]).astype(bf16), sb)
                st_ref[hh, j] = sb
                state[hh] = state[hh] * dec_ref[hh, 3] + _dot_tn((kb.astype(f32) * dec_ref[hh, 1]).astype(bf16), vb)
        for hh in range(HEADS):
            s_ref[hh] = state[hh]

    part = lambda j: pl.BlockSpec((per * CHUNK, RW), lambda n: (n, j))
    return pl.pallas_call(
        body, name=name, grid=(nch // per,),
        in_specs=[part(0), part(1), part(2), _resident((HEADS, 4, CHUNK, HD))],
        out_specs=[part(0), pl.BlockSpec((HEADS, per, HD, HD), lambda n: (0, n, 0, 0))],
        out_shape=[jax.ShapeDtypeStruct((tp, RW), f32), jax.ShapeDtypeStruct((HEADS, nch, HD, HD), bf16)],
        scratch_shapes=[pltpu.VMEM((HEADS, HD, HD), f32)],
        compiler_params=_cp(32, ("arbitrary",)))(zq, zq, zq, dec)


def retention_bwd(name, zq, cos2, sin2, dec, states, dout, pad, deps=()):
    tp = zq.shape[0]
    nch = tp // CHUNK
    per = _chunks_per_step(nch)
    nblk = nch // per
    scale = HD ** -0.5

    def body(q_ref, k_ref, v_ref, cos_ref, sin_ref, dec_ref, st_ref, do_ref, dq_ref, dk_ref, dv_ref, g_ref):
        @pl.when(pl.program_id(0) == 0)
        def _():
            g_ref[...] = jnp.zeros_like(g_ref)

        first_row = (nblk - 1 - pl.program_id(0)) * (per * CHUNK)
        gstate = [g_ref[hh] for hh in range(HEADS)]
        for j in reversed(range(per)):
            rows = slice(j * CHUNK, (j + 1) * CHUNK)
            cosv, sinv = cos_ref[rows, :], sin_ref[rows, :]
            keep = (lax.broadcasted_iota(jnp.int32, (CHUNK, HD), 0) + (first_row + j * CHUNK)) >= pad
            for hh in range(HEADS):
                cols = slice(hh * HD, (hh + 1) * HD)
                intra, kdec, qdec = dec_ref[hh, 0], dec_ref[hh, 1], dec_ref[hh, 2]
                qb, kb, vb = q_ref[rows, cols], k_ref[rows, cols], v_ref[rows, cols]
                qd = (qb.astype(f32) * qdec).astype(bf16)
                kd = (kb.astype(f32) * kdec).astype(bf16)
                sc = (_dot_nt(qb, kb) * intra).astype(bf16)
                dob = do_ref[rows, cols]
                sb = st_ref[hh, j]
                gb = gstate[hh].astype(bf16)
                dsc = (_dot_nt(dob, vb) * intra).astype(bf16)
                dv = _dot_tn(sc, dob) + _dot_nn(kd, gb)
                dqr = _dot_nn(dsc, kb) + _dot_nt(dob, sb) * qdec
                dkr = _dot_tn(dsc, qb) + _dot_nt(vb, gb) * kdec
                gstate[hh] = gstate[hh] * dec_ref[hh, 3] + _dot_tn(qd, dob)
                dq_ref[rows, cols] = jnp.where(keep, _rot_t(dqr * scale, cosv, sinv), 0.0).astype(bf16)
                dk_ref[rows, cols] = jnp.where(keep, _rot_t(dkr, cosv, sinv), 0.0).astype(bf16)
                dv_ref[rows, cols] = jnp.where(keep, dv, 0.0).astype(bf16)
        for hh in range(HEADS):
            g_ref[hh] = gstate[hh]

    part = lambda j: pl.BlockSpec((per * CHUNK, RW), lambda t: (nblk - 1 - t, j))
    table = pl.BlockSpec((per * CHUNK, HD), lambda t: (nblk - 1 - t, 0))
    return pl.pallas_call(
        _skip(len(deps), body), name=name, grid=(nblk,),
        in_specs=[_ANY] * len(deps) + [part(0), part(1), part(2), table, table, _resident((HEADS, 4, CHUNK, HD)),
                                       pl.BlockSpec((HEADS, per, HD, HD), lambda t: (0, nblk - 1 - t, 0, 0)), part(0)],
        out_specs=[part(0)] * 3,
        out_shape=[jax.ShapeDtypeStruct((tp, RW), bf16)] * 3,
        scratch_shapes=[pltpu.VMEM((HEADS, HD, HD), f32)],
        compiler_params=_cp(32, ("arbitrary",)))(*deps, zq, zq, zq, cos2, sin2, dec, states, dout)


def _window_sum(xv, steps, tp, forward):
    s = xv
    for j in range(steps):
        sh = 2 ** j
        s = s + pltpu.roll(s, (tp - sh) if forward else sh, 0)
    return s


def pool_fwd(name, zu, maps, scale, pad):
    tp = zu.shape[0]

    def body(u_ref, maps_ref, scale_ref, pooled_ref, p_ref):
        row = lax.broadcasted_iota(jnp.int32, (tp, HD), 0)
        for gi, w in enumerate(POOL_WINDOWS):
            cols = slice(gi * HD, (gi + 1) * HD)
            xv = u_ref[:, cols]
            cnt = jnp.clip(row - (pad - 1), 1, w).astype(f32)
            pooled = jnp.where(row >= pad, _window_sum(xv, gi + 1, tp, False) / cnt - xv, 0.0).astype(bf16)
            pooled_ref[:, cols] = pooled
            p_ref[:, cols] = (_dot_nn(pooled, maps_ref[gi].astype(bf16)) * scale_ref[:, cols]).astype(bf16)

    return pl.pallas_call(
        body, name=name,
        out_shape=[jax.ShapeDtypeStruct((tp, RW), bf16), jax.ShapeDtypeStruct((tp, RW), bf16)],
        compiler_params=_cp(56))(zu, maps, scale)


def pool_bwd(name, dp, pooled, maps, scale, pad):
    tp = dp.shape[0]

    def body(dp_ref, pooled_ref, maps_ref, scale_ref, du_ref, dmaps_ref, dscale_ref):
        row = lax.broadcasted_iota(jnp.int32, (tp, HD), 0)
        dscale_ref[...] = jnp.zeros_like(dscale_ref)
        for gi, w in enumerate(POOL_WINDOWS):
            cols = slice(gi * HD, (gi + 1) * HD)
            mb = maps_ref[gi].astype(bf16)
            pooled = pooled_ref[:, cols]
            dpf = dp_ref[:, cols].astype(f32)
            dscale_ref[0:1, cols] = jnp.sum(dpf * _dot_nn(pooled, mb), axis=0, keepdims=True)
            dpm = (dpf * scale_ref[:, cols]).astype(bf16)
            dmaps_ref[gi * HD:(gi + 1) * HD, :] = _dot_tn(pooled, dpm)
            dpool = jnp.where(row >= pad, _dot_nt(dpm, mb), 0.0)
            cnt = jnp.clip(row - (pad - 1), 1, w).astype(f32)
            du = _window_sum(dpool / cnt, gi + 1, tp, True) - dpool
            du_ref[:, cols] = jnp.where(row >= pad, du, 0.0).astype(bf16)

    return pl.pallas_call(
        body, name=name,
        out_shape=[jax.ShapeDtypeStruct((tp, RW), bf16), jax.ShapeDtypeStruct((HEADS * HD, HD), f32),
                   jax.ShapeDtypeStruct((8, RW), f32)],
        compiler_params=_cp(56))(dp, pooled, maps, scale)


def _group_norm(o):
    mu = jnp.mean(o, axis=-1, keepdims=True)
    oc = o - mu
    rstd = lax.rsqrt(jnp.mean(oc * oc, axis=-1, keepdims=True) + EPS)
    return oc * rstd, rstd


def mix_out_fwd(name, h, oraw, zg, zgate, p, wretT, wpoolT, wout, deps=()):
    tp, d = h.shape
    tm = _row_tile(tp)

    def body(h_ref, o_ref, zg_ref, zgate_ref, p_ref, wr_ref, wp_ref, wo_ref, ho_ref, rp_ref, ret_ref, pool_ref, mixed_ref):
        parts = []
        for hh in range(HEADS):
            cols = slice(hh * HD, (hh + 1) * HD)
            rhat, _ = _group_norm(o_ref[:, cols])
            gv = zg_ref[:, cols].astype(f32)
            parts.append(rhat * (gv * _sigmoid(gv)))
        r = jnp.concatenate(parts, axis=-1).astype(bf16)
        pv = p_ref[...]
        ret = _dot_nt(r, wr_ref[...])
        pool = _dot_nt(pv, wp_ref[...])
        mixed = (_sigmoid(zgate_ref[:, :d].astype(f32)) * ret + _sigmoid(zgate_ref[:, d:].astype(f32)) * pool).astype(bf16)
        ho_ref[...] = h_ref[...] + _dot_nn(mixed, wo_ref[...])
        rp_ref[0] = r
        rp_ref[1] = pv
        ret_ref[...] = ret.astype(bf16)
        pool_ref[...] = pool.astype(bf16)
        mixed_ref[...] = mixed

    row = lambda w: pl.BlockSpec((tm, w), lambda i: (i, 0))
    return pl.pallas_call(
        _skip(len(deps), body), name=name, grid=(tp // tm,),
        in_specs=[_ANY] * len(deps) + [row(d), row(RW), row(RW), row(2 * d), row(RW), _resident((d, RW)), _resident((d, RW)),
                                       _resident((d, d))],
        out_specs=[row(d), pl.BlockSpec((2, tm, RW), lambda i: (0, i, 0)), row(d), row(d),
                   pl.BlockSpec((None, tm, d), lambda i: (0, i, 0))],
        out_shape=[jax.ShapeDtypeStruct((tp, d), f32), jax.ShapeDtypeStruct((2, tp, RW), bf16),
                   jax.ShapeDtypeStruct((tp, d), bf16), jax.ShapeDtypeStruct((tp, d), bf16),
                   jax.ShapeDtypeStruct((1, tp, d), bf16)],
        compiler_params=_cp(48, ("arbitrary",)))(*deps, h, oraw, zg, zgate, p, wretT, wpoolT, wout)


def mix_out_bwd(name, dy, oraw, zg, zgate, ret, pool, wretT, wpoolT, wout, deps=()):
    tp, d = dy.shape
    tm = _row_tile(tp)

    def body(dy_ref, o_ref, zg_ref, zgate_ref, ret_ref, pool_ref, wr_ref, wp_ref, wo_ref,
             do_ref, dzg_ref, dzgate_ref, dp_ref, drp_ref, dyb_ref):
        dyb = dy_ref[...].astype(bf16)
        dmixed = _dot_nt(dyb, wo_ref[...])
        sa = _sigmoid(zgate_ref[:, :d].astype(f32))
        sb = _sigmoid(zgate_ref[:, d:].astype(f32))
        dret = dmixed * sa
        dpool = dmixed * sb
        dzgate_ref[:, :d] = (dret * ret_ref[...].astype(f32) * (1.0 - sa)).astype(bf16)
        dzgate_ref[:, d:] = (dpool * pool_ref[...].astype(f32) * (1.0 - sb)).astype(bf16)
        dretb, dpoolb = dret.astype(bf16), dpool.astype(bf16)
        dr = _dot_nn(dretb, wr_ref[...])
        dp_ref[...] = _dot_nn(dpoolb, wp_ref[...]).astype(bf16)
        for hh in range(HEADS):
            cols = slice(hh * HD, (hh + 1) * HD)
            rhat, rstd = _group_norm(o_ref[:, cols])
            gv = zg_ref[:, cols].astype(f32)
            sg = _sigmoid(gv)
            drh = dr[:, cols]
            drhat = drh * (gv * sg)
            dzg_ref[:, cols] = (drh * rhat * (sg * (1.0 + gv * (1.0 - sg)))).astype(bf16)
            do = rstd * (drhat - jnp.mean(drhat, axis=-1, keepdims=True)
                         - rhat * jnp.mean(drhat * rhat, axis=-1, keepdims=True))
            do_ref[:, cols] = do.astype(bf16)
        drp_ref[0] = dretb
        drp_ref[1] = dpoolb
        dyb_ref[...] = dyb

    row = lambda w: pl.BlockSpec((tm, w), lambda i: (i, 0))
    return pl.pallas_call(
        _skip(len(deps), body), name=name, grid=(tp // tm,),
        in_specs=[_ANY] * len(deps) + [row(d), row(RW), row(RW), row(2 * d), row(d), row(d), _resident((d, RW)), _resident((d, RW)),
                  _resident((d, d))],
        out_specs=[row(RW), row(RW), row(2 * d), row(RW), pl.BlockSpec((2, tm, d), lambda i: (0, i, 0)),
                   pl.BlockSpec((None, tm, d), lambda i: (0, i, 0))],
        out_shape=[jax.ShapeDtypeStruct((tp, RW), bf16), jax.ShapeDtypeStruct((tp, RW), bf16),
                   jax.ShapeDtypeStruct((tp, 2 * d), bf16), jax.ShapeDtypeStruct((tp, RW), bf16),
                   jax.ShapeDtypeStruct((2, tp, d), bf16), jax.ShapeDtypeStruct((1, tp, d), bf16)],
        compiler_params=_cp(48, ("arbitrary",)))(*deps, dy, oraw, zg, zgate, ret, pool, wretT, wpoolT, wout)


def final_loss(name, h, g, target):
    tp, d = h.shape
    tm = _row_tile(tp)
    nsub = tm // CHUNK

    def body(h_ref, g_ref, *rest):
        t_refs = rest[:nsub]
        dh_ref, loss_ref, dg_ref = rest[nsub:]
        i = pl.program_id(0)

        @pl.when(i == 0)
        def _():
            loss_ref[...] = jnp.zeros_like(loss_ref)
            dg_ref[...] = jnp.zeros_like(dg_ref)

        gv = g_ref[...]
        for j in range(nsub):
            rows = slice(j * CHUNK, (j + 1) * CHUNK)
            hhat, rs = _rms(h_ref[rows, :])
            err = jnp.where(i * nsub + j >= 1, hhat * gv - t_refs[j][...], 0.0)
            dyv = err / d
            dh_ref[rows, :] = _rms_bwd(dyv, gv, hhat, rs)
            loss_ref[...] += 0.5 * jnp.sum(jnp.sum(err * err, axis=-1, keepdims=True) / d)
            dg_ref[0:1, :] += jnp.sum(dyv * hhat, axis=0, keepdims=True)

    lagged = lambda j: pl.BlockSpec((CHUNK, d), lambda i: (jnp.maximum(i * nsub + j - 1, 0), 0))
    return pl.pallas_call(
        body, name=name, grid=(tp // tm,),
        in_specs=[pl.BlockSpec((tm, d), lambda i: (i, 0)), _resident((1, d))] + [lagged(j) for j in range(nsub)],
        out_specs=[pl.BlockSpec((tm, d), lambda i: (i, 0)), pl.BlockSpec((8, 128), lambda i: (0, 0)),
                   pl.BlockSpec((8, d), lambda i: (0, 0))],
        out_shape=[jax.ShapeDtypeStruct((tp, d), f32), jax.ShapeDtypeStruct((8, 128), f32),
                   jax.ShapeDtypeStruct((8, d), f32)],
        compiler_params=_cp(32, ("arbitrary",)))(h, g, *[target] * nsub)


def _adamw(w, g, m, v):
    m = ADAM_B1 * m + (1.0 - ADAM_B1) * g
    v = ADAM_B2 * v + (1.0 - ADAM_B2) * (g * g)
    m_hat = m / (1.0 - ADAM_B1 ** ADAM_STEP)
    v_hat = v / (1.0 - ADAM_B2 ** ADAM_STEP)
    delta = -ADAM_LR * (m_hat / (jnp.sqrt(v_hat) + ADAM_EPS) + ADAM_WD * w)
    return delta, m, v


def adam_big(name, me, recv, own, b, layer, transposed, w, m, v, prev):
    r, c = recv.shape[2:]
    wshape = w.shape[1:]
    nchunk = 1 if transposed else next(k for k in (4, 2, 1) if r % (16 * k) == 0)
    rc = r // nchunk

    def body(me_ref, recv_ref, own_ref, w_ref, m_ref, v_ref, *rest):
        g_ref, d_ref, nm_ref, nv_ref = rest[-4:]
        g = own_ref[...].astype(f32)
        for j in range(NDEV - 1):
            g = g + recv_ref[j].astype(f32)
        if transposed:
            g = g.T
        delta, nm, nv = _adamw(w_ref[...], g, m_ref[...], v_ref[...])
        g_ref[...] = g
        d_ref[...] = delta
        nm_ref[...] = nm
        nv_ref[...] = nv

    wblock = wshape if transposed else (rc, c)
    wspec = pl.BlockSpec((None,) + wblock, lambda i, me_ref: (layer, i, 0))
    in_specs = [pl.BlockSpec((NDEV - 1, None, rc, c), lambda i, me_ref: (0, b, i, 0)),
                pl.BlockSpec((None, None, rc, c), lambda i, me_ref: (b, me_ref[0], i, 0)), wspec, wspec, wspec]
    args = [recv, own, w, m, v]
    aliases = {}
    if prev is not None:
        in_specs += [_ANY] * 4
        args += list(prev)
        aliases = {6 + k: k for k in range(4)}
    return pl.pallas_call(
        body, name=name,
        grid_spec=pltpu.PrefetchScalarGridSpec(num_scalar_prefetch=1, grid=(nchunk,), in_specs=in_specs,
                                               out_specs=[wspec] * 4),
        out_shape=[jax.ShapeDtypeStruct(w.shape, f32)] * 4, input_output_aliases=aliases,
        compiler_params=_cp(56))(me, *args)


def adam_small(name, ga0, gmaps0, gmeta, ga1, gmaps1, norms, pool_scale, pool_maps, meta, final_norm, d):
    def body(ga0_ref, gmaps0_ref, gmeta_ref, ga1_ref, gmaps1_ref, *refs):
        ins, outs = refs[:21], refs[21:]
        x, y, c = _me()
        me = 4 * x + 2 * y + c

        def total(ref, rows):
            t = ref[0, rows, :]
            for j in range(1, NDEV):
                t = t + ref[j, rows, :]
            return t

        row = lambda r: slice(r, r + 1)
        outs[0][...] = jnp.broadcast_to(total(ga1_ref, row(0))[:, :128], (8, 128))

        def update(k, g, o):
            w_ref, m_ref, v_ref = ins[3 * k:3 * k + 3]
            delta, nm, nv = _adamw(w_ref[...], g, m_ref[...], v_ref[...])
            for ref, val in zip(outs[o:o + 4], (g, delta, nm, nv)):
                ref[...] = val

        two = lax.broadcasted_iota(jnp.int32, (2, d), 0)
        for k in range(3):
            update(k, jnp.where(two == 0, total(ga0_ref, row(k)), total(ga1_ref, row(2 + k))), 1 + 4 * k)
        update(3, jnp.where(two[:, :RW] == 0, total(ga0_ref, row(3))[:, :RW], total(ga1_ref, row(5))[:, :RW]), 13)
        update(4, jnp.concatenate([total(gmaps0_ref, slice(None)), total(gmaps1_ref, slice(None))], axis=0), 17)
        update(5, total(gmeta_ref, pl.ds(pl.multiple_of(me * N_META, N_META), N_META)), 21)
        update(6, total(ga1_ref, row(1)), 25)

    flat = []
    for trip in (*norms, pool_scale, pool_maps, meta, final_norm):
        flat += list(trip)
    out_shapes = [jax.ShapeDtypeStruct((8, 128), f32)]
    for trip in (*norms, pool_scale, pool_maps, meta, final_norm):
        out_shapes += [jax.ShapeDtypeStruct(trip[0].shape, f32)] * 4
    return pl.pallas_call(body, name=name, out_shape=out_shapes,
                          compiler_params=_cp(32))(ga0, gmaps0, gmeta, ga1, gmaps1, *flat)


def kernel(x, meta, ffn1_norm, ffn1_gate, ffn1_up, ffn1_down, mix_norm, w_in, pool_maps, pool_scale, w_ret_up, w_pool_up, w_out, ffn2_norm, ffn2_gate, ffn2_up, ffn2_down, final_norm, loss_target, m_meta, m_ffn1_norm, m_ffn1_gate, m_ffn1_up, m_ffn1_down, m_mix_norm, m_w_in, m_pool_maps, m_pool_scale, m_w_ret_up, m_w_pool_up, m_w_out, m_ffn2_norm, m_ffn2_gate, m_ffn2_up, m_ffn2_down, m_final_norm, v_meta, v_ffn1_norm, v_ffn1_gate, v_ffn1_up, v_ffn1_down, v_mix_norm, v_w_in, v_pool_maps, v_pool_scale, v_w_ret_up, v_w_pool_up, v_w_out, v_ffn2_norm, v_ffn2_gate, v_ffn2_up, v_ffn2_down, v_final_norm):
    seq, d = x.shape[1], x.shape[2]
    depth = ffn1_gate.shape[0]
    ff = ffn1_gate.shape[2] * NDEV
    nin = w_in.shape[2] * NDEV
    length = seq + N_META
    pad = (-length) % CHUNK
    tp = length + pad
    assert pad % 8 == 0 and pad + N_META == CHUNK and depth == 2 and nin == 5 * RW + 2 * d

    ix, iy, ic = _me()
    me = (4 * ix + 2 * iy + ic).astype(jnp.int32).reshape(1)

    meta_all, = all_gather("gather_meta", [meta])
    meta_full = jnp.transpose(meta_all, (1, 0, 2)).reshape(N_META, d)

    gathers = {}
    token = meta_all
    tview = lambda *arrs: [jnp.swapaxes(a, 1, 2) for a in arrs]
    t_g1, t_u1, t_g2, t_u2, t_in = (tview(w, m, v) for w, m, v in (
        (ffn1_gate, m_ffn1_gate, v_ffn1_gate), (ffn1_up, m_ffn1_up, v_ffn1_up), (ffn2_gate, m_ffn2_gate, v_ffn2_gate),
        (ffn2_up, m_ffn2_up, v_ffn2_up), (w_in, m_w_in, v_w_in)))
    keys, groups = [], []
    for layer in range(depth):
        lands = prep_layer(layer, me, [w_ret_up, w_pool_up],
                           [t_g1[0], t_u1[0], t_g2[0], t_u2[0], t_in[0], ffn1_down, ffn2_down, w_out],
                           (token,) if layer else ())
        wretT, wpoolT, g1T, u1T, g2T, u2T, winT, d1, d2, wout = lands
        keys += [("ffn1", layer), ("mix", layer), ("ffn2", layer)]
        groups += [[g1T, u1T, d1], [winT, wretT, wpoolT, wout], [g2T, u2T, d2]]
        if layer == 0:
            first, token = gather_start_chips("gather_start_first", groups[:1], (token,))
    started, token = gather_start_chips("gather_start_rest", groups[1:], (token,))
    gathers = dict(zip(keys, first + started))

    def forward(part, layer, after):
        ssem, rsem, group = gathers[(part, layer)]
        ssem, rsem, group, tok = gather_forward(f"gather_forward_{part}{layer}", ssem, rsem, group, after)
        gathers[(part, layer)] = (ssem, rsem, group)
        return tok

    def gathered(part, layer, after):
        ssem, rsem, group = gathers[(part, layer)]
        _, full = copies_wait(f"gather_wait_{part}{layer}", ssem, rsem, (), group, after, 3)
        return [_in_hbm(a.reshape((NDEV * a.shape[1],) + a.shape[2:])) for a in full]

    cos2, sin2, dec = (_in_hbm(a) for a in _retention_tables(tp, pad))
    h = jnp.concatenate([jnp.zeros((pad, d), f32), meta_full + token[0, 0], x[0]], axis=0)

    saved = []
    weights = []
    tok = forward("ffn1", 0, token)
    for layer in range(depth):
        row = lambda a: a[layer:layer + 1]
        s = {"h0": h}
        g1T, u1T, d1 = gathered("ffn1", layer, tok if layer == 0 else h)
        tok = forward("mix", layer, h) if layer else None
        h, s["a1"], s["b1"], *s["s1"] = ffn_fwd(f"ffn1_fwd{layer}", h, row(ffn1_norm), g1T, u1T, d1, (tok,) if layer else (),
                                                keep_s=layer == 0)
        s["h1"] = h
        if layer == 0:
            tok = forward("mix", layer, h)
        winT, wretT, wpoolT, wout = gathered("mix", layer, tok if layer == 0 else h)
        s["zq"], s["zg"], zu, s["zgate"] = mix_in_fwd(f"mix_in_fwd{layer}", h, row(mix_norm), winT, cos2, sin2)
        s["oraw"], s["states"] = retention_fwd(f"retention_fwd{layer}", s["zq"], dec)
        s["pooled"], p = pool_fwd(f"pool_fwd{layer}", zu, pool_maps[layer], row(pool_scale), pad)
        tok = forward("ffn2", layer, p)
        h, s["rp"], s["ret"], s["pool"], s["mixed"] = mix_out_fwd(
            f"mix_out_fwd{layer}", h, s["oraw"], s["zg"], s["zgate"], p, wretT, wpoolT, wout, (tok,))
        s["h2"] = h
        g2T, u2T, d2 = gathered("ffn2", layer, h)
        tok = (forward("ffn1", layer + 1, h),) if layer + 1 < depth else ()
        h, s["a2"], s["b2"] = ffn_fwd(f"ffn2_fwd{layer}", h, row(ffn2_norm), g2T, u2T, d2, tok)
        saved.append(s)
        weights.append((g1T, u1T, g2T, u2T, winT, wretT, wpoolT, d1, d2, wout))

    dh, loss_part, dg_final = final_loss("final_loss", h, final_norm.reshape(1, d), loss_target[0])

    small = {}
    small_gathers = {}
    exchanges = {}
    token = None

    def rows8(vals):
        at = lax.broadcasted_iota(jnp.int32, (8, d), 0)
        out = jnp.zeros((8, d), f32)
        for k, v in enumerate(vals):
            r0 = v[0:1]
            r0 = r0 if r0.shape[1] == d else jnp.pad(r0, ((0, 0), (0, d - r0.shape[1])))
            out = jnp.where(at == k, r0, out)
        return out

    def exchange(part, layer, grads):
        by_dest = [g.reshape(g.shape[0], NDEV, g.shape[1] // NDEV, g.shape[2]) for g in grads]
        ssem, rsem, sent, lands, tok = exchange_start(f"exchange_start_{part}{layer}", by_dest)
        exchanges[(part, layer)] = (ssem, rsem, sent, lands)
        return (tok,)

    for layer in reversed(range(depth)):
        g1T, u1T, g2T, u2T, winT, wretT, wpoolT, d1, d2, wout = weights[layer]
        row = lambda a: a[layer:layer + 1]
        s = saved[layer]
        dh, lhs2, rhs2, small[("ffn2", layer)] = ffn_bwd(
            f"ffn2_bwd{layer}", dh, s["h2"], row(ffn2_norm), s["a2"], s["b2"], g2T, u2T, d2, () if token is None else token)
        token = exchange("ffn2", layer, [mm_tn(f"ffn2_wgrad{layer}", lhs2, rhs2, lambda b: b // 2)])
        do, dzg, dzgate, dp, drp, dyb = mix_out_bwd(
            f"mix_out_bwd{layer}", dh, s["oraw"], s["zg"], s["zgate"], s["ret"], s["pool"], wretT, wpoolT, wout, token)
        gw_mix = [mm_tn(f"w_out_wgrad{layer}", s["mixed"], dyb, lambda b: b),
                  mm_tn(f"up_wgrad{layer}", drp, s["rp"], lambda b: b)]
        dq, dk, dv = retention_bwd(f"retention_bwd{layer}", s["zq"], cos2, sin2, dec, s["states"], do, pad, token)
        dzu, small[("maps", layer)], small[("scale", layer)] = pool_bwd(
            f"pool_bwd{layer}", dp, s["pooled"], pool_maps[layer], row(pool_scale), pad)
        dh, dz, n2, small[("mix", layer)], *half = mix_in_bwd(
            f"mix_in_bwd{layer}", dq, dk, dv, dzg, dzu, dzgate, s["h1"], row(mix_norm), winT, dh, keep_half=layer == 0)
        token = exchange("mix", layer, gw_mix + [mm_tn(f"w_in_wgrad{layer}", dz, n2, lambda b: b)])
        if layer == 0:
            token = exchange("ffn1_down", layer, [mm_tn("ffn1_down_wgrad0", s["s1"][0], half[0], lambda b: b, token)])
        dh, lhs1, rhs1, small[("ffn1", layer)] = ffn_bwd(
            f"ffn1_bwd{layer}", dh, s["h0"], row(ffn1_norm), s["a1"], s["b1"], g1T, u1T, d1, token)
        rows = [small[("ffn1", layer)], small[("mix", layer)], small[("ffn2", layer)], small[("scale", layer)]]
        packs = [rows8([loss_part, dg_final] + rows if layer == depth - 1 else rows), small[("maps", layer)]]
        if layer == 0:
            dmeta = dh[pad:CHUNK]
            packs.append(jnp.transpose(dmeta.reshape(N_META, NDEV, d // NDEV), (1, 0, 2)).reshape(NDEV * N_META, d // NDEV))
        ssem, rsem, lands, tok = gather_start(f"small_start{layer}", slot_in(f"small_slot{layer}", me, packs))
        small_gathers[layer] = (ssem, rsem, lands)
        if layer:
            token = exchange("ffn1", layer, [mm_tn(f"ffn1_wgrad{layer}", lhs1, rhs1, lambda b: b // 2, (tok,))])
        else:
            token = (tok,)
            for j, nm in enumerate(("ffn1_gate", "ffn1_up")):
                token = exchange(nm, layer, [mm_tn(f"{nm}_wgrad{layer}", lhs1, rhs1, lambda b: b // 2, token, only=j)])

    grad_x = dh[CHUNK:][None]

    big = {}
    after = token[0]
    plans = {
        "ffn2": [("ffn2_gate", 0, 0, False, *t_g2), ("ffn2_up", 0, 1, False, *t_u2),
                 ("ffn2_down", 0, 2, False, ffn2_down, m_ffn2_down, v_ffn2_down)],
        "mix": [("w_out", 0, 0, False, w_out, m_w_out, v_w_out),
                ("w_ret_up", 1, 0, True, w_ret_up, m_w_ret_up, v_w_ret_up),
                ("w_pool_up", 1, 1, True, w_pool_up, m_w_pool_up, v_w_pool_up), ("w_in", 2, 0, False, *t_in)],
        "ffn1": [("ffn1_gate", 0, 0, False, *t_g1), ("ffn1_up", 0, 1, False, *t_u1),
                 ("ffn1_down", 0, 2, False, ffn1_down, m_ffn1_down, v_ffn1_down)]}
    for nm, k, b, tr, w, m, v in plans["ffn1"]:
        plans[nm] = [(nm, 0, 0, tr, w, m, v)]
    for layer in reversed(range(depth)):
        for part in ("ffn2", "mix") + (("ffn1",) if layer else ("ffn1_down", "ffn1_gate", "ffn1_up")):
            ssem, rsem, sent, lands = exchanges[(part, layer)]
            sent, lands = copies_wait(f"exchange_wait_{part}{layer}", ssem, rsem, sent, lands, after)
            for nm, k, b, tr, w, m, v in plans[part]:
                big[nm] = adam_big(f"adam_{nm}{layer}", me, lands[k], sent[k], b, layer, tr, w, m, v, big.get(nm))
                after = big[nm][0]

    gsmall = []
    for layer in range(depth):
        ssem, rsem, lands = small_gathers[layer]
        gsmall += copies_wait(f"small_wait{layer}", ssem, rsem, (), lands, after)[1]

    maps2 = lambda a: a.reshape(depth * HEADS * HD, HD)
    res = adam_small(
        "adam_small", *gsmall,
        [(ffn1_norm, m_ffn1_norm, v_ffn1_norm), (mix_norm, m_mix_norm, v_mix_norm), (ffn2_norm, m_ffn2_norm, v_ffn2_norm)],
        (pool_scale, m_pool_scale, v_pool_scale), (maps2(pool_maps), maps2(m_pool_maps), maps2(v_pool_maps)),
        (meta, m_meta, v_meta), tuple(a.reshape(1, d) for a in (final_norm, m_final_norm, v_final_norm)), d)
    loss = res[0][0, 0]
    sm = {}
    for k, nm in enumerate(["ffn1_norm", "mix_norm", "ffn2_norm", "pool_scale", "pool_maps", "meta", "final_norm"]):
        sm[nm] = list(res[1 + 4 * k:5 + 4 * k])
    sm["pool_maps"] = [a.reshape(pool_maps.shape) for a in sm["pool_maps"]]
    sm["final_norm"] = [a.reshape(d) for a in sm["final_norm"]]

    names = ["meta", "ffn1_norm", "ffn1_gate", "ffn1_up", "ffn1_down", "mix_norm", "w_in", "pool_maps", "pool_scale",
             "w_ret_up", "w_pool_up", "w_out", "ffn2_norm", "ffn2_gate", "ffn2_up", "ffn2_down", "final_norm"]
    for nm in ("ffn1_gate", "ffn1_up", "ffn2_gate", "ffn2_up", "w_in"):
        big[nm] = tview(*big[nm])
    allw = {**{k: list(v) for k, v in big.items()}, **sm}
    outs = [loss, grad_x]
    for kind in range(4):
        outs += [allw[nm][kind] for nm in names]
    return tuple(outs)
```

```python
import functools

import jax
import jax.numpy as jnp
from jax import lax
from jax.experimental import pallas as pl
from jax.experimental.pallas import tpu as pltpu

f32 = jnp.float32
bf16 = jnp.bfloat16
MESH = pl.DeviceIdType.MESH
NDEV = 8
N_META = 16
HEADS = 4
HD = 128
CHUNK = 128
RW = HEADS * HD
POOL_WINDOWS = (2, 4, 8, 16)
ROPE_BASE = 10000.0
EPS = 1e-6
ADAM_LR = 0.001
ADAM_B1 = 0.9
ADAM_B2 = 0.999
ADAM_EPS = 1e-08
ADAM_WD = 0.01
ADAM_STEP = 10
VMEM_CAP_MB = 60


def _cp(vmem_mb, sem=None):
    return pltpu.CompilerParams(vmem_limit_bytes=min(vmem_mb, VMEM_CAP_MB) * 2**20, dimension_semantics=sem)


def _row_tile(tp, want=384):
    return want if tp % want == 0 else 128


def _resident(shape):
    nd = len(shape)
    return pl.BlockSpec(shape, lambda *_: (0,) * nd, pipeline_mode=pl.Buffered(1))


def _skip(nd, body):
    return (lambda *refs: body(*refs[nd:])) if nd else body


def _dot_nn(a, b):
    return lax.dot_general(a, b, (((1,), (0,)), ((), ())), preferred_element_type=f32)


def _dot_nt(a, b):
    return lax.dot_general(a, b, (((1,), (1,)), ((), ())), preferred_element_type=f32)


def _dot_tn(a, b):
    return lax.dot_general(a, b, (((0,), (0,)), ((), ())), preferred_element_type=f32)


def _rms(h):
    rs = lax.rsqrt(jnp.mean(h * h, axis=-1, keepdims=True) + EPS)
    return h * rs, rs


def _rms_bwd(dn, g, hhat, rs):
    dhh = dn * g
    return rs * (dhh - hhat * jnp.mean(dhh * hhat, axis=-1, keepdims=True))


def _sigmoid(x):
    return jax.nn.sigmoid(x)


def _me():
    return lax.axis_index("x"), lax.axis_index("y"), lax.axis_index("c")


def _peer(idx):
    return (idx // 4, (idx // 2) % 2, idx % 2)


def all_gather(name, arrays, deps=()):
    n = len(arrays)

    def body(*refs):
        refs = refs[len(deps):]
        ins, outs = refs[:n], refs[n:2 * n]
        send_sems, recv_sems, local_sems = refs[2 * n:]
        x, y, c = _me()
        me = 4 * x + 2 * y + c
        locals_ = []
        for k in range(n):
            cp = pltpu.make_async_copy(ins[k], outs[k].at[me], local_sems.at[k])
            cp.start()
            locals_.append(cp)
        for d in range(1, NDEV):
            for k in range(n):
                pltpu.make_async_remote_copy(
                    src_ref=ins[k], dst_ref=outs[k].at[me], send_sem=send_sems.at[k], recv_sem=recv_sems.at[k],
                    device_id=_peer((me + d) % NDEV), device_id_type=MESH).start()
        for k in range(n):
            seven = outs[k].at[pl.ds(0, NDEV - 1)]
            w = pltpu.make_async_remote_copy(src_ref=seven, dst_ref=seven, send_sem=send_sems.at[k],
                                             recv_sem=recv_sems.at[k], device_id=(x, y, c), device_id_type=MESH)
            w.wait_send()
            w.wait_recv()
            locals_[k].wait()

    anyspec = pl.BlockSpec(memory_space=pl.ANY)
    return pl.pallas_call(
        body, name=name,
        out_shape=[jax.ShapeDtypeStruct((NDEV,) + a.shape, a.dtype) for a in arrays],
        in_specs=[anyspec] * (len(deps) + n), out_specs=[anyspec] * n,
        scratch_shapes=[pltpu.SemaphoreType.DMA((n,)), pltpu.SemaphoreType.DMA((n,)), pltpu.SemaphoreType.DMA((n,))],
    )(*deps, *arrays)


_HBM =pl.BlockSpec(memory_space=pltpu.HBM)
_SEM = pl.BlockSpec(memory_space=pltpu.SEMAPHORE)
_ANY = pl.BlockSpec(memory_space=pl.ANY)
_EFFECT = pltpu.SideEffectType.DATAFLOW_SIDE_EFFECTING


def _in_hbm(a):
    return pltpu.with_memory_space_constraint(a, pltpu.HBM)


def gather_start(name, lands, deps=()):
    n, nd = len(lands), len(deps)

    def body(*refs):
        land = refs[nd:nd + n]
        send_sems, recv_sems = refs[nd + n:nd + n + 2]
        token = refs[-1]
        x, y, c = _me()
        me = 4 * x + 2 * y + c
        for d in range(1, NDEV):
            for k in range(n):
                pltpu.make_async_remote_copy(
                    src_ref=land[k].at[me], dst_ref=land[k].at[me], send_sem=send_sems.at[k], recv_sem=recv_sems.at[k],
                    device_id=_peer((me + d) % NDEV), device_id_type=MESH).start()
        token[...] = jnp.zeros_like(token)

    res = pl.pallas_call(
        body, name=name,
        out_shape=(pltpu.SemaphoreType.DMA((n,)), pltpu.SemaphoreType.DMA((n,)),
                   *[pltpu.HBM(a.shape, a.dtype) for a in lands], jax.ShapeDtypeStruct((8, 128), f32)),
        in_specs=[_ANY] * nd + [_HBM] * n,
        out_specs=(_SEM, _SEM, *[_HBM] * n, pl.BlockSpec(memory_space=pltpu.VMEM)),
        input_output_aliases={nd + k: 2 + k for k in range(n)},
        compiler_params=pltpu.CompilerParams(has_side_effects=_EFFECT),
    )(*deps, *[_in_hbm(a) for a in lands])
    return res[0], res[1], list(res[2:2 + n]), res[-1]


def _other_chips(x, y):
    return [(1 - x, y), (x, 1 - y), (1 - x, 1 - y)]


def gather_start_chips(name, groups, deps=()):
    sizes = [len(g) for g in groups]
    lands = [a for g in groups for a in g]
    n, nd, ng = len(lands), len(deps), len(groups)

    def body(*refs):
        land = refs[nd:nd + n]
        sems = refs[nd + n:nd + n + 2 * ng]
        token = refs[-1]
        x, y, c = _me()
        me = 4 * x + 2 * y + c
        k = 0
        for g, size in enumerate(sizes):
            for j in range(size):
                for to in [(x, y, 1 - c)] + [(cx, cy, c) for cx, cy in _other_chips(x, y)]:
                    pltpu.make_async_remote_copy(
                        src_ref=land[k].at[me], dst_ref=land[k].at[me], send_sem=sems[2 * g].at[j],
                        recv_sem=sems[2 * g + 1].at[j], device_id=to, device_id_type=MESH).start()
                k += 1
        token[...] = jnp.zeros_like(token)

    res = pl.pallas_call(
        body, name=name,
        out_shape=(*[pltpu.SemaphoreType.DMA((size,)) for size in sizes for _ in range(2)],
                   *[pltpu.HBM(a.shape, a.dtype) for a in lands], jax.ShapeDtypeStruct((8, 128), f32)),
        in_specs=[_ANY] * nd + [_HBM] * n,
        out_specs=(*[_SEM] * (2 * ng), *[_HBM] * n, pl.BlockSpec(memory_space=pltpu.VMEM)),
        input_output_aliases={nd + k: 2 * ng + k for k in range(n)},
        compiler_params=pltpu.CompilerParams(has_side_effects=_EFFECT),
    )(*deps, *[_in_hbm(a) for a in lands])
    out, k = [], 2 * ng
    for g, size in enumerate(sizes):
        out.append((res[2 * g], res[2 * g + 1], list(res[k:k + size])))
        k += size
    return out, res[-1]


def gather_forward(name, send_sems, recv_sems, lands, after):
    n = len(lands)

    def body(*refs):
        land = refs[:n]
        ssem, rsem = refs[n:n + 2]
        send2, recv2 = refs[n + 3:n + 5]
        token = refs[-1]
        x, y, c = _me()
        for k in range(n):
            four = land[k].at[pl.ds(0, 4)]
            w = pltpu.make_async_remote_copy(src_ref=four, dst_ref=four, send_sem=ssem.at[k], recv_sem=rsem.at[k],
                                             device_id=(x, y, c), device_id_type=MESH)
            w.wait_send()
            w.wait_recv()
            for cx, cy in _other_chips(x, y):
                slot = 4 * cx + 2 * cy + c
                pltpu.make_async_remote_copy(
                    src_ref=land[k].at[slot], dst_ref=land[k].at[slot], send_sem=send2.at[k], recv_sem=recv2.at[k],
                    device_id=(x, y, 1 - c), device_id_type=MESH).start()
        token[...] = jnp.zeros_like(token)

    res = pl.pallas_call(
        body, name=name,
        out_shape=(pltpu.SemaphoreType.DMA((n,)), pltpu.SemaphoreType.DMA((n,)),
                   *[pltpu.HBM(a.shape, a.dtype) for a in lands], jax.ShapeDtypeStruct((8, 128), f32)),
        in_specs=[_HBM] * n + [_SEM, _SEM, _ANY],
        out_specs=(_SEM, _SEM, *[_HBM] * n, pl.BlockSpec(memory_space=pltpu.VMEM)),
        input_output_aliases={k: 2 + k for k in range(n)},
        compiler_params=pltpu.CompilerParams(has_side_effects=_EFFECT),
    )(*lands, send_sems, recv_sems, after)
    return res[0], res[1], list(res[2:2 + n]), res[-1]


def exchange_start(name, grads, deps=()):
    n, nd = len(grads), len(deps)
    lands = [lax.empty((NDEV - 1, g.shape[0]) + g.shape[2:], g.dtype) for g in grads]

    def body(*refs):
        src = refs[nd:nd + n]
        land = refs[nd + n:nd + 2 * n]
        send_sems, recv_sems = refs[nd + 2 * n:nd + 2 * n + 2]
        token = refs[-1]
        x, y, c = _me()
        me = 4 * x + 2 * y + c
        for d in range(1, NDEV):
            p = (me + d) % NDEV
            for k in range(n):
                pltpu.make_async_remote_copy(
                    src_ref=src[k].at[:, p], dst_ref=land[k].at[d - 1], send_sem=send_sems.at[k], recv_sem=recv_sems.at[k],
                    device_id=_peer(p), device_id_type=MESH).start()
        token[...] = jnp.zeros_like(token)

    both = list(grads) + lands
    res = pl.pallas_call(
        body, name=name,
        out_shape=(pltpu.SemaphoreType.DMA((n,)), pltpu.SemaphoreType.DMA((n,)),
                   *[pltpu.HBM(a.shape, a.dtype) for a in both], jax.ShapeDtypeStruct((8, 128), f32)),
        in_specs=[_ANY] * nd + [_HBM] * (2 * n),
        out_specs=(_SEM, _SEM, *[_HBM] * (2 * n), pl.BlockSpec(memory_space=pltpu.VMEM)),
        input_output_aliases={nd + k: 2 + k for k in range(2 * n)},
        compiler_params=pltpu.CompilerParams(has_side_effects=_EFFECT),
    )(*deps, *[_in_hbm(a) for a in both])
    return res[0], res[1], list(res[2:2 + n]), list(res[2 + n:2 + 2 * n]), res[-1]


def copies_wait(name, send_sems, recv_sems, sent, lands, after, count=NDEV - 1):
    ns, n = len(sent), len(lands)

    def body(*refs):
        land = refs[ns:ns + n]
        ssem, rsem = refs[ns + n:ns + n + 2]
        x, y, c = _me()
        for k in range(n):
            seven = land[k].at[pl.ds(0, count)]
            w = pltpu.make_async_remote_copy(src_ref=seven, dst_ref=seven, send_sem=ssem.at[k], recv_sem=rsem.at[k],
                                             device_id=(x, y, c), device_id_type=MESH)
            w.wait_send()
            w.wait_recv()

    both = list(sent) + list(lands)
    res = pl.pallas_call(
        body, name=name, out_shape=tuple(pltpu.HBM(a.shape, a.dtype) for a in both),
        in_specs=[_HBM] * (ns + n) + [_SEM, _SEM, _ANY], out_specs=tuple([_HBM] * (ns + n)),
        input_output_aliases={k: k for k in range(ns + n)},
        compiler_params=pltpu.CompilerParams(has_side_effects=_EFFECT),
    )(*both, send_sems, recv_sems, after)
    return list(res[:ns]), list(res[ns:])


def prep_layer(layer, me, col_sharded, row_sharded, deps=()):
    nc, nr = len(col_sharded), len(row_sharded)

    def body(me_ref, *refs):
        ins, outs = refs[:nc + nr], refs[nc + nr + len(deps):]
        for k in range(nc):
            outs[k][...] = ins[k][...].T.astype(bf16)
        for k in range(nc, nc + nr):
            outs[k][...] = ins[k][...].astype(bf16)

    arrs = list(col_sharded) + list(row_sharded)
    in_specs = [pl.BlockSpec((None,) + a.shape[1:], lambda i, me_ref: (layer, 0, 0)) for a in arrs]
    shapes = [(a.shape[2], a.shape[1]) for a in col_sharded] + [a.shape[1:] for a in row_sharded]
    out_specs = [pl.BlockSpec((None,) + s, lambda i, me_ref: (me_ref[0], 0, 0)) for s in shapes]
    return pl.pallas_call(
        body, name=f"prep_layer{layer}",
        grid_spec=pltpu.PrefetchScalarGridSpec(num_scalar_prefetch=1, grid=(1,), in_specs=in_specs + [_ANY] * len(deps),
                                               out_specs=out_specs),
        out_shape=[jax.ShapeDtypeStruct((NDEV,) + s, bf16) for s in shapes], compiler_params=_cp(48))(me, *arrs, *deps)


def slot_in(name, me, arrays):
    n = len(arrays)

    def body(me_ref, *refs):
        for k in range(n):
            refs[n + k][...] = refs[k][...]

    in_specs = [pl.BlockSpec(a.shape, lambda i, me_ref: (0, 0)) for a in arrays]
    out_specs = [pl.BlockSpec((None,) + a.shape, lambda i, me_ref: (me_ref[0], 0, 0)) for a in arrays]
    return pl.pallas_call(
        body, name=name,
        grid_spec=pltpu.PrefetchScalarGridSpec(num_scalar_prefetch=1, grid=(1,), in_specs=in_specs, out_specs=out_specs),
        out_shape=[jax.ShapeDtypeStruct((NDEV,) + a.shape, a.dtype) for a in arrays])(me, *arrays)


def _ff_chunks(ff, want=768):
    if ff % 256:
        return [slice(0, ff)]
    return [slice(c, min(c + want, ff)) for c in range(0, ff, want)]


def ffn_fwd(name, h, g, wgT, wuT, wd, deps=()):
    tp, d = h.shape
    ff = wgT.shape[0]
    tm = _row_tile(tp, 704)

    def body(h_ref, g_ref, wg_ref, wu_ref, wd_ref, ho_ref, a_ref, b_ref):
        hh = h_ref[...]
        hhat, _ = _rms(hh)
        n = (hhat * g_ref[...]).astype(bf16)
        acc = None
        for cols in _ff_chunks(ff):
            a = _dot_nt(n, wg_ref[cols, :])
            b = _dot_nt(n, wu_ref[cols, :])
            part = _dot_nn(((a * _sigmoid(a)) * b).astype(bf16), wd_ref[cols, :])
            acc = part if acc is None else acc + part
            a_ref[:, cols] = a.astype(bf16)
            b_ref[:, cols] = b.astype(bf16)
        ho_ref[...] = hh + 0.5 * acc

    row = lambda w: pl.BlockSpec((tm, w), lambda i: (i, 0))
    return pl.pallas_call(
        _skip(len(deps), body), name=name, grid=(tp // tm,),
        in_specs=[_ANY] * len(deps) + [row(d), _resident((1, d)), _resident((ff, d)), _resident((ff, d)), _resident((ff, d))],
        out_specs=[row(d), row(ff), row(ff)],
        out_shape=[jax.ShapeDtypeStruct((tp, d), f32), jax.ShapeDtypeStruct((tp, ff), bf16),
                   jax.ShapeDtypeStruct((tp, ff), bf16)],
        compiler_params=_cp(56, ("arbitrary",)))(*deps, h, g, wgT, wuT, wd)


def ffn_bwd(name, dy, h, g, a, b, wgT, wuT, wd, deps=()):
    tp, d = h.shape
    ff = wgT.shape[0]
    tm = _row_tile(tp, 384)

    def body(dy_ref, h_ref, g_ref, a_ref, b_ref, wg_ref, wu_ref, wd_ref, dh_ref, lhs_ref, rhs_ref, dg_ref):
        dyv = dy_ref[...]
        hhat, rs = _rms(h_ref[...])
        gv = g_ref[...]
        n = hhat * gv
        dyh = (0.5 * dyv).astype(bf16)
        dn = None
        for cols in _ff_chunks(ff):
            ds = _dot_nt(dyh, wd_ref[cols, :])
            av = a_ref[:, cols].astype(f32)
            bv = b_ref[:, cols].astype(f32)
            sg = _sigmoid(av)
            sa = av * sg
            da = (ds * bv * (sg * (1.0 + av * (1.0 - sg)))).astype(bf16)
            db = (ds * sa).astype(bf16)
            part = _dot_nn(da, wg_ref[cols, :]) + _dot_nn(db, wu_ref[cols, :])
            dn = part if dn is None else dn + part
            lhs_ref[0, :, cols] = da
            lhs_ref[1, :, cols] = db
            lhs_ref[2, :, cols] = (sa * bv).astype(bf16)
        dh_ref[...] = dyv + _rms_bwd(dn, gv, hhat, rs)

        @pl.when(pl.program_id(0) == 0)
        def _():
            dg_ref[...] = jnp.zeros_like(dg_ref)

        dg_ref[0:1, :] += jnp.sum(dn * hhat, axis=0, keepdims=True)
        rhs_ref[0] = n.astype(bf16)
        rhs_ref[1] = dyh

    row = lambda w: pl.BlockSpec((tm, w), lambda i: (i, 0))
    return pl.pallas_call(
        _skip(len(deps), body), name=name, grid=(tp // tm,),
        in_specs=[_ANY] * len(deps) + [row(d), row(d), _resident((1, d)), row(ff), row(ff),
                  _resident((ff, d)), _resident((ff, d)), _resident((ff, d))],
        out_specs=[row(d), pl.BlockSpec((3, tm, ff), lambda i: (0, i, 0)), pl.BlockSpec((2, tm, d), lambda i: (0, i, 0)),
                   pl.BlockSpec((8, d), lambda i: (0, 0))],
        out_shape=[jax.ShapeDtypeStruct((tp, d), f32), jax.ShapeDtypeStruct((3, tp, ff), bf16),
                   jax.ShapeDtypeStruct((2, tp, d), bf16), jax.ShapeDtypeStruct((8, d), f32)],
        compiler_params=_cp(58, ("arbitrary",)))(*deps, dy, h, g, a, b, wgT, wuT, wd)


def mm_tn(name, lhs, rhs, rhs_of, deps=(), only=None):
    _, tp, m = lhs.shape
    b0, nb = (0, lhs.shape[0]) if only is None else (only, 1)
    n = rhs.shape[2]
    def fits(t, ms):
        mb = m // ms
        return (tp % t == 0 and m % (128 * ms) == 0
                and 2 * t * (mb + n) * 2 + mb * n * (2 * 2 + 4 + (4 if t < tp else 0)) <= 54 * 2**20)

    tk, msplit = next(((t, ms) for t in (tp, 1408, 704, 384) for ms in (1, 2, 4) if fits(t, ms)), (128, 1))
    nk = tp // tk
    mb = m // msplit

    def body(l_ref, r_ref, o_ref, acc_ref):
        if nk == 1:
            o_ref[...] = _dot_tn(l_ref[...], r_ref[...]).astype(o_ref.dtype)
            return
        k = pl.program_id(2)

        @pl.when(k == 0)
        def _():
            acc_ref[...] = jnp.zeros_like(acc_ref)

        acc_ref[...] += _dot_tn(l_ref[...], r_ref[...])

        @pl.when(k == nk - 1)
        def _():
            o_ref[...] = acc_ref[...].astype(o_ref.dtype)

    return pl.pallas_call(
        _skip(len(deps), body), name=name, grid=(nb, msplit, nk),
        in_specs=[_ANY] * len(deps) + [pl.BlockSpec((None, tk, mb), lambda b, j, k: (b0 + b, k, j)),
                                       pl.BlockSpec((None, tk, n), lambda b, j, k: (rhs_of(b0 + b), k, 0))],
        out_specs=pl.BlockSpec((None, mb, n), lambda b, j, k: (b, j, 0)),
        out_shape=jax.ShapeDtypeStruct((nb, m, n), bf16),
        scratch_shapes=[pltpu.VMEM((mb, n) if nk > 1 else (8, 128), f32)],
        compiler_params=_cp(60, ("arbitrary", "arbitrary", "arbitrary")))(*deps, lhs, rhs)


def mix_in_fwd(name, h, g, winT, cos2, sin2):
    tp, d = h.shape
    nin = winT.shape[0]
    tm = _row_tile(tp)

    def body(h_ref, g_ref, w_ref, cos_ref, sin_ref, zq_ref, zg_ref, zu_ref, zgate_ref):
        hhat, _ = _rms(h_ref[...])
        z = _dot_nt((hhat * g_ref[...]).astype(bf16), w_ref[...])
        cosv, sinv = cos_ref[...], sin_ref[...]
        for hh in range(HEADS):
            qcols, kcols = slice(hh * HD, (hh + 1) * HD), slice(RW + hh * HD, RW + (hh + 1) * HD)
            zq_ref[:, qcols] = (_rot(z[:, qcols], cosv, sinv) * HD ** -0.5).astype(bf16)
            zq_ref[:, kcols] = _rot(z[:, kcols], cosv, sinv).astype(bf16)
        zq_ref[:, 2 * RW:] = z[:, 2 * RW:3 * RW].astype(bf16)
        zg_ref[...] = z[:, 3 * RW:4 * RW].astype(zg_ref.dtype)
        zu_ref[...] = z[:, 4 * RW:5 * RW]
        zgate_ref[...] = z[:, 5 * RW:].astype(zgate_ref.dtype)

    row = lambda w: pl.BlockSpec((tm, w), lambda i: (i, 0))
    widths = (3 * RW, RW, RW, 2 * d)
    return pl.pallas_call(
        body, name=name, grid=(tp // tm,),
        in_specs=[row(d), _resident((1, d)), _resident((nin, d)), row(HD), row(HD)],
        out_specs=[row(w) for w in widths],
        out_shape=[jax.ShapeDtypeStruct((tp, w), dt) for w, dt in zip(widths, (bf16, bf16, f32, bf16))],
        compiler_params=_cp(56, ("arbitrary",)))(h, g, winT, cos2, sin2)


def mix_in_bwd(name, dq, dk, dv, dzg, dzu, dzgate, h, g, winT, dres):
    tp, d = h.shape
    nin = winT.shape[0]
    tm = _row_tile(tp)

    def body(dq_ref, dk_ref, dv_ref, dzg_ref, dzu_ref, dzgate_ref, h_ref, g_ref, w_ref, dres_ref, dh_ref, dz_ref, n_ref, dg_ref):
        dz = jnp.concatenate([dq_ref[...], dk_ref[...], dv_ref[...], dzg_ref[...], dzu_ref[...], dzgate_ref[...]], axis=-1)
        dn = _dot_nn(dz, w_ref[...])
        hhat, rs = _rms(h_ref[...])
        gv = g_ref[...]
        dh_ref[...] = dres_ref[...] + _rms_bwd(dn, gv, hhat, rs)

        @pl.when(pl.program_id(0) == 0)
        def _():
            dg_ref[...] = jnp.zeros_like(dg_ref)

        dg_ref[0:1, :] += jnp.sum(dn * hhat, axis=0, keepdims=True)
        dz_ref[...] = dz
        n_ref[...] = (hhat * gv).astype(bf16)

    row = lambda w: pl.BlockSpec((tm, w), lambda i: (i, 0))
    return pl.pallas_call(
        body, name=name, grid=(tp // tm,),
        in_specs=[row(RW)] * 5 + [row(2 * d), row(d), _resident((1, d)), _resident((nin, d)), row(d)],
        out_specs=[row(d), pl.BlockSpec((None, tm, nin), lambda i: (0, i, 0)), pl.BlockSpec((None, tm, d), lambda i: (0, i, 0)),
                   pl.BlockSpec((8, d), lambda i: (0, 0))],
        out_shape=[jax.ShapeDtypeStruct((tp, d), f32), jax.ShapeDtypeStruct((1, tp, nin), bf16),
                   jax.ShapeDtypeStruct((1, tp, d), bf16), jax.ShapeDtypeStruct((8, d), f32)],
        compiler_params=_cp(56, ("arbitrary",)))(dq, dk, dv, dzg, dzu, dzgate, h, g, winT, dres)


def _retention_tables(tp, pad):
    half = HD // 2
    inv_freq = ROPE_BASE ** (-jnp.arange(half, dtype=f32) / half)
    pos = jnp.arange(tp, dtype=f32) - pad
    ang = pos[:, None] * inv_freq[None, :]
    cos, sin = jnp.cos(ang), jnp.sin(ang)
    cos2 = jnp.concatenate([cos, cos], axis=-1)
    sin2 = jnp.concatenate([-sin, sin], axis=-1)
    log_gamma = jnp.log1p(-(2.0 ** (-5.0 - jnp.arange(HEADS, dtype=f32))))
    idx = jnp.arange(CHUNK, dtype=f32)
    diff = idx[:, None] - idx[None, :]
    intra = jnp.where(diff[None] >= 0, jnp.exp(diff[None] * log_gamma[:, None, None]), 0.0)
    k_decay = jnp.exp((CHUNK - 1.0 - idx)[None, :] * log_gamma[:, None])
    q_decay = jnp.exp((idx + 1.0)[None, :] * log_gamma[:, None])
    chunk_decay = jnp.exp(CHUNK * log_gamma)
    full = (HEADS, CHUNK, HD)
    dec = jnp.stack([intra, jnp.broadcast_to(k_decay[:, :, None], full), jnp.broadcast_to(q_decay[:, :, None], full),
                     jnp.broadcast_to(chunk_decay[:, None, None], full)], axis=1)
    return cos2, sin2, dec


def _rot(t, cos2, sin2):
    return t * cos2 + pltpu.roll(t, HD // 2, 1) * sin2


def _rot_t(t, cos2, sin2):
    return t * cos2 - pltpu.roll(t, HD // 2, 1) * sin2


def _chunks_per_step(nch):
    return 3 if nch % 3 == 0 else 1


def retention_fwd(name, zq, dec):
    tp = zq.shape[0]
    nch = tp // CHUNK
    per = _chunks_per_step(nch)

    def body(q_ref, k_ref, v_ref, dec_ref, out_ref, st_ref, s_ref):
        @pl.when(pl.program_id(0) == 0)
        def _():
            s_ref[...] = jnp.zeros_like(s_ref)

        state = [s_ref[hh] for hh in range(HEADS)]
        for j in range(per):
            rows = slice(j * CHUNK, (j + 1) * CHUNK)
            for hh in range(HEADS):
                cols = slice(hh * HD, (hh + 1) * HD)
                qb, kb, vb = q_ref[rows, cols], k_ref[rows, cols], v_ref[rows, cols]
                sc = (_dot_nt(qb, kb) * dec_ref[hh, 0]).astype(bf16)
                sb = state[hh].astype(bf16)
                out_ref[rows, cols] = _dot_nn(sc, vb) + _dot_nn((qb.astype(f32) * dec_ref[hh, 2]).astype(bf16), sb)
                st_ref[hh, j] = sb
                state[hh] = state[hh] * dec_ref[hh, 3] + _dot_tn((kb.astype(f32) * dec_ref[hh, 1]).astype(bf16), vb)
        for hh in range(HEADS):
            s_ref[hh] = state[hh]

    part = lambda j: pl.BlockSpec((per * CHUNK, RW), lambda n: (n, j))
    return pl.pallas_call(
        body, name=name, grid=(nch // per,),
        in_specs=[part(0), part(1), part(2), _resident((HEADS, 4, CHUNK, HD))],
        out_specs=[part(0), pl.BlockSpec((HEADS, per, HD, HD), lambda n: (0, n, 0, 0))],
        out_shape=[jax.ShapeDtypeStruct((tp, RW), f32), jax.ShapeDtypeStruct((HEADS, nch, HD, HD), bf16)],
        scratch_shapes=[pltpu.VMEM((HEADS, HD, HD), f32)],
        compiler_params=_cp(32, ("arbitrary",)))(zq, zq, zq, dec)


def retention_bwd(name, zq, cos2, sin2, dec, states, dout, pad, deps=()):
    tp = zq.shape[0]
    nch = tp // CHUNK
    per = _chunks_per_step(nch)
    nblk = nch // per
    scale = HD ** -0.5

    def body(q_ref, k_ref, v_ref, cos_ref, sin_ref, dec_ref, st_ref, do_ref, dq_ref, dk_ref, dv_ref, g_ref):
        @pl.when(pl.program_id(0) == 0)
        def _():
            g_ref[...] = jnp.zeros_like(g_ref)

        first_row = (nblk - 1 - pl.program_id(0)) * (per * CHUNK)
        gstate = [g_ref[hh] for hh in range(HEADS)]
        for j in reversed(range(per)):
            rows = slice(j * CHUNK, (j + 1) * CHUNK)
            cosv, sinv = cos_ref[rows, :], sin_ref[rows, :]
            keep = (lax.broadcasted_iota(jnp.int32, (CHUNK, HD), 0) + (first_row + j * CHUNK)) >= pad
            for hh in range(HEADS):
                cols = slice(hh * HD, (hh + 1) * HD)
                intra, kdec, qdec = dec_ref[hh, 0], dec_ref[hh, 1], dec_ref[hh, 2]
                qb, kb, vb = q_ref[rows, cols], k_ref[rows, cols], v_ref[rows, cols]
                qd = (qb.astype(f32) * qdec).astype(bf16)
                kd = (kb.astype(f32) * kdec).astype(bf16)
                sc = (_dot_nt(qb, kb) * intra).astype(bf16)
                dob = do_ref[rows, cols]
                sb = st_ref[hh, j]
                gb = gstate[hh].astype(bf16)
                dsc = (_dot_nt(dob, vb) * intra).astype(bf16)
                dv = _dot_tn(sc, dob) + _dot_nn(kd, gb)
                dqr = _dot_nn(dsc, kb) + _dot_nt(dob, sb) * qdec
                dkr = _dot_tn(dsc, qb) + _dot_nt(vb, gb) * kdec
                gstate[hh] = gstate[hh] * dec_ref[hh, 3] + _dot_tn(qd, dob)
                dq_ref[rows, cols] = jnp.where(keep, _rot_t(dqr * scale, cosv, sinv), 0.0).astype(bf16)
                dk_ref[rows, cols] = jnp.where(keep, _rot_t(dkr, cosv, sinv), 0.0).astype(bf16)
                dv_ref[rows, cols] = jnp.where(keep, dv, 0.0).astype(bf16)
        for hh in range(HEADS):
            g_ref[hh] = gstate[hh]

    part = lambda j: pl.BlockSpec((per * CHUNK, RW), lambda t: (nblk - 1 - t, j))
    table = pl.BlockSpec((per * CHUNK, HD), lambda t: (nblk - 1 - t, 0))
    return pl.pallas_call(
        _skip(len(deps), body), name=name, grid=(nblk,),
        in_specs=[_ANY] * len(deps) + [part(0), part(1), part(2), table, table, _resident((HEADS, 4, CHUNK, HD)),
                                       pl.BlockSpec((HEADS, per, HD, HD), lambda t: (0, nblk - 1 - t, 0, 0)), part(0)],
        out_specs=[part(0)] * 3,
        out_shape=[jax.ShapeDtypeStruct((tp, RW), bf16)] * 3,
        scratch_shapes=[pltpu.VMEM((HEADS, HD, HD), f32)],
        compiler_params=_cp(32, ("arbitrary",)))(*deps, zq, zq, zq, cos2, sin2, dec, states, dout)


def _window_sum(xv, steps, tp, forward):
    s = xv
    for j in range(steps):
        sh = 2 ** j
        s = s + pltpu.roll(s, (tp - sh) if forward else sh, 0)
    return s


def pool_fwd(name, zu, maps, scale, pad):
    tp = zu.shape[0]

    def body(u_ref, maps_ref, scale_ref, pooled_ref, p_ref):
        row = lax.broadcasted_iota(jnp.int32, (tp, HD), 0)
        for gi, w in enumerate(POOL_WINDOWS):
            cols = slice(gi * HD, (gi + 1) * HD)
            xv = u_ref[:, cols]
            cnt = jnp.clip(row - (pad - 1), 1, w).astype(f32)
            pooled = jnp.where(row >= pad, _window_sum(xv, gi + 1, tp, False) / cnt - xv, 0.0).astype(bf16)
            pooled_ref[:, cols] = pooled
            p_ref[:, cols] = (_dot_nn(pooled, maps_ref[gi].astype(bf16)) * scale_ref[:, cols]).astype(bf16)

    return pl.pallas_call(
        body, name=name,
        out_shape=[jax.ShapeDtypeStruct((tp, RW), bf16), jax.ShapeDtypeStruct((tp, RW), bf16)],
        compiler_params=_cp(56))(zu, maps, scale)


def pool_bwd(name, dp, pooled, maps, scale, pad):
    tp = dp.shape[0]

    def body(dp_ref, pooled_ref, maps_ref, scale_ref, du_ref, dmaps_ref, dscale_ref):
        row = lax.broadcasted_iota(jnp.int32, (tp, HD), 0)
        dscale_ref[...] = jnp.zeros_like(dscale_ref)
        for gi, w in enumerate(POOL_WINDOWS):
            cols = slice(gi * HD, (gi + 1) * HD)
            mb = maps_ref[gi].astype(bf16)
            pooled = pooled_ref[:, cols]
            dpf = dp_ref[:, cols].astype(f32)
            dscale_ref[0:1, cols] = jnp.sum(dpf * _dot_nn(pooled, mb), axis=0, keepdims=True)
            dpm = (dpf * scale_ref[:, cols]).astype(bf16)
            dmaps_ref[gi * HD:(gi + 1) * HD, :] = _dot_tn(pooled, dpm)
            dpool = jnp.where(row >= pad, _dot_nt(dpm, mb), 0.0)
            cnt = jnp.clip(row - (pad - 1), 1, w).astype(f32)
            du = _window_sum(dpool / cnt, gi + 1, tp, True) - dpool
            du_ref[:, cols] = jnp.where(row >= pad, du, 0.0).astype(bf16)

    return pl.pallas_call(
        body, name=name,
        out_shape=[jax.ShapeDtypeStruct((tp, RW), bf16), jax.ShapeDtypeStruct((HEADS * HD, HD), f32),
                   jax.ShapeDtypeStruct((8, RW), f32)],
        compiler_params=_cp(56))(dp, pooled, maps, scale)


def _group_norm(o):
    mu = jnp.mean(o, axis=-1, keepdims=True)
    oc = o - mu
    rstd = lax.rsqrt(jnp.mean(oc * oc, axis=-1, keepdims=True) + EPS)
    return oc * rstd, rstd


def mix_out_fwd(name, h, oraw, zg, zgate, p, wretT, wpoolT, wout, deps=()):
    tp, d = h.shape
    tm = _row_tile(tp)

    def body(h_ref, o_ref, zg_ref, zgate_ref, p_ref, wr_ref, wp_ref, wo_ref, ho_ref, rp_ref, ret_ref, pool_ref, mixed_ref):
        parts = []
        for hh in range(HEADS):
            cols = slice(hh * HD, (hh + 1) * HD)
            rhat, _ = _group_norm(o_ref[:, cols])
            gv = zg_ref[:, cols].astype(f32)
            parts.append(rhat * (gv * _sigmoid(gv)))
        r = jnp.concatenate(parts, axis=-1).astype(bf16)
        pv = p_ref[...]
        ret = _dot_nt(r, wr_ref[...])
        pool = _dot_nt(pv, wp_ref[...])
        mixed = (_sigmoid(zgate_ref[:, :d].astype(f32)) * ret + _sigmoid(zgate_ref[:, d:].astype(f32)) * pool).astype(bf16)
        ho_ref[...] = h_ref[...] + _dot_nn(mixed, wo_ref[...])
        rp_ref[0] = r
        rp_ref[1] = pv
        ret_ref[...] = ret.astype(bf16)
        pool_ref[...] = pool.astype(bf16)
        mixed_ref[...] = mixed

    row = lambda w: pl.BlockSpec((tm, w), lambda i: (i, 0))
    return pl.pallas_call(
        _skip(len(deps), body), name=name, grid=(tp // tm,),
        in_specs=[_ANY] * len(deps) + [row(d), row(RW), row(RW), row(2 * d), row(RW), _resident((d, RW)), _resident((d, RW)),
                                       _resident((d, d))],
        out_specs=[row(d), pl.BlockSpec((2, tm, RW), lambda i: (0, i, 0)), row(d), row(d),
                   pl.BlockSpec((None, tm, d), lambda i: (0, i, 0))],
        out_shape=[jax.ShapeDtypeStruct((tp, d), f32), jax.ShapeDtypeStruct((2, tp, RW), bf16),
                   jax.ShapeDtypeStruct((tp, d), bf16), jax.ShapeDtypeStruct((tp, d), bf16),
                   jax.ShapeDtypeStruct((1, tp, d), bf16)],
        compiler_params=_cp(48, ("arbitrary",)))(*deps, h, oraw, zg, zgate, p, wretT, wpoolT, wout)


def mix_out_bwd(name, dy, oraw, zg, zgate, ret, pool, wretT, wpoolT, wout, deps=()):
    tp, d = dy.shape
    tm = _row_tile(tp)

    def body(dy_ref, o_ref, zg_ref, zgate_ref, ret_ref, pool_ref, wr_ref, wp_ref, wo_ref,
             do_ref, dzg_ref, dzgate_ref, dp_ref, drp_ref, dyb_ref):
        dyb = dy_ref[...].astype(bf16)
        dmixed = _dot_nt(dyb, wo_ref[...])
        sa = _sigmoid(zgate_ref[:, :d].astype(f32))
        sb = _sigmoid(zgate_ref[:, d:].astype(f32))
        dret = dmixed * sa
        dpool = dmixed * sb
        dzgate_ref[:, :d] = (dret * ret_ref[...].astype(f32) * (1.0 - sa)).astype(bf16)
        dzgate_ref[:, d:] = (dpool * pool_ref[...].astype(f32) * (1.0 - sb)).astype(bf16)
        dretb, dpoolb = dret.astype(bf16), dpool.astype(bf16)
        dr = _dot_nn(dretb, wr_ref[...])
        dp_ref[...] = _dot_nn(dpoolb, wp_ref[...]).astype(bf16)
        for hh in range(HEADS):
            cols = slice(hh * HD, (hh + 1) * HD)
            rhat, rstd = _group_norm(o_ref[:, cols])
            gv = zg_ref[:, cols].astype(f32)
            sg = _sigmoid(gv)
            drh = dr[:, cols]
            drhat = drh * (gv * sg)
            dzg_ref[:, cols] = (drh * rhat * (sg * (1.0 + gv * (1.0 - sg)))).astype(bf16)
            do = rstd * (drhat - jnp.mean(drhat, axis=-1, keepdims=True)
                         - rhat * jnp.mean(drhat * rhat, axis=-1, keepdims=True))
            do_ref[:, cols] = do.astype(bf16)
        drp_ref[0] = dretb
        drp_ref[1] = dpoolb
        dyb_ref[...] = dyb

    row = lambda w: pl.BlockSpec((tm, w), lambda i: (i, 0))
    return pl.pallas_call(
        _skip(len(deps), body), name=name, grid=(tp // tm,),
        in_specs=[_ANY] * len(deps) + [row(d), row(RW), row(RW), row(2 * d), row(d), row(d), _resident((d, RW)), _resident((d, RW)),
                  _resident((d, d))],
        out_specs=[row(RW), row(RW), row(2 * d), row(RW), pl.BlockSpec((2, tm, d), lambda i: (0, i, 0)),
                   pl.BlockSpec((None, tm, d), lambda i: (0, i, 0))],
        out_shape=[jax.ShapeDtypeStruct((tp, RW), bf16), jax.ShapeDtypeStruct((tp, RW), bf16),
                   jax.ShapeDtypeStruct((tp, 2 * d), bf16), jax.ShapeDtypeStruct((tp, RW), bf16),
                   jax.ShapeDtypeStruct((2, tp, d), bf16), jax.ShapeDtypeStruct((1, tp, d), bf16)],
        compiler_params=_cp(48, ("arbitrary",)))(*deps, dy, oraw, zg, zgate, ret, pool, wretT, wpoolT, wout)


def final_loss(name, h, g, target):
    tp, d = h.shape
    tm = _row_tile(tp)
    nsub = tm // CHUNK

    def body(h_ref, g_ref, *rest):
        t_refs = rest[:nsub]
        dh_ref, loss_ref, dg_ref = rest[nsub:]
        i = pl.program_id(0)

        @pl.when(i == 0)
        def _():
            loss_ref[...] = jnp.zeros_like(loss_ref)
            dg_ref[...] = jnp.zeros_like(dg_ref)

        gv = g_ref[...]
        for j in range(nsub):
            rows = slice(j * CHUNK, (j + 1) * CHUNK)
            hhat, rs = _rms(h_ref[rows, :])
            err = jnp.where(i * nsub + j >= 1, hhat * gv - t_refs[j][...], 0.0)
            dyv = err / d
            dh_ref[rows, :] = _rms_bwd(dyv, gv, hhat, rs)
            loss_ref[...] += 0.5 * jnp.sum(jnp.sum(err * err, axis=-1, keepdims=True) / d)
            dg_ref[0:1, :] += jnp.sum(dyv * hhat, axis=0, keepdims=True)

    lagged = lambda j: pl.BlockSpec((CHUNK, d), lambda i: (jnp.maximum(i * nsub + j - 1, 0), 0))
    return pl.pallas_call(
        body, name=name, grid=(tp // tm,),
        in_specs=[pl.BlockSpec((tm, d), lambda i: (i, 0)), _resident((1, d))] + [lagged(j) for j in range(nsub)],
        out_specs=[pl.BlockSpec((tm, d), lambda i: (i, 0)), pl.BlockSpec((8, 128), lambda i: (0, 0)),
                   pl.BlockSpec((8, d), lambda i: (0, 0))],
        out_shape=[jax.ShapeDtypeStruct((tp, d), f32), jax.ShapeDtypeStruct((8, 128), f32),
                   jax.ShapeDtypeStruct((8, d), f32)],
        compiler_params=_cp(32, ("arbitrary",)))(h, g, *[target] * nsub)


def _adamw(w, g, m, v):
    m = ADAM_B1 * m + (1.0 - ADAM_B1) * g
    v = ADAM_B2 * v + (1.0 - ADAM_B2) * (g * g)
    m_hat = m / (1.0 - ADAM_B1 ** ADAM_STEP)
    v_hat = v / (1.0 - ADAM_B2 ** ADAM_STEP)
    delta = -ADAM_LR * (m_hat / (jnp.sqrt(v_hat) + ADAM_EPS) + ADAM_WD * w)
    return delta, m, v


def adam_big(name, me, recv, own, b, layer, transposed, w, m, v, prev):
    r, c = recv.shape[2:]
    wshape = w.shape[1:]
    nchunk = 1 if transposed else next(k for k in (4, 2, 1) if r % (16 * k) == 0)
    rc = r // nchunk

    def body(me_ref, recv_ref, own_ref, w_ref, m_ref, v_ref, *rest):
        g_ref, d_ref, nm_ref, nv_ref = rest[-4:]
        g = own_ref[...].astype(f32)
        for j in range(NDEV - 1):
            g = g + recv_ref[j].astype(f32)
        if transposed:
            g = g.T
        delta, nm, nv = _adamw(w_ref[...], g, m_ref[...], v_ref[...])
        g_ref[...] = g
        d_ref[...] = delta
        nm_ref[...] = nm
        nv_ref[...] = nv

    wblock = wshape if transposed else (rc, c)
    wspec = pl.BlockSpec((None,) + wblock, lambda i, me_ref: (layer, i, 0))
    in_specs = [pl.BlockSpec((NDEV - 1, None, rc, c), lambda i, me_ref: (0, b, i, 0)),
                pl.BlockSpec((None, None, rc, c), lambda i, me_ref: (b, me_ref[0], i, 0)), wspec, wspec, wspec]
    args = [recv, own, w, m, v]
    aliases = {}
    if prev is not None:
        in_specs += [_ANY] * 4
        args += list(prev)
        aliases = {6 + k: k for k in range(4)}
    return pl.pallas_call(
        body, name=name,
        grid_spec=pltpu.PrefetchScalarGridSpec(num_scalar_prefetch=1, grid=(nchunk,), in_specs=in_specs,
                                               out_specs=[wspec] * 4),
        out_shape=[jax.ShapeDtypeStruct(w.shape, f32)] * 4, input_output_aliases=aliases,
        compiler_params=_cp(56))(me, *args)


def adam_small(name, ga0, gmaps0, gmeta, ga1, gmaps1, norms, pool_scale, pool_maps, meta, final_norm, d):
    def body(ga0_ref, gmaps0_ref, gmeta_ref, ga1_ref, gmaps1_ref, *refs):
        ins, outs = refs[:21], refs[21:]
        x, y, c = _me()
        me = 4 * x + 2 * y + c

        def total(ref, rows):
            t = ref[0, rows, :]
            for j in range(1, NDEV):
                t = t + ref[j, rows, :]
            return t

        row = lambda r: slice(r, r + 1)
        outs[0][...] = jnp.broadcast_to(total(ga1_ref, row(0))[:, :128], (8, 128))

        def update(k, g, o):
            w_ref, m_ref, v_ref = ins[3 * k:3 * k + 3]
            delta, nm, nv = _adamw(w_ref[...], g, m_ref[...], v_ref[...])
            for ref, val in zip(outs[o:o + 4], (g, delta, nm, nv)):
                ref[...] = val

        two = lax.broadcasted_iota(jnp.int32, (2, d), 0)
        for k in range(3):
            update(k, jnp.where(two == 0, total(ga0_ref, row(k)), total(ga1_ref, row(2 + k))), 1 + 4 * k)
        update(3, jnp.where(two[:, :RW] == 0, total(ga0_ref, row(3))[:, :RW], total(ga1_ref, row(5))[:, :RW]), 13)
        update(4, jnp.concatenate([total(gmaps0_ref, slice(None)), total(gmaps1_ref, slice(None))], axis=0), 17)
        update(5, total(gmeta_ref, pl.ds(pl.multiple_of(me * N_META, N_META), N_META)), 21)
        update(6, total(ga1_ref, row(1)), 25)

    flat = []
    for trip in (*norms, pool_scale, pool_maps, meta, final_norm):
        flat += list(trip)
    out_shapes = [jax.ShapeDtypeStruct((8, 128), f32)]
    for trip in (*norms, pool_scale, pool_maps, meta, final_norm):
        out_shapes += [jax.ShapeDtypeStruct(trip[0].shape, f32)] * 4
    return pl.pallas_call(body, name=name, out_shape=out_shapes,
                          compiler_params=_cp(32))(ga0, gmaps0, gmeta, ga1, gmaps1, *flat)


def kernel(x, meta, ffn1_norm, ffn1_gate, ffn1_up, ffn1_down, mix_norm, w_in, pool_maps, pool_scale, w_ret_up, w_pool_up, w_out, ffn2_norm, ffn2_gate, ffn2_up, ffn2_down, final_norm, loss_target, m_meta, m_ffn1_norm, m_ffn1_gate, m_ffn1_up, m_ffn1_down, m_mix_norm, m_w_in, m_pool_maps, m_pool_scale, m_w_ret_up, m_w_pool_up, m_w_out, m_ffn2_norm, m_ffn2_gate, m_ffn2_up, m_ffn2_down, m_final_norm, v_meta, v_ffn1_norm, v_ffn1_gate, v_ffn1_up, v_ffn1_down, v_mix_norm, v_w_in, v_pool_maps, v_pool_scale, v_w_ret_up, v_w_pool_up, v_w_out, v_ffn2_norm, v_ffn2_gate, v_ffn2_up, v_ffn2_down, v_final_norm):
    seq, d = x.shape[1], x.shape[2]
    depth = ffn1_gate.shape[0]
    ff = ffn1_gate.shape[2] * NDEV
    nin = w_in.shape[2] * NDEV
    length = seq + N_META
    pad = (-length) % CHUNK
    tp = length + pad
    assert pad % 8 == 0 and pad + N_META == CHUNK and depth == 2 and nin == 5 * RW + 2 * d

    ix, iy, ic = _me()
    me = (4 * ix + 2 * iy + ic).astype(jnp.int32).reshape(1)

    tview = lambda *arrs: [jnp.swapaxes(a, 1, 2) for a in arrs]
    t_g1, t_u1, t_g2, t_u2, t_in = (tview(w, m, v) for w, m, v in (
        (ffn1_gate, m_ffn1_gate, v_ffn1_gate), (ffn1_up, m_ffn1_up, v_ffn1_up), (ffn2_gate, m_ffn2_gate, v_ffn2_gate),
        (ffn2_up, m_ffn2_up, v_ffn2_up), (w_in, m_w_in, v_w_in)))
    keys, groups = [], []
    for layer in range(depth):
        lands = prep_layer(layer, me, [w_ret_up, w_pool_up],
                           [t_g1[0], t_u1[0], t_g2[0], t_u2[0], t_in[0], ffn1_down, ffn2_down, w_out],
                           (token,) if layer else ())
        wretT, wpoolT, g1T, u1T, g2T, u2T, winT, d1, d2, wout = lands
        keys += [("ffn1", layer), ("mix", layer), ("ffn2", layer)]
        groups += [[g1T, u1T, d1], [winT, wretT, wpoolT, wout], [g2T, u2T, d2]]
        if layer == 0:
            first, token = gather_start_chips("gather_start_first", groups[:1])
            meta_all, = all_gather("gather_meta", [meta], (token,))
            meta_full = jnp.transpose(meta_all, (1, 0, 2)).reshape(N_META, d)
            token = meta_all
    started, token = gather_start_chips("gather_start_rest", groups[1:], (token,))
    gathers = dict(zip(keys, first + started))

    def forward(part, layer, after):
        ssem, rsem, group = gathers[(part, layer)]
        ssem, rsem, group, tok = gather_forward(f"gather_forward_{part}{layer}", ssem, rsem, group, after)
        gathers[(part, layer)] = (ssem, rsem, group)
        return tok

    def gathered(part, layer, after):
        ssem, rsem, group = gathers[(part, layer)]
        _, full = copies_wait(f"gather_wait_{part}{layer}", ssem, rsem, (), group, after, 3)
        return [a.reshape((NDEV * a.shape[1],) + a.shape[2:]) for a in full]

    cos2, sin2, dec = _retention_tables(tp, pad)
    h = jnp.concatenate([jnp.zeros((pad, d), f32), meta_full + token[0, 0], x[0]], axis=0)

    saved = []
    weights = []
    tok = forward("ffn1", 0, h)
    for layer in range(depth):
        row = lambda a: a[layer:layer + 1]
        s = {"h0": h}
        g1T, u1T, d1 = gathered("ffn1", layer, tok if layer == 0 else h)
        tok = forward("mix", layer, h) if layer else None
        h, s["a1"], s["b1"] = ffn_fwd(f"ffn1_fwd{layer}", h, row(ffn1_norm), g1T, u1T, d1, (tok,) if layer else ())
        s["h1"] = h
        if layer == 0:
            tok = forward("mix", layer, h)
        winT, wretT, wpoolT, wout = gathered("mix", layer, tok if layer == 0 else h)
        s["zq"], s["zg"], zu, s["zgate"] = mix_in_fwd(f"mix_in_fwd{layer}", h, row(mix_norm), winT, cos2, sin2)
        s["oraw"], s["states"] = retention_fwd(f"retention_fwd{layer}", s["zq"], dec)
        s["pooled"], p = pool_fwd(f"pool_fwd{layer}", zu, pool_maps[layer], row(pool_scale), pad)
        tok = forward("ffn2", layer, p)
        h, s["rp"], s["ret"], s["pool"], s["mixed"] = mix_out_fwd(
            f"mix_out_fwd{layer}", h, s["oraw"], s["zg"], s["zgate"], p, wretT, wpoolT, wout, (tok,))
        s["h2"] = h
        g2T, u2T, d2 = gathered("ffn2", layer, h)
        tok = (forward("ffn1", layer + 1, h),) if layer + 1 < depth else ()
        h, s["a2"], s["b2"] = ffn_fwd(f"ffn2_fwd{layer}", h, row(ffn2_norm), g2T, u2T, d2, tok)
        saved.append(s)
        weights.append((g1T, u1T, g2T, u2T, winT, wretT, wpoolT, d1, d2, wout))

    dh, loss_part, dg_final = final_loss("final_loss", h, final_norm.reshape(1, d), loss_target[0])

    small = {}
    small_gathers = {}
    exchanges = {}
    token = None

    def rows8(vals):
        at = lax.broadcasted_iota(jnp.int32, (8, d), 0)
        out = jnp.zeros((8, d), f32)
        for k, v in enumerate(vals):
            r0 = v[0:1]
            r0 = r0 if r0.shape[1] == d else jnp.pad(r0, ((0, 0), (0, d - r0.shape[1])))
            out = jnp.where(at == k, r0, out)
        return out

    def exchange(part, layer, grads):
        by_dest = [g.reshape(g.shape[0], NDEV, g.shape[1] // NDEV, g.shape[2]) for g in grads]
        ssem, rsem, sent, lands, tok = exchange_start(f"exchange_start_{part}{layer}", by_dest)
        exchanges[(part, layer)] = (ssem, rsem, sent, lands)
        return (tok,)

    for layer in reversed(range(depth)):
        g1T, u1T, g2T, u2T, winT, wretT, wpoolT, d1, d2, wout = weights[layer]
        row = lambda a: a[layer:layer + 1]
        s = saved[layer]
        dh, lhs2, rhs2, small[("ffn2", layer)] = ffn_bwd(
            f"ffn2_bwd{layer}", dh, s["h2"], row(ffn2_norm), s["a2"], s["b2"], g2T, u2T, d2, () if token is None else token)
        token = exchange("ffn2", layer, [mm_tn(f"ffn2_wgrad{layer}", lhs2, rhs2, lambda b: b // 2)])
        do, dzg, dzgate, dp, drp, dyb = mix_out_bwd(
            f"mix_out_bwd{layer}", dh, s["oraw"], s["zg"], s["zgate"], s["ret"], s["pool"], wretT, wpoolT, wout, token)
        gw_mix = [mm_tn(f"w_out_wgrad{layer}", s["mixed"], dyb, lambda b: b),
                  mm_tn(f"up_wgrad{layer}", drp, s["rp"], lambda b: b)]
        dq, dk, dv = retention_bwd(f"retention_bwd{layer}", s["zq"], cos2, sin2, dec, s["states"], do, pad, token)
        dzu, small[("maps", layer)], small[("scale", layer)] = pool_bwd(
            f"pool_bwd{layer}", dp, s["pooled"], pool_maps[layer], row(pool_scale), pad)
        dh, dz, n2, small[("mix", layer)] = mix_in_bwd(
            f"mix_in_bwd{layer}", dq, dk, dv, dzg, dzu, dzgate, s["h1"], row(mix_norm), winT, dh)
        token = exchange("mix", layer, gw_mix + [mm_tn(f"w_in_wgrad{layer}", dz, n2, lambda b: b)])
        dh, lhs1, rhs1, small[("ffn1", layer)] = ffn_bwd(
            f"ffn1_bwd{layer}", dh, s["h0"], row(ffn1_norm), s["a1"], s["b1"], g1T, u1T, d1, token)
        rows = [small[("ffn1", layer)], small[("mix", layer)], small[("ffn2", layer)], small[("scale", layer)]]
        packs = [rows8([loss_part, dg_final] + rows if layer == depth - 1 else rows), small[("maps", layer)]]
        if layer == 0:
            dmeta = dh[pad:CHUNK]
            packs.append(jnp.transpose(dmeta.reshape(N_META, NDEV, d // NDEV), (1, 0, 2)).reshape(NDEV * N_META, d // NDEV))
        ssem, rsem, lands, tok = gather_start(f"small_start{layer}", slot_in(f"small_slot{layer}", me, packs))
        small_gathers[layer] = (ssem, rsem, lands)
        if layer:
            token = exchange("ffn1", layer, [mm_tn(f"ffn1_wgrad{layer}", lhs1, rhs1, lambda b: b // 2, (tok,))])
        else:
            token = (tok,)
            for j, nm in enumerate(("ffn1_gate", "ffn1_up", "ffn1_down")):
                token = exchange(nm, layer, [mm_tn(f"{nm}_wgrad{layer}", lhs1, rhs1, lambda b: b // 2, token, only=j)])

    grad_x = dh[CHUNK:][None]

    big = {}
    after = token[0]
    plans = {
        "ffn2": [("ffn2_gate", 0, 0, False, *t_g2), ("ffn2_up", 0, 1, False, *t_u2),
                 ("ffn2_down", 0, 2, False, ffn2_down, m_ffn2_down, v_ffn2_down)],
        "mix": [("w_out", 0, 0, False, w_out, m_w_out, v_w_out),
                ("w_ret_up", 1, 0, True, w_ret_up, m_w_ret_up, v_w_ret_up),
                ("w_pool_up", 1, 1, True, w_pool_up, m_w_pool_up, v_w_pool_up), ("w_in", 2, 0, False, *t_in)],
        "ffn1": [("ffn1_gate", 0, 0, False, *t_g1), ("ffn1_up", 0, 1, False, *t_u1),
                 ("ffn1_down", 0, 2, False, ffn1_down, m_ffn1_down, v_ffn1_down)]}
    for nm, k, b, tr, w, m, v in plans["ffn1"]:
        plans[nm] = [(nm, 0, 0, tr, w, m, v)]
    for layer in reversed(range(depth)):
        for part in ("ffn2", "mix") + (("ffn1",) if layer else ("ffn1_gate", "ffn1_up", "ffn1_down")):
            ssem, rsem, sent, lands = exchanges[(part, layer)]
            sent, lands = copies_wait(f"exchange_wait_{part}{layer}", ssem, rsem, sent, lands, after)
            for nm, k, b, tr, w, m, v in plans[part]:
                big[nm] = adam_big(f"adam_{nm}{layer}", me, lands[k], sent[k], b, layer, tr, w, m, v, big.get(nm))
                after = big[nm][0]

    gsmall = []
    for layer in range(depth):
        ssem, rsem, lands = small_gathers[layer]
        gsmall += copies_wait(f"small_wait{layer}", ssem, rsem, (), lands, after)[1]

    maps2 = lambda a: a.reshape(depth * HEADS * HD, HD)
    res = adam_small(
        "adam_small", *gsmall,
        [(ffn1_norm, m_ffn1_norm, v_ffn1_norm), (mix_norm, m_mix_norm, v_mix_norm), (ffn2_norm, m_ffn2_norm, v_ffn2_norm)],
        (pool_scale, m_pool_scale, v_pool_scale), (maps2(pool_maps), maps2(m_pool_maps), maps2(v_pool_maps)),
        (meta, m_meta, v_meta), tuple(a.reshape(1, d) for a in (final_norm, m_final_norm, v_final_norm)), d)
    loss = res[0][0, 0]
    sm = {}
    for k, nm in enumerate(["ffn1_norm", "mix_norm", "ffn2_norm", "pool_scale", "pool_maps", "meta", "final_norm"]):
        sm[nm] = list(res[1 + 4 * k:5 + 4 * k])
    sm["pool_maps"] = [a.reshape(pool_maps.shape) for a in sm["pool_maps"]]
    sm["final_norm"] = [a.reshape(d) for a in sm["final_norm"]]

    names = ["meta", "ffn1_norm", "ffn1_gate", "ffn1_up", "ffn1_down", "mix_norm", "w_in", "pool_maps", "pool_scale",
             "w_ret_up", "w_pool_up", "w_out", "ffn2_norm", "ffn2_gate", "ffn2_up", "ffn2_down", "final_norm"]
    for nm in ("ffn1_gate", "ffn1_up", "ffn2_gate", "ffn2_up", "w_in"):
        big[nm] = tview(*big[nm])
    allw = {**{k: list(v) for k, v in big.items()}, **sm}
    outs = [loss, grad_x]
    for kind in range(4):
        outs += [allw[nm][kind] for nm in names]
    return tuple(outs)
```

```python
import functools

import jax
import jax.numpy as jnp
from jax import lax
from jax.experimental import pallas as pl
from jax.experimental.pallas import tpu as pltpu

f32 = jnp.float32
bf16 = jnp.bfloat16
MESH = pl.DeviceIdType.MESH
NDEV = 8
N_META = 16
HEADS = 4
HD = 128
CHUNK = 128
RW = HEADS * HD
POOL_WINDOWS = (2, 4, 8, 16)
ROPE_BASE = 10000.0
EPS = 1e-6
ADAM_LR = 0.001
ADAM_B1 = 0.9
ADAM_B2 = 0.999
ADAM_EPS = 1e-08
ADAM_WD = 0.01
ADAM_STEP = 10
VMEM_CAP_MB = 60


def _cp(vmem_mb, sem=None):
    return pltpu.CompilerParams(vmem_limit_bytes=min(vmem_mb, VMEM_CAP_MB) * 2**20, dimension_semantics=sem)


def _row_tile(tp, want=384):
    return want if tp % want == 0 else 128


def _resident(shape):
    nd = len(shape)
    return pl.BlockSpec(shape, lambda *_: (0,) * nd, pipeline_mode=pl.Buffered(1))


def _skip(nd, body):
    return (lambda *refs: body(*refs[nd:])) if nd else body


def _dot_nn(a, b):
    return lax.dot_general(a, b, (((1,), (0,)), ((), ())), preferred_element_type=f32)


def _dot_nt(a, b):
    return lax.dot_general(a, b, (((1,), (1,)), ((), ())), preferred_element_type=f32)


def _dot_tn(a, b):
    return lax.dot_general(a, b, (((0,), (0,)), ((), ())), preferred_element_type=f32)


def _rms(h):
    rs = lax.rsqrt(jnp.mean(h * h, axis=-1, keepdims=True) + EPS)
    return h * rs, rs


def _rms_bwd(dn, g, hhat, rs):
    dhh = dn * g
    return rs * (dhh - hhat * jnp.mean(dhh * hhat, axis=-1, keepdims=True))


def _sigmoid(x):
    return jax.nn.sigmoid(x)


def _me():
    return lax.axis_index("x"), lax.axis_index("y"), lax.axis_index("c")


def _peer(idx):
    return (idx // 4, (idx // 2) % 2, idx % 2)


def all_gather(name, arrays):
    n = len(arrays)

    def body(*refs):
        ins, outs = refs[:n], refs[n:2 * n]
        send_sems, recv_sems, local_sems = refs[2 * n:]
        x, y, c = _me()
        me = 4 * x + 2 * y + c
        locals_ = []
        for k in range(n):
            cp = pltpu.make_async_copy(ins[k], outs[k].at[me], local_sems.at[k])
            cp.start()
            locals_.append(cp)
        for d in range(1, NDEV):
            for k in range(n):
                pltpu.make_async_remote_copy(
                    src_ref=ins[k], dst_ref=outs[k].at[me], send_sem=send_sems.at[k], recv_sem=recv_sems.at[k],
                    device_id=_peer((me + d) % NDEV), device_id_type=MESH).start()
        for k in range(n):
            seven = outs[k].at[pl.ds(0, NDEV - 1)]
            w = pltpu.make_async_remote_copy(src_ref=seven, dst_ref=seven, send_sem=send_sems.at[k],
                                             recv_sem=recv_sems.at[k], device_id=(x, y, c), device_id_type=MESH)
            w.wait_send()
            w.wait_recv()
            locals_[k].wait()

    anyspec = pl.BlockSpec(memory_space=pl.ANY)
    return pl.pallas_call(
        body, name=name,
        out_shape=[jax.ShapeDtypeStruct((NDEV,) + a.shape, a.dtype) for a in arrays],
        in_specs=[anyspec] * n, out_specs=[anyspec] * n,
        scratch_shapes=[pltpu.SemaphoreType.DMA((n,)), pltpu.SemaphoreType.DMA((n,)), pltpu.SemaphoreType.DMA((n,))],
    )(*arrays)


_HBM = pl.BlockSpec(memory_space=pltpu.HBM)
_SEM = pl.BlockSpec(memory_space=pltpu.SEMAPHORE)
_ANY = pl.BlockSpec(memory_space=pl.ANY)
_EFFECT = pltpu.SideEffectType.DATAFLOW_SIDE_EFFECTING


def _in_hbm(a):
    return pltpu.with_memory_space_constraint(a, pltpu.HBM)


def gather_start(name, lands, deps=()):
    n, nd = len(lands), len(deps)

    def body(*refs):
        land = refs[nd:nd + n]
        send_sems, recv_sems = refs[nd + n:nd + n + 2]
        token = refs[-1]
        x, y, c = _me()
        me = 4 * x + 2 * y + c
        for d in range(1, NDEV):
            for k in range(n):
                pltpu.make_async_remote_copy(
                    src_ref=land[k].at[me], dst_ref=land[k].at[me], send_sem=send_sems.at[k], recv_sem=recv_sems.at[k],
                    device_id=_peer((me + d) % NDEV), device_id_type=MESH).start()
        token[...] = jnp.zeros_like(token)

    res = pl.pallas_call(
        body, name=name,
        out_shape=(pltpu.SemaphoreType.DMA((n,)), pltpu.SemaphoreType.DMA((n,)),
                   *[pltpu.HBM(a.shape, a.dtype) for a in lands], jax.ShapeDtypeStruct((8, 128), f32)),
        in_specs=[_ANY] * nd + [_HBM] * n,
        out_specs=(_SEM, _SEM, *[_HBM] * n, pl.BlockSpec(memory_space=pltpu.VMEM)),
        input_output_aliases={nd + k: 2 + k for k in range(n)},
        compiler_params=pltpu.CompilerParams(has_side_effects=_EFFECT),
    )(*deps, *[_in_hbm(a) for a in lands])
    return res[0], res[1], list(res[2:2 + n]), res[-1]


def _other_chips(x, y):
    return [(1 - x, y), (x, 1 - y), (1 - x, 1 - y)]


def gather_start_chips(name, groups, deps=()):
    sizes = [len(g) for g in groups]
    lands = [a for g in groups for a in g]
    n, nd, ng = len(lands), len(deps), len(groups)

    def body(*refs):
        land = refs[nd:nd + n]
        sems = refs[nd + n:nd + n + 2 * ng]
        token = refs[-1]
        x, y, c = _me()
        me = 4 * x + 2 * y + c
        k = 0
        for g, size in enumerate(sizes):
            for j in range(size):
                for to in [(x, y, 1 - c)] + [(cx, cy, c) for cx, cy in _other_chips(x, y)]:
                    pltpu.make_async_remote_copy(
                        src_ref=land[k].at[me], dst_ref=land[k].at[me], send_sem=sems[2 * g].at[j],
                        recv_sem=sems[2 * g + 1].at[j], device_id=to, device_id_type=MESH).start()
                k += 1
        token[...] = jnp.zeros_like(token)

    res = pl.pallas_call(
        body, name=name,
        out_shape=(*[pltpu.SemaphoreType.DMA((size,)) for size in sizes for _ in range(2)],
                   *[pltpu.HBM(a.shape, a.dtype) for a in lands], jax.ShapeDtypeStruct((8, 128), f32)),
        in_specs=[_ANY] * nd + [_HBM] * n,
        out_specs=(*[_SEM] * (2 * ng), *[_HBM] * n, pl.BlockSpec(memory_space=pltpu.VMEM)),
        input_output_aliases={nd + k: 2 * ng + k for k in range(n)},
        compiler_params=pltpu.CompilerParams(has_side_effects=_EFFECT),
    )(*deps, *[_in_hbm(a) for a in lands])
    out, k = [], 2 * ng
    for g, size in enumerate(sizes):
        out.append((res[2 * g], res[2 * g + 1], list(res[k:k + size])))
        k += size
    return out, res[-1]


def gather_forward(name, send_sems, recv_sems, lands, after):
    n = len(lands)

    def body(*refs):
        land = refs[:n]
        ssem, rsem = refs[n:n + 2]
        send2, recv2 = refs[n + 3:n + 5]
        token = refs[-1]
        x, y, c = _me()
        for k in range(n):
            four = land[k].at[pl.ds(0, 4)]
            w = pltpu.make_async_remote_copy(src_ref=four, dst_ref=four, send_sem=ssem.at[k], recv_sem=rsem.at[k],
                                             device_id=(x, y, c), device_id_type=MESH)
            w.wait_send()
            w.wait_recv()
            for cx, cy in _other_chips(x, y):
                slot = 4 * cx + 2 * cy + c
                pltpu.make_async_remote_copy(
                    src_ref=land[k].at[slot], dst_ref=land[k].at[slot], send_sem=send2.at[k], recv_sem=recv2.at[k],
                    device_id=(x, y, 1 - c), device_id_type=MESH).start()
        token[...] = jnp.zeros_like(token)

    res = pl.pallas_call(
        body, name=name,
        out_shape=(pltpu.SemaphoreType.DMA((n,)), pltpu.SemaphoreType.DMA((n,)),
                   *[pltpu.HBM(a.shape, a.dtype) for a in lands], jax.ShapeDtypeStruct((8, 128), f32)),
        in_specs=[_HBM] * n + [_SEM, _SEM, _ANY],
        out_specs=(_SEM, _SEM, *[_HBM] * n, pl.BlockSpec(memory_space=pltpu.VMEM)),
        input_output_aliases={k: 2 + k for k in range(n)},
        compiler_params=pltpu.CompilerParams(has_side_effects=_EFFECT),
    )(*lands, send_sems, recv_sems, after)
    return res[0], res[1], list(res[2:2 + n]), res[-1]


def exchange_start(name, grads, deps=()):
    n, nd = len(grads), len(deps)
    lands = [lax.empty((NDEV - 1, g.shape[0]) + g.shape[2:], g.dtype) for g in grads]

    def body(*refs):
        src = refs[nd:nd + n]
        land = refs[nd + n:nd + 2 * n]
        send_sems, recv_sems = refs[nd + 2 * n:nd + 2 * n + 2]
        token = refs[-1]
        x, y, c = _me()
        me = 4 * x + 2 * y + c
        for d in range(1, NDEV):
            p = (me + d) % NDEV
            for k in range(n):
                pltpu.make_async_remote_copy(
                    src_ref=src[k].at[:, p], dst_ref=land[k].at[d - 1], send_sem=send_sems.at[k], recv_sem=recv_sems.at[k],
                    device_id=_peer(p), device_id_type=MESH).start()
        token[...] = jnp.zeros_like(token)

    both = list(grads) + lands
    res = pl.pallas_call(
        body, name=name,
        out_shape=(pltpu.SemaphoreType.DMA((n,)), pltpu.SemaphoreType.DMA((n,)),
                   *[pltpu.HBM(a.shape, a.dtype) for a in both], jax.ShapeDtypeStruct((8, 128), f32)),
        in_specs=[_ANY] * nd + [_HBM] * (2 * n),
        out_specs=(_SEM, _SEM, *[_HBM] * (2 * n), pl.BlockSpec(memory_space=pltpu.VMEM)),
        input_output_aliases={nd + k: 2 + k for k in range(2 * n)},
        compiler_params=pltpu.CompilerParams(has_side_effects=_EFFECT),
    )(*deps, *[_in_hbm(a) for a in both])
    return res[0], res[1], list(res[2:2 + n]), list(res[2 + n:2 + 2 * n]), res[-1]


def copies_wait(name, send_sems, recv_sems, sent, lands, after, count=NDEV - 1):
    ns, n = len(sent), len(lands)

    def body(*refs):
        land = refs[ns:ns + n]
        ssem, rsem = refs[ns + n:ns + n + 2]
        x, y, c = _me()
        for k in range(n):
            seven = land[k].at[pl.ds(0, count)]
            w = pltpu.make_async_remote_copy(src_ref=seven, dst_ref=seven, send_sem=ssem.at[k], recv_sem=rsem.at[k],
                                             device_id=(x, y, c), device_id_type=MESH)
            w.wait_send()
            w.wait_recv()

    both = list(sent) + list(lands)
    res = pl.pallas_call(
        body, name=name, out_shape=tuple(pltpu.HBM(a.shape, a.dtype) for a in both),
        in_specs=[_HBM] * (ns + n) + [_SEM, _SEM, _ANY], out_specs=tuple([_HBM] * (ns + n)),
        input_output_aliases={k: k for k in range(ns + n)},
        compiler_params=pltpu.CompilerParams(has_side_effects=_EFFECT),
    )(*both, send_sems, recv_sems, after)
    return list(res[:ns]), list(res[ns:])


def prep_layer(layer, me, col_sharded, row_sharded, deps=()):
    nc, nr = len(col_sharded), len(row_sharded)

    def body(me_ref, *refs):
        ins, outs = refs[:nc + nr], refs[nc + nr + len(deps):]
        for k in range(nc):
            outs[k][...] = ins[k][...].T.astype(bf16)
        for k in range(nc, nc + nr):
            outs[k][...] = ins[k][...].astype(bf16)

    arrs = list(col_sharded) + list(row_sharded)
    in_specs = [pl.BlockSpec((None,) + a.shape[1:], lambda i, me_ref: (layer, 0, 0)) for a in arrs]
    shapes = [(a.shape[2], a.shape[1]) for a in col_sharded] + [a.shape[1:] for a in row_sharded]
    out_specs = [pl.BlockSpec((None,) + s, lambda i, me_ref: (me_ref[0], 0, 0)) for s in shapes]
    return pl.pallas_call(
        body, name=f"prep_layer{layer}",
        grid_spec=pltpu.PrefetchScalarGridSpec(num_scalar_prefetch=1, grid=(1,), in_specs=in_specs + [_ANY] * len(deps),
                                               out_specs=out_specs),
        out_shape=[jax.ShapeDtypeStruct((NDEV,) + s, bf16) for s in shapes], compiler_params=_cp(48))(me, *arrs, *deps)


def slot_in(name, me, arrays):
    n = len(arrays)

    def body(me_ref, *refs):
        for k in range(n):
            refs[n + k][...] = refs[k][...]

    in_specs = [pl.BlockSpec(a.shape, lambda i, me_ref: (0, 0)) for a in arrays]
    out_specs = [pl.BlockSpec((None,) + a.shape, lambda i, me_ref: (me_ref[0], 0, 0)) for a in arrays]
    return pl.pallas_call(
        body, name=name,
        grid_spec=pltpu.PrefetchScalarGridSpec(num_scalar_prefetch=1, grid=(1,), in_specs=in_specs, out_specs=out_specs),
        out_shape=[jax.ShapeDtypeStruct((NDEV,) + a.shape, a.dtype) for a in arrays])(me, *arrays)


def _ff_chunks(ff, want=768):
    if ff % 256:
        return [slice(0, ff)]
    return [slice(c, min(c + want, ff)) for c in range(0, ff, want)]


def ffn_fwd(name, h, g, wgT, wuT, wd, deps=()):
    tp, d = h.shape
    ff = wgT.shape[0]
    tm = _row_tile(tp, 704)

    def body(h_ref, g_ref, wg_ref, wu_ref, wd_ref, ho_ref, a_ref, b_ref):
        hh = h_ref[...]
        hhat, _ = _rms(hh)
        n = (hhat * g_ref[...]).astype(bf16)
        acc = None
        for cols in _ff_chunks(ff):
            a = _dot_nt(n, wg_ref[cols, :])
            b = _dot_nt(n, wu_ref[cols, :])
            part = _dot_nn(((a * _sigmoid(a)) * b).astype(bf16), wd_ref[cols, :])
            acc = part if acc is None else acc + part
            a_ref[:, cols] = a.astype(bf16)
            b_ref[:, cols] = b.astype(bf16)
        ho_ref[...] = hh + 0.5 * acc

    row = lambda w: pl.BlockSpec((tm, w), lambda i: (i, 0))
    return pl.pallas_call(
        _skip(len(deps), body), name=name, grid=(tp // tm,),
        in_specs=[_ANY] * len(deps) + [row(d), _resident((1, d)), _resident((ff, d)), _resident((ff, d)), _resident((ff, d))],
        out_specs=[row(d), row(ff), row(ff)],
        out_shape=[jax.ShapeDtypeStruct((tp, d), f32), jax.ShapeDtypeStruct((tp, ff), bf16),
                   jax.ShapeDtypeStruct((tp, ff), bf16)],
        compiler_params=_cp(56, ("arbitrary",)))(*deps, h, g, wgT, wuT, wd)


def ffn_bwd(name, dy, h, g, a, b, wgT, wuT, wd, deps=()):
    tp, d = h.shape
    ff = wgT.shape[0]
    tm = _row_tile(tp, 384)

    def body(dy_ref, h_ref, g_ref, a_ref, b_ref, wg_ref, wu_ref, wd_ref, dh_ref, lhs_ref, rhs_ref, dg_ref):
        dyv = dy_ref[...]
        hhat, rs = _rms(h_ref[...])
        gv = g_ref[...]
        n = hhat * gv
        dyh = (0.5 * dyv).astype(bf16)
        dn = None
        for cols in _ff_chunks(ff):
            ds = _dot_nt(dyh, wd_ref[cols, :])
            av = a_ref[:, cols].astype(f32)
            bv = b_ref[:, cols].astype(f32)
            sg = _sigmoid(av)
            sa = av * sg
            da = (ds * bv * (sg * (1.0 + av * (1.0 - sg)))).astype(bf16)
            db = (ds * sa).astype(bf16)
            part = _dot_nn(da, wg_ref[cols, :]) + _dot_nn(db, wu_ref[cols, :])
            dn = part if dn is None else dn + part
            lhs_ref[0, :, cols] = da
            lhs_ref[1, :, cols] = db
            lhs_ref[2, :, cols] = (sa * bv).astype(bf16)
        dh_ref[...] = dyv + _rms_bwd(dn, gv, hhat, rs)

        @pl.when(pl.program_id(0) == 0)
        def _():
            dg_ref[...] = jnp.zeros_like(dg_ref)

        dg_ref[0:1, :] += jnp.sum(dn * hhat, axis=0, keepdims=True)
        rhs_ref[0] = n.astype(bf16)
        rhs_ref[1] = dyh

    row = lambda w: pl.BlockSpec((tm, w), lambda i: (i, 0))
    return pl.pallas_call(
        _skip(len(deps), body), name=name, grid=(tp // tm,),
        in_specs=[_ANY] * len(deps) + [row(d), row(d), _resident((1, d)), row(ff), row(ff),
                  _resident((ff, d)), _resident((ff, d)), _resident((ff, d))],
        out_specs=[row(d), pl.BlockSpec((3, tm, ff), lambda i: (0, i, 0)), pl.BlockSpec((2, tm, d), lambda i: (0, i, 0)),
                   pl.BlockSpec((8, d), lambda i: (0, 0))],
        out_shape=[jax.ShapeDtypeStruct((tp, d), f32), jax.ShapeDtypeStruct((3, tp, ff), bf16),
                   jax.ShapeDtypeStruct((2, tp, d), bf16), jax.ShapeDtypeStruct((8, d), f32)],
        compiler_params=_cp(58, ("arbitrary",)))(*deps, dy, h, g, a, b, wgT, wuT, wd)


def mm_tn(name, lhs, rhs, rhs_of, deps=(), only=None):
    _, tp, m = lhs.shape
    b0, nb = (0, lhs.shape[0]) if only is None else (only, 1)
    n = rhs.shape[2]
    def fits(t, ms):
        mb = m // ms
        return (tp % t == 0 and m % (128 * ms) == 0
                and 2 * t * (mb + n) * 2 + mb * n * (2 * 2 + 4 + (4 if t < tp else 0)) <= 54 * 2**20)

    tk, msplit = next(((t, ms) for t in (tp, 1408, 704, 384) for ms in (1, 2, 4) if fits(t, ms)), (128, 1))
    nk = tp // tk
    mb = m // msplit

    def body(l_ref, r_ref, o_ref, acc_ref):
        if nk == 1:
            o_ref[...] = _dot_tn(l_ref[...], r_ref[...]).astype(o_ref.dtype)
            return
        k = pl.program_id(2)

        @pl.when(k == 0)
        def _():
            acc_ref[...] = jnp.zeros_like(acc_ref)

        acc_ref[...] += _dot_tn(l_ref[...], r_ref[...])

        @pl.when(k == nk - 1)
        def _():
            o_ref[...] = acc_ref[...].astype(o_ref.dtype)

    return pl.pallas_call(
        _skip(len(deps), body), name=name, grid=(nb, msplit, nk),
        in_specs=[_ANY] * len(deps) + [pl.BlockSpec((None, tk, mb), lambda b, j, k: (b0 + b, k, j)),
                                       pl.BlockSpec((None, tk, n), lambda b, j, k: (rhs_of(b0 + b), k, 0))],
        out_specs=pl.BlockSpec((None, mb, n), lambda b, j, k: (b, j, 0)),
        out_shape=jax.ShapeDtypeStruct((nb, m, n), bf16),
        scratch_shapes=[pltpu.VMEM((mb, n) if nk > 1 else (8, 128), f32)],
        compiler_params=_cp(60, ("arbitrary", "arbitrary", "arbitrary")))(*deps, lhs, rhs)


def mix_in_fwd(name, h, g, winT, cos2, sin2):
    tp, d = h.shape
    nin = winT.shape[0]
    tm = _row_tile(tp)

    def body(h_ref, g_ref, w_ref, cos_ref, sin_ref, zq_ref, zg_ref, zu_ref, zgate_ref):
        hhat, _ = _rms(h_ref[...])
        z = _dot_nt((hhat * g_ref[...]).astype(bf16), w_ref[...])
        cosv, sinv = cos_ref[...], sin_ref[...]
        for hh in range(HEADS):
            qcols, kcols = slice(hh * HD, (hh + 1) * HD), slice(RW + hh * HD, RW + (hh + 1) * HD)
            zq_ref[:, qcols] = (_rot(z[:, qcols], cosv, sinv) * HD ** -0.5).astype(bf16)
            zq_ref[:, kcols] = _rot(z[:, kcols], cosv, sinv).astype(bf16)
        zq_ref[:, 2 * RW:] = z[:, 2 * RW:3 * RW].astype(bf16)
        zg_ref[...] = z[:, 3 * RW:4 * RW].astype(zg_ref.dtype)
        zu_ref[...] = z[:, 4 * RW:5 * RW]
        zgate_ref[...] = z[:, 5 * RW:].astype(zgate_ref.dtype)

    row = lambda w: pl.BlockSpec((tm, w), lambda i: (i, 0))
    widths = (3 * RW, RW, RW, 2 * d)
    return pl.pallas_call(
        body, name=name, grid=(tp // tm,),
        in_specs=[row(d), _resident((1, d)), _resident((nin, d)), row(HD), row(HD)],
        out_specs=[row(w) for w in widths],
        out_shape=[jax.ShapeDtypeStruct((tp, w), dt) for w, dt in zip(widths, (bf16, bf16, f32, bf16))],
        compiler_params=_cp(56, ("arbitrary",)))(h, g, winT, cos2, sin2)


def mix_in_bwd(name, dq, dk, dv, dzg, dzu, dzgate, h, g, winT, dres):
    tp, d = h.shape
    nin = winT.shape[0]
    tm = _row_tile(tp)

    def body(dq_ref, dk_ref, dv_ref, dzg_ref, dzu_ref, dzgate_ref, h_ref, g_ref, w_ref, dres_ref, dh_ref, dz_ref, n_ref, dg_ref):
        dn, col = None, 0
        for piece in (dq_ref, dk_ref, dv_ref, dzg_ref, dzu_ref, dzgate_ref):
            v = piece[...]
            part = _dot_nn(v, w_ref[col:col + v.shape[1], :])
            dn = part if dn is None else dn + part
            dz_ref[:, col:col + v.shape[1]] = v
            col += v.shape[1]
        hhat, rs = _rms(h_ref[...])
        gv = g_ref[...]
        dh_ref[...] = dres_ref[...] + _rms_bwd(dn, gv, hhat, rs)

        @pl.when(pl.program_id(0) == 0)
        def _():
            dg_ref[...] = jnp.zeros_like(dg_ref)

        dg_ref[0:1, :] += jnp.sum(dn * hhat, axis=0, keepdims=True)
        n_ref[...] = (hhat * gv).astype(bf16)

    row = lambda w: pl.BlockSpec((tm, w), lambda i: (i, 0))
    return pl.pallas_call(
        body, name=name, grid=(tp // tm,),
        in_specs=[row(RW)] * 5 + [row(2 * d), row(d), _resident((1, d)), _resident((nin, d)), row(d)],
        out_specs=[row(d), pl.BlockSpec((None, tm, nin), lambda i: (0, i, 0)), pl.BlockSpec((None, tm, d), lambda i: (0, i, 0)),
                   pl.BlockSpec((8, d), lambda i: (0, 0))],
        out_shape=[jax.ShapeDtypeStruct((tp, d), f32), jax.ShapeDtypeStruct((1, tp, nin), bf16),
                   jax.ShapeDtypeStruct((1, tp, d), bf16), jax.ShapeDtypeStruct((8, d), f32)],
        compiler_params=_cp(56, ("arbitrary",)))(dq, dk, dv, dzg, dzu, dzgate, h, g, winT, dres)


def _retention_tables(tp, pad):
    half = HD // 2
    inv_freq = ROPE_BASE ** (-jnp.arange(half, dtype=f32) / half)
    pos = jnp.arange(tp, dtype=f32) - pad
    ang = pos[:, None] * inv_freq[None, :]
    cos, sin = jnp.cos(ang), jnp.sin(ang)
    cos2 = jnp.concatenate([cos, cos], axis=-1)
    sin2 = jnp.concatenate([-sin, sin], axis=-1)
    log_gamma = jnp.log1p(-(2.0 ** (-5.0 - jnp.arange(HEADS, dtype=f32))))
    idx = jnp.arange(CHUNK, dtype=f32)
    diff = idx[:, None] - idx[None, :]
    intra = jnp.where(diff[None] >= 0, jnp.exp(diff[None] * log_gamma[:, None, None]), 0.0)
    k_decay = jnp.exp((CHUNK - 1.0 - idx)[None, :] * log_gamma[:, None])
    q_decay = jnp.exp((idx + 1.0)[None, :] * log_gamma[:, None])
    chunk_decay = jnp.exp(CHUNK * log_gamma)
    full = (HEADS, CHUNK, HD)
    dec = jnp.stack([intra, jnp.broadcast_to(k_decay[:, :, None], full), jnp.broadcast_to(q_decay[:, :, None], full),
                     jnp.broadcast_to(chunk_decay[:, None, None], full)], axis=1)
    return cos2, sin2, dec


def _rot(t, cos2, sin2):
    return t * cos2 + pltpu.roll(t, HD // 2, 1) * sin2


def _rot_t(t, cos2, sin2):
    return t * cos2 - pltpu.roll(t, HD // 2, 1) * sin2


def _chunks_per_step(nch):
    return 3 if nch % 3 == 0 else 1


def retention_fwd(name, zq, dec):
    tp = zq.shape[0]
    nch = tp // CHUNK
    per = _chunks_per_step(nch)

    def body(q_ref, k_ref, v_ref, dec_ref, out_ref, st_ref, s_ref):
        @pl.when(pl.program_id(0) == 0)
        def _():
            s_ref[...] = jnp.zeros_like(s_ref)

        state = [s_ref[hh] for hh in range(HEADS)]
        for j in range(per):
            rows = slice(j * CHUNK, (j + 1) * CHUNK)
            for hh in range(HEADS):
                cols = slice(hh * HD, (hh + 1) * HD)
                qb, kb, vb = q_ref[rows, cols], k_ref[rows, cols], v_ref[rows, cols]
                sc = (_dot_nt(qb, kb) * dec_ref[hh, 0]).astype(bf16)
                sb = state[hh].astype(bf16)
                cross = _dot_nn((qb.astype(f32) * dec_ref[hh, 2]).astype(bf16), sb)
                out_ref[rows, cols] = (_dot_nn(sc, vb) + cross).astype(out_ref.dtype)
                st_ref[hh, j] = sb
                state[hh] = state[hh] * dec_ref[hh, 3] + _dot_tn((kb.astype(f32) * dec_ref[hh, 1]).astype(bf16), vb)
        for hh in range(HEADS):
            s_ref[hh] = state[hh]

    part = lambda j: pl.BlockSpec((per * CHUNK, RW), lambda n: (n, j))
    return pl.pallas_call(
        body, name=name, grid=(nch // per,),
        in_specs=[part(0), part(1), part(2), _resident((HEADS, 4, CHUNK, HD))],
        out_specs=[part(0), pl.BlockSpec((HEADS, per, HD, HD), lambda n: (0, n, 0, 0))],
        out_shape=[jax.ShapeDtypeStruct((tp, RW), bf16), jax.ShapeDtypeStruct((HEADS, nch, HD, HD), bf16)],
        scratch_shapes=[pltpu.VMEM((HEADS, HD, HD), f32)],
        compiler_params=_cp(32, ("arbitrary",)))(zq, zq, zq, dec)


def retention_bwd(name, zq, cos2, sin2, dec, states, dout, pad, deps=()):
    tp = zq.shape[0]
    nch = tp // CHUNK
    per = _chunks_per_step(nch)
    nblk = nch // per
    scale = HD ** -0.5

    def body(q_ref, k_ref, v_ref, cos_ref, sin_ref, dec_ref, st_ref, do_ref, dq_ref, dk_ref, dv_ref, g_ref):
        @pl.when(pl.program_id(0) == 0)
        def _():
            g_ref[...] = jnp.zeros_like(g_ref)

        first_row = (nblk - 1 - pl.program_id(0)) * (per * CHUNK)
        gstate = [g_ref[hh] for hh in range(HEADS)]
        for j in reversed(range(per)):
            rows = slice(j * CHUNK, (j + 1) * CHUNK)
            cosv, sinv = cos_ref[rows, :], sin_ref[rows, :]
            keep = (lax.broadcasted_iota(jnp.int32, (CHUNK, HD), 0) + (first_row + j * CHUNK)) >= pad
            for hh in range(HEADS):
                cols = slice(hh * HD, (hh + 1) * HD)
                intra, kdec, qdec = dec_ref[hh, 0], dec_ref[hh, 1], dec_ref[hh, 2]
                qb, kb, vb = q_ref[rows, cols], k_ref[rows, cols], v_ref[rows, cols]
                qd = (qb.astype(f32) * qdec).astype(bf16)
                kd = (kb.astype(f32) * kdec).astype(bf16)
                sc = (_dot_nt(qb, kb) * intra).astype(bf16)
                dob = do_ref[rows, cols]
                sb = st_ref[hh, j]
                gb = gstate[hh].astype(bf16)
                dsc = (_dot_nt(dob, vb) * intra).astype(bf16)
                dv = _dot_tn(sc, dob) + _dot_nn(kd, gb)
                dqr = _dot_nn(dsc, kb) + _dot_nt(dob, sb) * qdec
                dkr = _dot_tn(dsc, qb) + _dot_nt(vb, gb) * kdec
                gstate[hh] = gstate[hh] * dec_ref[hh, 3] + _dot_tn(qd, dob)
                dq_ref[rows, cols] = jnp.where(keep, _rot_t(dqr * scale, cosv, sinv), 0.0).astype(bf16)
                dk_ref[rows, cols] = jnp.where(keep, _rot_t(dkr, cosv, sinv), 0.0).astype(bf16)
                dv_ref[rows, cols] = jnp.where(keep, dv, 0.0).astype(bf16)
        for hh in range(HEADS):
            g_ref[hh] = gstate[hh]

    part = lambda j: pl.BlockSpec((per * CHUNK, RW), lambda t: (nblk - 1 - t, j))
    table = pl.BlockSpec((per * CHUNK, HD), lambda t: (nblk - 1 - t, 0))
    return pl.pallas_call(
        _skip(len(deps), body), name=name, grid=(nblk,),
        in_specs=[_ANY] * len(deps) + [part(0), part(1), part(2), table, table, _resident((HEADS, 4, CHUNK, HD)),
                                       pl.BlockSpec((HEADS, per, HD, HD), lambda t: (0, nblk - 1 - t, 0, 0)), part(0)],
        out_specs=[part(0)] * 3,
        out_shape=[jax.ShapeDtypeStruct((tp, RW), bf16)] * 3,
        scratch_shapes=[pltpu.VMEM((HEADS, HD, HD), f32)],
        compiler_params=_cp(32, ("arbitrary",)))(*deps, zq, zq, zq, cos2, sin2, dec, states, dout)


def _window_sum(xv, steps, tp, forward):
    s = xv
    for j in range(steps):
        sh = 2 ** j
        s = s + pltpu.roll(s, (tp - sh) if forward else sh, 0)
    return s


def pool_fwd(name, zu, maps, scale, pad):
    tp = zu.shape[0]

    def body(u_ref, maps_ref, scale_ref, pooled_ref, p_ref):
        row = lax.broadcasted_iota(jnp.int32, (tp, HD), 0)
        for gi, w in enumerate(POOL_WINDOWS):
            cols = slice(gi * HD, (gi + 1) * HD)
            xv = u_ref[:, cols]
            cnt = jnp.clip(row - (pad - 1), 1, w).astype(f32)
            pooled = jnp.where(row >= pad, _window_sum(xv, gi + 1, tp, False) / cnt - xv, 0.0).astype(bf16)
            pooled_ref[:, cols] = pooled
            p_ref[:, cols] = (_dot_nn(pooled, maps_ref[gi].astype(bf16)) * scale_ref[:, cols]).astype(bf16)

    return pl.pallas_call(
        body, name=name,
        out_shape=[jax.ShapeDtypeStruct((tp, RW), bf16), jax.ShapeDtypeStruct((tp, RW), bf16)],
        compiler_params=_cp(56))(zu, maps, scale)


def pool_bwd(name, dp, pooled, maps, scale, pad):
    tp = dp.shape[0]

    def body(dp_ref, pooled_ref, maps_ref, scale_ref, du_ref, dmaps_ref, dscale_ref):
        row = lax.broadcasted_iota(jnp.int32, (tp, HD), 0)
        dscale_ref[...] = jnp.zeros_like(dscale_ref)
        for gi, w in enumerate(POOL_WINDOWS):
            cols = slice(gi * HD, (gi + 1) * HD)
            mb = maps_ref[gi].astype(bf16)
            pooled = pooled_ref[:, cols]
            dpf = dp_ref[:, cols].astype(f32)
            dscale_ref[0:1, cols] = jnp.sum(dpf * _dot_nn(pooled, mb), axis=0, keepdims=True)
            dpm = (dpf * scale_ref[:, cols]).astype(bf16)
            dmaps_ref[gi * HD:(gi + 1) * HD, :] = _dot_tn(pooled, dpm)
            dpool = jnp.where(row >= pad, _dot_nt(dpm, mb), 0.0)
            cnt = jnp.clip(row - (pad - 1), 1, w).astype(f32)
            du = _window_sum(dpool / cnt, gi + 1, tp, True) - dpool
            du_ref[:, cols] = jnp.where(row >= pad, du, 0.0).astype(bf16)

    return pl.pallas_call(
        body, name=name,
        out_shape=[jax.ShapeDtypeStruct((tp, RW), bf16), jax.ShapeDtypeStruct((HEADS * HD, HD), f32),
                   jax.ShapeDtypeStruct((8, RW), f32)],
        compiler_params=_cp(56))(dp, pooled, maps, scale)


def _group_norm(o):
    mu = jnp.mean(o, axis=-1, keepdims=True)
    oc = o - mu
    rstd = lax.rsqrt(jnp.mean(oc * oc, axis=-1, keepdims=True) + EPS)
    return oc * rstd, rstd


def mix_out_fwd(name, h, oraw, zg, zgate, p, wretT, wpoolT, wout, deps=()):
    tp, d = h.shape
    tm = _row_tile(tp)

    def body(h_ref, o_ref, zg_ref, zgate_ref, p_ref, wr_ref, wp_ref, wo_ref, ho_ref, rp_ref, mixed_ref):
        parts = []
        for hh in range(HEADS):
            cols = slice(hh * HD, (hh + 1) * HD)
            rhat, _ = _group_norm(o_ref[:, cols].astype(f32))
            gv = zg_ref[:, cols].astype(f32)
            parts.append(rhat * (gv * _sigmoid(gv)))
        r = jnp.concatenate(parts, axis=-1).astype(bf16)
        pv = p_ref[...]
        ret = _dot_nt(r, wr_ref[...])
        pool = _dot_nt(pv, wp_ref[...])
        mixed = (_sigmoid(zgate_ref[:, :d].astype(f32)) * ret + _sigmoid(zgate_ref[:, d:].astype(f32)) * pool).astype(bf16)
        ho_ref[...] = h_ref[...] + _dot_nn(mixed, wo_ref[...])
        rp_ref[0] = r
        rp_ref[1] = pv
        mixed_ref[...] = mixed

    row = lambda w: pl.BlockSpec((tm, w), lambda i: (i, 0))
    return pl.pallas_call(
        _skip(len(deps), body), name=name, grid=(tp // tm,),
        in_specs=[_ANY] * len(deps) + [row(d), row(RW), row(RW), row(2 * d), row(RW), _resident((d, RW)), _resident((d, RW)),
                                       _resident((d, d))],
        out_specs=[row(d), pl.BlockSpec((2, tm, RW), lambda i: (0, i, 0)), pl.BlockSpec((None, tm, d), lambda i: (0, i, 0))],
        out_shape=[jax.ShapeDtypeStruct((tp, d), f32), jax.ShapeDtypeStruct((2, tp, RW), bf16),
                   jax.ShapeDtypeStruct((1, tp, d), bf16)],
        compiler_params=_cp(48, ("arbitrary",)))(*deps, h, oraw, zg, zgate, p, wretT, wpoolT, wout)


def mix_out_bwd(name, dy, oraw, zg, zgate, rp, wretT, wpoolT, wout, deps=()):
    tp, d = dy.shape
    tm = _row_tile(tp)

    def body(dy_ref, o_ref, zg_ref, zgate_ref, rp_ref, wr_ref, wp_ref, wo_ref,
             do_ref, dzg_ref, dzgate_ref, dp_ref, drp_ref, dyb_ref):
        dyb = dy_ref[...].astype(bf16)
        dmixed = _dot_nt(dyb, wo_ref[...])
        sa = _sigmoid(zgate_ref[:, :d].astype(f32))
        sb = _sigmoid(zgate_ref[:, d:].astype(f32))
        dret = dmixed * sa
        dpool = dmixed * sb
        dzgate_ref[:, :d] = (dret * _dot_nt(rp_ref[0], wr_ref[...]) * (1.0 - sa)).astype(bf16)
        dzgate_ref[:, d:] = (dpool * _dot_nt(rp_ref[1], wp_ref[...]) * (1.0 - sb)).astype(bf16)
        dretb, dpoolb = dret.astype(bf16), dpool.astype(bf16)
        dr = _dot_nn(dretb, wr_ref[...])
        dp_ref[...] = _dot_nn(dpoolb, wp_ref[...]).astype(bf16)
        for hh in range(HEADS):
            cols = slice(hh * HD, (hh + 1) * HD)
            rhat, rstd = _group_norm(o_ref[:, cols].astype(f32))
            gv = zg_ref[:, cols].astype(f32)
            sg = _sigmoid(gv)
            drh = dr[:, cols]
            drhat = drh * (gv * sg)
            dzg_ref[:, cols] = (drh * rhat * (sg * (1.0 + gv * (1.0 - sg)))).astype(bf16)
            do = rstd * (drhat - jnp.mean(drhat, axis=-1, keepdims=True)
                         - rhat * jnp.mean(drhat * rhat, axis=-1, keepdims=True))
            do_ref[:, cols] = do.astype(bf16)
        drp_ref[0] = dretb
        drp_ref[1] = dpoolb
        dyb_ref[...] = dyb

    row = lambda w: pl.BlockSpec((tm, w), lambda i: (i, 0))
    return pl.pallas_call(
        _skip(len(deps), body), name=name, grid=(tp // tm,),
        in_specs=[_ANY] * len(deps) + [row(d), row(RW), row(RW), row(2 * d), pl.BlockSpec((2, tm, RW), lambda i: (0, i, 0)),
                                       _resident((d, RW)), _resident((d, RW)), _resident((d, d))],
        out_specs=[row(RW), row(RW), row(2 * d), row(RW), pl.BlockSpec((2, tm, d), lambda i: (0, i, 0)),
                   pl.BlockSpec((None, tm, d), lambda i: (0, i, 0))],
        out_shape=[jax.ShapeDtypeStruct((tp, RW), bf16), jax.ShapeDtypeStruct((tp, RW), bf16),
                   jax.ShapeDtypeStruct((tp, 2 * d), bf16), jax.ShapeDtypeStruct((tp, RW), bf16),
                   jax.ShapeDtypeStruct((2, tp, d), bf16), jax.ShapeDtypeStruct((1, tp, d), bf16)],
        compiler_params=_cp(48, ("arbitrary",)))(*deps, dy, oraw, zg, zgate, rp, wretT, wpoolT, wout)


def final_loss(name, h, g, target):
    tp, d = h.shape
    tm = _row_tile(tp)
    nsub = tm // CHUNK

    def body(h_ref, g_ref, *rest):
        t_refs = rest[:nsub]
        dh_ref, loss_ref, dg_ref = rest[nsub:]
        i = pl.program_id(0)

        @pl.when(i == 0)
        def _():
            loss_ref[...] = jnp.zeros_like(loss_ref)
            dg_ref[...] = jnp.zeros_like(dg_ref)

        gv = g_ref[...]
        for j in range(nsub):
            rows = slice(j * CHUNK, (j + 1) * CHUNK)
            hhat, rs = _rms(h_ref[rows, :])
            err = jnp.where(i * nsub + j >= 1, hhat * gv - t_refs[j][...], 0.0)
            dyv = err / d
            dh_ref[rows, :] = _rms_bwd(dyv, gv, hhat, rs)
            loss_ref[...] += 0.5 * jnp.sum(jnp.sum(err * err, axis=-1, keepdims=True) / d)
            dg_ref[0:1, :] += jnp.sum(dyv * hhat, axis=0, keepdims=True)

    lagged = lambda j: pl.BlockSpec((CHUNK, d), lambda i: (jnp.maximum(i * nsub + j - 1, 0), 0))
    return pl.pallas_call(
        body, name=name, grid=(tp // tm,),
        in_specs=[pl.BlockSpec((tm, d), lambda i: (i, 0)), _resident((1, d))] + [lagged(j) for j in range(nsub)],
        out_specs=[pl.BlockSpec((tm, d), lambda i: (i, 0)), pl.BlockSpec((8, 128), lambda i: (0, 0)),
                   pl.BlockSpec((8, d), lambda i: (0, 0))],
        out_shape=[jax.ShapeDtypeStruct((tp, d), f32), jax.ShapeDtypeStruct((8, 128), f32),
                   jax.ShapeDtypeStruct((8, d), f32)],
        compiler_params=_cp(32, ("arbitrary",)))(h, g, *[target] * nsub)


def _adamw(w, g, m, v):
    m = ADAM_B1 * m + (1.0 - ADAM_B1) * g
    v = ADAM_B2 * v + (1.0 - ADAM_B2) * (g * g)
    m_hat = m / (1.0 - ADAM_B1 ** ADAM_STEP)
    v_hat = v / (1.0 - ADAM_B2 ** ADAM_STEP)
    delta = -ADAM_LR * (m_hat / (jnp.sqrt(v_hat) + ADAM_EPS) + ADAM_WD * w)
    return delta, m, v


def adam_big(name, me, recv, own, b, layer, transposed, w, m, v, prev):
    r, c = recv.shape[2:]
    wshape = w.shape[1:]
    nchunk = 1 if transposed else next(k for k in (4, 2, 1) if r % (16 * k) == 0)
    rc = r // nchunk

    def body(me_ref, recv_ref, own_ref, w_ref, m_ref, v_ref, *rest):
        g_ref, d_ref, nm_ref, nv_ref = rest[-4:]
        g = own_ref[...].astype(f32)
        for j in range(NDEV - 1):
            g = g + recv_ref[j].astype(f32)
        if transposed:
            g = g.T
        delta, nm, nv = _adamw(w_ref[...], g, m_ref[...], v_ref[...])
        g_ref[...] = g
        d_ref[...] = delta
        nm_ref[...] = nm
        nv_ref[...] = nv

    wblock = wshape if transposed else (rc, c)
    wspec = pl.BlockSpec((None,) + wblock, lambda i, me_ref: (layer, i, 0))
    in_specs = [pl.BlockSpec((NDEV - 1, None, rc, c), lambda i, me_ref: (0, b, i, 0)),
                pl.BlockSpec((None, None, rc, c), lambda i, me_ref: (b, me_ref[0], i, 0)), wspec, wspec, wspec]
    args = [recv, own, w, m, v]
    aliases = {}
    if prev is not None:
        in_specs += [_ANY] * 4
        args += list(prev)
        aliases = {6 + k: k for k in range(4)}
    return pl.pallas_call(
        body, name=name,
        grid_spec=pltpu.PrefetchScalarGridSpec(num_scalar_prefetch=1, grid=(nchunk,), in_specs=in_specs,
                                               out_specs=[wspec] * 4),
        out_shape=[jax.ShapeDtypeStruct(w.shape, f32)] * 4, input_output_aliases=aliases,
        compiler_params=_cp(56))(me, *args)


def adam_small(name, ga0, gmaps0, gmeta, ga1, gmaps1, norms, pool_scale, pool_maps, meta, final_norm, d):
    def body(ga0_ref, gmaps0_ref, gmeta_ref, ga1_ref, gmaps1_ref, *refs):
        ins, outs = refs[:21], refs[21:]
        x, y, c = _me()
        me = 4 * x + 2 * y + c

        def total(ref, rows):
            t = ref[0, rows, :]
            for j in range(1, NDEV):
                t = t + ref[j, rows, :]
            return t

        row = lambda r: slice(r, r + 1)
        outs[0][...] = jnp.broadcast_to(total(ga1_ref, row(0))[:, :128], (8, 128))

        def update(k, g, o):
            w_ref, m_ref, v_ref = ins[3 * k:3 * k + 3]
            delta, nm, nv = _adamw(w_ref[...], g, m_ref[...], v_ref[...])
            for ref, val in zip(outs[o:o + 4], (g, delta, nm, nv)):
                ref[...] = val

        two = lax.broadcasted_iota(jnp.int32, (2, d), 0)
        for k in range(3):
            update(k, jnp.where(two == 0, total(ga0_ref, row(k)), total(ga1_ref, row(2 + k))), 1 + 4 * k)
        update(3, jnp.where(two[:, :RW] == 0, total(ga0_ref, row(3))[:, :RW], total(ga1_ref, row(5))[:, :RW]), 13)
        update(4, jnp.concatenate([total(gmaps0_ref, slice(None)), total(gmaps1_ref, slice(None))], axis=0), 17)
        update(5, total(gmeta_ref, pl.ds(pl.multiple_of(me * N_META, N_META), N_META)), 21)
        update(6, total(ga1_ref, row(1)), 25)

    flat = []
    for trip in (*norms, pool_scale, pool_maps, meta, final_norm):
        flat += list(trip)
    out_shapes = [jax.ShapeDtypeStruct((8, 128), f32)]
    for trip in (*norms, pool_scale, pool_maps, meta, final_norm):
        out_shapes += [jax.ShapeDtypeStruct(trip[0].shape, f32)] * 4
    return pl.pallas_call(body, name=name, out_shape=out_shapes,
                          compiler_params=_cp(32))(ga0, gmaps0, gmeta, ga1, gmaps1, *flat)


def kernel(x, meta, ffn1_norm, ffn1_gate, ffn1_up, ffn1_down, mix_norm, w_in, pool_maps, pool_scale, w_ret_up, w_pool_up, w_out, ffn2_norm, ffn2_gate, ffn2_up, ffn2_down, final_norm, loss_target, m_meta, m_ffn1_norm, m_ffn1_gate, m_ffn1_up, m_ffn1_down, m_mix_norm, m_w_in, m_pool_maps, m_pool_scale, m_w_ret_up, m_w_pool_up, m_w_out, m_ffn2_norm, m_ffn2_gate, m_ffn2_up, m_ffn2_down, m_final_norm, v_meta, v_ffn1_norm, v_ffn1_gate, v_ffn1_up, v_ffn1_down, v_mix_norm, v_w_in, v_pool_maps, v_pool_scale, v_w_ret_up, v_w_pool_up, v_w_out, v_ffn2_norm, v_ffn2_gate, v_ffn2_up, v_ffn2_down, v_final_norm):
    seq, d = x.shape[1], x.shape[2]
    depth = ffn1_gate.shape[0]
    ff = ffn1_gate.shape[2] * NDEV
    nin = w_in.shape[2] * NDEV
    length = seq + N_META
    pad = (-length) % CHUNK
    tp = length + pad
    assert pad % 8 == 0 and pad + N_META == CHUNK and depth == 2 and nin == 5 * RW + 2 * d

    ix, iy, ic = _me()
    me = (4 * ix + 2 * iy + ic).astype(jnp.int32).reshape(1)

    meta_all, = all_gather("gather_meta", [meta])
    meta_full = jnp.transpose(meta_all, (1, 0, 2)).reshape(N_META, d)

    token = meta_all
    tview = lambda *arrs: [jnp.swapaxes(a, 1, 2) for a in arrs]
    t_g1, t_u1, t_g2, t_u2, t_in = (tview(w, m, v) for w, m, v in (
        (ffn1_gate, m_ffn1_gate, v_ffn1_gate), (ffn1_up, m_ffn1_up, v_ffn1_up), (ffn2_gate, m_ffn2_gate, v_ffn2_gate),
        (ffn2_up, m_ffn2_up, v_ffn2_up), (w_in, m_w_in, v_w_in)))
    keys, groups = [], []
    for layer in range(depth):
        lands = prep_layer(layer, me, [w_ret_up, w_pool_up],
                           [t_g1[0], t_u1[0], t_g2[0], t_u2[0], t_in[0], ffn1_down, ffn2_down, w_out],
                           (token,) if layer else ())
        wretT, wpoolT, g1T, u1T, g2T, u2T, winT, d1, d2, wout = lands
        keys += [("ffn1", layer), ("mix", layer), ("ffn2", layer)]
        groups += [[g1T, u1T, d1], [winT, wretT, wpoolT, wout], [g2T, u2T, d2]]
        if layer == 0:
            first, token = gather_start_chips("gather_start_first", groups[:1], (token,))
    started, token = gather_start_chips("gather_start_rest", groups[1:], (token,))
    gathers = dict(zip(keys, first + started))

    def forward(part, layer, after):
        ssem, rsem, group = gathers[(part, layer)]
        ssem, rsem, group, tok = gather_forward(f"gather_forward_{part}{layer}", ssem, rsem, group, after)
        gathers[(part, layer)] = (ssem, rsem, group)
        return tok

    def gathered(part, layer, after):
        ssem, rsem, group = gathers[(part, layer)]
        _, full = copies_wait(f"gather_wait_{part}{layer}", ssem, rsem, (), group, after, 3)
        return [a.reshape((NDEV * a.shape[1],) + a.shape[2:]) for a in full]

    cos2, sin2, dec = _retention_tables(tp, pad)
    h = jnp.concatenate([jnp.zeros((pad, d), f32), meta_full + token[0, 0], x[0]], axis=0)

    saved = []
    weights = []
    tok = forward("ffn1", 0, token)
    for layer in range(depth):
        row = lambda a: a[layer:layer + 1]
        s = {"h0": h}
        g1T, u1T, d1 = gathered("ffn1", layer, tok if layer == 0 else h)
        tok = forward("mix", layer, h) if layer else None
        h, s["a1"], s["b1"] = ffn_fwd(f"ffn1_fwd{layer}", h, row(ffn1_norm), g1T, u1T, d1, (tok,) if layer else ())
        s["h1"] = h
        if layer == 0:
            tok = forward("mix", layer, h)
        winT, wretT, wpoolT, wout = gathered("mix", layer, tok if layer == 0 else h)
        s["zq"], s["zg"], zu, s["zgate"] = mix_in_fwd(f"mix_in_fwd{layer}", h, row(mix_norm), winT, cos2, sin2)
        s["oraw"], s["states"] = retention_fwd(f"retention_fwd{layer}", s["zq"], dec)
        s["pooled"], p = pool_fwd(f"pool_fwd{layer}", zu, pool_maps[layer], row(pool_scale), pad)
        tok = forward("ffn2", layer, p)
        h, s["rp"], s["mixed"] = mix_out_fwd(
            f"mix_out_fwd{layer}", h, s["oraw"], s["zg"], s["zgate"], p, wretT, wpoolT, wout, (tok,))
        s["h2"] = h
        g2T, u2T, d2 = gathered("ffn2", layer, h)
        tok = (forward("ffn1", layer + 1, h),) if layer + 1 < depth else ()
        h, s["a2"], s["b2"] = ffn_fwd(f"ffn2_fwd{layer}", h, row(ffn2_norm), g2T, u2T, d2, tok)
        saved.append(s)
        weights.append((g1T, u1T, g2T, u2T, winT, wretT, wpoolT, d1, d2, wout))

    dh, loss_part, dg_final = final_loss("final_loss", h, final_norm.reshape(1, d), loss_target[0])

    small = {}
    small_gathers = {}
    exchanges = {}
    token = None

    def rows8(vals):
        at = lax.broadcasted_iota(jnp.int32, (8, d), 0)
        out = jnp.zeros((8, d), f32)
        for k, v in enumerate(vals):
            r0 = v[0:1]
            r0 = r0 if r0.shape[1] == d else jnp.pad(r0, ((0, 0), (0, d - r0.shape[1])))
            out = jnp.where(at == k, r0, out)
        return out

    def exchange(part, layer, grads):
        by_dest = [g.reshape(g.shape[0], NDEV, g.shape[1] // NDEV, g.shape[2]) for g in grads]
        ssem, rsem, sent, lands, tok = exchange_start(f"exchange_start_{part}{layer}", by_dest)
        exchanges[(part, layer)] = (ssem, rsem, sent, lands)
        return (tok,)

    for layer in reversed(range(depth)):
        g1T, u1T, g2T, u2T, winT, wretT, wpoolT, d1, d2, wout = weights[layer]
        row = lambda a: a[layer:layer + 1]
        s = saved[layer]
        dh, lhs2, rhs2, small[("ffn2", layer)] = ffn_bwd(
            f"ffn2_bwd{layer}", dh, s["h2"], row(ffn2_norm), s["a2"], s["b2"], g2T, u2T, d2, () if token is None else token)
        token = exchange("ffn2", layer, [mm_tn(f"ffn2_wgrad{layer}", lhs2, rhs2, lambda b: b // 2)])
        do, dzg, dzgate, dp, drp, dyb = mix_out_bwd(
            f"mix_out_bwd{layer}", dh, s["oraw"], s["zg"], s["zgate"], s["rp"], wretT, wpoolT, wout, token)
        gw_mix = [mm_tn(f"w_out_wgrad{layer}", s["mixed"], dyb, lambda b: b),
                  mm_tn(f"up_wgrad{layer}", drp, s["rp"], lambda b: b)]
        dq, dk, dv = retention_bwd(f"retention_bwd{layer}", s["zq"], cos2, sin2, dec, s["states"], do, pad, token)
        dzu, small[("maps", layer)], small[("scale", layer)] = pool_bwd(
            f"pool_bwd{layer}", dp, s["pooled"], pool_maps[layer], row(pool_scale), pad)
        dh, dz, n2, small[("mix", layer)] = mix_in_bwd(
            f"mix_in_bwd{layer}", dq, dk, dv, dzg, dzu, dzgate, s["h1"], row(mix_norm), winT, dh)
        token = exchange("mix", layer, gw_mix + [mm_tn(f"w_in_wgrad{layer}", dz, n2, lambda b: b)])
        dh, lhs1, rhs1, small[("ffn1", layer)] = ffn_bwd(
            f"ffn1_bwd{layer}", dh, s["h0"], row(ffn1_norm), s["a1"], s["b1"], g1T, u1T, d1, token)
        rows = [small[("ffn1", layer)], small[("mix", layer)], small[("ffn2", layer)], small[("scale", layer)]]
        packs = [rows8([loss_part, dg_final] + rows if layer == depth - 1 else rows), small[("maps", layer)]]
        if layer == 0:
            dmeta = dh[pad:CHUNK]
            packs.append(jnp.transpose(dmeta.reshape(N_META, NDEV, d // NDEV), (1, 0, 2)).reshape(NDEV * N_META, d // NDEV))
        ssem, rsem, lands, tok = gather_start(f"small_start{layer}", slot_in(f"small_slot{layer}", me, packs))
        small_gathers[layer] = (ssem, rsem, lands)
        if layer:
            token = exchange("ffn1", layer, [mm_tn(f"ffn1_wgrad{layer}", lhs1, rhs1, lambda b: b // 2, (tok,))])
        else:
            token = (tok,)
            for j, nm in enumerate(("ffn1_gate", "ffn1_up", "ffn1_down")):
                token = exchange(nm, layer, [mm_tn(f"{nm}_wgrad{layer}", lhs1, rhs1, lambda b: b // 2, token, only=j)])

    grad_x = dh[CHUNK:][None]

    big = {}
    after = token[0]
    plans = {
        "ffn2": [("ffn2_gate", 0, 0, False, *t_g2), ("ffn2_up", 0, 1, False, *t_u2),
                 ("ffn2_down", 0, 2, False, ffn2_down, m_ffn2_down, v_ffn2_down)],
        "mix": [("w_out", 0, 0, False, w_out, m_w_out, v_w_out),
                ("w_ret_up", 1, 0, True, w_ret_up, m_w_ret_up, v_w_ret_up),
                ("w_pool_up", 1, 1, True, w_pool_up, m_w_pool_up, v_w_pool_up), ("w_in", 2, 0, False, *t_in)],
        "ffn1": [("ffn1_gate", 0, 0, False, *t_g1), ("ffn1_up", 0, 1, False, *t_u1),
                 ("ffn1_down", 0, 2, False, ffn1_down, m_ffn1_down, v_ffn1_down)]}
    for nm, k, b, tr, w, m, v in plans["ffn1"]:
        plans[nm] = [(nm, 0, 0, tr, w, m, v)]
    for layer in reversed(range(depth)):
        for part in ("ffn2", "mix") + (("ffn1",) if layer else ("ffn1_gate", "ffn1_up", "ffn1_down")):
            ssem, rsem, sent, lands = exchanges[(part, layer)]
            sent, lands = copies_wait(f"exchange_wait_{part}{layer}", ssem, rsem, sent, lands, after)
            for nm, k, b, tr, w, m, v in plans[part]:
                big[nm] = adam_big(f"adam_{nm}{layer}", me, lands[k], sent[k], b, layer, tr, w, m, v, big.get(nm))
                after = big[nm][0]

    gsmall = []
    for layer in range(depth):
        ssem, rsem, lands = small_gathers[layer]
        gsmall += copies_wait(f"small_wait{layer}", ssem, rsem, (), lands, after)[1]

    maps2 = lambda a: a.reshape(depth * HEADS * HD, HD)
    res = adam_small(
        "adam_small", *gsmall,
        [(ffn1_norm, m_ffn1_norm, v_ffn1_norm), (mix_norm, m_mix_norm, v_mix_norm), (ffn2_norm, m_ffn2_norm, v_ffn2_norm)],
        (pool_scale, m_pool_scale, v_pool_scale), (maps2(pool_maps), maps2(m_pool_maps), maps2(v_pool_maps)),
        (meta, m_meta, v_meta), tuple(a.reshape(1, d) for a in (final_norm, m_final_norm, v_final_norm)), d)
    loss = res[0][0, 0]
    sm = {}
    for k, nm in enumerate(["ffn1_norm", "mix_norm", "ffn2_norm", "pool_scale", "pool_maps", "meta", "final_norm"]):
        sm[nm] = list(res[1 + 4 * k:5 + 4 * k])
    sm["pool_maps"] = [a.reshape(pool_maps.shape) for a in sm["pool_maps"]]
    sm["final_norm"] = [a.reshape(d) for a in sm["final_norm"]]

    names = ["meta", "ffn1_norm", "ffn1_gate", "ffn1_up", "ffn1_down", "mix_norm", "w_in", "pool_maps", "pool_scale",
             "w_ret_up", "w_pool_up", "w_out", "ffn2_norm", "ffn2_gate", "ffn2_up", "ffn2_down", "final_norm"]
    for nm in ("ffn1_gate", "ffn1_up", "ffn2_gate", "ffn2_up", "w_in"):
        big[nm] = tview(*big[nm])
    allw = {**{k: list(v) for k, v in big.items()}, **sm}
    outs = [loss, grad_x]
    for kind in range(4):
        outs += [allw[nm][kind] for nm in names]
    return tuple(outs)
```

```python
import functools

import jax
import jax.numpy as jnp
from jax import lax
from jax.experimental import pallas as pl
from jax.experimental.pallas import tpu as pltpu

f32 = jnp.float32
bf16 = jnp.bfloat16
MESH = pl.DeviceIdType.MESH
NDEV = 8
N_META = 16
HEADS = 4
HD = 128
CHUNK = 128
RW = HEADS * HD
POOL_WINDOWS = (2, 4, 8, 16)
ROPE_BASE = 10000.0
EPS = 1e-6
ADAM_LR = 0.001
ADAM_B1 = 0.9
ADAM_B2 = 0.999
ADAM_EPS = 1e-08
ADAM_WD = 0.01
ADAM_STEP = 10
VMEM_CAP_MB = 60


def _cp(vmem_mb, sem=None):
    return pltpu.CompilerParams(vmem_limit_bytes=min(vmem_mb, VMEM_CAP_MB) * 2**20, dimension_semantics=sem)


def _row_tile(tp, want=384):
    return want if tp % want == 0 else 128


def _resident(shape):
    nd = len(shape)
    return pl.BlockSpec(shape, lambda *_: (0,) * nd, pipeline_mode=pl.Buffered(1))


def _skip(nd, body):
    return (lambda *refs: body(*refs[nd:])) if nd else body


def _dot_nn(a, b):
    return lax.dot_general(a, b, (((1,), (0,)), ((), ())), preferred_element_type=f32)


def _dot_nt(a, b):
    return lax.dot_general(a, b, (((1,), (1,)), ((), ())), preferred_element_type=f32)


def _dot_tn(a, b):
    return lax.dot_general(a, b, (((0,), (0,)), ((), ())), preferred_element_type=f32)


def _rms(h):
    rs = lax.rsqrt(jnp.mean(h * h, axis=-1, keepdims=True) + EPS)
    return h * rs, rs


def _rms_bwd(dn, g, hhat, rs):
    dhh = dn * g
    return rs * (dhh - hhat * jnp.mean(dhh * hhat, axis=-1, keepdims=True))


def _sigmoid(x):
    return jax.nn.sigmoid(x)


def _me():
    return lax.axis_index("x"), lax.axis_index("y"), lax.axis_index("c")


def _peer(idx):
    return (idx // 4, (idx // 2) % 2, idx % 2)


def all_gather(name, arrays):
    n = len(arrays)

    def body(*refs):
        ins, outs = refs[:n], refs[n:2 * n]
        send_sems, recv_sems, local_sems = refs[2 * n:]
        x, y, c = _me()
        me = 4 * x + 2 * y + c
        locals_ = []
        for k in range(n):
            cp = pltpu.make_async_copy(ins[k], outs[k].at[me], local_sems.at[k])
            cp.start()
            locals_.append(cp)
        for d in range(1, NDEV):
            for k in range(n):
                pltpu.make_async_remote_copy(
                    src_ref=ins[k], dst_ref=outs[k].at[me], send_sem=send_sems.at[k], recv_sem=recv_sems.at[k],
                    device_id=_peer((me + d) % NDEV), device_id_type=MESH).start()
        for k in range(n):
            seven = outs[k].at[pl.ds(0, NDEV - 1)]
            w = pltpu.make_async_remote_copy(src_ref=seven, dst_ref=seven, send_sem=send_sems.at[k],
                                             recv_sem=recv_sems.at[k], device_id=(x, y, c), device_id_type=MESH)
            w.wait_send()
            w.wait_recv()
            locals_[k].wait()

    anyspec = pl.BlockSpec(memory_space=pl.ANY)
    return pl.pallas_call(
        body, name=name,
        out_shape=[jax.ShapeDtypeStruct((NDEV,) + a.shape, a.dtype) for a in arrays],
        in_specs=[anyspec] * n, out_specs=[anyspec] * n,
        scratch_shapes=[pltpu.SemaphoreType.DMA((n,)), pltpu.SemaphoreType.DMA((n,)), pltpu.SemaphoreType.DMA((n,))],
    )(*arrays)


_HBM = pl.BlockSpec(memory_space=pltpu.HBM)
_SEM = pl.BlockSpec(memory_space=pltpu.SEMAPHORE)
_ANY = pl.BlockSpec(memory_space=pl.ANY)
_EFFECT = pltpu.SideEffectType.DATAFLOW_SIDE_EFFECTING


def _in_hbm(a):
    return pltpu.with_memory_space_constraint(a, pltpu.HBM)


def gather_start(name, lands, deps=()):
    n, nd = len(lands), len(deps)

    def body(*refs):
        land = refs[nd:nd + n]
        send_sems, recv_sems = refs[nd + n:nd + n + 2]
        token = refs[-1]
        x, y, c = _me()
        me = 4 * x + 2 * y + c
        for d in range(1, NDEV):
            for k in range(n):
                pltpu.make_async_remote_copy(
                    src_ref=land[k].at[me], dst_ref=land[k].at[me], send_sem=send_sems.at[k], recv_sem=recv_sems.at[k],
                    device_id=_peer((me + d) % NDEV), device_id_type=MESH).start()
        token[...] = jnp.zeros_like(token)

    res = pl.pallas_call(
        body, name=name,
        out_shape=(pltpu.SemaphoreType.DMA((n,)), pltpu.SemaphoreType.DMA((n,)),
                   *[pltpu.HBM(a.shape, a.dtype) for a in lands], jax.ShapeDtypeStruct((8, 128), f32)),
        in_specs=[_ANY] * nd + [_HBM] * n,
        out_specs=(_SEM, _SEM, *[_HBM] * n, pl.BlockSpec(memory_space=pltpu.VMEM)),
        input_output_aliases={nd + k: 2 + k for k in range(n)},
        compiler_params=pltpu.CompilerParams(has_side_effects=_EFFECT),
    )(*deps, *[_in_hbm(a) for a in lands])
    return res[0], res[1], list(res[2:2 + n]), res[-1]


def _other_chips(x, y):
    return [(1 - x, y), (x, 1 - y), (1 - x, 1 - y)]


def gather_start_chips(name, groups, deps=()):
    sizes = [len(g) for g in groups]
    lands = [a for g in groups for a in g]
    n, nd, ng = len(lands), len(deps), len(groups)

    def body(*refs):
        land = refs[nd:nd + n]
        sems = refs[nd + n:nd + n + 2 * ng]
        token = refs[-1]
        x, y, c = _me()
        me = 4 * x + 2 * y + c
        k = 0
        for g, size in enumerate(sizes):
            for j in range(size):
                for to in [(x, y, 1 - c)] + [(cx, cy, c) for cx, cy in _other_chips(x, y)]:
                    pltpu.make_async_remote_copy(
                        src_ref=land[k].at[me], dst_ref=land[k].at[me], send_sem=sems[2 * g].at[j],
                        recv_sem=sems[2 * g + 1].at[j], device_id=to, device_id_type=MESH).start()
                k += 1
        token[...] = jnp.zeros_like(token)

    res = pl.pallas_call(
        body, name=name,
        out_shape=(*[pltpu.SemaphoreType.DMA((size,)) for size in sizes for _ in range(2)],
                   *[pltpu.HBM(a.shape, a.dtype) for a in lands], jax.ShapeDtypeStruct((8, 128), f32)),
        in_specs=[_ANY] * nd + [_HBM] * n,
        out_specs=(*[_SEM] * (2 * ng), *[_HBM] * n, pl.BlockSpec(memory_space=pltpu.VMEM)),
        input_output_aliases={nd + k: 2 * ng + k for k in range(n)},
        compiler_params=pltpu.CompilerParams(has_side_effects=_EFFECT),
    )(*deps, *[_in_hbm(a) for a in lands])
    out, k = [], 2 * ng
    for g, size in enumerate(sizes):
        out.append((res[2 * g], res[2 * g + 1], list(res[k:k + size])))
        k += size
    return out, res[-1]


def gather_forward(name, send_sems, recv_sems, lands, after):
    n = len(lands)

    def body(*refs):
        land = refs[:n]
        ssem, rsem = refs[n:n + 2]
        send2, recv2 = refs[n + 3:n + 5]
        token = refs[-1]
        x, y, c = _me()
        for k in range(n):
            four = land[k].at[pl.ds(0, 4)]
            w = pltpu.make_async_remote_copy(src_ref=four, dst_ref=four, send_sem=ssem.at[k], recv_sem=rsem.at[k],
                                             device_id=(x, y, c), device_id_type=MESH)
            w.wait_send()
            w.wait_recv()
            for cx, cy in _other_chips(x, y):
                slot = 4 * cx + 2 * cy + c
                pltpu.make_async_remote_copy(
                    src_ref=land[k].at[slot], dst_ref=land[k].at[slot], send_sem=send2.at[k], recv_sem=recv2.at[k],
                    device_id=(x, y, 1 - c), device_id_type=MESH).start()
        token[...] = jnp.zeros_like(token)

    res = pl.pallas_call(
        body, name=name,
        out_shape=(pltpu.SemaphoreType.DMA((n,)), pltpu.SemaphoreType.DMA((n,)),
                   *[pltpu.HBM(a.shape, a.dtype) for a in lands], jax.ShapeDtypeStruct((8, 128), f32)),
        in_specs=[_HBM] * n + [_SEM, _SEM, _ANY],
        out_specs=(_SEM, _SEM, *[_HBM] * n, pl.BlockSpec(memory_space=pltpu.VMEM)),
        input_output_aliases={k: 2 + k for k in range(n)},
        compiler_params=pltpu.CompilerParams(has_side_effects=_EFFECT),
    )(*lands, send_sems, recv_sems, after)
    return res[0], res[1], list(res[2:2 + n]), res[-1]


def exchange_start(name, grads, deps=()):
    n, nd = len(grads), len(deps)
    lands = [lax.empty((NDEV - 1, g.shape[0]) + g.shape[2:], g.dtype) for g in grads]

    def body(*refs):
        src = refs[nd:nd + n]
        land = refs[nd + n:nd + 2 * n]
        send_sems, recv_sems = refs[nd + 2 * n:nd + 2 * n + 2]
        token = refs[-1]
        x, y, c = _me()
        me = 4 * x + 2 * y + c
        for d in range(1, NDEV):
            p = (me + d) % NDEV
            for k in range(n):
                pltpu.make_async_remote_copy(
                    src_ref=src[k].at[:, p], dst_ref=land[k].at[d - 1], send_sem=send_sems.at[k], recv_sem=recv_sems.at[k],
                    device_id=_peer(p), device_id_type=MESH).start()
        token[...] = jnp.zeros_like(token)

    both = list(grads) + lands
    res = pl.pallas_call(
        body, name=name,
        out_shape=(pltpu.SemaphoreType.DMA((n,)), pltpu.SemaphoreType.DMA((n,)),
                   *[pltpu.HBM(a.shape, a.dtype) for a in both], jax.ShapeDtypeStruct((8, 128), f32)),
        in_specs=[_ANY] * nd + [_HBM] * (2 * n),
        out_specs=(_SEM, _SEM, *[_HBM] * (2 * n), pl.BlockSpec(memory_space=pltpu.VMEM)),
        input_output_aliases={nd + k: 2 + k for k in range(2 * n)},
        compiler_params=pltpu.CompilerParams(has_side_effects=_EFFECT),
    )(*deps, *[_in_hbm(a) for a in both])
    return res[0], res[1], list(res[2:2 + n]), list(res[2 + n:2 + 2 * n]), res[-1]


def copies_wait(name, send_sems, recv_sems, sent, lands, after, count=NDEV - 1):
    ns, n = len(sent), len(lands)

    def body(*refs):
        land = refs[ns:ns + n]
        ssem, rsem = refs[ns + n:ns + n + 2]
        x, y, c = _me()
        for k in range(n):
            seven = land[k].at[pl.ds(0, count)]
            w = pltpu.make_async_remote_copy(src_ref=seven, dst_ref=seven, send_sem=ssem.at[k], recv_sem=rsem.at[k],
                                             device_id=(x, y, c), device_id_type=MESH)
            w.wait_send()
            w.wait_recv()

    both = list(sent) + list(lands)
    res = pl.pallas_call(
        body, name=name, out_shape=tuple(pltpu.HBM(a.shape, a.dtype) for a in both),
        in_specs=[_HBM] * (ns + n) + [_SEM, _SEM, _ANY], out_specs=tuple([_HBM] * (ns + n)),
        input_output_aliases={k: k for k in range(ns + n)},
        compiler_params=pltpu.CompilerParams(has_side_effects=_EFFECT),
    )(*both, send_sems, recv_sems, after)
    return list(res[:ns]), list(res[ns:])


def prep_layer(layer, me, col_sharded, row_sharded, deps=()):
    nc, nr = len(col_sharded), len(row_sharded)

    def body(me_ref, *refs):
        ins, outs = refs[:nc + nr], refs[nc + nr + len(deps):]
        for k in range(nc):
            outs[k][...] = ins[k][...].T.astype(bf16)
        for k in range(nc, nc + nr):
            outs[k][...] = ins[k][...].astype(bf16)

    arrs = list(col_sharded) + list(row_sharded)
    in_specs = [pl.BlockSpec((None,) + a.shape[1:], lambda i, me_ref: (layer, 0, 0)) for a in arrs]
    shapes = [(a.shape[2], a.shape[1]) for a in col_sharded] + [a.shape[1:] for a in row_sharded]
    out_specs = [pl.BlockSpec((None,) + s, lambda i, me_ref: (me_ref[0], 0, 0)) for s in shapes]
    return pl.pallas_call(
        body, name=f"prep_layer{layer}",
        grid_spec=pltpu.PrefetchScalarGridSpec(num_scalar_prefetch=1, grid=(1,), in_specs=in_specs + [_ANY] * len(deps),
                                               out_specs=out_specs),
        out_shape=[jax.ShapeDtypeStruct((NDEV,) + s, bf16) for s in shapes], compiler_params=_cp(48))(me, *arrs, *deps)


def slot_in(name, me, arrays):
    n = len(arrays)

    def body(me_ref, *refs):
        for k in range(n):
            refs[n + k][...] = refs[k][...]

    in_specs = [pl.BlockSpec(a.shape, lambda i, me_ref: (0, 0)) for a in arrays]
    out_specs = [pl.BlockSpec((None,) + a.shape, lambda i, me_ref: (me_ref[0], 0, 0)) for a in arrays]
    return pl.pallas_call(
        body, name=name,
        grid_spec=pltpu.PrefetchScalarGridSpec(num_scalar_prefetch=1, grid=(1,), in_specs=in_specs, out_specs=out_specs),
        out_shape=[jax.ShapeDtypeStruct((NDEV,) + a.shape, a.dtype) for a in arrays])(me, *arrays)


def _ff_chunks(ff, want=768):
    if ff % 256:
        return [slice(0, ff)]
    return [slice(c, min(c + want, ff)) for c in range(0, ff, want)]


def ffn_fwd(name, h, g, wgT, wuT, wd, deps=()):
    tp, d = h.shape
    ff = wgT.shape[0]
    tm = _row_tile(tp, 704)

    def body(h_ref, g_ref, wg_ref, wu_ref, wd_ref, ho_ref, a_ref, b_ref):
        hh = h_ref[...]
        hhat, _ = _rms(hh)
        n = (hhat * g_ref[...]).astype(bf16)
        acc = None
        for cols in _ff_chunks(ff):
            a = _dot_nt(n, wg_ref[cols, :])
            b = _dot_nt(n, wu_ref[cols, :])
            part = _dot_nn(((a * _sigmoid(a)) * b).astype(bf16), wd_ref[cols, :])
            acc = part if acc is None else acc + part
            a_ref[:, cols] = a.astype(bf16)
            b_ref[:, cols] = b.astype(bf16)
        ho_ref[...] = hh + 0.5 * acc

    row = lambda w: pl.BlockSpec((tm, w), lambda i: (i, 0))
    return pl.pallas_call(
        _skip(len(deps), body), name=name, grid=(tp // tm,),
        in_specs=[_ANY] * len(deps) + [row(d), _resident((1, d)), _resident((ff, d)), _resident((ff, d)), _resident((ff, d))],
        out_specs=[row(d), row(ff), row(ff)],
        out_shape=[jax.ShapeDtypeStruct((tp, d), f32), jax.ShapeDtypeStruct((tp, ff), bf16),
                   jax.ShapeDtypeStruct((tp, ff), bf16)],
        compiler_params=_cp(56, ("arbitrary",)))(*deps, h, g, wgT, wuT, wd)


def ffn_bwd(name, dy, h, g, a, b, wgT, wuT, wd, deps=()):
    tp, d = h.shape
    ff = wgT.shape[0]
    tm = _row_tile(tp, 384)

    def body(dy_ref, h_ref, g_ref, a_ref, b_ref, wg_ref, wu_ref, wd_ref, dh_ref, lhs_ref, rhs_ref, dg_ref):
        dyv = dy_ref[...]
        hhat, rs = _rms(h_ref[...])
        gv = g_ref[...]
        n = hhat * gv
        dyh = (0.5 * dyv).astype(bf16)
        dn = None
        for cols in _ff_chunks(ff):
            ds = _dot_nt(dyh, wd_ref[cols, :])
            av = a_ref[:, cols].astype(f32)
            bv = b_ref[:, cols].astype(f32)
            sg = _sigmoid(av)
            sa = av * sg
            da = (ds * bv * (sg * (1.0 + av * (1.0 - sg)))).astype(bf16)
            db = (ds * sa).astype(bf16)
            part = _dot_nn(da, wg_ref[cols, :]) + _dot_nn(db, wu_ref[cols, :])
            dn = part if dn is None else dn + part
            lhs_ref[0, :, cols] = da
            lhs_ref[1, :, cols] = db
            lhs_ref[2, :, cols] = (sa * bv).astype(bf16)
        dh_ref[...] = dyv + _rms_bwd(dn, gv, hhat, rs)

        @pl.when(pl.program_id(0) == 0)
        def _():
            dg_ref[...] = jnp.zeros_like(dg_ref)

        dg_ref[0:1, :] += jnp.sum(dn * hhat, axis=0, keepdims=True)
        rhs_ref[0] = n.astype(bf16)
        rhs_ref[1] = dyh

    row = lambda w: pl.BlockSpec((tm, w), lambda i: (i, 0))
    return pl.pallas_call(
        _skip(len(deps), body), name=name, grid=(tp // tm,),
        in_specs=[_ANY] * len(deps) + [row(d), row(d), _resident((1, d)), row(ff), row(ff),
                  _resident((ff, d)), _resident((ff, d)), _resident((ff, d))],
        out_specs=[row(d), pl.BlockSpec((3, tm, ff), lambda i: (0, i, 0)), pl.BlockSpec((2, tm, d), lambda i: (0, i, 0)),
                   pl.BlockSpec((8, d), lambda i: (0, 0))],
        out_shape=[jax.ShapeDtypeStruct((tp, d), f32), jax.ShapeDtypeStruct((3, tp, ff), bf16),
                   jax.ShapeDtypeStruct((2, tp, d), bf16), jax.ShapeDtypeStruct((8, d), f32)],
        compiler_params=_cp(58, ("arbitrary",)))(*deps, dy, h, g, a, b, wgT, wuT, wd)


def mm_tn(name, lhs, rhs, rhs_of, deps=(), only=None):
    _, tp, m = lhs.shape
    b0, nb = (0, lhs.shape[0]) if only is None else (only, 1)
    n = rhs.shape[2]
    def fits(t, ms):
        mb = m // ms
        return (tp % t == 0 and m % (128 * ms) == 0
                and 2 * t * (mb + n) * 2 + mb * n * (2 * 2 + 4 + (4 if t < tp else 0)) <= 54 * 2**20)

    tk, msplit = next(((t, ms) for t in (tp, 1408, 704, 384) for ms in (1, 2, 4) if fits(t, ms)), (128, 1))
    nk = tp // tk
    mb = m // msplit

    def body(l_ref, r_ref, o_ref, acc_ref):
        if nk == 1:
            o_ref[...] = _dot_tn(l_ref[...], r_ref[...]).astype(o_ref.dtype)
            return
        k = pl.program_id(2)

        @pl.when(k == 0)
        def _():
            acc_ref[...] = jnp.zeros_like(acc_ref)

        acc_ref[...] += _dot_tn(l_ref[...], r_ref[...])

        @pl.when(k == nk - 1)
        def _():
            o_ref[...] = acc_ref[...].astype(o_ref.dtype)

    return pl.pallas_call(
        _skip(len(deps), body), name=name, grid=(nb, msplit, nk),
        in_specs=[_ANY] * len(deps) + [pl.BlockSpec((None, tk, mb), lambda b, j, k: (b0 + b, k, j)),
                                       pl.BlockSpec((None, tk, n), lambda b, j, k: (rhs_of(b0 + b), k, 0))],
        out_specs=pl.BlockSpec((None, mb, n), lambda b, j, k: (b, j, 0)),
        out_shape=jax.ShapeDtypeStruct((nb, m, n), bf16),
        scratch_shapes=[pltpu.VMEM((mb, n) if nk > 1 else (8, 128), f32)],
        compiler_params=_cp(60, ("arbitrary", "arbitrary", "arbitrary")))(*deps, lhs, rhs)


def mix_in_fwd(name, h, g, winT, cos2, sin2):
    tp, d = h.shape
    nin = winT.shape[0]
    tm = _row_tile(tp)

    def body(h_ref, g_ref, w_ref, cos_ref, sin_ref, zq_ref, zg_ref, zu_ref, zgate_ref):
        hhat, _ = _rms(h_ref[...])
        z = _dot_nt((hhat * g_ref[...]).astype(bf16), w_ref[...])
        cosv, sinv = cos_ref[...], sin_ref[...]
        for hh in range(HEADS):
            qcols, kcols = slice(hh * HD, (hh + 1) * HD), slice(RW + hh * HD, RW + (hh + 1) * HD)
            zq_ref[:, qcols] = (_rot(z[:, qcols], cosv, sinv) * HD ** -0.5).astype(bf16)
            zq_ref[:, kcols] = _rot(z[:, kcols], cosv, sinv).astype(bf16)
        zq_ref[:, 2 * RW:] = z[:, 2 * RW:3 * RW].astype(bf16)
        zg_ref[...] = z[:, 3 * RW:4 * RW].astype(zg_ref.dtype)
        zu_ref[...] = z[:, 4 * RW:5 * RW]
        zgate_ref[...] = z[:, 5 * RW:].astype(zgate_ref.dtype)

    row = lambda w: pl.BlockSpec((tm, w), lambda i: (i, 0))
    widths = (3 * RW, RW, RW, 2 * d)
    return pl.pallas_call(
        body, name=name, grid=(tp // tm,),
        in_specs=[row(d), _resident((1, d)), _resident((nin, d)), row(HD), row(HD)],
        out_specs=[row(w) for w in widths],
        out_shape=[jax.ShapeDtypeStruct((tp, w), dt) for w, dt in zip(widths, (bf16, bf16, f32, bf16))],
        compiler_params=_cp(56, ("arbitrary",)))(h, g, winT, cos2, sin2)


def mix_in_bwd(name, dq, dk, dv, dzg, dzu, dzgate, h, g, winT, dres):
    tp, d = h.shape
    nin = winT.shape[0]
    tm = _row_tile(tp)

    def body(dq_ref, dk_ref, dv_ref, dzg_ref, dzu_ref, dzgate_ref, h_ref, g_ref, w_ref, dres_ref, dh_ref, dz_ref, n_ref, dg_ref):
        dn, col = None, 0
        for piece in (dq_ref, dk_ref, dv_ref, dzg_ref, dzu_ref, dzgate_ref):
            v = piece[...]
            part = _dot_nn(v, w_ref[col:col + v.shape[1], :])
            dn = part if dn is None else dn + part
            dz_ref[:, col:col + v.shape[1]] = v
            col += v.shape[1]
        hhat, rs = _rms(h_ref[...])
        gv = g_ref[...]
        dh_ref[...] = dres_ref[...] + _rms_bwd(dn, gv, hhat, rs)

        @pl.when(pl.program_id(0) == 0)
        def _():
            dg_ref[...] = jnp.zeros_like(dg_ref)

        dg_ref[0:1, :] += jnp.sum(dn * hhat, axis=0, keepdims=True)
        n_ref[...] = (hhat * gv).astype(bf16)

    row = lambda w: pl.BlockSpec((tm, w), lambda i: (i, 0))
    return pl.pallas_call(
        body, name=name, grid=(tp // tm,),
        in_specs=[row(RW)] * 5 + [row(2 * d), row(d), _resident((1, d)), _resident((nin, d)), row(d)],
        out_specs=[row(d), pl.BlockSpec((None, tm, nin), lambda i: (0, i, 0)), pl.BlockSpec((None, tm, d), lambda i: (0, i, 0)),
                   pl.BlockSpec((8, d), lambda i: (0, 0))],
        out_shape=[jax.ShapeDtypeStruct((tp, d), f32), jax.ShapeDtypeStruct((1, tp, nin), bf16),
                   jax.ShapeDtypeStruct((1, tp, d), bf16), jax.ShapeDtypeStruct((8, d), f32)],
        compiler_params=_cp(56, ("arbitrary",)))(dq, dk, dv, dzg, dzu, dzgate, h, g, winT, dres)


def _retention_tables(tp, pad, zero):
    half = HD // 2
    inv_freq = ROPE_BASE ** (-jnp.arange(half, dtype=f32) / half)
    pos = jnp.arange(tp, dtype=f32) - pad + zero
    ang = pos[:, None] * inv_freq[None, :]
    cos, sin = jnp.cos(ang), jnp.sin(ang)
    cos2 = jnp.concatenate([cos, cos], axis=-1)
    sin2 = jnp.concatenate([-sin, sin], axis=-1)
    log_gamma = jnp.log1p(-(2.0 ** (-5.0 - jnp.arange(HEADS, dtype=f32))))
    idx = jnp.arange(CHUNK, dtype=f32)
    diff = idx[:, None] - idx[None, :]
    intra = jnp.where(diff[None] >= 0, jnp.exp(diff[None] * log_gamma[:, None, None]), 0.0)
    k_decay = jnp.exp((CHUNK - 1.0 - idx)[None, :] * log_gamma[:, None])
    q_decay = jnp.exp((idx + 1.0)[None, :] * log_gamma[:, None])
    chunk_decay = jnp.exp(CHUNK * log_gamma)
    full = (HEADS, CHUNK, HD)
    dec = jnp.stack([intra, jnp.broadcast_to(k_decay[:, :, None], full), jnp.broadcast_to(q_decay[:, :, None], full),
                     jnp.broadcast_to(chunk_decay[:, None, None], full)], axis=1)
    return cos2, sin2, dec


def _rot(t, cos2, sin2):
    return t * cos2 + pltpu.roll(t, HD // 2, 1) * sin2


def _rot_t(t, cos2, sin2):
    return t * cos2 - pltpu.roll(t, HD // 2, 1) * sin2


def _chunks_per_step(nch):
    return 3 if nch % 3 == 0 else 1


def retention_fwd(name, zq, dec):
    tp = zq.shape[0]
    nch = tp // CHUNK
    per = _chunks_per_step(nch)

    def body(q_ref, k_ref, v_ref, dec_ref, out_ref, st_ref, s_ref):
        @pl.when(pl.program_id(0) == 0)
        def _():
            s_ref[...] = jnp.zeros_like(s_ref)

        state = [s_ref[hh] for hh in range(HEADS)]
        for j in range(per):
            rows = slice(j * CHUNK, (j + 1) * CHUNK)
            for hh in range(HEADS):
                cols = slice(hh * HD, (hh + 1) * HD)
                qb, kb, vb = q_ref[rows, cols], k_ref[rows, cols], v_ref[rows, cols]
                sc = (_dot_nt(qb, kb) * dec_ref[hh, 0]).astype(bf16)
                sb = state[hh].astype(bf16)
                cross = _dot_nn((qb.astype(f32) * dec_ref[hh, 2]).astype(bf16), sb)
                out_ref[rows, cols] = (_dot_nn(sc, vb) + cross).astype(out_ref.dtype)
                st_ref[hh, j] = sb
                state[hh] = state[hh] * dec_ref[hh, 3] + _dot_tn((kb.astype(f32) * dec_ref[hh, 1]).astype(bf16), vb)
        for hh in range(HEADS):
            s_ref[hh] = state[hh]

    part = lambda j: pl.BlockSpec((per * CHUNK, RW), lambda n: (n, j))
    return pl.pallas_call(
        body, name=name, grid=(nch // per,),
        in_specs=[part(0), part(1), part(2), _resident((HEADS, 4, CHUNK, HD))],
        out_specs=[part(0), pl.BlockSpec((HEADS, per, HD, HD), lambda n: (0, n, 0, 0))],
        out_shape=[jax.ShapeDtypeStruct((tp, RW), bf16), jax.ShapeDtypeStruct((HEADS, nch, HD, HD), bf16)],
        scratch_shapes=[pltpu.VMEM((HEADS, HD, HD), f32)],
        compiler_params=_cp(32, ("arbitrary",)))(zq, zq, zq, dec)


def retention_bwd(name, zq, cos2, sin2, dec, states, dout, pad, deps=()):
    tp = zq.shape[0]
    nch = tp // CHUNK
    per = _chunks_per_step(nch)
    nblk = nch // per
    scale = HD ** -0.5

    def body(q_ref, k_ref, v_ref, cos_ref, sin_ref, dec_ref, st_ref, do_ref, dq_ref, dk_ref, dv_ref, g_ref):
        @pl.when(pl.program_id(0) == 0)
        def _():
            g_ref[...] = jnp.zeros_like(g_ref)

        first_row = (nblk - 1 - pl.program_id(0)) * (per * CHUNK)
        gstate = [g_ref[hh] for hh in range(HEADS)]
        for j in reversed(range(per)):
            rows = slice(j * CHUNK, (j + 1) * CHUNK)
            cosv, sinv = cos_ref[rows, :], sin_ref[rows, :]
            keep = (lax.broadcasted_iota(jnp.int32, (CHUNK, HD), 0) + (first_row + j * CHUNK)) >= pad
            for hh in range(HEADS):
                cols = slice(hh * HD, (hh + 1) * HD)
                intra, kdec, qdec = dec_ref[hh, 0], dec_ref[hh, 1], dec_ref[hh, 2]
                qb, kb, vb = q_ref[rows, cols], k_ref[rows, cols], v_ref[rows, cols]
                qd = (qb.astype(f32) * qdec).astype(bf16)
                kd = (kb.astype(f32) * kdec).astype(bf16)
                sc = (_dot_nt(qb, kb) * intra).astype(bf16)
                dob = do_ref[rows, cols]
                sb = st_ref[hh, j]
                gb = gstate[hh].astype(bf16)
                dsc = (_dot_nt(dob, vb) * intra).astype(bf16)
                dv = _dot_tn(sc, dob) + _dot_nn(kd, gb)
                dqr = _dot_nn(dsc, kb) + _dot_nt(dob, sb) * qdec
                dkr = _dot_tn(dsc, qb) + _dot_nt(vb, gb) * kdec
                gstate[hh] = gstate[hh] * dec_ref[hh, 3] + _dot_tn(qd, dob)
                dq_ref[rows, cols] = jnp.where(keep, _rot_t(dqr * scale, cosv, sinv), 0.0).astype(bf16)
                dk_ref[rows, cols] = jnp.where(keep, _rot_t(dkr, cosv, sinv), 0.0).astype(bf16)
                dv_ref[rows, cols] = jnp.where(keep, dv, 0.0).astype(bf16)
        for hh in range(HEADS):
            g_ref[hh] = gstate[hh]

    part = lambda j: pl.BlockSpec((per * CHUNK, RW), lambda t: (nblk - 1 - t, j))
    table = pl.BlockSpec((per * CHUNK, HD), lambda t: (nblk - 1 - t, 0))
    return pl.pallas_call(
        _skip(len(deps), body), name=name, grid=(nblk,),
        in_specs=[_ANY] * len(deps) + [part(0), part(1), part(2), table, table, _resident((HEADS, 4, CHUNK, HD)),
                                       pl.BlockSpec((HEADS, per, HD, HD), lambda t: (0, nblk - 1 - t, 0, 0)), part(0)],
        out_specs=[part(0)] * 3,
        out_shape=[jax.ShapeDtypeStruct((tp, RW), bf16)] * 3,
        scratch_shapes=[pltpu.VMEM((HEADS, HD, HD), f32)],
        compiler_params=_cp(32, ("arbitrary",)))(*deps, zq, zq, zq, cos2, sin2, dec, states, dout)


def _window_sum(xv, steps, tp, forward):
    s = xv
    for j in range(steps):
        sh = 2 ** j
        s = s + pltpu.roll(s, (tp - sh) if forward else sh, 0)
    return s


def pool_fwd(name, zu, maps, scale, pad):
    tp = zu.shape[0]

    def body(u_ref, maps_ref, scale_ref, pooled_ref, p_ref):
        row = lax.broadcasted_iota(jnp.int32, (tp, HD), 0)
        for gi, w in enumerate(POOL_WINDOWS):
            cols = slice(gi * HD, (gi + 1) * HD)
            xv = u_ref[:, cols]
            cnt = jnp.clip(row - (pad - 1), 1, w).astype(f32)
            pooled = jnp.where(row >= pad, _window_sum(xv, gi + 1, tp, False) / cnt - xv, 0.0).astype(bf16)
            pooled_ref[:, cols] = pooled
            p_ref[:, cols] = (_dot_nn(pooled, maps_ref[gi].astype(bf16)) * scale_ref[:, cols]).astype(bf16)

    return pl.pallas_call(
        body, name=name,
        out_shape=[jax.ShapeDtypeStruct((tp, RW), bf16), jax.ShapeDtypeStruct((tp, RW), bf16)],
        compiler_params=_cp(56))(zu, maps, scale)


def pool_bwd(name, dp, pooled, maps, scale, pad):
    tp = dp.shape[0]

    def body(dp_ref, pooled_ref, maps_ref, scale_ref, du_ref, dmaps_ref, dscale_ref):
        row = lax.broadcasted_iota(jnp.int32, (tp, HD), 0)
        dscale_ref[...] = jnp.zeros_like(dscale_ref)
        for gi, w in enumerate(POOL_WINDOWS):
            cols = slice(gi * HD, (gi + 1) * HD)
            mb = maps_ref[gi].astype(bf16)
            pooled = pooled_ref[:, cols]
            dpf = dp_ref[:, cols].astype(f32)
            dscale_ref[0:1, cols] = jnp.sum(dpf * _dot_nn(pooled, mb), axis=0, keepdims=True)
            dpm = (dpf * scale_ref[:, cols]).astype(bf16)
            dmaps_ref[gi * HD:(gi + 1) * HD, :] = _dot_tn(pooled, dpm)
            dpool = jnp.where(row >= pad, _dot_nt(dpm, mb), 0.0)
            cnt = jnp.clip(row - (pad - 1), 1, w).astype(f32)
            du = _window_sum(dpool / cnt, gi + 1, tp, True) - dpool
            du_ref[:, cols] = jnp.where(row >= pad, du, 0.0).astype(bf16)

    return pl.pallas_call(
        body, name=name,
        out_shape=[jax.ShapeDtypeStruct((tp, RW), bf16), jax.ShapeDtypeStruct((HEADS * HD, HD), f32),
                   jax.ShapeDtypeStruct((8, RW), f32)],
        compiler_params=_cp(56))(dp, pooled, maps, scale)


def _group_norm(o):
    mu = jnp.mean(o, axis=-1, keepdims=True)
    oc = o - mu
    rstd = lax.rsqrt(jnp.mean(oc * oc, axis=-1, keepdims=True) + EPS)
    return oc * rstd, rstd


def mix_out_fwd(name, h, oraw, zg, zgate, p, wretT, wpoolT, wout, deps=()):
    tp, d = h.shape
    tm = _row_tile(tp)

    def body(h_ref, o_ref, zg_ref, zgate_ref, p_ref, wr_ref, wp_ref, wo_ref, ho_ref, rp_ref, mixed_ref):
        parts = []
        for hh in range(HEADS):
            cols = slice(hh * HD, (hh + 1) * HD)
            rhat, _ = _group_norm(o_ref[:, cols].astype(f32))
            gv = zg_ref[:, cols].astype(f32)
            parts.append(rhat * (gv * _sigmoid(gv)))
        r = jnp.concatenate(parts, axis=-1).astype(bf16)
        pv = p_ref[...]
        ret = _dot_nt(r, wr_ref[...])
        pool = _dot_nt(pv, wp_ref[...])
        mixed = (_sigmoid(zgate_ref[:, :d].astype(f32)) * ret + _sigmoid(zgate_ref[:, d:].astype(f32)) * pool).astype(bf16)
        ho_ref[...] = h_ref[...] + _dot_nn(mixed, wo_ref[...])
        rp_ref[0] = r
        rp_ref[1] = pv
        mixed_ref[...] = mixed

    row = lambda w: pl.BlockSpec((tm, w), lambda i: (i, 0))
    return pl.pallas_call(
        _skip(len(deps), body), name=name, grid=(tp // tm,),
        in_specs=[_ANY] * len(deps) + [row(d), row(RW), row(RW), row(2 * d), row(RW), _resident((d, RW)), _resident((d, RW)),
                                       _resident((d, d))],
        out_specs=[row(d), pl.BlockSpec((2, tm, RW), lambda i: (0, i, 0)), pl.BlockSpec((None, tm, d), lambda i: (0, i, 0))],
        out_shape=[jax.ShapeDtypeStruct((tp, d), f32), jax.ShapeDtypeStruct((2, tp, RW), bf16),
                   jax.ShapeDtypeStruct((1, tp, d), bf16)],
        compiler_params=_cp(48, ("arbitrary",)))(*deps, h, oraw, zg, zgate, p, wretT, wpoolT, wout)


def mix_out_bwd(name, dy, oraw, zg, zgate, rp, wretT, wpoolT, wout, deps=()):
    tp, d = dy.shape
    tm = _row_tile(tp)

    def body(dy_ref, o_ref, zg_ref, zgate_ref, rp_ref, wr_ref, wp_ref, wo_ref,
             do_ref, dzg_ref, dzgate_ref, dp_ref, drp_ref, dyb_ref):
        dyb = dy_ref[...].astype(bf16)
        dmixed = _dot_nt(dyb, wo_ref[...])
        sa = _sigmoid(zgate_ref[:, :d].astype(f32))
        sb = _sigmoid(zgate_ref[:, d:].astype(f32))
        dret = dmixed * sa
        dpool = dmixed * sb
        dzgate_ref[:, :d] = (dret * _dot_nt(rp_ref[0], wr_ref[...]) * (1.0 - sa)).astype(bf16)
        dzgate_ref[:, d:] = (dpool * _dot_nt(rp_ref[1], wp_ref[...]) * (1.0 - sb)).astype(bf16)
        dretb, dpoolb = dret.astype(bf16), dpool.astype(bf16)
        dr = _dot_nn(dretb, wr_ref[...])
        dp_ref[...] = _dot_nn(dpoolb, wp_ref[...]).astype(bf16)
        for hh in range(HEADS):
            cols = slice(hh * HD, (hh + 1) * HD)
            rhat, rstd = _group_norm(o_ref[:, cols].astype(f32))
            gv = zg_ref[:, cols].astype(f32)
            sg = _sigmoid(gv)
            drh = dr[:, cols]
            drhat = drh * (gv * sg)
            dzg_ref[:, cols] = (drh * rhat * (sg * (1.0 + gv * (1.0 - sg)))).astype(bf16)
            do = rstd * (drhat - jnp.mean(drhat, axis=-1, keepdims=True)
                         - rhat * jnp.mean(drhat * rhat, axis=-1, keepdims=True))
            do_ref[:, cols] = do.astype(bf16)
        drp_ref[0] = dretb
        drp_ref[1] = dpoolb
        dyb_ref[...] = dyb

    row = lambda w: pl.BlockSpec((tm, w), lambda i: (i, 0))
    return pl.pallas_call(
        _skip(len(deps), body), name=name, grid=(tp // tm,),
        in_specs=[_ANY] * len(deps) + [row(d), row(RW), row(RW), row(2 * d), pl.BlockSpec((2, tm, RW), lambda i: (0, i, 0)),
                                       _resident((d, RW)), _resident((d, RW)), _resident((d, d))],
        out_specs=[row(RW), row(RW), row(2 * d), row(RW), pl.BlockSpec((2, tm, d), lambda i: (0, i, 0)),
                   pl.BlockSpec((None, tm, d), lambda i: (0, i, 0))],
        out_shape=[jax.ShapeDtypeStruct((tp, RW), bf16), jax.ShapeDtypeStruct((tp, RW), bf16),
                   jax.ShapeDtypeStruct((tp, 2 * d), bf16), jax.ShapeDtypeStruct((tp, RW), bf16),
                   jax.ShapeDtypeStruct((2, tp, d), bf16), jax.ShapeDtypeStruct((1, tp, d), bf16)],
        compiler_params=_cp(48, ("arbitrary",)))(*deps, dy, oraw, zg, zgate, rp, wretT, wpoolT, wout)


def final_loss(name, h, g, target):
    tp, d = h.shape
    tm = _row_tile(tp)
    nsub = tm // CHUNK

    def body(h_ref, g_ref, *rest):
        t_refs = rest[:nsub]
        dh_ref, loss_ref, dg_ref = rest[nsub:]
        i = pl.program_id(0)

        @pl.when(i == 0)
        def _():
            loss_ref[...] = jnp.zeros_like(loss_ref)
            dg_ref[...] = jnp.zeros_like(dg_ref)

        gv = g_ref[...]
        for j in range(nsub):
            rows = slice(j * CHUNK, (j + 1) * CHUNK)
            hhat, rs = _rms(h_ref[rows, :])
            err = jnp.where(i * nsub + j >= 1, hhat * gv - t_refs[j][...], 0.0)
            dyv = err / d
            dh_ref[rows, :] = _rms_bwd(dyv, gv, hhat, rs)
            loss_ref[...] += 0.5 * jnp.sum(jnp.sum(err * err, axis=-1, keepdims=True) / d)
            dg_ref[0:1, :] += jnp.sum(dyv * hhat, axis=0, keepdims=True)

    lagged = lambda j: pl.BlockSpec((CHUNK, d), lambda i: (jnp.maximum(i * nsub + j - 1, 0), 0))
    return pl.pallas_call(
        body, name=name, grid=(tp // tm,),
        in_specs=[pl.BlockSpec((tm, d), lambda i: (i, 0)), _resident((1, d))] + [lagged(j) for j in range(nsub)],
        out_specs=[pl.BlockSpec((tm, d), lambda i: (i, 0)), pl.BlockSpec((8, 128), lambda i: (0, 0)),
                   pl.BlockSpec((8, d), lambda i: (0, 0))],
        out_shape=[jax.ShapeDtypeStruct((tp, d), f32), jax.ShapeDtypeStruct((8, 128), f32),
                   jax.ShapeDtypeStruct((8, d), f32)],
        compiler_params=_cp(32, ("arbitrary",)))(h, g, *[target] * nsub)


def _adamw(w, g, m, v):
    m = ADAM_B1 * m + (1.0 - ADAM_B1) * g
    v = ADAM_B2 * v + (1.0 - ADAM_B2) * (g * g)
    m_hat = m / (1.0 - ADAM_B1 ** ADAM_STEP)
    v_hat = v / (1.0 - ADAM_B2 ** ADAM_STEP)
    delta = -ADAM_LR * (m_hat / (jnp.sqrt(v_hat) + ADAM_EPS) + ADAM_WD * w)
    return delta, m, v


def adam_big(name, me, recv, own, b, layer, transposed, w, m, v, prev):
    r, c = recv.shape[2:]
    wshape = w.shape[1:]
    nchunk = 1 if transposed else next(k for k in (4, 2, 1) if r % (16 * k) == 0)
    rc = r // nchunk

    def body(me_ref, recv_ref, own_ref, w_ref, m_ref, v_ref, *rest):
        g_ref, d_ref, nm_ref, nv_ref = rest[-4:]
        g = own_ref[...].astype(f32)
        for j in range(NDEV - 1):
            g = g + recv_ref[j].astype(f32)
        if transposed:
            g = g.T
        delta, nm, nv = _adamw(w_ref[...], g, m_ref[...], v_ref[...])
        g_ref[...] = g
        d_ref[...] = delta
        nm_ref[...] = nm
        nv_ref[...] = nv

    wblock = wshape if transposed else (rc, c)
    wspec = pl.BlockSpec((None,) + wblock, lambda i, me_ref: (layer, i, 0))
    in_specs = [pl.BlockSpec((NDEV - 1, None, rc, c), lambda i, me_ref: (0, b, i, 0)),
                pl.BlockSpec((None, None, rc, c), lambda i, me_ref: (b, me_ref[0], i, 0)), wspec, wspec, wspec]
    args = [recv, own, w, m, v]
    aliases = {}
    if prev is not None:
        in_specs += [_ANY] * 4
        args += list(prev)
        aliases = {6 + k: k for k in range(4)}
    return pl.pallas_call(
        body, name=name,
        grid_spec=pltpu.PrefetchScalarGridSpec(num_scalar_prefetch=1, grid=(nchunk,), in_specs=in_specs,
                                               out_specs=[wspec] * 4),
        out_shape=[jax.ShapeDtypeStruct(w.shape, f32)] * 4, input_output_aliases=aliases,
        compiler_params=_cp(56))(me, *args)


def adam_small(name, ga0, gmaps0, gmeta, ga1, gmaps1, norms, pool_scale, pool_maps, meta, final_norm, d):
    def body(ga0_ref, gmaps0_ref, gmeta_ref, ga1_ref, gmaps1_ref, *refs):
        ins, outs = refs[:21], refs[21:]
        x, y, c = _me()
        me = 4 * x + 2 * y + c

        def total(ref, rows):
            t = ref[0, rows, :]
            for j in range(1, NDEV):
                t = t + ref[j, rows, :]
            return t

        row = lambda r: slice(r, r + 1)
        outs[0][...] = jnp.broadcast_to(total(ga1_ref, row(0))[:, :128], (8, 128))

        def update(k, g, o):
            w_ref, m_ref, v_ref = ins[3 * k:3 * k + 3]
            delta, nm, nv = _adamw(w_ref[...], g, m_ref[...], v_ref[...])
            for ref, val in zip(outs[o:o + 4], (g, delta, nm, nv)):
                ref[...] = val

        two = lax.broadcasted_iota(jnp.int32, (2, d), 0)
        for k in range(3):
            update(k, jnp.where(two == 0, total(ga0_ref, row(k)), total(ga1_ref, row(2 + k))), 1 + 4 * k)
        update(3, jnp.where(two[:, :RW] == 0, total(ga0_ref, row(3))[:, :RW], total(ga1_ref, row(5))[:, :RW]), 13)
        update(4, jnp.concatenate([total(gmaps0_ref, slice(None)), total(gmaps1_ref, slice(None))], axis=0), 17)
        update(5, total(gmeta_ref, pl.ds(pl.multiple_of(me * N_META, N_META), N_META)), 21)
        update(6, total(ga1_ref, row(1)), 25)

    flat = []
    for trip in (*norms, pool_scale, pool_maps, meta, final_norm):
        flat += list(trip)
    out_shapes = [jax.ShapeDtypeStruct((8, 128), f32)]
    for trip in (*norms, pool_scale, pool_maps, meta, final_norm):
        out_shapes += [jax.ShapeDtypeStruct(trip[0].shape, f32)] * 4
    return pl.pallas_call(body, name=name, out_shape=out_shapes,
                          compiler_params=_cp(32))(ga0, gmaps0, gmeta, ga1, gmaps1, *flat)


def kernel(x, meta, ffn1_norm, ffn1_gate, ffn1_up, ffn1_down, mix_norm, w_in, pool_maps, pool_scale, w_ret_up, w_pool_up, w_out, ffn2_norm, ffn2_gate, ffn2_up, ffn2_down, final_norm, loss_target, m_meta, m_ffn1_norm, m_ffn1_gate, m_ffn1_up, m_ffn1_down, m_mix_norm, m_w_in, m_pool_maps, m_pool_scale, m_w_ret_up, m_w_pool_up, m_w_out, m_ffn2_norm, m_ffn2_gate, m_ffn2_up, m_ffn2_down, m_final_norm, v_meta, v_ffn1_norm, v_ffn1_gate, v_ffn1_up, v_ffn1_down, v_mix_norm, v_w_in, v_pool_maps, v_pool_scale, v_w_ret_up, v_w_pool_up, v_w_out, v_ffn2_norm, v_ffn2_gate, v_ffn2_up, v_ffn2_down, v_final_norm):
    seq, d = x.shape[1], x.shape[2]
    depth = ffn1_gate.shape[0]
    ff = ffn1_gate.shape[2] * NDEV
    nin = w_in.shape[2] * NDEV
    length = seq + N_META
    pad = (-length) % CHUNK
    tp = length + pad
    assert pad % 8 == 0 and pad + N_META == CHUNK and depth == 2 and nin == 5 * RW + 2 * d

    ix, iy, ic = _me()
    me = (4 * ix + 2 * iy + ic).astype(jnp.int32).reshape(1)

    meta_all, = all_gather("gather_meta", [meta])
    meta_full = jnp.transpose(meta_all, (1, 0, 2)).reshape(N_META, d)

    token = meta_all
    tview = lambda *arrs: [jnp.swapaxes(a, 1, 2) for a in arrs]
    t_g1, t_u1, t_g2, t_u2, t_in = (tview(w, m, v) for w, m, v in (
        (ffn1_gate, m_ffn1_gate, v_ffn1_gate), (ffn1_up, m_ffn1_up, v_ffn1_up), (ffn2_gate, m_ffn2_gate, v_ffn2_gate),
        (ffn2_up, m_ffn2_up, v_ffn2_up), (w_in, m_w_in, v_w_in)))
    keys, groups = [], []
    for layer in range(depth):
        lands = prep_layer(layer, me, [w_ret_up, w_pool_up],
                           [t_g1[0], t_u1[0], t_g2[0], t_u2[0], t_in[0], ffn1_down, ffn2_down, w_out],
                           (token,) if layer else ())
        wretT, wpoolT, g1T, u1T, g2T, u2T, winT, d1, d2, wout = lands
        keys += [("ffn1", layer), ("mix", layer), ("ffn2", layer)]
        groups += [[g1T, u1T, d1], [winT, wretT, wpoolT, wout], [g2T, u2T, d2]]
        if layer == 0:
            first, token = gather_start_chips("gather_start_first", groups[:1], (token,))
    started, token = gather_start_chips("gather_start_rest", groups[1:], (token,))
    gathers = dict(zip(keys, first + started))

    def forward(part, layer, after):
        ssem, rsem, group = gathers[(part, layer)]
        ssem, rsem, group, tok = gather_forward(f"gather_forward_{part}{layer}", ssem, rsem, group, after)
        gathers[(part, layer)] = (ssem, rsem, group)
        return tok

    def gathered(part, layer, after):
        ssem, rsem, group = gathers[(part, layer)]
        _, full = copies_wait(f"gather_wait_{part}{layer}", ssem, rsem, (), group, after, 3)
        return [a.reshape((NDEV * a.shape[1],) + a.shape[2:]) for a in full]

    cos2, sin2, dec = _retention_tables(tp, pad, token[0, 0])
    h = jnp.concatenate([jnp.zeros((pad, d), f32), meta_full + token[0, 0], x[0]], axis=0)

    saved = []
    weights = []
    tok = forward("ffn1", 0, token)
    for layer in range(depth):
        row = lambda a: a[layer:layer + 1]
        s = {"h0": h}
        g1T, u1T, d1 = gathered("ffn1", layer, tok if layer == 0 else h)
        tok = forward("mix", layer, h) if layer else None
        h, s["a1"], s["b1"] = ffn_fwd(f"ffn1_fwd{layer}", h, row(ffn1_norm), g1T, u1T, d1, (tok,) if layer else ())
        s["h1"] = h
        if layer == 0:
            tok = forward("mix", layer, h)
        winT, wretT, wpoolT, wout = gathered("mix", layer, tok if layer == 0 else h)
        s["zq"], s["zg"], zu, s["zgate"] = mix_in_fwd(f"mix_in_fwd{layer}", h, row(mix_norm), winT, cos2, sin2)
        s["oraw"], s["states"] = retention_fwd(f"retention_fwd{layer}", s["zq"], dec)
        s["pooled"], p = pool_fwd(f"pool_fwd{layer}", zu, pool_maps[layer], row(pool_scale), pad)
        tok = forward("ffn2", layer, p)
        h, s["rp"], s["mixed"] = mix_out_fwd(
            f"mix_out_fwd{layer}", h, s["oraw"], s["zg"], s["zgate"], p, wretT, wpoolT, wout, (tok,))
        s["h2"] = h
        g2T, u2T, d2 = gathered("ffn2", layer, h)
        tok = (forward("ffn1", layer + 1, h),) if layer + 1 < depth else ()
        h, s["a2"], s["b2"] = ffn_fwd(f"ffn2_fwd{layer}", h, row(ffn2_norm), g2T, u2T, d2, tok)
        saved.append(s)
        weights.append((g1T, u1T, g2T, u2T, winT, wretT, wpoolT, d1, d2, wout))

    dh, loss_part, dg_final = final_loss("final_loss", h, final_norm.reshape(1, d), loss_target[0])

    small = {}
    small_gathers = {}
    exchanges = {}
    token = None

    def rows8(vals):
        at = lax.broadcasted_iota(jnp.int32, (8, d), 0)
        out = jnp.zeros((8, d), f32)
        for k, v in enumerate(vals):
            r0 = v[0:1]
            r0 = r0 if r0.shape[1] == d else jnp.pad(r0, ((0, 0), (0, d - r0.shape[1])))
            out = jnp.where(at == k, r0, out)
        return out

    def exchange(part, layer, grads):
        by_dest = [g.reshape(g.shape[0], NDEV, g.shape[1] // NDEV, g.shape[2]) for g in grads]
        ssem, rsem, sent, lands, tok = exchange_start(f"exchange_start_{part}{layer}", by_dest)
        exchanges[(part, layer)] = (ssem, rsem, sent, lands)
        return (tok,)

    for layer in reversed(range(depth)):
        g1T, u1T, g2T, u2T, winT, wretT, wpoolT, d1, d2, wout = weights[layer]
        row = lambda a: a[layer:layer + 1]
        s = saved[layer]
        dh, lhs2, rhs2, small[("ffn2", layer)] = ffn_bwd(
            f"ffn2_bwd{layer}", dh, s["h2"], row(ffn2_norm), s["a2"], s["b2"], g2T, u2T, d2, () if token is None else token)
        token = exchange("ffn2", layer, [mm_tn(f"ffn2_wgrad{layer}", lhs2, rhs2, lambda b: b // 2)])
        do, dzg, dzgate, dp, drp, dyb = mix_out_bwd(
            f"mix_out_bwd{layer}", dh, s["oraw"], s["zg"], s["zgate"], s["rp"], wretT, wpoolT, wout, token)
        gw_mix = [mm_tn(f"w_out_wgrad{layer}", s["mixed"], dyb, lambda b: b),
                  mm_tn(f"up_wgrad{layer}", drp, s["rp"], lambda b: b)]
        dq, dk, dv = retention_bwd(f"retention_bwd{layer}", s["zq"], cos2, sin2, dec, s["states"], do, pad, token)
        dzu, small[("maps", layer)], small[("scale", layer)] = pool_bwd(
            f"pool_bwd{layer}", dp, s["pooled"], pool_maps[layer], row(pool_scale), pad)
        dh, dz, n2, small[("mix", layer)] = mix_in_bwd(
            f"mix_in_bwd{layer}", dq, dk, dv, dzg, dzu, dzgate, s["h1"], row(mix_norm), winT, dh)
        token = exchange("mix", layer, gw_mix + [mm_tn(f"w_in_wgrad{layer}", dz, n2, lambda b: b)])
        dh, lhs1, rhs1, small[("ffn1", layer)] = ffn_bwd(
            f"ffn1_bwd{layer}", dh, s["h0"], row(ffn1_norm), s["a1"], s["b1"], g1T, u1T, d1, token)
        rows = [small[("ffn1", layer)], small[("mix", layer)], small[("ffn2", layer)], small[("scale", layer)]]
        packs = [rows8([loss_part, dg_final] + rows if layer == depth - 1 else rows), small[("maps", layer)]]
        if layer == 0:
            dmeta = dh[pad:CHUNK]
            packs.append(jnp.transpose(dmeta.reshape(N_META, NDEV, d // NDEV), (1, 0, 2)).reshape(NDEV * N_META, d // NDEV))
        ssem, rsem, lands, tok = gather_start(f"small_start{layer}", slot_in(f"small_slot{layer}", me, packs))
        small_gathers[layer] = (ssem, rsem, lands)
        if layer:
            token = exchange("ffn1", layer, [mm_tn(f"ffn1_wgrad{layer}", lhs1, rhs1, lambda b: b // 2, (tok,))])
        else:
            token = (tok,)
            for j, nm in enumerate(("ffn1_gate", "ffn1_up", "ffn1_down")):
                token = exchange(nm, layer, [mm_tn(f"{nm}_wgrad{layer}", lhs1, rhs1, lambda b: b // 2, token, only=j)])

    grad_x = (dh[CHUNK:] + token[0][0, 0])[None]

    big = {}
    after = token[0]
    plans = {
        "ffn2": [("ffn2_gate", 0, 0, False, *t_g2), ("ffn2_up", 0, 1, False, *t_u2),
                 ("ffn2_down", 0, 2, False, ffn2_down, m_ffn2_down, v_ffn2_down)],
        "mix": [("w_out", 0, 0, False, w_out, m_w_out, v_w_out),
                ("w_ret_up", 1, 0, True, w_ret_up, m_w_ret_up, v_w_ret_up),
                ("w_pool_up", 1, 1, True, w_pool_up, m_w_pool_up, v_w_pool_up), ("w_in", 2, 0, False, *t_in)],
        "ffn1": [("ffn1_gate", 0, 0, False, *t_g1), ("ffn1_up", 0, 1, False, *t_u1),
                 ("ffn1_down", 0, 2, False, ffn1_down, m_ffn1_down, v_ffn1_down)]}
    for nm, k, b, tr, w, m, v in plans["ffn1"]:
        plans[nm] = [(nm, 0, 0, tr, w, m, v)]
    for layer in reversed(range(depth)):
        for part in ("ffn2", "mix") + (("ffn1",) if layer else ("ffn1_gate", "ffn1_up", "ffn1_down")):
            ssem, rsem, sent, lands = exchanges[(part, layer)]
            sent, lands = copies_wait(f"exchange_wait_{part}{layer}", ssem, rsem, sent, lands, after)
            for nm, k, b, tr, w, m, v in plans[part]:
                big[nm] = adam_big(f"adam_{nm}{layer}", me, lands[k], sent[k], b, layer, tr, w, m, v, big.get(nm))
                after = big[nm][0]

    gsmall = []
    for layer in range(depth):
        ssem, rsem, lands = small_gathers[layer]
        gsmall += copies_wait(f"small_wait{layer}", ssem, rsem, (), lands, after)[1]

    maps2 = lambda a: a.reshape(depth * HEADS * HD, HD)
    res = adam_small(
        "adam_small", *gsmall,
        [(ffn1_norm, m_ffn1_norm, v_ffn1_norm), (mix_norm, m_mix_norm, v_mix_norm), (ffn2_norm, m_ffn2_norm, v_ffn2_norm)],
        (pool_scale, m_pool_scale, v_pool_scale), (maps2(pool_maps), maps2(m_pool_maps), maps2(v_pool_maps)),
        (meta, m_meta, v_meta), tuple(a.reshape(1, d) for a in (final_norm, m_final_norm, v_final_norm)), d)
    loss = res[0][0, 0]
    sm = {}
    for k, nm in enumerate(["ffn1_norm", "mix_norm", "ffn2_norm", "pool_scale", "pool_maps", "meta", "final_norm"]):
        sm[nm] = list(res[1 + 4 * k:5 + 4 * k])
    sm["pool_maps"] = [a.reshape(pool_maps.shape) for a in sm["pool_maps"]]
    sm["final_norm"] = [a.reshape(d) for a in sm["final_norm"]]

    names = ["meta", "ffn1_norm", "ffn1_gate", "ffn1_up", "ffn1_down", "mix_norm", "w_in", "pool_maps", "pool_scale",
             "w_ret_up", "w_pool_up", "w_out", "ffn2_norm", "ffn2_gate", "ffn2_up", "ffn2_down", "final_norm"]
    for nm in ("ffn1_gate", "ffn1_up", "ffn2_gate", "ffn2_up", "w_in"):
        big[nm] = tview(*big[nm])
    allw = {**{k: list(v) for k, v in big.items()}, **sm}
    outs = [loss, grad_x]
    for kind in range(4):
        outs += [allw[nm][kind] for nm in names]
    return tuple(outs)
```

```python
import functools

import jax
import jax.numpy as jnp
from jax import lax
from jax.experimental import pallas as pl
from jax.experimental.pallas import tpu as pltpu

f32 = jnp.float32
bf16 = jnp.bfloat16
MESH = pl.DeviceIdType.MESH
NDEV = 8
N_META = 16
HEADS = 4
HD = 128
CHUNK = 128
RW = HEADS * HD
POOL_WINDOWS = (2, 4, 8, 16)
ROPE_BASE = 10000.0
EPS = 1e-6
ADAM_LR = 0.001
ADAM_B1 = 0.9
ADAM_B2 = 0.999
ADAM_EPS = 1e-08
ADAM_WD = 0.01
ADAM_STEP = 10
VMEM_CAP_MB = 60


def _cp(sem=None):
    return pltpu.CompilerParams(vmem_limit_bytes=VMEM_CAP_MB * 2**20, dimension_semantics=sem)


def _row_tile(tp, want=384):
    return want if tp % want == 0 else 128


def _resident(shape):
    nd = len(shape)
    return pl.BlockSpec(shape, lambda *_: (0,) * nd, pipeline_mode=pl.Buffered(1))


def _skip(nd, body):
    return (lambda *refs: body(*refs[nd:])) if nd else body


def _dot_nn(a, b):
    return lax.dot_general(a, b, (((1,), (0,)), ((), ())), preferred_element_type=f32)


def _dot_nt(a, b):
    return lax.dot_general(a, b, (((1,), (1,)), ((), ())), preferred_element_type=f32)


def _dot_tn(a, b):
    return lax.dot_general(a, b, (((0,), (0,)), ((), ())), preferred_element_type=f32)


def _rms(h):
    rs = lax.rsqrt(jnp.mean(h * h, axis=-1, keepdims=True) + EPS)
    return h * rs, rs


def _rms_bwd(dn, g, hhat, rs):
    dhh = dn * g
    return rs * (dhh - hhat * jnp.mean(dhh * hhat, axis=-1, keepdims=True))


def _sigmoid(x):
    return jax.nn.sigmoid(x)


def _me():
    return lax.axis_index("x"), lax.axis_index("y"), lax.axis_index("c")


def _peer(idx):
    return (idx // 4, (idx // 2) % 2, idx % 2)


def all_gather(name, arrays):
    n = len(arrays)

    def body(*refs):
        ins, outs = refs[:n], refs[n:2 * n]
        send_sems, recv_sems, local_sems = refs[2 * n:]
        x, y, c = _me()
        me = 4 * x + 2 * y + c
        locals_ = []
        for k in range(n):
            cp = pltpu.make_async_copy(ins[k], outs[k].at[me], local_sems.at[k])
            cp.start()
            locals_.append(cp)
        for d in range(1, NDEV):
            for k in range(n):
                pltpu.make_async_remote_copy(
                    src_ref=ins[k], dst_ref=outs[k].at[me], send_sem=send_sems.at[k], recv_sem=recv_sems.at[k],
                    device_id=_peer((me + d) % NDEV), device_id_type=MESH).start()
        for k in range(n):
            seven = outs[k].at[pl.ds(0, NDEV - 1)]
            w = pltpu.make_async_remote_copy(src_ref=seven, dst_ref=seven, send_sem=send_sems.at[k],
                                             recv_sem=recv_sems.at[k], device_id=(x, y, c), device_id_type=MESH)
            w.wait_send()
            w.wait_recv()
            locals_[k].wait()

    anyspec = pl.BlockSpec(memory_space=pl.ANY)
    return pl.pallas_call(
        body, name=name,
        out_shape=[jax.ShapeDtypeStruct((NDEV,) + a.shape, a.dtype) for a in arrays],
        in_specs=[anyspec] * n, out_specs=[anyspec] * n,
        scratch_shapes=[pltpu.SemaphoreType.DMA((n,)), pltpu.SemaphoreType.DMA((n,)), pltpu.SemaphoreType.DMA((n,))],
    )(*arrays)


_HBM = pl.BlockSpec(memory_space=pltpu.HBM)
_SEM = pl.BlockSpec(memory_space=pltpu.SEMAPHORE)
_ANY = pl.BlockSpec(memory_space=pl.ANY)
_EFFECT = pltpu.SideEffectType.DATAFLOW_SIDE_EFFECTING


def _in_hbm(a):
    return pltpu.with_memory_space_constraint(a, pltpu.HBM)


def gather_start(name, lands, deps=()):
    n, nd = len(lands), len(deps)

    def body(*refs):
        land = refs[nd:nd + n]
        send_sems, recv_sems = refs[nd + n:nd + n + 2]
        token = refs[-1]
        x, y, c = _me()
        me = 4 * x + 2 * y + c
        for d in range(1, NDEV):
            for k in range(n):
                pltpu.make_async_remote_copy(
                    src_ref=land[k].at[me], dst_ref=land[k].at[me], send_sem=send_sems.at[k], recv_sem=recv_sems.at[k],
                    device_id=_peer((me + d) % NDEV), device_id_type=MESH).start()
        token[...] = jnp.zeros_like(token)

    res = pl.pallas_call(
        body, name=name,
        out_shape=(pltpu.SemaphoreType.DMA((n,)), pltpu.SemaphoreType.DMA((n,)),
                   *[pltpu.HBM(a.shape, a.dtype) for a in lands], jax.ShapeDtypeStruct((8, 128), f32)),
        in_specs=[_ANY] * nd + [_HBM] * n,
        out_specs=(_SEM, _SEM, *[_HBM] * n, pl.BlockSpec(memory_space=pltpu.VMEM)),
        input_output_aliases={nd + k: 2 + k for k in range(n)},
        compiler_params=pltpu.CompilerParams(has_side_effects=_EFFECT),
    )(*deps, *[_in_hbm(a) for a in lands])
    return res[0], res[1], list(res[2:2 + n]), res[-1]


def _other_chips(x, y):
    return [(1 - x, y), (x, 1 - y), (1 - x, 1 - y)]


def gather_start_chips(name, groups, deps=()):
    sizes = [len(g) for g in groups]
    lands = [a for g in groups for a in g]
    n, nd, ng = len(lands), len(deps), len(groups)

    def body(*refs):
        land = refs[nd:nd + n]
        sems = refs[nd + n:nd + n + 2 * ng]
        token = refs[-1]
        x, y, c = _me()
        me = 4 * x + 2 * y + c
        k = 0
        for g, size in enumerate(sizes):
            for j in range(size):
                for to in [(x, y, 1 - c)] + [(cx, cy, c) for cx, cy in _other_chips(x, y)]:
                    pltpu.make_async_remote_copy(
                        src_ref=land[k].at[me], dst_ref=land[k].at[me], send_sem=sems[2 * g].at[j],
                        recv_sem=sems[2 * g + 1].at[j], device_id=to, device_id_type=MESH).start()
                k += 1
        token[...] = jnp.zeros_like(token)

    res = pl.pallas_call(
        body, name=name,
        out_shape=(*[pltpu.SemaphoreType.DMA((size,)) for size in sizes for _ in range(2)],
                   *[pltpu.HBM(a.shape, a.dtype) for a in lands], jax.ShapeDtypeStruct((8, 128), f32)),
        in_specs=[_ANY] * nd + [_HBM] * n,
        out_specs=(*[_SEM] * (2 * ng), *[_HBM] * n, pl.BlockSpec(memory_space=pltpu.VMEM)),
        input_output_aliases={nd + k: 2 * ng + k for k in range(n)},
        compiler_params=pltpu.CompilerParams(has_side_effects=_EFFECT),
    )(*deps, *[_in_hbm(a) for a in lands])
    out, k = [], 2 * ng
    for g, size in enumerate(sizes):
        out.append((res[2 * g], res[2 * g + 1], list(res[k:k + size])))
        k += size
    return out, res[-1]


def gather_forward(name, send_sems, recv_sems, lands, after):
    n = len(lands)

    def body(*refs):
        land = refs[:n]
        ssem, rsem = refs[n:n + 2]
        send2, recv2 = refs[n + 3:n + 5]
        token = refs[-1]
        x, y, c = _me()
        for k in range(n):
            four = land[k].at[pl.ds(0, 4)]
            w = pltpu.make_async_remote_copy(src_ref=four, dst_ref=four, send_sem=ssem.at[k], recv_sem=rsem.at[k],
                                             device_id=(x, y, c), device_id_type=MESH)
            w.wait_send()
            w.wait_recv()
            for cx, cy in _other_chips(x, y):
                slot = 4 * cx + 2 * cy + c
                pltpu.make_async_remote_copy(
                    src_ref=land[k].at[slot], dst_ref=land[k].at[slot], send_sem=send2.at[k], recv_sem=recv2.at[k],
                    device_id=(x, y, 1 - c), device_id_type=MESH).start()
        token[...] = jnp.zeros_like(token)

    res = pl.pallas_call(
        body, name=name,
        out_shape=(pltpu.SemaphoreType.DMA((n,)), pltpu.SemaphoreType.DMA((n,)),
                   *[pltpu.HBM(a.shape, a.dtype) for a in lands], jax.ShapeDtypeStruct((8, 128), f32)),
        in_specs=[_HBM] * n + [_SEM, _SEM, _ANY],
        out_specs=(_SEM, _SEM, *[_HBM] * n, pl.BlockSpec(memory_space=pltpu.VMEM)),
        input_output_aliases={k: 2 + k for k in range(n)},
        compiler_params=pltpu.CompilerParams(has_side_effects=_EFFECT),
    )(*lands, send_sems, recv_sems, after)
    return res[0], res[1], list(res[2:2 + n]), res[-1]


def exchange_start(name, grads, deps=()):
    n, nd = len(grads), len(deps)
    lands = [lax.empty((NDEV - 1, g.shape[0]) + g.shape[2:], g.dtype) for g in grads]

    def body(*refs):
        src = refs[nd:nd + n]
        land = refs[nd + n:nd + 2 * n]
        send_sems, recv_sems = refs[nd + 2 * n:nd + 2 * n + 2]
        token = refs[-1]
        x, y, c = _me()
        me = 4 * x + 2 * y + c
        for d in range(1, NDEV):
            p = (me + d) % NDEV
            for k in range(n):
                pltpu.make_async_remote_copy(
                    src_ref=src[k].at[:, p], dst_ref=land[k].at[d - 1], send_sem=send_sems.at[k], recv_sem=recv_sems.at[k],
                    device_id=_peer(p), device_id_type=MESH).start()
        token[...] = jnp.zeros_like(token)

    both = list(grads) + lands
    res = pl.pallas_call(
        body, name=name,
        out_shape=(pltpu.SemaphoreType.DMA((n,)), pltpu.SemaphoreType.DMA((n,)),
                   *[pltpu.HBM(a.shape, a.dtype) for a in both], jax.ShapeDtypeStruct((8, 128), f32)),
        in_specs=[_ANY] * nd + [_HBM] * (2 * n),
        out_specs=(_SEM, _SEM, *[_HBM] * (2 * n), pl.BlockSpec(memory_space=pltpu.VMEM)),
        input_output_aliases={nd + k: 2 + k for k in range(2 * n)},
        compiler_params=pltpu.CompilerParams(has_side_effects=_EFFECT),
    )(*deps, *[_in_hbm(a) for a in both])
    return res[0], res[1], list(res[2:2 + n]), list(res[2 + n:2 + 2 * n]), res[-1]


def copies_wait(name, send_sems, recv_sems, sent, lands, after, count=NDEV - 1):
    ns, n = len(sent), len(lands)

    def body(*refs):
        land = refs[ns:ns + n]
        ssem, rsem = refs[ns + n:ns + n + 2]
        x, y, c = _me()
        for k in range(n):
            seven = land[k].at[pl.ds(0, count)]
            w = pltpu.make_async_remote_copy(src_ref=seven, dst_ref=seven, send_sem=ssem.at[k], recv_sem=rsem.at[k],
                                             device_id=(x, y, c), device_id_type=MESH)
            w.wait_send()
            w.wait_recv()

    both = list(sent) + list(lands)
    res = pl.pallas_call(
        body, name=name, out_shape=tuple(pltpu.HBM(a.shape, a.dtype) for a in both),
        in_specs=[_HBM] * (ns + n) + [_SEM, _SEM, _ANY], out_specs=tuple([_HBM] * (ns + n)),
        input_output_aliases={k: k for k in range(ns + n)},
        compiler_params=pltpu.CompilerParams(has_side_effects=_EFFECT),
    )(*both, send_sems, recv_sems, after)
    return list(res[:ns]), list(res[ns:])


def prep_layer(layer, me, col_sharded, row_sharded, deps=()):
    nc, nr = len(col_sharded), len(row_sharded)

    def body(me_ref, *refs):
        ins, outs = refs[:nc + nr], refs[nc + nr + len(deps):]
        for k in range(nc):
            outs[k][...] = ins[k][...].T.astype(bf16)
        for k in range(nc, nc + nr):
            outs[k][...] = ins[k][...].astype(bf16)

    arrs = list(col_sharded) + list(row_sharded)
    in_specs = [pl.BlockSpec((None,) + a.shape[1:], lambda i, me_ref: (layer, 0, 0)) for a in arrs]
    shapes = [(a.shape[2], a.shape[1]) for a in col_sharded] + [a.shape[1:] for a in row_sharded]
    out_specs = [pl.BlockSpec((None,) + s, lambda i, me_ref: (me_ref[0], 0, 0)) for s in shapes]
    return pl.pallas_call(
        body, name=f"prep_layer{layer}",
        grid_spec=pltpu.PrefetchScalarGridSpec(num_scalar_prefetch=1, grid=(1,), in_specs=in_specs + [_ANY] * len(deps),
                                               out_specs=out_specs),
        out_shape=[jax.ShapeDtypeStruct((NDEV,) + s, bf16) for s in shapes], compiler_params=_cp())(me, *arrs, *deps)


def slot_in(name, me, arrays):
    n = len(arrays)

    def body(me_ref, *refs):
        for k in range(n):
            refs[n + k][...] = refs[k][...]

    in_specs = [pl.BlockSpec(a.shape, lambda i, me_ref: (0, 0)) for a in arrays]
    out_specs = [pl.BlockSpec((None,) + a.shape, lambda i, me_ref: (me_ref[0], 0, 0)) for a in arrays]
    return pl.pallas_call(
        body, name=name,
        grid_spec=pltpu.PrefetchScalarGridSpec(num_scalar_prefetch=1, grid=(1,), in_specs=in_specs, out_specs=out_specs),
        out_shape=[jax.ShapeDtypeStruct((NDEV,) + a.shape, a.dtype) for a in arrays])(me, *arrays)


def _ff_chunks(ff, want=768):
    if ff % 256:
        return [slice(0, ff)]
    return [slice(c, min(c + want, ff)) for c in range(0, ff, want)]


def ffn_fwd(name, h, g, wgT, wuT, wd, deps=()):
    tp, d = h.shape
    ff = wgT.shape[0]
    tm = _row_tile(tp, 704)

    def body(h_ref, g_ref, wg_ref, wu_ref, wd_ref, ho_ref, a_ref, b_ref):
        hh = h_ref[...]
        hhat, _ = _rms(hh)
        n = (hhat * g_ref[...]).astype(bf16)
        acc = None
        for cols in _ff_chunks(ff):
            a = _dot_nt(n, wg_ref[cols, :])
            b = _dot_nt(n, wu_ref[cols, :])
            part = _dot_nn(((a * _sigmoid(a)) * b).astype(bf16), wd_ref[cols, :])
            acc = part if acc is None else acc + part
            a_ref[:, cols] = a.astype(bf16)
            b_ref[:, cols] = b.astype(bf16)
        ho_ref[...] = hh + 0.5 * acc

    row = lambda w: pl.BlockSpec((tm, w), lambda i: (i, 0))
    return pl.pallas_call(
        _skip(len(deps), body), name=name, grid=(tp // tm,),
        in_specs=[_ANY] * len(deps) + [row(d), _resident((1, d)), _resident((ff, d)), _resident((ff, d)), _resident((ff, d))],
        out_specs=[row(d), row(ff), row(ff)],
        out_shape=[jax.ShapeDtypeStruct((tp, d), f32), jax.ShapeDtypeStruct((tp, ff), bf16),
                   jax.ShapeDtypeStruct((tp, ff), bf16)],
        compiler_params=_cp(("arbitrary",)))(*deps, h, g, wgT, wuT, wd)


def ffn_bwd(name, dy, h, g, a, b, wgT, wuT, wd, deps=()):
    tp, d = h.shape
    ff = wgT.shape[0]
    tm = _row_tile(tp, 384)

    def body(dy_ref, h_ref, g_ref, a_ref, b_ref, wg_ref, wu_ref, wd_ref, dh_ref, lhs_ref, rhs_ref, dg_ref):
        dyv = dy_ref[...]
        hhat, rs = _rms(h_ref[...])
        gv = g_ref[...]
        n = hhat * gv
        dyh = (0.5 * dyv).astype(bf16)
        dn = None
        for cols in _ff_chunks(ff):
            ds = _dot_nt(dyh, wd_ref[cols, :])
            av = a_ref[:, cols].astype(f32)
            bv = b_ref[:, cols].astype(f32)
            sg = _sigmoid(av)
            sa = av * sg
            da = (ds * bv * (sg * (1.0 + av * (1.0 - sg)))).astype(bf16)
            db = (ds * sa).astype(bf16)
            part = _dot_nn(da, wg_ref[cols, :]) + _dot_nn(db, wu_ref[cols, :])
            dn = part if dn is None else dn + part
            lhs_ref[0, :, cols] = da
            lhs_ref[1, :, cols] = db
            lhs_ref[2, :, cols] = (sa * bv).astype(bf16)
        dh_ref[...] = dyv + _rms_bwd(dn, gv, hhat, rs)

        @pl.when(pl.program_id(0) == 0)
        def _():
            dg_ref[...] = jnp.zeros_like(dg_ref)

        dg_ref[0:1, :] += jnp.sum(dn * hhat, axis=0, keepdims=True)
        rhs_ref[0] = n.astype(bf16)
        rhs_ref[1] = dyh

    row = lambda w: pl.BlockSpec((tm, w), lambda i: (i, 0))
    return pl.pallas_call(
        _skip(len(deps), body), name=name, grid=(tp // tm,),
        in_specs=[_ANY] * len(deps) + [row(d), row(d), _resident((1, d)), row(ff), row(ff),
                  _resident((ff, d)), _resident((ff, d)), _resident((ff, d))],
        out_specs=[row(d), pl.BlockSpec((3, tm, ff), lambda i: (0, i, 0)), pl.BlockSpec((2, tm, d), lambda i: (0, i, 0)),
                   pl.BlockSpec((8, d), lambda i: (0, 0))],
        out_shape=[jax.ShapeDtypeStruct((tp, d), f32), jax.ShapeDtypeStruct((3, tp, ff), bf16),
                   jax.ShapeDtypeStruct((2, tp, d), bf16), jax.ShapeDtypeStruct((8, d), f32)],
        compiler_params=_cp(("arbitrary",)))(*deps, dy, h, g, a, b, wgT, wuT, wd)


def mm_tn(name, lhs, rhs, rhs_of, deps=(), only=None):
    _, tp, m = lhs.shape
    b0, nb = (0, lhs.shape[0]) if only is None else (only, 1)
    n = rhs.shape[2]
    def fits(t, ms):
        mb = m // ms
        return (tp % t == 0 and m % (128 * ms) == 0
                and 2 * t * (mb + n) * 2 + mb * n * (2 * 2 + 4 + (4 if t < tp else 0)) <= 54 * 2**20)

    tk, msplit = next(((t, ms) for t in (tp, 1408, 704, 384) for ms in (1, 2, 4) if fits(t, ms)), (128, 1))
    nk = tp // tk
    mb = m // msplit

    def body(l_ref, r_ref, o_ref, acc_ref):
        if nk == 1:
            o_ref[...] = _dot_tn(l_ref[...], r_ref[...]).astype(o_ref.dtype)
            return
        k = pl.program_id(2)

        @pl.when(k == 0)
        def _():
            acc_ref[...] = jnp.zeros_like(acc_ref)

        acc_ref[...] += _dot_tn(l_ref[...], r_ref[...])

        @pl.when(k == nk - 1)
        def _():
            o_ref[...] = acc_ref[...].astype(o_ref.dtype)

    return pl.pallas_call(
        _skip(len(deps), body), name=name, grid=(nb, msplit, nk),
        in_specs=[_ANY] * len(deps) + [pl.BlockSpec((None, tk, mb), lambda b, j, k: (b0 + b, k, j)),
                                       pl.BlockSpec((None, tk, n), lambda b, j, k: (rhs_of(b0 + b), k, 0))],
        out_specs=pl.BlockSpec((None, mb, n), lambda b, j, k: (b, j, 0)),
        out_shape=jax.ShapeDtypeStruct((nb, m, n), bf16),
        scratch_shapes=[pltpu.VMEM((mb, n) if nk > 1 else (8, 128), f32)],
        compiler_params=_cp(("arbitrary", "arbitrary", "arbitrary")))(*deps, lhs, rhs)


def mix_in_fwd(name, h, g, winT, cos2, sin2):
    tp, d = h.shape
    nin = winT.shape[0]
    tm = _row_tile(tp)

    def body(h_ref, g_ref, w_ref, cos_ref, sin_ref, zq_ref, zg_ref, zu_ref, zgate_ref):
        hhat, _ = _rms(h_ref[...])
        z = _dot_nt((hhat * g_ref[...]).astype(bf16), w_ref[...])
        cosv, sinv = cos_ref[...], sin_ref[...]
        for hh in range(HEADS):
            qcols, kcols = slice(hh * HD, (hh + 1) * HD), slice(RW + hh * HD, RW + (hh + 1) * HD)
            zq_ref[:, qcols] = (_rot(z[:, qcols], cosv, sinv) * HD ** -0.5).astype(bf16)
            zq_ref[:, kcols] = _rot(z[:, kcols], cosv, sinv).astype(bf16)
        zq_ref[:, 2 * RW:] = z[:, 2 * RW:3 * RW].astype(bf16)
        zg_ref[...] = z[:, 3 * RW:4 * RW].astype(zg_ref.dtype)
        zu_ref[...] = z[:, 4 * RW:5 * RW]
        zgate_ref[...] = z[:, 5 * RW:].astype(zgate_ref.dtype)

    row = lambda w: pl.BlockSpec((tm, w), lambda i: (i, 0))
    widths = (3 * RW, RW, RW, 2 * d)
    return pl.pallas_call(
        body, name=name, grid=(tp // tm,),
        in_specs=[row(d), _resident((1, d)), _resident((nin, d)), row(HD), row(HD)],
        out_specs=[row(w) for w in widths],
        out_shape=[jax.ShapeDtypeStruct((tp, w), dt) for w, dt in zip(widths, (bf16, bf16, f32, bf16))],
        compiler_params=_cp(("arbitrary",)))(h, g, winT, cos2, sin2)


def mix_in_bwd(name, dq, dk, dv, dzg, dzu, dzgate, h, g, winT, dres):
    tp, d = h.shape
    nin = winT.shape[0]
    tm = _row_tile(tp)

    def body(dq_ref, dk_ref, dv_ref, dzg_ref, dzu_ref, dzgate_ref, h_ref, g_ref, w_ref, dres_ref, dh_ref, dz_ref, n_ref, dg_ref):
        dn, col = None, 0
        for piece in (dq_ref, dk_ref, dv_ref, dzg_ref, dzu_ref, dzgate_ref):
            v = piece[...]
            part = _dot_nn(v, w_ref[col:col + v.shape[1], :])
            dn = part if dn is None else dn + part
            dz_ref[:, col:col + v.shape[1]] = v
            col += v.shape[1]
        hhat, rs = _rms(h_ref[...])
        gv = g_ref[...]
        dh_ref[...] = dres_ref[...] + _rms_bwd(dn, gv, hhat, rs)

        @pl.when(pl.program_id(0) == 0)
        def _():
            dg_ref[...] = jnp.zeros_like(dg_ref)

        dg_ref[0:1, :] += jnp.sum(dn * hhat, axis=0, keepdims=True)
        n_ref[...] = (hhat * gv).astype(bf16)

    row = lambda w: pl.BlockSpec((tm, w), lambda i: (i, 0))
    return pl.pallas_call(
        body, name=name, grid=(tp // tm,),
        in_specs=[row(RW)] * 5 + [row(2 * d), row(d), _resident((1, d)), _resident((nin, d)), row(d)],
        out_specs=[row(d), pl.BlockSpec((None, tm, nin), lambda i: (0, i, 0)), pl.BlockSpec((None, tm, d), lambda i: (0, i, 0)),
                   pl.BlockSpec((8, d), lambda i: (0, 0))],
        out_shape=[jax.ShapeDtypeStruct((tp, d), f32), jax.ShapeDtypeStruct((1, tp, nin), bf16),
                   jax.ShapeDtypeStruct((1, tp, d), bf16), jax.ShapeDtypeStruct((8, d), f32)],
        compiler_params=_cp(("arbitrary",)))(dq, dk, dv, dzg, dzu, dzgate, h, g, winT, dres)


def _retention_tables(tp, pad, zero):
    half = HD // 2
    inv_freq = ROPE_BASE ** (-jnp.arange(half, dtype=f32) / half)
    pos = jnp.arange(tp, dtype=f32) - pad + zero
    ang = pos[:, None] * inv_freq[None, :]
    cos, sin = jnp.cos(ang), jnp.sin(ang)
    cos2 = jnp.concatenate([cos, cos], axis=-1)
    sin2 = jnp.concatenate([-sin, sin], axis=-1)
    log_gamma = jnp.log1p(-(2.0 ** (-5.0 - jnp.arange(HEADS, dtype=f32))))
    idx = jnp.arange(CHUNK, dtype=f32)
    diff = idx[:, None] - idx[None, :]
    intra = jnp.where(diff[None] >= 0, jnp.exp(diff[None] * log_gamma[:, None, None]), 0.0)
    k_decay = jnp.exp((CHUNK - 1.0 - idx)[None, :] * log_gamma[:, None])
    q_decay = jnp.exp((idx + 1.0)[None, :] * log_gamma[:, None])
    chunk_decay = jnp.exp(CHUNK * log_gamma)
    full = (HEADS, CHUNK, HD)
    dec = jnp.stack([intra, jnp.broadcast_to(k_decay[:, :, None], full), jnp.broadcast_to(q_decay[:, :, None], full),
                     jnp.broadcast_to(chunk_decay[:, None, None], full)], axis=1)
    return cos2, sin2, dec


def _rot(t, cos2, sin2):
    return t * cos2 + pltpu.roll(t, HD // 2, 1) * sin2


def _rot_t(t, cos2, sin2):
    return t * cos2 - pltpu.roll(t, HD // 2, 1) * sin2


def _chunks_per_step(nch):
    return 3 if nch % 3 == 0 else 1


def retention_fwd(name, zq, dec):
    tp = zq.shape[0]
    nch = tp // CHUNK
    per = _chunks_per_step(nch)

    def body(q_ref, k_ref, v_ref, dec_ref, out_ref, st_ref, s_ref):
        @pl.when(pl.program_id(0) == 0)
        def _():
            s_ref[...] = jnp.zeros_like(s_ref)

        state = [s_ref[hh] for hh in range(HEADS)]
        for j in range(per):
            rows = slice(j * CHUNK, (j + 1) * CHUNK)
            for hh in range(HEADS):
                cols = slice(hh * HD, (hh + 1) * HD)
                qb, kb, vb = q_ref[rows, cols], k_ref[rows, cols], v_ref[rows, cols]
                sc = (_dot_nt(qb, kb) * dec_ref[hh, 0]).astype(bf16)
                sb = state[hh].astype(bf16)
                cross = _dot_nn((qb.astype(f32) * dec_ref[hh, 2]).astype(bf16), sb)
                out_ref[rows, cols] = (_dot_nn(sc, vb) + cross).astype(out_ref.dtype)
                st_ref[hh, j] = sb
                state[hh] = state[hh] * dec_ref[hh, 3] + _dot_tn((kb.astype(f32) * dec_ref[hh, 1]).astype(bf16), vb)
        for hh in range(HEADS):
            s_ref[hh] = state[hh]

    part = lambda j: pl.BlockSpec((per * CHUNK, RW), lambda n: (n, j))
    return pl.pallas_call(
        body, name=name, grid=(nch // per,),
        in_specs=[part(0), part(1), part(2), _resident((HEADS, 4, CHUNK, HD))],
        out_specs=[part(0), pl.BlockSpec((HEADS, per, HD, HD), lambda n: (0, n, 0, 0))],
        out_shape=[jax.ShapeDtypeStruct((tp, RW), bf16), jax.ShapeDtypeStruct((HEADS, nch, HD, HD), bf16)],
        scratch_shapes=[pltpu.VMEM((HEADS, HD, HD), f32)],
        compiler_params=_cp(("arbitrary",)))(zq, zq, zq, dec)


def retention_bwd(name, zq, cos2, sin2, dec, states, dout, pad, deps=()):
    tp = zq.shape[0]
    nch = tp // CHUNK
    per = _chunks_per_step(nch)
    nblk = nch // per
    scale = HD ** -0.5

    def body(q_ref, k_ref, v_ref, cos_ref, sin_ref, dec_ref, st_ref, do_ref, dq_ref, dk_ref, dv_ref, g_ref):
        @pl.when(pl.program_id(0) == 0)
        def _():
            g_ref[...] = jnp.zeros_like(g_ref)

        first_row = (nblk - 1 - pl.program_id(0)) * (per * CHUNK)
        gstate = [g_ref[hh] for hh in range(HEADS)]
        for j in reversed(range(per)):
            rows = slice(j * CHUNK, (j + 1) * CHUNK)
            cosv, sinv = cos_ref[rows, :], sin_ref[rows, :]
            keep = (lax.broadcasted_iota(jnp.int32, (CHUNK, HD), 0) + (first_row + j * CHUNK)) >= pad
            for hh in range(HEADS):
                cols = slice(hh * HD, (hh + 1) * HD)
                intra, kdec, qdec = dec_ref[hh, 0], dec_ref[hh, 1], dec_ref[hh, 2]
                qb, kb, vb = q_ref[rows, cols], k_ref[rows, cols], v_ref[rows, cols]
                qd = (qb.astype(f32) * qdec).astype(bf16)
                kd = (kb.astype(f32) * kdec).astype(bf16)
                sc = (_dot_nt(qb, kb) * intra).astype(bf16)
                dob = do_ref[rows, cols]
                sb = st_ref[hh, j]
                gb = gstate[hh].astype(bf16)
                dsc = (_dot_nt(dob, vb) * intra).astype(bf16)
                dv = _dot_tn(sc, dob) + _dot_nn(kd, gb)
                dqr = _dot_nn(dsc, kb) + _dot_nt(dob, sb) * qdec
                dkr = _dot_tn(dsc, qb) + _dot_nt(vb, gb) * kdec
                gstate[hh] = gstate[hh] * dec_ref[hh, 3] + _dot_tn(qd, dob)
                dq_ref[rows, cols] = jnp.where(keep, _rot_t(dqr * scale, cosv, sinv), 0.0).astype(bf16)
                dk_ref[rows, cols] = jnp.where(keep, _rot_t(dkr, cosv, sinv), 0.0).astype(bf16)
                dv_ref[rows, cols] = jnp.where(keep, dv, 0.0).astype(bf16)
        for hh in range(HEADS):
            g_ref[hh] = gstate[hh]

    part = lambda j: pl.BlockSpec((per * CHUNK, RW), lambda t: (nblk - 1 - t, j))
    table = pl.BlockSpec((per * CHUNK, HD), lambda t: (nblk - 1 - t, 0))
    return pl.pallas_call(
        _skip(len(deps), body), name=name, grid=(nblk,),
        in_specs=[_ANY] * len(deps) + [part(0), part(1), part(2), table, table, _resident((HEADS, 4, CHUNK, HD)),
                                       pl.BlockSpec((HEADS, per, HD, HD), lambda t: (0, nblk - 1 - t, 0, 0)), part(0)],
        out_specs=[part(0)] * 3,
        out_shape=[jax.ShapeDtypeStruct((tp, RW), bf16)] * 3,
        scratch_shapes=[pltpu.VMEM((HEADS, HD, HD), f32)],
        compiler_params=_cp(("arbitrary",)))(*deps, zq, zq, zq, cos2, sin2, dec, states, dout)


def _window_sum(xv, steps, tp, forward):
    s = xv
    for j in range(steps):
        sh = 2 ** j
        s = s + pltpu.roll(s, (tp - sh) if forward else sh, 0)
    return s


def pool_fwd(name, zu, maps, scale, pad):
    tp = zu.shape[0]

    def body(u_ref, maps_ref, scale_ref, pooled_ref, p_ref):
        row = lax.broadcasted_iota(jnp.int32, (tp, HD), 0)
        for gi, w in enumerate(POOL_WINDOWS):
            cols = slice(gi * HD, (gi + 1) * HD)
            xv = u_ref[:, cols]
            cnt = jnp.clip(row - (pad - 1), 1, w).astype(f32)
            pooled = jnp.where(row >= pad, _window_sum(xv, gi + 1, tp, False) / cnt - xv, 0.0).astype(bf16)
            pooled_ref[:, cols] = pooled
            p_ref[:, cols] = (_dot_nn(pooled, maps_ref[gi].astype(bf16)) * scale_ref[:, cols]).astype(bf16)

    return pl.pallas_call(
        body, name=name,
        out_shape=[jax.ShapeDtypeStruct((tp, RW), bf16), jax.ShapeDtypeStruct((tp, RW), bf16)],
        compiler_params=_cp())(zu, maps, scale)


def pool_bwd(name, dp, pooled, maps, scale, pad):
    tp = dp.shape[0]

    def body(dp_ref, pooled_ref, maps_ref, scale_ref, du_ref, dmaps_ref, dscale_ref):
        row = lax.broadcasted_iota(jnp.int32, (tp, HD), 0)
        dscale_ref[...] = jnp.zeros_like(dscale_ref)
        for gi, w in enumerate(POOL_WINDOWS):
            cols = slice(gi * HD, (gi + 1) * HD)
            mb = maps_ref[gi].astype(bf16)
            pooled = pooled_ref[:, cols]
            dpf = dp_ref[:, cols].astype(f32)
            dscale_ref[0:1, cols] = jnp.sum(dpf * _dot_nn(pooled, mb), axis=0, keepdims=True)
            dpm = (dpf * scale_ref[:, cols]).astype(bf16)
            dmaps_ref[gi * HD:(gi + 1) * HD, :] = _dot_tn(pooled, dpm)
            dpool = jnp.where(row >= pad, _dot_nt(dpm, mb), 0.0)
            cnt = jnp.clip(row - (pad - 1), 1, w).astype(f32)
            du = _window_sum(dpool / cnt, gi + 1, tp, True) - dpool
            du_ref[:, cols] = jnp.where(row >= pad, du, 0.0).astype(bf16)

    return pl.pallas_call(
        body, name=name,
        out_shape=[jax.ShapeDtypeStruct((tp, RW), bf16), jax.ShapeDtypeStruct((HEADS * HD, HD), f32),
                   jax.ShapeDtypeStruct((8, RW), f32)],
        compiler_params=_cp())(dp, pooled, maps, scale)


def _group_norm(o):
    mu = jnp.mean(o, axis=-1, keepdims=True)
    oc = o - mu
    rstd = lax.rsqrt(jnp.mean(oc * oc, axis=-1, keepdims=True) + EPS)
    return oc * rstd, rstd


def mix_out_fwd(name, h, oraw, zg, zgate, p, wretT, wpoolT, wout, deps=()):
    tp, d = h.shape
    tm = _row_tile(tp)

    def body(h_ref, o_ref, zg_ref, zgate_ref, p_ref, wr_ref, wp_ref, wo_ref, ho_ref, rp_ref, mixed_ref):
        parts = []
        for hh in range(HEADS):
            cols = slice(hh * HD, (hh + 1) * HD)
            rhat, _ = _group_norm(o_ref[:, cols].astype(f32))
            gv = zg_ref[:, cols].astype(f32)
            parts.append(rhat * (gv * _sigmoid(gv)))
        r = jnp.concatenate(parts, axis=-1).astype(bf16)
        pv = p_ref[...]
        ret = _dot_nt(r, wr_ref[...])
        pool = _dot_nt(pv, wp_ref[...])
        mixed = (_sigmoid(zgate_ref[:, :d].astype(f32)) * ret + _sigmoid(zgate_ref[:, d:].astype(f32)) * pool).astype(bf16)
        ho_ref[...] = h_ref[...] + _dot_nn(mixed, wo_ref[...])
        rp_ref[0] = r
        rp_ref[1] = pv
        mixed_ref[...] = mixed

    row = lambda w: pl.BlockSpec((tm, w), lambda i: (i, 0))
    return pl.pallas_call(
        _skip(len(deps), body), name=name, grid=(tp // tm,),
        in_specs=[_ANY] * len(deps) + [row(d), row(RW), row(RW), row(2 * d), row(RW), _resident((d, RW)), _resident((d, RW)),
                                       _resident((d, d))],
        out_specs=[row(d), pl.BlockSpec((2, tm, RW), lambda i: (0, i, 0)), pl.BlockSpec((None, tm, d), lambda i: (0, i, 0))],
        out_shape=[jax.ShapeDtypeStruct((tp, d), f32), jax.ShapeDtypeStruct((2, tp, RW), bf16),
                   jax.ShapeDtypeStruct((1, tp, d), bf16)],
        compiler_params=_cp(("arbitrary",)))(*deps, h, oraw, zg, zgate, p, wretT, wpoolT, wout)


def mix_out_bwd(name, dy, oraw, zg, zgate, rp, wretT, wpoolT, wout, deps=()):
    tp, d = dy.shape
    tm = _row_tile(tp)

    def body(dy_ref, o_ref, zg_ref, zgate_ref, rp_ref, wr_ref, wp_ref, wo_ref,
             do_ref, dzg_ref, dzgate_ref, dp_ref, drp_ref, dyb_ref):
        dyb = dy_ref[...].astype(bf16)
        dmixed = _dot_nt(dyb, wo_ref[...])
        sa = _sigmoid(zgate_ref[:, :d].astype(f32))
        sb = _sigmoid(zgate_ref[:, d:].astype(f32))
        dret = dmixed * sa
        dpool = dmixed * sb
        dzgate_ref[:, :d] = (dret * _dot_nt(rp_ref[0], wr_ref[...]) * (1.0 - sa)).astype(bf16)
        dzgate_ref[:, d:] = (dpool * _dot_nt(rp_ref[1], wp_ref[...]) * (1.0 - sb)).astype(bf16)
        dretb, dpoolb = dret.astype(bf16), dpool.astype(bf16)
        dr = _dot_nn(dretb, wr_ref[...])
        dp_ref[...] = _dot_nn(dpoolb, wp_ref[...]).astype(bf16)
        for hh in range(HEADS):
            cols = slice(hh * HD, (hh + 1) * HD)
            rhat, rstd = _group_norm(o_ref[:, cols].astype(f32))
            gv = zg_ref[:, cols].astype(f32)
            sg = _sigmoid(gv)
            drh = dr[:, cols]
            drhat = drh * (gv * sg)
            dzg_ref[:, cols] = (drh * rhat * (sg * (1.0 + gv * (1.0 - sg)))).astype(bf16)
            do = rstd * (drhat - jnp.mean(drhat, axis=-1, keepdims=True)
                         - rhat * jnp.mean(drhat * rhat, axis=-1, keepdims=True))
            do_ref[:, cols] = do.astype(bf16)
        drp_ref[0] = dretb
        drp_ref[1] = dpoolb
        dyb_ref[...] = dyb

    row = lambda w: pl.BlockSpec((tm, w), lambda i: (i, 0))
    return pl.pallas_call(
        _skip(len(deps), body), name=name, grid=(tp // tm,),
        in_specs=[_ANY] * len(deps) + [row(d), row(RW), row(RW), row(2 * d), pl.BlockSpec((2, tm, RW), lambda i: (0, i, 0)),
                                       _resident((d, RW)), _resident((d, RW)), _resident((d, d))],
        out_specs=[row(RW), row(RW), row(2 * d), row(RW), pl.BlockSpec((2, tm, d), lambda i: (0, i, 0)),
                   pl.BlockSpec((None, tm, d), lambda i: (0, i, 0))],
        out_shape=[jax.ShapeDtypeStruct((tp, RW), bf16), jax.ShapeDtypeStruct((tp, RW), bf16),
                   jax.ShapeDtypeStruct((tp, 2 * d), bf16), jax.ShapeDtypeStruct((tp, RW), bf16),
                   jax.ShapeDtypeStruct((2, tp, d), bf16), jax.ShapeDtypeStruct((1, tp, d), bf16)],
        compiler_params=_cp(("arbitrary",)))(*deps, dy, oraw, zg, zgate, rp, wretT, wpoolT, wout)


def final_loss(name, h, g, target):
    tp, d = h.shape
    tm = _row_tile(tp)
    nsub = tm // CHUNK

    def body(h_ref, g_ref, *rest):
        t_refs = rest[:nsub]
        dh_ref, loss_ref, dg_ref = rest[nsub:]
        i = pl.program_id(0)

        @pl.when(i == 0)
        def _():
            loss_ref[...] = jnp.zeros_like(loss_ref)
            dg_ref[...] = jnp.zeros_like(dg_ref)

        gv = g_ref[...]
        for j in range(nsub):
            rows = slice(j * CHUNK, (j + 1) * CHUNK)
            hhat, rs = _rms(h_ref[rows, :])
            err = jnp.where(i * nsub + j >= 1, hhat * gv - t_refs[j][...], 0.0)
            dyv = err / d
            dh_ref[rows, :] = _rms_bwd(dyv, gv, hhat, rs)
            loss_ref[...] += 0.5 * jnp.sum(jnp.sum(err * err, axis=-1, keepdims=True) / d)
            dg_ref[0:1, :] += jnp.sum(dyv * hhat, axis=0, keepdims=True)

    lagged = lambda j: pl.BlockSpec((CHUNK, d), lambda i: (jnp.maximum(i * nsub + j - 1, 0), 0))
    return pl.pallas_call(
        body, name=name, grid=(tp // tm,),
        in_specs=[pl.BlockSpec((tm, d), lambda i: (i, 0)), _resident((1, d))] + [lagged(j) for j in range(nsub)],
        out_specs=[pl.BlockSpec((tm, d), lambda i: (i, 0)), pl.BlockSpec((8, 128), lambda i: (0, 0)),
                   pl.BlockSpec((8, d), lambda i: (0, 0))],
        out_shape=[jax.ShapeDtypeStruct((tp, d), f32), jax.ShapeDtypeStruct((8, 128), f32),
                   jax.ShapeDtypeStruct((8, d), f32)],
        compiler_params=_cp(("arbitrary",)))(h, g, *[target] * nsub)


def _adamw(w, g, m, v):
    m = ADAM_B1 * m + (1.0 - ADAM_B1) * g
    v = ADAM_B2 * v + (1.0 - ADAM_B2) * (g * g)
    m_hat = m / (1.0 - ADAM_B1 ** ADAM_STEP)
    v_hat = v / (1.0 - ADAM_B2 ** ADAM_STEP)
    delta = -ADAM_LR * (m_hat / (jnp.sqrt(v_hat) + ADAM_EPS) + ADAM_WD * w)
    return delta, m, v


def adam_big(name, me, recv, own, b, layer, transposed, w, m, v, prev):
    r, c = recv.shape[2:]
    wshape = w.shape[1:]
    nchunk = 1 if transposed else next(k for k in (4, 2, 1) if r % (16 * k) == 0)
    rc = r // nchunk

    def body(me_ref, recv_ref, own_ref, w_ref, m_ref, v_ref, *rest):
        g_ref, d_ref, nm_ref, nv_ref = rest[-4:]
        g = own_ref[...].astype(f32)
        for j in range(NDEV - 1):
            g = g + recv_ref[j].astype(f32)
        if transposed:
            g = g.T
        delta, nm, nv = _adamw(w_ref[...], g, m_ref[...], v_ref[...])
        g_ref[...] = g
        d_ref[...] = delta
        nm_ref[...] = nm
        nv_ref[...] = nv

    wblock = wshape if transposed else (rc, c)
    wspec = pl.BlockSpec((None,) + wblock, lambda i, me_ref: (layer, i, 0))
    in_specs = [pl.BlockSpec((NDEV - 1, None, rc, c), lambda i, me_ref: (0, b, i, 0)),
                pl.BlockSpec((None, None, rc, c), lambda i, me_ref: (b, me_ref[0], i, 0)), wspec, wspec, wspec]
    args = [recv, own, w, m, v]
    aliases = {}
    if prev is not None:
        in_specs += [_ANY] * 4
        args += list(prev)
        aliases = {6 + k: k for k in range(4)}
    return pl.pallas_call(
        body, name=name,
        grid_spec=pltpu.PrefetchScalarGridSpec(num_scalar_prefetch=1, grid=(nchunk,), in_specs=in_specs,
                                               out_specs=[wspec] * 4),
        out_shape=[jax.ShapeDtypeStruct(w.shape, f32)] * 4, input_output_aliases=aliases,
        compiler_params=_cp())(me, *args)


def adam_small(name, ga0, gmaps0, gmeta, ga1, gmaps1, norms, pool_scale, pool_maps, meta, final_norm, d):
    def body(ga0_ref, gmaps0_ref, gmeta_ref, ga1_ref, gmaps1_ref, *refs):
        ins, outs = refs[:21], refs[21:]
        x, y, c = _me()
        me = 4 * x + 2 * y + c

        def total(ref, rows):
            t = ref[0, rows, :]
            for j in range(1, NDEV):
                t = t + ref[j, rows, :]
            return t

        row = lambda r: slice(r, r + 1)
        outs[0][...] = jnp.broadcast_to(total(ga1_ref, row(0))[:, :128], (8, 128))

        def update(k, g, o):
            w_ref, m_ref, v_ref = ins[3 * k:3 * k + 3]
            delta, nm, nv = _adamw(w_ref[...], g, m_ref[...], v_ref[...])
            for ref, val in zip(outs[o:o + 4], (g, delta, nm, nv)):
                ref[...] = val

        two = lax.broadcasted_iota(jnp.int32, (2, d), 0)
        for k in range(3):
            update(k, jnp.where(two == 0, total(ga0_ref, row(k)), total(ga1_ref, row(2 + k))), 1 + 4 * k)
        update(3, jnp.where(two[:, :RW] == 0, total(ga0_ref, row(3))[:, :RW], total(ga1_ref, row(5))[:, :RW]), 13)
        update(4, jnp.concatenate([total(gmaps0_ref, slice(None)), total(gmaps1_ref, slice(None))], axis=0), 17)
        update(5, total(gmeta_ref, pl.ds(pl.multiple_of(me * N_META, N_META), N_META)), 21)
        update(6, total(ga1_ref, row(1)), 25)

    flat = []
    for trip in (*norms, pool_scale, pool_maps, meta, final_norm):
        flat += list(trip)
    out_shapes = [jax.ShapeDtypeStruct((8, 128), f32)]
    for trip in (*norms, pool_scale, pool_maps, meta, final_norm):
        out_shapes += [jax.ShapeDtypeStruct(trip[0].shape, f32)] * 4
    return pl.pallas_call(body, name=name, out_shape=out_shapes,
                          compiler_params=_cp())(ga0, gmaps0, gmeta, ga1, gmaps1, *flat)


def kernel(x, meta, ffn1_norm, ffn1_gate, ffn1_up, ffn1_down, mix_norm, w_in, pool_maps, pool_scale, w_ret_up, w_pool_up, w_out, ffn2_norm, ffn2_gate, ffn2_up, ffn2_down, final_norm, loss_target, m_meta, m_ffn1_norm, m_ffn1_gate, m_ffn1_up, m_ffn1_down, m_mix_norm, m_w_in, m_pool_maps, m_pool_scale, m_w_ret_up, m_w_pool_up, m_w_out, m_ffn2_norm, m_ffn2_gate, m_ffn2_up, m_ffn2_down, m_final_norm, v_meta, v_ffn1_norm, v_ffn1_gate, v_ffn1_up, v_ffn1_down, v_mix_norm, v_w_in, v_pool_maps, v_pool_scale, v_w_ret_up, v_w_pool_up, v_w_out, v_ffn2_norm, v_ffn2_gate, v_ffn2_up, v_ffn2_down, v_final_norm):
    seq, d = x.shape[1], x.shape[2]
    depth = ffn1_gate.shape[0]
    ff = ffn1_gate.shape[2] * NDEV
    nin = w_in.shape[2] * NDEV
    length = seq + N_META
    pad = (-length) % CHUNK
    tp = length + pad
    assert pad % 8 == 0 and pad + N_META == CHUNK and depth == 2 and nin == 5 * RW + 2 * d

    ix, iy, ic = _me()
    me = (4 * ix + 2 * iy + ic).astype(jnp.int32).reshape(1)

    meta_all, = all_gather("gather_meta", [meta])
    meta_full = jnp.transpose(meta_all, (1, 0, 2)).reshape(N_META, d)

    token = meta_all
    tview = lambda *arrs: [jnp.swapaxes(a, 1, 2) for a in arrs]
    t_g1, t_u1, t_g2, t_u2, t_in = (tview(w, m, v) for w, m, v in (
        (ffn1_gate, m_ffn1_gate, v_ffn1_gate), (ffn1_up, m_ffn1_up, v_ffn1_up), (ffn2_gate, m_ffn2_gate, v_ffn2_gate),
        (ffn2_up, m_ffn2_up, v_ffn2_up), (w_in, m_w_in, v_w_in)))
    keys, groups = [], []
    for layer in range(depth):
        lands = prep_layer(layer, me, [w_ret_up, w_pool_up],
                           [t_g1[0], t_u1[0], t_g2[0], t_u2[0], t_in[0], ffn1_down, ffn2_down, w_out],
                           (token,) if layer else ())
        wretT, wpoolT, g1T, u1T, g2T, u2T, winT, d1, d2, wout = lands
        keys += [("ffn1", layer), ("mix", layer), ("ffn2", layer)]
        groups += [[g1T, u1T, d1], [winT, wretT, wpoolT, wout], [g2T, u2T, d2]]
        if layer == 0:
            first, token = gather_start_chips("gather_start_first", groups[:1], (token,))
    started, token = gather_start_chips("gather_start_rest", groups[1:], (token,))
    gathers = dict(zip(keys, first + started))

    def forward(part, layer, after):
        ssem, rsem, group = gathers[(part, layer)]
        ssem, rsem, group, tok = gather_forward(f"gather_forward_{part}{layer}", ssem, rsem, group, after)
        gathers[(part, layer)] = (ssem, rsem, group)
        return tok

    def gathered(part, layer, after):
        ssem, rsem, group = gathers[(part, layer)]
        _, full = copies_wait(f"gather_wait_{part}{layer}", ssem, rsem, (), group, after, 3)
        return [a.reshape((NDEV * a.shape[1],) + a.shape[2:]) for a in full]

    cos2, sin2, dec = _retention_tables(tp, pad, token[0, 0])
    h = jnp.concatenate([jnp.zeros((pad, d), f32), meta_full + (token[0, 0] + 0.0 * cos2[0, 0] + 0.0 * dec[0, 0, 0, 0]), x[0]],
                        axis=0)

    saved = []
    weights = []
    tok = forward("ffn1", 0, token)
    for layer in range(depth):
        row = lambda a: a[layer:layer + 1]
        s = {"h0": h}
        g1T, u1T, d1 = gathered("ffn1", layer, tok if layer == 0 else h)
        tok = forward("mix", layer, h) if layer else None
        h, s["a1"], s["b1"] = ffn_fwd(f"ffn1_fwd{layer}", h, row(ffn1_norm), g1T, u1T, d1, (tok,) if layer else ())
        s["h1"] = h
        if layer == 0:
            tok = forward("mix", layer, h)
        winT, wretT, wpoolT, wout = gathered("mix", layer, tok if layer == 0 else h)
        s["zq"], s["zg"], zu, s["zgate"] = mix_in_fwd(f"mix_in_fwd{layer}", h, row(mix_norm), winT, cos2, sin2)
        s["oraw"], s["states"] = retention_fwd(f"retention_fwd{layer}", s["zq"], dec)
        s["pooled"], p = pool_fwd(f"pool_fwd{layer}", zu, pool_maps[layer], row(pool_scale), pad)
        tok = forward("ffn2", layer, p)
        h, s["rp"], s["mixed"] = mix_out_fwd(
            f"mix_out_fwd{layer}", h, s["oraw"], s["zg"], s["zgate"], p, wretT, wpoolT, wout, (tok,))
        s["h2"] = h
        g2T, u2T, d2 = gathered("ffn2", layer, h)
        tok = (forward("ffn1", layer + 1, h),) if layer + 1 < depth else ()
        h, s["a2"], s["b2"] = ffn_fwd(f"ffn2_fwd{layer}", h, row(ffn2_norm), g2T, u2T, d2, tok)
        saved.append(s)
        weights.append((g1T, u1T, g2T, u2T, winT, wretT, wpoolT, d1, d2, wout))

    dh, loss_part, dg_final = final_loss("final_loss", h, final_norm.reshape(1, d), loss_target[0])

    small = {}
    small_gathers = {}
    exchanges = {}
    token = None

    def rows8(vals):
        at = lax.broadcasted_iota(jnp.int32, (8, d), 0)
        out = jnp.zeros((8, d), f32)
        for k, v in enumerate(vals):
            r0 = v[0:1]
            r0 = r0 if r0.shape[1] == d else jnp.pad(r0, ((0, 0), (0, d - r0.shape[1])))
            out = jnp.where(at == k, r0, out)
        return out

    def exchange(part, layer, grads):
        by_dest = [g.reshape(g.shape[0], NDEV, g.shape[1] // NDEV, g.shape[2]) for g in grads]
        ssem, rsem, sent, lands, tok = exchange_start(f"exchange_start_{part}{layer}", by_dest)
        exchanges[(part, layer)] = (ssem, rsem, sent, lands)
        return (tok,)

    for layer in reversed(range(depth)):
        g1T, u1T, g2T, u2T, winT, wretT, wpoolT, d1, d2, wout = weights[layer]
        row = lambda a: a[layer:layer + 1]
        s = saved[layer]
        dh, lhs2, rhs2, small[("ffn2", layer)] = ffn_bwd(
            f"ffn2_bwd{layer}", dh, s["h2"], row(ffn2_norm), s["a2"], s["b2"], g2T, u2T, d2, () if token is None else token)
        token = exchange("ffn2", layer, [mm_tn(f"ffn2_wgrad{layer}", lhs2, rhs2, lambda b: b // 2)])
        do, dzg, dzgate, dp, drp, dyb = mix_out_bwd(
            f"mix_out_bwd{layer}", dh, s["oraw"], s["zg"], s["zgate"], s["rp"], wretT, wpoolT, wout, token)
        gw_mix = [mm_tn(f"w_out_wgrad{layer}", s["mixed"], dyb, lambda b: b),
                  mm_tn(f"up_wgrad{layer}", drp, s["rp"], lambda b: b)]
        dq, dk, dv = retention_bwd(f"retention_bwd{layer}", s["zq"], cos2, sin2, dec, s["states"], do, pad, token)
        dzu, small[("maps", layer)], small[("scale", layer)] = pool_bwd(
            f"pool_bwd{layer}", dp, s["pooled"], pool_maps[layer], row(pool_scale), pad)
        dh, dz, n2, small[("mix", layer)] = mix_in_bwd(
            f"mix_in_bwd{layer}", dq, dk, dv, dzg, dzu, dzgate, s["h1"], row(mix_norm), winT, dh)
        token = exchange("mix", layer, gw_mix + [mm_tn(f"w_in_wgrad{layer}", dz, n2, lambda b: b)])
        dh, lhs1, rhs1, small[("ffn1", layer)] = ffn_bwd(
            f"ffn1_bwd{layer}", dh, s["h0"], row(ffn1_norm), s["a1"], s["b1"], g1T, u1T, d1, token)
        rows = [small[("ffn1", layer)], small[("mix", layer)], small[("ffn2", layer)], small[("scale", layer)]]
        packs = [rows8([loss_part, dg_final] + rows if layer == depth - 1 else rows), small[("maps", layer)]]
        if layer == 0:
            dmeta = dh[pad:CHUNK]
            packs.append(jnp.transpose(dmeta.reshape(N_META, NDEV, d // NDEV), (1, 0, 2)).reshape(NDEV * N_META, d // NDEV))
        ssem, rsem, lands, tok = gather_start(f"small_start{layer}", slot_in(f"small_slot{layer}", me, packs))
        small_gathers[layer] = (ssem, rsem, lands)
        if layer:
            token = exchange("ffn1", layer, [mm_tn(f"ffn1_wgrad{layer}", lhs1, rhs1, lambda b: b // 2, (tok,))])
        else:
            token = (tok,)
            for j, nm in enumerate(("ffn1_gate", "ffn1_up", "ffn1_down")):
                token = exchange(nm, layer, [mm_tn(f"{nm}_wgrad{layer}", lhs1, rhs1, lambda b: b // 2, token, only=j)])

    grad_x = (dh[CHUNK:] + token[0][0, 0])[None]

    big = {}
    after = token[0]
    plans = {
        "ffn2": [("ffn2_gate", 0, 0, False, *t_g2), ("ffn2_up", 0, 1, False, *t_u2),
                 ("ffn2_down", 0, 2, False, ffn2_down, m_ffn2_down, v_ffn2_down)],
        "mix": [("w_out", 0, 0, False, w_out, m_w_out, v_w_out),
                ("w_ret_up", 1, 0, True, w_ret_up, m_w_ret_up, v_w_ret_up),
                ("w_pool_up", 1, 1, True, w_pool_up, m_w_pool_up, v_w_pool_up), ("w_in", 2, 0, False, *t_in)],
        "ffn1": [("ffn1_gate", 0, 0, False, *t_g1), ("ffn1_up", 0, 1, False, *t_u1),
                 ("ffn1_down", 0, 2, False, ffn1_down, m_ffn1_down, v_ffn1_down)]}
    for nm, k, b, tr, w, m, v in plans["ffn1"]:
        plans[nm] = [(nm, 0, 0, tr, w, m, v)]
    for layer in reversed(range(depth)):
        for part in ("ffn2", "mix") + (("ffn1",) if layer else ("ffn1_gate", "ffn1_up", "ffn1_down")):
            ssem, rsem, sent, lands = exchanges[(part, layer)]
            sent, lands = copies_wait(f"exchange_wait_{part}{layer}", ssem, rsem, sent, lands, after)
            for nm, k, b, tr, w, m, v in plans[part]:
                big[nm] = adam_big(f"adam_{nm}{layer}", me, lands[k], sent[k], b, layer, tr, w, m, v, big.get(nm))
                after = big[nm][0]

    gsmall = []
    for layer in range(depth):
        ssem, rsem, lands = small_gathers[layer]
        gsmall += copies_wait(f"small_wait{layer}", ssem, rsem, (), lands, after)[1]

    maps2 = lambda a: a.reshape(depth * HEADS * HD, HD)
    res = adam_small(
        "adam_small", *gsmall,
        [(ffn1_norm, m_ffn1_norm, v_ffn1_norm), (mix_norm, m_mix_norm, v_mix_norm), (ffn2_norm, m_ffn2_norm, v_ffn2_norm)],
        (pool_scale, m_pool_scale, v_pool_scale), (maps2(pool_maps), maps2(m_pool_maps), maps2(v_pool_maps)),
        (meta, m_meta, v_meta), tuple(a.reshape(1, d) for a in (final_norm, m_final_norm, v_final_norm)), d)
    loss = res[0][0, 0]
    sm = {}
    for k, nm in enumerate(["ffn1_norm", "mix_norm", "ffn2_norm", "pool_scale", "pool_maps", "meta", "final_norm"]):
        sm[nm] = list(res[1 + 4 * k:5 + 4 * k])
    sm["pool_maps"] = [a.reshape(pool_maps.shape) for a in sm["pool_maps"]]
    sm["final_norm"] = [a.reshape(d) for a in sm["final_norm"]]

    names = ["meta", "ffn1_norm", "ffn1_gate", "ffn1_up", "ffn1_down", "mix_norm", "w_in", "pool_maps", "pool_scale",
             "w_ret_up", "w_pool_up", "w_out", "ffn2_norm", "ffn2_gate", "ffn2_up", "ffn2_down", "final_norm"]
    for nm in ("ffn1_gate", "ffn1_up", "ffn2_gate", "ffn2_up", "w_in"):
        big[nm] = tview(*big[nm])
    allw = {**{k: list(v) for k, v in big.items()}, **sm}
    outs = [loss, grad_x]
    for kind in range(4):
        outs += [allw[nm][kind] for nm in names]
    return tuple(outs)
```

```python
import functools

import jax
import jax.numpy as jnp
from jax import lax
from jax.experimental import pallas as pl
from jax.experimental.pallas import tpu as pltpu

f32 = jnp.float32
bf16 = jnp.bfloat16
MESH = pl.DeviceIdType.MESH
NDEV = 8
N_META = 16
HEADS = 4
HD = 128
CHUNK = 128
RW = HEADS * HD
POOL_WINDOWS = (2, 4, 8, 16)
ROPE_BASE = 10000.0
EPS = 1e-6
ADAM_LR = 0.001
ADAM_B1 = 0.9
ADAM_B2 = 0.999
ADAM_EPS = 1e-08
ADAM_WD = 0.01
ADAM_STEP = 10
VMEM_CAP_MB = 60


def _cp(sem=None):
    return pltpu.CompilerParams(vmem_limit_bytes=VMEM_CAP_MB * 2**20, dimension_semantics=sem)


def _row_tile(tp, want=384):
    return want if tp % want == 0 else 128


def _resident(shape):
    nd = len(shape)
    return pl.BlockSpec(shape, lambda *_: (0,) * nd, pipeline_mode=pl.Buffered(1))


def _skip(nd, body):
    return (lambda *refs: body(*refs[nd:])) if nd else body


def _dot_nn(a, b):
    return lax.dot_general(a, b, (((1,), (0,)), ((), ())), preferred_element_type=f32)


def _dot_nt(a, b):
    return lax.dot_general(a, b, (((1,), (1,)), ((), ())), preferred_element_type=f32)


def _dot_tn(a, b):
    return lax.dot_general(a, b, (((0,), (0,)), ((), ())), preferred_element_type=f32)


def _rms(h):
    rs = lax.rsqrt(jnp.mean(h * h, axis=-1, keepdims=True) + EPS)
    return h * rs, rs


def _rms_bwd(dn, g, hhat, rs):
    dhh = dn * g
    return rs * (dhh - hhat * jnp.mean(dhh * hhat, axis=-1, keepdims=True))


def _sigmoid(x):
    return jax.nn.sigmoid(x)


def _me():
    return lax.axis_index("x"), lax.axis_index("y"), lax.axis_index("c")


def _peer(idx):
    return (idx // 4, (idx // 2) % 2, idx % 2)


def all_gather(name, arrays):
    n = len(arrays)

    def body(*refs):
        ins, outs = refs[:n], refs[n:2 * n]
        send_sems, recv_sems, local_sems = refs[2 * n:]
        x, y, c = _me()
        me = 4 * x + 2 * y + c
        locals_ = []
        for k in range(n):
            cp = pltpu.make_async_copy(ins[k], outs[k].at[me], local_sems.at[k])
            cp.start()
            locals_.append(cp)
        for d in range(1, NDEV):
            for k in range(n):
                pltpu.make_async_remote_copy(
                    src_ref=ins[k], dst_ref=outs[k].at[me], send_sem=send_sems.at[k], recv_sem=recv_sems.at[k],
                    device_id=_peer((me + d) % NDEV), device_id_type=MESH).start()
        for k in range(n):
            seven = outs[k].at[pl.ds(0, NDEV - 1)]
            w = pltpu.make_async_remote_copy(src_ref=seven, dst_ref=seven, send_sem=send_sems.at[k],
                                             recv_sem=recv_sems.at[k], device_id=(x, y, c), device_id_type=MESH)
            w.wait_send()
            w.wait_recv()
            locals_[k].wait()

    anyspec = pl.BlockSpec(memory_space=pl.ANY)
    return pl.pallas_call(
        body, name=name,
        out_shape=[jax.ShapeDtypeStruct((NDEV,) + a.shape, a.dtype) for a in arrays],
        in_specs=[anyspec] * n, out_specs=[anyspec] * n,
        scratch_shapes=[pltpu.SemaphoreType.DMA((n,)), pltpu.SemaphoreType.DMA((n,)), pltpu.SemaphoreType.DMA((n,))],
    )(*arrays)


_HBM = pl.BlockSpec(memory_space=pltpu.HBM)
_SEM = pl.BlockSpec(memory_space=pltpu.SEMAPHORE)
_ANY = pl.BlockSpec(memory_space=pl.ANY)
_EFFECT = pltpu.SideEffectType.DATAFLOW_SIDE_EFFECTING


def _in_hbm(a):
    return pltpu.with_memory_space_constraint(a, pltpu.HBM)


def gather_start(name, lands, deps=()):
    n, nd = len(lands), len(deps)

    def body(*refs):
        land = refs[nd:nd + n]
        send_sems, recv_sems = refs[nd + n:nd + n + 2]
        token = refs[-1]
        x, y, c = _me()
        me = 4 * x + 2 * y + c
        for d in range(1, NDEV):
            for k in range(n):
                pltpu.make_async_remote_copy(
                    src_ref=land[k].at[me], dst_ref=land[k].at[me], send_sem=send_sems.at[k], recv_sem=recv_sems.at[k],
                    device_id=_peer((me + d) % NDEV), device_id_type=MESH).start()
        token[...] = jnp.zeros_like(token)

    res = pl.pallas_call(
        body, name=name,
        out_shape=(pltpu.SemaphoreType.DMA((n,)), pltpu.SemaphoreType.DMA((n,)),
                   *[pltpu.HBM(a.shape, a.dtype) for a in lands], jax.ShapeDtypeStruct((8, 128), f32)),
        in_specs=[_ANY] * nd + [_HBM] * n,
        out_specs=(_SEM, _SEM, *[_HBM] * n, pl.BlockSpec(memory_space=pltpu.VMEM)),
        input_output_aliases={nd + k: 2 + k for k in range(n)},
        compiler_params=pltpu.CompilerParams(has_side_effects=_EFFECT),
    )(*deps, *[_in_hbm(a) for a in lands])
    return res[0], res[1], list(res[2:2 + n]), res[-1]


def _other_chips(x, y):
    return [(1 - x, y), (x, 1 - y), (1 - x, 1 - y)]


def gather_start_chips(name, groups, deps=()):
    sizes = [len(g) for g in groups]
    lands = [a for g in groups for a in g]
    n, nd, ng = len(lands), len(deps), len(groups)

    def body(*refs):
        land = refs[nd:nd + n]
        sems = refs[nd + n:nd + n + 2 * ng]
        token = refs[-1]
        x, y, c = _me()
        me = 4 * x + 2 * y + c
        k = 0
        for g, size in enumerate(sizes):
            for j in range(size):
                for to in [(x, y, 1 - c)] + [(cx, cy, c) for cx, cy in _other_chips(x, y)]:
                    pltpu.make_async_remote_copy(
                        src_ref=land[k].at[me], dst_ref=land[k].at[me], send_sem=sems[2 * g].at[j],
                        recv_sem=sems[2 * g + 1].at[j], device_id=to, device_id_type=MESH).start()
                k += 1
        token[...] = jnp.zeros_like(token)

    res = pl.pallas_call(
        body, name=name,
        out_shape=(*[pltpu.SemaphoreType.DMA((size,)) for size in sizes for _ in range(2)],
                   *[pltpu.HBM(a.shape, a.dtype) for a in lands], jax.ShapeDtypeStruct((8, 128), f32)),
        in_specs=[_ANY] * nd + [_HBM] * n,
        out_specs=(*[_SEM] * (2 * ng), *[_HBM] * n, pl.BlockSpec(memory_space=pltpu.VMEM)),
        input_output_aliases={nd + k: 2 * ng + k for k in range(n)},
        compiler_params=pltpu.CompilerParams(has_side_effects=_EFFECT),
    )(*deps, *[_in_hbm(a) for a in lands])
    out, k = [], 2 * ng
    for g, size in enumerate(sizes):
        out.append((res[2 * g], res[2 * g + 1], list(res[k:k + size])))
        k += size
    return out, res[-1]


def gather_forward(name, send_sems, recv_sems, lands, after):
    n = len(lands)

    def body(*refs):
        land = refs[:n]
        ssem, rsem = refs[n:n + 2]
        send2, recv2 = refs[n + 3:n + 5]
        token = refs[-1]
        x, y, c = _me()
        for k in range(n):
            four = land[k].at[pl.ds(0, 4)]
            w = pltpu.make_async_remote_copy(src_ref=four, dst_ref=four, send_sem=ssem.at[k], recv_sem=rsem.at[k],
                                             device_id=(x, y, c), device_id_type=MESH)
            w.wait_send()
            w.wait_recv()
            for cx, cy in _other_chips(x, y):
                slot = 4 * cx + 2 * cy + c
                pltpu.make_async_remote_copy(
                    src_ref=land[k].at[slot], dst_ref=land[k].at[slot], send_sem=send2.at[k], recv_sem=recv2.at[k],
                    device_id=(x, y, 1 - c), device_id_type=MESH).start()
        token[...] = jnp.zeros_like(token)

    res = pl.pallas_call(
        body, name=name,
        out_shape=(pltpu.SemaphoreType.DMA((n,)), pltpu.SemaphoreType.DMA((n,)),
                   *[pltpu.HBM(a.shape, a.dtype) for a in lands], jax.ShapeDtypeStruct((8, 128), f32)),
        in_specs=[_HBM] * n + [_SEM, _SEM, _ANY],
        out_specs=(_SEM, _SEM, *[_HBM] * n, pl.BlockSpec(memory_space=pltpu.VMEM)),
        input_output_aliases={k: 2 + k for k in range(n)},
        compiler_params=pltpu.CompilerParams(has_side_effects=_EFFECT),
    )(*lands, send_sems, recv_sems, after)
    return res[0], res[1], list(res[2:2 + n]), res[-1]


def exchange_start(name, grads, deps=()):
    n, nd = len(grads), len(deps)
    lands = [lax.empty((NDEV - 1, g.shape[0]) + g.shape[2:], g.dtype) for g in grads]

    def body(*refs):
        src = refs[nd:nd + n]
        land = refs[nd + n:nd + 2 * n]
        send_sems, recv_sems = refs[nd + 2 * n:nd + 2 * n + 2]
        token = refs[-1]
        x, y, c = _me()
        me = 4 * x + 2 * y + c
        for d in range(1, NDEV):
            p = (me + d) % NDEV
            for k in range(n):
                pltpu.make_async_remote_copy(
                    src_ref=src[k].at[:, p], dst_ref=land[k].at[d - 1], send_sem=send_sems.at[k], recv_sem=recv_sems.at[k],
                    device_id=_peer(p), device_id_type=MESH).start()
        token[...] = jnp.zeros_like(token)

    both = list(grads) + lands
    res = pl.pallas_call(
        body, name=name,
        out_shape=(pltpu.SemaphoreType.DMA((n,)), pltpu.SemaphoreType.DMA((n,)),
                   *[pltpu.HBM(a.shape, a.dtype) for a in both], jax.ShapeDtypeStruct((8, 128), f32)),
        in_specs=[_ANY] * nd + [_HBM] * (2 * n),
        out_specs=(_SEM, _SEM, *[_HBM] * (2 * n), pl.BlockSpec(memory_space=pltpu.VMEM)),
        input_output_aliases={nd + k: 2 + k for k in range(2 * n)},
        compiler_params=pltpu.CompilerParams(has_side_effects=_EFFECT),
    )(*deps, *[_in_hbm(a) for a in both])
    return res[0], res[1], list(res[2:2 + n]), list(res[2 + n:2 + 2 * n]), res[-1]


def copies_wait(name, send_sems, recv_sems, sent, lands, after, count=NDEV - 1):
    ns, n = len(sent), len(lands)

    def body(*refs):
        land = refs[ns:ns + n]
        ssem, rsem = refs[ns + n:ns + n + 2]
        x, y, c = _me()
        for k in range(n):
            seven = land[k].at[pl.ds(0, count)]
            w = pltpu.make_async_remote_copy(src_ref=seven, dst_ref=seven, send_sem=ssem.at[k], recv_sem=rsem.at[k],
                                             device_id=(x, y, c), device_id_type=MESH)
            w.wait_send()
            w.wait_recv()

    both = list(sent) + list(lands)
    res = pl.pallas_call(
        body, name=name, out_shape=tuple(pltpu.HBM(a.shape, a.dtype) for a in both),
        in_specs=[_HBM] * (ns + n) + [_SEM, _SEM, _ANY], out_specs=tuple([_HBM] * (ns + n)),
        input_output_aliases={k: k for k in range(ns + n)},
        compiler_params=pltpu.CompilerParams(has_side_effects=_EFFECT),
    )(*both, send_sems, recv_sems, after)
    return list(res[:ns]), list(res[ns:])


def prep_layer(layer, me, col_sharded, row_sharded, deps=()):
    nc, nr = len(col_sharded), len(row_sharded)

    def body(me_ref, *refs):
        ins, outs = refs[:nc + nr], refs[nc + nr + len(deps):]
        for k in range(nc):
            outs[k][...] = ins[k][...].T.astype(bf16)
        for k in range(nc, nc + nr):
            outs[k][...] = ins[k][...].astype(bf16)

    arrs = list(col_sharded) + list(row_sharded)
    in_specs = [pl.BlockSpec((None,) + a.shape[1:], lambda i, me_ref: (layer, 0, 0)) for a in arrs]
    shapes = [(a.shape[2], a.shape[1]) for a in col_sharded] + [a.shape[1:] for a in row_sharded]
    out_specs = [pl.BlockSpec((None,) + s, lambda i, me_ref: (me_ref[0], 0, 0)) for s in shapes]
    return pl.pallas_call(
        body, name=f"prep_layer{layer}",
        grid_spec=pltpu.PrefetchScalarGridSpec(num_scalar_prefetch=1, grid=(1,), in_specs=in_specs + [_ANY] * len(deps),
                                               out_specs=out_specs),
        out_shape=[jax.ShapeDtypeStruct((NDEV,) + s, bf16) for s in shapes], compiler_params=_cp())(me, *arrs, *deps)


def slot_in(name, me, arrays):
    n = len(arrays)

    def body(me_ref, *refs):
        for k in range(n):
            refs[n + k][...] = refs[k][...]

    in_specs = [pl.BlockSpec(a.shape, lambda i, me_ref: (0, 0)) for a in arrays]
    out_specs = [pl.BlockSpec((None,) + a.shape, lambda i, me_ref: (me_ref[0], 0, 0)) for a in arrays]
    return pl.pallas_call(
        body, name=name,
        grid_spec=pltpu.PrefetchScalarGridSpec(num_scalar_prefetch=1, grid=(1,), in_specs=in_specs, out_specs=out_specs),
        out_shape=[jax.ShapeDtypeStruct((NDEV,) + a.shape, a.dtype) for a in arrays])(me, *arrays)


def _ff_chunks(ff, want=768):
    if ff % 256:
        return [slice(0, ff)]
    return [slice(c, min(c + want, ff)) for c in range(0, ff, want)]


def ffn_fwd(name, h, g, wgT, wuT, wd, deps=()):
    tp, d = h.shape
    ff = wgT.shape[0]
    tm = _row_tile(tp, 704)

    def body(h_ref, g_ref, wg_ref, wu_ref, wd_ref, ho_ref, a_ref, b_ref):
        hh = h_ref[...]
        hhat, _ = _rms(hh)
        n = (hhat * g_ref[...]).astype(bf16)
        acc = None
        for cols in _ff_chunks(ff):
            a = _dot_nt(n, wg_ref[cols, :])
            b = _dot_nt(n, wu_ref[cols, :])
            part = _dot_nn(((a * _sigmoid(a)) * b).astype(bf16), wd_ref[cols, :])
            acc = part if acc is None else acc + part
            a_ref[:, cols] = a.astype(bf16)
            b_ref[:, cols] = b.astype(bf16)
        ho_ref[...] = hh + 0.5 * acc

    row = lambda w: pl.BlockSpec((tm, w), lambda i: (i, 0))
    return pl.pallas_call(
        _skip(len(deps), body), name=name, grid=(tp // tm,),
        in_specs=[_ANY] * len(deps) + [row(d), _resident((1, d)), _resident((ff, d)), _resident((ff, d)), _resident((ff, d))],
        out_specs=[row(d), row(ff), row(ff)],
        out_shape=[jax.ShapeDtypeStruct((tp, d), f32), jax.ShapeDtypeStruct((tp, ff), bf16),
                   jax.ShapeDtypeStruct((tp, ff), bf16)],
        compiler_params=_cp(("arbitrary",)))(*deps, h, g, wgT, wuT, wd)


def ffn_bwd(name, dy, h, g, a, b, wgT, wuT, wd, deps=()):
    tp, d = h.shape
    ff = wgT.shape[0]
    tm = _row_tile(tp, 384)

    def body(dy_ref, h_ref, g_ref, a_ref, b_ref, wg_ref, wu_ref, wd_ref, dh_ref, lhs_ref, rhs_ref, dg_ref):
        dyv = dy_ref[...]
        hhat, rs = _rms(h_ref[...])
        gv = g_ref[...]
        n = hhat * gv
        dyh = (0.5 * dyv).astype(bf16)
        dn = None
        for cols in _ff_chunks(ff):
            ds = _dot_nt(dyh, wd_ref[cols, :])
            av = a_ref[:, cols].astype(f32)
            bv = b_ref[:, cols].astype(f32)
            sg = _sigmoid(av)
            sa = av * sg
            da = (ds * bv * (sg * (1.0 + av * (1.0 - sg)))).astype(bf16)
            db = (ds * sa).astype(bf16)
            part = _dot_nn(da, wg_ref[cols, :]) + _dot_nn(db, wu_ref[cols, :])
            dn = part if dn is None else dn + part
            lhs_ref[0, :, cols] = da
            lhs_ref[1, :, cols] = db
            lhs_ref[2, :, cols] = (sa * bv).astype(bf16)
        dh_ref[...] = dyv + _rms_bwd(dn, gv, hhat, rs)

        @pl.when(pl.program_id(0) == 0)
        def _():
            dg_ref[...] = jnp.zeros_like(dg_ref)

        dg_ref[0:1, :] += jnp.sum(dn * hhat, axis=0, keepdims=True)
        rhs_ref[0] = n.astype(bf16)
        rhs_ref[1] = dyh

    row = lambda w: pl.BlockSpec((tm, w), lambda i: (i, 0))
    return pl.pallas_call(
        _skip(len(deps), body), name=name, grid=(tp // tm,),
        in_specs=[_ANY] * len(deps) + [row(d), row(d), _resident((1, d)), row(ff), row(ff),
                  _resident((ff, d)), _resident((ff, d)), _resident((ff, d))],
        out_specs=[row(d), pl.BlockSpec((3, tm, ff), lambda i: (0, i, 0)), pl.BlockSpec((2, tm, d), lambda i: (0, i, 0)),
                   pl.BlockSpec((8, d), lambda i: (0, 0))],
        out_shape=[jax.ShapeDtypeStruct((tp, d), f32), jax.ShapeDtypeStruct((3, tp, ff), bf16),
                   jax.ShapeDtypeStruct((2, tp, d), bf16), jax.ShapeDtypeStruct((8, d), f32)],
        compiler_params=_cp(("arbitrary",)))(*deps, dy, h, g, a, b, wgT, wuT, wd)


def mm_tn(name, lhs, rhs, rhs_of, deps=(), only=None):
    _, tp, m = lhs.shape
    b0, nb = (0, lhs.shape[0]) if only is None else (only, 1)
    n = rhs.shape[2]
    def fits(t, ms):
        mb = m // ms
        return (tp % t == 0 and m % (128 * ms) == 0
                and 2 * t * (mb + n) * 2 + mb * n * (2 * 2 + 4 + (4 if t < tp else 0)) <= 54 * 2**20)

    tk, msplit = next(((t, ms) for t in (tp, 1408, 704, 384) for ms in (1, 2, 4) if fits(t, ms)), (128, 1))
    nk = tp // tk
    mb = m // msplit

    def body(l_ref, r_ref, o_ref, acc_ref):
        if nk == 1:
            o_ref[...] = _dot_tn(l_ref[...], r_ref[...]).astype(o_ref.dtype)
            return
        k = pl.program_id(2)

        @pl.when(k == 0)
        def _():
            acc_ref[...] = jnp.zeros_like(acc_ref)

        acc_ref[...] += _dot_tn(l_ref[...], r_ref[...])

        @pl.when(k == nk - 1)
        def _():
            o_ref[...] = acc_ref[...].astype(o_ref.dtype)

    return pl.pallas_call(
        _skip(len(deps), body), name=name, grid=(nb, msplit, nk),
        in_specs=[_ANY] * len(deps) + [pl.BlockSpec((None, tk, mb), lambda b, j, k: (b0 + b, k, j)),
                                       pl.BlockSpec((None, tk, n), lambda b, j, k: (rhs_of(b0 + b), k, 0))],
        out_specs=pl.BlockSpec((None, mb, n), lambda b, j, k: (b, j, 0)),
        out_shape=jax.ShapeDtypeStruct((nb, m, n), bf16),
        scratch_shapes=[pltpu.VMEM((mb, n) if nk > 1 else (8, 128), f32)],
        compiler_params=_cp(("arbitrary", "arbitrary", "arbitrary")))(*deps, lhs, rhs)


def mix_in_fwd(name, h, g, winT, cos2, sin2):
    tp, d = h.shape
    nin = winT.shape[0]
    tm = _row_tile(tp)

    def body(h_ref, g_ref, w_ref, cos_ref, sin_ref, zq_ref, zg_ref, zu_ref, zgate_ref):
        hhat, _ = _rms(h_ref[...])
        z = _dot_nt((hhat * g_ref[...]).astype(bf16), w_ref[...])
        cosv, sinv = cos_ref[...], sin_ref[...]
        for hh in range(HEADS):
            qcols, kcols = slice(hh * HD, (hh + 1) * HD), slice(RW + hh * HD, RW + (hh + 1) * HD)
            zq_ref[:, qcols] = (_rot(z[:, qcols], cosv, sinv) * HD ** -0.5).astype(bf16)
            zq_ref[:, kcols] = _rot(z[:, kcols], cosv, sinv).astype(bf16)
        zq_ref[:, 2 * RW:] = z[:, 2 * RW:3 * RW].astype(bf16)
        zg_ref[...] = z[:, 3 * RW:4 * RW].astype(zg_ref.dtype)
        zu_ref[...] = z[:, 4 * RW:5 * RW]
        zgate_ref[...] = z[:, 5 * RW:].astype(zgate_ref.dtype)

    row = lambda w: pl.BlockSpec((tm, w), lambda i: (i, 0))
    widths = (3 * RW, RW, RW, 2 * d)
    return pl.pallas_call(
        body, name=name, grid=(tp // tm,),
        in_specs=[row(d), _resident((1, d)), _resident((nin, d)), row(HD), row(HD)],
        out_specs=[row(w) for w in widths],
        out_shape=[jax.ShapeDtypeStruct((tp, w), dt) for w, dt in zip(widths, (bf16, bf16, f32, bf16))],
        compiler_params=_cp(("arbitrary",)))(h, g, winT, cos2, sin2)


def mix_in_bwd(name, dq, dk, dv, dzg, dzu, dzgate, h, g, winT, dres):
    tp, d = h.shape
    nin = winT.shape[0]
    tm = _row_tile(tp)

    def body(dq_ref, dk_ref, dv_ref, dzg_ref, dzu_ref, dzgate_ref, h_ref, g_ref, w_ref, dres_ref, dh_ref, dz_ref, n_ref, dg_ref):
        dn, col = None, 0
        for piece in (dq_ref, dk_ref, dv_ref, dzg_ref, dzu_ref, dzgate_ref):
            v = piece[...]
            part = _dot_nn(v, w_ref[col:col + v.shape[1], :])
            dn = part if dn is None else dn + part
            dz_ref[:, col:col + v.shape[1]] = v
            col += v.shape[1]
        hhat, rs = _rms(h_ref[...])
        gv = g_ref[...]
        dh_ref[...] = dres_ref[...] + _rms_bwd(dn, gv, hhat, rs)

        @pl.when(pl.program_id(0) == 0)
        def _():
            dg_ref[...] = jnp.zeros_like(dg_ref)

        dg_ref[0:1, :] += jnp.sum(dn * hhat, axis=0, keepdims=True)
        n_ref[...] = (hhat * gv).astype(bf16)

    row = lambda w: pl.BlockSpec((tm, w), lambda i: (i, 0))
    return pl.pallas_call(
        body, name=name, grid=(tp // tm,),
        in_specs=[row(RW)] * 5 + [row(2 * d), row(d), _resident((1, d)), _resident((nin, d)), row(d)],
        out_specs=[row(d), pl.BlockSpec((None, tm, nin), lambda i: (0, i, 0)), pl.BlockSpec((None, tm, d), lambda i: (0, i, 0)),
                   pl.BlockSpec((8, d), lambda i: (0, 0))],
        out_shape=[jax.ShapeDtypeStruct((tp, d), f32), jax.ShapeDtypeStruct((1, tp, nin), bf16),
                   jax.ShapeDtypeStruct((1, tp, d), bf16), jax.ShapeDtypeStruct((8, d), f32)],
        compiler_params=_cp(("arbitrary",)))(dq, dk, dv, dzg, dzu, dzgate, h, g, winT, dres)


def _retention_tables(tp, pad, zero):
    half = HD // 2
    lane = jnp.arange(HD)
    inv_freq = ROPE_BASE ** (-(lane % half).astype(f32) / half)
    pos = jnp.arange(tp, dtype=f32) - pad + zero
    ang = pos[:, None] * inv_freq[None, :]
    cos2 = jnp.cos(ang)
    sin2 = jnp.where(lane[None, :] < half, -1.0, 1.0) * jnp.sin(ang)
    log_gamma = jnp.log1p(-(2.0 ** (-5.0 - jnp.arange(HEADS, dtype=f32))))
    idx = jnp.arange(CHUNK, dtype=f32)
    diff = idx[:, None] - idx[None, :]
    intra = jnp.where(diff[None] >= 0, jnp.exp(diff[None] * log_gamma[:, None, None]), 0.0)
    k_decay = jnp.exp((CHUNK - 1.0 - idx)[None, :] * log_gamma[:, None])
    q_decay = jnp.exp((idx + 1.0)[None, :] * log_gamma[:, None])
    chunk_decay = jnp.exp(CHUNK * log_gamma)
    full = (HEADS, CHUNK, HD)
    dec = jnp.stack([intra, jnp.broadcast_to(k_decay[:, :, None], full), jnp.broadcast_to(q_decay[:, :, None], full),
                     jnp.broadcast_to(chunk_decay[:, None, None], full)], axis=1)
    return cos2, sin2, dec


def _rot(t, cos2, sin2):
    return t * cos2 + pltpu.roll(t, HD // 2, 1) * sin2


def _rot_t(t, cos2, sin2):
    return t * cos2 - pltpu.roll(t, HD // 2, 1) * sin2


def _chunks_per_step(nch):
    return 3 if nch % 3 == 0 else 1


def retention_fwd(name, zq, dec):
    tp = zq.shape[0]
    nch = tp // CHUNK
    per = _chunks_per_step(nch)

    def body(q_ref, k_ref, v_ref, dec_ref, out_ref, st_ref, s_ref):
        @pl.when(pl.program_id(0) == 0)
        def _():
            s_ref[...] = jnp.zeros_like(s_ref)

        state = [s_ref[hh] for hh in range(HEADS)]
        for j in range(per):
            rows = slice(j * CHUNK, (j + 1) * CHUNK)
            for hh in range(HEADS):
                cols = slice(hh * HD, (hh + 1) * HD)
                qb, kb, vb = q_ref[rows, cols], k_ref[rows, cols], v_ref[rows, cols]
                sc = (_dot_nt(qb, kb) * dec_ref[hh, 0]).astype(bf16)
                sb = state[hh].astype(bf16)
                cross = _dot_nn((qb.astype(f32) * dec_ref[hh, 2]).astype(bf16), sb)
                out_ref[rows, cols] = (_dot_nn(sc, vb) + cross).astype(out_ref.dtype)
                st_ref[hh, j] = sb
                state[hh] = state[hh] * dec_ref[hh, 3] + _dot_tn((kb.astype(f32) * dec_ref[hh, 1]).astype(bf16), vb)
        for hh in range(HEADS):
            s_ref[hh] = state[hh]

    part = lambda j: pl.BlockSpec((per * CHUNK, RW), lambda n: (n, j))
    return pl.pallas_call(
        body, name=name, grid=(nch // per,),
        in_specs=[part(0), part(1), part(2), _resident((HEADS, 4, CHUNK, HD))],
        out_specs=[part(0), pl.BlockSpec((HEADS, per, HD, HD), lambda n: (0, n, 0, 0))],
        out_shape=[jax.ShapeDtypeStruct((tp, RW), bf16), jax.ShapeDtypeStruct((HEADS, nch, HD, HD), bf16)],
        scratch_shapes=[pltpu.VMEM((HEADS, HD, HD), f32)],
        compiler_params=_cp(("arbitrary",)))(zq, zq, zq, dec)


def retention_bwd(name, zq, cos2, sin2, dec, states, dout, pad, deps=()):
    tp = zq.shape[0]
    nch = tp // CHUNK
    per = _chunks_per_step(nch)
    nblk = nch // per
    scale = HD ** -0.5

    def body(q_ref, k_ref, v_ref, cos_ref, sin_ref, dec_ref, st_ref, do_ref, dq_ref, dk_ref, dv_ref, g_ref):
        @pl.when(pl.program_id(0) == 0)
        def _():
            g_ref[...] = jnp.zeros_like(g_ref)

        first_row = (nblk - 1 - pl.program_id(0)) * (per * CHUNK)
        gstate = [g_ref[hh] for hh in range(HEADS)]
        for j in reversed(range(per)):
            rows = slice(j * CHUNK, (j + 1) * CHUNK)
            cosv, sinv = cos_ref[rows, :], sin_ref[rows, :]
            keep = (lax.broadcasted_iota(jnp.int32, (CHUNK, HD), 0) + (first_row + j * CHUNK)) >= pad
            for hh in range(HEADS):
                cols = slice(hh * HD, (hh + 1) * HD)
                intra, kdec, qdec = dec_ref[hh, 0], dec_ref[hh, 1], dec_ref[hh, 2]
                qb, kb, vb = q_ref[rows, cols], k_ref[rows, cols], v_ref[rows, cols]
                qd = (qb.astype(f32) * qdec).astype(bf16)
                kd = (kb.astype(f32) * kdec).astype(bf16)
                sc = (_dot_nt(qb, kb) * intra).astype(bf16)
                dob = do_ref[rows, cols]
                sb = st_ref[hh, j]
                gb = gstate[hh].astype(bf16)
                dsc = (_dot_nt(dob, vb) * intra).astype(bf16)
                dv = _dot_tn(sc, dob) + _dot_nn(kd, gb)
                dqr = _dot_nn(dsc, kb) + _dot_nt(dob, sb) * qdec
                dkr = _dot_tn(dsc, qb) + _dot_nt(vb, gb) * kdec
                gstate[hh] = gstate[hh] * dec_ref[hh, 3] + _dot_tn(qd, dob)
                dq_ref[rows, cols] = jnp.where(keep, _rot_t(dqr * scale, cosv, sinv), 0.0).astype(bf16)
                dk_ref[rows, cols] = jnp.where(keep, _rot_t(dkr, cosv, sinv), 0.0).astype(bf16)
                dv_ref[rows, cols] = jnp.where(keep, dv, 0.0).astype(bf16)
        for hh in range(HEADS):
            g_ref[hh] = gstate[hh]

    part = lambda j: pl.BlockSpec((per * CHUNK, RW), lambda t: (nblk - 1 - t, j))
    table = pl.BlockSpec((per * CHUNK, HD), lambda t: (nblk - 1 - t, 0))
    return pl.pallas_call(
        _skip(len(deps), body), name=name, grid=(nblk,),
        in_specs=[_ANY] * len(deps) + [part(0), part(1), part(2), table, table, _resident((HEADS, 4, CHUNK, HD)),
                                       pl.BlockSpec((HEADS, per, HD, HD), lambda t: (0, nblk - 1 - t, 0, 0)), part(0)],
        out_specs=[part(0)] * 3,
        out_shape=[jax.ShapeDtypeStruct((tp, RW), bf16)] * 3,
        scratch_shapes=[pltpu.VMEM((HEADS, HD, HD), f32)],
        compiler_params=_cp(("arbitrary",)))(*deps, zq, zq, zq, cos2, sin2, dec, states, dout)


def _window_sum(xv, steps, tp, forward):
    s = xv
    for j in range(steps):
        sh = 2 ** j
        s = s + pltpu.roll(s, (tp - sh) if forward else sh, 0)
    return s


def pool_fwd(name, zu, maps, scale, pad):
    tp = zu.shape[0]

    def body(u_ref, maps_ref, scale_ref, pooled_ref, p_ref):
        row = lax.broadcasted_iota(jnp.int32, (tp, HD), 0)
        for gi, w in enumerate(POOL_WINDOWS):
            cols = slice(gi * HD, (gi + 1) * HD)
            xv = u_ref[:, cols]
            cnt = jnp.clip(row - (pad - 1), 1, w).astype(f32)
            pooled = jnp.where(row >= pad, _window_sum(xv, gi + 1, tp, False) / cnt - xv, 0.0).astype(bf16)
            pooled_ref[:, cols] = pooled
            p_ref[:, cols] = (_dot_nn(pooled, maps_ref[gi].astype(bf16)) * scale_ref[:, cols]).astype(bf16)

    return pl.pallas_call(
        body, name=name,
        out_shape=[jax.ShapeDtypeStruct((tp, RW), bf16), jax.ShapeDtypeStruct((tp, RW), bf16)],
        compiler_params=_cp())(zu, maps, scale)


def pool_bwd(name, dp, pooled, maps, scale, pad):
    tp = dp.shape[0]

    def body(dp_ref, pooled_ref, maps_ref, scale_ref, du_ref, dmaps_ref, dscale_ref):
        row = lax.broadcasted_iota(jnp.int32, (tp, HD), 0)
        dscale_ref[...] = jnp.zeros_like(dscale_ref)
        for gi, w in enumerate(POOL_WINDOWS):
            cols = slice(gi * HD, (gi + 1) * HD)
            mb = maps_ref[gi].astype(bf16)
            pooled = pooled_ref[:, cols]
            dpf = dp_ref[:, cols].astype(f32)
            dscale_ref[0:1, cols] = jnp.sum(dpf * _dot_nn(pooled, mb), axis=0, keepdims=True)
            dpm = (dpf * scale_ref[:, cols]).astype(bf16)
            dmaps_ref[gi * HD:(gi + 1) * HD, :] = _dot_tn(pooled, dpm)
            dpool = jnp.where(row >= pad, _dot_nt(dpm, mb), 0.0)
            cnt = jnp.clip(row - (pad - 1), 1, w).astype(f32)
            du = _window_sum(dpool / cnt, gi + 1, tp, True) - dpool
            du_ref[:, cols] = jnp.where(row >= pad, du, 0.0).astype(bf16)

    return pl.pallas_call(
        body, name=name,
        out_shape=[jax.ShapeDtypeStruct((tp, RW), bf16), jax.ShapeDtypeStruct((HEADS * HD, HD), f32),
                   jax.ShapeDtypeStruct((8, RW), f32)],
        compiler_params=_cp())(dp, pooled, maps, scale)


def _group_norm(o):
    mu = jnp.mean(o, axis=-1, keepdims=True)
    oc = o - mu
    rstd = lax.rsqrt(jnp.mean(oc * oc, axis=-1, keepdims=True) + EPS)
    return oc * rstd, rstd


def mix_out_fwd(name, h, oraw, zg, zgate, p, wretT, wpoolT, wout, deps=()):
    tp, d = h.shape
    tm = _row_tile(tp)

    def body(h_ref, o_ref, zg_ref, zgate_ref, p_ref, wr_ref, wp_ref, wo_ref, ho_ref, rp_ref, mixed_ref):
        parts = []
        for hh in range(HEADS):
            cols = slice(hh * HD, (hh + 1) * HD)
            rhat, _ = _group_norm(o_ref[:, cols].astype(f32))
            gv = zg_ref[:, cols].astype(f32)
            parts.append(rhat * (gv * _sigmoid(gv)))
        r = jnp.concatenate(parts, axis=-1).astype(bf16)
        pv = p_ref[...]
        ret = _dot_nt(r, wr_ref[...])
        pool = _dot_nt(pv, wp_ref[...])
        mixed = (_sigmoid(zgate_ref[:, :d].astype(f32)) * ret + _sigmoid(zgate_ref[:, d:].astype(f32)) * pool).astype(bf16)
        ho_ref[...] = h_ref[...] + _dot_nn(mixed, wo_ref[...])
        rp_ref[0] = r
        rp_ref[1] = pv
        mixed_ref[...] = mixed

    row = lambda w: pl.BlockSpec((tm, w), lambda i: (i, 0))
    return pl.pallas_call(
        _skip(len(deps), body), name=name, grid=(tp // tm,),
        in_specs=[_ANY] * len(deps) + [row(d), row(RW), row(RW), row(2 * d), row(RW), _resident((d, RW)), _resident((d, RW)),
                                       _resident((d, d))],
        out_specs=[row(d), pl.BlockSpec((2, tm, RW), lambda i: (0, i, 0)), pl.BlockSpec((None, tm, d), lambda i: (0, i, 0))],
        out_shape=[jax.ShapeDtypeStruct((tp, d), f32), jax.ShapeDtypeStruct((2, tp, RW), bf16),
                   jax.ShapeDtypeStruct((1, tp, d), bf16)],
        compiler_params=_cp(("arbitrary",)))(*deps, h, oraw, zg, zgate, p, wretT, wpoolT, wout)


def mix_out_bwd(name, dy, oraw, zg, zgate, rp, wretT, wpoolT, wout, deps=()):
    tp, d = dy.shape
    tm = _row_tile(tp)

    def body(dy_ref, o_ref, zg_ref, zgate_ref, rp_ref, wr_ref, wp_ref, wo_ref,
             do_ref, dzg_ref, dzgate_ref, dp_ref, drp_ref, dyb_ref):
        dyb = dy_ref[...].astype(bf16)
        dmixed = _dot_nt(dyb, wo_ref[...])
        sa = _sigmoid(zgate_ref[:, :d].astype(f32))
        sb = _sigmoid(zgate_ref[:, d:].astype(f32))
        dret = dmixed * sa
        dpool = dmixed * sb
        dzgate_ref[:, :d] = (dret * _dot_nt(rp_ref[0], wr_ref[...]) * (1.0 - sa)).astype(bf16)
        dzgate_ref[:, d:] = (dpool * _dot_nt(rp_ref[1], wp_ref[...]) * (1.0 - sb)).astype(bf16)
        dretb, dpoolb = dret.astype(bf16), dpool.astype(bf16)
        dr = _dot_nn(dretb, wr_ref[...])
        dp_ref[...] = _dot_nn(dpoolb, wp_ref[...]).astype(bf16)
        for hh in range(HEADS):
            cols = slice(hh * HD, (hh + 1) * HD)
            rhat, rstd = _group_norm(o_ref[:, cols].astype(f32))
            gv = zg_ref[:, cols].astype(f32)
            sg = _sigmoid(gv)
            drh = dr[:, cols]
            drhat = drh * (gv * sg)
            dzg_ref[:, cols] = (drh * rhat * (sg * (1.0 + gv * (1.0 - sg)))).astype(bf16)
            do = rstd * (drhat - jnp.mean(drhat, axis=-1, keepdims=True)
                         - rhat * jnp.mean(drhat * rhat, axis=-1, keepdims=True))
            do_ref[:, cols] = do.astype(bf16)
        drp_ref[0] = dretb
        drp_ref[1] = dpoolb
        dyb_ref[...] = dyb

    row = lambda w: pl.BlockSpec((tm, w), lambda i: (i, 0))
    return pl.pallas_call(
        _skip(len(deps), body), name=name, grid=(tp // tm,),
        in_specs=[_ANY] * len(deps) + [row(d), row(RW), row(RW), row(2 * d), pl.BlockSpec((2, tm, RW), lambda i: (0, i, 0)),
                                       _resident((d, RW)), _resident((d, RW)), _resident((d, d))],
        out_specs=[row(RW), row(RW), row(2 * d), row(RW), pl.BlockSpec((2, tm, d), lambda i: (0, i, 0)),
                   pl.BlockSpec((None, tm, d), lambda i: (0, i, 0))],
        out_shape=[jax.ShapeDtypeStruct((tp, RW), bf16), jax.ShapeDtypeStruct((tp, RW), bf16),
                   jax.ShapeDtypeStruct((tp, 2 * d), bf16), jax.ShapeDtypeStruct((tp, RW), bf16),
                   jax.ShapeDtypeStruct((2, tp, d), bf16), jax.ShapeDtypeStruct((1, tp, d), bf16)],
        compiler_params=_cp(("arbitrary",)))(*deps, dy, oraw, zg, zgate, rp, wretT, wpoolT, wout)


def final_loss(name, h, g, target):
    tp, d = h.shape
    tm = _row_tile(tp)
    nsub = tm // CHUNK

    def body(h_ref, g_ref, *rest):
        t_refs = rest[:nsub]
        dh_ref, loss_ref, dg_ref = rest[nsub:]
        i = pl.program_id(0)

        @pl.when(i == 0)
        def _():
            loss_ref[...] = jnp.zeros_like(loss_ref)
            dg_ref[...] = jnp.zeros_like(dg_ref)

        gv = g_ref[...]
        for j in range(nsub):
            rows = slice(j * CHUNK, (j + 1) * CHUNK)
            hhat, rs = _rms(h_ref[rows, :])
            err = jnp.where(i * nsub + j >= 1, hhat * gv - t_refs[j][...], 0.0)
            dyv = err / d
            dh_ref[rows, :] = _rms_bwd(dyv, gv, hhat, rs)
            loss_ref[...] += 0.5 * jnp.sum(jnp.sum(err * err, axis=-1, keepdims=True) / d)
            dg_ref[0:1, :] += jnp.sum(dyv * hhat, axis=0, keepdims=True)

    lagged = lambda j: pl.BlockSpec((CHUNK, d), lambda i: (jnp.maximum(i * nsub + j - 1, 0), 0))
    return pl.pallas_call(
        body, name=name, grid=(tp // tm,),
        in_specs=[pl.BlockSpec((tm, d), lambda i: (i, 0)), _resident((1, d))] + [lagged(j) for j in range(nsub)],
        out_specs=[pl.BlockSpec((tm, d), lambda i: (i, 0)), pl.BlockSpec((8, 128), lambda i: (0, 0)),
                   pl.BlockSpec((8, d), lambda i: (0, 0))],
        out_shape=[jax.ShapeDtypeStruct((tp, d), f32), jax.ShapeDtypeStruct((8, 128), f32),
                   jax.ShapeDtypeStruct((8, d), f32)],
        compiler_params=_cp(("arbitrary",)))(h, g, *[target] * nsub)


def _adamw(w, g, m, v):
    m = ADAM_B1 * m + (1.0 - ADAM_B1) * g
    v = ADAM_B2 * v + (1.0 - ADAM_B2) * (g * g)
    m_hat = m / (1.0 - ADAM_B1 ** ADAM_STEP)
    v_hat = v / (1.0 - ADAM_B2 ** ADAM_STEP)
    delta = -ADAM_LR * (m_hat / (jnp.sqrt(v_hat) + ADAM_EPS) + ADAM_WD * w)
    return delta, m, v


def adam_big(name, me, recv, own, b, layer, transposed, w, m, v, prev):
    r, c = recv.shape[2:]
    wshape = w.shape[1:]
    nchunk = 1 if transposed else next(k for k in (4, 2, 1) if r % (16 * k) == 0)
    rc = r // nchunk

    def body(me_ref, recv_ref, own_ref, w_ref, m_ref, v_ref, *rest):
        g_ref, d_ref, nm_ref, nv_ref = rest[-4:]
        g = own_ref[...].astype(f32)
        for j in range(NDEV - 1):
            g = g + recv_ref[j].astype(f32)
        if transposed:
            g = g.T
        delta, nm, nv = _adamw(w_ref[...], g, m_ref[...], v_ref[...])
        g_ref[...] = g
        d_ref[...] = delta
        nm_ref[...] = nm
        nv_ref[...] = nv

    wblock = wshape if transposed else (rc, c)
    wspec = pl.BlockSpec((None,) + wblock, lambda i, me_ref: (layer, i, 0))
    in_specs = [pl.BlockSpec((NDEV - 1, None, rc, c), lambda i, me_ref: (0, b, i, 0)),
                pl.BlockSpec((None, None, rc, c), lambda i, me_ref: (b, me_ref[0], i, 0)), wspec, wspec, wspec]
    args = [recv, own, w, m, v]
    aliases = {}
    if prev is not None:
        in_specs += [_ANY] * 4
        args += list(prev)
        aliases = {6 + k: k for k in range(4)}
    return pl.pallas_call(
        body, name=name,
        grid_spec=pltpu.PrefetchScalarGridSpec(num_scalar_prefetch=1, grid=(nchunk,), in_specs=in_specs,
                                               out_specs=[wspec] * 4),
        out_shape=[jax.ShapeDtypeStruct(w.shape, f32)] * 4, input_output_aliases=aliases,
        compiler_params=_cp())(me, *args)


def adam_small(name, ga0, gmaps0, gmeta, ga1, gmaps1, norms, pool_scale, pool_maps, meta, final_norm, d):
    def body(ga0_ref, gmaps0_ref, gmeta_ref, ga1_ref, gmaps1_ref, *refs):
        ins, outs = refs[:21], refs[21:]
        x, y, c = _me()
        me = 4 * x + 2 * y + c

        def total(ref, rows):
            t = ref[0, rows, :]
            for j in range(1, NDEV):
                t = t + ref[j, rows, :]
            return t

        row = lambda r: slice(r, r + 1)
        outs[0][...] = jnp.broadcast_to(total(ga1_ref, row(0))[:, :128], (8, 128))

        def update(k, g, o):
            w_ref, m_ref, v_ref = ins[3 * k:3 * k + 3]
            delta, nm, nv = _adamw(w_ref[...], g, m_ref[...], v_ref[...])
            for ref, val in zip(outs[o:o + 4], (g, delta, nm, nv)):
                ref[...] = val

        two = lax.broadcasted_iota(jnp.int32, (2, d), 0)
        for k in range(3):
            update(k, jnp.where(two == 0, total(ga0_ref, row(k)), total(ga1_ref, row(2 + k))), 1 + 4 * k)
        update(3, jnp.where(two[:, :RW] == 0, total(ga0_ref, row(3))[:, :RW], total(ga1_ref, row(5))[:, :RW]), 13)
        update(4, jnp.concatenate([total(gmaps0_ref, slice(None)), total(gmaps1_ref, slice(None))], axis=0), 17)
        update(5, total(gmeta_ref, pl.ds(pl.multiple_of(me * N_META, N_META), N_META)), 21)
        update(6, total(ga1_ref, row(1)), 25)

    flat = []
    for trip in (*norms, pool_scale, pool_maps, meta, final_norm):
        flat += list(trip)
    out_shapes = [jax.ShapeDtypeStruct((8, 128), f32)]
    for trip in (*norms, pool_scale, pool_maps, meta, final_norm):
        out_shapes += [jax.ShapeDtypeStruct(trip[0].shape, f32)] * 4
    return pl.pallas_call(body, name=name, out_shape=out_shapes,
                          compiler_params=_cp())(ga0, gmaps0, gmeta, ga1, gmaps1, *flat)


def kernel(x, meta, ffn1_norm, ffn1_gate, ffn1_up, ffn1_down, mix_norm, w_in, pool_maps, pool_scale, w_ret_up, w_pool_up, w_out, ffn2_norm, ffn2_gate, ffn2_up, ffn2_down, final_norm, loss_target, m_meta, m_ffn1_norm, m_ffn1_gate, m_ffn1_up, m_ffn1_down, m_mix_norm, m_w_in, m_pool_maps, m_pool_scale, m_w_ret_up, m_w_pool_up, m_w_out, m_ffn2_norm, m_ffn2_gate, m_ffn2_up, m_ffn2_down, m_final_norm, v_meta, v_ffn1_norm, v_ffn1_gate, v_ffn1_up, v_ffn1_down, v_mix_norm, v_w_in, v_pool_maps, v_pool_scale, v_w_ret_up, v_w_pool_up, v_w_out, v_ffn2_norm, v_ffn2_gate, v_ffn2_up, v_ffn2_down, v_final_norm):
    seq, d = x.shape[1], x.shape[2]
    depth = ffn1_gate.shape[0]
    ff = ffn1_gate.shape[2] * NDEV
    nin = w_in.shape[2] * NDEV
    length = seq + N_META
    pad = (-length) % CHUNK
    tp = length + pad
    assert pad % 8 == 0 and pad + N_META == CHUNK and depth == 2 and nin == 5 * RW + 2 * d

    ix, iy, ic = _me()
    me = (4 * ix + 2 * iy + ic).astype(jnp.int32).reshape(1)

    meta_all, = all_gather("gather_meta", [meta])
    meta_full = jnp.transpose(meta_all, (1, 0, 2)).reshape(N_META, d)

    token = meta_all
    tview = lambda *arrs: [jnp.swapaxes(a, 1, 2) for a in arrs]
    t_g1, t_u1, t_g2, t_u2, t_in = (tview(w, m, v) for w, m, v in (
        (ffn1_gate, m_ffn1_gate, v_ffn1_gate), (ffn1_up, m_ffn1_up, v_ffn1_up), (ffn2_gate, m_ffn2_gate, v_ffn2_gate),
        (ffn2_up, m_ffn2_up, v_ffn2_up), (w_in, m_w_in, v_w_in)))
    keys, groups = [], []
    for layer in range(depth):
        lands = prep_layer(layer, me, [w_ret_up, w_pool_up],
                           [t_g1[0], t_u1[0], t_g2[0], t_u2[0], t_in[0], ffn1_down, ffn2_down, w_out],
                           (token,) if layer else ())
        wretT, wpoolT, g1T, u1T, g2T, u2T, winT, d1, d2, wout = lands
        keys += [("ffn1", layer), ("mix", layer), ("ffn2", layer)]
        groups += [[g1T, u1T, d1], [winT, wretT, wpoolT, wout], [g2T, u2T, d2]]
        if layer == 0:
            first, token = gather_start_chips("gather_start_first", groups[:1], (token,))
    second, token = gather_start_chips("gather_start_second", groups[1:2], (token,))
    gathers = dict(zip(keys, first + second))

    def forward(part, layer, after):
        ssem, rsem, group = gathers[(part, layer)]
        ssem, rsem, group, tok = gather_forward(f"gather_forward_{part}{layer}", ssem, rsem, group, after)
        gathers[(part, layer)] = (ssem, rsem, group)
        return tok

    def gathered(part, layer, after):
        ssem, rsem, group = gathers[(part, layer)]
        _, full = copies_wait(f"gather_wait_{part}{layer}", ssem, rsem, (), group, after, 3)
        return [a.reshape((NDEV * a.shape[1],) + a.shape[2:]) for a in full]

    cos2, sin2, dec = _retention_tables(tp, pad, token[0, 0])
    h = jnp.concatenate([jnp.zeros((pad, d), f32), meta_full + (token[0, 0] + 0.0 * cos2[0, 0] + 0.0 * dec[0, 0, 0, 0]), x[0]],
                        axis=0)

    saved = []
    weights = []
    tok = forward("ffn1", 0, h)
    rest, tok = gather_start_chips("gather_start_rest", groups[2:], (tok,))
    gathers.update(zip(keys[2:], rest))
    for layer in range(depth):
        row = lambda a: a[layer:layer + 1]
        s = {"h0": h}
        g1T, u1T, d1 = gathered("ffn1", layer, tok if layer == 0 else h)
        tok = forward("mix", layer, h) if layer else None
        h, s["a1"], s["b1"] = ffn_fwd(f"ffn1_fwd{layer}", h, row(ffn1_norm), g1T, u1T, d1, (tok,) if layer else ())
        s["h1"] = h
        if layer == 0:
            tok = forward("mix", layer, h)
        winT, wretT, wpoolT, wout = gathered("mix", layer, tok if layer == 0 else h)
        s["zq"], s["zg"], zu, s["zgate"] = mix_in_fwd(f"mix_in_fwd{layer}", h, row(mix_norm), winT, cos2, sin2)
        s["oraw"], s["states"] = retention_fwd(f"retention_fwd{layer}", s["zq"], dec)
        s["pooled"], p = pool_fwd(f"pool_fwd{layer}", zu, pool_maps[layer], row(pool_scale), pad)
        tok = forward("ffn2", layer, p)
        h, s["rp"], s["mixed"] = mix_out_fwd(
            f"mix_out_fwd{layer}", h, s["oraw"], s["zg"], s["zgate"], p, wretT, wpoolT, wout, (tok,))
        s["h2"] = h
        g2T, u2T, d2 = gathered("ffn2", layer, h)
        tok = (forward("ffn1", layer + 1, h),) if layer + 1 < depth else ()
        h, s["a2"], s["b2"] = ffn_fwd(f"ffn2_fwd{layer}", h, row(ffn2_norm), g2T, u2T, d2, tok)
        saved.append(s)
        weights.append((g1T, u1T, g2T, u2T, winT, wretT, wpoolT, d1, d2, wout))

    dh, loss_part, dg_final = final_loss("final_loss", h, final_norm.reshape(1, d), loss_target[0])

    small = {}
    small_gathers = {}
    exchanges = {}
    token = None

    def rows8(vals):
        at = lax.broadcasted_iota(jnp.int32, (8, d), 0)
        out = jnp.zeros((8, d), f32)
        for k, v in enumerate(vals):
            r0 = v[0:1]
            r0 = r0 if r0.shape[1] == d else jnp.pad(r0, ((0, 0), (0, d - r0.shape[1])))
            out = jnp.where(at == k, r0, out)
        return out

    def exchange(part, layer, grads):
        by_dest = [g.reshape(g.shape[0], NDEV, g.shape[1] // NDEV, g.shape[2]) for g in grads]
        ssem, rsem, sent, lands, tok = exchange_start(f"exchange_start_{part}{layer}", by_dest)
        exchanges[(part, layer)] = (ssem, rsem, sent, lands)
        return (tok,)

    for layer in reversed(range(depth)):
        g1T, u1T, g2T, u2T, winT, wretT, wpoolT, d1, d2, wout = weights[layer]
        row = lambda a: a[layer:layer + 1]
        s = saved[layer]
        dh, lhs2, rhs2, small[("ffn2", layer)] = ffn_bwd(
            f"ffn2_bwd{layer}", dh, s["h2"], row(ffn2_norm), s["a2"], s["b2"], g2T, u2T, d2, () if token is None else token)
        token = exchange("ffn2", layer, [mm_tn(f"ffn2_wgrad{layer}", lhs2, rhs2, lambda b: b // 2)])
        do, dzg, dzgate, dp, drp, dyb = mix_out_bwd(
            f"mix_out_bwd{layer}", dh, s["oraw"], s["zg"], s["zgate"], s["rp"], wretT, wpoolT, wout, token)
        gw_mix = [mm_tn(f"w_out_wgrad{layer}", s["mixed"], dyb, lambda b: b),
                  mm_tn(f"up_wgrad{layer}", drp, s["rp"], lambda b: b)]
        dq, dk, dv = retention_bwd(f"retention_bwd{layer}", s["zq"], cos2, sin2, dec, s["states"], do, pad, token)
        dzu, small[("maps", layer)], small[("scale", layer)] = pool_bwd(
            f"pool_bwd{layer}", dp, s["pooled"], pool_maps[layer], row(pool_scale), pad)
        dh, dz, n2, small[("mix", layer)] = mix_in_bwd(
            f"mix_in_bwd{layer}", dq, dk, dv, dzg, dzu, dzgate, s["h1"], row(mix_norm), winT, dh)
        token = exchange("mix", layer, gw_mix + [mm_tn(f"w_in_wgrad{layer}", dz, n2, lambda b: b)])
        dh, lhs1, rhs1, small[("ffn1", layer)] = ffn_bwd(
            f"ffn1_bwd{layer}", dh, s["h0"], row(ffn1_norm), s["a1"], s["b1"], g1T, u1T, d1, token)
        rows = [small[("ffn1", layer)], small[("mix", layer)], small[("ffn2", layer)], small[("scale", layer)]]
        packs = [rows8([loss_part, dg_final] + rows if layer == depth - 1 else rows), small[("maps", layer)]]
        if layer == 0:
            dmeta = dh[pad:CHUNK]
            packs.append(jnp.transpose(dmeta.reshape(N_META, NDEV, d // NDEV), (1, 0, 2)).reshape(NDEV * N_META, d // NDEV))
        ssem, rsem, lands, tok = gather_start(f"small_start{layer}", slot_in(f"small_slot{layer}", me, packs))
        small_gathers[layer] = (ssem, rsem, lands)
        if layer:
            token = exchange("ffn1", layer, [mm_tn(f"ffn1_wgrad{layer}", lhs1, rhs1, lambda b: b // 2, (tok,))])
        else:
            token = (tok,)
            for j, nm in enumerate(("ffn1_gate", "ffn1_up", "ffn1_down")):
                token = exchange(nm, layer, [mm_tn(f"{nm}_wgrad{layer}", lhs1, rhs1, lambda b: b // 2, token, only=j)])

    grad_x = (dh[CHUNK:] + token[0][0, 0])[None]

    big = {}
    after = token[0]
    plans = {
        "ffn2": [("ffn2_gate", 0, 0, False, *t_g2), ("ffn2_up", 0, 1, False, *t_u2),
                 ("ffn2_down", 0, 2, False, ffn2_down, m_ffn2_down, v_ffn2_down)],
        "mix": [("w_out", 0, 0, False, w_out, m_w_out, v_w_out),
                ("w_ret_up", 1, 0, True, w_ret_up, m_w_ret_up, v_w_ret_up),
                ("w_pool_up", 1, 1, True, w_pool_up, m_w_pool_up, v_w_pool_up), ("w_in", 2, 0, False, *t_in)],
        "ffn1": [("ffn1_gate", 0, 0, False, *t_g1), ("ffn1_up", 0, 1, False, *t_u1),
                 ("ffn1_down", 0, 2, False, ffn1_down, m_ffn1_down, v_ffn1_down)]}
    for nm, k, b, tr, w, m, v in plans["ffn1"]:
        plans[nm] = [(nm, 0, 0, tr, w, m, v)]
    for layer in reversed(range(depth)):
        for part in ("ffn2", "mix") + (("ffn1",) if layer else ("ffn1_gate", "ffn1_up", "ffn1_down")):
            ssem, rsem, sent, lands = exchanges[(part, layer)]
            sent, lands = copies_wait(f"exchange_wait_{part}{layer}", ssem, rsem, sent, lands, after)
            for nm, k, b, tr, w, m, v in plans[part]:
                big[nm] = adam_big(f"adam_{nm}{layer}", me, lands[k], sent[k], b, layer, tr, w, m, v, big.get(nm))
                after = big[nm][0]

    gsmall = []
    for layer in range(depth):
        ssem, rsem, lands = small_gathers[layer]
        gsmall += copies_wait(f"small_wait{layer}", ssem, rsem, (), lands, after)[1]

    maps2 = lambda a: a.reshape(depth * HEADS * HD, HD)
    res = adam_small(
        "adam_small", *gsmall,
        [(ffn1_norm, m_ffn1_norm, v_ffn1_norm), (mix_norm, m_mix_norm, v_mix_norm), (ffn2_norm, m_ffn2_norm, v_ffn2_norm)],
        (pool_scale, m_pool_scale, v_pool_scale), (maps2(pool_maps), maps2(m_pool_maps), maps2(v_pool_maps)),
        (meta, m_meta, v_meta), tuple(a.reshape(1, d) for a in (final_norm, m_final_norm, v_final_norm)), d)
    loss = res[0][0, 0]
    sm = {}
    for k, nm in enumerate(["ffn1_norm", "mix_norm", "ffn2_norm", "pool_scale", "pool_maps", "meta", "final_norm"]):
        sm[nm] = list(res[1 + 4 * k:5 + 4 * k])
    sm["pool_maps"] = [a.reshape(pool_maps.shape) for a in sm["pool_maps"]]
    sm["final_norm"] = [a.reshape(d) for a in sm["final_norm"]]

    names = ["meta", "ffn1_norm", "ffn1_gate", "ffn1_up", "ffn1_down", "mix_norm", "w_in", "pool_maps", "pool_scale",
             "w_ret_up", "w_pool_up", "w_out", "ffn2_norm", "ffn2_gate", "ffn2_up", "ffn2_down", "final_norm"]
    for nm in ("ffn1_gate", "ffn1_up", "ffn2_gate", "ffn2_up", "w_in"):
        big[nm] = tview(*big[nm])
    allw = {**{k: list(v) for k, v in big.items()}, **sm}
    outs = [loss, grad_x]
    for kind in range(4):
        outs += [allw[nm][kind] for nm in names]
    return tuple(outs)
```

```python
import functools

import jax
import jax.numpy as jnp
from jax import lax
from jax.experimental import pallas as pl
from jax.experimental.pallas import tpu as pltpu

f32 = jnp.float32
bf16 = jnp.bfloat16
MESH = pl.DeviceIdType.MESH
NDEV = 8
N_META = 16
HEADS = 4
HD = 128
CHUNK = 128
RW = HEADS * HD
POOL_WINDOWS = (2, 4, 8, 16)
ROPE_BASE = 10000.0
EPS = 1e-6
ADAM_LR = 0.001
ADAM_B1 = 0.9
ADAM_B2 = 0.999
ADAM_EPS = 1e-08
ADAM_WD = 0.01
ADAM_STEP = 10
VMEM_CAP_MB = 60


def _cp(sem=None):
    return pltpu.CompilerParams(vmem_limit_bytes=VMEM_CAP_MB * 2**20, dimension_semantics=sem)


def _row_tile(tp, want=384):
    return want if tp % want == 0 else 128


def _resident(shape):
    nd = len(shape)
    return pl.BlockSpec(shape, lambda *_: (0,) * nd, pipeline_mode=pl.Buffered(1))


def _skip(nd, body):
    return (lambda *refs: body(*refs[nd:])) if nd else body


def _dot_nn(a, b):
    return lax.dot_general(a, b, (((1,), (0,)), ((), ())), preferred_element_type=f32)


def _dot_nt(a, b):
    return lax.dot_general(a, b, (((1,), (1,)), ((), ())), preferred_element_type=f32)


def _dot_tn(a, b):
    return lax.dot_general(a, b, (((0,), (0,)), ((), ())), preferred_element_type=f32)


def _rms(h):
    rs = lax.rsqrt(jnp.mean(h * h, axis=-1, keepdims=True) + EPS)
    return h * rs, rs


def _rms_bwd(dn, g, hhat, rs):
    dhh = dn * g
    return rs * (dhh - hhat * jnp.mean(dhh * hhat, axis=-1, keepdims=True))


def _sigmoid(x):
    return jax.nn.sigmoid(x)


def _me():
    return lax.axis_index("x"), lax.axis_index("y"), lax.axis_index("c")


def _peer(idx):
    return (idx // 4, (idx // 2) % 2, idx % 2)


def all_gather(name, arrays):
    n = len(arrays)

    def body(*refs):
        ins, outs = refs[:n], refs[n:2 * n]
        send_sems, recv_sems, local_sems = refs[2 * n:]
        x, y, c = _me()
        me = 4 * x + 2 * y + c
        locals_ = []
        for k in range(n):
            cp = pltpu.make_async_copy(ins[k], outs[k].at[me], local_sems.at[k])
            cp.start()
            locals_.append(cp)
        for d in range(1, NDEV):
            for k in range(n):
                pltpu.make_async_remote_copy(
                    src_ref=ins[k], dst_ref=outs[k].at[me], send_sem=send_sems.at[k], recv_sem=recv_sems.at[k],
                    device_id=_peer((me + d) % NDEV), device_id_type=MESH).start()
        for k in range(n):
            seven = outs[k].at[pl.ds(0, NDEV - 1)]
            w = pltpu.make_async_remote_copy(src_ref=seven, dst_ref=seven, send_sem=send_sems.at[k],
                                             recv_sem=recv_sems.at[k], device_id=(x, y, c), device_id_type=MESH)
            w.wait_send()
            w.wait_recv()
            locals_[k].wait()

    anyspec = pl.BlockSpec(memory_space=pl.ANY)
    return pl.pallas_call(
        body, name=name,
        out_shape=[jax.ShapeDtypeStruct((NDEV,) + a.shape, a.dtype) for a in arrays],
        in_specs=[anyspec] * n, out_specs=[anyspec] * n,
        scratch_shapes=[pltpu.SemaphoreType.DMA((n,)), pltpu.SemaphoreType.DMA((n,)), pltpu.SemaphoreType.DMA((n,))],
    )(*arrays)


_HBM = pl.BlockSpec(memory_space=pltpu.HBM)
_SEM = pl.BlockSpec(memory_space=pltpu.SEMAPHORE)
_ANY = pl.BlockSpec(memory_space=pl.ANY)
_EFFECT = pltpu.SideEffectType.DATAFLOW_SIDE_EFFECTING


def _in_hbm(a):
    return pltpu.with_memory_space_constraint(a, pltpu.HBM)


def gather_start(name, lands, deps=()):
    n, nd = len(lands), len(deps)

    def body(*refs):
        land = refs[nd:nd + n]
        send_sems, recv_sems = refs[nd + n:nd + n + 2]
        token = refs[-1]
        x, y, c = _me()
        me = 4 * x + 2 * y + c
        for d in range(1, NDEV):
            for k in range(n):
                pltpu.make_async_remote_copy(
                    src_ref=land[k].at[me], dst_ref=land[k].at[me], send_sem=send_sems.at[k], recv_sem=recv_sems.at[k],
                    device_id=_peer((me + d) % NDEV), device_id_type=MESH).start()
        token[...] = jnp.zeros_like(token)

    res = pl.pallas_call(
        body, name=name,
        out_shape=(pltpu.SemaphoreType.DMA((n,)), pltpu.SemaphoreType.DMA((n,)),
                   *[pltpu.HBM(a.shape, a.dtype) for a in lands], jax.ShapeDtypeStruct((8, 128), f32)),
        in_specs=[_ANY] * nd + [_HBM] * n,
        out_specs=(_SEM, _SEM, *[_HBM] * n, pl.BlockSpec(memory_space=pltpu.VMEM)),
        input_output_aliases={nd + k: 2 + k for k in range(n)},
        compiler_params=pltpu.CompilerParams(has_side_effects=_EFFECT),
    )(*deps, *[_in_hbm(a) for a in lands])
    return res[0], res[1], list(res[2:2 + n]), res[-1]


def _other_chips(x, y):
    return [(1 - x, y), (x, 1 - y), (1 - x, 1 - y)]


def gather_start_chips(name, groups, deps=()):
    sizes = [len(g) for g in groups]
    lands = [a for g in groups for a in g]
    n, nd, ng = len(lands), len(deps), len(groups)

    def body(*refs):
        land = refs[nd:nd + n]
        sems = refs[nd + n:nd + n + 2 * ng]
        token = refs[-1]
        x, y, c = _me()
        me = 4 * x + 2 * y + c
        k = 0
        for g, size in enumerate(sizes):
            for j in range(size):
                for to in [(x, y, 1 - c)] + [(cx, cy, c) for cx, cy in _other_chips(x, y)]:
                    pltpu.make_async_remote_copy(
                        src_ref=land[k].at[me], dst_ref=land[k].at[me], send_sem=sems[2 * g].at[j],
                        recv_sem=sems[2 * g + 1].at[j], device_id=to, device_id_type=MESH).start()
                k += 1
        token[...] = jnp.zeros_like(token)

    res = pl.pallas_call(
        body, name=name,
        out_shape=(*[pltpu.SemaphoreType.DMA((size,)) for size in sizes for _ in range(2)],
                   *[pltpu.HBM(a.shape, a.dtype) for a in lands], jax.ShapeDtypeStruct((8, 128), f32)),
        in_specs=[_ANY] * nd + [_HBM] * n,
        out_specs=(*[_SEM] * (2 * ng), *[_HBM] * n, pl.BlockSpec(memory_space=pltpu.VMEM)),
        input_output_aliases={nd + k: 2 * ng + k for k in range(n)},
        compiler_params=pltpu.CompilerParams(has_side_effects=_EFFECT),
    )(*deps, *[_in_hbm(a) for a in lands])
    out, k = [], 2 * ng
    for g, size in enumerate(sizes):
        out.append((res[2 * g], res[2 * g + 1], list(res[k:k + size])))
        k += size
    return out, res[-1]


def gather_forward(name, send_sems, recv_sems, lands, after):
    n = len(lands)

    def body(*refs):
        land = refs[:n]
        ssem, rsem = refs[n:n + 2]
        send2, recv2 = refs[n + 3:n + 5]
        token = refs[-1]
        x, y, c = _me()
        for k in range(n):
            four = land[k].at[pl.ds(0, 4)]
            w = pltpu.make_async_remote_copy(src_ref=four, dst_ref=four, send_sem=ssem.at[k], recv_sem=rsem.at[k],
                                             device_id=(x, y, c), device_id_type=MESH)
            w.wait_send()
            w.wait_recv()
            for cx, cy in _other_chips(x, y):
                slot = 4 * cx + 2 * cy + c
                pltpu.make_async_remote_copy(
                    src_ref=land[k].at[slot], dst_ref=land[k].at[slot], send_sem=send2.at[k], recv_sem=recv2.at[k],
                    device_id=(x, y, 1 - c), device_id_type=MESH).start()
        token[...] = jnp.zeros_like(token)

    res = pl.pallas_call(
        body, name=name,
        out_shape=(pltpu.SemaphoreType.DMA((n,)), pltpu.SemaphoreType.DMA((n,)),
                   *[pltpu.HBM(a.shape, a.dtype) for a in lands], jax.ShapeDtypeStruct((8, 128), f32)),
        in_specs=[_HBM] * n + [_SEM, _SEM, _ANY],
        out_specs=(_SEM, _SEM, *[_HBM] * n, pl.BlockSpec(memory_space=pltpu.VMEM)),
        input_output_aliases={k: 2 + k for k in range(n)},
        compiler_params=pltpu.CompilerParams(has_side_effects=_EFFECT),
    )(*lands, send_sems, recv_sems, after)
    return res[0], res[1], list(res[2:2 + n]), res[-1]


def exchange_start(name, grads, deps=()):
    n, nd = len(grads), len(deps)
    lands = [lax.empty((NDEV - 1, g.shape[0]) + g.shape[2:], g.dtype) for g in grads]

    def body(*refs):
        src = refs[nd:nd + n]
        land = refs[nd + n:nd + 2 * n]
        send_sems, recv_sems = refs[nd + 2 * n:nd + 2 * n + 2]
        token = refs[-1]
        x, y, c = _me()
        me = 4 * x + 2 * y + c
        for d in range(1, NDEV):
            p = (me + d) % NDEV
            for k in range(n):
                pltpu.make_async_remote_copy(
                    src_ref=src[k].at[:, p], dst_ref=land[k].at[d - 1], send_sem=send_sems.at[k], recv_sem=recv_sems.at[k],
                    device_id=_peer(p), device_id_type=MESH).start()
        token[...] = jnp.zeros_like(token)

    both = list(grads) + lands
    res = pl.pallas_call(
        body, name=name,
        out_shape=(pltpu.SemaphoreType.DMA((n,)), pltpu.SemaphoreType.DMA((n,)),
                   *[pltpu.HBM(a.shape, a.dtype) for a in both], jax.ShapeDtypeStruct((8, 128), f32)),
        in_specs=[_ANY] * nd + [_HBM] * (2 * n),
        out_specs=(_SEM, _SEM, *[_HBM] * (2 * n), pl.BlockSpec(memory_space=pltpu.VMEM)),
        input_output_aliases={nd + k: 2 + k for k in range(2 * n)},
        compiler_params=pltpu.CompilerParams(has_side_effects=_EFFECT),
    )(*deps, *[_in_hbm(a) for a in both])
    return res[0], res[1], list(res[2:2 + n]), list(res[2 + n:2 + 2 * n]), res[-1]


def copies_wait(name, send_sems, recv_sems, sent, lands, after, count=NDEV - 1):
    ns, n = len(sent), len(lands)

    def body(*refs):
        land = refs[ns:ns + n]
        ssem, rsem = refs[ns + n:ns + n + 2]
        x, y, c = _me()
        for k in range(n):
            seven = land[k].at[pl.ds(0, count)]
            w = pltpu.make_async_remote_copy(src_ref=seven, dst_ref=seven, send_sem=ssem.at[k], recv_sem=rsem.at[k],
                                             device_id=(x, y, c), device_id_type=MESH)
            w.wait_send()
            w.wait_recv()

    both = list(sent) + list(lands)
    res = pl.pallas_call(
        body, name=name, out_shape=tuple(pltpu.HBM(a.shape, a.dtype) for a in both),
        in_specs=[_HBM] * (ns + n) + [_SEM, _SEM, _ANY], out_specs=tuple([_HBM] * (ns + n)),
        input_output_aliases={k: k for k in range(ns + n)},
        compiler_params=pltpu.CompilerParams(has_side_effects=_EFFECT),
    )(*both, send_sems, recv_sems, after)
    return list(res[:ns]), list(res[ns:])


def prep_layer(layer, me, col_sharded, row_sharded, deps=()):
    nc, nr = len(col_sharded), len(row_sharded)

    def body(me_ref, *refs):
        ins, outs = refs[:nc + nr], refs[nc + nr + len(deps):]
        for k in range(nc):
            outs[k][...] = ins[k][...].T.astype(bf16)
        for k in range(nc, nc + nr):
            outs[k][...] = ins[k][...].astype(bf16)

    arrs = list(col_sharded) + list(row_sharded)
    in_specs = [pl.BlockSpec((None,) + a.shape[1:], lambda i, me_ref: (layer, 0, 0)) for a in arrs]
    shapes = [(a.shape[2], a.shape[1]) for a in col_sharded] + [a.shape[1:] for a in row_sharded]
    out_specs = [pl.BlockSpec((None,) + s, lambda i, me_ref: (me_ref[0], 0, 0)) for s in shapes]
    return pl.pallas_call(
        body, name=f"prep_layer{layer}",
        grid_spec=pltpu.PrefetchScalarGridSpec(num_scalar_prefetch=1, grid=(1,), in_specs=in_specs + [_ANY] * len(deps),
                                               out_specs=out_specs),
        out_shape=[jax.ShapeDtypeStruct((NDEV,) + s, bf16) for s in shapes], compiler_params=_cp())(me, *arrs, *deps)


def slot_in(name, me, arrays):
    n = len(arrays)

    def body(me_ref, *refs):
        for k in range(n):
            refs[n + k][...] = refs[k][...]

    in_specs = [pl.BlockSpec(a.shape, lambda i, me_ref: (0, 0)) for a in arrays]
    out_specs = [pl.BlockSpec((None,) + a.shape, lambda i, me_ref: (me_ref[0], 0, 0)) for a in arrays]
    return pl.pallas_call(
        body, name=name,
        grid_spec=pltpu.PrefetchScalarGridSpec(num_scalar_prefetch=1, grid=(1,), in_specs=in_specs, out_specs=out_specs),
        out_shape=[jax.ShapeDtypeStruct((NDEV,) + a.shape, a.dtype) for a in arrays])(me, *arrays)


def _ff_chunks(ff, want=768):
    if ff % 256:
        return [slice(0, ff)]
    return [slice(c, min(c + want, ff)) for c in range(0, ff, want)]


def ffn_fwd(name, h, g, wgT, wuT, wd, deps=()):
    tp, d = h.shape
    ff = wgT.shape[0]
    tm = _row_tile(tp, 704)

    def body(h_ref, g_ref, wg_ref, wu_ref, wd_ref, ho_ref, a_ref, b_ref):
        hh = h_ref[...]
        hhat, _ = _rms(hh)
        n = (hhat * g_ref[...]).astype(bf16)
        acc = None
        for cols in _ff_chunks(ff):
            a = _dot_nt(n, wg_ref[cols, :])
            b = _dot_nt(n, wu_ref[cols, :])
            part = _dot_nn(((a * _sigmoid(a)) * b).astype(bf16), wd_ref[cols, :])
            acc = part if acc is None else acc + part
            a_ref[:, cols] = a.astype(bf16)
            b_ref[:, cols] = b.astype(bf16)
        ho_ref[...] = hh + 0.5 * acc

    row = lambda w: pl.BlockSpec((tm, w), lambda i: (i, 0))
    return pl.pallas_call(
        _skip(len(deps), body), name=name, grid=(tp // tm,),
        in_specs=[_ANY] * len(deps) + [row(d), _resident((1, d)), _resident((ff, d)), _resident((ff, d)), _resident((ff, d))],
        out_specs=[row(d), row(ff), row(ff)],
        out_shape=[jax.ShapeDtypeStruct((tp, d), f32), jax.ShapeDtypeStruct((tp, ff), bf16),
                   jax.ShapeDtypeStruct((tp, ff), bf16)],
        compiler_params=_cp(("arbitrary",)))(*deps, h, g, wgT, wuT, wd)


def ffn_bwd(name, dy, h, g, a, b, wgT, wuT, wd, deps=()):
    tp, d = h.shape
    ff = wgT.shape[0]
    tm = _row_tile(tp, 384)

    def body(dy_ref, h_ref, g_ref, a_ref, b_ref, wg_ref, wu_ref, wd_ref, dh_ref, lhs_ref, rhs_ref, dg_ref):
        dyv = dy_ref[...]
        hhat, rs = _rms(h_ref[...])
        gv = g_ref[...]
        n = hhat * gv
        dyh = (0.5 * dyv).astype(bf16)
        dn = None
        for cols in _ff_chunks(ff):
            ds = _dot_nt(dyh, wd_ref[cols, :])
            av = a_ref[:, cols].astype(f32)
            bv = b_ref[:, cols].astype(f32)
            sg = _sigmoid(av)
            sa = av * sg
            da = (ds * bv * (sg * (1.0 + av * (1.0 - sg)))).astype(bf16)
            db = (ds * sa).astype(bf16)
            part = _dot_nn(da, wg_ref[cols, :]) + _dot_nn(db, wu_ref[cols, :])
            dn = part if dn is None else dn + part
            lhs_ref[0, :, cols] = da
            lhs_ref[1, :, cols] = db
            lhs_ref[2, :, cols] = (sa * bv).astype(bf16)
        dh_ref[...] = dyv + _rms_bwd(dn, gv, hhat, rs)

        @pl.when(pl.program_id(0) == 0)
        def _():
            dg_ref[...] = jnp.zeros_like(dg_ref)

        dg_ref[0:1, :] += jnp.sum(dn * hhat, axis=0, keepdims=True)
        rhs_ref[0] = n.astype(bf16)
        rhs_ref[1] = dyh

    row = lambda w: pl.BlockSpec((tm, w), lambda i: (i, 0))
    return pl.pallas_call(
        _skip(len(deps), body), name=name, grid=(tp // tm,),
        in_specs=[_ANY] * len(deps) + [row(d), row(d), _resident((1, d)), row(ff), row(ff),
                  _resident((ff, d)), _resident((ff, d)), _resident((ff, d))],
        out_specs=[row(d), pl.BlockSpec((3, tm, ff), lambda i: (0, i, 0)), pl.BlockSpec((2, tm, d), lambda i: (0, i, 0)),
                   pl.BlockSpec((8, d), lambda i: (0, 0))],
        out_shape=[jax.ShapeDtypeStruct((tp, d), f32), jax.ShapeDtypeStruct((3, tp, ff), bf16),
                   jax.ShapeDtypeStruct((2, tp, d), bf16), jax.ShapeDtypeStruct((8, d), f32)],
        compiler_params=_cp(("arbitrary",)))(*deps, dy, h, g, a, b, wgT, wuT, wd)


def mm_tn(name, lhs, rhs, rhs_of, deps=(), only=None):
    _, tp, m = lhs.shape
    b0, nb = (0, lhs.shape[0]) if only is None else (only, 1)
    n = rhs.shape[2]
    def fits(t, ms):
        mb = m // ms
        return (tp % t == 0 and m % (128 * ms) == 0
                and 2 * t * (mb + n) * 2 + mb * n * (2 * 2 + 4 + (4 if t < tp else 0)) <= 54 * 2**20)

    tk, msplit = next(((t, ms) for t in (tp, 1408, 704, 384) for ms in (1, 2, 4) if fits(t, ms)), (128, 1))
    nk = tp // tk
    mb = m // msplit

    def body(l_ref, r_ref, o_ref, acc_ref):
        if nk == 1:
            o_ref[...] = _dot_tn(l_ref[...], r_ref[...]).astype(o_ref.dtype)
            return
        k = pl.program_id(2)

        @pl.when(k == 0)
        def _():
            acc_ref[...] = jnp.zeros_like(acc_ref)

        acc_ref[...] += _dot_tn(l_ref[...], r_ref[...])

        @pl.when(k == nk - 1)
        def _():
            o_ref[...] = acc_ref[...].astype(o_ref.dtype)

    return pl.pallas_call(
        _skip(len(deps), body), name=name, grid=(nb, msplit, nk),
        in_specs=[_ANY] * len(deps) + [pl.BlockSpec((None, tk, mb), lambda b, j, k: (b0 + b, k, j)),
                                       pl.BlockSpec((None, tk, n), lambda b, j, k: (rhs_of(b0 + b), k, 0))],
        out_specs=pl.BlockSpec((None, mb, n), lambda b, j, k: (b, j, 0)),
        out_shape=jax.ShapeDtypeStruct((nb, m, n), bf16),
        scratch_shapes=[pltpu.VMEM((mb, n) if nk > 1 else (8, 128), f32)],
        compiler_params=_cp(("arbitrary", "arbitrary", "arbitrary")))(*deps, lhs, rhs)


def mix_in_fwd(name, h, g, winT, cos2, sin2):
    tp, d = h.shape
    nin = winT.shape[0]
    tm = _row_tile(tp)

    def body(h_ref, g_ref, w_ref, cos_ref, sin_ref, zq_ref, zg_ref, zu_ref, zgate_ref):
        hhat, _ = _rms(h_ref[...])
        z = _dot_nt((hhat * g_ref[...]).astype(bf16), w_ref[...])
        cosv, sinv = cos_ref[...], sin_ref[...]
        for hh in range(HEADS):
            qcols, kcols = slice(hh * HD, (hh + 1) * HD), slice(RW + hh * HD, RW + (hh + 1) * HD)
            zq_ref[:, qcols] = (_rot(z[:, qcols], cosv, sinv) * HD ** -0.5).astype(bf16)
            zq_ref[:, kcols] = _rot(z[:, kcols], cosv, sinv).astype(bf16)
        zq_ref[:, 2 * RW:] = z[:, 2 * RW:3 * RW].astype(bf16)
        zg_ref[...] = z[:, 3 * RW:4 * RW].astype(zg_ref.dtype)
        zu_ref[...] = z[:, 4 * RW:5 * RW]
        zgate_ref[...] = z[:, 5 * RW:].astype(zgate_ref.dtype)

    row = lambda w: pl.BlockSpec((tm, w), lambda i: (i, 0))
    widths = (3 * RW, RW, RW, 2 * d)
    return pl.pallas_call(
        body, name=name, grid=(tp // tm,),
        in_specs=[row(d), _resident((1, d)), _resident((nin, d)), row(HD), row(HD)],
        out_specs=[row(w) for w in widths],
        out_shape=[jax.ShapeDtypeStruct((tp, w), dt) for w, dt in zip(widths, (bf16, bf16, f32, bf16))],
        compiler_params=_cp(("arbitrary",)))(h, g, winT, cos2, sin2)


def mix_in_bwd(name, dq, dk, dv, dzg, dzu, dzgate, h, g, winT, dres):
    tp, d = h.shape
    nin = winT.shape[0]
    tm = _row_tile(tp)

    def body(dq_ref, dk_ref, dv_ref, dzg_ref, dzu_ref, dzgate_ref, h_ref, g_ref, w_ref, dres_ref, dh_ref, dz_ref, n_ref, dg_ref):
        dn, col = None, 0
        for piece in (dq_ref, dk_ref, dv_ref, dzg_ref, dzu_ref, dzgate_ref):
            v = piece[...]
            part = _dot_nn(v, w_ref[col:col + v.shape[1], :])
            dn = part if dn is None else dn + part
            dz_ref[:, col:col + v.shape[1]] = v
            col += v.shape[1]
        hhat, rs = _rms(h_ref[...])
        gv = g_ref[...]
        dh_ref[...] = dres_ref[...] + _rms_bwd(dn, gv, hhat, rs)

        @pl.when(pl.program_id(0) == 0)
        def _():
            dg_ref[...] = jnp.zeros_like(dg_ref)

        dg_ref[0:1, :] += jnp.sum(dn * hhat, axis=0, keepdims=True)
        n_ref[...] = (hhat * gv).astype(bf16)

    row = lambda w: pl.BlockSpec((tm, w), lambda i: (i, 0))
    return pl.pallas_call(
        body, name=name, grid=(tp // tm,),
        in_specs=[row(RW)] * 5 + [row(2 * d), row(d), _resident((1, d)), _resident((nin, d)), row(d)],
        out_specs=[row(d), pl.BlockSpec((None, tm, nin), lambda i: (0, i, 0)), pl.BlockSpec((None, tm, d), lambda i: (0, i, 0)),
                   pl.BlockSpec((8, d), lambda i: (0, 0))],
        out_shape=[jax.ShapeDtypeStruct((tp, d), f32), jax.ShapeDtypeStruct((1, tp, nin), bf16),
                   jax.ShapeDtypeStruct((1, tp, d), bf16), jax.ShapeDtypeStruct((8, d), f32)],
        compiler_params=_cp(("arbitrary",)))(dq, dk, dv, dzg, dzu, dzgate, h, g, winT, dres)


def _retention_tables(tp, pad, zero):
    half = HD // 2
    lane = jnp.arange(HD)
    inv_freq = ROPE_BASE ** (-(lane % half).astype(f32) / half)
    pos = jnp.arange(tp, dtype=f32) - pad + zero
    ang = pos[:, None] * inv_freq[None, :]
    cos2 = jnp.cos(ang)
    sin2 = jnp.where(lane[None, :] < half, -1.0, 1.0) * jnp.sin(ang)
    log_gamma = jnp.log1p(-(2.0 ** (-5.0 - jnp.arange(HEADS, dtype=f32))))
    idx = jnp.arange(CHUNK, dtype=f32)
    diff = idx[:, None] - idx[None, :]
    intra = jnp.where(diff[None] >= 0, jnp.exp(diff[None] * log_gamma[:, None, None]), 0.0)
    k_decay = jnp.exp((CHUNK - 1.0 - idx)[None, :] * log_gamma[:, None])
    q_decay = jnp.exp((idx + 1.0)[None, :] * log_gamma[:, None])
    chunk_decay = jnp.exp(CHUNK * log_gamma)
    full = (HEADS, CHUNK, HD)
    dec = jnp.stack([intra, jnp.broadcast_to(k_decay[:, :, None], full), jnp.broadcast_to(q_decay[:, :, None], full),
                     jnp.broadcast_to(chunk_decay[:, None, None], full)], axis=1)
    return cos2, sin2, dec


def _rot(t, cos2, sin2):
    return t * cos2 + pltpu.roll(t, HD // 2, 1) * sin2


def _rot_t(t, cos2, sin2):
    return t * cos2 - pltpu.roll(t, HD // 2, 1) * sin2


def _chunks_per_step(nch):
    return 3 if nch % 3 == 0 else 1


def retention_fwd(name, zq, dec):
    tp = zq.shape[0]
    nch = tp // CHUNK
    per = _chunks_per_step(nch)

    def body(q_ref, k_ref, v_ref, dec_ref, out_ref, st_ref, s_ref):
        @pl.when(pl.program_id(0) == 0)
        def _():
            s_ref[...] = jnp.zeros_like(s_ref)

        state = [s_ref[hh] for hh in range(HEADS)]
        for j in range(per):
            rows = slice(j * CHUNK, (j + 1) * CHUNK)
            for hh in range(HEADS):
                cols = slice(hh * HD, (hh + 1) * HD)
                qb, kb, vb = q_ref[rows, cols], k_ref[rows, cols], v_ref[rows, cols]
                sc = (_dot_nt(qb, kb) * dec_ref[hh, 0]).astype(bf16)
                sb = state[hh].astype(bf16)
                cross = _dot_nn((qb.astype(f32) * dec_ref[hh, 2]).astype(bf16), sb)
                out_ref[rows, cols] = (_dot_nn(sc, vb) + cross).astype(out_ref.dtype)
                st_ref[hh, j] = sb
                state[hh] = state[hh] * dec_ref[hh, 3] + _dot_tn((kb.astype(f32) * dec_ref[hh, 1]).astype(bf16), vb)
        for hh in range(HEADS):
            s_ref[hh] = state[hh]

    part = lambda j: pl.BlockSpec((per * CHUNK, RW), lambda n: (n, j))
    return pl.pallas_call(
        body, name=name, grid=(nch // per,),
        in_specs=[part(0), part(1), part(2), _resident((HEADS, 4, CHUNK, HD))],
        out_specs=[part(0), pl.BlockSpec((HEADS, per, HD, HD), lambda n: (0, n, 0, 0))],
        out_shape=[jax.ShapeDtypeStruct((tp, RW), bf16), jax.ShapeDtypeStruct((HEADS, nch, HD, HD), bf16)],
        scratch_shapes=[pltpu.VMEM((HEADS, HD, HD), f32)],
        compiler_params=_cp(("arbitrary",)))(zq, zq, zq, dec)


def retention_bwd(name, zq, cos2, sin2, dec, states, dout, pad, deps=()):
    tp = zq.shape[0]
    nch = tp // CHUNK
    per = _chunks_per_step(nch)
    nblk = nch // per
    scale = HD ** -0.5

    def body(q_ref, k_ref, v_ref, cos_ref, sin_ref, dec_ref, st_ref, do_ref, dq_ref, dk_ref, dv_ref, g_ref):
        @pl.when(pl.program_id(0) == 0)
        def _():
            g_ref[...] = jnp.zeros_like(g_ref)

        first_row = (nblk - 1 - pl.program_id(0)) * (per * CHUNK)
        gstate = [g_ref[hh] for hh in range(HEADS)]
        for j in reversed(range(per)):
            rows = slice(j * CHUNK, (j + 1) * CHUNK)
            cosv, sinv = cos_ref[rows, :], sin_ref[rows, :]
            keep = (lax.broadcasted_iota(jnp.int32, (CHUNK, HD), 0) + (first_row + j * CHUNK)) >= pad
            for hh in range(HEADS):
                cols = slice(hh * HD, (hh + 1) * HD)
                intra, kdec, qdec = dec_ref[hh, 0], dec_ref[hh, 1], dec_ref[hh, 2]
                qb, kb, vb = q_ref[rows, cols], k_ref[rows, cols], v_ref[rows, cols]
                qd = (qb.astype(f32) * qdec).astype(bf16)
                kd = (kb.astype(f32) * kdec).astype(bf16)
                sc = (_dot_nt(qb, kb) * intra).astype(bf16)
                dob = do_ref[rows, cols]
                sb = st_ref[hh, j]
                gb = gstate[hh].astype(bf16)
                dsc = (_dot_nt(dob, vb) * intra).astype(bf16)
                dv = _dot_tn(sc, dob) + _dot_nn(kd, gb)
                dqr = _dot_nn(dsc, kb) + _dot_nt(dob, sb) * qdec
                dkr = _dot_tn(dsc, qb) + _dot_nt(vb, gb) * kdec
                gstate[hh] = gstate[hh] * dec_ref[hh, 3] + _dot_tn(qd, dob)
                dq_ref[rows, cols] = jnp.where(keep, _rot_t(dqr * scale, cosv, sinv), 0.0).astype(bf16)
                dk_ref[rows, cols] = jnp.where(keep, _rot_t(dkr, cosv, sinv), 0.0).astype(bf16)
                dv_ref[rows, cols] = jnp.where(keep, dv, 0.0).astype(bf16)
        for hh in range(HEADS):
            g_ref[hh] = gstate[hh]

    part = lambda j: pl.BlockSpec((per * CHUNK, RW), lambda t: (nblk - 1 - t, j))
    table = pl.BlockSpec((per * CHUNK, HD), lambda t: (nblk - 1 - t, 0))
    return pl.pallas_call(
        _skip(len(deps), body), name=name, grid=(nblk,),
        in_specs=[_ANY] * len(deps) + [part(0), part(1), part(2), table, table, _resident((HEADS, 4, CHUNK, HD)),
                                       pl.BlockSpec((HEADS, per, HD, HD), lambda t: (0, nblk - 1 - t, 0, 0)), part(0)],
        out_specs=[part(0)] * 3,
        out_shape=[jax.ShapeDtypeStruct((tp, RW), bf16)] * 3,
        scratch_shapes=[pltpu.VMEM((HEADS, HD, HD), f32)],
        compiler_params=_cp(("arbitrary",)))(*deps, zq, zq, zq, cos2, sin2, dec, states, dout)


def _window_sum(xv, steps, tp, forward):
    s = xv
    for j in range(steps):
        sh = 2 ** j
        s = s + pltpu.roll(s, (tp - sh) if forward else sh, 0)
    return s


def pool_fwd(name, zu, maps, scale, pad):
    tp = zu.shape[0]

    def body(u_ref, maps_ref, scale_ref, pooled_ref, p_ref):
        row = lax.broadcasted_iota(jnp.int32, (tp, HD), 0)
        for gi, w in enumerate(POOL_WINDOWS):
            cols = slice(gi * HD, (gi + 1) * HD)
            xv = u_ref[:, cols]
            cnt = jnp.clip(row - (pad - 1), 1, w).astype(f32)
            pooled = jnp.where(row >= pad, _window_sum(xv, gi + 1, tp, False) / cnt - xv, 0.0).astype(bf16)
            pooled_ref[:, cols] = pooled
            p_ref[:, cols] = (_dot_nn(pooled, maps_ref[gi].astype(bf16)) * scale_ref[:, cols]).astype(bf16)

    return pl.pallas_call(
        body, name=name,
        out_shape=[jax.ShapeDtypeStruct((tp, RW), bf16), jax.ShapeDtypeStruct((tp, RW), bf16)],
        compiler_params=_cp())(zu, maps, scale)


def pool_bwd(name, dp, pooled, maps, scale, pad):
    tp = dp.shape[0]

    def body(dp_ref, pooled_ref, maps_ref, scale_ref, du_ref, dmaps_ref, dscale_ref):
        row = lax.broadcasted_iota(jnp.int32, (tp, HD), 0)
        dscale_ref[...] = jnp.zeros_like(dscale_ref)
        for gi, w in enumerate(POOL_WINDOWS):
            cols = slice(gi * HD, (gi + 1) * HD)
            mb = maps_ref[gi].astype(bf16)
            pooled = pooled_ref[:, cols]
            dpf = dp_ref[:, cols].astype(f32)
            dscale_ref[0:1, cols] = jnp.sum(dpf * _dot_nn(pooled, mb), axis=0, keepdims=True)
            dpm = (dpf * scale_ref[:, cols]).astype(bf16)
            dmaps_ref[gi * HD:(gi + 1) * HD, :] = _dot_tn(pooled, dpm)
            dpool = jnp.where(row >= pad, _dot_nt(dpm, mb), 0.0)
            cnt = jnp.clip(row - (pad - 1), 1, w).astype(f32)
            du = _window_sum(dpool / cnt, gi + 1, tp, True) - dpool
            du_ref[:, cols] = jnp.where(row >= pad, du, 0.0).astype(bf16)

    return pl.pallas_call(
        body, name=name,
        out_shape=[jax.ShapeDtypeStruct((tp, RW), bf16), jax.ShapeDtypeStruct((HEADS * HD, HD), f32),
                   jax.ShapeDtypeStruct((8, RW), f32)],
        compiler_params=_cp())(dp, pooled, maps, scale)


def _group_norm(o):
    mu = jnp.mean(o, axis=-1, keepdims=True)
    oc = o - mu
    rstd = lax.rsqrt(jnp.mean(oc * oc, axis=-1, keepdims=True) + EPS)
    return oc * rstd, rstd


def mix_out_fwd(name, h, oraw, zg, zgate, p, wretT, wpoolT, wout, deps=()):
    tp, d = h.shape
    tm = _row_tile(tp)

    def body(h_ref, o_ref, zg_ref, zgate_ref, p_ref, wr_ref, wp_ref, wo_ref, ho_ref, rp_ref, mixed_ref):
        parts = []
        for hh in range(HEADS):
            cols = slice(hh * HD, (hh + 1) * HD)
            rhat, _ = _group_norm(o_ref[:, cols].astype(f32))
            gv = zg_ref[:, cols].astype(f32)
            parts.append(rhat * (gv * _sigmoid(gv)))
        r = jnp.concatenate(parts, axis=-1).astype(bf16)
        pv = p_ref[...]
        ret = _dot_nt(r, wr_ref[...])
        pool = _dot_nt(pv, wp_ref[...])
        mixed = (_sigmoid(zgate_ref[:, :d].astype(f32)) * ret + _sigmoid(zgate_ref[:, d:].astype(f32)) * pool).astype(bf16)
        ho_ref[...] = h_ref[...] + _dot_nn(mixed, wo_ref[...])
        rp_ref[0] = r
        rp_ref[1] = pv
        mixed_ref[...] = mixed

    row = lambda w: pl.BlockSpec((tm, w), lambda i: (i, 0))
    return pl.pallas_call(
        _skip(len(deps), body), name=name, grid=(tp // tm,),
        in_specs=[_ANY] * len(deps) + [row(d), row(RW), row(RW), row(2 * d), row(RW), _resident((d, RW)), _resident((d, RW)),
                                       _resident((d, d))],
        out_specs=[row(d), pl.BlockSpec((2, tm, RW), lambda i: (0, i, 0)), pl.BlockSpec((None, tm, d), lambda i: (0, i, 0))],
        out_shape=[jax.ShapeDtypeStruct((tp, d), f32), jax.ShapeDtypeStruct((2, tp, RW), bf16),
                   jax.ShapeDtypeStruct((1, tp, d), bf16)],
        compiler_params=_cp(("arbitrary",)))(*deps, h, oraw, zg, zgate, p, wretT, wpoolT, wout)


def mix_out_bwd(name, dy, oraw, zg, zgate, rp, wretT, wpoolT, wout, deps=()):
    tp, d = dy.shape
    tm = _row_tile(tp)

    def body(dy_ref, o_ref, zg_ref, zgate_ref, rp_ref, wr_ref, wp_ref, wo_ref,
             do_ref, dzg_ref, dzgate_ref, dp_ref, drp_ref, dyb_ref):
        dyb = dy_ref[...].astype(bf16)
        dmixed = _dot_nt(dyb, wo_ref[...])
        sa = _sigmoid(zgate_ref[:, :d].astype(f32))
        sb = _sigmoid(zgate_ref[:, d:].astype(f32))
        dret = dmixed * sa
        dpool = dmixed * sb
        dzgate_ref[:, :d] = (dret * _dot_nt(rp_ref[0], wr_ref[...]) * (1.0 - sa)).astype(bf16)
        dzgate_ref[:, d:] = (dpool * _dot_nt(rp_ref[1], wp_ref[...]) * (1.0 - sb)).astype(bf16)
        dretb, dpoolb = dret.astype(bf16), dpool.astype(bf16)
        dr = _dot_nn(dretb, wr_ref[...])
        dp_ref[...] = _dot_nn(dpoolb, wp_ref[...]).astype(bf16)
        for hh in range(HEADS):
            cols = slice(hh * HD, (hh + 1) * HD)
            rhat, rstd = _group_norm(o_ref[:, cols].astype(f32))
            gv = zg_ref[:, cols].astype(f32)
            sg = _sigmoid(gv)
            drh = dr[:, cols]
            drhat = drh * (gv * sg)
            dzg_ref[:, cols] = (drh * rhat * (sg * (1.0 + gv * (1.0 - sg)))).astype(bf16)
            do = rstd * (drhat - jnp.mean(drhat, axis=-1, keepdims=True)
                         - rhat * jnp.mean(drhat * rhat, axis=-1, keepdims=True))
            do_ref[:, cols] = do.astype(bf16)
        drp_ref[0] = dretb
        drp_ref[1] = dpoolb
        dyb_ref[...] = dyb

    row = lambda w: pl.BlockSpec((tm, w), lambda i: (i, 0))
    return pl.pallas_call(
        _skip(len(deps), body), name=name, grid=(tp // tm,),
        in_specs=[_ANY] * len(deps) + [row(d), row(RW), row(RW), row(2 * d), pl.BlockSpec((2, tm, RW), lambda i: (0, i, 0)),
                                       _resident((d, RW)), _resident((d, RW)), _resident((d, d))],
        out_specs=[row(RW), row(RW), row(2 * d), row(RW), pl.BlockSpec((2, tm, d), lambda i: (0, i, 0)),
                   pl.BlockSpec((None, tm, d), lambda i: (0, i, 0))],
        out_shape=[jax.ShapeDtypeStruct((tp, RW), bf16), jax.ShapeDtypeStruct((tp, RW), bf16),
                   jax.ShapeDtypeStruct((tp, 2 * d), bf16), jax.ShapeDtypeStruct((tp, RW), bf16),
                   jax.ShapeDtypeStruct((2, tp, d), bf16), jax.ShapeDtypeStruct((1, tp, d), bf16)],
        compiler_params=_cp(("arbitrary",)))(*deps, dy, oraw, zg, zgate, rp, wretT, wpoolT, wout)


def final_loss(name, h, g, target):
    tp, d = h.shape
    tm = _row_tile(tp)
    nsub = tm // CHUNK

    def body(h_ref, g_ref, *rest):
        t_refs = rest[:nsub]
        dh_ref, loss_ref, dg_ref = rest[nsub:]
        i = pl.program_id(0)

        @pl.when(i == 0)
        def _():
            loss_ref[...] = jnp.zeros_like(loss_ref)
            dg_ref[...] = jnp.zeros_like(dg_ref)

        gv = g_ref[...]
        for j in range(nsub):
            rows = slice(j * CHUNK, (j + 1) * CHUNK)
            hhat, rs = _rms(h_ref[rows, :])
            err = jnp.where(i * nsub + j >= 1, hhat * gv - t_refs[j][...], 0.0)
            dyv = err / d
            dh_ref[rows, :] = _rms_bwd(dyv, gv, hhat, rs)
            loss_ref[...] += 0.5 * jnp.sum(jnp.sum(err * err, axis=-1, keepdims=True) / d)
            dg_ref[0:1, :] += jnp.sum(dyv * hhat, axis=0, keepdims=True)

    lagged = lambda j: pl.BlockSpec((CHUNK, d), lambda i: (jnp.maximum(i * nsub + j - 1, 0), 0))
    return pl.pallas_call(
        body, name=name, grid=(tp // tm,),
        in_specs=[pl.BlockSpec((tm, d), lambda i: (i, 0)), _resident((1, d))] + [lagged(j) for j in range(nsub)],
        out_specs=[pl.BlockSpec((tm, d), lambda i: (i, 0)), pl.BlockSpec((8, 128), lambda i: (0, 0)),
                   pl.BlockSpec((8, d), lambda i: (0, 0))],
        out_shape=[jax.ShapeDtypeStruct((tp, d), f32), jax.ShapeDtypeStruct((8, 128), f32),
                   jax.ShapeDtypeStruct((8, d), f32)],
        compiler_params=_cp(("arbitrary",)))(h, g, *[target] * nsub)


def _adamw(w, g, m, v):
    m = ADAM_B1 * m + (1.0 - ADAM_B1) * g
    v = ADAM_B2 * v + (1.0 - ADAM_B2) * (g * g)
    m_hat = m / (1.0 - ADAM_B1 ** ADAM_STEP)
    v_hat = v / (1.0 - ADAM_B2 ** ADAM_STEP)
    delta = -ADAM_LR * (m_hat / (jnp.sqrt(v_hat) + ADAM_EPS) + ADAM_WD * w)
    return delta, m, v


def adam_big(name, me, recv, own, b, layer, transposed, w, m, v, prev):
    r, c = recv.shape[2:]
    wshape = w.shape[1:]
    nchunk = 1 if transposed else next(k for k in (4, 2, 1) if r % (16 * k) == 0)
    rc = r // nchunk

    def body(me_ref, recv_ref, own_ref, w_ref, m_ref, v_ref, *rest):
        g_ref, d_ref, nm_ref, nv_ref = rest[-4:]
        g = own_ref[...].astype(f32)
        for j in range(NDEV - 1):
            g = g + recv_ref[j].astype(f32)
        if transposed:
            g = g.T
        delta, nm, nv = _adamw(w_ref[...], g, m_ref[...], v_ref[...])
        g_ref[...] = g
        d_ref[...] = delta
        nm_ref[...] = nm
        nv_ref[...] = nv

    wblock = wshape if transposed else (rc, c)
    wspec = pl.BlockSpec((None,) + wblock, lambda i, me_ref: (layer, i, 0))
    in_specs = [pl.BlockSpec((NDEV - 1, None, rc, c), lambda i, me_ref: (0, b, i, 0)),
                pl.BlockSpec((None, None, rc, c), lambda i, me_ref: (b, me_ref[0], i, 0)), wspec, wspec, wspec]
    args = [recv, own, w, m, v]
    aliases = {}
    if prev is not None:
        in_specs += [_ANY] * 4
        args += list(prev)
        aliases = {6 + k: k for k in range(4)}
    return pl.pallas_call(
        body, name=name,
        grid_spec=pltpu.PrefetchScalarGridSpec(num_scalar_prefetch=1, grid=(nchunk,), in_specs=in_specs,
                                               out_specs=[wspec] * 4),
        out_shape=[jax.ShapeDtypeStruct(w.shape, f32)] * 4, input_output_aliases=aliases,
        compiler_params=_cp())(me, *args)


def adam_small(name, ga0, gmaps0, gmeta, ga1, gmaps1, norms, pool_scale, pool_maps, meta, final_norm, d):
    def body(ga0_ref, gmaps0_ref, gmeta_ref, ga1_ref, gmaps1_ref, *refs):
        ins, outs = refs[:21], refs[21:]
        x, y, c = _me()
        me = 4 * x + 2 * y + c

        def total(ref, rows):
            t = ref[0, rows, :]
            for j in range(1, NDEV):
                t = t + ref[j, rows, :]
            return t

        row = lambda r: slice(r, r + 1)
        outs[0][...] = jnp.broadcast_to(total(ga1_ref, row(0))[:, :128], (8, 128))

        def update(k, g, o):
            w_ref, m_ref, v_ref = ins[3 * k:3 * k + 3]
            delta, nm, nv = _adamw(w_ref[...], g, m_ref[...], v_ref[...])
            for ref, val in zip(outs[o:o + 4], (g, delta, nm, nv)):
                ref[...] = val

        two = lax.broadcasted_iota(jnp.int32, (2, d), 0)
        for k in range(3):
            update(k, jnp.where(two == 0, total(ga0_ref, row(k)), total(ga1_ref, row(2 + k))), 1 + 4 * k)
        update(3, jnp.where(two[:, :RW] == 0, total(ga0_ref, row(3))[:, :RW], total(ga1_ref, row(5))[:, :RW]), 13)
        update(4, jnp.concatenate([total(gmaps0_ref, slice(None)), total(gmaps1_ref, slice(None))], axis=0), 17)
        update(5, total(gmeta_ref, pl.ds(pl.multiple_of(me * N_META, N_META), N_META)), 21)
        update(6, total(ga1_ref, row(1)), 25)

    flat = []
    for trip in (*norms, pool_scale, pool_maps, meta, final_norm):
        flat += list(trip)
    out_shapes = [jax.ShapeDtypeStruct((8, 128), f32)]
    for trip in (*norms, pool_scale, pool_maps, meta, final_norm):
        out_shapes += [jax.ShapeDtypeStruct(trip[0].shape, f32)] * 4
    return pl.pallas_call(body, name=name, out_shape=out_shapes,
                          compiler_params=_cp())(ga0, gmaps0, gmeta, ga1, gmaps1, *flat)


def kernel(x, meta, ffn1_norm, ffn1_gate, ffn1_up, ffn1_down, mix_norm, w_in, pool_maps, pool_scale, w_ret_up, w_pool_up, w_out, ffn2_norm, ffn2_gate, ffn2_up, ffn2_down, final_norm, loss_target, m_meta, m_ffn1_norm, m_ffn1_gate, m_ffn1_up, m_ffn1_down, m_mix_norm, m_w_in, m_pool_maps, m_pool_scale, m_w_ret_up, m_w_pool_up, m_w_out, m_ffn2_norm, m_ffn2_gate, m_ffn2_up, m_ffn2_down, m_final_norm, v_meta, v_ffn1_norm, v_ffn1_gate, v_ffn1_up, v_ffn1_down, v_mix_norm, v_w_in, v_pool_maps, v_pool_scale, v_w_ret_up, v_w_pool_up, v_w_out, v_ffn2_norm, v_ffn2_gate, v_ffn2_up, v_ffn2_down, v_final_norm):
    seq, d = x.shape[1], x.shape[2]
    depth = ffn1_gate.shape[0]
    ff = ffn1_gate.shape[2] * NDEV
    nin = w_in.shape[2] * NDEV
    length = seq + N_META
    pad = (-length) % CHUNK
    tp = length + pad
    assert pad % 8 == 0 and pad + N_META == CHUNK and depth == 2 and nin == 5 * RW + 2 * d

    ix, iy, ic = _me()
    me = (4 * ix + 2 * iy + ic).astype(jnp.int32).reshape(1)

    meta_all, = all_gather("gather_meta", [meta])
    meta_full = jnp.transpose(meta_all, (1, 0, 2)).reshape(N_META, d)

    token = meta_all
    tview = lambda *arrs: [jnp.swapaxes(a, 1, 2) for a in arrs]
    t_g1, t_u1, t_g2, t_u2, t_in = (tview(w, m, v) for w, m, v in (
        (ffn1_gate, m_ffn1_gate, v_ffn1_gate), (ffn1_up, m_ffn1_up, v_ffn1_up), (ffn2_gate, m_ffn2_gate, v_ffn2_gate),
        (ffn2_up, m_ffn2_up, v_ffn2_up), (w_in, m_w_in, v_w_in)))
    keys, groups = [], []
    for layer in range(depth):
        lands = prep_layer(layer, me, [w_ret_up, w_pool_up],
                           [t_g1[0], t_u1[0], t_g2[0], t_u2[0], t_in[0], ffn1_down, ffn2_down, w_out],
                           (token,) if layer else ())
        wretT, wpoolT, g1T, u1T, g2T, u2T, winT, d1, d2, wout = lands
        keys += [("ffn1", layer), ("mix", layer), ("ffn2", layer)]
        groups += [[g1T, u1T, d1], [winT, wretT, wpoolT, wout], [g2T, u2T, d2]]
        if layer == 0:
            first, token = gather_start_chips("gather_start_first", groups[:1], (token,))
    second, token = gather_start_chips("gather_start_second", groups[1:2], (token,))
    gathers = dict(zip(keys, first + second))

    def forward(part, layer, after):
        ssem, rsem, group = gathers[(part, layer)]
        ssem, rsem, group, tok = gather_forward(f"gather_forward_{part}{layer}", ssem, rsem, group, after)
        gathers[(part, layer)] = (ssem, rsem, group)
        return tok

    def gathered(part, layer, after):
        ssem, rsem, group = gathers[(part, layer)]
        _, full = copies_wait(f"gather_wait_{part}{layer}", ssem, rsem, (), group, after, 3)
        return [a.reshape((NDEV * a.shape[1],) + a.shape[2:]) for a in full]

    cos2, sin2, dec = _retention_tables(tp, pad, token[0, 0])
    h = jnp.concatenate([jnp.zeros((pad, d), f32), meta_full + (token[0, 0] + 0.0 * cos2[0, 0] + 0.0 * dec[0, 0, 0, 0]), x[0]],
                        axis=0)

    saved = []
    weights = []
    tok = forward("ffn1", 0, h)
    rest, tok = gather_start_chips("gather_start_rest", groups[2:], (tok,))
    gathers.update(zip(keys[2:], rest))
    for layer in range(depth):
        row = lambda a: a[layer:layer + 1]
        s = {"h0": h}
        g1T, u1T, d1 = gathered("ffn1", layer, tok if layer == 0 else h)
        tok = forward("mix", layer, h) if layer else None
        h, s["a1"], s["b1"] = ffn_fwd(f"ffn1_fwd{layer}", h, row(ffn1_norm), g1T, u1T, d1, (tok,) if layer else ())
        s["h1"] = h
        if layer == 0:
            tok = forward("mix", layer, h)
        winT, wretT, wpoolT, wout = gathered("mix", layer, tok if layer == 0 else h)
        s["zq"], s["zg"], zu, s["zgate"] = mix_in_fwd(f"mix_in_fwd{layer}", h, row(mix_norm), winT, cos2, sin2)
        s["oraw"], s["states"] = retention_fwd(f"retention_fwd{layer}", s["zq"], dec)
        s["pooled"], p = pool_fwd(f"pool_fwd{layer}", zu, pool_maps[layer], row(pool_scale), pad)
        tok = forward("ffn2", layer, p)
        h, s["rp"], s["mixed"] = mix_out_fwd(
            f"mix_out_fwd{layer}", h, s["oraw"], s["zg"], s["zgate"], p, wretT, wpoolT, wout, (tok,))
        s["h2"] = h
        g2T, u2T, d2 = gathered("ffn2", layer, h)
        tok = (forward("ffn1", layer + 1, h),) if layer + 1 < depth else ()
        h, s["a2"], s["b2"] = ffn_fwd(f"ffn2_fwd{layer}", h, row(ffn2_norm), g2T, u2T, d2, tok)
        saved.append(s)
        weights.append((g1T, u1T, g2T, u2T, winT, wretT, wpoolT, d1, d2, wout))

    dh, loss_part, dg_final = final_loss("final_loss", h, final_norm.reshape(1, d), loss_target[0])

    small = {}
    small_gathers = {}
    exchanges = {}
    token = ()

    def rows8(vals):
        at = lax.broadcasted_iota(jnp.int32, (8, d), 0)
        out = jnp.zeros((8, d), f32)
        for k, v in enumerate(vals):
            r0 = v[0:1]
            r0 = r0 if r0.shape[1] == d else jnp.pad(r0, ((0, 0), (0, d - r0.shape[1])))
            out = jnp.where(at == k, r0, out)
        return out

    def exchange(part, layer, grads):
        by_dest = [g.reshape(g.shape[0], NDEV, g.shape[1] // NDEV, g.shape[2]) for g in grads]
        ssem, rsem, sent, lands, tok = exchange_start(f"exchange_start_{part}{layer}", by_dest)
        exchanges[(part, layer)] = (ssem, rsem, sent, lands)
        return (tok,)

    for layer in reversed(range(depth)):
        g1T, u1T, g2T, u2T, winT, wretT, wpoolT, d1, d2, wout = weights[layer]
        row = lambda a: a[layer:layer + 1]
        s = saved[layer]
        dh, lhs2, rhs2, small[("ffn2", layer)] = ffn_bwd(
            f"ffn2_bwd{layer}", dh, s["h2"], row(ffn2_norm), s["a2"], s["b2"], g2T, u2T, d2, token)
        gw_ffn2 = mm_tn(f"ffn2_wgrad{layer}", lhs2, rhs2, lambda b: b // 2)
        if layer == 0:
            token = exchange("ffn2", layer, [gw_ffn2])
        do, dzg, dzgate, dp, drp, dyb = mix_out_bwd(
            f"mix_out_bwd{layer}", dh, s["oraw"], s["zg"], s["zgate"], s["rp"], wretT, wpoolT, wout, token)
        gw_mix = [mm_tn(f"w_out_wgrad{layer}", s["mixed"], dyb, lambda b: b),
                  mm_tn(f"up_wgrad{layer}", drp, s["rp"], lambda b: b)]
        dq, dk, dv = retention_bwd(f"retention_bwd{layer}", s["zq"], cos2, sin2, dec, s["states"], do, pad, token)
        dzu, small[("maps", layer)], small[("scale", layer)] = pool_bwd(
            f"pool_bwd{layer}", dp, s["pooled"], pool_maps[layer], row(pool_scale), pad)
        dh, dz, n2, small[("mix", layer)] = mix_in_bwd(
            f"mix_in_bwd{layer}", dq, dk, dv, dzg, dzu, dzgate, s["h1"], row(mix_norm), winT, dh)
        gw_mix.append(mm_tn(f"w_in_wgrad{layer}", dz, n2, lambda b: b))
        if layer == 0:
            token = exchange("mix", layer, gw_mix)
        dh, lhs1, rhs1, small[("ffn1", layer)] = ffn_bwd(
            f"ffn1_bwd{layer}", dh, s["h0"], row(ffn1_norm), s["a1"], s["b1"], g1T, u1T, d1, token)
        rows = [small[("ffn1", layer)], small[("mix", layer)], small[("ffn2", layer)], small[("scale", layer)]]
        packs = [rows8([loss_part, dg_final] + rows if layer == depth - 1 else rows), small[("maps", layer)]]
        if layer == 0:
            dmeta = dh[pad:CHUNK]
            packs.append(jnp.transpose(dmeta.reshape(N_META, NDEV, d // NDEV), (1, 0, 2)).reshape(NDEV * N_META, d // NDEV))
        ssem, rsem, lands, tok = gather_start(f"small_start{layer}", slot_in(f"small_slot{layer}", me, packs))
        small_gathers[layer] = (ssem, rsem, lands)
        if layer:
            gw_ffn1 = mm_tn(f"ffn1_wgrad{layer}", lhs1, rhs1, lambda b: b // 2, (tok,))
            token = exchange("all", layer, [gw_ffn2] + gw_mix + [gw_ffn1])
        else:
            token = (tok,)
            for j, nm in enumerate(("ffn1_gate", "ffn1_up", "ffn1_down")):
                token = exchange(nm, layer, [mm_tn(f"{nm}_wgrad{layer}", lhs1, rhs1, lambda b: b // 2, token, only=j)])

    grad_x = (dh[CHUNK:] + token[0][0, 0])[None]

    big = {}
    after = token[0]
    plans = {
        "ffn2": [("ffn2_gate", 0, 0, False, *t_g2), ("ffn2_up", 0, 1, False, *t_u2),
                 ("ffn2_down", 0, 2, False, ffn2_down, m_ffn2_down, v_ffn2_down)],
        "mix": [("w_out", 0, 0, False, w_out, m_w_out, v_w_out),
                ("w_ret_up", 1, 0, True, w_ret_up, m_w_ret_up, v_w_ret_up),
                ("w_pool_up", 1, 1, True, w_pool_up, m_w_pool_up, v_w_pool_up), ("w_in", 2, 0, False, *t_in)],
        "ffn1": [("ffn1_gate", 0, 0, False, *t_g1), ("ffn1_up", 0, 1, False, *t_u1),
                 ("ffn1_down", 0, 2, False, ffn1_down, m_ffn1_down, v_ffn1_down)]}
    for nm, k, b, tr, w, m, v in plans["ffn1"]:
        plans[nm] = [(nm, 0, 0, tr, w, m, v)]
    plans["all"] = (plans["ffn2"] + [(nm, k + 1, b, tr, w, m, v) for nm, k, b, tr, w, m, v in plans["mix"]]
                    + [(nm, 4, b, tr, w, m, v) for nm, k, b, tr, w, m, v in plans["ffn1"]])
    for layer in reversed(range(depth)):
        for part in ("all",) if layer else ("ffn2", "mix", "ffn1_gate", "ffn1_up", "ffn1_down"):
            ssem, rsem, sent, lands = exchanges[(part, layer)]
            sent, lands = copies_wait(f"exchange_wait_{part}{layer}", ssem, rsem, sent, lands, after)
            for nm, k, b, tr, w, m, v in plans[part]:
                big[nm] = adam_big(f"adam_{nm}{layer}", me, lands[k], sent[k], b, layer, tr, w, m, v, big.get(nm))
                after = big[nm][0]

    gsmall = []
    for layer in range(depth):
        ssem, rsem, lands = small_gathers[layer]
        gsmall += copies_wait(f"small_wait{layer}", ssem, rsem, (), lands, after)[1]

    maps2 = lambda a: a.reshape(depth * HEADS * HD, HD)
    res = adam_small(
        "adam_small", *gsmall,
        [(ffn1_norm, m_ffn1_norm, v_ffn1_norm), (mix_norm, m_mix_norm, v_mix_norm), (ffn2_norm, m_ffn2_norm, v_ffn2_norm)],
        (pool_scale, m_pool_scale, v_pool_scale), (maps2(pool_maps), maps2(m_pool_maps), maps2(v_pool_maps)),
        (meta, m_meta, v_meta), tuple(a.reshape(1, d) for a in (final_norm, m_final_norm, v_final_norm)), d)
    loss = res[0][0, 0]
    sm = {}
    for k, nm in enumerate(["ffn1_norm", "mix_norm", "ffn2_norm", "pool_scale", "pool_maps", "meta", "final_norm"]):
        sm[nm] = list(res[1 + 4 * k:5 + 4 * k])
    sm["pool_maps"] = [a.reshape(pool_maps.shape) for a in sm["pool_maps"]]
    sm["final_norm"] = [a.reshape(d) for a in sm["final_norm"]]

    names = ["meta", "ffn1_norm", "ffn1_gate", "ffn1_up", "ffn1_down", "mix_norm", "w_in", "pool_maps", "pool_scale",
             "w_ret_up", "w_pool_up", "w_out", "ffn2_norm", "ffn2_gate", "ffn2_up", "ffn2_down", "final_norm"]
    for nm in ("ffn1_gate", "ffn1_up", "ffn2_gate", "ffn2_up", "w_in"):
        big[nm] = tview(*big[nm])
    allw = {**{k: list(v) for k, v in big.items()}, **sm}
    outs = [loss, grad_x]
    for kind in range(4):
        outs += [allw[nm][kind] for nm in names]
    return tuple(outs)
```

```python
import functools

import jax
import jax.numpy as jnp
from jax import lax
from jax.experimental import pallas as pl
from jax.experimental.pallas import tpu as pltpu

f32 = jnp.float32
bf16 = jnp.bfloat16
MESH = pl.DeviceIdType.MESH
NDEV = 8
N_META = 16
HEADS = 4
HD = 128
CHUNK = 128
RW = HEADS * HD
POOL_WINDOWS = (2, 4, 8, 16)
ROPE_BASE = 10000.0
EPS = 1e-6
ADAM_LR = 0.001
ADAM_B1 = 0.9
ADAM_B2 = 0.999
ADAM_EPS = 1e-08
ADAM_WD = 0.01
ADAM_STEP = 10
VMEM_CAP_MB = 60


def _cp(sem=None):
    return pltpu.CompilerParams(vmem_limit_bytes=VMEM_CAP_MB * 2**20, dimension_semantics=sem)


def _row_tile(tp, want=384):
    return want if tp % want == 0 else 128


def _resident(shape):
    nd = len(shape)
    return pl.BlockSpec(shape, lambda *_: (0,) * nd, pipeline_mode=pl.Buffered(1))


def _skip(nd, body):
    return (lambda *refs: body(*refs[nd:])) if nd else body


def _dot_nn(a, b):
    return lax.dot_general(a, b, (((1,), (0,)), ((), ())), preferred_element_type=f32)


def _dot_nt(a, b):
    return lax.dot_general(a, b, (((1,), (1,)), ((), ())), preferred_element_type=f32)


def _dot_tn(a, b):
    return lax.dot_general(a, b, (((0,), (0,)), ((), ())), preferred_element_type=f32)


def _rms(h):
    rs = lax.rsqrt(jnp.mean(h * h, axis=-1, keepdims=True) + EPS)
    return h * rs, rs


def _rms_bwd(dn, g, hhat, rs):
    dhh = dn * g
    return rs * (dhh - hhat * jnp.mean(dhh * hhat, axis=-1, keepdims=True))


def _sigmoid(x):
    return jax.nn.sigmoid(x)


def _me():
    return lax.axis_index("x"), lax.axis_index("y"), lax.axis_index("c")


def _peer(idx):
    return (idx // 4, (idx // 2) % 2, idx % 2)


def all_gather(name, arrays):
    n = len(arrays)

    def body(*refs):
        ins, outs = refs[:n], refs[n:2 * n]
        send_sems, recv_sems, local_sems = refs[2 * n:]
        x, y, c = _me()
        me = 4 * x + 2 * y + c
        locals_ = []
        for k in range(n):
            cp = pltpu.make_async_copy(ins[k], outs[k].at[me], local_sems.at[k])
            cp.start()
            locals_.append(cp)
        for d in range(1, NDEV):
            for k in range(n):
                pltpu.make_async_remote_copy(
                    src_ref=ins[k], dst_ref=outs[k].at[me], send_sem=send_sems.at[k], recv_sem=recv_sems.at[k],
                    device_id=_peer((me + d) % NDEV), device_id_type=MESH).start()
        for k in range(n):
            seven = outs[k].at[pl.ds(0, NDEV - 1)]
            w = pltpu.make_async_remote_copy(src_ref=seven, dst_ref=seven, send_sem=send_sems.at[k],
                                             recv_sem=recv_sems.at[k], device_id=(x, y, c), device_id_type=MESH)
            w.wait_send()
            w.wait_recv()
            locals_[k].wait()

    anyspec = pl.BlockSpec(memory_space=pl.ANY)
    return pl.pallas_call(
        body, name=name,
        out_shape=[jax.ShapeDtypeStruct((NDEV,) + a.shape, a.dtype) for a in arrays],
        in_specs=[anyspec] * n, out_specs=[anyspec] * n,
        scratch_shapes=[pltpu.SemaphoreType.DMA((n,)), pltpu.SemaphoreType.DMA((n,)), pltpu.SemaphoreType.DMA((n,))],
    )(*arrays)


_HBM = pl.BlockSpec(memory_space=pltpu.HBM)
_SEM = pl.BlockSpec(memory_space=pltpu.SEMAPHORE)
_ANY = pl.BlockSpec(memory_space=pl.ANY)
_EFFECT = pltpu.SideEffectType.DATAFLOW_SIDE_EFFECTING


def _in_hbm(a):
    return pltpu.with_memory_space_constraint(a, pltpu.HBM)


def gather_start(name, lands, deps=()):
    n, nd = len(lands), len(deps)

    def body(*refs):
        land = refs[nd:nd + n]
        send_sems, recv_sems = refs[nd + n:nd + n + 2]
        token = refs[-1]
        x, y, c = _me()
        me = 4 * x + 2 * y + c
        for d in range(1, NDEV):
            for k in range(n):
                pltpu.make_async_remote_copy(
                    src_ref=land[k].at[me], dst_ref=land[k].at[me], send_sem=send_sems.at[k], recv_sem=recv_sems.at[k],
                    device_id=_peer((me + d) % NDEV), device_id_type=MESH).start()
        token[...] = jnp.zeros_like(token)

    res = pl.pallas_call(
        body, name=name,
        out_shape=(pltpu.SemaphoreType.DMA((n,)), pltpu.SemaphoreType.DMA((n,)),
                   *[pltpu.HBM(a.shape, a.dtype) for a in lands], jax.ShapeDtypeStruct((8, 128), f32)),
        in_specs=[_ANY] * nd + [_HBM] * n,
        out_specs=(_SEM, _SEM, *[_HBM] * n, pl.BlockSpec(memory_space=pltpu.VMEM)),
        input_output_aliases={nd + k: 2 + k for k in range(n)},
        compiler_params=pltpu.CompilerParams(has_side_effects=_EFFECT),
    )(*deps, *[_in_hbm(a) for a in lands])
    return res[0], res[1], list(res[2:2 + n]), res[-1]


def _other_chips(x, y):
    return [(1 - x, y), (x, 1 - y), (1 - x, 1 - y)]


def gather_start_chips(name, groups, deps=()):
    sizes = [len(g) for g in groups]
    lands = [a for g in groups for a in g]
    n, nd, ng = len(lands), len(deps), len(groups)

    def body(*refs):
        land = refs[nd:nd + n]
        sems = refs[nd + n:nd + n + 2 * ng]
        token = refs[-1]
        x, y, c = _me()
        me = 4 * x + 2 * y + c
        k = 0
        for g, size in enumerate(sizes):
            for j in range(size):
                for to in [(x, y, 1 - c)] + [(cx, cy, c) for cx, cy in _other_chips(x, y)]:
                    pltpu.make_async_remote_copy(
                        src_ref=land[k].at[me], dst_ref=land[k].at[me], send_sem=sems[2 * g].at[j],
                        recv_sem=sems[2 * g + 1].at[j], device_id=to, device_id_type=MESH).start()
                k += 1
        token[...] = jnp.zeros_like(token)

    res = pl.pallas_call(
        body, name=name,
        out_shape=(*[pltpu.SemaphoreType.DMA((size,)) for size in sizes for _ in range(2)],
                   *[pltpu.HBM(a.shape, a.dtype) for a in lands], jax.ShapeDtypeStruct((8, 128), f32)),
        in_specs=[_ANY] * nd + [_HBM] * n,
        out_specs=(*[_SEM] * (2 * ng), *[_HBM] * n, pl.BlockSpec(memory_space=pltpu.VMEM)),
        input_output_aliases={nd + k: 2 * ng + k for k in range(n)},
        compiler_params=pltpu.CompilerParams(has_side_effects=_EFFECT),
    )(*deps, *[_in_hbm(a) for a in lands])
    out, k = [], 2 * ng
    for g, size in enumerate(sizes):
        out.append((res[2 * g], res[2 * g + 1], list(res[k:k + size])))
        k += size
    return out, res[-1]


def gather_forward(name, send_sems, recv_sems, lands, after):
    n = len(lands)

    def body(*refs):
        land = refs[:n]
        ssem, rsem = refs[n:n + 2]
        send2, recv2 = refs[n + 3:n + 5]
        token = refs[-1]
        x, y, c = _me()
        for k in range(n):
            four = land[k].at[pl.ds(0, 4)]
            w = pltpu.make_async_remote_copy(src_ref=four, dst_ref=four, send_sem=ssem.at[k], recv_sem=rsem.at[k],
                                             device_id=(x, y, c), device_id_type=MESH)
            w.wait_send()
            w.wait_recv()
            for cx, cy in _other_chips(x, y):
                slot = 4 * cx + 2 * cy + c
                pltpu.make_async_remote_copy(
                    src_ref=land[k].at[slot], dst_ref=land[k].at[slot], send_sem=send2.at[k], recv_sem=recv2.at[k],
                    device_id=(x, y, 1 - c), device_id_type=MESH).start()
        token[...] = jnp.zeros_like(token)

    res = pl.pallas_call(
        body, name=name,
        out_shape=(pltpu.SemaphoreType.DMA((n,)), pltpu.SemaphoreType.DMA((n,)),
                   *[pltpu.HBM(a.shape, a.dtype) for a in lands], jax.ShapeDtypeStruct((8, 128), f32)),
        in_specs=[_HBM] * n + [_SEM, _SEM, _ANY],
        out_specs=(_SEM, _SEM, *[_HBM] * n, pl.BlockSpec(memory_space=pltpu.VMEM)),
        input_output_aliases={k: 2 + k for k in range(n)},
        compiler_params=pltpu.CompilerParams(has_side_effects=_EFFECT),
    )(*lands, send_sems, recv_sems, after)
    return res[0], res[1], list(res[2:2 + n]), res[-1]


def exchange_start(name, grads, deps=()):
    n, nd = len(grads), len(deps)
    lands = [lax.empty((NDEV - 1, g.shape[0]) + g.shape[2:], g.dtype) for g in grads]

    def body(*refs):
        src = refs[nd:nd + n]
        land = refs[nd + n:nd + 2 * n]
        send_sems, recv_sems = refs[nd + 2 * n:nd + 2 * n + 2]
        token = refs[-1]
        x, y, c = _me()
        me = 4 * x + 2 * y + c
        for d in range(1, NDEV):
            p = (me + d) % NDEV
            for k in range(n):
                pltpu.make_async_remote_copy(
                    src_ref=src[k].at[:, p], dst_ref=land[k].at[d - 1], send_sem=send_sems.at[k], recv_sem=recv_sems.at[k],
                    device_id=_peer(p), device_id_type=MESH).start()
        token[...] = jnp.zeros_like(token)

    both = list(grads) + lands
    res = pl.pallas_call(
        body, name=name,
        out_shape=(pltpu.SemaphoreType.DMA((n,)), pltpu.SemaphoreType.DMA((n,)),
                   *[pltpu.HBM(a.shape, a.dtype) for a in both], jax.ShapeDtypeStruct((8, 128), f32)),
        in_specs=[_ANY] * nd + [_HBM] * (2 * n),
        out_specs=(_SEM, _SEM, *[_HBM] * (2 * n), pl.BlockSpec(memory_space=pltpu.VMEM)),
        input_output_aliases={nd + k: 2 + k for k in range(2 * n)},
        compiler_params=pltpu.CompilerParams(has_side_effects=_EFFECT),
    )(*deps, *[_in_hbm(a) for a in both])
    return res[0], res[1], list(res[2:2 + n]), list(res[2 + n:2 + 2 * n]), res[-1]


def copies_wait(name, send_sems, recv_sems, sent, lands, after, count=NDEV - 1):
    ns, n = len(sent), len(lands)

    def body(*refs):
        land = refs[ns:ns + n]
        ssem, rsem = refs[ns + n:ns + n + 2]
        x, y, c = _me()
        for k in range(n):
            seven = land[k].at[pl.ds(0, count)]
            w = pltpu.make_async_remote_copy(src_ref=seven, dst_ref=seven, send_sem=ssem.at[k], recv_sem=rsem.at[k],
                                             device_id=(x, y, c), device_id_type=MESH)
            w.wait_send()
            w.wait_recv()

    both = list(sent) + list(lands)
    res = pl.pallas_call(
        body, name=name, out_shape=tuple(pltpu.HBM(a.shape, a.dtype) for a in both),
        in_specs=[_HBM] * (ns + n) + [_SEM, _SEM, _ANY], out_specs=tuple([_HBM] * (ns + n)),
        input_output_aliases={k: k for k in range(ns + n)},
        compiler_params=pltpu.CompilerParams(has_side_effects=_EFFECT),
    )(*both, send_sems, recv_sems, after)
    return list(res[:ns]), list(res[ns:])


def prep_layer(layer, me, col_sharded, row_sharded, deps=()):
    nc, nr = len(col_sharded), len(row_sharded)

    def body(me_ref, *refs):
        ins, outs = refs[:nc + nr], refs[nc + nr + len(deps):]
        for k in range(nc):
            outs[k][...] = ins[k][...].T.astype(bf16)
        for k in range(nc, nc + nr):
            outs[k][...] = ins[k][...].astype(bf16)

    arrs = list(col_sharded) + list(row_sharded)
    in_specs = [pl.BlockSpec((None,) + a.shape[1:], lambda i, me_ref: (layer, 0, 0)) for a in arrs]
    shapes = [(a.shape[2], a.shape[1]) for a in col_sharded] + [a.shape[1:] for a in row_sharded]
    out_specs = [pl.BlockSpec((None,) + s, lambda i, me_ref: (me_ref[0], 0, 0)) for s in shapes]
    return pl.pallas_call(
        body, name=f"prep_layer{layer}",
        grid_spec=pltpu.PrefetchScalarGridSpec(num_scalar_prefetch=1, grid=(1,), in_specs=in_specs + [_ANY] * len(deps),
                                               out_specs=out_specs),
        out_shape=[jax.ShapeDtypeStruct((NDEV,) + s, bf16) for s in shapes], compiler_params=_cp())(me, *arrs, *deps)


def slot_in(name, me, arrays):
    n = len(arrays)

    def body(me_ref, *refs):
        for k in range(n):
            refs[n + k][...] = refs[k][...]

    in_specs = [pl.BlockSpec(a.shape, lambda i, me_ref: (0, 0)) for a in arrays]
    out_specs = [pl.BlockSpec((None,) + a.shape, lambda i, me_ref: (me_ref[0], 0, 0)) for a in arrays]
    return pl.pallas_call(
        body, name=name,
        grid_spec=pltpu.PrefetchScalarGridSpec(num_scalar_prefetch=1, grid=(1,), in_specs=in_specs, out_specs=out_specs),
        out_shape=[jax.ShapeDtypeStruct((NDEV,) + a.shape, a.dtype) for a in arrays])(me, *arrays)


def _ff_chunks(ff, want=768):
    if ff % 256:
        return [slice(0, ff)]
    return [slice(c, min(c + want, ff)) for c in range(0, ff, want)]


def ffn_fwd(name, h, g, wgT, wuT, wd, deps=()):
    tp, d = h.shape
    ff = wgT.shape[0]
    tm = _row_tile(tp, 704)

    def body(h_ref, g_ref, wg_ref, wu_ref, wd_ref, ho_ref, a_ref, b_ref):
        hh = h_ref[...]
        hhat, _ = _rms(hh)
        n = (hhat * g_ref[...]).astype(bf16)
        acc = None
        for cols in _ff_chunks(ff):
            a = _dot_nt(n, wg_ref[cols, :])
            b = _dot_nt(n, wu_ref[cols, :])
            part = _dot_nn(((a * _sigmoid(a)) * b).astype(bf16), wd_ref[cols, :])
            acc = part if acc is None else acc + part
            a_ref[:, cols] = a.astype(bf16)
            b_ref[:, cols] = b.astype(bf16)
        ho_ref[...] = hh + 0.5 * acc

    row = lambda w: pl.BlockSpec((tm, w), lambda i: (i, 0))
    return pl.pallas_call(
        _skip(len(deps), body), name=name, grid=(tp // tm,),
        in_specs=[_ANY] * len(deps) + [row(d), _resident((1, d)), _resident((ff, d)), _resident((ff, d)), _resident((ff, d))],
        out_specs=[row(d), row(ff), row(ff)],
        out_shape=[jax.ShapeDtypeStruct((tp, d), f32), jax.ShapeDtypeStruct((tp, ff), bf16),
                   jax.ShapeDtypeStruct((tp, ff), bf16)],
        compiler_params=_cp(("arbitrary",)))(*deps, h, g, wgT, wuT, wd)


def ffn_bwd(name, dy, h, g, a, b, wgT, wuT, wd, deps=()):
    tp, d = h.shape
    ff = wgT.shape[0]
    tm = _row_tile(tp, 384)

    def body(dy_ref, h_ref, g_ref, a_ref, b_ref, wg_ref, wu_ref, wd_ref, dh_ref, lhs_ref, rhs_ref, dg_ref):
        dyv = dy_ref[...]
        hhat, rs = _rms(h_ref[...])
        gv = g_ref[...]
        n = hhat * gv
        dyh = (0.5 * dyv).astype(bf16)
        dn = None
        for cols in _ff_chunks(ff):
            ds = _dot_nt(dyh, wd_ref[cols, :])
            av = a_ref[:, cols].astype(f32)
            bv = b_ref[:, cols].astype(f32)
            sg = _sigmoid(av)
            sa = av * sg
            da = (ds * bv * (sg * (1.0 + av * (1.0 - sg)))).astype(bf16)
            db = (ds * sa).astype(bf16)
            part = _dot_nn(da, wg_ref[cols, :]) + _dot_nn(db, wu_ref[cols, :])
            dn = part if dn is None else dn + part
            lhs_ref[0, :, cols] = da
            lhs_ref[1, :, cols] = db
            lhs_ref[2, :, cols] = (sa * bv).astype(bf16)
        dh_ref[...] = dyv + _rms_bwd(dn, gv, hhat, rs)

        @pl.when(pl.program_id(0) == 0)
        def _():
            dg_ref[...] = jnp.zeros_like(dg_ref)

        dg_ref[0:1, :] += jnp.sum(dn * hhat, axis=0, keepdims=True)
        rhs_ref[0] = n.astype(bf16)
        rhs_ref[1] = dyh

    row = lambda w: pl.BlockSpec((tm, w), lambda i: (i, 0))
    return pl.pallas_call(
        _skip(len(deps), body), name=name, grid=(tp // tm,),
        in_specs=[_ANY] * len(deps) + [row(d), row(d), _resident((1, d)), row(ff), row(ff),
                  _resident((ff, d)), _resident((ff, d)), _resident((ff, d))],
        out_specs=[row(d), pl.BlockSpec((3, tm, ff), lambda i: (0, i, 0)), pl.BlockSpec((2, tm, d), lambda i: (0, i, 0)),
                   pl.BlockSpec((8, d), lambda i: (0, 0))],
        out_shape=[jax.ShapeDtypeStruct((tp, d), f32), jax.ShapeDtypeStruct((3, tp, ff), bf16),
                   jax.ShapeDtypeStruct((2, tp, d), bf16), jax.ShapeDtypeStruct((8, d), f32)],
        compiler_params=_cp(("arbitrary",)))(*deps, dy, h, g, a, b, wgT, wuT, wd)


def mm_tn(name, lhs, rhs, rhs_of, deps=(), only=None):
    _, tp, m = lhs.shape
    b0, nb = (0, lhs.shape[0]) if only is None else (only, 1)
    n = rhs.shape[2]
    def fits(t, ms):
        mb = m // ms
        return (tp % t == 0 and m % (128 * ms) == 0
                and 2 * t * (mb + n) * 2 + mb * n * (2 * 2 + 4 + (4 if t < tp else 0)) <= 54 * 2**20)

    tk, msplit = next(((t, ms) for t in (tp, 1408, 704, 384) for ms in (1, 2, 4) if fits(t, ms)), (128, 1))
    nk = tp // tk
    mb = m // msplit

    def body(l_ref, r_ref, o_ref, acc_ref):
        if nk == 1:
            o_ref[...] = _dot_tn(l_ref[...], r_ref[...]).astype(o_ref.dtype)
            return
        k = pl.program_id(2)

        @pl.when(k == 0)
        def _():
            acc_ref[...] = jnp.zeros_like(acc_ref)

        acc_ref[...] += _dot_tn(l_ref[...], r_ref[...])

        @pl.when(k == nk - 1)
        def _():
            o_ref[...] = acc_ref[...].astype(o_ref.dtype)

    return pl.pallas_call(
        _skip(len(deps), body), name=name, grid=(nb, msplit, nk),
        in_specs=[_ANY] * len(deps) + [pl.BlockSpec((None, tk, mb), lambda b, j, k: (b0 + b, k, j)),
                                       pl.BlockSpec((None, tk, n), lambda b, j, k: (rhs_of(b0 + b), k, 0))],
        out_specs=pl.BlockSpec((None, mb, n), lambda b, j, k: (b, j, 0)),
        out_shape=jax.ShapeDtypeStruct((nb, m, n), bf16),
        scratch_shapes=[pltpu.VMEM((mb, n) if nk > 1 else (8, 128), f32)],
        compiler_params=_cp(("arbitrary", "arbitrary", "arbitrary")))(*deps, lhs, rhs)


def mix_in_fwd(name, h, g, winT, cos2, sin2):
    tp, d = h.shape
    nin = winT.shape[0]
    tm = _row_tile(tp)

    def body(h_ref, g_ref, w_ref, cos_ref, sin_ref, zq_ref, zg_ref, zu_ref, zgate_ref):
        hhat, _ = _rms(h_ref[...])
        z = _dot_nt((hhat * g_ref[...]).astype(bf16), w_ref[...])
        cosv, sinv = cos_ref[...], sin_ref[...]
        for hh in range(HEADS):
            qcols, kcols = slice(hh * HD, (hh + 1) * HD), slice(RW + hh * HD, RW + (hh + 1) * HD)
            zq_ref[:, qcols] = (_rot(z[:, qcols], cosv, sinv) * HD ** -0.5).astype(bf16)
            zq_ref[:, kcols] = _rot(z[:, kcols], cosv, sinv).astype(bf16)
        zq_ref[:, 2 * RW:] = z[:, 2 * RW:3 * RW].astype(bf16)
        zg_ref[...] = z[:, 3 * RW:4 * RW].astype(zg_ref.dtype)
        zu_ref[...] = z[:, 4 * RW:5 * RW]
        zgate_ref[...] = z[:, 5 * RW:].astype(zgate_ref.dtype)

    row = lambda w: pl.BlockSpec((tm, w), lambda i: (i, 0))
    widths = (3 * RW, RW, RW, 2 * d)
    return pl.pallas_call(
        body, name=name, grid=(tp // tm,),
        in_specs=[row(d), _resident((1, d)), _resident((nin, d)), row(HD), row(HD)],
        out_specs=[row(w) for w in widths],
        out_shape=[jax.ShapeDtypeStruct((tp, w), dt) for w, dt in zip(widths, (bf16, bf16, f32, bf16))],
        compiler_params=_cp(("arbitrary",)))(h, g, winT, cos2, sin2)


def mix_in_bwd(name, dq, dk, dv, dzg, dzu, dzgate, h, g, winT, dres):
    tp, d = h.shape
    nin = winT.shape[0]
    tm = _row_tile(tp)

    def body(dq_ref, dk_ref, dv_ref, dzg_ref, dzu_ref, dzgate_ref, h_ref, g_ref, w_ref, dres_ref, dh_ref, dz_ref, n_ref, dg_ref):
        dn, col = None, 0
        for piece in (dq_ref, dk_ref, dv_ref, dzg_ref, dzu_ref, dzgate_ref):
            v = piece[...]
            part = _dot_nn(v, w_ref[col:col + v.shape[1], :])
            dn = part if dn is None else dn + part
            dz_ref[:, col:col + v.shape[1]] = v
            col += v.shape[1]
        hhat, rs = _rms(h_ref[...])
        gv = g_ref[...]
        dh_ref[...] = dres_ref[...] + _rms_bwd(dn, gv, hhat, rs)

        @pl.when(pl.program_id(0) == 0)
        def _():
            dg_ref[...] = jnp.zeros_like(dg_ref)

        dg_ref[0:1, :] += jnp.sum(dn * hhat, axis=0, keepdims=True)
        n_ref[...] = (hhat * gv).astype(bf16)

    row = lambda w: pl.BlockSpec((tm, w), lambda i: (i, 0))
    return pl.pallas_call(
        body, name=name, grid=(tp // tm,),
        in_specs=[row(RW)] * 5 + [row(2 * d), row(d), _resident((1, d)), _resident((nin, d)), row(d)],
        out_specs=[row(d), pl.BlockSpec((None, tm, nin), lambda i: (0, i, 0)), pl.BlockSpec((None, tm, d), lambda i: (0, i, 0)),
                   pl.BlockSpec((8, d), lambda i: (0, 0))],
        out_shape=[jax.ShapeDtypeStruct((tp, d), f32), jax.ShapeDtypeStruct((1, tp, nin), bf16),
                   jax.ShapeDtypeStruct((1, tp, d), bf16), jax.ShapeDtypeStruct((8, d), f32)],
        compiler_params=_cp(("arbitrary",)))(dq, dk, dv, dzg, dzu, dzgate, h, g, winT, dres)


def _retention_tables(tp, pad, zero):
    half = HD // 2
    lane = jnp.arange(HD)
    inv_freq = ROPE_BASE ** (-(lane % half).astype(f32) / half)
    pos = jnp.arange(tp, dtype=f32) - pad + zero
    ang = pos[:, None] * inv_freq[None, :]
    cos2 = jnp.cos(ang)
    sin2 = jnp.where(lane[None, :] < half, -1.0, 1.0) * jnp.sin(ang)
    log_gamma = jnp.log1p(-(2.0 ** (-5.0 - jnp.arange(HEADS, dtype=f32))))
    idx = jnp.arange(CHUNK, dtype=f32)
    diff = idx[:, None] - idx[None, :]
    intra = jnp.where(diff[None] >= 0, jnp.exp(diff[None] * log_gamma[:, None, None]), 0.0)
    k_decay = jnp.exp((CHUNK - 1.0 - idx)[None, :] * log_gamma[:, None])
    q_decay = jnp.exp((idx + 1.0)[None, :] * log_gamma[:, None])
    chunk_decay = jnp.exp(CHUNK * log_gamma)
    full = (HEADS, CHUNK, HD)
    dec = jnp.stack([intra, jnp.broadcast_to(k_decay[:, :, None], full), jnp.broadcast_to(q_decay[:, :, None], full),
                     jnp.broadcast_to(chunk_decay[:, None, None], full)], axis=1)
    return cos2, sin2, dec


def _rot(t, cos2, sin2):
    return t * cos2 + pltpu.roll(t, HD // 2, 1) * sin2


def _rot_t(t, cos2, sin2):
    return t * cos2 - pltpu.roll(t, HD // 2, 1) * sin2


def _chunks_per_step(nch):
    return 3 if nch % 3 == 0 else 1


def retention_fwd(name, zq, dec):
    tp = zq.shape[0]
    nch = tp // CHUNK
    per = _chunks_per_step(nch)

    def body(q_ref, k_ref, v_ref, dec_ref, out_ref, st_ref, s_ref):
        @pl.when(pl.program_id(0) == 0)
        def _():
            s_ref[...] = jnp.zeros_like(s_ref)

        state = [s_ref[hh] for hh in range(HEADS)]
        for j in range(per):
            rows = slice(j * CHUNK, (j + 1) * CHUNK)
            for hh in range(HEADS):
                cols = slice(hh * HD, (hh + 1) * HD)
                qb, kb, vb = q_ref[rows, cols], k_ref[rows, cols], v_ref[rows, cols]
                sc = (_dot_nt(qb, kb) * dec_ref[hh, 0]).astype(bf16)
                sb = state[hh].astype(bf16)
                cross = _dot_nn((qb.astype(f32) * dec_ref[hh, 2]).astype(bf16), sb)
                out_ref[rows, cols] = (_dot_nn(sc, vb) + cross).astype(out_ref.dtype)
                st_ref[hh, j] = sb
                state[hh] = state[hh] * dec_ref[hh, 3] + _dot_tn((kb.astype(f32) * dec_ref[hh, 1]).astype(bf16), vb)
        for hh in range(HEADS):
            s_ref[hh] = state[hh]

    part = lambda j: pl.BlockSpec((per * CHUNK, RW), lambda n: (n, j))
    return pl.pallas_call(
        body, name=name, grid=(nch // per,),
        in_specs=[part(0), part(1), part(2), _resident((HEADS, 4, CHUNK, HD))],
        out_specs=[part(0), pl.BlockSpec((HEADS, per, HD, HD), lambda n: (0, n, 0, 0))],
        out_shape=[jax.ShapeDtypeStruct((tp, RW), bf16), jax.ShapeDtypeStruct((HEADS, nch, HD, HD), bf16)],
        scratch_shapes=[pltpu.VMEM((HEADS, HD, HD), f32)],
        compiler_params=_cp(("arbitrary",)))(zq, zq, zq, dec)


def retention_bwd(name, zq, cos2, sin2, dec, states, dout, pad, deps=()):
    tp = zq.shape[0]
    nch = tp // CHUNK
    per = _chunks_per_step(nch)
    nblk = nch // per
    scale = HD ** -0.5

    def body(q_ref, k_ref, v_ref, cos_ref, sin_ref, dec_ref, st_ref, do_ref, dq_ref, dk_ref, dv_ref, g_ref):
        @pl.when(pl.program_id(0) == 0)
        def _():
            g_ref[...] = jnp.zeros_like(g_ref)

        first_row = (nblk - 1 - pl.program_id(0)) * (per * CHUNK)
        gstate = [g_ref[hh] for hh in range(HEADS)]
        for j in reversed(range(per)):
            rows = slice(j * CHUNK, (j + 1) * CHUNK)
            cosv, sinv = cos_ref[rows, :], sin_ref[rows, :]
            keep = (lax.broadcasted_iota(jnp.int32, (CHUNK, HD), 0) + (first_row + j * CHUNK)) >= pad
            for hh in range(HEADS):
                cols = slice(hh * HD, (hh + 1) * HD)
                intra, kdec, qdec = dec_ref[hh, 0], dec_ref[hh, 1], dec_ref[hh, 2]
                qb, kb, vb = q_ref[rows, cols], k_ref[rows, cols], v_ref[rows, cols]
                qd = (qb.astype(f32) * qdec).astype(bf16)
                kd = (kb.astype(f32) * kdec).astype(bf16)
                sc = (_dot_nt(qb, kb) * intra).astype(bf16)
                dob = do_ref[rows, cols]
                sb = st_ref[hh, j]
                gb = gstate[hh].astype(bf16)
                dsc = (_dot_nt(dob, vb) * intra).astype(bf16)
                dv = _dot_tn(sc, dob) + _dot_nn(kd, gb)
                dqr = _dot_nn(dsc, kb) + _dot_nt(dob, sb) * qdec
                dkr = _dot_tn(dsc, qb) + _dot_nt(vb, gb) * kdec
                gstate[hh] = gstate[hh] * dec_ref[hh, 3] + _dot_tn(qd, dob)
                dq_ref[rows, cols] = jnp.where(keep, _rot_t(dqr * scale, cosv, sinv), 0.0).astype(bf16)
                dk_ref[rows, cols] = jnp.where(keep, _rot_t(dkr, cosv, sinv), 0.0).astype(bf16)
                dv_ref[rows, cols] = jnp.where(keep, dv, 0.0).astype(bf16)
        for hh in range(HEADS):
            g_ref[hh] = gstate[hh]

    part = lambda j: pl.BlockSpec((per * CHUNK, RW), lambda t: (nblk - 1 - t, j))
    table = pl.BlockSpec((per * CHUNK, HD), lambda t: (nblk - 1 - t, 0))
    return pl.pallas_call(
        _skip(len(deps), body), name=name, grid=(nblk,),
        in_specs=[_ANY] * len(deps) + [part(0), part(1), part(2), table, table, _resident((HEADS, 4, CHUNK, HD)),
                                       pl.BlockSpec((HEADS, per, HD, HD), lambda t: (0, nblk - 1 - t, 0, 0)), part(0)],
        out_specs=[part(0)] * 3,
        out_shape=[jax.ShapeDtypeStruct((tp, RW), bf16)] * 3,
        scratch_shapes=[pltpu.VMEM((HEADS, HD, HD), f32)],
        compiler_params=_cp(("arbitrary",)))(*deps, zq, zq, zq, cos2, sin2, dec, states, dout)


def _window_sum(xv, steps, tp, forward):
    s = xv
    for j in range(steps):
        sh = 2 ** j
        s = s + pltpu.roll(s, (tp - sh) if forward else sh, 0)
    return s


def pool_fwd(name, zu, maps, scale, pad):
    tp = zu.shape[0]

    def body(u_ref, maps_ref, scale_ref, pooled_ref, p_ref):
        row = lax.broadcasted_iota(jnp.int32, (tp, HD), 0)
        for gi, w in enumerate(POOL_WINDOWS):
            cols = slice(gi * HD, (gi + 1) * HD)
            xv = u_ref[:, cols]
            cnt = jnp.clip(row - (pad - 1), 1, w).astype(f32)
            pooled = jnp.where(row >= pad, _window_sum(xv, gi + 1, tp, False) / cnt - xv, 0.0).astype(bf16)
            pooled_ref[:, cols] = pooled
            p_ref[:, cols] = (_dot_nn(pooled, maps_ref[gi].astype(bf16)) * scale_ref[:, cols]).astype(bf16)

    return pl.pallas_call(
        body, name=name,
        out_shape=[jax.ShapeDtypeStruct((tp, RW), bf16), jax.ShapeDtypeStruct((tp, RW), bf16)],
        compiler_params=_cp())(zu, maps, scale)


def pool_bwd(name, dp, pooled, maps, scale, pad):
    tp = dp.shape[0]

    def body(dp_ref, pooled_ref, maps_ref, scale_ref, du_ref, dmaps_ref, dscale_ref):
        row = lax.broadcasted_iota(jnp.int32, (tp, HD), 0)
        dscale_ref[...] = jnp.zeros_like(dscale_ref)
        for gi, w in enumerate(POOL_WINDOWS):
            cols = slice(gi * HD, (gi + 1) * HD)
            mb = maps_ref[gi].astype(bf16)
            pooled = pooled_ref[:, cols]
            dpf = dp_ref[:, cols].astype(f32)
            dscale_ref[0:1, cols] = jnp.sum(dpf * _dot_nn(pooled, mb), axis=0, keepdims=True)
            dpm = (dpf * scale_ref[:, cols]).astype(bf16)
            dmaps_ref[gi * HD:(gi + 1) * HD, :] = _dot_tn(pooled, dpm)
            dpool = jnp.where(row >= pad, _dot_nt(dpm, mb), 0.0)
            cnt = jnp.clip(row - (pad - 1), 1, w).astype(f32)
            du = _window_sum(dpool / cnt, gi + 1, tp, True) - dpool
            du_ref[:, cols] = jnp.where(row >= pad, du, 0.0).astype(bf16)

    return pl.pallas_call(
        body, name=name,
        out_shape=[jax.ShapeDtypeStruct((tp, RW), bf16), jax.ShapeDtypeStruct((HEADS * HD, HD), f32),
                   jax.ShapeDtypeStruct((8, RW), f32)],
        compiler_params=_cp())(dp, pooled, maps, scale)


def _group_norm(o):
    mu = jnp.mean(o, axis=-1, keepdims=True)
    oc = o - mu
    rstd = lax.rsqrt(jnp.mean(oc * oc, axis=-1, keepdims=True) + EPS)
    return oc * rstd, rstd


def mix_out_fwd(name, h, oraw, zg, zgate, p, wretT, wpoolT, wout, deps=()):
    tp, d = h.shape
    tm = _row_tile(tp)

    def body(h_ref, o_ref, zg_ref, zgate_ref, p_ref, wr_ref, wp_ref, wo_ref, ho_ref, rp_ref, mixed_ref):
        parts = []
        for hh in range(HEADS):
            cols = slice(hh * HD, (hh + 1) * HD)
            rhat, _ = _group_norm(o_ref[:, cols].astype(f32))
            gv = zg_ref[:, cols].astype(f32)
            parts.append(rhat * (gv * _sigmoid(gv)))
        r = jnp.concatenate(parts, axis=-1).astype(bf16)
        pv = p_ref[...]
        ret = _dot_nt(r, wr_ref[...])
        pool = _dot_nt(pv, wp_ref[...])
        mixed = (_sigmoid(zgate_ref[:, :d].astype(f32)) * ret + _sigmoid(zgate_ref[:, d:].astype(f32)) * pool).astype(bf16)
        ho_ref[...] = h_ref[...] + _dot_nn(mixed, wo_ref[...])
        rp_ref[0] = r
        rp_ref[1] = pv
        mixed_ref[...] = mixed

    row = lambda w: pl.BlockSpec((tm, w), lambda i: (i, 0))
    return pl.pallas_call(
        _skip(len(deps), body), name=name, grid=(tp // tm,),
        in_specs=[_ANY] * len(deps) + [row(d), row(RW), row(RW), row(2 * d), row(RW), _resident((d, RW)), _resident((d, RW)),
                                       _resident((d, d))],
        out_specs=[row(d), pl.BlockSpec((2, tm, RW), lambda i: (0, i, 0)), pl.BlockSpec((None, tm, d), lambda i: (0, i, 0))],
        out_shape=[jax.ShapeDtypeStruct((tp, d), f32), jax.ShapeDtypeStruct((2, tp, RW), bf16),
                   jax.ShapeDtypeStruct((1, tp, d), bf16)],
        compiler_params=_cp(("arbitrary",)))(*deps, h, oraw, zg, zgate, p, wretT, wpoolT, wout)


def mix_out_bwd(name, dy, oraw, zg, zgate, rp, wretT, wpoolT, wout, deps=()):
    tp, d = dy.shape
    tm = _row_tile(tp)

    def body(dy_ref, o_ref, zg_ref, zgate_ref, rp_ref, wr_ref, wp_ref, wo_ref,
             do_ref, dzg_ref, dzgate_ref, dp_ref, drp_ref, dyb_ref):
        dyb = dy_ref[...].astype(bf16)
        dmixed = _dot_nt(dyb, wo_ref[...])
        sa = _sigmoid(zgate_ref[:, :d].astype(f32))
        sb = _sigmoid(zgate_ref[:, d:].astype(f32))
        dret = dmixed * sa
        dpool = dmixed * sb
        dzgate_ref[:, :d] = (dret * _dot_nt(rp_ref[0], wr_ref[...]) * (1.0 - sa)).astype(bf16)
        dzgate_ref[:, d:] = (dpool * _dot_nt(rp_ref[1], wp_ref[...]) * (1.0 - sb)).astype(bf16)
        dretb, dpoolb = dret.astype(bf16), dpool.astype(bf16)
        dr = _dot_nn(dretb, wr_ref[...])
        dp_ref[...] = _dot_nn(dpoolb, wp_ref[...]).astype(bf16)
        for hh in range(HEADS):
            cols = slice(hh * HD, (hh + 1) * HD)
            rhat, rstd = _group_norm(o_ref[:, cols].astype(f32))
            gv = zg_ref[:, cols].astype(f32)
            sg = _sigmoid(gv)
            drh = dr[:, cols]
            drhat = drh * (gv * sg)
            dzg_ref[:, cols] = (drh * rhat * (sg * (1.0 + gv * (1.0 - sg)))).astype(bf16)
            do = rstd * (drhat - jnp.mean(drhat, axis=-1, keepdims=True)
                         - rhat * jnp.mean(drhat * rhat, axis=-1, keepdims=True))
            do_ref[:, cols] = do.astype(bf16)
        drp_ref[0] = dretb
        drp_ref[1] = dpoolb
        dyb_ref[...] = dyb

    row = lambda w: pl.BlockSpec((tm, w), lambda i: (i, 0))
    return pl.pallas_call(
        _skip(len(deps), body), name=name, grid=(tp // tm,),
        in_specs=[_ANY] * len(deps) + [row(d), row(RW), row(RW), row(2 * d), pl.BlockSpec((2, tm, RW), lambda i: (0, i, 0)),
                                       _resident((d, RW)), _resident((d, RW)), _resident((d, d))],
        out_specs=[row(RW), row(RW), row(2 * d), row(RW), pl.BlockSpec((2, tm, d), lambda i: (0, i, 0)),
                   pl.BlockSpec((None, tm, d), lambda i: (0, i, 0))],
        out_shape=[jax.ShapeDtypeStruct((tp, RW), bf16), jax.ShapeDtypeStruct((tp, RW), bf16),
                   jax.ShapeDtypeStruct((tp, 2 * d), bf16), jax.ShapeDtypeStruct((tp, RW), bf16),
                   jax.ShapeDtypeStruct((2, tp, d), bf16), jax.ShapeDtypeStruct((1, tp, d), bf16)],
        compiler_params=_cp(("arbitrary",)))(*deps, dy, oraw, zg, zgate, rp, wretT, wpoolT, wout)


def final_loss(name, h, g, target):
    tp, d = h.shape
    tm = _row_tile(tp)
    nsub = tm // CHUNK

    def body(h_ref, g_ref, *rest):
        t_refs = rest[:nsub]
        dh_ref, loss_ref, dg_ref = rest[nsub:]
        i = pl.program_id(0)

        @pl.when(i == 0)
        def _():
            loss_ref[...] = jnp.zeros_like(loss_ref)
            dg_ref[...] = jnp.zeros_like(dg_ref)

        gv = g_ref[...]
        for j in range(nsub):
            rows = slice(j * CHUNK, (j + 1) * CHUNK)
            hhat, rs = _rms(h_ref[rows, :])
            err = jnp.where(i * nsub + j >= 1, hhat * gv - t_refs[j][...], 0.0)
            dyv = err / d
            dh_ref[rows, :] = _rms_bwd(dyv, gv, hhat, rs)
            loss_ref[...] += 0.5 * jnp.sum(jnp.sum(err * err, axis=-1, keepdims=True) / d)
            dg_ref[0:1, :] += jnp.sum(dyv * hhat, axis=0, keepdims=True)

    lagged = lambda j: pl.BlockSpec((CHUNK, d), lambda i: (jnp.maximum(i * nsub + j - 1, 0), 0))
    return pl.pallas_call(
        body, name=name, grid=(tp // tm,),
        in_specs=[pl.BlockSpec((tm, d), lambda i: (i, 0)), _resident((1, d))] + [lagged(j) for j in range(nsub)],
        out_specs=[pl.BlockSpec((tm, d), lambda i: (i, 0)), pl.BlockSpec((8, 128), lambda i: (0, 0)),
                   pl.BlockSpec((8, d), lambda i: (0, 0))],
        out_shape=[jax.ShapeDtypeStruct((tp, d), f32), jax.ShapeDtypeStruct((8, 128), f32),
                   jax.ShapeDtypeStruct((8, d), f32)],
        compiler_params=_cp(("arbitrary",)))(h, g, *[target] * nsub)


def _adamw(w, g, m, v):
    m = ADAM_B1 * m + (1.0 - ADAM_B1) * g
    v = ADAM_B2 * v + (1.0 - ADAM_B2) * (g * g)
    m_hat = m / (1.0 - ADAM_B1 ** ADAM_STEP)
    v_hat = v / (1.0 - ADAM_B2 ** ADAM_STEP)
    delta = -ADAM_LR * (m_hat / (jnp.sqrt(v_hat) + ADAM_EPS) + ADAM_WD * w)
    return delta, m, v


def adam_big(name, me, recv, own, b, layer, transposed, w, m, v, prev):
    r, c = recv.shape[2:]
    wshape = w.shape[1:]
    nchunk = 1 if transposed else next(k for k in (4, 2, 1) if r % (16 * k) == 0)
    rc = r // nchunk

    def body(me_ref, recv_ref, own_ref, w_ref, m_ref, v_ref, *rest):
        g_ref, d_ref, nm_ref, nv_ref = rest[-4:]
        g = own_ref[...].astype(f32)
        for j in range(NDEV - 1):
            g = g + recv_ref[j].astype(f32)
        if transposed:
            g = g.T
        delta, nm, nv = _adamw(w_ref[...], g, m_ref[...], v_ref[...])
        g_ref[...] = g
        d_ref[...] = delta
        nm_ref[...] = nm
        nv_ref[...] = nv

    wblock = wshape if transposed else (rc, c)
    wspec = pl.BlockSpec((None,) + wblock, lambda i, me_ref: (layer, i, 0))
    in_specs = [pl.BlockSpec((NDEV - 1, None, rc, c), lambda i, me_ref: (0, b, i, 0)),
                pl.BlockSpec((None, None, rc, c), lambda i, me_ref: (b, me_ref[0], i, 0)), wspec, wspec, wspec]
    args = [recv, own, w, m, v]
    aliases = {}
    if prev is not None:
        in_specs += [_ANY] * 4
        args += list(prev)
        aliases = {6 + k: k for k in range(4)}
    return pl.pallas_call(
        body, name=name,
        grid_spec=pltpu.PrefetchScalarGridSpec(num_scalar_prefetch=1, grid=(nchunk,), in_specs=in_specs,
                                               out_specs=[wspec] * 4),
        out_shape=[jax.ShapeDtypeStruct(w.shape, f32)] * 4, input_output_aliases=aliases,
        compiler_params=_cp())(me, *args)


def adam_small(name, ga0, gmaps0, gmeta, ga1, gmaps1, norms, pool_scale, pool_maps, meta, final_norm, d):
    def body(ga0_ref, gmaps0_ref, gmeta_ref, ga1_ref, gmaps1_ref, *refs):
        ins, outs = refs[:21], refs[21:]
        x, y, c = _me()
        me = 4 * x + 2 * y + c

        def total(ref, rows):
            t = ref[0, rows, :]
            for j in range(1, NDEV):
                t = t + ref[j, rows, :]
            return t

        row = lambda r: slice(r, r + 1)
        outs[0][...] = jnp.broadcast_to(total(ga1_ref, row(0))[:, :128], (8, 128))

        def update(k, g, o):
            w_ref, m_ref, v_ref = ins[3 * k:3 * k + 3]
            delta, nm, nv = _adamw(w_ref[...], g, m_ref[...], v_ref[...])
            for ref, val in zip(outs[o:o + 4], (g, delta, nm, nv)):
                ref[...] = val

        two = lax.broadcasted_iota(jnp.int32, (2, d), 0)
        for k in range(3):
            update(k, jnp.where(two == 0, total(ga0_ref, row(k)), total(ga1_ref, row(2 + k))), 1 + 4 * k)
        update(3, jnp.where(two[:, :RW] == 0, total(ga0_ref, row(3))[:, :RW], total(ga1_ref, row(5))[:, :RW]), 13)
        update(4, jnp.concatenate([total(gmaps0_ref, slice(None)), total(gmaps1_ref, slice(None))], axis=0), 17)
        update(5, total(gmeta_ref, pl.ds(pl.multiple_of(me * N_META, N_META), N_META)), 21)
        update(6, total(ga1_ref, row(1)), 25)

    flat = []
    for trip in (*norms, pool_scale, pool_maps, meta, final_norm):
        flat += list(trip)
    out_shapes = [jax.ShapeDtypeStruct((8, 128), f32)]
    for trip in (*norms, pool_scale, pool_maps, meta, final_norm):
        out_shapes += [jax.ShapeDtypeStruct(trip[0].shape, f32)] * 4
    return pl.pallas_call(body, name=name, out_shape=out_shapes,
                          compiler_params=_cp())(ga0, gmaps0, gmeta, ga1, gmaps1, *flat)


def kernel(x, meta, ffn1_norm, ffn1_gate, ffn1_up, ffn1_down, mix_norm, w_in, pool_maps, pool_scale, w_ret_up, w_pool_up, w_out, ffn2_norm, ffn2_gate, ffn2_up, ffn2_down, final_norm, loss_target, m_meta, m_ffn1_norm, m_ffn1_gate, m_ffn1_up, m_ffn1_down, m_mix_norm, m_w_in, m_pool_maps, m_pool_scale, m_w_ret_up, m_w_pool_up, m_w_out, m_ffn2_norm, m_ffn2_gate, m_ffn2_up, m_ffn2_down, m_final_norm, v_meta, v_ffn1_norm, v_ffn1_gate, v_ffn1_up, v_ffn1_down, v_mix_norm, v_w_in, v_pool_maps, v_pool_scale, v_w_ret_up, v_w_pool_up, v_w_out, v_ffn2_norm, v_ffn2_gate, v_ffn2_up, v_ffn2_down, v_final_norm):
    seq, d = x.shape[1], x.shape[2]
    depth = ffn1_gate.shape[0]
    ff = ffn1_gate.shape[2] * NDEV
    nin = w_in.shape[2] * NDEV
    length = seq + N_META
    pad = (-length) % CHUNK
    tp = length + pad
    assert pad % 8 == 0 and pad + N_META == CHUNK and depth == 2 and nin == 5 * RW + 2 * d

    ix, iy, ic = _me()
    me = (4 * ix + 2 * iy + ic).astype(jnp.int32).reshape(1)

    meta_all, = all_gather("gather_meta", [meta])
    meta_full = jnp.transpose(meta_all, (1, 0, 2)).reshape(N_META, d)

    token = meta_all
    tview = lambda *arrs: [jnp.swapaxes(a, 1, 2) for a in arrs]
    t_g1, t_u1, t_g2, t_u2, t_in = (tview(w, m, v) for w, m, v in (
        (ffn1_gate, m_ffn1_gate, v_ffn1_gate), (ffn1_up, m_ffn1_up, v_ffn1_up), (ffn2_gate, m_ffn2_gate, v_ffn2_gate),
        (ffn2_up, m_ffn2_up, v_ffn2_up), (w_in, m_w_in, v_w_in)))
    keys, groups = [], []
    for layer in range(depth):
        lands = prep_layer(layer, me, [w_ret_up, w_pool_up],
                           [t_g1[0], t_u1[0], t_g2[0], t_u2[0], t_in[0], ffn1_down, ffn2_down, w_out],
                           (token,) if layer else ())
        wretT, wpoolT, g1T, u1T, g2T, u2T, winT, d1, d2, wout = lands
        keys += [("ffn1", layer), ("mix", layer), ("ffn2", layer)]
        groups += [[g1T, u1T, d1], [winT, wretT, wpoolT, wout], [g2T, u2T, d2]]
        if layer == 0:
            first, token = gather_start_chips("gather_start_first", groups[:1], (token,))
    second, token = gather_start_chips("gather_start_second", groups[1:2], (token,))
    gathers = dict(zip(keys, first + second))

    def forward(part, layer, after):
        ssem, rsem, group = gathers[(part, layer)]
        ssem, rsem, group, tok = gather_forward(f"gather_forward_{part}{layer}", ssem, rsem, group, after)
        gathers[(part, layer)] = (ssem, rsem, group)
        return tok

    def gathered(part, layer, after):
        ssem, rsem, group = gathers[(part, layer)]
        _, full = copies_wait(f"gather_wait_{part}{layer}", ssem, rsem, (), group, after, 3)
        return [a.reshape((NDEV * a.shape[1],) + a.shape[2:]) for a in full]

    cos2, sin2, dec = _retention_tables(tp, pad, token[0, 0])
    h = jnp.concatenate([jnp.zeros((pad, d), f32), meta_full + (token[0, 0] + 0.0 * cos2[0, 0] + 0.0 * dec[0, 0, 0, 0]), x[0]],
                        axis=0)

    saved = []
    weights = []
    tok = forward("ffn1", 0, h)
    rest, tok = gather_start_chips("gather_start_rest", groups[2:], (tok,))
    gathers.update(zip(keys[2:], rest))
    for layer in range(depth):
        row = lambda a: a[layer:layer + 1]
        s = {"h0": h}
        g1T, u1T, d1 = gathered("ffn1", layer, tok if layer == 0 else h)
        tok = forward("mix", layer, h) if layer else None
        h, s["a1"], s["b1"] = ffn_fwd(f"ffn1_fwd{layer}", h, row(ffn1_norm), g1T, u1T, d1, (tok,) if layer else ())
        s["h1"] = h
        if layer == 0:
            tok = forward("mix", layer, h)
        winT, wretT, wpoolT, wout = gathered("mix", layer, tok if layer == 0 else h)
        s["zq"], s["zg"], zu, s["zgate"] = mix_in_fwd(f"mix_in_fwd{layer}", h, row(mix_norm), winT, cos2, sin2)
        s["oraw"], s["states"] = retention_fwd(f"retention_fwd{layer}", s["zq"], dec)
        s["pooled"], p = pool_fwd(f"pool_fwd{layer}", zu, pool_maps[layer], row(pool_scale), pad)
        tok = forward("ffn2", layer, p)
        h, s["rp"], s["mixed"] = mix_out_fwd(
            f"mix_out_fwd{layer}", h, s["oraw"], s["zg"], s["zgate"], p, wretT, wpoolT, wout, (tok,))
        s["h2"] = h
        g2T, u2T, d2 = gathered("ffn2", layer, h)
        tok = (forward("ffn1", layer + 1, h),) if layer + 1 < depth else ()
        h, s["a2"], s["b2"] = ffn_fwd(f"ffn2_fwd{layer}", h, row(ffn2_norm), g2T, u2T, d2, tok)
        saved.append(s)
        weights.append((g1T, u1T, g2T, u2T, winT, wretT, wpoolT, d1, d2, wout))

    dh, loss_part, dg_final = final_loss("final_loss", h, final_norm.reshape(1, d), loss_target[0])

    small = {}
    small_gathers = {}
    exchanges = {}
    token = None

    def rows8(vals):
        at = lax.broadcasted_iota(jnp.int32, (8, d), 0)
        out = jnp.zeros((8, d), f32)
        for k, v in enumerate(vals):
            r0 = v[0:1]
            r0 = r0 if r0.shape[1] == d else jnp.pad(r0, ((0, 0), (0, d - r0.shape[1])))
            out = jnp.where(at == k, r0, out)
        return out

    def exchange(part, layer, grads):
        by_dest = [g.reshape(g.shape[0], NDEV, g.shape[1] // NDEV, g.shape[2]) for g in grads]
        ssem, rsem, sent, lands, tok = exchange_start(f"exchange_start_{part}{layer}", by_dest)
        exchanges[(part, layer)] = (ssem, rsem, sent, lands)
        return (tok,)

    for layer in reversed(range(depth)):
        g1T, u1T, g2T, u2T, winT, wretT, wpoolT, d1, d2, wout = weights[layer]
        row = lambda a: a[layer:layer + 1]
        s = saved[layer]
        dh, lhs2, rhs2, small[("ffn2", layer)] = ffn_bwd(
            f"ffn2_bwd{layer}", dh, s["h2"], row(ffn2_norm), s["a2"], s["b2"], g2T, u2T, d2, () if token is None else token)
        token = exchange("ffn2", layer, [mm_tn(f"ffn2_wgrad{layer}", lhs2, rhs2, lambda b: b // 2)])
        do, dzg, dzgate, dp, drp, dyb = mix_out_bwd(
            f"mix_out_bwd{layer}", dh, s["oraw"], s["zg"], s["zgate"], s["rp"], wretT, wpoolT, wout, token)
        gw_mix = [mm_tn(f"w_out_wgrad{layer}", s["mixed"], dyb, lambda b: b),
                  mm_tn(f"up_wgrad{layer}", drp, s["rp"], lambda b: b)]
        if layer == 0:
            token = exchange("mix_up", layer, gw_mix)
            gw_mix = []
        dq, dk, dv = retention_bwd(f"retention_bwd{layer}", s["zq"], cos2, sin2, dec, s["states"], do, pad, token)
        dzu, small[("maps", layer)], small[("scale", layer)] = pool_bwd(
            f"pool_bwd{layer}", dp, s["pooled"], pool_maps[layer], row(pool_scale), pad)
        dh, dz, n2, small[("mix", layer)] = mix_in_bwd(
            f"mix_in_bwd{layer}", dq, dk, dv, dzg, dzu, dzgate, s["h1"], row(mix_norm), winT, dh)
        token = exchange("mix" if layer else "w_in", layer, gw_mix + [mm_tn(f"w_in_wgrad{layer}", dz, n2, lambda b: b)])
        dh, lhs1, rhs1, small[("ffn1", layer)] = ffn_bwd(
            f"ffn1_bwd{layer}", dh, s["h0"], row(ffn1_norm), s["a1"], s["b1"], g1T, u1T, d1, token)
        rows = [small[("ffn1", layer)], small[("mix", layer)], small[("ffn2", layer)], small[("scale", layer)]]
        packs = [rows8([loss_part, dg_final] + rows if layer == depth - 1 else rows), small[("maps", layer)]]
        if layer == 0:
            dmeta = dh[pad:CHUNK]
            packs.append(jnp.transpose(dmeta.reshape(N_META, NDEV, d // NDEV), (1, 0, 2)).reshape(NDEV * N_META, d // NDEV))
        ssem, rsem, lands, tok = gather_start(f"small_start{layer}", slot_in(f"small_slot{layer}", me, packs))
        small_gathers[layer] = (ssem, rsem, lands)
        if layer:
            token = exchange("ffn1", layer, [mm_tn(f"ffn1_wgrad{layer}", lhs1, rhs1, lambda b: b // 2, (tok,))])
        else:
            token = (tok,)
            for j, nm in enumerate(("ffn1_gate", "ffn1_up", "ffn1_down")):
                token = exchange(nm, layer, [mm_tn(f"{nm}_wgrad{layer}", lhs1, rhs1, lambda b: b // 2, token, only=j)])

    grad_x = (dh[CHUNK:] + token[0][0, 0])[None]

    big = {}
    after = token[0]
    plans = {
        "ffn2": [("ffn2_gate", 0, 0, False, *t_g2), ("ffn2_up", 0, 1, False, *t_u2),
                 ("ffn2_down", 0, 2, False, ffn2_down, m_ffn2_down, v_ffn2_down)],
        "mix": [("w_out", 0, 0, False, w_out, m_w_out, v_w_out),
                ("w_ret_up", 1, 0, True, w_ret_up, m_w_ret_up, v_w_ret_up),
                ("w_pool_up", 1, 1, True, w_pool_up, m_w_pool_up, v_w_pool_up), ("w_in", 2, 0, False, *t_in)],
        "ffn1": [("ffn1_gate", 0, 0, False, *t_g1), ("ffn1_up", 0, 1, False, *t_u1),
                 ("ffn1_down", 0, 2, False, ffn1_down, m_ffn1_down, v_ffn1_down)]}
    for nm, k, b, tr, w, m, v in plans["ffn1"]:
        plans[nm] = [(nm, 0, 0, tr, w, m, v)]
    plans["mix_up"], plans["w_in"] = plans["mix"][:3], [("w_in", 0, 0, False, *t_in)]
    for layer in reversed(range(depth)):
        for part in ("ffn2", "mix", "ffn1") if layer else ("ffn2", "mix_up", "w_in", "ffn1_gate", "ffn1_up", "ffn1_down"):
            ssem, rsem, sent, lands = exchanges[(part, layer)]
            sent, lands = copies_wait(f"exchange_wait_{part}{layer}", ssem, rsem, sent, lands, after)
            for nm, k, b, tr, w, m, v in plans[part]:
                big[nm] = adam_big(f"adam_{nm}{layer}", me, lands[k], sent[k], b, layer, tr, w, m, v, big.get(nm))
                after = big[nm][0]

    gsmall = []
    for layer in range(depth):
        ssem, rsem, lands = small_gathers[layer]
        gsmall += copies_wait(f"small_wait{layer}", ssem, rsem, (), lands, after)[1]

    maps2 = lambda a: a.reshape(depth * HEADS * HD, HD)
    res = adam_small(
        "adam_small", *gsmall,
        [(ffn1_norm, m_ffn1_norm, v_ffn1_norm), (mix_norm, m_mix_norm, v_mix_norm), (ffn2_norm, m_ffn2_norm, v_ffn2_norm)],
        (pool_scale, m_pool_scale, v_pool_scale), (maps2(pool_maps), maps2(m_pool_maps), maps2(v_pool_maps)),
        (meta, m_meta, v_meta), tuple(a.reshape(1, d) for a in (final_norm, m_final_norm, v_final_norm)), d)
    loss = res[0][0, 0]
    sm = {}
    for k, nm in enumerate(["ffn1_norm", "mix_norm", "ffn2_norm", "pool_scale", "pool_maps", "meta", "final_norm"]):
        sm[nm] = list(res[1 + 4 * k:5 + 4 * k])
    sm["pool_maps"] = [a.reshape(pool_maps.shape) for a in sm["pool_maps"]]
    sm["final_norm"] = [a.reshape(d) for a in sm["final_norm"]]

    names = ["meta", "ffn1_norm", "ffn1_gate", "ffn1_up", "ffn1_down", "mix_norm", "w_in", "pool_maps", "pool_scale",
             "w_ret_up", "w_pool_up", "w_out", "ffn2_norm", "ffn2_gate", "ffn2_up", "ffn2_down", "final_norm"]
    for nm in ("ffn1_gate", "ffn1_up", "ffn2_gate", "ffn2_up", "w_in"):
        big[nm] = tview(*big[nm])
    allw = {**{k: list(v) for k, v in big.items()}, **sm}
    outs = [loss, grad_x]
    for kind in range(4):
        outs += [allw[nm][kind] for nm in names]
    return tuple(outs)
```

```python
import functools

import jax
import jax.numpy as jnp
from jax import lax
from jax.experimental import pallas as pl
from jax.experimental.pallas import tpu as pltpu

f32 = jnp.float32
bf16 = jnp.bfloat16
MESH = pl.DeviceIdType.MESH
NDEV = 8
N_META = 16
HEADS = 4
HD = 128
CHUNK = 128
RW = HEADS * HD
POOL_WINDOWS = (2, 4, 8, 16)
ROPE_BASE = 10000.0
EPS = 1e-6
ADAM_LR = 0.001
ADAM_B1 = 0.9
ADAM_B2 = 0.999
ADAM_EPS = 1e-08
ADAM_WD = 0.01
ADAM_STEP = 10
VMEM_CAP_MB = 60


def _cp(sem=None):
    return pltpu.CompilerParams(vmem_limit_bytes=VMEM_CAP_MB * 2**20, dimension_semantics=sem)


def _row_tile(tp, want=384):
    return want if tp % want == 0 else 128


def _resident(shape):
    nd = len(shape)
    return pl.BlockSpec(shape, lambda *_: (0,) * nd, pipeline_mode=pl.Buffered(1))


def _skip(nd, body):
    return (lambda *refs: body(*refs[nd:])) if nd else body


def _dot_nn(a, b):
    return lax.dot_general(a, b, (((1,), (0,)), ((), ())), preferred_element_type=f32)


def _dot_nt(a, b):
    return lax.dot_general(a, b, (((1,), (1,)), ((), ())), preferred_element_type=f32)


def _dot_tn(a, b):
    return lax.dot_general(a, b, (((0,), (0,)), ((), ())), preferred_element_type=f32)


def _rms(h):
    rs = lax.rsqrt(jnp.mean(h * h, axis=-1, keepdims=True) + EPS)
    return h * rs, rs


def _rms_bwd(dn, g, hhat, rs):
    dhh = dn * g
    return rs * (dhh - hhat * jnp.mean(dhh * hhat, axis=-1, keepdims=True))


def _sigmoid(x):
    return jax.nn.sigmoid(x)


def _me():
    return lax.axis_index("x"), lax.axis_index("y"), lax.axis_index("c")


def _peer(idx):
    return (idx // 4, (idx // 2) % 2, idx % 2)


def all_gather(name, arrays):
    n = len(arrays)

    def body(*refs):
        ins, outs = refs[:n], refs[n:2 * n]
        send_sems, recv_sems, local_sems = refs[2 * n:]
        x, y, c = _me()
        me = 4 * x + 2 * y + c
        locals_ = []
        for k in range(n):
            cp = pltpu.make_async_copy(ins[k], outs[k].at[me], local_sems.at[k])
            cp.start()
            locals_.append(cp)
        for d in range(1, NDEV):
            for k in range(n):
                pltpu.make_async_remote_copy(
                    src_ref=ins[k], dst_ref=outs[k].at[me], send_sem=send_sems.at[k], recv_sem=recv_sems.at[k],
                    device_id=_peer((me + d) % NDEV), device_id_type=MESH).start()
        for k in range(n):
            seven = outs[k].at[pl.ds(0, NDEV - 1)]
            w = pltpu.make_async_remote_copy(src_ref=seven, dst_ref=seven, send_sem=send_sems.at[k],
                                             recv_sem=recv_sems.at[k], device_id=(x, y, c), device_id_type=MESH)
            w.wait_send()
            w.wait_recv()
            locals_[k].wait()

    anyspec = pl.BlockSpec(memory_space=pl.ANY)
    return pl.pallas_call(
        body, name=name,
        out_shape=[jax.ShapeDtypeStruct((NDEV,) + a.shape, a.dtype) for a in arrays],
        in_specs=[anyspec] * n, out_specs=[anyspec] * n,
        scratch_shapes=[pltpu.SemaphoreType.DMA((n,)), pltpu.SemaphoreType.DMA((n,)), pltpu.SemaphoreType.DMA((n,))],
    )(*arrays)


_HBM = pl.BlockSpec(memory_space=pltpu.HBM)
_SEM = pl.BlockSpec(memory_space=pltpu.SEMAPHORE)
_ANY = pl.BlockSpec(memory_space=pl.ANY)
_EFFECT = pltpu.SideEffectType.DATAFLOW_SIDE_EFFECTING


def _in_hbm(a):
    return pltpu.with_memory_space_constraint(a, pltpu.HBM)


def gather_start(name, lands, deps=()):
    n, nd = len(lands), len(deps)

    def body(*refs):
        land = refs[nd:nd + n]
        send_sems, recv_sems = refs[nd + n:nd + n + 2]
        token = refs[-1]
        x, y, c = _me()
        me = 4 * x + 2 * y + c
        for d in range(1, NDEV):
            for k in range(n):
                pltpu.make_async_remote_copy(
                    src_ref=land[k].at[me], dst_ref=land[k].at[me], send_sem=send_sems.at[k], recv_sem=recv_sems.at[k],
                    device_id=_peer((me + d) % NDEV), device_id_type=MESH).start()
        token[...] = jnp.zeros_like(token)

    res = pl.pallas_call(
        body, name=name,
        out_shape=(pltpu.SemaphoreType.DMA((n,)), pltpu.SemaphoreType.DMA((n,)),
                   *[pltpu.HBM(a.shape, a.dtype) for a in lands], jax.ShapeDtypeStruct((8, 128), f32)),
        in_specs=[_ANY] * nd + [_HBM] * n,
        out_specs=(_SEM, _SEM, *[_HBM] * n, pl.BlockSpec(memory_space=pltpu.VMEM)),
        input_output_aliases={nd + k: 2 + k for k in range(n)},
        compiler_params=pltpu.CompilerParams(has_side_effects=_EFFECT),
    )(*deps, *[_in_hbm(a) for a in lands])
    return res[0], res[1], list(res[2:2 + n]), res[-1]


def _other_chips(x, y):
    return [(1 - x, y), (x, 1 - y), (1 - x, 1 - y)]


def gather_start_chips(name, groups, deps=()):
    sizes = [len(g) for g in groups]
    lands = [a for g in groups for a in g]
    n, nd, ng = len(lands), len(deps), len(groups)

    def body(*refs):
        land = refs[nd:nd + n]
        sems = refs[nd + n:nd + n + 2 * ng]
        token = refs[-1]
        x, y, c = _me()
        me = 4 * x + 2 * y + c
        k = 0
        for g, size in enumerate(sizes):
            for j in range(size):
                for to in [(x, y, 1 - c)] + [(cx, cy, c) for cx, cy in _other_chips(x, y)]:
                    pltpu.make_async_remote_copy(
                        src_ref=land[k].at[me], dst_ref=land[k].at[me], send_sem=sems[2 * g].at[j],
                        recv_sem=sems[2 * g + 1].at[j], device_id=to, device_id_type=MESH).start()
                k += 1
        token[...] = jnp.zeros_like(token)

    res = pl.pallas_call(
        body, name=name,
        out_shape=(*[pltpu.SemaphoreType.DMA((size,)) for size in sizes for _ in range(2)],
                   *[pltpu.HBM(a.shape, a.dtype) for a in lands], jax.ShapeDtypeStruct((8, 128), f32)),
        in_specs=[_ANY] * nd + [_HBM] * n,
        out_specs=(*[_SEM] * (2 * ng), *[_HBM] * n, pl.BlockSpec(memory_space=pltpu.VMEM)),
        input_output_aliases={nd + k: 2 * ng + k for k in range(n)},
        compiler_params=pltpu.CompilerParams(has_side_effects=_EFFECT),
    )(*deps, *[_in_hbm(a) for a in lands])
    out, k = [], 2 * ng
    for g, size in enumerate(sizes):
        out.append((res[2 * g], res[2 * g + 1], list(res[k:k + size])))
        k += size
    return out, res[-1]


def gather_forward(name, send_sems, recv_sems, lands, after):
    n = len(lands)

    def body(*refs):
        land = refs[:n]
        ssem, rsem = refs[n:n + 2]
        send2, recv2 = refs[n + 3:n + 5]
        token = refs[-1]
        x, y, c = _me()
        for k in range(n):
            four = land[k].at[pl.ds(0, 4)]
            w = pltpu.make_async_remote_copy(src_ref=four, dst_ref=four, send_sem=ssem.at[k], recv_sem=rsem.at[k],
                                             device_id=(x, y, c), device_id_type=MESH)
            w.wait_send()
            w.wait_recv()
            for cx, cy in _other_chips(x, y):
                slot = 4 * cx + 2 * cy + c
                pltpu.make_async_remote_copy(
                    src_ref=land[k].at[slot], dst_ref=land[k].at[slot], send_sem=send2.at[k], recv_sem=recv2.at[k],
                    device_id=(x, y, 1 - c), device_id_type=MESH).start()
        token[...] = jnp.zeros_like(token)

    res = pl.pallas_call(
        body, name=name,
        out_shape=(pltpu.SemaphoreType.DMA((n,)), pltpu.SemaphoreType.DMA((n,)),
                   *[pltpu.HBM(a.shape, a.dtype) for a in lands], jax.ShapeDtypeStruct((8, 128), f32)),
        in_specs=[_HBM] * n + [_SEM, _SEM, _ANY],
        out_specs=(_SEM, _SEM, *[_HBM] * n, pl.BlockSpec(memory_space=pltpu.VMEM)),
        input_output_aliases={k: 2 + k for k in range(n)},
        compiler_params=pltpu.CompilerParams(has_side_effects=_EFFECT),
    )(*lands, send_sems, recv_sems, after)
    return res[0], res[1], list(res[2:2 + n]), res[-1]


def exchange_start(name, grads, deps=()):
    n, nd = len(grads), len(deps)
    lands = [lax.empty((NDEV - 1, g.shape[0]) + g.shape[2:], g.dtype) for g in grads]

    def body(*refs):
        src = refs[nd:nd + n]
        land = refs[nd + n:nd + 2 * n]
        send_sems, recv_sems = refs[nd + 2 * n:nd + 2 * n + 2]
        token = refs[-1]
        x, y, c = _me()
        me = 4 * x + 2 * y + c
        for d in range(1, NDEV):
            p = (me + d) % NDEV
            for k in range(n):
                pltpu.make_async_remote_copy(
                    src_ref=src[k].at[:, p], dst_ref=land[k].at[d - 1], send_sem=send_sems.at[k], recv_sem=recv_sems.at[k],
                    device_id=_peer(p), device_id_type=MESH).start()
        token[...] = jnp.zeros_like(token)

    both = list(grads) + lands
    res = pl.pallas_call(
        body, name=name,
        out_shape=(pltpu.SemaphoreType.DMA((n,)), pltpu.SemaphoreType.DMA((n,)),
                   *[pltpu.HBM(a.shape, a.dtype) for a in both], jax.ShapeDtypeStruct((8, 128), f32)),
        in_specs=[_ANY] * nd + [_HBM] * (2 * n),
        out_specs=(_SEM, _SEM, *[_HBM] * (2 * n), pl.BlockSpec(memory_space=pltpu.VMEM)),
        input_output_aliases={nd + k: 2 + k for k in range(2 * n)},
        compiler_params=pltpu.CompilerParams(has_side_effects=_EFFECT),
    )(*deps, *[_in_hbm(a) for a in both])
    return res[0], res[1], list(res[2:2 + n]), list(res[2 + n:2 + 2 * n]), res[-1]


def copies_wait(name, send_sems, recv_sems, sent, lands, after, count=NDEV - 1):
    ns, n = len(sent), len(lands)

    def body(*refs):
        land = refs[ns:ns + n]
        ssem, rsem = refs[ns + n:ns + n + 2]
        x, y, c = _me()
        for k in range(n):
            seven = land[k].at[pl.ds(0, count)]
            w = pltpu.make_async_remote_copy(src_ref=seven, dst_ref=seven, send_sem=ssem.at[k], recv_sem=rsem.at[k],
                                             device_id=(x, y, c), device_id_type=MESH)
            w.wait_send()
            w.wait_recv()

    both = list(sent) + list(lands)
    res = pl.pallas_call(
        body, name=name, out_shape=tuple(pltpu.HBM(a.shape, a.dtype) for a in both),
        in_specs=[_HBM] * (ns + n) + [_SEM, _SEM, _ANY], out_specs=tuple([_HBM] * (ns + n)),
        input_output_aliases={k: k for k in range(ns + n)},
        compiler_params=pltpu.CompilerParams(has_side_effects=_EFFECT),
    )(*both, send_sems, recv_sems, after)
    return list(res[:ns]), list(res[ns:])


def prep_layer(layer, me, col_sharded, row_sharded, deps=()):
    nc, nr = len(col_sharded), len(row_sharded)

    def body(me_ref, *refs):
        ins, outs = refs[:nc + nr], refs[nc + nr + len(deps):]
        for k in range(nc):
            outs[k][...] = ins[k][...].T.astype(bf16)
        for k in range(nc, nc + nr):
            outs[k][...] = ins[k][...].astype(bf16)

    arrs = list(col_sharded) + list(row_sharded)
    in_specs = [pl.BlockSpec((None,) + a.shape[1:], lambda i, me_ref: (layer, 0, 0)) for a in arrs]
    shapes = [(a.shape[2], a.shape[1]) for a in col_sharded] + [a.shape[1:] for a in row_sharded]
    out_specs = [pl.BlockSpec((None,) + s, lambda i, me_ref: (me_ref[0], 0, 0)) for s in shapes]
    return pl.pallas_call(
        body, name=f"prep_layer{layer}",
        grid_spec=pltpu.PrefetchScalarGridSpec(num_scalar_prefetch=1, grid=(1,), in_specs=in_specs + [_ANY] * len(deps),
                                               out_specs=out_specs),
        out_shape=[jax.ShapeDtypeStruct((NDEV,) + s, bf16) for s in shapes], compiler_params=_cp())(me, *arrs, *deps)


def slot_in(name, me, arrays):
    n = len(arrays)

    def body(me_ref, *refs):
        for k in range(n):
            refs[n + k][...] = refs[k][...]

    in_specs = [pl.BlockSpec(a.shape, lambda i, me_ref: (0, 0)) for a in arrays]
    out_specs = [pl.BlockSpec((None,) + a.shape, lambda i, me_ref: (me_ref[0], 0, 0)) for a in arrays]
    return pl.pallas_call(
        body, name=name,
        grid_spec=pltpu.PrefetchScalarGridSpec(num_scalar_prefetch=1, grid=(1,), in_specs=in_specs, out_specs=out_specs),
        out_shape=[jax.ShapeDtypeStruct((NDEV,) + a.shape, a.dtype) for a in arrays])(me, *arrays)


def _ff_chunks(ff, want=768):
    if ff % 256:
        return [slice(0, ff)]
    return [slice(c, min(c + want, ff)) for c in range(0, ff, want)]


def ffn_fwd(name, h, g, wgT, wuT, wd, deps=()):
    tp, d = h.shape
    ff = wgT.shape[0]
    tm = _row_tile(tp, 704)

    def body(h_ref, g_ref, wg_ref, wu_ref, wd_ref, ho_ref, a_ref, b_ref):
        hh = h_ref[...]
        hhat, _ = _rms(hh)
        n = (hhat * g_ref[...]).astype(bf16)
        acc = None
        for cols in _ff_chunks(ff):
            a = _dot_nt(n, wg_ref[cols, :])
            b = _dot_nt(n, wu_ref[cols, :])
            part = _dot_nn(((a * _sigmoid(a)) * b).astype(bf16), wd_ref[cols, :])
            acc = part if acc is None else acc + part
            a_ref[:, cols] = a.astype(bf16)
            b_ref[:, cols] = b.astype(bf16)
        ho_ref[...] = hh + 0.5 * acc

    row = lambda w: pl.BlockSpec((tm, w), lambda i: (i, 0))
    return pl.pallas_call(
        _skip(len(deps), body), name=name, grid=(tp // tm,),
        in_specs=[_ANY] * len(deps) + [row(d), _resident((1, d)), _resident((ff, d)), _resident((ff, d)), _resident((ff, d))],
        out_specs=[row(d), row(ff), row(ff)],
        out_shape=[jax.ShapeDtypeStruct((tp, d), f32), jax.ShapeDtypeStruct((tp, ff), bf16),
                   jax.ShapeDtypeStruct((tp, ff), bf16)],
        compiler_params=_cp(("arbitrary",)))(*deps, h, g, wgT, wuT, wd)


def ffn_bwd(name, dy, h, g, a, b, wgT, wuT, wd, deps=()):
    tp, d = h.shape
    ff = wgT.shape[0]
    tm = _row_tile(tp, 384)

    def body(dy_ref, h_ref, g_ref, a_ref, b_ref, wg_ref, wu_ref, wd_ref, dh_ref, lhs_ref, rhs_ref, dg_ref):
        dyv = dy_ref[...]
        hhat, rs = _rms(h_ref[...])
        gv = g_ref[...]
        n = hhat * gv
        dyh = (0.5 * dyv).astype(bf16)
        dn = None
        for cols in _ff_chunks(ff):
            ds = _dot_nt(dyh, wd_ref[cols, :])
            av = a_ref[:, cols].astype(f32)
            bv = b_ref[:, cols].astype(f32)
            sg = _sigmoid(av)
            sa = av * sg
            da = (ds * bv * (sg * (1.0 + av * (1.0 - sg)))).astype(bf16)
            db = (ds * sa).astype(bf16)
            part = _dot_nn(da, wg_ref[cols, :]) + _dot_nn(db, wu_ref[cols, :])
            dn = part if dn is None else dn + part
            lhs_ref[0, :, cols] = da
            lhs_ref[1, :, cols] = db
            lhs_ref[2, :, cols] = (sa * bv).astype(bf16)
        dh_ref[...] = dyv + _rms_bwd(dn, gv, hhat, rs)

        @pl.when(pl.program_id(0) == 0)
        def _():
            dg_ref[...] = jnp.zeros_like(dg_ref)

        dg_ref[0:1, :] += jnp.sum(dn * hhat, axis=0, keepdims=True)
        rhs_ref[0] = n.astype(bf16)
        rhs_ref[1] = dyh

    row = lambda w: pl.BlockSpec((tm, w), lambda i: (i, 0))
    return pl.pallas_call(
        _skip(len(deps), body), name=name, grid=(tp // tm,),
        in_specs=[_ANY] * len(deps) + [row(d), row(d), _resident((1, d)), row(ff), row(ff),
                  _resident((ff, d)), _resident((ff, d)), _resident((ff, d))],
        out_specs=[row(d), pl.BlockSpec((3, tm, ff), lambda i: (0, i, 0)), pl.BlockSpec((2, tm, d), lambda i: (0, i, 0)),
                   pl.BlockSpec((8, d), lambda i: (0, 0))],
        out_shape=[jax.ShapeDtypeStruct((tp, d), f32), jax.ShapeDtypeStruct((3, tp, ff), bf16),
                   jax.ShapeDtypeStruct((2, tp, d), bf16), jax.ShapeDtypeStruct((8, d), f32)],
        compiler_params=_cp(("arbitrary",)))(*deps, dy, h, g, a, b, wgT, wuT, wd)


def mm_tn(name, lhs, rhs, rhs_of, deps=(), only=None):
    _, tp, m = lhs.shape
    b0, nb = (0, lhs.shape[0]) if only is None else (only, 1)
    n = rhs.shape[2]
    def fits(t, ms):
        mb = m // ms
        return (tp % t == 0 and m % (128 * ms) == 0
                and 2 * t * (mb + n) * 2 + mb * n * (2 * 2 + 4 + (4 if t < tp else 0)) <= 54 * 2**20)

    tk, msplit = next(((t, ms) for t in (tp, 1408, 704, 384) for ms in (1, 2, 4) if fits(t, ms)), (128, 1))
    nk = tp // tk
    mb = m // msplit

    def body(l_ref, r_ref, o_ref, acc_ref):
        if nk == 1:
            o_ref[...] = _dot_tn(l_ref[...], r_ref[...]).astype(o_ref.dtype)
            return
        k = pl.program_id(2)

        @pl.when(k == 0)
        def _():
            acc_ref[...] = jnp.zeros_like(acc_ref)

        acc_ref[...] += _dot_tn(l_ref[...], r_ref[...])

        @pl.when(k == nk - 1)
        def _():
            o_ref[...] = acc_ref[...].astype(o_ref.dtype)

    return pl.pallas_call(
        _skip(len(deps), body), name=name, grid=(nb, msplit, nk),
        in_specs=[_ANY] * len(deps) + [pl.BlockSpec((None, tk, mb), lambda b, j, k: (b0 + b, k, j)),
                                       pl.BlockSpec((None, tk, n), lambda b, j, k: (rhs_of(b0 + b), k, 0))],
        out_specs=pl.BlockSpec((None, mb, n), lambda b, j, k: (b, j, 0)),
        out_shape=jax.ShapeDtypeStruct((nb, m, n), bf16),
        scratch_shapes=[pltpu.VMEM((mb, n) if nk > 1 else (8, 128), f32)],
        compiler_params=_cp(("arbitrary", "arbitrary", "arbitrary")))(*deps, lhs, rhs)


def mix_in_fwd(name, h, g, winT, cos2, sin2):
    tp, d = h.shape
    nin = winT.shape[0]
    tm = _row_tile(tp, 704)

    def body(h_ref, g_ref, w_ref, cos_ref, sin_ref, zq_ref, zg_ref, zu_ref, zgate_ref):
        hhat, _ = _rms(h_ref[...])
        z = _dot_nt((hhat * g_ref[...]).astype(bf16), w_ref[...])
        cosv, sinv = cos_ref[...], sin_ref[...]
        for hh in range(HEADS):
            qcols, kcols = slice(hh * HD, (hh + 1) * HD), slice(RW + hh * HD, RW + (hh + 1) * HD)
            zq_ref[:, qcols] = (_rot(z[:, qcols], cosv, sinv) * HD ** -0.5).astype(bf16)
            zq_ref[:, kcols] = _rot(z[:, kcols], cosv, sinv).astype(bf16)
        zq_ref[:, 2 * RW:] = z[:, 2 * RW:3 * RW].astype(bf16)
        zg_ref[...] = z[:, 3 * RW:4 * RW].astype(zg_ref.dtype)
        zu_ref[...] = z[:, 4 * RW:5 * RW]
        zgate_ref[...] = z[:, 5 * RW:].astype(zgate_ref.dtype)

    row = lambda w: pl.BlockSpec((tm, w), lambda i: (i, 0))
    widths = (3 * RW, RW, RW, 2 * d)
    return pl.pallas_call(
        body, name=name, grid=(tp // tm,),
        in_specs=[row(d), _resident((1, d)), _resident((nin, d)), row(HD), row(HD)],
        out_specs=[row(w) for w in widths],
        out_shape=[jax.ShapeDtypeStruct((tp, w), dt) for w, dt in zip(widths, (bf16, bf16, f32, bf16))],
        compiler_params=_cp(("arbitrary",)))(h, g, winT, cos2, sin2)


def mix_in_bwd(name, dq, dk, dv, dzg, dzu, dzgate, h, g, winT, dres):
    tp, d = h.shape
    nin = winT.shape[0]
    tm = _row_tile(tp)

    def body(dq_ref, dk_ref, dv_ref, dzg_ref, dzu_ref, dzgate_ref, h_ref, g_ref, w_ref, dres_ref, dh_ref, dz_ref, n_ref, dg_ref):
        dn, col = None, 0
        for piece in (dq_ref, dk_ref, dv_ref, dzg_ref, dzu_ref, dzgate_ref):
            v = piece[...]
            part = _dot_nn(v, w_ref[col:col + v.shape[1], :])
            dn = part if dn is None else dn + part
            dz_ref[:, col:col + v.shape[1]] = v
            col += v.shape[1]
        hhat, rs = _rms(h_ref[...])
        gv = g_ref[...]
        dh_ref[...] = dres_ref[...] + _rms_bwd(dn, gv, hhat, rs)

        @pl.when(pl.program_id(0) == 0)
        def _():
            dg_ref[...] = jnp.zeros_like(dg_ref)

        dg_ref[0:1, :] += jnp.sum(dn * hhat, axis=0, keepdims=True)
        n_ref[...] = (hhat * gv).astype(bf16)

    row = lambda w: pl.BlockSpec((tm, w), lambda i: (i, 0))
    return pl.pallas_call(
        body, name=name, grid=(tp // tm,),
        in_specs=[row(RW)] * 5 + [row(2 * d), row(d), _resident((1, d)), _resident((nin, d)), row(d)],
        out_specs=[row(d), pl.BlockSpec((None, tm, nin), lambda i: (0, i, 0)), pl.BlockSpec((None, tm, d), lambda i: (0, i, 0)),
                   pl.BlockSpec((8, d), lambda i: (0, 0))],
        out_shape=[jax.ShapeDtypeStruct((tp, d), f32), jax.ShapeDtypeStruct((1, tp, nin), bf16),
                   jax.ShapeDtypeStruct((1, tp, d), bf16), jax.ShapeDtypeStruct((8, d), f32)],
        compiler_params=_cp(("arbitrary",)))(dq, dk, dv, dzg, dzu, dzgate, h, g, winT, dres)


def _retention_tables(tp, pad, zero):
    half = HD // 2
    lane = jnp.arange(HD)
    inv_freq = ROPE_BASE ** (-(lane % half).astype(f32) / half)
    pos = jnp.arange(tp, dtype=f32) - pad + zero
    ang = pos[:, None] * inv_freq[None, :]
    cos2 = jnp.cos(ang)
    sin2 = jnp.where(lane[None, :] < half, -1.0, 1.0) * jnp.sin(ang)
    log_gamma = jnp.log1p(-(2.0 ** (-5.0 - jnp.arange(HEADS, dtype=f32))))
    idx = jnp.arange(CHUNK, dtype=f32)
    diff = idx[:, None] - idx[None, :]
    intra = jnp.where(diff[None] >= 0, jnp.exp(diff[None] * log_gamma[:, None, None]), 0.0)
    k_decay = jnp.exp((CHUNK - 1.0 - idx)[None, :] * log_gamma[:, None])
    q_decay = jnp.exp((idx + 1.0)[None, :] * log_gamma[:, None])
    chunk_decay = jnp.exp(CHUNK * log_gamma)
    full = (HEADS, CHUNK, HD)
    dec = jnp.stack([intra, jnp.broadcast_to(k_decay[:, :, None], full), jnp.broadcast_to(q_decay[:, :, None], full),
                     jnp.broadcast_to(chunk_decay[:, None, None], full)], axis=1)
    return cos2, sin2, dec


def _rot(t, cos2, sin2):
    return t * cos2 + pltpu.roll(t, HD // 2, 1) * sin2


def _rot_t(t, cos2, sin2):
    return t * cos2 - pltpu.roll(t, HD // 2, 1) * sin2


def _chunks_per_step(nch):
    return 3 if nch % 3 == 0 else 1


def retention_fwd(name, zq, dec):
    tp = zq.shape[0]
    nch = tp // CHUNK
    per = _chunks_per_step(nch)

    def body(q_ref, k_ref, v_ref, dec_ref, out_ref, st_ref, s_ref):
        @pl.when(pl.program_id(0) == 0)
        def _():
            s_ref[...] = jnp.zeros_like(s_ref)

        state = [s_ref[hh] for hh in range(HEADS)]
        for j in range(per):
            rows = slice(j * CHUNK, (j + 1) * CHUNK)
            for hh in range(HEADS):
                cols = slice(hh * HD, (hh + 1) * HD)
                qb, kb, vb = q_ref[rows, cols], k_ref[rows, cols], v_ref[rows, cols]
                sc = (_dot_nt(qb, kb) * dec_ref[hh, 0]).astype(bf16)
                sb = state[hh].astype(bf16)
                cross = _dot_nn((qb.astype(f32) * dec_ref[hh, 2]).astype(bf16), sb)
                out_ref[rows, cols] = (_dot_nn(sc, vb) + cross).astype(out_ref.dtype)
                st_ref[hh, j] = sb
                state[hh] = state[hh] * dec_ref[hh, 3] + _dot_tn((kb.astype(f32) * dec_ref[hh, 1]).astype(bf16), vb)
        for hh in range(HEADS):
            s_ref[hh] = state[hh]

    part = lambda j: pl.BlockSpec((per * CHUNK, RW), lambda n: (n, j))
    return pl.pallas_call(
        body, name=name, grid=(nch // per,),
        in_specs=[part(0), part(1), part(2), _resident((HEADS, 4, CHUNK, HD))],
        out_specs=[part(0), pl.BlockSpec((HEADS, per, HD, HD), lambda n: (0, n, 0, 0))],
        out_shape=[jax.ShapeDtypeStruct((tp, RW), bf16), jax.ShapeDtypeStruct((HEADS, nch, HD, HD), bf16)],
        scratch_shapes=[pltpu.VMEM((HEADS, HD, HD), f32)],
        compiler_params=_cp(("arbitrary",)))(zq, zq, zq, dec)


def retention_bwd(name, zq, cos2, sin2, dec, states, dout, pad, deps=()):
    tp = zq.shape[0]
    nch = tp // CHUNK
    per = _chunks_per_step(nch)
    nblk = nch // per
    scale = HD ** -0.5

    def body(q_ref, k_ref, v_ref, cos_ref, sin_ref, dec_ref, st_ref, do_ref, dq_ref, dk_ref, dv_ref, g_ref):
        @pl.when(pl.program_id(0) == 0)
        def _():
            g_ref[...] = jnp.zeros_like(g_ref)

        first_row = (nblk - 1 - pl.program_id(0)) * (per * CHUNK)
        gstate = [g_ref[hh] for hh in range(HEADS)]
        for j in reversed(range(per)):
            rows = slice(j * CHUNK, (j + 1) * CHUNK)
            cosv, sinv = cos_ref[rows, :], sin_ref[rows, :]
            keep = (lax.broadcasted_iota(jnp.int32, (CHUNK, HD), 0) + (first_row + j * CHUNK)) >= pad
            for hh in range(HEADS):
                cols = slice(hh * HD, (hh + 1) * HD)
                intra, kdec, qdec = dec_ref[hh, 0], dec_ref[hh, 1], dec_ref[hh, 2]
                qb, kb, vb = q_ref[rows, cols], k_ref[rows, cols], v_ref[rows, cols]
                qd = (qb.astype(f32) * qdec).astype(bf16)
                kd = (kb.astype(f32) * kdec).astype(bf16)
                sc = (_dot_nt(qb, kb) * intra).astype(bf16)
                dob = do_ref[rows, cols]
                sb = st_ref[hh, j]
                gb = gstate[hh].astype(bf16)
                dsc = (_dot_nt(dob, vb) * intra).astype(bf16)
                dv = _dot_tn(sc, dob) + _dot_nn(kd, gb)
                dqr = _dot_nn(dsc, kb) + _dot_nt(dob, sb) * qdec
                dkr = _dot_tn(dsc, qb) + _dot_nt(vb, gb) * kdec
                gstate[hh] = gstate[hh] * dec_ref[hh, 3] + _dot_tn(qd, dob)
                dq_ref[rows, cols] = jnp.where(keep, _rot_t(dqr * scale, cosv, sinv), 0.0).astype(bf16)
                dk_ref[rows, cols] = jnp.where(keep, _rot_t(dkr, cosv, sinv), 0.0).astype(bf16)
                dv_ref[rows, cols] = jnp.where(keep, dv, 0.0).astype(bf16)
        for hh in range(HEADS):
            g_ref[hh] = gstate[hh]

    part = lambda j: pl.BlockSpec((per * CHUNK, RW), lambda t: (nblk - 1 - t, j))
    table = pl.BlockSpec((per * CHUNK, HD), lambda t: (nblk - 1 - t, 0))
    return pl.pallas_call(
        _skip(len(deps), body), name=name, grid=(nblk,),
        in_specs=[_ANY] * len(deps) + [part(0), part(1), part(2), table, table, _resident((HEADS, 4, CHUNK, HD)),
                                       pl.BlockSpec((HEADS, per, HD, HD), lambda t: (0, nblk - 1 - t, 0, 0)), part(0)],
        out_specs=[part(0)] * 3,
        out_shape=[jax.ShapeDtypeStruct((tp, RW), bf16)] * 3,
        scratch_shapes=[pltpu.VMEM((HEADS, HD, HD), f32)],
        compiler_params=_cp(("arbitrary",)))(*deps, zq, zq, zq, cos2, sin2, dec, states, dout)


def _window_sum(xv, steps, tp, forward):
    s = xv
    for j in range(steps):
        sh = 2 ** j
        s = s + pltpu.roll(s, (tp - sh) if forward else sh, 0)
    return s


def pool_fwd(name, zu, maps, scale, pad):
    tp = zu.shape[0]

    def body(u_ref, maps_ref, scale_ref, pooled_ref, p_ref):
        row = lax.broadcasted_iota(jnp.int32, (tp, HD), 0)
        for gi, w in enumerate(POOL_WINDOWS):
            cols = slice(gi * HD, (gi + 1) * HD)
            xv = u_ref[:, cols]
            cnt = jnp.clip(row - (pad - 1), 1, w).astype(f32)
            pooled = jnp.where(row >= pad, _window_sum(xv, gi + 1, tp, False) / cnt - xv, 0.0).astype(bf16)
            pooled_ref[:, cols] = pooled
            p_ref[:, cols] = (_dot_nn(pooled, maps_ref[gi].astype(bf16)) * scale_ref[:, cols]).astype(bf16)

    return pl.pallas_call(
        body, name=name,
        out_shape=[jax.ShapeDtypeStruct((tp, RW), bf16), jax.ShapeDtypeStruct((tp, RW), bf16)],
        compiler_params=_cp())(zu, maps, scale)


def pool_bwd(name, dp, pooled, maps, scale, pad):
    tp = dp.shape[0]

    def body(dp_ref, pooled_ref, maps_ref, scale_ref, du_ref, dmaps_ref, dscale_ref):
        row = lax.broadcasted_iota(jnp.int32, (tp, HD), 0)
        dscale_ref[...] = jnp.zeros_like(dscale_ref)
        for gi, w in enumerate(POOL_WINDOWS):
            cols = slice(gi * HD, (gi + 1) * HD)
            mb = maps_ref[gi].astype(bf16)
            pooled = pooled_ref[:, cols]
            dpf = dp_ref[:, cols].astype(f32)
            dscale_ref[0:1, cols] = jnp.sum(dpf * _dot_nn(pooled, mb), axis=0, keepdims=True)
            dpm = (dpf * scale_ref[:, cols]).astype(bf16)
            dmaps_ref[gi * HD:(gi + 1) * HD, :] = _dot_tn(pooled, dpm)
            dpool = jnp.where(row >= pad, _dot_nt(dpm, mb), 0.0)
            cnt = jnp.clip(row - (pad - 1), 1, w).astype(f32)
            du = _window_sum(dpool / cnt, gi + 1, tp, True) - dpool
            du_ref[:, cols] = jnp.where(row >= pad, du, 0.0).astype(bf16)

    return pl.pallas_call(
        body, name=name,
        out_shape=[jax.ShapeDtypeStruct((tp, RW), bf16), jax.ShapeDtypeStruct((HEADS * HD, HD), f32),
                   jax.ShapeDtypeStruct((8, RW), f32)],
        compiler_params=_cp())(dp, pooled, maps, scale)


def _group_norm(o):
    mu = jnp.mean(o, axis=-1, keepdims=True)
    oc = o - mu
    rstd = lax.rsqrt(jnp.mean(oc * oc, axis=-1, keepdims=True) + EPS)
    return oc * rstd, rstd


def mix_out_fwd(name, h, oraw, zg, zgate, p, wretT, wpoolT, wout, deps=()):
    tp, d = h.shape
    tm = _row_tile(tp, 704)

    def body(h_ref, o_ref, zg_ref, zgate_ref, p_ref, wr_ref, wp_ref, wo_ref, ho_ref, rp_ref, mixed_ref):
        parts = []
        for hh in range(HEADS):
            cols = slice(hh * HD, (hh + 1) * HD)
            rhat, _ = _group_norm(o_ref[:, cols].astype(f32))
            gv = zg_ref[:, cols].astype(f32)
            parts.append(rhat * (gv * _sigmoid(gv)))
        r = jnp.concatenate(parts, axis=-1).astype(bf16)
        pv = p_ref[...]
        ret = _dot_nt(r, wr_ref[...])
        pool = _dot_nt(pv, wp_ref[...])
        mixed = (_sigmoid(zgate_ref[:, :d].astype(f32)) * ret + _sigmoid(zgate_ref[:, d:].astype(f32)) * pool).astype(bf16)
        ho_ref[...] = h_ref[...] + _dot_nn(mixed, wo_ref[...])
        rp_ref[0] = r
        rp_ref[1] = pv
        mixed_ref[...] = mixed

    row = lambda w: pl.BlockSpec((tm, w), lambda i: (i, 0))
    return pl.pallas_call(
        _skip(len(deps), body), name=name, grid=(tp // tm,),
        in_specs=[_ANY] * len(deps) + [row(d), row(RW), row(RW), row(2 * d), row(RW), _resident((d, RW)), _resident((d, RW)),
                                       _resident((d, d))],
        out_specs=[row(d), pl.BlockSpec((2, tm, RW), lambda i: (0, i, 0)), pl.BlockSpec((None, tm, d), lambda i: (0, i, 0))],
        out_shape=[jax.ShapeDtypeStruct((tp, d), f32), jax.ShapeDtypeStruct((2, tp, RW), bf16),
                   jax.ShapeDtypeStruct((1, tp, d), bf16)],
        compiler_params=_cp(("arbitrary",)))(*deps, h, oraw, zg, zgate, p, wretT, wpoolT, wout)


def mix_out_bwd(name, dy, oraw, zg, zgate, rp, wretT, wpoolT, wout, deps=()):
    tp, d = dy.shape
    tm = _row_tile(tp, 704)

    def body(dy_ref, o_ref, zg_ref, zgate_ref, rp_ref, wr_ref, wp_ref, wo_ref,
             do_ref, dzg_ref, dzgate_ref, dp_ref, drp_ref, dyb_ref):
        dyb = dy_ref[...].astype(bf16)
        dmixed = _dot_nt(dyb, wo_ref[...])
        sa = _sigmoid(zgate_ref[:, :d].astype(f32))
        sb = _sigmoid(zgate_ref[:, d:].astype(f32))
        dret = dmixed * sa
        dpool = dmixed * sb
        dzgate_ref[:, :d] = (dret * _dot_nt(rp_ref[0], wr_ref[...]) * (1.0 - sa)).astype(bf16)
        dzgate_ref[:, d:] = (dpool * _dot_nt(rp_ref[1], wp_ref[...]) * (1.0 - sb)).astype(bf16)
        dretb, dpoolb = dret.astype(bf16), dpool.astype(bf16)
        dr = _dot_nn(dretb, wr_ref[...])
        dp_ref[...] = _dot_nn(dpoolb, wp_ref[...]).astype(bf16)
        for hh in range(HEADS):
            cols = slice(hh * HD, (hh + 1) * HD)
            rhat, rstd = _group_norm(o_ref[:, cols].astype(f32))
            gv = zg_ref[:, cols].astype(f32)
            sg = _sigmoid(gv)
            drh = dr[:, cols]
            drhat = drh * (gv * sg)
            dzg_ref[:, cols] = (drh * rhat * (sg * (1.0 + gv * (1.0 - sg)))).astype(bf16)
            do = rstd * (drhat - jnp.mean(drhat, axis=-1, keepdims=True)
                         - rhat * jnp.mean(drhat * rhat, axis=-1, keepdims=True))
            do_ref[:, cols] = do.astype(bf16)
        drp_ref[0] = dretb
        drp_ref[1] = dpoolb
        dyb_ref[...] = dyb

    row = lambda w: pl.BlockSpec((tm, w), lambda i: (i, 0))
    return pl.pallas_call(
        _skip(len(deps), body), name=name, grid=(tp // tm,),
        in_specs=[_ANY] * len(deps) + [row(d), row(RW), row(RW), row(2 * d), pl.BlockSpec((2, tm, RW), lambda i: (0, i, 0)),
                                       _resident((d, RW)), _resident((d, RW)), _resident((d, d))],
        out_specs=[row(RW), row(RW), row(2 * d), row(RW), pl.BlockSpec((2, tm, d), lambda i: (0, i, 0)),
                   pl.BlockSpec((None, tm, d), lambda i: (0, i, 0))],
        out_shape=[jax.ShapeDtypeStruct((tp, RW), bf16), jax.ShapeDtypeStruct((tp, RW), bf16),
                   jax.ShapeDtypeStruct((tp, 2 * d), bf16), jax.ShapeDtypeStruct((tp, RW), bf16),
                   jax.ShapeDtypeStruct((2, tp, d), bf16), jax.ShapeDtypeStruct((1, tp, d), bf16)],
        compiler_params=_cp(("arbitrary",)))(*deps, dy, oraw, zg, zgate, rp, wretT, wpoolT, wout)


def final_loss(name, h, g, target):
    tp, d = h.shape
    tm = _row_tile(tp)
    nsub = tm // CHUNK

    def body(h_ref, g_ref, *rest):
        t_refs = rest[:nsub]
        dh_ref, loss_ref, dg_ref = rest[nsub:]
        i = pl.program_id(0)

        @pl.when(i == 0)
        def _():
            loss_ref[...] = jnp.zeros_like(loss_ref)
            dg_ref[...] = jnp.zeros_like(dg_ref)

        gv = g_ref[...]
        for j in range(nsub):
            rows = slice(j * CHUNK, (j + 1) * CHUNK)
            hhat, rs = _rms(h_ref[rows, :])
            err = jnp.where(i * nsub + j >= 1, hhat * gv - t_refs[j][...], 0.0)
            dyv = err / d
            dh_ref[rows, :] = _rms_bwd(dyv, gv, hhat, rs)
            loss_ref[...] += 0.5 * jnp.sum(jnp.sum(err * err, axis=-1, keepdims=True) / d)
            dg_ref[0:1, :] += jnp.sum(dyv * hhat, axis=0, keepdims=True)

    lagged = lambda j: pl.BlockSpec((CHUNK, d), lambda i: (jnp.maximum(i * nsub + j - 1, 0), 0))
    return pl.pallas_call(
        body, name=name, grid=(tp // tm,),
        in_specs=[pl.BlockSpec((tm, d), lambda i: (i, 0)), _resident((1, d))] + [lagged(j) for j in range(nsub)],
        out_specs=[pl.BlockSpec((tm, d), lambda i: (i, 0)), pl.BlockSpec((8, 128), lambda i: (0, 0)),
                   pl.BlockSpec((8, d), lambda i: (0, 0))],
        out_shape=[jax.ShapeDtypeStruct((tp, d), f32), jax.ShapeDtypeStruct((8, 128), f32),
                   jax.ShapeDtypeStruct((8, d), f32)],
        compiler_params=_cp(("arbitrary",)))(h, g, *[target] * nsub)


def _adamw(w, g, m, v):
    m = ADAM_B1 * m + (1.0 - ADAM_B1) * g
    v = ADAM_B2 * v + (1.0 - ADAM_B2) * (g * g)
    m_hat = m / (1.0 - ADAM_B1 ** ADAM_STEP)
    v_hat = v / (1.0 - ADAM_B2 ** ADAM_STEP)
    delta = -ADAM_LR * (m_hat / (jnp.sqrt(v_hat) + ADAM_EPS) + ADAM_WD * w)
    return delta, m, v


def adam_big(name, me, recv, own, b, layer, transposed, w, m, v, prev):
    r, c = recv.shape[2:]
    wshape = w.shape[1:]
    nchunk = 1 if transposed else next(k for k in (4, 2, 1) if r % (16 * k) == 0)
    rc = r // nchunk

    def body(me_ref, recv_ref, own_ref, w_ref, m_ref, v_ref, *rest):
        g_ref, d_ref, nm_ref, nv_ref = rest[-4:]
        g = own_ref[...].astype(f32)
        for j in range(NDEV - 1):
            g = g + recv_ref[j].astype(f32)
        if transposed:
            g = g.T
        delta, nm, nv = _adamw(w_ref[...], g, m_ref[...], v_ref[...])
        g_ref[...] = g
        d_ref[...] = delta
        nm_ref[...] = nm
        nv_ref[...] = nv

    wblock = wshape if transposed else (rc, c)
    wspec = pl.BlockSpec((None,) + wblock, lambda i, me_ref: (layer, i, 0))
    in_specs = [pl.BlockSpec((NDEV - 1, None, rc, c), lambda i, me_ref: (0, b, i, 0)),
                pl.BlockSpec((None, None, rc, c), lambda i, me_ref: (b, me_ref[0], i, 0)), wspec, wspec, wspec]
    args = [recv, own, w, m, v]
    aliases = {}
    if prev is not None:
        in_specs += [_ANY] * 4
        args += list(prev)
        aliases = {6 + k: k for k in range(4)}
    return pl.pallas_call(
        body, name=name,
        grid_spec=pltpu.PrefetchScalarGridSpec(num_scalar_prefetch=1, grid=(nchunk,), in_specs=in_specs,
                                               out_specs=[wspec] * 4),
        out_shape=[jax.ShapeDtypeStruct(w.shape, f32)] * 4, input_output_aliases=aliases,
        compiler_params=_cp())(me, *args)


def adam_small(name, ga0, gmaps0, gmeta, ga1, gmaps1, norms, pool_scale, pool_maps, meta, final_norm, d):
    def body(ga0_ref, gmaps0_ref, gmeta_ref, ga1_ref, gmaps1_ref, *refs):
        ins, outs = refs[:21], refs[21:]
        x, y, c = _me()
        me = 4 * x + 2 * y + c

        def total(ref, rows):
            t = ref[0, rows, :]
            for j in range(1, NDEV):
                t = t + ref[j, rows, :]
            return t

        row = lambda r: slice(r, r + 1)
        outs[0][...] = jnp.broadcast_to(total(ga1_ref, row(0))[:, :128], (8, 128))

        def update(k, g, o):
            w_ref, m_ref, v_ref = ins[3 * k:3 * k + 3]
            delta, nm, nv = _adamw(w_ref[...], g, m_ref[...], v_ref[...])
            for ref, val in zip(outs[o:o + 4], (g, delta, nm, nv)):
                ref[...] = val

        two = lax.broadcasted_iota(jnp.int32, (2, d), 0)
        for k in range(3):
            update(k, jnp.where(two == 0, total(ga0_ref, row(k)), total(ga1_ref, row(2 + k))), 1 + 4 * k)
        update(3, jnp.where(two[:, :RW] == 0, total(ga0_ref, row(3))[:, :RW], total(ga1_ref, row(5))[:, :RW]), 13)
        update(4, jnp.concatenate([total(gmaps0_ref, slice(None)), total(gmaps1_ref, slice(None))], axis=0), 17)
        update(5, total(gmeta_ref, pl.ds(pl.multiple_of(me * N_META, N_META), N_META)), 21)
        update(6, total(ga1_ref, row(1)), 25)

    flat = []
    for trip in (*norms, pool_scale, pool_maps, meta, final_norm):
        flat += list(trip)
    out_shapes = [jax.ShapeDtypeStruct((8, 128), f32)]
    for trip in (*norms, pool_scale, pool_maps, meta, final_norm):
        out_shapes += [jax.ShapeDtypeStruct(trip[0].shape, f32)] * 4
    return pl.pallas_call(body, name=name, out_shape=out_shapes,
                          compiler_params=_cp())(ga0, gmaps0, gmeta, ga1, gmaps1, *flat)


def kernel(x, meta, ffn1_norm, ffn1_gate, ffn1_up, ffn1_down, mix_norm, w_in, pool_maps, pool_scale, w_ret_up, w_pool_up, w_out, ffn2_norm, ffn2_gate, ffn2_up, ffn2_down, final_norm, loss_target, m_meta, m_ffn1_norm, m_ffn1_gate, m_ffn1_up, m_ffn1_down, m_mix_norm, m_w_in, m_pool_maps, m_pool_scale, m_w_ret_up, m_w_pool_up, m_w_out, m_ffn2_norm, m_ffn2_gate, m_ffn2_up, m_ffn2_down, m_final_norm, v_meta, v_ffn1_norm, v_ffn1_gate, v_ffn1_up, v_ffn1_down, v_mix_norm, v_w_in, v_pool_maps, v_pool_scale, v_w_ret_up, v_w_pool_up, v_w_out, v_ffn2_norm, v_ffn2_gate, v_ffn2_up, v_ffn2_down, v_final_norm):
    seq, d = x.shape[1], x.shape[2]
    depth = ffn1_gate.shape[0]
    ff = ffn1_gate.shape[2] * NDEV
    nin = w_in.shape[2] * NDEV
    length = seq + N_META
    pad = (-length) % CHUNK
    tp = length + pad
    assert pad % 8 == 0 and pad + N_META == CHUNK and depth == 2 and nin == 5 * RW + 2 * d

    ix, iy, ic = _me()
    me = (4 * ix + 2 * iy + ic).astype(jnp.int32).reshape(1)

    meta_all, = all_gather("gather_meta", [meta])
    meta_full = jnp.transpose(meta_all, (1, 0, 2)).reshape(N_META, d)

    token = meta_all
    tview = lambda *arrs: [jnp.swapaxes(a, 1, 2) for a in arrs]
    t_g1, t_u1, t_g2, t_u2, t_in = (tview(w, m, v) for w, m, v in (
        (ffn1_gate, m_ffn1_gate, v_ffn1_gate), (ffn1_up, m_ffn1_up, v_ffn1_up), (ffn2_gate, m_ffn2_gate, v_ffn2_gate),
        (ffn2_up, m_ffn2_up, v_ffn2_up), (w_in, m_w_in, v_w_in)))
    keys, groups = [], []
    for layer in range(depth):
        lands = prep_layer(layer, me, [w_ret_up, w_pool_up],
                           [t_g1[0], t_u1[0], t_g2[0], t_u2[0], t_in[0], ffn1_down, ffn2_down, w_out],
                           (token,) if layer else ())
        wretT, wpoolT, g1T, u1T, g2T, u2T, winT, d1, d2, wout = lands
        keys += [("ffn1", layer), ("mix", layer), ("ffn2", layer)]
        groups += [[g1T, u1T, d1], [winT, wretT, wpoolT, wout], [g2T, u2T, d2]]
        if layer == 0:
            first, token = gather_start_chips("gather_start_first", groups[:1], (token,))
    second, token = gather_start_chips("gather_start_second", groups[1:2], (token,))
    gathers = dict(zip(keys, first + second))

    def forward(part, layer, after):
        ssem, rsem, group = gathers[(part, layer)]
        ssem, rsem, group, tok = gather_forward(f"gather_forward_{part}{layer}", ssem, rsem, group, after)
        gathers[(part, layer)] = (ssem, rsem, group)
        return tok

    def gathered(part, layer, after):
        ssem, rsem, group = gathers[(part, layer)]
        _, full = copies_wait(f"gather_wait_{part}{layer}", ssem, rsem, (), group, after, 3)
        return [a.reshape((NDEV * a.shape[1],) + a.shape[2:]) for a in full]

    cos2, sin2, dec = _retention_tables(tp, pad, token[0, 0])
    h = jnp.concatenate([jnp.zeros((pad, d), f32), meta_full + (token[0, 0] + 0.0 * cos2[0, 0] + 0.0 * dec[0, 0, 0, 0]), x[0]],
                        axis=0)

    saved = []
    weights = []
    tok = forward("ffn1", 0, h)
    rest, tok = gather_start_chips("gather_start_rest", groups[2:], (tok,))
    gathers.update(zip(keys[2:], rest))
    for layer in range(depth):
        row = lambda a: a[layer:layer + 1]
        s = {"h0": h}
        g1T, u1T, d1 = gathered("ffn1", layer, tok if layer == 0 else h)
        tok = forward("mix", layer, h) if layer else None
        h, s["a1"], s["b1"] = ffn_fwd(f"ffn1_fwd{layer}", h, row(ffn1_norm), g1T, u1T, d1, (tok,) if layer else ())
        s["h1"] = h
        if layer == 0:
            tok = forward("mix", layer, h)
        winT, wretT, wpoolT, wout = gathered("mix", layer, tok if layer == 0 else h)
        s["zq"], s["zg"], zu, s["zgate"] = mix_in_fwd(f"mix_in_fwd{layer}", h, row(mix_norm), winT, cos2, sin2)
        s["oraw"], s["states"] = retention_fwd(f"retention_fwd{layer}", s["zq"], dec)
        s["pooled"], p = pool_fwd(f"pool_fwd{layer}", zu, pool_maps[layer], row(pool_scale), pad)
        tok = forward("ffn2", layer, p)
        h, s["rp"], s["mixed"] = mix_out_fwd(
            f"mix_out_fwd{layer}", h, s["oraw"], s["zg"], s["zgate"], p, wretT, wpoolT, wout, (tok,))
        s["h2"] = h
        g2T, u2T, d2 = gathered("ffn2", layer, h)
        tok = (forward("ffn1", layer + 1, h),) if layer + 1 < depth else ()
        h, s["a2"], s["b2"] = ffn_fwd(f"ffn2_fwd{layer}", h, row(ffn2_norm), g2T, u2T, d2, tok)
        saved.append(s)
        weights.append((g1T, u1T, g2T, u2T, winT, wretT, wpoolT, d1, d2, wout))

    dh, loss_part, dg_final = final_loss("final_loss", h, final_norm.reshape(1, d), loss_target[0])

    small = {}
    small_gathers = {}
    exchanges = {}
    token = None

    def rows8(vals):
        at = lax.broadcasted_iota(jnp.int32, (8, d), 0)
        out = jnp.zeros((8, d), f32)
        for k, v in enumerate(vals):
            r0 = v[0:1]
            r0 = r0 if r0.shape[1] == d else jnp.pad(r0, ((0, 0), (0, d - r0.shape[1])))
            out = jnp.where(at == k, r0, out)
        return out

    def exchange(part, layer, grads):
        by_dest = [g.reshape(g.shape[0], NDEV, g.shape[1] // NDEV, g.shape[2]) for g in grads]
        ssem, rsem, sent, lands, tok = exchange_start(f"exchange_start_{part}{layer}", by_dest)
        exchanges[(part, layer)] = (ssem, rsem, sent, lands)
        return (tok,)

    for layer in reversed(range(depth)):
        g1T, u1T, g2T, u2T, winT, wretT, wpoolT, d1, d2, wout = weights[layer]
        row = lambda a: a[layer:layer + 1]
        s = saved[layer]
        dh, lhs2, rhs2, small[("ffn2", layer)] = ffn_bwd(
            f"ffn2_bwd{layer}", dh, s["h2"], row(ffn2_norm), s["a2"], s["b2"], g2T, u2T, d2, () if token is None else token)
        token = exchange("ffn2", layer, [mm_tn(f"ffn2_wgrad{layer}", lhs2, rhs2, lambda b: b // 2)])
        do, dzg, dzgate, dp, drp, dyb = mix_out_bwd(
            f"mix_out_bwd{layer}", dh, s["oraw"], s["zg"], s["zgate"], s["rp"], wretT, wpoolT, wout, token)
        gw_mix = [mm_tn(f"w_out_wgrad{layer}", s["mixed"], dyb, lambda b: b),
                  mm_tn(f"up_wgrad{layer}", drp, s["rp"], lambda b: b)]
        if layer == 0:
            token = exchange("mix_up", layer, gw_mix)
            gw_mix = []
        dq, dk, dv = retention_bwd(f"retention_bwd{layer}", s["zq"], cos2, sin2, dec, s["states"], do, pad, token)
        dzu, small[("maps", layer)], small[("scale", layer)] = pool_bwd(
            f"pool_bwd{layer}", dp, s["pooled"], pool_maps[layer], row(pool_scale), pad)
        dh, dz, n2, small[("mix", layer)] = mix_in_bwd(
            f"mix_in_bwd{layer}", dq, dk, dv, dzg, dzu, dzgate, s["h1"], row(mix_norm), winT, dh)
        token = exchange("mix" if layer else "w_in", layer, gw_mix + [mm_tn(f"w_in_wgrad{layer}", dz, n2, lambda b: b)])
        dh, lhs1, rhs1, small[("ffn1", layer)] = ffn_bwd(
            f"ffn1_bwd{layer}", dh, s["h0"], row(ffn1_norm), s["a1"], s["b1"], g1T, u1T, d1, token)
        rows = [small[("ffn1", layer)], small[("mix", layer)], small[("ffn2", layer)], small[("scale", layer)]]
        packs = [rows8([loss_part, dg_final] + rows if layer == depth - 1 else rows), small[("maps", layer)]]
        if layer == 0:
            dmeta = dh[pad:CHUNK]
            packs.append(jnp.transpose(dmeta.reshape(N_META, NDEV, d // NDEV), (1, 0, 2)).reshape(NDEV * N_META, d // NDEV))
        ssem, rsem, lands, tok = gather_start(f"small_start{layer}", slot_in(f"small_slot{layer}", me, packs))
        small_gathers[layer] = (ssem, rsem, lands)
        if layer:
            token = exchange("ffn1", layer, [mm_tn(f"ffn1_wgrad{layer}", lhs1, rhs1, lambda b: b // 2, (tok,))])
        else:
            token = (tok,)
            for j, nm in enumerate(("ffn1_gate", "ffn1_up", "ffn1_down")):
                token = exchange(nm, layer, [mm_tn(f"{nm}_wgrad{layer}", lhs1, rhs1, lambda b: b // 2, token, only=j)])

    grad_x = (dh[CHUNK:] + token[0][0, 0])[None]

    big = {}
    after = token[0]
    plans = {
        "ffn2": [("ffn2_gate", 0, 0, False, *t_g2), ("ffn2_up", 0, 1, False, *t_u2),
                 ("ffn2_down", 0, 2, False, ffn2_down, m_ffn2_down, v_ffn2_down)],
        "mix": [("w_out", 0, 0, False, w_out, m_w_out, v_w_out),
                ("w_ret_up", 1, 0, True, w_ret_up, m_w_ret_up, v_w_ret_up),
                ("w_pool_up", 1, 1, True, w_pool_up, m_w_pool_up, v_w_pool_up), ("w_in", 2, 0, False, *t_in)],
        "ffn1": [("ffn1_gate", 0, 0, False, *t_g1), ("ffn1_up", 0, 1, False, *t_u1),
                 ("ffn1_down", 0, 2, False, ffn1_down, m_ffn1_down, v_ffn1_down)]}
    for nm, k, b, tr, w, m, v in plans["ffn1"]:
        plans[nm] = [(nm, 0, 0, tr, w, m, v)]
    plans["mix_up"], plans["w_in"] = plans["mix"][:3], [("w_in", 0, 0, False, *t_in)]
    for layer in reversed(range(depth)):
        for part in ("ffn2", "mix", "ffn1") if layer else ("ffn2", "mix_up", "w_in", "ffn1_gate", "ffn1_up", "ffn1_down"):
            ssem, rsem, sent, lands = exchanges[(part, layer)]
            sent, lands = copies_wait(f"exchange_wait_{part}{layer}", ssem, rsem, sent, lands, after)
            for nm, k, b, tr, w, m, v in plans[part]:
                big[nm] = adam_big(f"adam_{nm}{layer}", me, lands[k], sent[k], b, layer, tr, w, m, v, big.get(nm))
                after = big[nm][0]

    gsmall = []
    for layer in range(depth):
        ssem, rsem, lands = small_gathers[layer]
        gsmall += copies_wait(f"small_wait{layer}", ssem, rsem, (), lands, after)[1]

    maps2 = lambda a: a.reshape(depth * HEADS * HD, HD)
    res = adam_small(
        "adam_small", *gsmall,
        [(ffn1_norm, m_ffn1_norm, v_ffn1_norm), (mix_norm, m_mix_norm, v_mix_norm), (ffn2_norm, m_ffn2_norm, v_ffn2_norm)],
        (pool_scale, m_pool_scale, v_pool_scale), (maps2(pool_maps), maps2(m_pool_maps), maps2(v_pool_maps)),
        (meta, m_meta, v_meta), tuple(a.reshape(1, d) for a in (final_norm, m_final_norm, v_final_norm)), d)
    loss = res[0][0, 0]
    sm = {}
    for k, nm in enumerate(["ffn1_norm", "mix_norm", "ffn2_norm", "pool_scale", "pool_maps", "meta", "final_norm"]):
        sm[nm] = list(res[1 + 4 * k:5 + 4 * k])
    sm["pool_maps"] = [a.reshape(pool_maps.shape) for a in sm["pool_maps"]]
    sm["final_norm"] = [a.reshape(d) for a in sm["final_norm"]]

    names = ["meta", "ffn1_norm", "ffn1_gate", "ffn1_up", "ffn1_down", "mix_norm", "w_in", "pool_maps", "pool_scale",
             "w_ret_up", "w_pool_up", "w_out", "ffn2_norm", "ffn2_gate", "ffn2_up", "ffn2_down", "final_norm"]
    for nm in ("ffn1_gate", "ffn1_up", "ffn2_gate", "ffn2_up", "w_in"):
        big[nm] = tview(*big[nm])
    allw = {**{k: list(v) for k, v in big.items()}, **sm}
    outs = [loss, grad_x]
    for kind in range(4):
        outs += [allw[nm][kind] for nm in names]
    return tuple(outs)
```

```python
import functools

import jax
import jax.numpy as jnp
from jax import lax
from jax.experimental import pallas as pl
from jax.experimental.pallas import tpu as pltpu

f32 = jnp.float32
bf16 = jnp.bfloat16
MESH = pl.DeviceIdType.MESH
NDEV = 8
N_META = 16
HEADS = 4
HD = 128
CHUNK = 128
RW = HEADS * HD
POOL_WINDOWS = (2, 4, 8, 16)
ROPE_BASE = 10000.0
EPS = 1e-6
ADAM_LR = 0.001
ADAM_B1 = 0.9
ADAM_B2 = 0.999
ADAM_EPS = 1e-08
ADAM_WD = 0.01
ADAM_STEP = 10
VMEM_CAP_MB = 60


def _cp(sem=None):
    return pltpu.CompilerParams(vmem_limit_bytes=VMEM_CAP_MB * 2**20, dimension_semantics=sem)


def _row_tile(tp, want=384):
    return want if tp % want == 0 else 128


def _resident(shape):
    nd = len(shape)
    return pl.BlockSpec(shape, lambda *_: (0,) * nd, pipeline_mode=pl.Buffered(1))


def _skip(nd, body):
    return (lambda *refs: body(*refs[nd:])) if nd else body


def _dot_nn(a, b):
    return lax.dot_general(a, b, (((1,), (0,)), ((), ())), preferred_element_type=f32)


def _dot_nt(a, b):
    return lax.dot_general(a, b, (((1,), (1,)), ((), ())), preferred_element_type=f32)


def _dot_tn(a, b):
    return lax.dot_general(a, b, (((0,), (0,)), ((), ())), preferred_element_type=f32)


def _rms(h):
    rs = lax.rsqrt(jnp.mean(h * h, axis=-1, keepdims=True) + EPS)
    return h * rs, rs


def _rms_bwd(dn, g, hhat, rs):
    dhh = dn * g
    return rs * (dhh - hhat * jnp.mean(dhh * hhat, axis=-1, keepdims=True))


def _sigmoid(x):
    return jax.nn.sigmoid(x)


def _me():
    return lax.axis_index("x"), lax.axis_index("y"), lax.axis_index("c")


def _peer(idx):
    return (idx // 4, (idx // 2) % 2, idx % 2)


def all_gather(name, arrays):
    n = len(arrays)

    def body(*refs):
        ins, outs = refs[:n], refs[n:2 * n]
        send_sems, recv_sems, local_sems = refs[2 * n:]
        x, y, c = _me()
        me = 4 * x + 2 * y + c
        locals_ = []
        for k in range(n):
            cp = pltpu.make_async_copy(ins[k], outs[k].at[me], local_sems.at[k])
            cp.start()
            locals_.append(cp)
        for d in range(1, NDEV):
            for k in range(n):
                pltpu.make_async_remote_copy(
                    src_ref=ins[k], dst_ref=outs[k].at[me], send_sem=send_sems.at[k], recv_sem=recv_sems.at[k],
                    device_id=_peer((me + d) % NDEV), device_id_type=MESH).start()
        for k in range(n):
            seven = outs[k].at[pl.ds(0, NDEV - 1)]
            w = pltpu.make_async_remote_copy(src_ref=seven, dst_ref=seven, send_sem=send_sems.at[k],
                                             recv_sem=recv_sems.at[k], device_id=(x, y, c), device_id_type=MESH)
            w.wait_send()
            w.wait_recv()
            locals_[k].wait()

    anyspec = pl.BlockSpec(memory_space=pl.ANY)
    return pl.pallas_call(
        body, name=name,
        out_shape=[jax.ShapeDtypeStruct((NDEV,) + a.shape, a.dtype) for a in arrays],
        in_specs=[anyspec] * n, out_specs=[anyspec] * n,
        scratch_shapes=[pltpu.SemaphoreType.DMA((n,)), pltpu.SemaphoreType.DMA((n,)), pltpu.SemaphoreType.DMA((n,))],
    )(*arrays)


_HBM = pl.BlockSpec(memory_space=pltpu.HBM)
_SEM = pl.BlockSpec(memory_space=pltpu.SEMAPHORE)
_ANY = pl.BlockSpec(memory_space=pl.ANY)
_EFFECT = pltpu.SideEffectType.DATAFLOW_SIDE_EFFECTING


def _in_hbm(a):
    return pltpu.with_memory_space_constraint(a, pltpu.HBM)


def gather_start(name, lands, deps=()):
    n, nd = len(lands), len(deps)

    def body(*refs):
        land = refs[nd:nd + n]
        send_sems, recv_sems = refs[nd + n:nd + n + 2]
        token = refs[-1]
        x, y, c = _me()
        me = 4 * x + 2 * y + c
        for d in range(1, NDEV):
            for k in range(n):
                pltpu.make_async_remote_copy(
                    src_ref=land[k].at[me], dst_ref=land[k].at[me], send_sem=send_sems.at[k], recv_sem=recv_sems.at[k],
                    device_id=_peer((me + d) % NDEV), device_id_type=MESH).start()
        token[...] = jnp.zeros_like(token)

    res = pl.pallas_call(
        body, name=name,
        out_shape=(pltpu.SemaphoreType.DMA((n,)), pltpu.SemaphoreType.DMA((n,)),
                   *[pltpu.HBM(a.shape, a.dtype) for a in lands], jax.ShapeDtypeStruct((8, 128), f32)),
        in_specs=[_ANY] * nd + [_HBM] * n,
        out_specs=(_SEM, _SEM, *[_HBM] * n, pl.BlockSpec(memory_space=pltpu.VMEM)),
        input_output_aliases={nd + k: 2 + k for k in range(n)},
        compiler_params=pltpu.CompilerParams(has_side_effects=_EFFECT),
    )(*deps, *[_in_hbm(a) for a in lands])
    return res[0], res[1], list(res[2:2 + n]), res[-1]


def _other_chips(x, y):
    return [(1 - x, y), (x, 1 - y), (1 - x, 1 - y)]


def gather_start_chips(name, groups, deps=()):
    sizes = [len(g) for g in groups]
    lands = [a for g in groups for a in g]
    n, nd, ng = len(lands), len(deps), len(groups)

    def body(*refs):
        land = refs[nd:nd + n]
        sems = refs[nd + n:nd + n + 2 * ng]
        token = refs[-1]
        x, y, c = _me()
        me = 4 * x + 2 * y + c
        k = 0
        for g, size in enumerate(sizes):
            for j in range(size):
                for to in [(x, y, 1 - c)] + [(cx, cy, c) for cx, cy in _other_chips(x, y)]:
                    pltpu.make_async_remote_copy(
                        src_ref=land[k].at[me], dst_ref=land[k].at[me], send_sem=sems[2 * g].at[j],
                        recv_sem=sems[2 * g + 1].at[j], device_id=to, device_id_type=MESH).start()
                k += 1
        token[...] = jnp.zeros_like(token)

    res = pl.pallas_call(
        body, name=name,
        out_shape=(*[pltpu.SemaphoreType.DMA((size,)) for size in sizes for _ in range(2)],
                   *[pltpu.HBM(a.shape, a.dtype) for a in lands], jax.ShapeDtypeStruct((8, 128), f32)),
        in_specs=[_ANY] * nd + [_HBM] * n,
        out_specs=(*[_SEM] * (2 * ng), *[_HBM] * n, pl.BlockSpec(memory_space=pltpu.VMEM)),
        input_output_aliases={nd + k: 2 * ng + k for k in range(n)},
        compiler_params=pltpu.CompilerParams(has_side_effects=_EFFECT),
    )(*deps, *[_in_hbm(a) for a in lands])
    out, k = [], 2 * ng
    for g, size in enumerate(sizes):
        out.append((res[2 * g], res[2 * g + 1], list(res[k:k + size])))
        k += size
    return out, res[-1]


def gather_forward(name, send_sems, recv_sems, lands, after):
    n = len(lands)

    def body(*refs):
        land = refs[:n]
        ssem, rsem = refs[n:n + 2]
        send2, recv2 = refs[n + 3:n + 5]
        token = refs[-1]
        x, y, c = _me()
        for k in range(n):
            four = land[k].at[pl.ds(0, 4)]
            w = pltpu.make_async_remote_copy(src_ref=four, dst_ref=four, send_sem=ssem.at[k], recv_sem=rsem.at[k],
                                             device_id=(x, y, c), device_id_type=MESH)
            w.wait_send()
            w.wait_recv()
            for cx, cy in _other_chips(x, y):
                slot = 4 * cx + 2 * cy + c
                pltpu.make_async_remote_copy(
                    src_ref=land[k].at[slot], dst_ref=land[k].at[slot], send_sem=send2.at[k], recv_sem=recv2.at[k],
                    device_id=(x, y, 1 - c), device_id_type=MESH).start()
        token[...] = jnp.zeros_like(token)

    res = pl.pallas_call(
        body, name=name,
        out_shape=(pltpu.SemaphoreType.DMA((n,)), pltpu.SemaphoreType.DMA((n,)),
                   *[pltpu.HBM(a.shape, a.dtype) for a in lands], jax.ShapeDtypeStruct((8, 128), f32)),
        in_specs=[_HBM] * n + [_SEM, _SEM, _ANY],
        out_specs=(_SEM, _SEM, *[_HBM] * n, pl.BlockSpec(memory_space=pltpu.VMEM)),
        input_output_aliases={k: 2 + k for k in range(n)},
        compiler_params=pltpu.CompilerParams(has_side_effects=_EFFECT),
    )(*lands, send_sems, recv_sems, after)
    return res[0], res[1], list(res[2:2 + n]), res[-1]


def exchange_start(name, grads, deps=()):
    n, nd = len(grads), len(deps)
    lands = [lax.empty((NDEV - 1, g.shape[0]) + g.shape[2:], g.dtype) for g in grads]

    def body(*refs):
        src = refs[nd:nd + n]
        land = refs[nd + n:nd + 2 * n]
        send_sems, recv_sems = refs[nd + 2 * n:nd + 2 * n + 2]
        token = refs[-1]
        x, y, c = _me()
        me = 4 * x + 2 * y + c
        for d in range(1, NDEV):
            p = (me + d) % NDEV
            for k in range(n):
                pltpu.make_async_remote_copy(
                    src_ref=src[k].at[:, p], dst_ref=land[k].at[d - 1], send_sem=send_sems.at[k], recv_sem=recv_sems.at[k],
                    device_id=_peer(p), device_id_type=MESH).start()
        token[...] = jnp.zeros_like(token)

    both = list(grads) + lands
    res = pl.pallas_call(
        body, name=name,
        out_shape=(pltpu.SemaphoreType.DMA((n,)), pltpu.SemaphoreType.DMA((n,)),
                   *[pltpu.HBM(a.shape, a.dtype) for a in both], jax.ShapeDtypeStruct((8, 128), f32)),
        in_specs=[_ANY] * nd + [_HBM] * (2 * n),
        out_specs=(_SEM, _SEM, *[_HBM] * (2 * n), pl.BlockSpec(memory_space=pltpu.VMEM)),
        input_output_aliases={nd + k: 2 + k for k in range(2 * n)},
        compiler_params=pltpu.CompilerParams(has_side_effects=_EFFECT),
    )(*deps, *[_in_hbm(a) for a in both])
    return res[0], res[1], list(res[2:2 + n]), list(res[2 + n:2 + 2 * n]), res[-1]


def copies_wait(name, send_sems, recv_sems, sent, lands, after, count=NDEV - 1):
    ns, n = len(sent), len(lands)

    def body(*refs):
        land = refs[ns:ns + n]
        ssem, rsem = refs[ns + n:ns + n + 2]
        x, y, c = _me()
        for k in range(n):
            seven = land[k].at[pl.ds(0, count)]
            w = pltpu.make_async_remote_copy(src_ref=seven, dst_ref=seven, send_sem=ssem.at[k], recv_sem=rsem.at[k],
                                             device_id=(x, y, c), device_id_type=MESH)
            w.wait_send()
            w.wait_recv()

    both = list(sent) + list(lands)
    res = pl.pallas_call(
        body, name=name, out_shape=tuple(pltpu.HBM(a.shape, a.dtype) for a in both),
        in_specs=[_HBM] * (ns + n) + [_SEM, _SEM, _ANY], out_specs=tuple([_HBM] * (ns + n)),
        input_output_aliases={k: k for k in range(ns + n)},
        compiler_params=pltpu.CompilerParams(has_side_effects=_EFFECT),
    )(*both, send_sems, recv_sems, after)
    return list(res[:ns]), list(res[ns:])


def prep_layer(layer, me, col_sharded, row_sharded, deps=()):
    nc, nr = len(col_sharded), len(row_sharded)

    def body(me_ref, *refs):
        ins, outs = refs[:nc + nr], refs[nc + nr + len(deps):]
        for k in range(nc):
            outs[k][...] = ins[k][...].T.astype(bf16)
        for k in range(nc, nc + nr):
            outs[k][...] = ins[k][...].astype(bf16)

    arrs = list(col_sharded) + list(row_sharded)
    in_specs = [pl.BlockSpec((None,) + a.shape[1:], lambda i, me_ref: (layer, 0, 0)) for a in arrs]
    shapes = [(a.shape[2], a.shape[1]) for a in col_sharded] + [a.shape[1:] for a in row_sharded]
    out_specs = [pl.BlockSpec((None,) + s, lambda i, me_ref: (me_ref[0], 0, 0)) for s in shapes]
    return pl.pallas_call(
        body, name=f"prep_layer{layer}",
        grid_spec=pltpu.PrefetchScalarGridSpec(num_scalar_prefetch=1, grid=(1,), in_specs=in_specs + [_ANY] * len(deps),
                                               out_specs=out_specs),
        out_shape=[jax.ShapeDtypeStruct((NDEV,) + s, bf16) for s in shapes], compiler_params=_cp())(me, *arrs, *deps)


def slot_in(name, me, arrays):
    n = len(arrays)

    def body(me_ref, *refs):
        for k in range(n):
            refs[n + k][...] = refs[k][...]

    in_specs = [pl.BlockSpec(a.shape, lambda i, me_ref: (0, 0)) for a in arrays]
    out_specs = [pl.BlockSpec((None,) + a.shape, lambda i, me_ref: (me_ref[0], 0, 0)) for a in arrays]
    return pl.pallas_call(
        body, name=name,
        grid_spec=pltpu.PrefetchScalarGridSpec(num_scalar_prefetch=1, grid=(1,), in_specs=in_specs, out_specs=out_specs),
        out_shape=[jax.ShapeDtypeStruct((NDEV,) + a.shape, a.dtype) for a in arrays])(me, *arrays)


def _ff_chunks(ff, want=768):
    if ff % 256:
        return [slice(0, ff)]
    return [slice(c, min(c + want, ff)) for c in range(0, ff, want)]


def ffn_fwd(name, h, g, wgT, wuT, wd, deps=()):
    tp, d = h.shape
    ff = wgT.shape[0]
    tm = _row_tile(tp, 704)

    def body(h_ref, g_ref, wg_ref, wu_ref, wd_ref, ho_ref, a_ref, b_ref):
        hh = h_ref[...]
        hhat, _ = _rms(hh)
        n = (hhat * g_ref[...]).astype(bf16)
        acc = None
        for cols in _ff_chunks(ff):
            a = _dot_nt(n, wg_ref[cols, :])
            b = _dot_nt(n, wu_ref[cols, :])
            part = _dot_nn(((a * _sigmoid(a)) * b).astype(bf16), wd_ref[cols, :])
            acc = part if acc is None else acc + part
            a_ref[:, cols] = a.astype(bf16)
            b_ref[:, cols] = b.astype(bf16)
        ho_ref[...] = hh + 0.5 * acc

    row = lambda w: pl.BlockSpec((tm, w), lambda i: (i, 0))
    return pl.pallas_call(
        _skip(len(deps), body), name=name, grid=(tp // tm,),
        in_specs=[_ANY] * len(deps) + [row(d), _resident((1, d)), _resident((ff, d)), _resident((ff, d)), _resident((ff, d))],
        out_specs=[row(d), row(ff), row(ff)],
        out_shape=[jax.ShapeDtypeStruct((tp, d), f32), jax.ShapeDtypeStruct((tp, ff), bf16),
                   jax.ShapeDtypeStruct((tp, ff), bf16)],
        compiler_params=_cp(("arbitrary",)))(*deps, h, g, wgT, wuT, wd)


def ffn_bwd(name, dy, h, g, a, b, wgT, wuT, wd, deps=()):
    tp, d = h.shape
    ff = wgT.shape[0]
    tm = _row_tile(tp, 384)

    def body(dy_ref, h_ref, g_ref, a_ref, b_ref, wg_ref, wu_ref, wd_ref, dh_ref, lhs_ref, rhs_ref, dg_ref):
        dyv = dy_ref[...]
        hhat, rs = _rms(h_ref[...])
        gv = g_ref[...]
        n = hhat * gv
        dyh = (0.5 * dyv).astype(bf16)
        dn = None
        for cols in _ff_chunks(ff):
            ds = _dot_nt(dyh, wd_ref[cols, :])
            av = a_ref[:, cols].astype(f32)
            bv = b_ref[:, cols].astype(f32)
            sg = _sigmoid(av)
            sa = av * sg
            da = (ds * bv * (sg * (1.0 + av * (1.0 - sg)))).astype(bf16)
            db = (ds * sa).astype(bf16)
            part = _dot_nn(da, wg_ref[cols, :]) + _dot_nn(db, wu_ref[cols, :])
            dn = part if dn is None else dn + part
            lhs_ref[0, :, cols] = da
            lhs_ref[1, :, cols] = db
            lhs_ref[2, :, cols] = (sa * bv).astype(bf16)
        dh_ref[...] = dyv + _rms_bwd(dn, gv, hhat, rs)

        @pl.when(pl.program_id(0) == 0)
        def _():
            dg_ref[...] = jnp.zeros_like(dg_ref)

        dg_ref[0:1, :] += jnp.sum(dn * hhat, axis=0, keepdims=True)
        rhs_ref[0] = n.astype(bf16)
        rhs_ref[1] = dyh

    row = lambda w: pl.BlockSpec((tm, w), lambda i: (i, 0))
    return pl.pallas_call(
        _skip(len(deps), body), name=name, grid=(tp // tm,),
        in_specs=[_ANY] * len(deps) + [row(d), row(d), _resident((1, d)), row(ff), row(ff),
                  _resident((ff, d)), _resident((ff, d)), _resident((ff, d))],
        out_specs=[row(d), pl.BlockSpec((3, tm, ff), lambda i: (0, i, 0)), pl.BlockSpec((2, tm, d), lambda i: (0, i, 0)),
                   pl.BlockSpec((8, d), lambda i: (0, 0))],
        out_shape=[jax.ShapeDtypeStruct((tp, d), f32), jax.ShapeDtypeStruct((3, tp, ff), bf16),
                   jax.ShapeDtypeStruct((2, tp, d), bf16), jax.ShapeDtypeStruct((8, d), f32)],
        compiler_params=_cp(("arbitrary",)))(*deps, dy, h, g, a, b, wgT, wuT, wd)


def mm_tn(name, lhs, rhs, rhs_of, deps=(), only=None):
    _, tp, m = lhs.shape
    b0, nb = (0, lhs.shape[0]) if only is None else (only, 1)
    n = rhs.shape[2]
    def fits(t, ms):
        mb = m // ms
        return (tp % t == 0 and m % (128 * ms) == 0
                and 2 * t * (mb + n) * 2 + mb * n * (2 * 2 + 4 + (4 if t < tp else 0)) <= 54 * 2**20)

    tk, msplit = next(((t, ms) for t in (tp, 1408, 704, 384) for ms in (1, 2, 4) if fits(t, ms)), (128, 1))
    nk = tp // tk
    mb = m // msplit

    def body(l_ref, r_ref, o_ref, acc_ref):
        if nk == 1:
            o_ref[...] = _dot_tn(l_ref[...], r_ref[...]).astype(o_ref.dtype)
            return
        k = pl.program_id(2)

        @pl.when(k == 0)
        def _():
            acc_ref[...] = jnp.zeros_like(acc_ref)

        acc_ref[...] += _dot_tn(l_ref[...], r_ref[...])

        @pl.when(k == nk - 1)
        def _():
            o_ref[...] = acc_ref[...].astype(o_ref.dtype)

    return pl.pallas_call(
        _skip(len(deps), body), name=name, grid=(nb, msplit, nk),
        in_specs=[_ANY] * len(deps) + [pl.BlockSpec((None, tk, mb), lambda b, j, k: (b0 + b, k, j)),
                                       pl.BlockSpec((None, tk, n), lambda b, j, k: (rhs_of(b0 + b), k, 0))],
        out_specs=pl.BlockSpec((None, mb, n), lambda b, j, k: (b, j, 0)),
        out_shape=jax.ShapeDtypeStruct((nb, m, n), bf16),
        scratch_shapes=[pltpu.VMEM((mb, n) if nk > 1 else (8, 128), f32)],
        compiler_params=_cp(("arbitrary", "arbitrary", "arbitrary")))(*deps, lhs, rhs)


def mix_in_fwd(name, h, g, winT, cos2, sin2):
    tp, d = h.shape
    nin = winT.shape[0]
    tm = _row_tile(tp, 704)

    def body(h_ref, g_ref, w_ref, cos_ref, sin_ref, zq_ref, zg_ref, zu_ref, zgate_ref):
        hhat, _ = _rms(h_ref[...])
        z = _dot_nt((hhat * g_ref[...]).astype(bf16), w_ref[...])
        cosv, sinv = cos_ref[...], sin_ref[...]
        for hh in range(HEADS):
            qcols, kcols = slice(hh * HD, (hh + 1) * HD), slice(RW + hh * HD, RW + (hh + 1) * HD)
            zq_ref[:, qcols] = (_rot(z[:, qcols], cosv, sinv) * HD ** -0.5).astype(bf16)
            zq_ref[:, kcols] = _rot(z[:, kcols], cosv, sinv).astype(bf16)
        zq_ref[:, 2 * RW:] = z[:, 2 * RW:3 * RW].astype(bf16)
        zg_ref[...] = z[:, 3 * RW:4 * RW].astype(zg_ref.dtype)
        zu_ref[...] = z[:, 4 * RW:5 * RW]
        zgate_ref[...] = z[:, 5 * RW:].astype(zgate_ref.dtype)

    row = lambda w: pl.BlockSpec((tm, w), lambda i: (i, 0))
    widths = (3 * RW, RW, RW, 2 * d)
    return pl.pallas_call(
        body, name=name, grid=(tp // tm,),
        in_specs=[row(d), _resident((1, d)), _resident((nin, d)), row(HD), row(HD)],
        out_specs=[row(w) for w in widths],
        out_shape=[jax.ShapeDtypeStruct((tp, w), dt) for w, dt in zip(widths, (bf16, bf16, f32, bf16))],
        compiler_params=_cp(("arbitrary",)))(h, g, winT, cos2, sin2)


def mix_in_bwd(name, dq, dk, dv, dzg, dzu, dzgate, h, g, winT, dres):
    tp, d = h.shape
    nin = winT.shape[0]
    tm = _row_tile(tp)

    def body(dq_ref, dk_ref, dv_ref, dzg_ref, dzu_ref, dzgate_ref, h_ref, g_ref, w_ref, dres_ref, dh_ref, dz_ref, n_ref, dg_ref):
        dn, col = None, 0
        for piece in (dq_ref, dk_ref, dv_ref, dzg_ref, dzu_ref, dzgate_ref):
            v = piece[...]
            part = _dot_nn(v, w_ref[col:col + v.shape[1], :])
            dn = part if dn is None else dn + part
            dz_ref[:, col:col + v.shape[1]] = v
            col += v.shape[1]
        hhat, rs = _rms(h_ref[...])
        gv = g_ref[...]
        dh_ref[...] = dres_ref[...] + _rms_bwd(dn, gv, hhat, rs)

        @pl.when(pl.program_id(0) == 0)
        def _():
            dg_ref[...] = jnp.zeros_like(dg_ref)

        dg_ref[0:1, :] += jnp.sum(dn * hhat, axis=0, keepdims=True)
        n_ref[...] = (hhat * gv).astype(bf16)

    row = lambda w: pl.BlockSpec((tm, w), lambda i: (i, 0))
    return pl.pallas_call(
        body, name=name, grid=(tp // tm,),
        in_specs=[row(RW)] * 5 + [row(2 * d), row(d), _resident((1, d)), _resident((nin, d)), row(d)],
        out_specs=[row(d), pl.BlockSpec((None, tm, nin), lambda i: (0, i, 0)), pl.BlockSpec((None, tm, d), lambda i: (0, i, 0)),
                   pl.BlockSpec((8, d), lambda i: (0, 0))],
        out_shape=[jax.ShapeDtypeStruct((tp, d), f32), jax.ShapeDtypeStruct((1, tp, nin), bf16),
                   jax.ShapeDtypeStruct((1, tp, d), bf16), jax.ShapeDtypeStruct((8, d), f32)],
        compiler_params=_cp(("arbitrary",)))(dq, dk, dv, dzg, dzu, dzgate, h, g, winT, dres)


def _retention_tables(tp, pad, zero):
    half = HD // 2
    lane = jnp.arange(HD)
    inv_freq = ROPE_BASE ** (-(lane % half).astype(f32) / half)
    pos = jnp.arange(tp, dtype=f32) - pad + zero
    ang = pos[:, None] * inv_freq[None, :]
    cos2 = jnp.cos(ang)
    sin2 = jnp.where(lane[None, :] < half, -1.0, 1.0) * jnp.sin(ang)
    log_gamma = jnp.log1p(-(2.0 ** (-5.0 - jnp.arange(HEADS, dtype=f32))))
    idx = jnp.arange(CHUNK, dtype=f32)
    diff = idx[:, None] - idx[None, :]
    intra = jnp.where(diff[None] >= 0, jnp.exp(diff[None] * log_gamma[:, None, None]), 0.0)
    k_decay = jnp.exp((CHUNK - 1.0 - idx)[None, :] * log_gamma[:, None])
    q_decay = jnp.exp((idx + 1.0)[None, :] * log_gamma[:, None])
    chunk_decay = jnp.exp(CHUNK * log_gamma)
    full = (HEADS, CHUNK, HD)
    dec = jnp.stack([intra, jnp.broadcast_to(k_decay[:, :, None], full), jnp.broadcast_to(q_decay[:, :, None], full),
                     jnp.broadcast_to(chunk_decay[:, None, None], full)], axis=1)
    return cos2, sin2, dec


def _rot(t, cos2, sin2):
    return t * cos2 + pltpu.roll(t, HD // 2, 1) * sin2


def _rot_t(t, cos2, sin2):
    return t * cos2 - pltpu.roll(t, HD // 2, 1) * sin2


def _chunks_per_step(nch):
    return 3 if nch % 3 == 0 else 1


def retention_fwd(name, zq, dec):
    tp = zq.shape[0]
    nch = tp // CHUNK
    per = _chunks_per_step(nch)

    def body(q_ref, k_ref, v_ref, dec_ref, out_ref, st_ref, s_ref):
        @pl.when(pl.program_id(0) == 0)
        def _():
            s_ref[...] = jnp.zeros_like(s_ref)

        state = [s_ref[hh] for hh in range(HEADS)]
        for j in range(per):
            rows = slice(j * CHUNK, (j + 1) * CHUNK)
            for hh in range(HEADS):
                cols = slice(hh * HD, (hh + 1) * HD)
                qb, kb, vb = q_ref[rows, cols], k_ref[rows, cols], v_ref[rows, cols]
                sc = (_dot_nt(qb, kb) * dec_ref[hh, 0]).astype(bf16)
                sb = state[hh].astype(bf16)
                cross = _dot_nn((qb.astype(f32) * dec_ref[hh, 2]).astype(bf16), sb)
                out_ref[rows, cols] = (_dot_nn(sc, vb) + cross).astype(out_ref.dtype)
                st_ref[hh, j] = sb
                state[hh] = state[hh] * dec_ref[hh, 3] + _dot_tn((kb.astype(f32) * dec_ref[hh, 1]).astype(bf16), vb)
        for hh in range(HEADS):
            s_ref[hh] = state[hh]

    part = lambda j: pl.BlockSpec((per * CHUNK, RW), lambda n: (n, j))
    return pl.pallas_call(
        body, name=name, grid=(nch // per,),
        in_specs=[part(0), part(1), part(2), _resident((HEADS, 4, CHUNK, HD))],
        out_specs=[part(0), pl.BlockSpec((HEADS, per, HD, HD), lambda n: (0, n, 0, 0))],
        out_shape=[jax.ShapeDtypeStruct((tp, RW), bf16), jax.ShapeDtypeStruct((HEADS, nch, HD, HD), bf16)],
        scratch_shapes=[pltpu.VMEM((HEADS, HD, HD), f32)],
        compiler_params=_cp(("arbitrary",)))(zq, zq, zq, dec)


def retention_bwd(name, zq, cos2, sin2, dec, states, dout, pad, deps=()):
    tp = zq.shape[0]
    nch = tp // CHUNK
    per = _chunks_per_step(nch)
    nblk = nch // per
    scale = HD ** -0.5

    def body(q_ref, k_ref, v_ref, cos_ref, sin_ref, dec_ref, st_ref, do_ref, dq_ref, dk_ref, dv_ref, g_ref):
        @pl.when(pl.program_id(0) == 0)
        def _():
            g_ref[...] = jnp.zeros_like(g_ref)

        first_row = (nblk - 1 - pl.program_id(0)) * (per * CHUNK)
        gstate = [g_ref[hh] for hh in range(HEADS)]
        for j in reversed(range(per)):
            rows = slice(j * CHUNK, (j + 1) * CHUNK)
            cosv, sinv = cos_ref[rows, :], sin_ref[rows, :]
            keep = (lax.broadcasted_iota(jnp.int32, (CHUNK, HD), 0) + (first_row + j * CHUNK)) >= pad
            for hh in range(HEADS):
                cols = slice(hh * HD, (hh + 1) * HD)
                intra, kdec, qdec = dec_ref[hh, 0], dec_ref[hh, 1], dec_ref[hh, 2]
                qb, kb, vb = q_ref[rows, cols], k_ref[rows, cols], v_ref[rows, cols]
                qd = (qb.astype(f32) * qdec).astype(bf16)
                kd = (kb.astype(f32) * kdec).astype(bf16)
                sc = (_dot_nt(qb, kb) * intra).astype(bf16)
                dob = do_ref[rows, cols]
                sb = st_ref[hh, j]
                gb = gstate[hh].astype(bf16)
                dsc = (_dot_nt(dob, vb) * intra).astype(bf16)
                dv = _dot_tn(sc, dob) + _dot_nn(kd, gb)
                dqr = _dot_nn(dsc, kb) + _dot_nt(dob, sb) * qdec
                dkr = _dot_tn(dsc, qb) + _dot_nt(vb, gb) * kdec
                gstate[hh] = gstate[hh] * dec_ref[hh, 3] + _dot_tn(qd, dob)
                dq_ref[rows, cols] = jnp.where(keep, _rot_t(dqr * scale, cosv, sinv), 0.0).astype(bf16)
                dk_ref[rows, cols] = jnp.where(keep, _rot_t(dkr, cosv, sinv), 0.0).astype(bf16)
                dv_ref[rows, cols] = jnp.where(keep, dv, 0.0).astype(bf16)
        for hh in range(HEADS):
            g_ref[hh] = gstate[hh]

    part = lambda j: pl.BlockSpec((per * CHUNK, RW), lambda t: (nblk - 1 - t, j))
    table = pl.BlockSpec((per * CHUNK, HD), lambda t: (nblk - 1 - t, 0))
    return pl.pallas_call(
        _skip(len(deps), body), name=name, grid=(nblk,),
        in_specs=[_ANY] * len(deps) + [part(0), part(1), part(2), table, table, _resident((HEADS, 4, CHUNK, HD)),
                                       pl.BlockSpec((HEADS, per, HD, HD), lambda t: (0, nblk - 1 - t, 0, 0)), part(0)],
        out_specs=[part(0)] * 3,
        out_shape=[jax.ShapeDtypeStruct((tp, RW), bf16)] * 3,
        scratch_shapes=[pltpu.VMEM((HEADS, HD, HD), f32)],
        compiler_params=_cp(("arbitrary",)))(*deps, zq, zq, zq, cos2, sin2, dec, states, dout)


def _window_sum(xv, steps, tp, forward):
    s = xv
    for j in range(steps):
        sh = 2 ** j
        s = s + pltpu.roll(s, (tp - sh) if forward else sh, 0)
    return s


def pool_fwd(name, zu, maps, scale, pad):
    tp = zu.shape[0]

    def body(u_ref, maps_ref, scale_ref, pooled_ref, p_ref):
        row = lax.broadcasted_iota(jnp.int32, (tp, HD), 0)
        for gi, w in enumerate(POOL_WINDOWS):
            cols = slice(gi * HD, (gi + 1) * HD)
            xv = u_ref[:, cols]
            cnt = jnp.clip(row - (pad - 1), 1, w).astype(f32)
            pooled = jnp.where(row >= pad, _window_sum(xv, gi + 1, tp, False) / cnt - xv, 0.0).astype(bf16)
            pooled_ref[:, cols] = pooled
            p_ref[:, cols] = (_dot_nn(pooled, maps_ref[gi].astype(bf16)) * scale_ref[:, cols]).astype(bf16)

    return pl.pallas_call(
        body, name=name,
        out_shape=[jax.ShapeDtypeStruct((tp, RW), bf16), jax.ShapeDtypeStruct((tp, RW), bf16)],
        compiler_params=_cp())(zu, maps, scale)


def pool_bwd(name, dp, pooled, maps, scale, pad):
    tp = dp.shape[0]

    def body(dp_ref, pooled_ref, maps_ref, scale_ref, du_ref, dmaps_ref, dscale_ref):
        row = lax.broadcasted_iota(jnp.int32, (tp, HD), 0)
        dscale_ref[...] = jnp.zeros_like(dscale_ref)
        for gi, w in enumerate(POOL_WINDOWS):
            cols = slice(gi * HD, (gi + 1) * HD)
            mb = maps_ref[gi].astype(bf16)
            pooled = pooled_ref[:, cols]
            dpf = dp_ref[:, cols].astype(f32)
            dscale_ref[0:1, cols] = jnp.sum(dpf * _dot_nn(pooled, mb), axis=0, keepdims=True)
            dpm = (dpf * scale_ref[:, cols]).astype(bf16)
            dmaps_ref[gi * HD:(gi + 1) * HD, :] = _dot_tn(pooled, dpm)
            dpool = jnp.where(row >= pad, _dot_nt(dpm, mb), 0.0)
            cnt = jnp.clip(row - (pad - 1), 1, w).astype(f32)
            du = _window_sum(dpool / cnt, gi + 1, tp, True) - dpool
            du_ref[:, cols] = jnp.where(row >= pad, du, 0.0).astype(bf16)

    return pl.pallas_call(
        body, name=name,
        out_shape=[jax.ShapeDtypeStruct((tp, RW), bf16), jax.ShapeDtypeStruct((HEADS * HD, HD), f32),
                   jax.ShapeDtypeStruct((8, RW), f32)],
        compiler_params=_cp())(dp, pooled, maps, scale)


def _group_norm(o):
    mu = jnp.mean(o, axis=-1, keepdims=True)
    oc = o - mu
    rstd = lax.rsqrt(jnp.mean(oc * oc, axis=-1, keepdims=True) + EPS)
    return oc * rstd, rstd


def mix_out_fwd(name, h, oraw, zg, zgate, p, wretT, wpoolT, wout, deps=()):
    tp, d = h.shape
    tm = _row_tile(tp, 704)

    def body(h_ref, o_ref, zg_ref, zgate_ref, p_ref, wr_ref, wp_ref, wo_ref, ho_ref, rp_ref, mixed_ref):
        parts = []
        for hh in range(HEADS):
            cols = slice(hh * HD, (hh + 1) * HD)
            rhat, _ = _group_norm(o_ref[:, cols].astype(f32))
            gv = zg_ref[:, cols].astype(f32)
            parts.append(rhat * (gv * _sigmoid(gv)))
        r = jnp.concatenate(parts, axis=-1).astype(bf16)
        pv = p_ref[...]
        ret = _dot_nt(r, wr_ref[...])
        pool = _dot_nt(pv, wp_ref[...])
        mixed = (_sigmoid(zgate_ref[:, :d].astype(f32)) * ret + _sigmoid(zgate_ref[:, d:].astype(f32)) * pool).astype(bf16)
        ho_ref[...] = h_ref[...] + _dot_nn(mixed, wo_ref[...])
        rp_ref[0] = r
        rp_ref[1] = pv
        mixed_ref[...] = mixed

    row = lambda w: pl.BlockSpec((tm, w), lambda i: (i, 0))
    return pl.pallas_call(
        _skip(len(deps), body), name=name, grid=(tp // tm,),
        in_specs=[_ANY] * len(deps) + [row(d), row(RW), row(RW), row(2 * d), row(RW), _resident((d, RW)), _resident((d, RW)),
                                       _resident((d, d))],
        out_specs=[row(d), pl.BlockSpec((2, tm, RW), lambda i: (0, i, 0)), pl.BlockSpec((None, tm, d), lambda i: (0, i, 0))],
        out_shape=[jax.ShapeDtypeStruct((tp, d), f32), jax.ShapeDtypeStruct((2, tp, RW), bf16),
                   jax.ShapeDtypeStruct((1, tp, d), bf16)],
        compiler_params=_cp(("arbitrary",)))(*deps, h, oraw, zg, zgate, p, wretT, wpoolT, wout)


def mix_out_bwd(name, dy, oraw, zg, zgate, rp, wretT, wpoolT, wout, deps=()):
    tp, d = dy.shape
    tm = _row_tile(tp, 704)

    def body(dy_ref, o_ref, zg_ref, zgate_ref, rp_ref, wr_ref, wp_ref, wo_ref,
             do_ref, dzg_ref, dzgate_ref, dp_ref, drp_ref, dyb_ref):
        dyb = dy_ref[...].astype(bf16)
        dmixed = _dot_nt(dyb, wo_ref[...])
        sa = _sigmoid(zgate_ref[:, :d].astype(f32))
        sb = _sigmoid(zgate_ref[:, d:].astype(f32))
        dret = dmixed * sa
        dpool = dmixed * sb
        dzgate_ref[:, :d] = (dret * _dot_nt(rp_ref[0], wr_ref[...]) * (1.0 - sa)).astype(bf16)
        dzgate_ref[:, d:] = (dpool * _dot_nt(rp_ref[1], wp_ref[...]) * (1.0 - sb)).astype(bf16)
        dretb, dpoolb = dret.astype(bf16), dpool.astype(bf16)
        dr = _dot_nn(dretb, wr_ref[...])
        dp_ref[...] = _dot_nn(dpoolb, wp_ref[...]).astype(bf16)
        for hh in range(HEADS):
            cols = slice(hh * HD, (hh + 1) * HD)
            rhat, rstd = _group_norm(o_ref[:, cols].astype(f32))
            gv = zg_ref[:, cols].astype(f32)
            sg = _sigmoid(gv)
            drh = dr[:, cols]
            drhat = drh * (gv * sg)
            dzg_ref[:, cols] = (drh * rhat * (sg * (1.0 + gv * (1.0 - sg)))).astype(bf16)
            do = rstd * (drhat - jnp.mean(drhat, axis=-1, keepdims=True)
                         - rhat * jnp.mean(drhat * rhat, axis=-1, keepdims=True))
            do_ref[:, cols] = do.astype(bf16)
        drp_ref[0] = dretb
        drp_ref[1] = dpoolb
        dyb_ref[...] = dyb

    row = lambda w: pl.BlockSpec((tm, w), lambda i: (i, 0))
    return pl.pallas_call(
        _skip(len(deps), body), name=name, grid=(tp // tm,),
        in_specs=[_ANY] * len(deps) + [row(d), row(RW), row(RW), row(2 * d), pl.BlockSpec((2, tm, RW), lambda i: (0, i, 0)),
                                       _resident((d, RW)), _resident((d, RW)), _resident((d, d))],
        out_specs=[row(RW), row(RW), row(2 * d), row(RW), pl.BlockSpec((2, tm, d), lambda i: (0, i, 0)),
                   pl.BlockSpec((None, tm, d), lambda i: (0, i, 0))],
        out_shape=[jax.ShapeDtypeStruct((tp, RW), bf16), jax.ShapeDtypeStruct((tp, RW), bf16),
                   jax.ShapeDtypeStruct((tp, 2 * d), bf16), jax.ShapeDtypeStruct((tp, RW), bf16),
                   jax.ShapeDtypeStruct((2, tp, d), bf16), jax.ShapeDtypeStruct((1, tp, d), bf16)],
        compiler_params=_cp(("arbitrary",)))(*deps, dy, oraw, zg, zgate, rp, wretT, wpoolT, wout)


def final_loss(name, h, g, target):
    tp, d = h.shape
    tm = _row_tile(tp)
    nsub = tm // CHUNK

    def body(h_ref, g_ref, *rest):
        t_refs = rest[:nsub]
        dh_ref, loss_ref, dg_ref = rest[nsub:]
        i = pl.program_id(0)

        @pl.when(i == 0)
        def _():
            loss_ref[...] = jnp.zeros_like(loss_ref)
            dg_ref[...] = jnp.zeros_like(dg_ref)

        gv = g_ref[...]
        for j in range(nsub):
            rows = slice(j * CHUNK, (j + 1) * CHUNK)
            hhat, rs = _rms(h_ref[rows, :])
            err = jnp.where(i * nsub + j >= 1, hhat * gv - t_refs[j][...], 0.0)
            dyv = err / d
            dh_ref[rows, :] = _rms_bwd(dyv, gv, hhat, rs)
            loss_ref[...] += 0.5 * jnp.sum(jnp.sum(err * err, axis=-1, keepdims=True) / d)
            dg_ref[0:1, :] += jnp.sum(dyv * hhat, axis=0, keepdims=True)

    lagged = lambda j: pl.BlockSpec((CHUNK, d), lambda i: (jnp.maximum(i * nsub + j - 1, 0), 0))
    return pl.pallas_call(
        body, name=name, grid=(tp // tm,),
        in_specs=[pl.BlockSpec((tm, d), lambda i: (i, 0)), _resident((1, d))] + [lagged(j) for j in range(nsub)],
        out_specs=[pl.BlockSpec((tm, d), lambda i: (i, 0)), pl.BlockSpec((8, 128), lambda i: (0, 0)),
                   pl.BlockSpec((8, d), lambda i: (0, 0))],
        out_shape=[jax.ShapeDtypeStruct((tp, d), f32), jax.ShapeDtypeStruct((8, 128), f32),
                   jax.ShapeDtypeStruct((8, d), f32)],
        compiler_params=_cp(("arbitrary",)))(h, g, *[target] * nsub)


def _adamw(w, g, m, v):
    m = ADAM_B1 * m + (1.0 - ADAM_B1) * g
    v = ADAM_B2 * v + (1.0 - ADAM_B2) * (g * g)
    m_hat = m / (1.0 - ADAM_B1 ** ADAM_STEP)
    v_hat = v / (1.0 - ADAM_B2 ** ADAM_STEP)
    delta = -ADAM_LR * (m_hat / (jnp.sqrt(v_hat) + ADAM_EPS) + ADAM_WD * w)
    return delta, m, v


def adam_big(name, me, recv, own, b, layer, transposed, w, m, v, prev):
    r, c = recv.shape[2:]
    wshape = w.shape[1:]
    nchunk = 1 if transposed else next(k for k in (4, 2, 1) if r % (16 * k) == 0)
    rc = r // nchunk

    def body(me_ref, recv_ref, own_ref, w_ref, m_ref, v_ref, *rest):
        g_ref, d_ref, nm_ref, nv_ref = rest[-4:]
        g = own_ref[...].astype(f32)
        for j in range(NDEV - 1):
            g = g + recv_ref[j].astype(f32)
        if transposed:
            g = g.T
        delta, nm, nv = _adamw(w_ref[...], g, m_ref[...], v_ref[...])
        g_ref[...] = g
        d_ref[...] = delta
        nm_ref[...] = nm
        nv_ref[...] = nv

    wblock = wshape if transposed else (rc, c)
    wspec = pl.BlockSpec((None,) + wblock, lambda i, me_ref: (layer, i, 0))
    in_specs = [pl.BlockSpec((NDEV - 1, None, rc, c), lambda i, me_ref: (0, b, i, 0)),
                pl.BlockSpec((None, None, rc, c), lambda i, me_ref: (b, me_ref[0], i, 0)), wspec, wspec, wspec]
    args = [recv, own, w, m, v]
    aliases = {}
    if prev is not None:
        in_specs += [_ANY] * 4
        args += list(prev)
        aliases = {6 + k: k for k in range(4)}
    return pl.pallas_call(
        body, name=name,
        grid_spec=pltpu.PrefetchScalarGridSpec(num_scalar_prefetch=1, grid=(nchunk,), in_specs=in_specs,
                                               out_specs=[wspec] * 4),
        out_shape=[jax.ShapeDtypeStruct(w.shape, f32)] * 4, input_output_aliases=aliases,
        compiler_params=_cp())(me, *args)


def adam_small(name, ga0, gmaps0, gmeta, ga1, gmaps1, norms, pool_scale, pool_maps, meta, final_norm, d):
    def body(ga0_ref, gmaps0_ref, gmeta_ref, ga1_ref, gmaps1_ref, *refs):
        ins, outs = refs[:21], refs[21:]
        x, y, c = _me()
        me = 4 * x + 2 * y + c

        def total(ref, rows):
            t = ref[0, rows, :].astype(f32)
            for j in range(1, NDEV):
                t = t + ref[j, rows, :].astype(f32)
            return t

        row = lambda r: slice(r, r + 1)
        outs[0][...] = jnp.broadcast_to(total(ga1_ref, row(0))[:, :128], (8, 128))

        def update(k, g, o):
            w_ref, m_ref, v_ref = ins[3 * k:3 * k + 3]
            delta, nm, nv = _adamw(w_ref[...], g, m_ref[...], v_ref[...])
            for ref, val in zip(outs[o:o + 4], (g, delta, nm, nv)):
                ref[...] = val

        two = lax.broadcasted_iota(jnp.int32, (2, d), 0)
        for k in range(3):
            update(k, jnp.where(two == 0, total(ga0_ref, row(k)), total(ga1_ref, row(2 + k))), 1 + 4 * k)
        update(3, jnp.where(two[:, :RW] == 0, total(ga0_ref, row(3))[:, :RW], total(ga1_ref, row(5))[:, :RW]), 13)
        update(4, jnp.concatenate([total(gmaps0_ref, slice(None)), total(gmaps1_ref, slice(None))], axis=0), 17)
        update(5, total(gmeta_ref, pl.ds(pl.multiple_of(me * N_META, N_META), N_META)), 21)
        update(6, total(ga1_ref, row(1)), 25)

    flat = []
    for trip in (*norms, pool_scale, pool_maps, meta, final_norm):
        flat += list(trip)
    out_shapes = [jax.ShapeDtypeStruct((8, 128), f32)]
    for trip in (*norms, pool_scale, pool_maps, meta, final_norm):
        out_shapes += [jax.ShapeDtypeStruct(trip[0].shape, f32)] * 4
    return pl.pallas_call(body, name=name, out_shape=out_shapes,
                          compiler_params=_cp())(ga0, gmaps0, gmeta, ga1, gmaps1, *flat)


def kernel(x, meta, ffn1_norm, ffn1_gate, ffn1_up, ffn1_down, mix_norm, w_in, pool_maps, pool_scale, w_ret_up, w_pool_up, w_out, ffn2_norm, ffn2_gate, ffn2_up, ffn2_down, final_norm, loss_target, m_meta, m_ffn1_norm, m_ffn1_gate, m_ffn1_up, m_ffn1_down, m_mix_norm, m_w_in, m_pool_maps, m_pool_scale, m_w_ret_up, m_w_pool_up, m_w_out, m_ffn2_norm, m_ffn2_gate, m_ffn2_up, m_ffn2_down, m_final_norm, v_meta, v_ffn1_norm, v_ffn1_gate, v_ffn1_up, v_ffn1_down, v_mix_norm, v_w_in, v_pool_maps, v_pool_scale, v_w_ret_up, v_w_pool_up, v_w_out, v_ffn2_norm, v_ffn2_gate, v_ffn2_up, v_ffn2_down, v_final_norm):
    seq, d = x.shape[1], x.shape[2]
    depth = ffn1_gate.shape[0]
    ff = ffn1_gate.shape[2] * NDEV
    nin = w_in.shape[2] * NDEV
    length = seq + N_META
    pad = (-length) % CHUNK
    tp = length + pad
    assert pad % 8 == 0 and pad + N_META == CHUNK and depth == 2 and nin == 5 * RW + 2 * d

    ix, iy, ic = _me()
    me = (4 * ix + 2 * iy + ic).astype(jnp.int32).reshape(1)

    meta_all, = all_gather("gather_meta", [meta])
    meta_full = jnp.transpose(meta_all, (1, 0, 2)).reshape(N_META, d)

    token = meta_all
    tview = lambda *arrs: [jnp.swapaxes(a, 1, 2) for a in arrs]
    t_g1, t_u1, t_g2, t_u2, t_in = (tview(w, m, v) for w, m, v in (
        (ffn1_gate, m_ffn1_gate, v_ffn1_gate), (ffn1_up, m_ffn1_up, v_ffn1_up), (ffn2_gate, m_ffn2_gate, v_ffn2_gate),
        (ffn2_up, m_ffn2_up, v_ffn2_up), (w_in, m_w_in, v_w_in)))
    keys, groups = [], []
    for layer in range(depth):
        lands = prep_layer(layer, me, [w_ret_up, w_pool_up],
                           [t_g1[0], t_u1[0], t_g2[0], t_u2[0], t_in[0], ffn1_down, ffn2_down, w_out],
                           (token,) if layer else ())
        wretT, wpoolT, g1T, u1T, g2T, u2T, winT, d1, d2, wout = lands
        keys += [("ffn1", layer), ("mix", layer), ("ffn2", layer)]
        groups += [[g1T, u1T, d1], [winT, wretT, wpoolT, wout], [g2T, u2T, d2]]
        if layer == 0:
            first, token = gather_start_chips("gather_start_first", groups[:1], (token,))
    second, token = gather_start_chips("gather_start_second", groups[1:2], (token,))
    gathers = dict(zip(keys, first + second))

    def forward(part, layer, after):
        ssem, rsem, group = gathers[(part, layer)]
        ssem, rsem, group, tok = gather_forward(f"gather_forward_{part}{layer}", ssem, rsem, group, after)
        gathers[(part, layer)] = (ssem, rsem, group)
        return tok

    def gathered(part, layer, after):
        ssem, rsem, group = gathers[(part, layer)]
        _, full = copies_wait(f"gather_wait_{part}{layer}", ssem, rsem, (), group, after, 3)
        return [a.reshape((NDEV * a.shape[1],) + a.shape[2:]) for a in full]

    cos2, sin2, dec = _retention_tables(tp, pad, token[0, 0])
    h = jnp.concatenate([jnp.zeros((pad, d), f32), meta_full + (token[0, 0] + 0.0 * cos2[0, 0] + 0.0 * dec[0, 0, 0, 0]), x[0]],
                        axis=0)

    saved = []
    weights = []
    tok = forward("ffn1", 0, h)
    rest, tok = gather_start_chips("gather_start_rest", groups[2:], (tok,))
    gathers.update(zip(keys[2:], rest))
    for layer in range(depth):
        row = lambda a: a[layer:layer + 1]
        s = {"h0": h}
        g1T, u1T, d1 = gathered("ffn1", layer, tok if layer == 0 else h)
        tok = forward("mix", layer, h) if layer else None
        h, s["a1"], s["b1"] = ffn_fwd(f"ffn1_fwd{layer}", h, row(ffn1_norm), g1T, u1T, d1, (tok,) if layer else ())
        s["h1"] = h
        if layer == 0:
            tok = forward("mix", layer, h)
        winT, wretT, wpoolT, wout = gathered("mix", layer, tok if layer == 0 else h)
        s["zq"], s["zg"], zu, s["zgate"] = mix_in_fwd(f"mix_in_fwd{layer}", h, row(mix_norm), winT, cos2, sin2)
        s["oraw"], s["states"] = retention_fwd(f"retention_fwd{layer}", s["zq"], dec)
        s["pooled"], p = pool_fwd(f"pool_fwd{layer}", zu, pool_maps[layer], row(pool_scale), pad)
        tok = forward("ffn2", layer, p)
        h, s["rp"], s["mixed"] = mix_out_fwd(
            f"mix_out_fwd{layer}", h, s["oraw"], s["zg"], s["zgate"], p, wretT, wpoolT, wout, (tok,))
        s["h2"] = h
        g2T, u2T, d2 = gathered("ffn2", layer, h)
        tok = (forward("ffn1", layer + 1, h),) if layer + 1 < depth else ()
        h, s["a2"], s["b2"] = ffn_fwd(f"ffn2_fwd{layer}", h, row(ffn2_norm), g2T, u2T, d2, tok)
        saved.append(s)
        weights.append((g1T, u1T, g2T, u2T, winT, wretT, wpoolT, d1, d2, wout))

    dh, loss_part, dg_final = final_loss("final_loss", h, final_norm.reshape(1, d), loss_target[0])

    small = {}
    small_gathers = {}
    exchanges = {}
    token = None

    def rows8(vals):
        at = lax.broadcasted_iota(jnp.int32, (8, d), 0)
        out = jnp.zeros((8, d), f32)
        for k, v in enumerate(vals):
            r0 = v[0:1]
            r0 = r0 if r0.shape[1] == d else jnp.pad(r0, ((0, 0), (0, d - r0.shape[1])))
            out = jnp.where(at == k, r0, out)
        return out

    def exchange(part, layer, grads):
        by_dest = [g.reshape(g.shape[0], NDEV, g.shape[1] // NDEV, g.shape[2]) for g in grads]
        ssem, rsem, sent, lands, tok = exchange_start(f"exchange_start_{part}{layer}", by_dest)
        exchanges[(part, layer)] = (ssem, rsem, sent, lands)
        return (tok,)

    for layer in reversed(range(depth)):
        g1T, u1T, g2T, u2T, winT, wretT, wpoolT, d1, d2, wout = weights[layer]
        row = lambda a: a[layer:layer + 1]
        s = saved[layer]
        dh, lhs2, rhs2, small[("ffn2", layer)] = ffn_bwd(
            f"ffn2_bwd{layer}", dh, s["h2"], row(ffn2_norm), s["a2"], s["b2"], g2T, u2T, d2, () if token is None else token)
        token = exchange("ffn2", layer, [mm_tn(f"ffn2_wgrad{layer}", lhs2, rhs2, lambda b: b // 2)])
        do, dzg, dzgate, dp, drp, dyb = mix_out_bwd(
            f"mix_out_bwd{layer}", dh, s["oraw"], s["zg"], s["zgate"], s["rp"], wretT, wpoolT, wout, token)
        gw_mix = [mm_tn(f"w_out_wgrad{layer}", s["mixed"], dyb, lambda b: b),
                  mm_tn(f"up_wgrad{layer}", drp, s["rp"], lambda b: b)]
        if layer == 0:
            token = exchange("mix_up", layer, gw_mix)
            gw_mix = []
        dq, dk, dv = retention_bwd(f"retention_bwd{layer}", s["zq"], cos2, sin2, dec, s["states"], do, pad, token)
        dzu, small[("maps", layer)], small[("scale", layer)] = pool_bwd(
            f"pool_bwd{layer}", dp, s["pooled"], pool_maps[layer], row(pool_scale), pad)
        dh, dz, n2, small[("mix", layer)] = mix_in_bwd(
            f"mix_in_bwd{layer}", dq, dk, dv, dzg, dzu, dzgate, s["h1"], row(mix_norm), winT, dh)
        token = exchange("mix" if layer else "w_in", layer, gw_mix + [mm_tn(f"w_in_wgrad{layer}", dz, n2, lambda b: b)])
        dh, lhs1, rhs1, small[("ffn1", layer)] = ffn_bwd(
            f"ffn1_bwd{layer}", dh, s["h0"], row(ffn1_norm), s["a1"], s["b1"], g1T, u1T, d1, token)
        rows = [small[("ffn1", layer)], small[("mix", layer)], small[("ffn2", layer)], small[("scale", layer)]]
        packs = [rows8([loss_part, dg_final] + rows if layer == depth - 1 else rows), small[("maps", layer)].astype(bf16)]
        if layer == 0:
            dmeta = dh[pad:CHUNK]
            packs.append(jnp.transpose(dmeta.reshape(N_META, NDEV, d // NDEV), (1, 0, 2)).reshape(NDEV * N_META, d // NDEV))
        ssem, rsem, lands, tok = gather_start(f"small_start{layer}", slot_in(f"small_slot{layer}", me, packs))
        small_gathers[layer] = (ssem, rsem, lands)
        if layer:
            token = exchange("ffn1", layer, [mm_tn(f"ffn1_wgrad{layer}", lhs1, rhs1, lambda b: b // 2, (tok,))])
        else:
            token = (tok,)
            for j, nm in enumerate(("ffn1_gate", "ffn1_up", "ffn1_down")):
                token = exchange(nm, layer, [mm_tn(f"{nm}_wgrad{layer}", lhs1, rhs1, lambda b: b // 2, token, only=j)])

    grad_x = (dh[CHUNK:] + token[0][0, 0])[None]

    big = {}
    after = token[0]
    plans = {
        "ffn2": [("ffn2_gate", 0, 0, False, *t_g2), ("ffn2_up", 0, 1, False, *t_u2),
                 ("ffn2_down", 0, 2, False, ffn2_down, m_ffn2_down, v_ffn2_down)],
        "mix": [("w_out", 0, 0, False, w_out, m_w_out, v_w_out),
                ("w_ret_up", 1, 0, True, w_ret_up, m_w_ret_up, v_w_ret_up),
                ("w_pool_up", 1, 1, True, w_pool_up, m_w_pool_up, v_w_pool_up), ("w_in", 2, 0, False, *t_in)],
        "ffn1": [("ffn1_gate", 0, 0, False, *t_g1), ("ffn1_up", 0, 1, False, *t_u1),
                 ("ffn1_down", 0, 2, False, ffn1_down, m_ffn1_down, v_ffn1_down)]}
    for nm, k, b, tr, w, m, v in plans["ffn1"]:
        plans[nm] = [(nm, 0, 0, tr, w, m, v)]
    plans["mix_up"], plans["w_in"] = plans["mix"][:3], [("w_in", 0, 0, False, *t_in)]
    for layer in reversed(range(depth)):
        for part in ("ffn2", "mix", "ffn1") if layer else ("ffn2", "mix_up", "w_in", "ffn1_gate", "ffn1_up", "ffn1_down"):
            ssem, rsem, sent, lands = exchanges[(part, layer)]
            sent, lands = copies_wait(f"exchange_wait_{part}{layer}", ssem, rsem, sent, lands, after)
            for nm, k, b, tr, w, m, v in plans[part]:
                big[nm] = adam_big(f"adam_{nm}{layer}", me, lands[k], sent[k], b, layer, tr, w, m, v, big.get(nm))
                after = big[nm][0]

    gsmall = []
    for layer in range(depth):
        ssem, rsem, lands = small_gathers[layer]
        gsmall += copies_wait(f"small_wait{layer}", ssem, rsem, (), lands, after)[1]

    maps2 = lambda a: a.reshape(depth * HEADS * HD, HD)
    res = adam_small(
        "adam_small", *gsmall,
        [(ffn1_norm, m_ffn1_norm, v_ffn1_norm), (mix_norm, m_mix_norm, v_mix_norm), (ffn2_norm, m_ffn2_norm, v_ffn2_norm)],
        (pool_scale, m_pool_scale, v_pool_scale), (maps2(pool_maps), maps2(m_pool_maps), maps2(v_pool_maps)),
        (meta, m_meta, v_meta), tuple(a.reshape(1, d) for a in (final_norm, m_final_norm, v_final_norm)), d)
    loss = res[0][0, 0]
    sm = {}
    for k, nm in enumerate(["ffn1_norm", "mix_norm", "ffn2_norm", "pool_scale", "pool_maps", "meta", "final_norm"]):
        sm[nm] = list(res[1 + 4 * k:5 + 4 * k])
    sm["pool_maps"] = [a.reshape(pool_maps.shape) for a in sm["pool_maps"]]
    sm["final_norm"] = [a.reshape(d) for a in sm["final_norm"]]

    names = ["meta", "ffn1_norm", "ffn1_gate", "ffn1_up", "ffn1_down", "mix_norm", "w_in", "pool_maps", "pool_scale",
             "w_ret_up", "w_pool_up", "w_out", "ffn2_norm", "ffn2_gate", "ffn2_up", "ffn2_down", "final_norm"]
    for nm in ("ffn1_gate", "ffn1_up", "ffn2_gate", "ffn2_up", "w_in"):
        big[nm] = tview(*big[nm])
    allw = {**{k: list(v) for k, v in big.items()}, **sm}
    outs = [loss, grad_x]
    for kind in range(4):
        outs += [allw[nm][kind] for nm in names]
    return tuple(outs)
```

```python
import functools

import jax
import jax.numpy as jnp
from jax import lax
from jax.experimental import pallas as pl
from jax.experimental.pallas import tpu as pltpu

f32 = jnp.float32
bf16 = jnp.bfloat16
MESH = pl.DeviceIdType.MESH
NDEV = 8
N_META = 16
HEADS = 4
HD = 128
CHUNK = 128
RW = HEADS * HD
POOL_WINDOWS = (2, 4, 8, 16)
ROPE_BASE = 10000.0
EPS = 1e-6
ADAM_LR = 0.001
ADAM_B1 = 0.9
ADAM_B2 = 0.999
ADAM_EPS = 1e-08
ADAM_WD = 0.01
ADAM_STEP = 10
VMEM_CAP_MB = 60


def _cp(sem=None):
    return pltpu.CompilerParams(vmem_limit_bytes=VMEM_CAP_MB * 2**20, dimension_semantics=sem)


def _row_tile(tp, want=384):
    return want if tp % want == 0 else 128


def _resident(shape):
    nd = len(shape)
    return pl.BlockSpec(shape, lambda *_: (0,) * nd, pipeline_mode=pl.Buffered(1))


def _skip(nd, body):
    return (lambda *refs: body(*refs[nd:])) if nd else body


def _dot_nn(a, b):
    return lax.dot_general(a, b, (((1,), (0,)), ((), ())), preferred_element_type=f32)


def _dot_nt(a, b):
    return lax.dot_general(a, b, (((1,), (1,)), ((), ())), preferred_element_type=f32)


def _dot_tn(a, b):
    return lax.dot_general(a, b, (((0,), (0,)), ((), ())), preferred_element_type=f32)


def _rms(h):
    rs = lax.rsqrt(jnp.mean(h * h, axis=-1, keepdims=True) + EPS)
    return h * rs, rs


def _rms_bwd(dn, g, hhat, rs):
    dhh = dn * g
    return rs * (dhh - hhat * jnp.mean(dhh * hhat, axis=-1, keepdims=True))


def _sigmoid(x):
    return jax.nn.sigmoid(x)


def _me():
    return lax.axis_index("x"), lax.axis_index("y"), lax.axis_index("c")


def _peer(idx):
    return (idx // 4, (idx // 2) % 2, idx % 2)


def all_gather(name, arrays):
    n = len(arrays)

    def body(*refs):
        ins, outs = refs[:n], refs[n:2 * n]
        send_sems, recv_sems, local_sems = refs[2 * n:]
        x, y, c = _me()
        me = 4 * x + 2 * y + c
        locals_ = []
        for k in range(n):
            cp = pltpu.make_async_copy(ins[k], outs[k].at[me], local_sems.at[k])
            cp.start()
            locals_.append(cp)
        for d in range(1, NDEV):
            for k in range(n):
                pltpu.make_async_remote_copy(
                    src_ref=ins[k], dst_ref=outs[k].at[me], send_sem=send_sems.at[k], recv_sem=recv_sems.at[k],
                    device_id=_peer((me + d) % NDEV), device_id_type=MESH).start()
        for k in range(n):
            seven = outs[k].at[pl.ds(0, NDEV - 1)]
            w = pltpu.make_async_remote_copy(src_ref=seven, dst_ref=seven, send_sem=send_sems.at[k],
                                             recv_sem=recv_sems.at[k], device_id=(x, y, c), device_id_type=MESH)
            w.wait_send()
            w.wait_recv()
            locals_[k].wait()

    anyspec = pl.BlockSpec(memory_space=pl.ANY)
    return pl.pallas_call(
        body, name=name,
        out_shape=[jax.ShapeDtypeStruct((NDEV,) + a.shape, a.dtype) for a in arrays],
        in_specs=[anyspec] * n, out_specs=[anyspec] * n,
        scratch_shapes=[pltpu.SemaphoreType.DMA((n,)), pltpu.SemaphoreType.DMA((n,)), pltpu.SemaphoreType.DMA((n,))],
    )(*arrays)


_HBM = pl.BlockSpec(memory_space=pltpu.HBM)
_SEM = pl.BlockSpec(memory_space=pltpu.SEMAPHORE)
_ANY = pl.BlockSpec(memory_space=pl.ANY)
_EFFECT = pltpu.SideEffectType.DATAFLOW_SIDE_EFFECTING


def _in_hbm(a):
    return pltpu.with_memory_space_constraint(a, pltpu.HBM)


def gather_start(name, lands, deps=()):
    n, nd = len(lands), len(deps)

    def body(*refs):
        land = refs[nd:nd + n]
        send_sems, recv_sems = refs[nd + n:nd + n + 2]
        token = refs[-1]
        x, y, c = _me()
        me = 4 * x + 2 * y + c
        for d in range(1, NDEV):
            for k in range(n):
                pltpu.make_async_remote_copy(
                    src_ref=land[k].at[me], dst_ref=land[k].at[me], send_sem=send_sems.at[k], recv_sem=recv_sems.at[k],
                    device_id=_peer((me + d) % NDEV), device_id_type=MESH).start()
        token[...] = jnp.zeros_like(token)

    res = pl.pallas_call(
        body, name=name,
        out_shape=(pltpu.SemaphoreType.DMA((n,)), pltpu.SemaphoreType.DMA((n,)),
                   *[pltpu.HBM(a.shape, a.dtype) for a in lands], jax.ShapeDtypeStruct((8, 128), f32)),
        in_specs=[_ANY] * nd + [_HBM] * n,
        out_specs=(_SEM, _SEM, *[_HBM] * n, pl.BlockSpec(memory_space=pltpu.VMEM)),
        input_output_aliases={nd + k: 2 + k for k in range(n)},
        compiler_params=pltpu.CompilerParams(has_side_effects=_EFFECT),
    )(*deps, *[_in_hbm(a) for a in lands])
    return res[0], res[1], list(res[2:2 + n]), res[-1]


def _other_chips(x, y):
    return [(1 - x, y), (x, 1 - y), (1 - x, 1 - y)]


def gather_start_chips(name, groups, deps=()):
    sizes = [len(g) for g in groups]
    lands = [a for g in groups for a in g]
    n, nd, ng = len(lands), len(deps), len(groups)

    def body(*refs):
        land = refs[nd:nd + n]
        sems = refs[nd + n:nd + n + 2 * ng]
        token = refs[-1]
        x, y, c = _me()
        me = 4 * x + 2 * y + c
        k = 0
        for g, size in enumerate(sizes):
            for j in range(size):
                for to in [(x, y, 1 - c)] + [(cx, cy, c) for cx, cy in _other_chips(x, y)]:
                    pltpu.make_async_remote_copy(
                        src_ref=land[k].at[me], dst_ref=land[k].at[me], send_sem=sems[2 * g].at[j],
                        recv_sem=sems[2 * g + 1].at[j], device_id=to, device_id_type=MESH).start()
                k += 1
        token[...] = jnp.zeros_like(token)

    res = pl.pallas_call(
        body, name=name,
        out_shape=(*[pltpu.SemaphoreType.DMA((size,)) for size in sizes for _ in range(2)],
                   *[pltpu.HBM(a.shape, a.dtype) for a in lands], jax.ShapeDtypeStruct((8, 128), f32)),
        in_specs=[_ANY] * nd + [_HBM] * n,
        out_specs=(*[_SEM] * (2 * ng), *[_HBM] * n, pl.BlockSpec(memory_space=pltpu.VMEM)),
        input_output_aliases={nd + k: 2 * ng + k for k in range(n)},
        compiler_params=pltpu.CompilerParams(has_side_effects=_EFFECT),
    )(*deps, *[_in_hbm(a) for a in lands])
    out, k = [], 2 * ng
    for g, size in enumerate(sizes):
        out.append((res[2 * g], res[2 * g + 1], list(res[k:k + size])))
        k += size
    return out, res[-1]


def gather_forward(name, send_sems, recv_sems, lands, after):
    n = len(lands)

    def body(*refs):
        land = refs[:n]
        ssem, rsem = refs[n:n + 2]
        send2, recv2 = refs[n + 3:n + 5]
        token = refs[-1]
        x, y, c = _me()
        for k in range(n):
            four = land[k].at[pl.ds(0, 4)]
            w = pltpu.make_async_remote_copy(src_ref=four, dst_ref=four, send_sem=ssem.at[k], recv_sem=rsem.at[k],
                                             device_id=(x, y, c), device_id_type=MESH)
            w.wait_send()
            w.wait_recv()
            for cx, cy in _other_chips(x, y):
                slot = 4 * cx + 2 * cy + c
                pltpu.make_async_remote_copy(
                    src_ref=land[k].at[slot], dst_ref=land[k].at[slot], send_sem=send2.at[k], recv_sem=recv2.at[k],
                    device_id=(x, y, 1 - c), device_id_type=MESH).start()
        token[...] = jnp.zeros_like(token)

    res = pl.pallas_call(
        body, name=name,
        out_shape=(pltpu.SemaphoreType.DMA((n,)), pltpu.SemaphoreType.DMA((n,)),
                   *[pltpu.HBM(a.shape, a.dtype) for a in lands], jax.ShapeDtypeStruct((8, 128), f32)),
        in_specs=[_HBM] * n + [_SEM, _SEM, _ANY],
        out_specs=(_SEM, _SEM, *[_HBM] * n, pl.BlockSpec(memory_space=pltpu.VMEM)),
        input_output_aliases={k: 2 + k for k in range(n)},
        compiler_params=pltpu.CompilerParams(has_side_effects=_EFFECT),
    )(*lands, send_sems, recv_sems, after)
    return res[0], res[1], list(res[2:2 + n]), res[-1]


def exchange_start(name, grads, deps=()):
    n, nd = len(grads), len(deps)
    lands = [lax.empty((NDEV - 1, g.shape[0]) + g.shape[2:], g.dtype) for g in grads]

    def body(*refs):
        src = refs[nd:nd + n]
        land = refs[nd + n:nd + 2 * n]
        send_sems, recv_sems = refs[nd + 2 * n:nd + 2 * n + 2]
        token = refs[-1]
        x, y, c = _me()
        me = 4 * x + 2 * y + c
        for d in range(1, NDEV):
            p = (me + d) % NDEV
            for k in range(n):
                pltpu.make_async_remote_copy(
                    src_ref=src[k].at[:, p], dst_ref=land[k].at[d - 1], send_sem=send_sems.at[k], recv_sem=recv_sems.at[k],
                    device_id=_peer(p), device_id_type=MESH).start()
        token[...] = jnp.zeros_like(token)

    both = list(grads) + lands
    res = pl.pallas_call(
        body, name=name,
        out_shape=(pltpu.SemaphoreType.DMA((n,)), pltpu.SemaphoreType.DMA((n,)),
                   *[pltpu.HBM(a.shape, a.dtype) for a in both], jax.ShapeDtypeStruct((8, 128), f32)),
        in_specs=[_ANY] * nd + [_HBM] * (2 * n),
        out_specs=(_SEM, _SEM, *[_HBM] * (2 * n), pl.BlockSpec(memory_space=pltpu.VMEM)),
        input_output_aliases={nd + k: 2 + k for k in range(2 * n)},
        compiler_params=pltpu.CompilerParams(has_side_effects=_EFFECT),
    )(*deps, *[_in_hbm(a) for a in both])
    return res[0], res[1], list(res[2:2 + n]), list(res[2 + n:2 + 2 * n]), res[-1]


def copies_wait(name, send_sems, recv_sems, sent, lands, after, count=NDEV - 1):
    ns, n = len(sent), len(lands)

    def body(*refs):
        land = refs[ns:ns + n]
        ssem, rsem = refs[ns + n:ns + n + 2]
        x, y, c = _me()
        for k in range(n):
            seven = land[k].at[pl.ds(0, count)]
            w = pltpu.make_async_remote_copy(src_ref=seven, dst_ref=seven, send_sem=ssem.at[k], recv_sem=rsem.at[k],
                                             device_id=(x, y, c), device_id_type=MESH)
            w.wait_send()
            w.wait_recv()

    both = list(sent) + list(lands)
    res = pl.pallas_call(
        body, name=name, out_shape=tuple(pltpu.HBM(a.shape, a.dtype) for a in both),
        in_specs=[_HBM] * (ns + n) + [_SEM, _SEM, _ANY], out_specs=tuple([_HBM] * (ns + n)),
        input_output_aliases={k: k for k in range(ns + n)},
        compiler_params=pltpu.CompilerParams(has_side_effects=_EFFECT),
    )(*both, send_sems, recv_sems, after)
    return list(res[:ns]), list(res[ns:])


def prep_layer(layer, me, col_sharded, row_sharded, deps=()):
    nc, nr = len(col_sharded), len(row_sharded)

    def body(me_ref, *refs):
        ins, outs = refs[:nc + nr], refs[nc + nr + len(deps):]
        for k in range(nc):
            outs[k][...] = ins[k][...].T.astype(bf16)
        for k in range(nc, nc + nr):
            outs[k][...] = ins[k][...].astype(bf16)

    arrs = list(col_sharded) + list(row_sharded)
    in_specs = [pl.BlockSpec((None,) + a.shape[1:], lambda i, me_ref: (layer, 0, 0)) for a in arrs]
    shapes = [(a.shape[2], a.shape[1]) for a in col_sharded] + [a.shape[1:] for a in row_sharded]
    out_specs = [pl.BlockSpec((None,) + s, lambda i, me_ref: (me_ref[0], 0, 0)) for s in shapes]
    return pl.pallas_call(
        body, name=f"prep_layer{layer}",
        grid_spec=pltpu.PrefetchScalarGridSpec(num_scalar_prefetch=1, grid=(1,), in_specs=in_specs + [_ANY] * len(deps),
                                               out_specs=out_specs),
        out_shape=[jax.ShapeDtypeStruct((NDEV,) + s, bf16) for s in shapes], compiler_params=_cp())(me, *arrs, *deps)


def slot_in(name, me, arrays):
    n = len(arrays)

    def body(me_ref, *refs):
        for k in range(n):
            refs[n + k][...] = refs[k][...]

    in_specs = [pl.BlockSpec(a.shape, lambda i, me_ref: (0, 0)) for a in arrays]
    out_specs = [pl.BlockSpec((None,) + a.shape, lambda i, me_ref: (me_ref[0], 0, 0)) for a in arrays]
    return pl.pallas_call(
        body, name=name,
        grid_spec=pltpu.PrefetchScalarGridSpec(num_scalar_prefetch=1, grid=(1,), in_specs=in_specs, out_specs=out_specs),
        out_shape=[jax.ShapeDtypeStruct((NDEV,) + a.shape, a.dtype) for a in arrays])(me, *arrays)


def _ff_chunks(ff, want=768):
    if ff % 256:
        return [slice(0, ff)]
    return [slice(c, min(c + want, ff)) for c in range(0, ff, want)]


def ffn_fwd(name, h, g, wgT, wuT, wd, deps=()):
    tp, d = h.shape
    ff = wgT.shape[0]
    tm = _row_tile(tp, 704)

    def body(h_ref, g_ref, wg_ref, wu_ref, wd_ref, ho_ref, a_ref, b_ref):
        hh = h_ref[...]
        hhat, _ = _rms(hh)
        n = (hhat * g_ref[...]).astype(bf16)
        acc = None
        for cols in _ff_chunks(ff):
            a = _dot_nt(n, wg_ref[cols, :])
            b = _dot_nt(n, wu_ref[cols, :])
            part = _dot_nn(((a * _sigmoid(a)) * b).astype(bf16), wd_ref[cols, :])
            acc = part if acc is None else acc + part
            a_ref[:, cols] = a.astype(bf16)
            b_ref[:, cols] = b.astype(bf16)
        ho_ref[...] = hh + 0.5 * acc

    row = lambda w: pl.BlockSpec((tm, w), lambda i: (i, 0))
    return pl.pallas_call(
        _skip(len(deps), body), name=name, grid=(tp // tm,),
        in_specs=[_ANY] * len(deps) + [row(d), _resident((1, d)), _resident((ff, d)), _resident((ff, d)), _resident((ff, d))],
        out_specs=[row(d), row(ff), row(ff)],
        out_shape=[jax.ShapeDtypeStruct((tp, d), f32), jax.ShapeDtypeStruct((tp, ff), bf16),
                   jax.ShapeDtypeStruct((tp, ff), bf16)],
        compiler_params=_cp(("arbitrary",)))(*deps, h, g, wgT, wuT, wd)


def ffn_bwd(name, dy, h, g, a, b, wgT, wuT, wd, deps=()):
    tp, d = h.shape
    ff = wgT.shape[0]
    tm = _row_tile(tp, 384)

    def body(dy_ref, h_ref, g_ref, a_ref, b_ref, wg_ref, wu_ref, wd_ref, dh_ref, lhs_ref, rhs_ref, dg_ref):
        dyv = dy_ref[...]
        hhat, rs = _rms(h_ref[...])
        gv = g_ref[...]
        n = hhat * gv
        dyh = (0.5 * dyv).astype(bf16)
        dn = None
        for cols in _ff_chunks(ff):
            ds = _dot_nt(dyh, wd_ref[cols, :])
            av = a_ref[:, cols].astype(f32)
            bv = b_ref[:, cols].astype(f32)
            sg = _sigmoid(av)
            sa = av * sg
            da = (ds * bv * (sg * (1.0 + av * (1.0 - sg)))).astype(bf16)
            db = (ds * sa).astype(bf16)
            part = _dot_nn(da, wg_ref[cols, :]) + _dot_nn(db, wu_ref[cols, :])
            dn = part if dn is None else dn + part
            lhs_ref[0, :, cols] = da
            lhs_ref[1, :, cols] = db
            lhs_ref[2, :, cols] = (sa * bv).astype(bf16)
        dh_ref[...] = dyv + _rms_bwd(dn, gv, hhat, rs)

        @pl.when(pl.program_id(0) == 0)
        def _():
            dg_ref[...] = jnp.zeros_like(dg_ref)

        dg_ref[0:1, :] += jnp.sum(dn * hhat, axis=0, keepdims=True)
        rhs_ref[0] = n.astype(bf16)
        rhs_ref[1] = dyh

    row = lambda w: pl.BlockSpec((tm, w), lambda i: (i, 0))
    return pl.pallas_call(
        _skip(len(deps), body), name=name, grid=(tp // tm,),
        in_specs=[_ANY] * len(deps) + [row(d), row(d), _resident((1, d)), row(ff), row(ff),
                  _resident((ff, d)), _resident((ff, d)), _resident((ff, d))],
        out_specs=[row(d), pl.BlockSpec((3, tm, ff), lambda i: (0, i, 0)), pl.BlockSpec((2, tm, d), lambda i: (0, i, 0)),
                   pl.BlockSpec((8, d), lambda i: (0, 0))],
        out_shape=[jax.ShapeDtypeStruct((tp, d), f32), jax.ShapeDtypeStruct((3, tp, ff), bf16),
                   jax.ShapeDtypeStruct((2, tp, d), bf16), jax.ShapeDtypeStruct((8, d), f32)],
        compiler_params=_cp(("arbitrary",)))(*deps, dy, h, g, a, b, wgT, wuT, wd)


def mm_tn(name, lhs, rhs, rhs_of, deps=(), only=None):
    _, tp, m = lhs.shape
    b0, nb = (0, lhs.shape[0]) if only is None else (only, 1)
    n = rhs.shape[2]
    def fits(t, ms):
        mb = m // ms
        return (tp % t == 0 and m % (128 * ms) == 0
                and 2 * t * (mb + n) * 2 + mb * n * (2 * 2 + 4 + (4 if t < tp else 0)) <= 54 * 2**20)

    tk, msplit = next(((t, ms) for t in (tp, 1408, 704, 384) for ms in (1, 2, 4) if fits(t, ms)), (128, 1))
    nk = tp // tk
    mb = m // msplit

    def body(l_ref, r_ref, o_ref, acc_ref):
        if nk == 1:
            o_ref[...] = _dot_tn(l_ref[...], r_ref[...]).astype(o_ref.dtype)
            return
        k = pl.program_id(2)

        @pl.when(k == 0)
        def _():
            acc_ref[...] = jnp.zeros_like(acc_ref)

        acc_ref[...] += _dot_tn(l_ref[...], r_ref[...])

        @pl.when(k == nk - 1)
        def _():
            o_ref[...] = acc_ref[...].astype(o_ref.dtype)

    return pl.pallas_call(
        _skip(len(deps), body), name=name, grid=(nb, msplit, nk),
        in_specs=[_ANY] * len(deps) + [pl.BlockSpec((None, tk, mb), lambda b, j, k: (b0 + b, k, j)),
                                       pl.BlockSpec((None, tk, n), lambda b, j, k: (rhs_of(b0 + b), k, 0))],
        out_specs=pl.BlockSpec((None, mb, n), lambda b, j, k: (b, j, 0)),
        out_shape=jax.ShapeDtypeStruct((nb, m, n), bf16),
        scratch_shapes=[pltpu.VMEM((mb, n) if nk > 1 else (8, 128), f32)],
        compiler_params=_cp(("arbitrary", "arbitrary", "arbitrary")))(*deps, lhs, rhs)


def mix_in_fwd(name, h, g, winT, cos2, sin2):
    tp, d = h.shape
    nin = winT.shape[0]
    tm = _row_tile(tp, 704)

    def body(h_ref, g_ref, w_ref, cos_ref, sin_ref, zq_ref, zg_ref, zu_ref, zgate_ref):
        hhat, _ = _rms(h_ref[...])
        z = _dot_nt((hhat * g_ref[...]).astype(bf16), w_ref[...])
        cosv, sinv = cos_ref[...], sin_ref[...]
        for hh in range(HEADS):
            qcols, kcols = slice(hh * HD, (hh + 1) * HD), slice(RW + hh * HD, RW + (hh + 1) * HD)
            zq_ref[:, qcols] = (_rot(z[:, qcols], cosv, sinv) * HD ** -0.5).astype(bf16)
            zq_ref[:, kcols] = _rot(z[:, kcols], cosv, sinv).astype(bf16)
        zq_ref[:, 2 * RW:] = z[:, 2 * RW:3 * RW].astype(bf16)
        zg_ref[...] = z[:, 3 * RW:4 * RW].astype(zg_ref.dtype)
        zu_ref[...] = z[:, 4 * RW:5 * RW]
        zgate_ref[...] = z[:, 5 * RW:].astype(zgate_ref.dtype)

    row = lambda w: pl.BlockSpec((tm, w), lambda i: (i, 0))
    widths = (3 * RW, RW, RW, 2 * d)
    return pl.pallas_call(
        body, name=name, grid=(tp // tm,),
        in_specs=[row(d), _resident((1, d)), _resident((nin, d)), row(HD), row(HD)],
        out_specs=[row(w) for w in widths],
        out_shape=[jax.ShapeDtypeStruct((tp, w), dt) for w, dt in zip(widths, (bf16, bf16, f32, bf16))],
        compiler_params=_cp(("arbitrary",)))(h, g, winT, cos2, sin2)


def mix_in_bwd(name, dq, dk, dv, dzg, dzu, dzgate, h, g, winT, dres):
    tp, d = h.shape
    nin = winT.shape[0]
    tm = _row_tile(tp)

    def body(dq_ref, dk_ref, dv_ref, dzg_ref, dzu_ref, dzgate_ref, h_ref, g_ref, w_ref, dres_ref, dh_ref, dz_ref, n_ref, dg_ref):
        dn, col = None, 0
        for piece in (dq_ref, dk_ref, dv_ref, dzg_ref, dzu_ref, dzgate_ref):
            v = piece[...]
            part = _dot_nn(v, w_ref[col:col + v.shape[1], :])
            dn = part if dn is None else dn + part
            dz_ref[:, col:col + v.shape[1]] = v
            col += v.shape[1]
        hhat, rs = _rms(h_ref[...])
        gv = g_ref[...]
        dh_ref[...] = dres_ref[...] + _rms_bwd(dn, gv, hhat, rs)

        @pl.when(pl.program_id(0) == 0)
        def _():
            dg_ref[...] = jnp.zeros_like(dg_ref)

        dg_ref[0:1, :] += jnp.sum(dn * hhat, axis=0, keepdims=True)
        n_ref[...] = (hhat * gv).astype(bf16)

    row = lambda w: pl.BlockSpec((tm, w), lambda i: (i, 0))
    return pl.pallas_call(
        body, name=name, grid=(tp // tm,),
        in_specs=[row(RW)] * 5 + [row(2 * d), row(d), _resident((1, d)), _resident((nin, d)), row(d)],
        out_specs=[row(d), pl.BlockSpec((None, tm, nin), lambda i: (0, i, 0)), pl.BlockSpec((None, tm, d), lambda i: (0, i, 0)),
                   pl.BlockSpec((8, d), lambda i: (0, 0))],
        out_shape=[jax.ShapeDtypeStruct((tp, d), f32), jax.ShapeDtypeStruct((1, tp, nin), bf16),
                   jax.ShapeDtypeStruct((1, tp, d), bf16), jax.ShapeDtypeStruct((8, d), f32)],
        compiler_params=_cp(("arbitrary",)))(dq, dk, dv, dzg, dzu, dzgate, h, g, winT, dres)


def _retention_tables(tp, pad, zero):
    half = HD // 2
    lane = jnp.arange(HD)
    inv_freq = ROPE_BASE ** (-(lane % half).astype(f32) / half)
    pos = jnp.arange(tp, dtype=f32) - pad + zero
    ang = pos[:, None] * inv_freq[None, :]
    cos2 = jnp.cos(ang)
    sin2 = jnp.where(lane[None, :] < half, -1.0, 1.0) * jnp.sin(ang)
    log_gamma = jnp.log1p(-(2.0 ** (-5.0 - jnp.arange(HEADS, dtype=f32))))
    idx = jnp.arange(CHUNK, dtype=f32)
    diff = idx[:, None] - idx[None, :]
    intra = jnp.where(diff[None] >= 0, jnp.exp(diff[None] * log_gamma[:, None, None]), 0.0)
    k_decay = jnp.exp((CHUNK - 1.0 - idx)[None, :] * log_gamma[:, None])
    q_decay = jnp.exp((idx + 1.0)[None, :] * log_gamma[:, None])
    chunk_decay = jnp.exp(CHUNK * log_gamma)
    full = (HEADS, CHUNK, HD)
    dec = jnp.stack([intra, jnp.broadcast_to(k_decay[:, :, None], full), jnp.broadcast_to(q_decay[:, :, None], full),
                     jnp.broadcast_to(chunk_decay[:, None, None], full)], axis=1)
    return cos2, sin2, dec


def _rot(t, cos2, sin2):
    return t * cos2 + pltpu.roll(t, HD // 2, 1) * sin2


def _rot_t(t, cos2, sin2):
    return t * cos2 - pltpu.roll(t, HD // 2, 1) * sin2


def _chunks_per_step(nch):
    return 3 if nch % 3 == 0 else 1


def retention_fwd(name, zq, dec):
    tp = zq.shape[0]
    nch = tp // CHUNK
    per = _chunks_per_step(nch)

    def body(q_ref, k_ref, v_ref, dec_ref, out_ref, st_ref, s_ref):
        @pl.when(pl.program_id(0) == 0)
        def _():
            s_ref[...] = jnp.zeros_like(s_ref)

        state = [s_ref[hh] for hh in range(HEADS)]
        for j in range(per):
            rows = slice(j * CHUNK, (j + 1) * CHUNK)
            for hh in range(HEADS):
                cols = slice(hh * HD, (hh + 1) * HD)
                qb, kb, vb = q_ref[rows, cols], k_ref[rows, cols], v_ref[rows, cols]
                sc = (_dot_nt(qb, kb) * dec_ref[hh, 0]).astype(bf16)
                sb = state[hh].astype(bf16)
                cross = _dot_nn((qb.astype(f32) * dec_ref[hh, 2]).astype(bf16), sb)
                out_ref[rows, cols] = (_dot_nn(sc, vb) + cross).astype(out_ref.dtype)
                st_ref[hh, j] = sb
                state[hh] = state[hh] * dec_ref[hh, 3] + _dot_tn((kb.astype(f32) * dec_ref[hh, 1]).astype(bf16), vb)
        for hh in range(HEADS):
            s_ref[hh] = state[hh]

    part = lambda j: pl.BlockSpec((per * CHUNK, RW), lambda n: (n, j))
    return pl.pallas_call(
        body, name=name, grid=(nch // per,),
        in_specs=[part(0), part(1), part(2), _resident((HEADS, 4, CHUNK, HD))],
        out_specs=[part(0), pl.BlockSpec((HEADS, per, HD, HD), lambda n: (0, n, 0, 0))],
        out_shape=[jax.ShapeDtypeStruct((tp, RW), bf16), jax.ShapeDtypeStruct((HEADS, nch, HD, HD), bf16)],
        scratch_shapes=[pltpu.VMEM((HEADS, HD, HD), f32)],
        compiler_params=_cp(("arbitrary",)))(zq, zq, zq, dec)


def retention_bwd(name, zq, cos2, sin2, dec, states, dout, pad, deps=()):
    tp = zq.shape[0]
    nch = tp // CHUNK
    per = _chunks_per_step(nch)
    nblk = nch // per
    scale = HD ** -0.5

    def body(q_ref, k_ref, v_ref, cos_ref, sin_ref, dec_ref, st_ref, do_ref, dq_ref, dk_ref, dv_ref, g_ref):
        @pl.when(pl.program_id(0) == 0)
        def _():
            g_ref[...] = jnp.zeros_like(g_ref)

        first_row = (nblk - 1 - pl.program_id(0)) * (per * CHUNK)
        gstate = [g_ref[hh] for hh in range(HEADS)]
        for j in reversed(range(per)):
            rows = slice(j * CHUNK, (j + 1) * CHUNK)
            cosv, sinv = cos_ref[rows, :], sin_ref[rows, :]
            keep = (lax.broadcasted_iota(jnp.int32, (CHUNK, HD), 0) + (first_row + j * CHUNK)) >= pad
            for hh in range(HEADS):
                cols = slice(hh * HD, (hh + 1) * HD)
                intra, kdec, qdec = dec_ref[hh, 0], dec_ref[hh, 1], dec_ref[hh, 2]
                qb, kb, vb = q_ref[rows, cols], k_ref[rows, cols], v_ref[rows, cols]
                qd = (qb.astype(f32) * qdec).astype(bf16)
                kd = (kb.astype(f32) * kdec).astype(bf16)
                sc = (_dot_nt(qb, kb) * intra).astype(bf16)
                dob = do_ref[rows, cols]
                sb = st_ref[hh, j]
                gb = gstate[hh].astype(bf16)
                dsc = (_dot_nt(dob, vb) * intra).astype(bf16)
                dv = _dot_tn(sc, dob) + _dot_nn(kd, gb)
                dqr = _dot_nn(dsc, kb) + _dot_nt(dob, sb) * qdec
                dkr = _dot_tn(dsc, qb) + _dot_nt(vb, gb) * kdec
                gstate[hh] = gstate[hh] * dec_ref[hh, 3] + _dot_tn(qd, dob)
                dq_ref[rows, cols] = jnp.where(keep, _rot_t(dqr * scale, cosv, sinv), 0.0).astype(bf16)
                dk_ref[rows, cols] = jnp.where(keep, _rot_t(dkr, cosv, sinv), 0.0).astype(bf16)
                dv_ref[rows, cols] = jnp.where(keep, dv, 0.0).astype(bf16)
        for hh in range(HEADS):
            g_ref[hh] = gstate[hh]

    part = lambda j: pl.BlockSpec((per * CHUNK, RW), lambda t: (nblk - 1 - t, j))
    table = pl.BlockSpec((per * CHUNK, HD), lambda t: (nblk - 1 - t, 0))
    return pl.pallas_call(
        _skip(len(deps), body), name=name, grid=(nblk,),
        in_specs=[_ANY] * len(deps) + [part(0), part(1), part(2), table, table, _resident((HEADS, 4, CHUNK, HD)),
                                       pl.BlockSpec((HEADS, per, HD, HD), lambda t: (0, nblk - 1 - t, 0, 0)), part(0)],
        out_specs=[part(0)] * 3,
        out_shape=[jax.ShapeDtypeStruct((tp, RW), bf16)] * 3,
        scratch_shapes=[pltpu.VMEM((HEADS, HD, HD), f32)],
        compiler_params=_cp(("arbitrary",)))(*deps, zq, zq, zq, cos2, sin2, dec, states, dout)


def _window_sum(xv, steps, tp, forward):
    s = xv
    for j in range(steps):
        sh = 2 ** j
        s = s + pltpu.roll(s, (tp - sh) if forward else sh, 0)
    return s


def pool_fwd(name, zu, maps, scale, pad):
    tp = zu.shape[0]

    def body(u_ref, maps_ref, scale_ref, pooled_ref, p_ref):
        row = lax.broadcasted_iota(jnp.int32, (tp, HD), 0)
        for gi, w in enumerate(POOL_WINDOWS):
            cols = slice(gi * HD, (gi + 1) * HD)
            xv = u_ref[:, cols]
            cnt = jnp.clip(row - (pad - 1), 1, w).astype(f32)
            pooled = jnp.where(row >= pad, _window_sum(xv, gi + 1, tp, False) / cnt - xv, 0.0).astype(bf16)
            pooled_ref[:, cols] = pooled
            p_ref[:, cols] = (_dot_nn(pooled, maps_ref[gi].astype(bf16)) * scale_ref[:, cols]).astype(bf16)

    return pl.pallas_call(
        body, name=name,
        out_shape=[jax.ShapeDtypeStruct((tp, RW), bf16), jax.ShapeDtypeStruct((tp, RW), bf16)],
        compiler_params=_cp())(zu, maps, scale)


def pool_bwd(name, dp, pooled, maps, scale, pad):
    tp = dp.shape[0]

    def body(dp_ref, pooled_ref, maps_ref, scale_ref, du_ref, dmaps_ref, dscale_ref):
        row = lax.broadcasted_iota(jnp.int32, (tp, HD), 0)
        dscale_ref[...] = jnp.zeros_like(dscale_ref)
        for gi, w in enumerate(POOL_WINDOWS):
            cols = slice(gi * HD, (gi + 1) * HD)
            mb = maps_ref[gi].astype(bf16)
            pooled = pooled_ref[:, cols]
            dpf = dp_ref[:, cols].astype(f32)
            dscale_ref[0:1, cols] = jnp.sum(dpf * _dot_nn(pooled, mb), axis=0, keepdims=True)
            dpm = (dpf * scale_ref[:, cols]).astype(bf16)
            dmaps_ref[gi * HD:(gi + 1) * HD, :] = _dot_tn(pooled, dpm)
            dpool = jnp.where(row >= pad, _dot_nt(dpm, mb), 0.0)
            cnt = jnp.clip(row - (pad - 1), 1, w).astype(f32)
            du = _window_sum(dpool / cnt, gi + 1, tp, True) - dpool
            du_ref[:, cols] = jnp.where(row >= pad, du, 0.0).astype(bf16)

    return pl.pallas_call(
        body, name=name,
        out_shape=[jax.ShapeDtypeStruct((tp, RW), bf16), jax.ShapeDtypeStruct((HEADS * HD, HD), f32),
                   jax.ShapeDtypeStruct((8, RW), f32)],
        compiler_params=_cp())(dp, pooled, maps, scale)


def _group_norm(o):
    mu = jnp.mean(o, axis=-1, keepdims=True)
    oc = o - mu
    rstd = lax.rsqrt(jnp.mean(oc * oc, axis=-1, keepdims=True) + EPS)
    return oc * rstd, rstd


def mix_out_fwd(name, h, oraw, zg, zgate, p, wretT, wpoolT, wout, deps=()):
    tp, d = h.shape
    tm = _row_tile(tp, 704)

    def body(h_ref, o_ref, zg_ref, zgate_ref, p_ref, wr_ref, wp_ref, wo_ref, ho_ref, rp_ref, mixed_ref):
        parts = []
        for hh in range(HEADS):
            cols = slice(hh * HD, (hh + 1) * HD)
            rhat, _ = _group_norm(o_ref[:, cols].astype(f32))
            gv = zg_ref[:, cols].astype(f32)
            parts.append(rhat * (gv * _sigmoid(gv)))
        r = jnp.concatenate(parts, axis=-1).astype(bf16)
        pv = p_ref[...]
        ret = _dot_nt(r, wr_ref[...])
        pool = _dot_nt(pv, wp_ref[...])
        mixed = (_sigmoid(zgate_ref[:, :d].astype(f32)) * ret + _sigmoid(zgate_ref[:, d:].astype(f32)) * pool).astype(bf16)
        ho_ref[...] = h_ref[...] + _dot_nn(mixed, wo_ref[...])
        rp_ref[0] = r
        rp_ref[1] = pv
        mixed_ref[...] = mixed

    row = lambda w: pl.BlockSpec((tm, w), lambda i: (i, 0))
    return pl.pallas_call(
        _skip(len(deps), body), name=name, grid=(tp // tm,),
        in_specs=[_ANY] * len(deps) + [row(d), row(RW), row(RW), row(2 * d), row(RW), _resident((d, RW)), _resident((d, RW)),
                                       _resident((d, d))],
        out_specs=[row(d), pl.BlockSpec((2, tm, RW), lambda i: (0, i, 0)), pl.BlockSpec((None, tm, d), lambda i: (0, i, 0))],
        out_shape=[jax.ShapeDtypeStruct((tp, d), f32), jax.ShapeDtypeStruct((2, tp, RW), bf16),
                   jax.ShapeDtypeStruct((1, tp, d), bf16)],
        compiler_params=_cp(("arbitrary",)))(*deps, h, oraw, zg, zgate, p, wretT, wpoolT, wout)


def mix_out_bwd(name, dy, oraw, zg, zgate, rp, wretT, wpoolT, wout, deps=()):
    tp, d = dy.shape
    tm = _row_tile(tp, 704)

    def body(dy_ref, o_ref, zg_ref, zgate_ref, rp_ref, wr_ref, wp_ref, wo_ref,
             do_ref, dzg_ref, dzgate_ref, dp_ref, drp_ref, dyb_ref):
        dyb = dy_ref[...].astype(bf16)
        dmixed = _dot_nt(dyb, wo_ref[...])
        sa = _sigmoid(zgate_ref[:, :d].astype(f32))
        sb = _sigmoid(zgate_ref[:, d:].astype(f32))
        dret = dmixed * sa
        dpool = dmixed * sb
        dzgate_ref[:, :d] = (dret * _dot_nt(rp_ref[0], wr_ref[...]) * (1.0 - sa)).astype(bf16)
        dzgate_ref[:, d:] = (dpool * _dot_nt(rp_ref[1], wp_ref[...]) * (1.0 - sb)).astype(bf16)
        dretb, dpoolb = dret.astype(bf16), dpool.astype(bf16)
        dr = _dot_nn(dretb, wr_ref[...])
        dp_ref[...] = _dot_nn(dpoolb, wp_ref[...]).astype(bf16)
        for hh in range(HEADS):
            cols = slice(hh * HD, (hh + 1) * HD)
            rhat, rstd = _group_norm(o_ref[:, cols].astype(f32))
            gv = zg_ref[:, cols].astype(f32)
            sg = _sigmoid(gv)
            drh = dr[:, cols]
            drhat = drh * (gv * sg)
            dzg_ref[:, cols] = (drh * rhat * (sg * (1.0 + gv * (1.0 - sg)))).astype(bf16)
            do = rstd * (drhat - jnp.mean(drhat, axis=-1, keepdims=True)
                         - rhat * jnp.mean(drhat * rhat, axis=-1, keepdims=True))
            do_ref[:, cols] = do.astype(bf16)
        drp_ref[0] = dretb
        drp_ref[1] = dpoolb
        dyb_ref[...] = dyb

    row = lambda w: pl.BlockSpec((tm, w), lambda i: (i, 0))
    return pl.pallas_call(
        _skip(len(deps), body), name=name, grid=(tp // tm,),
        in_specs=[_ANY] * len(deps) + [row(d), row(RW), row(RW), row(2 * d), pl.BlockSpec((2, tm, RW), lambda i: (0, i, 0)),
                                       _resident((d, RW)), _resident((d, RW)), _resident((d, d))],
        out_specs=[row(RW), row(RW), row(2 * d), row(RW), pl.BlockSpec((2, tm, d), lambda i: (0, i, 0)),
                   pl.BlockSpec((None, tm, d), lambda i: (0, i, 0))],
        out_shape=[jax.ShapeDtypeStruct((tp, RW), bf16), jax.ShapeDtypeStruct((tp, RW), bf16),
                   jax.ShapeDtypeStruct((tp, 2 * d), bf16), jax.ShapeDtypeStruct((tp, RW), bf16),
                   jax.ShapeDtypeStruct((2, tp, d), bf16), jax.ShapeDtypeStruct((1, tp, d), bf16)],
        compiler_params=_cp(("arbitrary",)))(*deps, dy, oraw, zg, zgate, rp, wretT, wpoolT, wout)


def final_loss(name, h, g, target):
    tp, d = h.shape
    tm = _row_tile(tp, 1408)
    nsub = tm // CHUNK

    def body(h_ref, g_ref, *rest):
        t_refs = rest[:nsub]
        dh_ref, loss_ref, dg_ref = rest[nsub:]
        i = pl.program_id(0)

        @pl.when(i == 0)
        def _():
            loss_ref[...] = jnp.zeros_like(loss_ref)
            dg_ref[...] = jnp.zeros_like(dg_ref)

        gv = g_ref[...]
        for j in range(nsub):
            rows = slice(j * CHUNK, (j + 1) * CHUNK)
            hhat, rs = _rms(h_ref[rows, :])
            err = jnp.where(i * nsub + j >= 1, hhat * gv - t_refs[j][...], 0.0)
            dyv = err / d
            dh_ref[rows, :] = _rms_bwd(dyv, gv, hhat, rs)
            loss_ref[...] += 0.5 * jnp.sum(jnp.sum(err * err, axis=-1, keepdims=True) / d)
            dg_ref[0:1, :] += jnp.sum(dyv * hhat, axis=0, keepdims=True)

    lagged = lambda j: pl.BlockSpec((CHUNK, d), lambda i: (jnp.maximum(i * nsub + j - 1, 0), 0))
    return pl.pallas_call(
        body, name=name, grid=(tp // tm,),
        in_specs=[pl.BlockSpec((tm, d), lambda i: (i, 0)), _resident((1, d))] + [lagged(j) for j in range(nsub)],
        out_specs=[pl.BlockSpec((tm, d), lambda i: (i, 0)), pl.BlockSpec((8, 128), lambda i: (0, 0)),
                   pl.BlockSpec((8, d), lambda i: (0, 0))],
        out_shape=[jax.ShapeDtypeStruct((tp, d), f32), jax.ShapeDtypeStruct((8, 128), f32),
                   jax.ShapeDtypeStruct((8, d), f32)],
        compiler_params=_cp(("arbitrary",)))(h, g, *[target] * nsub)


def _adamw(w, g, m, v):
    m = ADAM_B1 * m + (1.0 - ADAM_B1) * g
    v = ADAM_B2 * v + (1.0 - ADAM_B2) * (g * g)
    m_hat = m / (1.0 - ADAM_B1 ** ADAM_STEP)
    v_hat = v / (1.0 - ADAM_B2 ** ADAM_STEP)
    delta = -ADAM_LR * (m_hat / (jnp.sqrt(v_hat) + ADAM_EPS) + ADAM_WD * w)
    return delta, m, v


def adam_big(name, me, recv, own, b, layer, transposed, w, m, v, prev):
    r, c = recv.shape[2:]
    wshape = w.shape[1:]
    nchunk = 1 if transposed else next(k for k in (4, 2, 1) if r % (16 * k) == 0)
    rc = r // nchunk

    def body(me_ref, recv_ref, own_ref, w_ref, m_ref, v_ref, *rest):
        g_ref, d_ref, nm_ref, nv_ref = rest[-4:]
        g = own_ref[...].astype(f32)
        for j in range(NDEV - 1):
            g = g + recv_ref[j].astype(f32)
        if transposed:
            g = g.T
        delta, nm, nv = _adamw(w_ref[...], g, m_ref[...], v_ref[...])
        g_ref[...] = g
        d_ref[...] = delta
        nm_ref[...] = nm
        nv_ref[...] = nv

    wblock = wshape if transposed else (rc, c)
    wspec = pl.BlockSpec((None,) + wblock, lambda i, me_ref: (layer, i, 0))
    in_specs = [pl.BlockSpec((NDEV - 1, None, rc, c), lambda i, me_ref: (0, b, i, 0)),
                pl.BlockSpec((None, None, rc, c), lambda i, me_ref: (b, me_ref[0], i, 0)), wspec, wspec, wspec]
    args = [recv, own, w, m, v]
    aliases = {}
    if prev is not None:
        in_specs += [_ANY] * 4
        args += list(prev)
        aliases = {6 + k: k for k in range(4)}
    return pl.pallas_call(
        body, name=name,
        grid_spec=pltpu.PrefetchScalarGridSpec(num_scalar_prefetch=1, grid=(nchunk,), in_specs=in_specs,
                                               out_specs=[wspec] * 4),
        out_shape=[jax.ShapeDtypeStruct(w.shape, f32)] * 4, input_output_aliases=aliases,
        compiler_params=_cp())(me, *args)


def adam_small(name, ga0, gmaps0, gmeta, ga1, gmaps1, norms, pool_scale, pool_maps, meta, final_norm, d):
    def body(ga0_ref, gmaps0_ref, gmeta_ref, ga1_ref, gmaps1_ref, *refs):
        ins, outs = refs[:21], refs[21:]
        x, y, c = _me()
        me = 4 * x + 2 * y + c

        def total(ref, rows):
            t = ref[0, rows, :].astype(f32)
            for j in range(1, NDEV):
                t = t + ref[j, rows, :].astype(f32)
            return t

        row = lambda r: slice(r, r + 1)
        outs[0][...] = jnp.broadcast_to(total(ga1_ref, row(0))[:, :128], (8, 128))

        def update(k, g, o):
            w_ref, m_ref, v_ref = ins[3 * k:3 * k + 3]
            delta, nm, nv = _adamw(w_ref[...], g, m_ref[...], v_ref[...])
            for ref, val in zip(outs[o:o + 4], (g, delta, nm, nv)):
                ref[...] = val

        two = lax.broadcasted_iota(jnp.int32, (2, d), 0)
        for k in range(3):
            update(k, jnp.where(two == 0, total(ga0_ref, row(k)), total(ga1_ref, row(2 + k))), 1 + 4 * k)
        update(3, jnp.where(two[:, :RW] == 0, total(ga0_ref, row(3))[:, :RW], total(ga1_ref, row(5))[:, :RW]), 13)
        update(4, jnp.concatenate([total(gmaps0_ref, slice(None)), total(gmaps1_ref, slice(None))], axis=0), 17)
        update(5, total(gmeta_ref, pl.ds(pl.multiple_of(me * N_META, N_META), N_META)), 21)
        update(6, total(ga1_ref, row(1)), 25)

    flat = []
    for trip in (*norms, pool_scale, pool_maps, meta, final_norm):
        flat += list(trip)
    out_shapes = [jax.ShapeDtypeStruct((8, 128), f32)]
    for trip in (*norms, pool_scale, pool_maps, meta, final_norm):
        out_shapes += [jax.ShapeDtypeStruct(trip[0].shape, f32)] * 4
    return pl.pallas_call(body, name=name, out_shape=out_shapes,
                          compiler_params=_cp())(ga0, gmaps0, gmeta, ga1, gmaps1, *flat)


def kernel(x, meta, ffn1_norm, ffn1_gate, ffn1_up, ffn1_down, mix_norm, w_in, pool_maps, pool_scale, w_ret_up, w_pool_up, w_out, ffn2_norm, ffn2_gate, ffn2_up, ffn2_down, final_norm, loss_target, m_meta, m_ffn1_norm, m_ffn1_gate, m_ffn1_up, m_ffn1_down, m_mix_norm, m_w_in, m_pool_maps, m_pool_scale, m_w_ret_up, m_w_pool_up, m_w_out, m_ffn2_norm, m_ffn2_gate, m_ffn2_up, m_ffn2_down, m_final_norm, v_meta, v_ffn1_norm, v_ffn1_gate, v_ffn1_up, v_ffn1_down, v_mix_norm, v_w_in, v_pool_maps, v_pool_scale, v_w_ret_up, v_w_pool_up, v_w_out, v_ffn2_norm, v_ffn2_gate, v_ffn2_up, v_ffn2_down, v_final_norm):
    seq, d = x.shape[1], x.shape[2]
    depth = ffn1_gate.shape[0]
    ff = ffn1_gate.shape[2] * NDEV
    nin = w_in.shape[2] * NDEV
    length = seq + N_META
    pad = (-length) % CHUNK
    tp = length + pad
    assert pad % 8 == 0 and pad + N_META == CHUNK and depth == 2 and nin == 5 * RW + 2 * d

    ix, iy, ic = _me()
    me = (4 * ix + 2 * iy + ic).astype(jnp.int32).reshape(1)

    meta_all, = all_gather("gather_meta", [meta])
    meta_full = jnp.transpose(meta_all, (1, 0, 2)).reshape(N_META, d)

    token = meta_all
    tview = lambda *arrs: [jnp.swapaxes(a, 1, 2) for a in arrs]
    t_g1, t_u1, t_g2, t_u2, t_in = (tview(w, m, v) for w, m, v in (
        (ffn1_gate, m_ffn1_gate, v_ffn1_gate), (ffn1_up, m_ffn1_up, v_ffn1_up), (ffn2_gate, m_ffn2_gate, v_ffn2_gate),
        (ffn2_up, m_ffn2_up, v_ffn2_up), (w_in, m_w_in, v_w_in)))
    keys, groups = [], []
    for layer in range(depth):
        lands = prep_layer(layer, me, [w_ret_up, w_pool_up],
                           [t_g1[0], t_u1[0], t_g2[0], t_u2[0], t_in[0], ffn1_down, ffn2_down, w_out],
                           (token,) if layer else ())
        wretT, wpoolT, g1T, u1T, g2T, u2T, winT, d1, d2, wout = lands
        keys += [("ffn1", layer), ("mix", layer), ("ffn2", layer)]
        groups += [[g1T, u1T, d1], [winT, wretT, wpoolT, wout], [g2T, u2T, d2]]
        if layer == 0:
            first, token = gather_start_chips("gather_start_first", groups[:1], (token,))
    second, token = gather_start_chips("gather_start_second", groups[1:2], (token,))
    gathers = dict(zip(keys, first + second))

    def forward(part, layer, after):
        ssem, rsem, group = gathers[(part, layer)]
        ssem, rsem, group, tok = gather_forward(f"gather_forward_{part}{layer}", ssem, rsem, group, after)
        gathers[(part, layer)] = (ssem, rsem, group)
        return tok

    def gathered(part, layer, after):
        ssem, rsem, group = gathers[(part, layer)]
        _, full = copies_wait(f"gather_wait_{part}{layer}", ssem, rsem, (), group, after, 3)
        return [a.reshape((NDEV * a.shape[1],) + a.shape[2:]) for a in full]

    cos2, sin2, dec = _retention_tables(tp, pad, token[0, 0])
    h = jnp.concatenate([jnp.zeros((pad, d), f32), meta_full + (token[0, 0] + 0.0 * cos2[0, 0] + 0.0 * dec[0, 0, 0, 0]), x[0]],
                        axis=0)

    saved = []
    weights = []
    tok = forward("ffn1", 0, h)
    rest, tok = gather_start_chips("gather_start_rest", groups[2:], (tok,))
    gathers.update(zip(keys[2:], rest))
    for layer in range(depth):
        row = lambda a: a[layer:layer + 1]
        s = {"h0": h}
        g1T, u1T, d1 = gathered("ffn1", layer, tok if layer == 0 else h)
        tok = forward("mix", layer, h) if layer else None
        h, s["a1"], s["b1"] = ffn_fwd(f"ffn1_fwd{layer}", h, row(ffn1_norm), g1T, u1T, d1, (tok,) if layer else ())
        s["h1"] = h
        if layer == 0:
            tok = forward("mix", layer, h)
        winT, wretT, wpoolT, wout = gathered("mix", layer, tok if layer == 0 else h)
        s["zq"], s["zg"], zu, s["zgate"] = mix_in_fwd(f"mix_in_fwd{layer}", h, row(mix_norm), winT, cos2, sin2)
        s["oraw"], s["states"] = retention_fwd(f"retention_fwd{layer}", s["zq"], dec)
        s["pooled"], p = pool_fwd(f"pool_fwd{layer}", zu, pool_maps[layer], row(pool_scale), pad)
        tok = forward("ffn2", layer, p)
        h, s["rp"], s["mixed"] = mix_out_fwd(
            f"mix_out_fwd{layer}", h, s["oraw"], s["zg"], s["zgate"], p, wretT, wpoolT, wout, (tok,))
        s["h2"] = h
        g2T, u2T, d2 = gathered("ffn2", layer, h)
        tok = (forward("ffn1", layer + 1, h),) if layer + 1 < depth else ()
        h, s["a2"], s["b2"] = ffn_fwd(f"ffn2_fwd{layer}", h, row(ffn2_norm), g2T, u2T, d2, tok)
        saved.append(s)
        weights.append((g1T, u1T, g2T, u2T, winT, wretT, wpoolT, d1, d2, wout))

    dh, loss_part, dg_final = final_loss("final_loss", h, final_norm.reshape(1, d), loss_target[0])

    small = {}
    small_gathers = {}
    exchanges = {}
    token = None

    def rows8(vals):
        at = lax.broadcasted_iota(jnp.int32, (8, d), 0)
        out = jnp.zeros((8, d), f32)
        for k, v in enumerate(vals):
            r0 = v[0:1]
            r0 = r0 if r0.shape[1] == d else jnp.pad(r0, ((0, 0), (0, d - r0.shape[1])))
            out = jnp.where(at == k, r0, out)
        return out

    def exchange(part, layer, grads):
        by_dest = [g.reshape(g.shape[0], NDEV, g.shape[1] // NDEV, g.shape[2]) for g in grads]
        ssem, rsem, sent, lands, tok = exchange_start(f"exchange_start_{part}{layer}", by_dest)
        exchanges[(part, layer)] = (ssem, rsem, sent, lands)
        return (tok,)

    for layer in reversed(range(depth)):
        g1T, u1T, g2T, u2T, winT, wretT, wpoolT, d1, d2, wout = weights[layer]
        row = lambda a: a[layer:layer + 1]
        s = saved[layer]
        dh, lhs2, rhs2, small[("ffn2", layer)] = ffn_bwd(
            f"ffn2_bwd{layer}", dh, s["h2"], row(ffn2_norm), s["a2"], s["b2"], g2T, u2T, d2, () if token is None else token)
        token = exchange("ffn2", layer, [mm_tn(f"ffn2_wgrad{layer}", lhs2, rhs2, lambda b: b // 2)])
        do, dzg, dzgate, dp, drp, dyb = mix_out_bwd(
            f"mix_out_bwd{layer}", dh, s["oraw"], s["zg"], s["zgate"], s["rp"], wretT, wpoolT, wout, token)
        gw_mix = [mm_tn(f"w_out_wgrad{layer}", s["mixed"], dyb, lambda b: b),
                  mm_tn(f"up_wgrad{layer}", drp, s["rp"], lambda b: b)]
        if layer == 0:
            token = exchange("mix_up", layer, gw_mix)
            gw_mix = []
        dq, dk, dv = retention_bwd(f"retention_bwd{layer}", s["zq"], cos2, sin2, dec, s["states"], do, pad, token)
        dzu, small[("maps", layer)], small[("scale", layer)] = pool_bwd(
            f"pool_bwd{layer}", dp, s["pooled"], pool_maps[layer], row(pool_scale), pad)
        dh, dz, n2, small[("mix", layer)] = mix_in_bwd(
            f"mix_in_bwd{layer}", dq, dk, dv, dzg, dzu, dzgate, s["h1"], row(mix_norm), winT, dh)
        token = exchange("mix" if layer else "w_in", layer, gw_mix + [mm_tn(f"w_in_wgrad{layer}", dz, n2, lambda b: b)])
        dh, lhs1, rhs1, small[("ffn1", layer)] = ffn_bwd(
            f"ffn1_bwd{layer}", dh, s["h0"], row(ffn1_norm), s["a1"], s["b1"], g1T, u1T, d1, token)
        rows = [small[("ffn1", layer)], small[("mix", layer)], small[("ffn2", layer)], small[("scale", layer)]]
        packs = [rows8([loss_part, dg_final] + rows if layer == depth - 1 else rows), small[("maps", layer)].astype(bf16)]
        if layer == 0:
            dmeta = dh[pad:CHUNK]
            packs.append(jnp.transpose(dmeta.reshape(N_META, NDEV, d // NDEV), (1, 0, 2)).reshape(NDEV * N_META, d // NDEV))
        ssem, rsem, lands, tok = gather_start(f"small_start{layer}", slot_in(f"small_slot{layer}", me, packs))
        small_gathers[layer] = (ssem, rsem, lands)
        if layer:
            token = exchange("ffn1", layer, [mm_tn(f"ffn1_wgrad{layer}", lhs1, rhs1, lambda b: b // 2, (tok,))])
        else:
            token = (tok,)
            for j, nm in enumerate(("ffn1_gate", "ffn1_up", "ffn1_down")):
                token = exchange(nm, layer, [mm_tn(f"{nm}_wgrad{layer}", lhs1, rhs1, lambda b: b // 2, token, only=j)])

    grad_x = (dh[CHUNK:] + token[0][0, 0])[None]

    big = {}
    after = token[0]
    plans = {
        "ffn2": [("ffn2_gate", 0, 0, False, *t_g2), ("ffn2_up", 0, 1, False, *t_u2),
                 ("ffn2_down", 0, 2, False, ffn2_down, m_ffn2_down, v_ffn2_down)],
        "mix": [("w_out", 0, 0, False, w_out, m_w_out, v_w_out),
                ("w_ret_up", 1, 0, True, w_ret_up, m_w_ret_up, v_w_ret_up),
                ("w_pool_up", 1, 1, True, w_pool_up, m_w_pool_up, v_w_pool_up), ("w_in", 2, 0, False, *t_in)],
        "ffn1": [("ffn1_gate", 0, 0, False, *t_g1), ("ffn1_up", 0, 1, False, *t_u1),
                 ("ffn1_down", 0, 2, False, ffn1_down, m_ffn1_down, v_ffn1_down)]}
    for nm, k, b, tr, w, m, v in plans["ffn1"]:
        plans[nm] = [(nm, 0, 0, tr, w, m, v)]
    plans["mix_up"], plans["w_in"] = plans["mix"][:3], [("w_in", 0, 0, False, *t_in)]
    for layer in reversed(range(depth)):
        for part in ("ffn2", "mix", "ffn1") if layer else ("ffn2", "mix_up", "w_in", "ffn1_gate", "ffn1_up", "ffn1_down"):
            ssem, rsem, sent, lands = exchanges[(part, layer)]
            sent, lands = copies_wait(f"exchange_wait_{part}{layer}", ssem, rsem, sent, lands, after)
            for nm, k, b, tr, w, m, v in plans[part]:
                big[nm] = adam_big(f"adam_{nm}{layer}", me, lands[k], sent[k], b, layer, tr, w, m, v, big.get(nm))
                after = big[nm][0]

    gsmall = []
    for layer in range(depth):
        ssem, rsem, lands = small_gathers[layer]
        gsmall += copies_wait(f"small_wait{layer}", ssem, rsem, (), lands, after)[1]

    maps2 = lambda a: a.reshape(depth * HEADS * HD, HD)
    res = adam_small(
        "adam_small", *gsmall,
        [(ffn1_norm, m_ffn1_norm, v_ffn1_norm), (mix_norm, m_mix_norm, v_mix_norm), (ffn2_norm, m_ffn2_norm, v_ffn2_norm)],
        (pool_scale, m_pool_scale, v_pool_scale), (maps2(pool_maps), maps2(m_pool_maps), maps2(v_pool_maps)),
        (meta, m_meta, v_meta), tuple(a.reshape(1, d) for a in (final_norm, m_final_norm, v_final_norm)), d)
    loss = res[0][0, 0]
    sm = {}
    for k, nm in enumerate(["ffn1_norm", "mix_norm", "ffn2_norm", "pool_scale", "pool_maps", "meta", "final_norm"]):
        sm[nm] = list(res[1 + 4 * k:5 + 4 * k])
    sm["pool_maps"] = [a.reshape(pool_maps.shape) for a in sm["pool_maps"]]
    sm["final_norm"] = [a.reshape(d) for a in sm["final_norm"]]

    names = ["meta", "ffn1_norm", "ffn1_gate", "ffn1_up", "ffn1_down", "mix_norm", "w_in", "pool_maps", "pool_scale",
             "w_ret_up", "w_pool_up", "w_out", "ffn2_norm", "ffn2_gate", "ffn2_up", "ffn2_down", "final_norm"]
    for nm in ("ffn1_gate", "ffn1_up", "ffn2_gate", "ffn2_up", "w_in"):
        big[nm] = tview(*big[nm])
    allw = {**{k: list(v) for k, v in big.items()}, **sm}
    outs = [loss, grad_x]
    for kind in range(4):
        outs += [allw[nm][kind] for nm in names]
    return tuple(outs)
```
